```python
import jax, jax.numpy as jnp
from jax import lax
import numpy as np

D_MODEL = 1024
BATCH = 8
SEQ = 2048
DEPTH = 2

SG_WIDTH = D_MODEL // 2
SG_CHUNK = 128
SG_GROUPS = 4
CV_WIDTH = D_MODEL // 2
CV_KERNEL = 31
HEAD_DIM = 64
N_Q_HEADS = D_MODEL // 2 // HEAD_DIM
N_KV_HEADS = N_Q_HEADS // 4
Q_WIDTH = N_Q_HEADS * HEAD_DIM
KV_WIDTH = N_KV_HEADS * HEAD_DIM
WINDOW = 128
ROPE_THETA = 10000.0
SC_WIDTH = D_MODEL // 2
SC_KERNEL = 3
N_BRANCH = 4
BRANCH_WIDTH = D_MODEL // 2
D_FF = -(-8 * D_MODEL // (3 * 256)) * 256
EPS = 1e-6

_PROJ_WIDTHS = (2 * SG_WIDTH, 2 * CV_WIDTH, Q_WIDTH, KV_WIDTH, KV_WIDTH, 3 * SC_WIDTH, N_BRANCH * D_MODEL)
PROJ_WIDTH = sum(_PROJ_WIDTHS)
PROJ_SPLITS = tuple(sum(_PROJ_WIDTHS[:i + 1]) for i in range(len(_PROJ_WIDTHS) - 1))

kernel_name = "hybrid_gated_four_mixer_block"


def rmsnorm(x, g):
    xf = x.astype(jnp.float32)
    y = xf * lax.rsqrt(jnp.mean(xf * xf, axis=-1, keepdims=True) + EPS)
    return (y * g.astype(jnp.float32)).astype(x.dtype)


def layernorm(x, g, b):
    xf = x.astype(jnp.float32)
    mu = jnp.mean(xf, axis=-1, keepdims=True)
    var = jnp.mean(jnp.square(xf - mu), axis=-1, keepdims=True)
    y = (xf - mu) * lax.rsqrt(var + EPS)
    return (y * g.astype(jnp.float32) + b.astype(jnp.float32)).astype(x.dtype)


def causal_depthwise_conv(x, w):
    k = w.shape[0]
    return lax.conv_general_dilated(
        x, w[:, None, :].astype(x.dtype), window_strides=(1,), padding=[(k - 1, 0)],
        dimension_numbers=('NWC', 'WIO', 'NWC'), feature_group_count=x.shape[-1])


def rotary(t, cos, sin):
    t1, t2 = jnp.split(t, 2, axis=-1)
    c = cos[None, :, None, :]
    s = sin[None, :, None, :]
    return jnp.concatenate([t1 * c - t2 * s, t2 * c + t1 * s], axis=-1)


def spatial_gating(z, ln_g, ln_b, w_s, b_s):
    u, v = jnp.split(z, 2, axis=-1)
    v = layernorm(v, ln_g, ln_b)
    b, s, _ = v.shape
    nc = s // SG_CHUNK
    vc = v.reshape(b, nc, SG_CHUNK, SG_GROUPS, SG_WIDTH // SG_GROUPS)
    tril = jnp.tril(jnp.ones((SG_CHUNK, SG_CHUNK), dtype=bool))
    w = jnp.where(tril[None], w_s, jnp.zeros((), w_s.dtype))
    mixed = jnp.einsum('gts,bnsgc->bntgc', w, vc) + b_s.T[None, None, :, :, None]
    return u * mixed.reshape(b, s, SG_WIDTH)


def conformer_conv(z, w_dw, b_dw, ln_g, ln_b):
    a, gate = jnp.split(z, 2, axis=-1)
    y = a * jax.nn.sigmoid(gate)
    y = causal_depthwise_conv(y, w_dw) + b_dw
    y = layernorm(y, ln_g, ln_b)
    return jax.nn.silu(y)


def short_gated_conv(z, w_sc):
    bg, cg, h = jnp.split(z, 3, axis=-1)
    return bg * causal_depthwise_conv(cg * h, w_sc)


def sliding_window_attention(q, k, v, sinks):
    b, s, _, _ = q.shape
    nb = s // WINDOW
    g = N_Q_HEADS // N_KV_HEADS
    qb = q.reshape(b, nb, WINDOW, N_KV_HEADS, g, HEAD_DIM)

    def band(t):
        tp = jnp.pad(t, ((0, 0), (WINDOW, 0), (0, 0), (0, 0)))
        prev = tp[:, :s].reshape(b, nb, WINDOW, N_KV_HEADS, HEAD_DIM)
        cur = t.reshape(b, nb, WINDOW, N_KV_HEADS, HEAD_DIM)
        return jnp.concatenate([prev, cur], axis=2)

    kb, vb = band(k), band(v)
    scores = jnp.einsum('bnqhgd,bnkhd->bnhgqk', qb, kb).astype(jnp.float32) * (HEAD_DIM ** -0.5)
    qi = jnp.arange(WINDOW)[None, :, None]
    kj = jnp.arange(2 * WINDOW)[None, None, :]
    blk = jnp.arange(nb)[:, None, None]
    delta = qi + WINDOW - kj
    valid = (delta >= 0) & (delta < WINDOW) & (blk * WINDOW + kj - WINDOW >= 0)
    scores = jnp.where(valid[None, :, None, None], scores, -jnp.inf)
    sink = sinks.astype(jnp.float32).reshape(N_KV_HEADS, g)[None, None, :, :, None, None]
    m = jnp.maximum(jnp.max(scores, axis=-1, keepdims=True), sink)
    p = jnp.exp(scores - m)
    probs = (p / (jnp.sum(p, axis=-1, keepdims=True) + jnp.exp(sink - m))).astype(v.dtype)
    out = jnp.einsum('bnhgqk,bnkhd->bnqhgd', probs, vb)
    return out.reshape(b, s, Q_WIDTH)


def _fwd_setup_inputs(seed: int = 0) -> dict:
    key = jax.random.key(seed)
    ks = jax.random.split(key, 24)
    f32 = jnp.float32
    nrm = lambda k, shape, scale: jax.random.normal(k, shape, f32) * scale
    gain = lambda k, shape: 1.0 + 0.02 * jax.random.normal(k, shape, f32)
    return {
        "x": nrm(ks[0], (BATCH, SEQ, D_MODEL), 1.0),
        "norm_mix": gain(ks[1], (DEPTH, D_MODEL)),
        "w_in": nrm(ks[2], (DEPTH, D_MODEL, PROJ_WIDTH), D_MODEL ** -0.5),
        "sg_ln_g": gain(ks[3], (DEPTH, SG_WIDTH)),
        "sg_ln_b": nrm(ks[4], (DEPTH, SG_WIDTH), 0.02),
        "sg_w": nrm(ks[5], (DEPTH, SG_GROUPS, SG_CHUNK, SG_CHUNK), SG_CHUNK ** -0.5),
        "sg_b": 1.0 + nrm(ks[6], (DEPTH, SG_GROUPS, SG_CHUNK), 0.1),
        "cv_w": nrm(ks[7], (DEPTH, CV_KERNEL, CV_WIDTH), CV_KERNEL ** -0.5),
        "cv_b": nrm(ks[8], (DEPTH, CV_WIDTH), 0.02),
        "cv_ln_g": gain(ks[9], (DEPTH, CV_WIDTH)),
        "cv_ln_b": nrm(ks[10], (DEPTH, CV_WIDTH), 0.02),
        "attn_sinks": nrm(ks[11], (DEPTH, N_Q_HEADS), 1.0),
        "sc_w": nrm(ks[12], (DEPTH, SC_KERNEL, SC_WIDTH), SC_KERNEL ** -0.5),
        "w_branch": nrm(ks[13], (DEPTH, N_BRANCH, BRANCH_WIDTH, D_MODEL), BRANCH_WIDTH ** -0.5),
        "w_out": nrm(ks[14], (DEPTH, D_MODEL, D_MODEL), 0.5 * D_MODEL ** -0.5),
        "norm_ffn": gain(ks[15], (DEPTH, D_MODEL)),
        "w_gate_up": nrm(ks[16], (DEPTH, D_MODEL, 2 * D_FF), D_MODEL ** -0.5),
        "w_down": nrm(ks[17], (DEPTH, D_FF, D_MODEL), D_FF ** -0.5),
        "norm_final": gain(ks[18], (D_MODEL,)),
    }


def _fwd_reference(x, norm_mix, w_in, sg_ln_g, sg_ln_b, sg_w, sg_b, cv_w, cv_b, cv_ln_g, cv_ln_b,
              attn_sinks, sc_w, w_branch, w_out, norm_ffn, w_gate_up, w_down, norm_final):
    b, s, _ = x.shape
    pos = jnp.arange(s, dtype=jnp.float32)
    inv_freq = 1.0 / (ROPE_THETA ** (jnp.arange(0, HEAD_DIM, 2, dtype=jnp.float32) / HEAD_DIM))
    ang = pos[:, None] * inv_freq[None, :]
    cos = jnp.cos(ang).astype(x.dtype)
    sin = jnp.sin(ang).astype(x.dtype)

    for l in range(DEPTH):
        xn = rmsnorm(x, norm_mix[l])
        proj = xn @ w_in[l]
        z_a, z_b, q, k, v, z_d, z_g = jnp.split(proj, PROJ_SPLITS, axis=-1)

        y_a = spatial_gating(jax.nn.gelu(z_a, approximate=False), sg_ln_g[l], sg_ln_b[l], sg_w[l], sg_b[l])
        y_b = conformer_conv(z_b, cv_w[l], cv_b[l], cv_ln_g[l], cv_ln_b[l])
        q = rotary(q.reshape(b, s, N_Q_HEADS, HEAD_DIM), cos, sin)
        k = rotary(k.reshape(b, s, N_KV_HEADS, HEAD_DIM), cos, sin)
        v = v.reshape(b, s, N_KV_HEADS, HEAD_DIM)
        y_c = sliding_window_attention(q, k, v, attn_sinks[l])
        y_d = short_gated_conv(z_d, sc_w[l])

        ys = jnp.stack([y_a, y_b, y_c, y_d], axis=0)
        branch = jnp.einsum('nbsc,ncd->nbsd', ys, w_branch[l])
        gates = jax.nn.sigmoid(z_g.reshape(b, s, N_BRANCH, D_MODEL))
        merged = jnp.einsum('bsnd,nbsd->bsd', gates, branch)
        x = x + merged @ w_out[l]

        hn = rmsnorm(x, norm_ffn[l])
        gate, up = jnp.split(hn @ w_gate_up[l], 2, axis=-1)
        x = x + (jax.nn.silu(gate) * up) @ w_down[l]

    return rmsnorm(x, norm_final)


import jax as _jax
import jax.numpy as _jnp

TWIN_FORMAT = 'train_step'
FWD_PARAMS = ['x', 'norm_mix', 'w_in', 'sg_ln_g', 'sg_ln_b', 'sg_w', 'sg_b', 'cv_w', 'cv_b', 'cv_ln_g', 'cv_ln_b', 'attn_sinks', 'sc_w', 'w_branch', 'w_out', 'norm_ffn', 'w_gate_up', 'w_down', 'norm_final']
TWIN_WEIGHTS = ['norm_mix', 'w_in', 'sg_ln_g', 'sg_ln_b', 'sg_w', 'sg_b', 'cv_w', 'cv_b', 'cv_ln_g', 'cv_ln_b', 'attn_sinks', 'sc_w', 'w_branch', 'w_out', 'norm_ffn', 'w_gate_up', 'w_down', 'norm_final']
TWIN_DIFF_INPUT = 'x'
TWIN_INPUTS = ['x', 'norm_mix', 'w_in', 'sg_ln_g', 'sg_ln_b', 'sg_w', 'sg_b', 'cv_w', 'cv_b', 'cv_ln_g', 'cv_ln_b', 'attn_sinks', 'sc_w', 'w_branch', 'w_out', 'norm_ffn', 'w_gate_up', 'w_down', 'norm_final', 'loss_target', 'm_norm_mix', 'm_w_in', 'm_sg_ln_g', 'm_sg_ln_b', 'm_sg_w', 'm_sg_b', 'm_cv_w', 'm_cv_b', 'm_cv_ln_g', 'm_cv_ln_b', 'm_attn_sinks', 'm_sc_w', 'm_w_branch', 'm_w_out', 'm_norm_ffn', 'm_w_gate_up', 'm_w_down', 'm_norm_final', 'v_norm_mix', 'v_w_in', 'v_sg_ln_g', 'v_sg_ln_b', 'v_sg_w', 'v_sg_b', 'v_cv_w', 'v_cv_b', 'v_cv_ln_g', 'v_cv_ln_b', 'v_attn_sinks', 'v_sc_w', 'v_w_branch', 'v_w_out', 'v_norm_ffn', 'v_w_gate_up', 'v_w_down', 'v_norm_final']
TWIN_OUTPUTS = ['loss', 'grad_x', 'grad_norm_mix', 'grad_w_in', 'grad_sg_ln_g', 'grad_sg_ln_b', 'grad_sg_w', 'grad_sg_b', 'grad_cv_w', 'grad_cv_b', 'grad_cv_ln_g', 'grad_cv_ln_b', 'grad_attn_sinks', 'grad_sc_w', 'grad_w_branch', 'grad_w_out', 'grad_norm_ffn', 'grad_w_gate_up', 'grad_w_down', 'grad_norm_final', 'delta_norm_mix', 'delta_w_in', 'delta_sg_ln_g', 'delta_sg_ln_b', 'delta_sg_w', 'delta_sg_b', 'delta_cv_w', 'delta_cv_b', 'delta_cv_ln_g', 'delta_cv_ln_b', 'delta_attn_sinks', 'delta_sc_w', 'delta_w_branch', 'delta_w_out', 'delta_norm_ffn', 'delta_w_gate_up', 'delta_w_down', 'delta_norm_final', 'new_m_norm_mix', 'new_m_w_in', 'new_m_sg_ln_g', 'new_m_sg_ln_b', 'new_m_sg_w', 'new_m_sg_b', 'new_m_cv_w', 'new_m_cv_b', 'new_m_cv_ln_g', 'new_m_cv_ln_b', 'new_m_attn_sinks', 'new_m_sc_w', 'new_m_w_branch', 'new_m_w_out', 'new_m_norm_ffn', 'new_m_w_gate_up', 'new_m_w_down', 'new_m_norm_final', 'new_v_norm_mix', 'new_v_w_in', 'new_v_sg_ln_g', 'new_v_sg_ln_b', 'new_v_sg_w', 'new_v_sg_b', 'new_v_cv_w', 'new_v_cv_b', 'new_v_cv_ln_g', 'new_v_cv_ln_b', 'new_v_attn_sinks', 'new_v_sc_w', 'new_v_w_branch', 'new_v_w_out', 'new_v_norm_ffn', 'new_v_w_gate_up', 'new_v_w_down', 'new_v_norm_final']
TWIN_LEAF_KINDS = {'loss': 'loss', 'grad_x': 'grad_x', 'grad_norm_mix': 'grad_w', 'grad_w_in': 'grad_w', 'grad_sg_ln_g': 'grad_w', 'grad_sg_ln_b': 'grad_w', 'grad_sg_w': 'grad_w', 'grad_sg_b': 'grad_w', 'grad_cv_w': 'grad_w', 'grad_cv_b': 'grad_w', 'grad_cv_ln_g': 'grad_w', 'grad_cv_ln_b': 'grad_w', 'grad_attn_sinks': 'grad_w', 'grad_sc_w': 'grad_w', 'grad_w_branch': 'grad_w', 'grad_w_out': 'grad_w', 'grad_norm_ffn': 'grad_w', 'grad_w_gate_up': 'grad_w', 'grad_w_down': 'grad_w', 'grad_norm_final': 'grad_w', 'delta_norm_mix': 'delta_w', 'delta_w_in': 'delta_w', 'delta_sg_ln_g': 'delta_w', 'delta_sg_ln_b': 'delta_w', 'delta_sg_w': 'delta_w', 'delta_sg_b': 'delta_w', 'delta_cv_w': 'delta_w', 'delta_cv_b': 'delta_w', 'delta_cv_ln_g': 'delta_w', 'delta_cv_ln_b': 'delta_w', 'delta_attn_sinks': 'delta_w', 'delta_sc_w': 'delta_w', 'delta_w_branch': 'delta_w', 'delta_w_out': 'delta_w', 'delta_norm_ffn': 'delta_w', 'delta_w_gate_up': 'delta_w', 'delta_w_down': 'delta_w', 'delta_norm_final': 'delta_w', 'new_m_norm_mix': 'new_m', 'new_m_w_in': 'new_m', 'new_m_sg_ln_g': 'new_m', 'new_m_sg_ln_b': 'new_m', 'new_m_sg_w': 'new_m', 'new_m_sg_b': 'new_m', 'new_m_cv_w': 'new_m', 'new_m_cv_b': 'new_m', 'new_m_cv_ln_g': 'new_m', 'new_m_cv_ln_b': 'new_m', 'new_m_attn_sinks': 'new_m', 'new_m_sc_w': 'new_m', 'new_m_w_branch': 'new_m', 'new_m_w_out': 'new_m', 'new_m_norm_ffn': 'new_m', 'new_m_w_gate_up': 'new_m', 'new_m_w_down': 'new_m', 'new_m_norm_final': 'new_m', 'new_v_norm_mix': 'new_v', 'new_v_w_in': 'new_v', 'new_v_sg_ln_g': 'new_v', 'new_v_sg_ln_b': 'new_v', 'new_v_sg_w': 'new_v', 'new_v_sg_b': 'new_v', 'new_v_cv_w': 'new_v', 'new_v_cv_b': 'new_v', 'new_v_cv_ln_g': 'new_v', 'new_v_cv_ln_b': 'new_v', 'new_v_attn_sinks': 'new_v', 'new_v_sc_w': 'new_v', 'new_v_w_branch': 'new_v', 'new_v_w_out': 'new_v', 'new_v_norm_ffn': 'new_v', 'new_v_w_gate_up': 'new_v', 'new_v_w_down': 'new_v', 'new_v_norm_final': 'new_v'}


def _forward(args):
    return _fwd_reference(*[args[k] for k in FWD_PARAMS])


def _output_shape():
    out = _jax.eval_shape(lambda: _forward(_fwd_setup_inputs(0)))
    return out.shape, out.dtype

N_MICROBATCH = 1
ADAM_LR = 0.001
ADAM_B1 = 0.9
ADAM_B2 = 0.999
ADAM_EPS = 1e-08
ADAM_WD = 0.01
ADAM_STEP = 10
PER_EXAMPLE_BATCH_AXIS = {'x': 0, 'loss_target': 0}
SHARED_INPUTS = []
_WEIGHT_DTYPES = {'norm_mix': _jnp.float32, 'w_in': _jnp.float32, 'sg_ln_g': _jnp.float32, 'sg_ln_b': _jnp.float32, 'sg_w': _jnp.float32, 'sg_b': _jnp.float32, 'cv_w': _jnp.float32, 'cv_b': _jnp.float32, 'cv_ln_g': _jnp.float32, 'cv_ln_b': _jnp.float32, 'attn_sinks': _jnp.float32, 'sc_w': _jnp.float32, 'w_branch': _jnp.float32, 'w_out': _jnp.float32, 'norm_ffn': _jnp.float32, 'w_gate_up': _jnp.float32, 'w_down': _jnp.float32, 'norm_final': _jnp.float32}
MOMENT_SCALE = {'norm_mix': 8.087254e-02, 'w_in': 2.675932e-02, 'sg_ln_g': 2.394480e-02, 'sg_ln_b': 2.073724e-02, 'sg_w': 2.344911e-02, 'sg_b': 3.246554e-02, 'cv_w': 3.118806e-02, 'cv_b': 6.939503e-02, 'cv_ln_g': 3.800905e-02, 'cv_ln_b': 3.133546e-02, 'attn_sinks': 9.672546e-03, 'sc_w': 4.998494e-02, 'w_branch': 2.508293e-02, 'w_out': 1.004216e-01, 'norm_ffn': 9.282265e-02, 'w_gate_up': 3.808460e-02, 'w_down': 6.205777e-02, 'norm_final': 1.602443e+01}


def _to_microbatches(a, axis):
    t = _jnp.moveaxis(a, axis, 0)
    t = t.reshape((N_MICROBATCH, t.shape[0] // N_MICROBATCH) + t.shape[1:])
    return _jnp.moveaxis(t, 1, axis + 1)


def setup_inputs(seed: int = 0) -> dict:
    inp = _fwd_setup_inputs(seed)
    key = _jax.random.fold_in(_jax.random.key(seed), 7919)
    shape, _ = _output_shape()
    out = dict(inp)
    out["loss_target"] = _jax.random.normal(_jax.random.fold_in(key, 0), shape, _jnp.float32)
    for i, name in enumerate(TWIN_WEIGHTS):
        w = inp[name].astype(_jnp.float32)
        if MOMENT_SCALE is None:
            s = _jnp.sqrt(_jnp.mean(_jnp.square(w)) + 1e-30)
        else:
            s = MOMENT_SCALE[name]
        km, kv = _jax.random.split(_jax.random.fold_in(key, i + 1))
        out[name] = w
        out["m_" + name] = s * _jax.random.normal(km, w.shape, _jnp.float32)
        out["v_" + name] = (s * s) * _jax.random.uniform(kv, w.shape, _jnp.float32, 0.5, 1.5)
    if N_MICROBATCH > 1:
        for name, axis in PER_EXAMPLE_BATCH_AXIS.items():
            out[name] = _to_microbatches(out[name], axis)
    return {'x': out['x'], 'norm_mix': out['norm_mix'], 'w_in': out['w_in'], 'sg_ln_g': out['sg_ln_g'], 'sg_ln_b': out['sg_ln_b'], 'sg_w': out['sg_w'], 'sg_b': out['sg_b'], 'cv_w': out['cv_w'], 'cv_b': out['cv_b'], 'cv_ln_g': out['cv_ln_g'], 'cv_ln_b': out['cv_ln_b'], 'attn_sinks': out['attn_sinks'], 'sc_w': out['sc_w'], 'w_branch': out['w_branch'], 'w_out': out['w_out'], 'norm_ffn': out['norm_ffn'], 'w_gate_up': out['w_gate_up'], 'w_down': out['w_down'], 'norm_final': out['norm_final'], 'loss_target': out['loss_target'], 'm_norm_mix': out['m_norm_mix'], 'm_w_in': out['m_w_in'], 'm_sg_ln_g': out['m_sg_ln_g'], 'm_sg_ln_b': out['m_sg_ln_b'], 'm_sg_w': out['m_sg_w'], 'm_sg_b': out['m_sg_b'], 'm_cv_w': out['m_cv_w'], 'm_cv_b': out['m_cv_b'], 'm_cv_ln_g': out['m_cv_ln_g'], 'm_cv_ln_b': out['m_cv_ln_b'], 'm_attn_sinks': out['m_attn_sinks'], 'm_sc_w': out['m_sc_w'], 'm_w_branch': out['m_w_branch'], 'm_w_out': out['m_w_out'], 'm_norm_ffn': out['m_norm_ffn'], 'm_w_gate_up': out['m_w_gate_up'], 'm_w_down': out['m_w_down'], 'm_norm_final': out['m_norm_final'], 'v_norm_mix': out['v_norm_mix'], 'v_w_in': out['v_w_in'], 'v_sg_ln_g': out['v_sg_ln_g'], 'v_sg_ln_b': out['v_sg_ln_b'], 'v_sg_w': out['v_sg_w'], 'v_sg_b': out['v_sg_b'], 'v_cv_w': out['v_cv_w'], 'v_cv_b': out['v_cv_b'], 'v_cv_ln_g': out['v_cv_ln_g'], 'v_cv_ln_b': out['v_cv_ln_b'], 'v_attn_sinks': out['v_attn_sinks'], 'v_sc_w': out['v_sc_w'], 'v_w_branch': out['v_w_branch'], 'v_w_out': out['v_w_out'], 'v_norm_ffn': out['v_norm_ffn'], 'v_w_gate_up': out['v_w_gate_up'], 'v_w_down': out['v_w_down'], 'v_norm_final': out['v_norm_final']}


def _loss(weights, diff, rest, loss_target):
    with _jax.named_scope("forward"):
        args = {**rest, TWIN_DIFF_INPUT: diff, **{k: w.astype(_WEIGHT_DTYPES[k]) for k, w in weights.items()}}
        y = _forward(args)
    with _jax.named_scope("loss_head"):
        err = _jnp.square(y.astype(_jnp.float32) - loss_target)
        return 0.5 * _jnp.sum(_jnp.mean(err, axis=-1)) if err.ndim else 0.5 * err


def _adamw(w, g, m, v):
    m = ADAM_B1 * m + (1.0 - ADAM_B1) * g
    v = ADAM_B2 * v + (1.0 - ADAM_B2) * _jnp.square(g)
    m_hat = m / (1.0 - ADAM_B1 ** ADAM_STEP)
    v_hat = v / (1.0 - ADAM_B2 ** ADAM_STEP)
    delta = -ADAM_LR * (m_hat / (_jnp.sqrt(v_hat) + ADAM_EPS) + ADAM_WD * w)
    return delta, m, v


def reference(x, norm_mix, w_in, sg_ln_g, sg_ln_b, sg_w, sg_b, cv_w, cv_b, cv_ln_g, cv_ln_b, attn_sinks, sc_w, w_branch, w_out, norm_ffn, w_gate_up, w_down, norm_final, loss_target, m_norm_mix, m_w_in, m_sg_ln_g, m_sg_ln_b, m_sg_w, m_sg_b, m_cv_w, m_cv_b, m_cv_ln_g, m_cv_ln_b, m_attn_sinks, m_sc_w, m_w_branch, m_w_out, m_norm_ffn, m_w_gate_up, m_w_down, m_norm_final, v_norm_mix, v_w_in, v_sg_ln_g, v_sg_ln_b, v_sg_w, v_sg_b, v_cv_w, v_cv_b, v_cv_ln_g, v_cv_ln_b, v_attn_sinks, v_sc_w, v_w_branch, v_w_out, v_norm_ffn, v_w_gate_up, v_w_down, v_norm_final):
    given = dict(x=x, norm_mix=norm_mix, w_in=w_in, sg_ln_g=sg_ln_g, sg_ln_b=sg_ln_b, sg_w=sg_w, sg_b=sg_b, cv_w=cv_w, cv_b=cv_b, cv_ln_g=cv_ln_g, cv_ln_b=cv_ln_b, attn_sinks=attn_sinks, sc_w=sc_w, w_branch=w_branch, w_out=w_out, norm_ffn=norm_ffn, w_gate_up=w_gate_up, w_down=w_down, norm_final=norm_final, loss_target=loss_target, m_norm_mix=m_norm_mix, m_w_in=m_w_in, m_sg_ln_g=m_sg_ln_g, m_sg_ln_b=m_sg_ln_b, m_sg_w=m_sg_w, m_sg_b=m_sg_b, m_cv_w=m_cv_w, m_cv_b=m_cv_b, m_cv_ln_g=m_cv_ln_g, m_cv_ln_b=m_cv_ln_b, m_attn_sinks=m_attn_sinks, m_sc_w=m_sc_w, m_w_branch=m_w_branch, m_w_out=m_w_out, m_norm_ffn=m_norm_ffn, m_w_gate_up=m_w_gate_up, m_w_down=m_w_down, m_norm_final=m_norm_final, v_norm_mix=v_norm_mix, v_w_in=v_w_in, v_sg_ln_g=v_sg_ln_g, v_sg_ln_b=v_sg_ln_b, v_sg_w=v_sg_w, v_sg_b=v_sg_b, v_cv_w=v_cv_w, v_cv_b=v_cv_b, v_cv_ln_g=v_cv_ln_g, v_cv_ln_b=v_cv_ln_b, v_attn_sinks=v_attn_sinks, v_sc_w=v_sc_w, v_w_branch=v_w_branch, v_w_out=v_w_out, v_norm_ffn=v_norm_ffn, v_w_gate_up=v_w_gate_up, v_w_down=v_w_down, v_norm_final=v_norm_final)
    weights = {n: given[n] for n in TWIN_WEIGHTS}
    shared = {n: given[n] for n in SHARED_INPUTS}
    per_example = {n: given[n] for n in ['x']}
    grad_fn = _jax.value_and_grad(_loss, argnums=(0, 1))

    def one_microbatch(ex, loss_target):
        ex = dict(ex)
        diff = ex.pop(TWIN_DIFF_INPUT)
        return grad_fn(weights, diff, {**shared, **ex}, loss_target)

    if N_MICROBATCH == 1:
        loss, (grad_w, grad_x) = one_microbatch(per_example, given["loss_target"])
    else:
        def body(carry, xs):
            loss_sum, grad_sum = carry
            l_k, (gw_k, gx_k) = one_microbatch(xs[0], xs[1])
            with _jax.named_scope("update"):
                return (loss_sum + l_k, _jax.tree.map(_jnp.add, grad_sum, gw_k)), gx_k

        init = (_jnp.zeros((), _jnp.float32), _jax.tree.map(_jnp.zeros_like, weights))
        (loss, grad_w), grad_x = _jax.lax.scan(body, init, (per_example, given["loss_target"]))
    with _jax.named_scope("update"):
        delta_w, new_m, new_v = {}, {}, {}
        for n in TWIN_WEIGHTS:
            delta_w[n], new_m[n], new_v[n] = _adamw(weights[n], grad_w[n], given["m_" + n], given["v_" + n])
    return (loss, grad_x, *[grad_w[n] for n in TWIN_WEIGHTS], *[delta_w[n] for n in TWIN_WEIGHTS],
            *[new_m[n] for n in TWIN_WEIGHTS], *[new_v[n] for n in TWIN_WEIGHTS])
```

```python
import jax
import jax.numpy as jnp
from jax import lax
from jax.experimental import pallas as pl
from jax.experimental.pallas import tpu as pltpu

F32 = jnp.float32
BF16 = jnp.bfloat16

SEQ = 2048
D_MODEL = 1024
DEPTH = 2
SG_WIDTH = 512
SG_CHUNK = 128
SG_GROUPS = 4
CV_WIDTH = 512
CV_KERNEL = 31
HEAD_DIM = 64
N_Q_HEADS = 8
N_KV_HEADS = 2
Q_WIDTH = 512
KV_WIDTH = 128
WINDOW = 128
SC_WIDTH = 512
SC_KERNEL = 3
N_BRANCH = 4
D_FF = 2816
EPS = 1e-6
ROPE_THETA = 10000.0
PROJ_A = 4352
PROJ_WIDTH = 8448
N_DEV = 8

ADAM_LR = 0.001
ADAM_B1 = 0.9
ADAM_B2 = 0.999
ADAM_EPS = 1e-08
ADAM_WD = 0.01
ADAM_STEP = 10

LANES = 128
SUBLANES = 8
VMEM_LIMIT_BYTES = 48 * 1024 * 1024
HALO = 32
CONV_ROWS = 256
TOKEN_TILE = 256

_SQRT_HALF = 0.7071067811865476
_INV_SQRT_2PI = 0.3989422804014327


def _params(semantics=None):
    return pltpu.CompilerParams(dimension_semantics=semantics, vmem_limit_bytes=VMEM_LIMIT_BYTES)


def _divisor_tile(n, cap, unit):
    best = None
    for t in range(unit, min(n, cap) + 1, unit):
        if n % t == 0:
            best = t
    return best if best is not None else n


_DIMS = {"nn": (((1,), (0,)), ((), ())), "nt": (((1,), (1,)), ((), ())), "tn": (((0,), (0,)), ((), ()))}


def matmul(a, b, mode, out_dtype, name, add=None, tm_cap=512, tn_cap=512):
    if mode == "nn":
        (m, k), (_, n) = a.shape, b.shape
    elif mode == "nt":
        (m, k), (n, _) = a.shape, b.shape
    else:
        (k, m), (_, n) = a.shape, b.shape
    tm = _divisor_tile(m, tm_cap, LANES)
    tn = _divisor_tile(n, tn_cap, LANES)
    a_spec = pl.BlockSpec((k, tm), lambda i, j: (0, i)) if mode == "tn" else pl.BlockSpec((tm, k), lambda i, j: (i, 0))
    b_spec = pl.BlockSpec((tn, k), lambda i, j: (j, 0)) if mode == "nt" else pl.BlockSpec((k, tn), lambda i, j: (0, j))
    o_spec = pl.BlockSpec((tm, tn), lambda i, j: (i, j))
    dims = _DIMS[mode]

    def body(*refs):
        a_ref, b_ref = refs[0], refs[1]
        o_ref = refs[-1]
        acc = lax.dot_general(a_ref[...].astype(BF16), b_ref[...].astype(BF16), dims, preferred_element_type=F32)
        if add is not None:
            acc = acc + refs[2][...].astype(F32)
        o_ref[...] = acc.astype(out_dtype)

    operands = (a, b) if add is None else (a, b, add)
    in_specs = [a_spec, b_spec] + ([o_spec] if add is not None else [])
    return pl.pallas_call(
        body, name=name, out_shape=jax.ShapeDtypeStruct((m, n), out_dtype), grid=(m // tm, n // tn),
        in_specs=in_specs, out_specs=o_spec, compiler_params=_params(("parallel", "parallel")))(*operands)


def _sigmoid(x):
    return 1.0 / (1.0 + jnp.exp(-x))


def _gelu(x):
    return 0.5 * x * (1.0 + lax.erf(x * _SQRT_HALF))


def _gelu_grad(x):
    return 0.5 * (1.0 + lax.erf(x * _SQRT_HALF)) + x * _INV_SQRT_2PI * jnp.exp(-0.5 * x * x)


def _rms_stats(x):
    r = lax.rsqrt(jnp.mean(x * x, axis=-1, keepdims=True) + EPS)
    return x * r, r


def _rms_bwd(dxn, xhat, r, g):
    h = dxn * g
    return r * (h - xhat * jnp.mean(h * xhat, axis=-1, keepdims=True))


def _ln_stats(x):
    mu = jnp.mean(x, axis=-1, keepdims=True)
    xc = x - mu
    rstd = lax.rsqrt(jnp.mean(xc * xc, axis=-1, keepdims=True) + EPS)
    return xc * rstd, rstd


def _ln_bwd(dy, xhat, rstd, g):
    dxhat = dy * g
    return rstd * (dxhat - jnp.mean(dxhat, axis=-1, keepdims=True)
                   - xhat * jnp.mean(dxhat * xhat, axis=-1, keepdims=True))


def _accumulate(ref, value, first):
    @pl.when(first)
    def _():
        ref[...] = value

    @pl.when(jnp.logical_not(first))
    def _():
        ref[...] += value


def _shift_rows(win, shift, n_out):
    n = win.shape[0]
    if shift % n == 0:
        return win[:n_out]
    return pltpu.roll(win, n - shift, axis=0)[:n_out]


def _row_spec(width):
    return pl.BlockSpec((1, width), lambda i: (0, 0))


def rmsnorm_fwd(x, g, name):
    s, d = x.shape

    def body(x_ref, g_ref, o_ref):
        xhat, _ = _rms_stats(x_ref[...])
        o_ref[...] = (xhat * g_ref[...]).astype(BF16)

    tile = pl.BlockSpec((TOKEN_TILE, d), lambda i: (i, 0))
    return pl.pallas_call(
        body, name=name, out_shape=jax.ShapeDtypeStruct((s, d), BF16), grid=(s // TOKEN_TILE,),
        in_specs=[tile, _row_spec(d)], out_specs=tile, compiler_params=_params(("parallel",)))(x, g)


def rmsnorm_bwd(x, g, dxn, dres, name):
    s, d = x.shape

    def body(x_ref, g_ref, dxn_ref, dres_ref, dx_ref, dg_ref):
        xhat, r = _rms_stats(x_ref[...])
        dxn_v = dxn_ref[...].astype(F32)
        dx_ref[...] = dres_ref[...] + _rms_bwd(dxn_v, xhat, r, g_ref[...])
        _accumulate(dg_ref, jnp.sum(dxn_v * xhat, axis=0, keepdims=True), pl.program_id(0) == 0)

    tile = pl.BlockSpec((TOKEN_TILE, d), lambda i: (i, 0))
    return pl.pallas_call(
        body, name=name, out_shape=(jax.ShapeDtypeStruct((s, d), F32), jax.ShapeDtypeStruct((1, d), F32)),
        grid=(s // TOKEN_TILE,), in_specs=[tile, _row_spec(d), tile, tile], out_specs=(tile, _row_spec(d)),
        compiler_params=_params(("arbitrary",)))(x, g, dxn, dres)


def loss_head(x, g, target, name):
    s, d = x.shape

    def body(x_ref, g_ref, t_ref, loss_ref, dx_ref, dg_ref):
        first = pl.program_id(0) == 0
        xhat, r = _rms_stats(x_ref[...])
        gv = g_ref[...]
        err = xhat * gv - t_ref[...]
        part = 0.5 * jnp.sum(jnp.sum(err * err, axis=-1, keepdims=True), axis=0, keepdims=True) / d
        _accumulate(loss_ref, jnp.broadcast_to(part, (1, LANES)), first)
        dy = err / d
        dx_ref[...] = _rms_bwd(dy, xhat, r, gv)
        _accumulate(dg_ref, jnp.sum(dy * xhat, axis=0, keepdims=True), first)

    tile = pl.BlockSpec((TOKEN_TILE, d), lambda i: (i, 0))
    return pl.pallas_call(
        body, name=name,
        out_shape=(jax.ShapeDtypeStruct((1, LANES), F32), jax.ShapeDtypeStruct((s, d), F32),
                   jax.ShapeDtypeStruct((1, d), F32)),
        grid=(s // TOKEN_TILE,), in_specs=[tile, _row_spec(d), tile],
        out_specs=(_row_spec(LANES), tile, _row_spec(d)), compiler_params=_params(("arbitrary",)))(x, g, target)


def _tril_mask():
    row = lax.broadcasted_iota(jnp.int32, (SG_CHUNK, SG_CHUNK), 0)
    col = lax.broadcasted_iota(jnp.int32, (SG_CHUNK, SG_CHUNK), 1)
    return row >= col


def _sg_specs():
    vec = _row_spec(SG_WIDTH)
    mat = pl.BlockSpec((SG_GROUPS, SG_CHUNK, SG_CHUNK), lambda i: (0, 0, 0))
    return vec, mat


def mixer_a_fwd(proj, ln_g, ln_b, w_s, b_s, name):
    s = proj.shape[0]
    chunks = TOKEN_TILE // SG_CHUNK

    def body(z_ref, lg_ref, lb_ref, w_ref, b_ref, o_ref):
        ge = _gelu(z_ref[...].astype(F32))
        u = ge[:, :SG_WIDTH]
        xhat, _ = _ln_stats(ge[:, SG_WIDTH:])
        vn = xhat * lg_ref[...] + lb_ref[...]
        tril = _tril_mask()
        for ci in range(chunks):
            rows = slice(ci * SG_CHUNK, (ci + 1) * SG_CHUNK)
            for g in range(SG_GROUPS):
                cols = slice(g * LANES, (g + 1) * LANES)
                wm = jnp.where(tril, w_ref[g], 0.0).astype(BF16)
                mixed = jnp.dot(wm, vn[rows, cols].astype(BF16), preferred_element_type=F32) + b_ref[g]
                o_ref[rows, cols] = (u[rows, cols] * mixed).astype(BF16)

    vec, mat = _sg_specs()
    return pl.pallas_call(
        body, name=name, out_shape=jax.ShapeDtypeStruct((s, SG_WIDTH), BF16), grid=(s // TOKEN_TILE,),
        in_specs=[pl.BlockSpec((TOKEN_TILE, 2 * SG_WIDTH), lambda i: (i, 0)), vec, vec, mat, mat],
        out_specs=pl.BlockSpec((TOKEN_TILE, SG_WIDTH), lambda i: (i, 0)),
        compiler_params=_params(("parallel",)))(proj, ln_g, ln_b, w_s, b_s)


def mixer_a_bwd(proj, dy, ln_g, ln_b, w_s, b_s, name):
    s = proj.shape[0]
    chunks = TOKEN_TILE // SG_CHUNK

    def body(z_ref, dy_ref, lg_ref, lb_ref, w_ref, b_ref, dz_ref, dlg_ref, dlb_ref, dw_ref, db_ref, du_scr, dvn_scr):
        first = pl.program_id(0) == 0

        @pl.when(first)
        def _():
            dw_ref[...] = jnp.zeros_like(dw_ref)
            db_ref[...] = jnp.zeros_like(db_ref)

        z = z_ref[...].astype(F32)
        ge = _gelu(z)
        u = ge[:, :SG_WIDTH]
        xhat, rstd = _ln_stats(ge[:, SG_WIDTH:])
        lg = lg_ref[...]
        vn = xhat * lg + lb_ref[...]
        dyv = dy_ref[...].astype(F32)
        tril = _tril_mask()
        for ci in range(chunks):
            rows = slice(ci * SG_CHUNK, (ci + 1) * SG_CHUNK)
            for g in range(SG_GROUPS):
                cols = slice(g * LANES, (g + 1) * LANES)
                wm = jnp.where(tril, w_ref[g], 0.0).astype(BF16)
                vg = vn[rows, cols].astype(BF16)
                mixed = jnp.dot(wm, vg, preferred_element_type=F32) + b_ref[g]
                dyb = dyv[rows, cols]
                du_scr[rows, cols] = dyb * mixed
                dmix = dyb * u[rows, cols]
                db_ref[g] += jnp.broadcast_to(jnp.sum(dmix, axis=1, keepdims=True), (SG_CHUNK, LANES))
                dmb = dmix.astype(BF16)
                dwg = lax.dot_general(dmb, vg, _DIMS["nt"], preferred_element_type=F32)
                dw_ref[g] += jnp.where(tril, dwg, 0.0)
                dvn_scr[rows, cols] = lax.dot_general(wm, dmb, _DIMS["tn"], preferred_element_type=F32)
        dvn = dvn_scr[...]
        _accumulate(dlg_ref, jnp.sum(dvn * xhat, axis=0, keepdims=True), first)
        _accumulate(dlb_ref, jnp.sum(dvn, axis=0, keepdims=True), first)
        dvv = _ln_bwd(dvn, xhat, rstd, lg)
        gg = _gelu_grad(z)
        dz_ref[:, :SG_WIDTH] = (du_scr[...] * gg[:, :SG_WIDTH]).astype(BF16)
        dz_ref[:, SG_WIDTH:] = (dvv * gg[:, SG_WIDTH:]).astype(BF16)

    vec, mat = _sg_specs()
    mat_shape = jax.ShapeDtypeStruct((SG_GROUPS, SG_CHUNK, SG_CHUNK), F32)
    vec_shape = jax.ShapeDtypeStruct((1, SG_WIDTH), F32)
    return pl.pallas_call(
        body, name=name,
        out_shape=(jax.ShapeDtypeStruct((s, 2 * SG_WIDTH), BF16), vec_shape, vec_shape, mat_shape, mat_shape),
        grid=(s // TOKEN_TILE,),
        in_specs=[pl.BlockSpec((TOKEN_TILE, 2 * SG_WIDTH), lambda i: (i, 0)),
                  pl.BlockSpec((TOKEN_TILE, SG_WIDTH), lambda i: (i, 0)), vec, vec, mat, mat],
        out_specs=(pl.BlockSpec((TOKEN_TILE, 2 * SG_WIDTH), lambda i: (i, 0)), vec, vec, mat, mat),
        scratch_shapes=[pltpu.VMEM((TOKEN_TILE, SG_WIDTH), F32), pltpu.VMEM((TOKEN_TILE, SG_WIDTH), F32)],
        compiler_params=_params(("arbitrary",)))(proj, dy, ln_g, ln_b, w_s, b_s)


_B_A_BLOCK = 1024 // LANES
_B_G_BLOCK = 1536 // LANES
_CH_TILES = CV_WIDTH // LANES


def _col_spec(s, first_block):
    return pl.BlockSpec((s, LANES), lambda j: (0, first_block + j))


def conv_b_fwd(proj, w_pad, bias, name):
    s = proj.shape[0]

    def body(a_ref, g_ref, w_ref, b_ref, c_ref, upad):
        upad[0:HALO, :] = jnp.zeros((HALO, LANES), F32)
        upad[HALO:, :] = a_ref[...].astype(F32) * _sigmoid(g_ref[...].astype(F32))
        w = w_ref[...]
        bv = b_ref[...]

        def block(bi, carry):
            start = pl.multiple_of(bi * CONV_ROWS, CONV_ROWS)
            win = upad[pl.ds(start, CONV_ROWS + HALO), :]
            acc = jnp.zeros((CONV_ROWS, LANES), F32)
            for k in range(CV_KERNEL):
                acc = acc + w[k:k + 1, :] * _shift_rows(win, HALO - (CV_KERNEL - 1) + k, CONV_ROWS)
            c_ref[pl.ds(start, CONV_ROWS), :] = acc + bv
            return carry

        lax.fori_loop(0, s // CONV_ROWS, block, 0)

    return pl.pallas_call(
        body, name=name, out_shape=jax.ShapeDtypeStruct((s, CV_WIDTH), F32), grid=(_CH_TILES,),
        in_specs=[_col_spec(s, _B_A_BLOCK), _col_spec(s, _B_G_BLOCK), _col_spec(HALO, 0), _col_spec(1, 0)],
        out_specs=_col_spec(s, 0), scratch_shapes=[pltpu.VMEM((s + HALO, LANES), F32)],
        compiler_params=_params(("parallel",)))(proj, proj, w_pad, bias)


def conv_b_bwd(proj, w_pad, dc, name):
    s = proj.shape[0]

    def body(a_ref, g_ref, w_ref, dc_ref, da_ref, dg_ref, dw_ref, db_ref, upad, dpad, dw_scr):
        upad[0:HALO, :] = jnp.zeros((HALO, LANES), F32)
        upad[HALO:, :] = a_ref[...].astype(F32) * _sigmoid(g_ref[...].astype(F32))
        dcv = dc_ref[...]
        dpad[0:s, :] = dcv
        dpad[s:, :] = jnp.zeros((HALO, LANES), F32)
        db_ref[...] = jnp.sum(dcv, axis=0, keepdims=True)
        dw_scr[...] = jnp.zeros((HALO, LANES), F32)
        w = w_ref[...]

        def block(bi, carry):
            start = pl.multiple_of(bi * CONV_ROWS, CONV_ROWS)
            uwin = upad[pl.ds(start, CONV_ROWS + HALO), :]
            dwin = dpad[pl.ds(start, CONV_ROWS + HALO), :]
            dcb = dwin[:CONV_ROWS]
            du = jnp.zeros((CONV_ROWS, LANES), F32)
            for k in range(CV_KERNEL):
                du = du + w[k:k + 1, :] * _shift_rows(dwin, CV_KERNEL - 1 - k, CONV_ROWS)
                ush = _shift_rows(uwin, HALO - (CV_KERNEL - 1) + k, CONV_ROWS)
                dw_scr[k:k + 1, :] += jnp.sum(dcb * ush, axis=0, keepdims=True)
            av = a_ref[pl.ds(start, CONV_ROWS), :].astype(F32)
            sg = _sigmoid(g_ref[pl.ds(start, CONV_ROWS), :].astype(F32))
            da_ref[pl.ds(start, CONV_ROWS), :] = (du * sg).astype(BF16)
            dg_ref[pl.ds(start, CONV_ROWS), :] = (du * av * sg * (1.0 - sg)).astype(BF16)
            return carry

        lax.fori_loop(0, s // CONV_ROWS, block, 0)
        dw_ref[...] = dw_scr[...]

    act = jax.ShapeDtypeStruct((s, CV_WIDTH), BF16)
    return pl.pallas_call(
        body, name=name,
        out_shape=(act, act, jax.ShapeDtypeStruct((HALO, CV_WIDTH), F32), jax.ShapeDtypeStruct((1, CV_WIDTH), F32)),
        grid=(_CH_TILES,),
        in_specs=[_col_spec(s, _B_A_BLOCK), _col_spec(s, _B_G_BLOCK), _col_spec(HALO, 0), _col_spec(s, 0)],
        out_specs=(_col_spec(s, 0), _col_spec(s, 0), _col_spec(HALO, 0), _col_spec(1, 0)),
        scratch_shapes=[pltpu.VMEM((s + HALO, LANES), F32), pltpu.VMEM((s + HALO, LANES), F32),
                        pltpu.VMEM((HALO, LANES), F32)],
        compiler_params=_params(("parallel",)))(proj, proj, w_pad, dc)


def ln_silu_fwd(c, ln_g, ln_b, name):
    s, d = c.shape

    def body(c_ref, g_ref, b_ref, o_ref):
        xhat, _ = _ln_stats(c_ref[...])
        cn = xhat * g_ref[...] + b_ref[...]
        o_ref[...] = (cn * _sigmoid(cn)).astype(BF16)

    tile = pl.BlockSpec((TOKEN_TILE, d), lambda i: (i, 0))
    return pl.pallas_call(
        body, name=name, out_shape=jax.ShapeDtypeStruct((s, d), BF16), grid=(s // TOKEN_TILE,),
        in_specs=[tile, _row_spec(d), _row_spec(d)], out_specs=tile,
        compiler_params=_params(("parallel",)))(c, ln_g, ln_b)


def ln_silu_bwd(c, dy, ln_g, ln_b, name):
    s, d = c.shape

    def body(c_ref, dy_ref, g_ref, b_ref, dc_ref, dg_ref, db_ref):
        first = pl.program_id(0) == 0
        xhat, rstd = _ln_stats(c_ref[...])
        gv = g_ref[...]
        cn = xhat * gv + b_ref[...]
        sg = _sigmoid(cn)
        dcn = dy_ref[...].astype(F32) * sg * (1.0 + cn * (1.0 - sg))
        _accumulate(dg_ref, jnp.sum(dcn * xhat, axis=0, keepdims=True), first)
        _accumulate(db_ref, jnp.sum(dcn, axis=0, keepdims=True), first)
        dc_ref[...] = _ln_bwd(dcn, xhat, rstd, gv)

    tile = pl.BlockSpec((TOKEN_TILE, d), lambda i: (i, 0))
    vec_shape = jax.ShapeDtypeStruct((1, d), F32)
    return pl.pallas_call(
        body, name=name, out_shape=(jax.ShapeDtypeStruct((s, d), F32), vec_shape, vec_shape),
        grid=(s // TOKEN_TILE,), in_specs=[tile, tile, _row_spec(d), _row_spec(d)],
        out_specs=(tile, _row_spec(d), _row_spec(d)), compiler_params=_params(("arbitrary",)))(c, dy, ln_g, ln_b)


_D_BLOCK = 2816 // LANES


def _conv3(win, w):
    acc = jnp.zeros((CONV_ROWS, LANES), F32)
    for k in range(SC_KERNEL):
        acc = acc + w[k:k + 1, :] * _shift_rows(win, HALO - (SC_KERNEL - 1) + k, CONV_ROWS)
    return acc


def conv_d_fwd(proj, w_pad, name):
    s = proj.shape[0]

    def body(bg_ref, cg_ref, h_ref, w_ref, o_ref, ppad):
        ppad[0:HALO, :] = jnp.zeros((HALO, LANES), F32)
        ppad[HALO:, :] = cg_ref[...].astype(F32) * h_ref[...].astype(F32)
        w = w_ref[...]

        def block(bi, carry):
            start = pl.multiple_of(bi * CONV_ROWS, CONV_ROWS)
            cv = _conv3(ppad[pl.ds(start, CONV_ROWS + HALO), :], w)
            o_ref[pl.ds(start, CONV_ROWS), :] = (bg_ref[pl.ds(start, CONV_ROWS), :].astype(F32) * cv).astype(BF16)
            return carry

        lax.fori_loop(0, s // CONV_ROWS, block, 0)

    return pl.pallas_call(
        body, name=name, out_shape=jax.ShapeDtypeStruct((s, SC_WIDTH), BF16), grid=(_CH_TILES,),
        in_specs=[_col_spec(s, _D_BLOCK), _col_spec(s, _D_BLOCK + _CH_TILES), _col_spec(s, _D_BLOCK + 2 * _CH_TILES),
                  _col_spec(SUBLANES, 0)],
        out_specs=_col_spec(s, 0), scratch_shapes=[pltpu.VMEM((s + HALO, LANES), F32)],
        compiler_params=_params(("parallel",)))(proj, proj, proj, w_pad)


def conv_d_bwd(proj, w_pad, dy, name):
    s = proj.shape[0]

    def body(bg_ref, cg_ref, h_ref, w_ref, dy_ref, dbg_ref, dcg_ref, dh_ref, dw_ref, ppad, dpad, dw_scr):
        ppad[0:HALO, :] = jnp.zeros((HALO, LANES), F32)
        ppad[HALO:, :] = cg_ref[...].astype(F32) * h_ref[...].astype(F32)
        dpad[0:s, :] = dy_ref[...].astype(F32) * bg_ref[...].astype(F32)
        dpad[s:, :] = jnp.zeros((HALO, LANES), F32)
        dw_scr[...] = jnp.zeros((SUBLANES, LANES), F32)
        w = w_ref[...]

        def block(bi, carry):
            start = pl.multiple_of(bi * CONV_ROWS, CONV_ROWS)
            rows = pl.ds(start, CONV_ROWS)
            pwin = ppad[pl.ds(start, CONV_ROWS + HALO), :]
            dwin = dpad[pl.ds(start, CONV_ROWS + HALO), :]
            dcvb = dwin[:CONV_ROWS]
            dbg_ref[rows, :] = (dy_ref[rows, :].astype(F32) * _conv3(pwin, w)).astype(BF16)
            dp = jnp.zeros((CONV_ROWS, LANES), F32)
            for k in range(SC_KERNEL):
                dp = dp + w[k:k + 1, :] * _shift_rows(dwin, SC_KERNEL - 1 - k, CONV_ROWS)
                psh = _shift_rows(pwin, HALO - (SC_KERNEL - 1) + k, CONV_ROWS)
                dw_scr[k:k + 1, :] += jnp.sum(dcvb * psh, axis=0, keepdims=True)
            dcg_ref[rows, :] = (dp * h_ref[rows, :].astype(F32)).astype(BF16)
            dh_ref[rows, :] = (dp * cg_ref[rows, :].astype(F32)).astype(BF16)
            return carry

        lax.fori_loop(0, s // CONV_ROWS, block, 0)
        dw_ref[...] = dw_scr[...]

    act = jax.ShapeDtypeStruct((s, SC_WIDTH), BF16)
    return pl.pallas_call(
        body, name=name, out_shape=(act, act, act, jax.ShapeDtypeStruct((SUBLANES, SC_WIDTH), F32)),
        grid=(_CH_TILES,),
        in_specs=[_col_spec(s, _D_BLOCK), _col_spec(s, _D_BLOCK + _CH_TILES), _col_spec(s, _D_BLOCK + 2 * _CH_TILES),
                  _col_spec(SUBLANES, 0), _col_spec(s, 0)],
        out_specs=(_col_spec(s, 0), _col_spec(s, 0), _col_spec(s, 0), _col_spec(SUBLANES, 0)),
        scratch_shapes=[pltpu.VMEM((s + HALO, LANES), F32), pltpu.VMEM((s + HALO, LANES), F32),
                        pltpu.VMEM((SUBLANES, LANES), F32)],
        compiler_params=_params(("parallel",)))(proj, proj, proj, w_pad, dy)


_QK_BLOCK = 2048 // LANES
_QK_BLOCKS = (Q_WIDTH + KV_WIDTH) // LANES


def _swap_halves(t):
    lane = lax.broadcasted_iota(jnp.int32, t.shape, 1)
    low = (lane % HEAD_DIM) < (HEAD_DIM // 2)
    return jnp.where(low, pltpu.roll(t, LANES - HEAD_DIM // 2, axis=1), pltpu.roll(t, HEAD_DIM // 2, axis=1))


def rope_fwd(proj, cos_t, sin_t, name):
    s = proj.shape[0]

    def body(t_ref, c_ref, s_ref, o_ref):
        t = t_ref[...].astype(F32)
        o_ref[...] = (t * c_ref[...] + _swap_halves(t) * s_ref[...]).astype(BF16)

    tab = pl.BlockSpec((TOKEN_TILE, LANES), lambda i, j: (i, 0))
    return pl.pallas_call(
        body, name=name, out_shape=jax.ShapeDtypeStruct((s, Q_WIDTH + KV_WIDTH), BF16),
        grid=(s // TOKEN_TILE, _QK_BLOCKS),
        in_specs=[pl.BlockSpec((TOKEN_TILE, LANES), lambda i, j: (i, _QK_BLOCK + j)), tab, tab],
        out_specs=pl.BlockSpec((TOKEN_TILE, LANES), lambda i, j: (i, j)),
        compiler_params=_params(("parallel", "parallel")))(proj, cos_t, sin_t)


def rope_bwd(d_cur, d_prev, cos_t, sin_t, name):
    s, w = d_cur.shape

    def body(a_ref, b_ref, c_ref, s_ref, o_ref):
        d = a_ref[...] + b_ref[...]
        o_ref[...] = (d * c_ref[...] + _swap_halves(d) * s_ref[...]).astype(BF16)

    tab = pl.BlockSpec((TOKEN_TILE, LANES), lambda i, j: (i, 0))
    blk = pl.BlockSpec((TOKEN_TILE, LANES), lambda i, j: (i, j))
    return pl.pallas_call(
        body, name=name, out_shape=jax.ShapeDtypeStruct((s, w), BF16), grid=(s // TOKEN_TILE, w // LANES),
        in_specs=[blk, blk, tab, tab], out_specs=blk,
        compiler_params=_params(("parallel", "parallel")))(d_cur, d_prev, cos_t, sin_t)


_GROUP = N_Q_HEADS // N_KV_HEADS
_NEG = -1e30


def _attn_specs():
    q_spec = pl.BlockSpec((_GROUP, WINDOW, HEAD_DIM), lambda h, n: (h, n, 0))
    cur = pl.BlockSpec((1, WINDOW, HEAD_DIM), lambda h, n: (h, n, 0))
    prev = pl.BlockSpec((1, WINDOW, HEAD_DIM), lambda h, n: (h, jnp.maximum(n - 1, 0), 0))
    sink = pl.BlockSpec((_GROUP, 1, LANES), lambda h, n: (h, 0, 0))
    return q_spec, cur, prev, sink


def _attn_valid(n):
    qi = lax.broadcasted_iota(jnp.int32, (WINDOW, 2 * WINDOW), 0)
    kj = lax.broadcasted_iota(jnp.int32, (WINDOW, 2 * WINDOW), 1)
    delta = qi + WINDOW - kj
    return (delta >= 0) & (delta < WINDOW) & ((kj >= WINDOW) | (n > 0))


def _attn_probs(q, kcat, valid, sink_row):
    sc = lax.dot_general(q, kcat, _DIMS["nt"], preferred_element_type=F32) * (HEAD_DIM ** -0.5)
    sc = jnp.where(valid, sc, _NEG)
    sink = jnp.max(sink_row, axis=-1, keepdims=True)
    m = jnp.maximum(jnp.max(sc, axis=-1, keepdims=True), sink)
    p = jnp.where(valid, jnp.exp(sc - m), 0.0)
    es = jnp.exp(sink - m)
    inv = 1.0 / (jnp.sum(p, axis=-1, keepdims=True) + es)
    return p * inv, es * inv


def attention_fwd(qh, kh, vh, sinks_b, name):
    s = qh.shape[1]

    def body(q_ref, kc_ref, kp_ref, vc_ref, vp_ref, sk_ref, o_ref):
        valid = _attn_valid(pl.program_id(1))
        kcat = jnp.concatenate([kp_ref[0], kc_ref[0]], axis=0)
        vcat = jnp.concatenate([vp_ref[0], vc_ref[0]], axis=0)
        for g in range(_GROUP):
            probs, _ = _attn_probs(q_ref[g], kcat, valid, sk_ref[g])
            o_ref[g] = jnp.dot(probs.astype(BF16), vcat, preferred_element_type=F32).astype(BF16)

    q_spec, cur, prev, sink = _attn_specs()
    return pl.pallas_call(
        body, name=name, out_shape=jax.ShapeDtypeStruct(qh.shape, BF16), grid=(N_KV_HEADS, s // WINDOW),
        in_specs=[q_spec, cur, prev, cur, prev, sink], out_specs=q_spec,
        compiler_params=_params(("parallel", "parallel")))(qh, kh, kh, vh, vh, sinks_b)


def attention_bwd(qh, kh, vh, sinks_b, doh, name):
    s = qh.shape[1]

    def body(q_ref, kc_ref, kp_ref, vc_ref, vp_ref, sk_ref, do_ref, dq_ref, dkc_ref, dkp_ref, dvc_ref, dvp_ref, ds_ref):
        n = pl.program_id(1)
        valid = _attn_valid(n)
        kcat = jnp.concatenate([kp_ref[0], kc_ref[0]], axis=0)
        vcat = jnp.concatenate([vp_ref[0], vc_ref[0]], axis=0)
        dk = jnp.zeros((2 * WINDOW, HEAD_DIM), F32)
        dv = jnp.zeros((2 * WINDOW, HEAD_DIM), F32)
        for g in range(_GROUP):
            q = q_ref[g]
            do = do_ref[g]
            probs, ps = _attn_probs(q, kcat, valid, sk_ref[g])
            dprobs = lax.dot_general(do, vcat, _DIMS["nt"], preferred_element_type=F32)
            dv = dv + lax.dot_general(probs.astype(BF16), do, _DIMS["tn"], preferred_element_type=F32)
            rs = jnp.sum(probs * dprobs, axis=-1, keepdims=True)
            dsb = (probs * (dprobs - rs) * (HEAD_DIM ** -0.5)).astype(BF16)
            dq_ref[g] = jnp.dot(dsb, kcat, preferred_element_type=F32)
            dk = dk + lax.dot_general(dsb, q, _DIMS["tn"], preferred_element_type=F32)
            dsink = jnp.broadcast_to(-jnp.sum(ps * rs, axis=0, keepdims=True), (1, LANES))

            @pl.when(n == 0)
            def _():
                ds_ref[g] = dsink

            @pl.when(n > 0)
            def _():
                ds_ref[g] += dsink

        dkp_ref[0] = dk[:WINDOW]
        dkc_ref[0] = dk[WINDOW:]
        dvp_ref[0] = dv[:WINDOW]
        dvc_ref[0] = dv[WINDOW:]

    q_spec, cur, prev, sink = _attn_specs()
    kv_shape = jax.ShapeDtypeStruct(kh.shape, F32)
    return pl.pallas_call(
        body, name=name,
        out_shape=(jax.ShapeDtypeStruct(qh.shape, F32), kv_shape, kv_shape, kv_shape, kv_shape,
                   jax.ShapeDtypeStruct(sinks_b.shape, F32)),
        grid=(N_KV_HEADS, s // WINDOW), in_specs=[q_spec, cur, prev, cur, prev, sink, q_spec],
        out_specs=(q_spec, cur, cur, cur, cur, sink),
        compiler_params=_params(("parallel", "arbitrary")))(qh, kh, kh, vh, vh, sinks_b, doh)


def _to_heads(t, heads):
    return t.reshape(t.shape[0], heads, HEAD_DIM).transpose(1, 0, 2)


def _from_heads(t):
    return t.transpose(1, 0, 2).reshape(t.shape[1], t.shape[0] * HEAD_DIM)


def _shift_window(t):
    return jnp.concatenate([t[:, WINDOW:], jnp.zeros_like(t[:, :WINDOW])], axis=1)


def merge_fwd(zg, branches, name):
    s = zg.shape[0]

    def body(zg_ref, b0, b1, b2, b3, o_ref):
        acc = jnp.zeros((TOKEN_TILE, D_MODEL), F32)
        for n, b_ref in enumerate((b0, b1, b2, b3)):
            gate = _sigmoid(zg_ref[:, n * D_MODEL:(n + 1) * D_MODEL].astype(F32))
            acc = acc + gate * b_ref[...].astype(F32)
        o_ref[...] = acc.astype(BF16)

    tile = pl.BlockSpec((TOKEN_TILE, D_MODEL), lambda i: (i, 0))
    wide = pl.BlockSpec((TOKEN_TILE, N_BRANCH * D_MODEL), lambda i: (i, 0))
    return pl.pallas_call(
        body, name=name, out_shape=jax.ShapeDtypeStruct((s, D_MODEL), BF16), grid=(s // TOKEN_TILE,),
        in_specs=[wide, tile, tile, tile, tile], out_specs=tile,
        compiler_params=_params(("parallel",)))(zg, *branches)


def merge_bwd(zg, branches, dm, name):
    s = zg.shape[0]

    def body(zg_ref, b0, b1, b2, b3, dm_ref, dzg_ref, d0, d1, d2, d3):
        dmv = dm_ref[...].astype(F32)
        for n, (b_ref, d_ref) in enumerate(((b0, d0), (b1, d1), (b2, d2), (b3, d3))):
            cols = slice(n * D_MODEL, (n + 1) * D_MODEL)
            gate = _sigmoid(zg_ref[:, cols].astype(F32))
            d_ref[...] = (gate * dmv).astype(BF16)
            dzg_ref[:, cols] = (dmv * b_ref[...].astype(F32) * gate * (1.0 - gate)).astype(BF16)

    tile = pl.BlockSpec((TOKEN_TILE, D_MODEL), lambda i: (i, 0))
    wide = pl.BlockSpec((TOKEN_TILE, N_BRANCH * D_MODEL), lambda i: (i, 0))
    act = jax.ShapeDtypeStruct((s, D_MODEL), BF16)
    return pl.pallas_call(
        body, name=name, out_shape=(jax.ShapeDtypeStruct((s, N_BRANCH * D_MODEL), BF16), act, act, act, act),
        grid=(s // TOKEN_TILE,), in_specs=[wide, tile, tile, tile, tile, tile],
        out_specs=(wide, tile, tile, tile, tile), compiler_params=_params(("parallel",)))(zg, *branches, dm)


def swiglu_fwd(gu, name):
    s = gu.shape[0]

    def body(g_ref, u_ref, o_ref):
        gate = g_ref[...].astype(F32)
        o_ref[...] = (gate * _sigmoid(gate) * u_ref[...].astype(F32)).astype(BF16)

    return pl.pallas_call(
        body, name=name, out_shape=jax.ShapeDtypeStruct((s, D_FF), BF16), grid=(s // TOKEN_TILE,),
        in_specs=[pl.BlockSpec((TOKEN_TILE, D_FF), lambda i: (i, 0)), pl.BlockSpec((TOKEN_TILE, D_FF), lambda i: (i, 1))],
        out_specs=pl.BlockSpec((TOKEN_TILE, D_FF), lambda i: (i, 0)), compiler_params=_params(("parallel",)))(gu, gu)


def swiglu_bwd(gu, dact, name):
    s = gu.shape[0]

    def body(g_ref, u_ref, da_ref, o_ref):
        gate = g_ref[...].astype(F32)
        sg = _sigmoid(gate)
        da = da_ref[...].astype(F32)
        o_ref[:, :D_FF] = (da * u_ref[...].astype(F32) * sg * (1.0 + gate * (1.0 - sg))).astype(BF16)
        o_ref[:, D_FF:] = (da * gate * sg).astype(BF16)

    half = pl.BlockSpec((TOKEN_TILE, D_FF), lambda i: (i, 0))
    return pl.pallas_call(
        body, name=name, out_shape=jax.ShapeDtypeStruct((s, 2 * D_FF), BF16), grid=(s // TOKEN_TILE,),
        in_specs=[half, pl.BlockSpec((TOKEN_TILE, D_FF), lambda i: (i, 1)), half],
        out_specs=pl.BlockSpec((TOKEN_TILE, 2 * D_FF), lambda i: (i, 0)),
        compiler_params=_params(("parallel",)))(gu, gu, dact)


ADAMW_BLOCK_BYTES = 1 << 20


def adamw(parts, w, m, v, name):
    r, c = w.shape
    tr = _divisor_tile(r, max(SUBLANES, ADAMW_BLOCK_BYTES // (4 * c)), SUBLANES)

    def body(p_ref, w_ref, m_ref, v_ref, g_ref, d_ref, nm_ref, nv_ref):
        g = p_ref[0].astype(F32)
        for i in range(1, N_DEV):
            g = g + p_ref[i].astype(F32)
        nm = ADAM_B1 * m_ref[...] + (1.0 - ADAM_B1) * g
        nv = ADAM_B2 * v_ref[...] + (1.0 - ADAM_B2) * (g * g)
        m_hat = nm / (1.0 - ADAM_B1 ** ADAM_STEP)
        v_hat = nv / (1.0 - ADAM_B2 ** ADAM_STEP)
        g_ref[...] = g
        d_ref[...] = -ADAM_LR * (m_hat / (jnp.sqrt(v_hat) + ADAM_EPS) + ADAM_WD * w_ref[...])
        nm_ref[...] = nm
        nv_ref[...] = nv

    tile = pl.BlockSpec((tr, c), lambda i: (i, 0))
    shape = jax.ShapeDtypeStruct((r, c), F32)
    return pl.pallas_call(
        body, name=name, out_shape=(shape, shape, shape, shape), grid=(r // tr,),
        in_specs=[pl.BlockSpec((N_DEV, tr, c), lambda i: (0, i, 0)), tile, tile, tile],
        out_specs=(tile, tile, tile, tile), compiler_params=_params(("parallel",)))(parts, w, m, v)


_RELATIONS = [(a, b, e) for a in (0, 1) for b in (0, 1) for e in (0, 1)][1:]


def exchange(arrays, scatter, name):
    n = len(arrays)
    n_rel = len(_RELATIONS)

    def body(*refs):
        ins, outs = refs[:n], refs[n:2 * n]
        send_sems, recv_sems, local_sems = refs[2 * n:]
        x, y, c = lax.axis_index("x"), lax.axis_index("y"), lax.axis_index("c")
        me = 4 * x + 2 * y + c
        copies = []
        for t in range(n):
            own = ins[t].at[me] if scatter[t] else ins[t]
            local = pltpu.make_async_copy(own, outs[t].at[me], local_sems.at[t])
            local.start()
            copies.append(local)
            for k, (a, b, e) in enumerate(_RELATIONS):
                px, py, pc = (x + a) % 2, (y + b) % 2, (c + e) % 2
                src = ins[t].at[4 * px + 2 * py + pc] if scatter[t] else ins[t]
                remote = pltpu.make_async_remote_copy(
                    src_ref=src, dst_ref=outs[t].at[me], send_sem=send_sems.at[t, k], recv_sem=recv_sems.at[t, k],
                    device_id=(px, py, pc), device_id_type=pl.DeviceIdType.MESH)
                remote.start()
                copies.append(remote)
        for cp in copies:
            cp.wait()

    out_shape = tuple(
        jax.ShapeDtypeStruct(a.shape if scatter[t] else (N_DEV,) + a.shape, a.dtype) for t, a in enumerate(arrays))
    any_spec = pl.BlockSpec(memory_space=pl.ANY)
    return pl.pallas_call(
        body, name=name, out_shape=out_shape, in_specs=[any_spec] * n, out_specs=tuple([any_spec] * n),
        scratch_shapes=[pltpu.SemaphoreType.DMA((n, n_rel)), pltpu.SemaphoreType.DMA((n, n_rel)),
                        pltpu.SemaphoreType.DMA((n,))],
        compiler_params=pltpu.CompilerParams(has_side_effects=True))(*arrays)


_SMALL = ("norm_mix", "sg_ln_g", "sg_ln_b", "sg_w", "sg_b", "cv_b", "cv_ln_g", "cv_ln_b", "attn_sinks", "norm_ffn",
          "norm_final")
_PACK_UNIT = SUBLANES * LANES


def _pack(tensors):
    rows = []
    for t in tensors:
        flat = t.reshape(-1)
        pad = (-flat.shape[0]) % _PACK_UNIT
        rows.append(jnp.pad(flat, (0, pad)).reshape(-1, LANES))
    return jnp.concatenate(rows, axis=0)


def _unpack(packed, like):
    out, row = [], 0
    for t in like:
        size = 1
        for d in t.shape:
            size *= d
        rows = -(-size // _PACK_UNIT) * SUBLANES
        out.append(packed[row:row + rows].reshape(-1)[:size].reshape(t.shape))
        row += rows
    return out


def _layer_fwd(l, x, p):
    tag = f"l{l}_"
    xn = rmsnorm_fwd(x, p["norm_mix"], tag + "norm_mix")
    proj = matmul(xn, p["w_in_a"], "nn", BF16, tag + "proj_a")
    zg = matmul(xn, p["w_in_g"], "nn", BF16, tag + "proj_g")
    y_a = mixer_a_fwd(proj, p["sg_ln_g"], p["sg_ln_b"], p["sg_w"], p["sg_b"], tag + "mix_a")
    conv = conv_b_fwd(proj, p["cv_w"], p["cv_b"], tag + "conv_b")
    y_b = ln_silu_fwd(conv, p["cv_ln_g"], p["cv_ln_b"], tag + "ln_silu")
    qk = rope_fwd(proj, p["cos"], p["sin"], tag + "rope")
    qh = _to_heads(qk[:, :Q_WIDTH], N_Q_HEADS)
    kh = _to_heads(qk[:, Q_WIDTH:], N_KV_HEADS)
    vh = _to_heads(proj[:, 2688:2816], N_KV_HEADS)
    oh = attention_fwd(qh, kh, vh, p["sinks"], tag + "attn")
    y_c = _from_heads(oh)
    y_d = conv_d_fwd(proj, p["sc_w"], tag + "conv_d")
    ys = (y_a, y_b, y_c, y_d)
    branches = tuple(matmul(ys[n], p["w_branch"][n], "nn", BF16, tag + f"branch{n}") for n in range(N_BRANCH))
    merged = merge_fwd(zg, branches, tag + "merge")
    x_mid = matmul(merged, p["w_out"], "nn", F32, tag + "out", add=x)
    hn = rmsnorm_fwd(x_mid, p["norm_ffn"], tag + "norm_ffn")
    gu = matmul(hn, p["w_gate_up"], "nn", BF16, tag + "gate_up", tn_cap=512)
    act = swiglu_fwd(gu, tag + "swiglu")
    x_out = matmul(act, p["w_down"], "nn", F32, tag + "down", add=x_mid)
    saved = dict(x=x, xn=xn, proj=proj, zg=zg, conv=conv, qh=qh, kh=kh, vh=vh, ys=ys, branches=branches,
                 merged=merged, x_mid=x_mid, hn=hn, gu=gu, act=act)
    return x_out, saved


def _layer_bwd(l, dx_out, p, sv):
    tag = f"l{l}_b_"
    g = {}
    dact = matmul(dx_out, p["w_down"], "nt", BF16, tag + "dact")
    g["w_down"] = matmul(sv["act"], dx_out, "tn", BF16, tag + "dw_down")
    dgu = swiglu_bwd(sv["gu"], dact, tag + "swiglu")
    dhn = matmul(dgu, p["w_gate_up"], "nt", BF16, tag + "dhn", tm_cap=256, tn_cap=256)
    g["w_gate_up"] = matmul(sv["hn"], dgu, "tn", BF16, tag + "dw_gate_up")
    dx_mid, g["norm_ffn"] = rmsnorm_bwd(sv["x_mid"], p["norm_ffn"], dhn, dx_out, tag + "norm_ffn")
    dmerged = matmul(dx_mid, p["w_out"], "nt", BF16, tag + "dmerged")
    g["w_out"] = matmul(sv["merged"], dx_mid, "tn", BF16, tag + "dw_out")
    dzg, *dbranches = merge_bwd(sv["zg"], sv["branches"], dmerged, tag + "merge")
    dys = [matmul(dbranches[n], p["w_branch"][n], "nt", BF16, tag + f"dy{n}") for n in range(N_BRANCH)]
    g["w_branch"] = jnp.stack(
        [matmul(sv["ys"][n], dbranches[n], "tn", BF16, tag + f"dw_branch{n}") for n in range(N_BRANCH)])
    proj = sv["proj"]
    dz_a, g["sg_ln_g"], g["sg_ln_b"], g["sg_w"], dsb = mixer_a_bwd(
        proj, dys[0], p["sg_ln_g"], p["sg_ln_b"], p["sg_w"], p["sg_b"], tag + "mix_a")
    g["sg_b"] = dsb[:, :, 0]
    dconv, g["cv_ln_g"], g["cv_ln_b"] = ln_silu_bwd(sv["conv"], dys[1], p["cv_ln_g"], p["cv_ln_b"], tag + "ln_silu")
    da, dgate, dcw, g["cv_b"] = conv_b_bwd(proj, p["cv_w"], dconv, tag + "conv_b")
    g["cv_w"] = dcw[:CV_KERNEL]
    doh = _to_heads(dys[2], N_Q_HEADS)
    dqh, dkc, dkp, dvc, dvp, dsk = attention_bwd(sv["qh"], sv["kh"], sv["vh"], p["sinks"], doh, tag + "attn")
    g["attn_sinks"] = dsk[:, 0, 0]
    dqk_cur = jnp.concatenate([_from_heads(dqh), _from_heads(dkc)], axis=1)
    dqk_prev = jnp.concatenate([jnp.zeros((SEQ, Q_WIDTH), F32), _from_heads(_shift_window(dkp))], axis=1)
    dqk = rope_bwd(dqk_cur, dqk_prev, p["cos"], -p["sin"], tag + "rope")
    dv = (_from_heads(dvc) + _from_heads(_shift_window(dvp))).astype(BF16)
    dbg, dcg, dh, dsw = conv_d_bwd(proj, p["sc_w"], dys[3], tag + "conv_d")
    g["sc_w"] = dsw[:SC_KERNEL]
    dproj = jnp.concatenate([dz_a, da, dgate, dqk, dv, dbg, dcg, dh], axis=1)
    dxn = matmul(dproj, p["w_in_a"], "nt", F32, tag + "dxn_a", tm_cap=256, tn_cap=256)
    dxn = matmul(dzg, p["w_in_g"], "nt", F32, tag + "dxn_g", add=dxn, tm_cap=256, tn_cap=256)
    g["w_in"] = jnp.concatenate(
        [matmul(sv["xn"], dproj, "tn", BF16, tag + "dw_in_a"), matmul(sv["xn"], dzg, "tn", BF16, tag + "dw_in_g")],
        axis=1)
    dx_in, g["norm_mix"] = rmsnorm_bwd(sv["x"], p["norm_mix"], dxn, dx_mid, tag + "norm_mix")
    return dx_in, g


_BIG = ("w_in", "cv_w", "sc_w", "w_branch", "w_out", "w_gate_up", "w_down")


def _full_weight(name, gathered, l):
    t = gathered[:, l]
    if name in ("w_out", "w_down"):
        return t.reshape(-1, t.shape[-1])
    if name == "w_branch":
        return t.transpose(1, 2, 0, 3).reshape(N_BRANCH, SG_WIDTH, D_MODEL)
    return t.transpose(1, 0, 2).reshape(t.shape[1], -1)


def _to_blocks(name, full):
    if name in ("w_out", "w_down"):
        return full.reshape(DEPTH, N_DEV, -1, full.shape[-1]).transpose(1, 0, 2, 3)
    if name == "w_branch":
        return full.reshape(DEPTH, N_BRANCH, SG_WIDTH, N_DEV, -1).transpose(3, 0, 1, 2, 4)
    return full.reshape(DEPTH, full.shape[1], N_DEV, -1).transpose(2, 0, 1, 3)


def _rope_tables():
    pos = jnp.arange(SEQ, dtype=F32)
    inv_freq = 1.0 / (ROPE_THETA ** (jnp.arange(0, HEAD_DIM, 2, dtype=F32) / HEAD_DIM))
    ang = pos[:, None] * inv_freq[None, :]
    cos, sin = jnp.cos(ang), jnp.sin(ang)
    reps = LANES // HEAD_DIM
    return jnp.tile(jnp.concatenate([cos, cos], axis=1), (1, reps)), jnp.tile(jnp.concatenate([-sin, sin], axis=1), (1, reps))


def kernel(x, norm_mix, w_in, sg_ln_g, sg_ln_b, sg_w, sg_b, cv_w, cv_b, cv_ln_g, cv_ln_b, attn_sinks, sc_w, w_branch, w_out, norm_ffn, w_gate_up, w_down, norm_final, loss_target, m_norm_mix, m_w_in, m_sg_ln_g, m_sg_ln_b, m_sg_w, m_sg_b, m_cv_w, m_cv_b, m_cv_ln_g, m_cv_ln_b, m_attn_sinks, m_sc_w, m_w_branch, m_w_out, m_norm_ffn, m_w_gate_up, m_w_down, m_norm_final, v_norm_mix, v_w_in, v_sg_ln_g, v_sg_ln_b, v_sg_w, v_sg_b, v_cv_w, v_cv_b, v_cv_ln_g, v_cv_ln_b, v_attn_sinks, v_sc_w, v_w_branch, v_w_out, v_norm_ffn, v_w_gate_up, v_w_down, v_norm_final):
    names = ("norm_mix", "w_in", "sg_ln_g", "sg_ln_b", "sg_w", "sg_b", "cv_w", "cv_b", "cv_ln_g", "cv_ln_b",
             "attn_sinks", "sc_w", "w_branch", "w_out", "norm_ffn", "w_gate_up", "w_down", "norm_final")
    w = dict(zip(names, (norm_mix, w_in, sg_ln_g, sg_ln_b, sg_w, sg_b, cv_w, cv_b, cv_ln_g, cv_ln_b, attn_sinks,
                         sc_w, w_branch, w_out, norm_ffn, w_gate_up, w_down, norm_final)))
    m = dict(zip(names, (m_norm_mix, m_w_in, m_sg_ln_g, m_sg_ln_b, m_sg_w, m_sg_b, m_cv_w, m_cv_b, m_cv_ln_g,
                         m_cv_ln_b, m_attn_sinks, m_sc_w, m_w_branch, m_w_out, m_norm_ffn, m_w_gate_up, m_w_down,
                         m_norm_final)))
    v = dict(zip(names, (v_norm_mix, v_w_in, v_sg_ln_g, v_sg_ln_b, v_sg_w, v_sg_b, v_cv_w, v_cv_b, v_cv_ln_g,
                         v_cv_ln_b, v_attn_sinks, v_sc_w, v_w_branch, v_w_out, v_norm_ffn, v_w_gate_up, v_w_down,
                         v_norm_final)))

    gathered = dict(zip(_BIG, exchange([w[n].astype(BF16) for n in _BIG], [False] * len(_BIG), "gather_weights")))

    cos_t, sin_t = _rope_tables()
    layers = []
    for l in range(DEPTH):
        w_in_full = _full_weight("w_in", gathered["w_in"], l)
        layers.append(dict(
            norm_mix=w["norm_mix"][l][None], norm_ffn=w["norm_ffn"][l][None],
            w_in_a=w_in_full[:, :PROJ_A], w_in_g=w_in_full[:, PROJ_A:],
            sg_ln_g=w["sg_ln_g"][l][None], sg_ln_b=w["sg_ln_b"][l][None], sg_w=w["sg_w"][l],
            sg_b=jnp.broadcast_to(w["sg_b"][l][:, :, None], (SG_GROUPS, SG_CHUNK, LANES)),
            cv_w=jnp.pad(_full_weight("cv_w", gathered["cv_w"], l).astype(F32), ((0, HALO - CV_KERNEL), (0, 0))),
            cv_b=w["cv_b"][l][None], cv_ln_g=w["cv_ln_g"][l][None], cv_ln_b=w["cv_ln_b"][l][None],
            sinks=jnp.broadcast_to(w["attn_sinks"][l][:, None, None], (N_Q_HEADS, 1, LANES)),
            sc_w=jnp.pad(_full_weight("sc_w", gathered["sc_w"], l).astype(F32), ((0, SUBLANES - SC_KERNEL), (0, 0))),
            w_branch=_full_weight("w_branch", gathered["w_branch"], l),
            w_out=_full_weight("w_out", gathered["w_out"], l),
            w_gate_up=_full_weight("w_gate_up", gathered["w_gate_up"], l),
            w_down=_full_weight("w_down", gathered["w_down"], l),
            cos=cos_t, sin=sin_t))

    h = x[0]
    saved = []
    for l in range(DEPTH):
        h, sv = _layer_fwd(l, h, layers[l])
        saved.append(sv)
    loss_row, dh, d_norm_final = loss_head(h, w["norm_final"][None], loss_target[0], "loss_head")

    grads = [None] * DEPTH
    for l in reversed(range(DEPTH)):
        dh, grads[l] = _layer_bwd(l, dh, layers[l], saved[l])
    grad_x = dh[None]

    stacked = {n: jnp.stack([grads[l][n] for l in range(DEPTH)]) for n in names if n != "norm_final"}
    for n in ("norm_mix", "norm_ffn", "sg_ln_g", "sg_ln_b", "cv_b", "cv_ln_g", "cv_ln_b"):
        stacked[n] = stacked[n][:, 0]
    stacked["norm_final"] = d_norm_final[0]
    send = [_to_blocks(n, stacked[n].astype(BF16)) for n in _BIG]
    small_like = [w[n] for n in _SMALL]
    small_part = _pack([stacked[n] for n in _SMALL])
    received = exchange(send + [small_part], [True] * len(_BIG) + [False], "exchange_grads")

    out_g, out_d, out_m, out_v = {}, {}, {}, {}
    for i, n in enumerate(_BIG):
        shape = w[n].shape
        flat = (shape[0] * shape[1], shape[2]) if len(shape) == 3 else (shape[0] * shape[1] * shape[2], shape[3])
        res = adamw(received[i].reshape((N_DEV,) + flat), w[n].reshape(flat), m[n].reshape(flat), v[n].reshape(flat),
                    "adamw_" + n)
        out_g[n], out_d[n], out_m[n], out_v[n] = (t.reshape(shape) for t in res)
    res = adamw(received[-1], _pack(small_like), _pack([m[n] for n in _SMALL]), _pack([v[n] for n in _SMALL]),
                "adamw_small")
    for store, packed in zip((out_g, out_d, out_m, out_v), res):
        for n, t in zip(_SMALL, _unpack(packed, small_like)):
            store[n] = t

    loss = lax.psum(loss_row[0, 0], ("x", "y", "c"))
    return (loss, grad_x, *[out_g[n] for n in names], *[out_d[n] for n in names], *[out_m[n] for n in names],
            *[out_v[n] for n in names])
```

```python
import jax
import jax.numpy as jnp
from jax import lax
from jax.experimental import pallas as pl
from jax.experimental.pallas import tpu as pltpu

F32 = jnp.float32
BF16 = jnp.bfloat16

SEQ = 2048
D_MODEL = 1024
DEPTH = 2
SG_WIDTH = 512
SG_CHUNK = 128
SG_GROUPS = 4
CV_WIDTH = 512
CV_KERNEL = 31
HEAD_DIM = 64
N_Q_HEADS = 8
N_KV_HEADS = 2
Q_WIDTH = 512
KV_WIDTH = 128
WINDOW = 128
SC_WIDTH = 512
SC_KERNEL = 3
N_BRANCH = 4
D_FF = 2816
EPS = 1e-6
ROPE_THETA = 10000.0
PROJ_A = 4352
PROJ_WIDTH = 8448
N_DEV = 8

ADAM_LR = 0.001
ADAM_B1 = 0.9
ADAM_B2 = 0.999
ADAM_EPS = 1e-08
ADAM_WD = 0.01
ADAM_STEP = 10

LANES = 128
SUBLANES = 8
VMEM_LIMIT_BYTES = 48 * 1024 * 1024
HALO = 32
CONV_ROWS = 256
TOKEN_TILE = 256

_SQRT_HALF = 0.7071067811865476
_INV_SQRT_2PI = 0.3989422804014327


def _params(semantics=None):
    return pltpu.CompilerParams(dimension_semantics=semantics, vmem_limit_bytes=VMEM_LIMIT_BYTES)


def _divisor_tile(n, cap, unit):
    best = None
    for t in range(unit, min(n, cap) + 1, unit):
        if n % t == 0:
            best = t
    return best if best is not None else n


_DIMS = {"nn": (((1,), (0,)), ((), ())), "nt": (((1,), (1,)), ((), ())), "tn": (((0,), (0,)), ((), ()))}


def matmul(a, b, mode, out_dtype, name, add=None, tm_cap=512, tn_cap=512, after=None):
    if mode == "nn":
        (m, k), (_, n) = a.shape, b.shape
    elif mode == "nt":
        (m, k), (n, _) = a.shape, b.shape
    else:
        (k, m), (_, n) = a.shape, b.shape
    tm = _divisor_tile(m, tm_cap, LANES)
    tn = _divisor_tile(n, tn_cap, LANES)
    a_spec = pl.BlockSpec((k, tm), lambda i, j: (0, i)) if mode == "tn" else pl.BlockSpec((tm, k), lambda i, j: (i, 0))
    b_spec = pl.BlockSpec((tn, k), lambda i, j: (j, 0)) if mode == "nt" else pl.BlockSpec((k, tn), lambda i, j: (0, j))
    o_spec = pl.BlockSpec((tm, tn), lambda i, j: (i, j))
    dims = _DIMS[mode]

    def body(*refs):
        a_ref, b_ref = refs[0], refs[1]
        o_ref = refs[-1]
        acc = lax.dot_general(a_ref[...].astype(BF16), b_ref[...].astype(BF16), dims, preferred_element_type=F32)
        if add is not None:
            acc = acc + refs[2][...].astype(F32)
        o_ref[...] = acc.astype(out_dtype)

    operands = (a, b) + (() if add is None else (add,)) + (() if after is None else (after,))
    in_specs = [a_spec, b_spec] + ([o_spec] if add is not None else [])
    in_specs += [pl.BlockSpec(memory_space=pl.ANY)] if after is not None else []
    return pl.pallas_call(
        body, name=name, out_shape=jax.ShapeDtypeStruct((m, n), out_dtype), grid=(m // tm, n // tn),
        in_specs=in_specs, out_specs=o_spec, compiler_params=_params(("parallel", "parallel")))(*operands)


def _sigmoid(x):
    return 1.0 / (1.0 + jnp.exp(-x))


def _gelu(x):
    return 0.5 * x * (1.0 + lax.erf(x * _SQRT_HALF))


def _gelu_grad(x):
    return 0.5 * (1.0 + lax.erf(x * _SQRT_HALF)) + x * _INV_SQRT_2PI * jnp.exp(-0.5 * x * x)


def _rms_stats(x):
    r = lax.rsqrt(jnp.mean(x * x, axis=-1, keepdims=True) + EPS)
    return x * r, r


def _rms_bwd(dxn, xhat, r, g):
    h = dxn * g
    return r * (h - xhat * jnp.mean(h * xhat, axis=-1, keepdims=True))


def _ln_stats(x):
    mu = jnp.mean(x, axis=-1, keepdims=True)
    xc = x - mu
    rstd = lax.rsqrt(jnp.mean(xc * xc, axis=-1, keepdims=True) + EPS)
    return xc * rstd, rstd


def _ln_bwd(dy, xhat, rstd, g):
    dxhat = dy * g
    return rstd * (dxhat - jnp.mean(dxhat, axis=-1, keepdims=True)
                   - xhat * jnp.mean(dxhat * xhat, axis=-1, keepdims=True))


def _accumulate(ref, value, first):
    @pl.when(first)
    def _():
        ref[...] = value

    @pl.when(jnp.logical_not(first))
    def _():
        ref[...] += value


def _shift_rows(win, shift, n_out):
    n = win.shape[0]
    if shift % n == 0:
        return win[:n_out]
    return pltpu.roll(win, n - shift, axis=0)[:n_out]


def _row_spec(width):
    return pl.BlockSpec((1, width), lambda i: (0, 0))


def rmsnorm_fwd(x, g, name):
    s, d = x.shape

    def body(x_ref, g_ref, o_ref):
        xhat, _ = _rms_stats(x_ref[...])
        o_ref[...] = (xhat * g_ref[...]).astype(BF16)

    tile = pl.BlockSpec((TOKEN_TILE, d), lambda i: (i, 0))
    return pl.pallas_call(
        body, name=name, out_shape=jax.ShapeDtypeStruct((s, d), BF16), grid=(s // TOKEN_TILE,),
        in_specs=[tile, _row_spec(d)], out_specs=tile, compiler_params=_params(("parallel",)))(x, g)


def rmsnorm_bwd(x, g, dxn, dres, name):
    s, d = x.shape

    def body(x_ref, g_ref, dxn_ref, dres_ref, dx_ref, dg_ref):
        xhat, r = _rms_stats(x_ref[...])
        dxn_v = dxn_ref[...].astype(F32)
        dx_ref[...] = dres_ref[...] + _rms_bwd(dxn_v, xhat, r, g_ref[...])
        _accumulate(dg_ref, jnp.sum(dxn_v * xhat, axis=0, keepdims=True), pl.program_id(0) == 0)

    tile = pl.BlockSpec((TOKEN_TILE, d), lambda i: (i, 0))
    return pl.pallas_call(
        body, name=name, out_shape=(jax.ShapeDtypeStruct((s, d), F32), jax.ShapeDtypeStruct((1, d), F32)),
        grid=(s // TOKEN_TILE,), in_specs=[tile, _row_spec(d), tile, tile], out_specs=(tile, _row_spec(d)),
        compiler_params=_params(("arbitrary",)))(x, g, dxn, dres)


def loss_head(x, g, target, name):
    s, d = x.shape

    def body(x_ref, g_ref, t_ref, loss_ref, dx_ref, dg_ref):
        first = pl.program_id(0) == 0
        xhat, r = _rms_stats(x_ref[...])
        gv = g_ref[...]
        err = xhat * gv - t_ref[...]
        part = 0.5 * jnp.sum(jnp.sum(err * err, axis=-1, keepdims=True), axis=0, keepdims=True) / d
        _accumulate(loss_ref, jnp.broadcast_to(part, (1, LANES)), first)
        dy = err / d
        dx_ref[...] = _rms_bwd(dy, xhat, r, gv)
        _accumulate(dg_ref, jnp.sum(dy * xhat, axis=0, keepdims=True), first)

    tile = pl.BlockSpec((TOKEN_TILE, d), lambda i: (i, 0))
    return pl.pallas_call(
        body, name=name,
        out_shape=(jax.ShapeDtypeStruct((1, LANES), F32), jax.ShapeDtypeStruct((s, d), F32),
                   jax.ShapeDtypeStruct((1, d), F32)),
        grid=(s // TOKEN_TILE,), in_specs=[tile, _row_spec(d), tile],
        out_specs=(_row_spec(LANES), tile, _row_spec(d)), compiler_params=_params(("arbitrary",)))(x, g, target)


def _tril_mask():
    row = lax.broadcasted_iota(jnp.int32, (SG_CHUNK, SG_CHUNK), 0)
    col = lax.broadcasted_iota(jnp.int32, (SG_CHUNK, SG_CHUNK), 1)
    return row >= col


def _sg_specs():
    vec = _row_spec(SG_WIDTH)
    mat = pl.BlockSpec((SG_GROUPS, SG_CHUNK, SG_CHUNK), lambda i: (0, 0, 0))
    return vec, mat


def mixer_a_fwd(proj, ln_g, ln_b, w_s, b_s, name):
    s = proj.shape[0]
    chunks = TOKEN_TILE // SG_CHUNK

    def body(z_ref, lg_ref, lb_ref, w_ref, b_ref, o_ref):
        ge = _gelu(z_ref[...].astype(F32))
        u = ge[:, :SG_WIDTH]
        xhat, _ = _ln_stats(ge[:, SG_WIDTH:])
        vn = xhat * lg_ref[...] + lb_ref[...]
        tril = _tril_mask()
        for ci in range(chunks):
            rows = slice(ci * SG_CHUNK, (ci + 1) * SG_CHUNK)
            for g in range(SG_GROUPS):
                cols = slice(g * LANES, (g + 1) * LANES)
                wm = jnp.where(tril, w_ref[g], 0.0).astype(BF16)
                mixed = jnp.dot(wm, vn[rows, cols].astype(BF16), preferred_element_type=F32) + b_ref[g]
                o_ref[rows, cols] = (u[rows, cols] * mixed).astype(BF16)

    vec, mat = _sg_specs()
    return pl.pallas_call(
        body, name=name, out_shape=jax.ShapeDtypeStruct((s, SG_WIDTH), BF16), grid=(s // TOKEN_TILE,),
        in_specs=[pl.BlockSpec((TOKEN_TILE, 2 * SG_WIDTH), lambda i: (i, 0)), vec, vec, mat, mat],
        out_specs=pl.BlockSpec((TOKEN_TILE, SG_WIDTH), lambda i: (i, 0)),
        compiler_params=_params(("parallel",)))(proj, ln_g, ln_b, w_s, b_s)


def mixer_a_bwd(proj, dy, ln_g, ln_b, w_s, b_s, name):
    s = proj.shape[0]
    chunks = TOKEN_TILE // SG_CHUNK

    def body(z_ref, dy_ref, lg_ref, lb_ref, w_ref, b_ref, dz_ref, dlg_ref, dlb_ref, dw_ref, db_ref, du_scr, dvn_scr):
        first = pl.program_id(0) == 0

        @pl.when(first)
        def _():
            dw_ref[...] = jnp.zeros_like(dw_ref)
            db_ref[...] = jnp.zeros_like(db_ref)

        z = z_ref[...].astype(F32)
        ge = _gelu(z)
        u = ge[:, :SG_WIDTH]
        xhat, rstd = _ln_stats(ge[:, SG_WIDTH:])
        lg = lg_ref[...]
        vn = xhat * lg + lb_ref[...]
        dyv = dy_ref[...].astype(F32)
        tril = _tril_mask()
        for ci in range(chunks):
            rows = slice(ci * SG_CHUNK, (ci + 1) * SG_CHUNK)
            for g in range(SG_GROUPS):
                cols = slice(g * LANES, (g + 1) * LANES)
                wm = jnp.where(tril, w_ref[g], 0.0).astype(BF16)
                vg = vn[rows, cols].astype(BF16)
                mixed = jnp.dot(wm, vg, preferred_element_type=F32) + b_ref[g]
                dyb = dyv[rows, cols]
                du_scr[rows, cols] = dyb * mixed
                dmix = dyb * u[rows, cols]
                db_ref[g] += jnp.broadcast_to(jnp.sum(dmix, axis=1, keepdims=True), (SG_CHUNK, LANES))
                dmb = dmix.astype(BF16)
                dwg = lax.dot_general(dmb, vg, _DIMS["nt"], preferred_element_type=F32)
                dw_ref[g] += jnp.where(tril, dwg, 0.0)
                dvn_scr[rows, cols] = lax.dot_general(wm, dmb, _DIMS["tn"], preferred_element_type=F32)
        dvn = dvn_scr[...]
        _accumulate(dlg_ref, jnp.sum(dvn * xhat, axis=0, keepdims=True), first)
        _accumulate(dlb_ref, jnp.sum(dvn, axis=0, keepdims=True), first)
        dvv = _ln_bwd(dvn, xhat, rstd, lg)
        gg = _gelu_grad(z)
        dz_ref[:, :SG_WIDTH] = (du_scr[...] * gg[:, :SG_WIDTH]).astype(BF16)
        dz_ref[:, SG_WIDTH:] = (dvv * gg[:, SG_WIDTH:]).astype(BF16)

    vec, mat = _sg_specs()
    mat_shape = jax.ShapeDtypeStruct((SG_GROUPS, SG_CHUNK, SG_CHUNK), F32)
    vec_shape = jax.ShapeDtypeStruct((1, SG_WIDTH), F32)
    return pl.pallas_call(
        body, name=name,
        out_shape=(jax.ShapeDtypeStruct((s, 2 * SG_WIDTH), BF16), vec_shape, vec_shape, mat_shape, mat_shape),
        grid=(s // TOKEN_TILE,),
        in_specs=[pl.BlockSpec((TOKEN_TILE, 2 * SG_WIDTH), lambda i: (i, 0)),
                  pl.BlockSpec((TOKEN_TILE, SG_WIDTH), lambda i: (i, 0)), vec, vec, mat, mat],
        out_specs=(pl.BlockSpec((TOKEN_TILE, 2 * SG_WIDTH), lambda i: (i, 0)), vec, vec, mat, mat),
        scratch_shapes=[pltpu.VMEM((TOKEN_TILE, SG_WIDTH), F32), pltpu.VMEM((TOKEN_TILE, SG_WIDTH), F32)],
        compiler_params=_params(("arbitrary",)))(proj, dy, ln_g, ln_b, w_s, b_s)


_B_A_BLOCK = 1024 // LANES
_B_G_BLOCK = 1536 // LANES
_CH_TILES = CV_WIDTH // LANES


def _col_spec(s, first_block):
    return pl.BlockSpec((s, LANES), lambda j: (0, first_block + j))


def conv_b_fwd(proj, w_pad, bias, name):
    s = proj.shape[0]

    def body(a_ref, g_ref, w_ref, b_ref, c_ref, upad):
        upad[0:HALO, :] = jnp.zeros((HALO, LANES), F32)
        upad[HALO:, :] = a_ref[...].astype(F32) * _sigmoid(g_ref[...].astype(F32))
        w = w_ref[...]
        bv = b_ref[...]

        def block(bi, carry):
            start = pl.multiple_of(bi * CONV_ROWS, CONV_ROWS)
            win = upad[pl.ds(start, CONV_ROWS + HALO), :]
            acc = jnp.zeros((CONV_ROWS, LANES), F32)
            for k in range(CV_KERNEL):
                acc = acc + w[k:k + 1, :] * _shift_rows(win, HALO - (CV_KERNEL - 1) + k, CONV_ROWS)
            c_ref[pl.ds(start, CONV_ROWS), :] = acc + bv
            return carry

        lax.fori_loop(0, s // CONV_ROWS, block, 0)

    return pl.pallas_call(
        body, name=name, out_shape=jax.ShapeDtypeStruct((s, CV_WIDTH), F32), grid=(_CH_TILES,),
        in_specs=[_col_spec(s, _B_A_BLOCK), _col_spec(s, _B_G_BLOCK), _col_spec(HALO, 0), _col_spec(1, 0)],
        out_specs=_col_spec(s, 0), scratch_shapes=[pltpu.VMEM((s + HALO, LANES), F32)],
        compiler_params=_params(("parallel",)))(proj, proj, w_pad, bias)


def conv_b_bwd(proj, w_pad, dc, name):
    s = proj.shape[0]

    def body(a_ref, g_ref, w_ref, dc_ref, da_ref, dg_ref, dw_ref, db_ref, upad, dpad, dw_scr):
        upad[0:HALO, :] = jnp.zeros((HALO, LANES), F32)
        upad[HALO:, :] = a_ref[...].astype(F32) * _sigmoid(g_ref[...].astype(F32))
        dcv = dc_ref[...]
        dpad[0:s, :] = dcv
        dpad[s:, :] = jnp.zeros((HALO, LANES), F32)
        db_ref[...] = jnp.sum(dcv, axis=0, keepdims=True)
        dw_scr[...] = jnp.zeros((HALO, LANES), F32)
        w = w_ref[...]

        def block(bi, carry):
            start = pl.multiple_of(bi * CONV_ROWS, CONV_ROWS)
            uwin = upad[pl.ds(start, CONV_ROWS + HALO), :]
            dwin = dpad[pl.ds(start, CONV_ROWS + HALO), :]
            dcb = dwin[:CONV_ROWS]
            du = jnp.zeros((CONV_ROWS, LANES), F32)
            for k in range(CV_KERNEL):
                du = du + w[k:k + 1, :] * _shift_rows(dwin, CV_KERNEL - 1 - k, CONV_ROWS)
                ush = _shift_rows(uwin, HALO - (CV_KERNEL - 1) + k, CONV_ROWS)
                dw_scr[k:k + 1, :] += jnp.sum(dcb * ush, axis=0, keepdims=True)
            av = a_ref[pl.ds(start, CONV_ROWS), :].astype(F32)
            sg = _sigmoid(g_ref[pl.ds(start, CONV_ROWS), :].astype(F32))
            da_ref[pl.ds(start, CONV_ROWS), :] = (du * sg).astype(BF16)
            dg_ref[pl.ds(start, CONV_ROWS), :] = (du * av * sg * (1.0 - sg)).astype(BF16)
            return carry

        lax.fori_loop(0, s // CONV_ROWS, block, 0)
        dw_ref[...] = dw_scr[...]

    act = jax.ShapeDtypeStruct((s, CV_WIDTH), BF16)
    return pl.pallas_call(
        body, name=name,
        out_shape=(act, act, jax.ShapeDtypeStruct((HALO, CV_WIDTH), F32), jax.ShapeDtypeStruct((1, CV_WIDTH), F32)),
        grid=(_CH_TILES,),
        in_specs=[_col_spec(s, _B_A_BLOCK), _col_spec(s, _B_G_BLOCK), _col_spec(HALO, 0), _col_spec(s, 0)],
        out_specs=(_col_spec(s, 0), _col_spec(s, 0), _col_spec(HALO, 0), _col_spec(1, 0)),
        scratch_shapes=[pltpu.VMEM((s + HALO, LANES), F32), pltpu.VMEM((s + HALO, LANES), F32),
                        pltpu.VMEM((HALO, LANES), F32)],
        compiler_params=_params(("parallel",)))(proj, proj, w_pad, dc)


def ln_silu_fwd(c, ln_g, ln_b, name):
    s, d = c.shape

    def body(c_ref, g_ref, b_ref, o_ref):
        xhat, _ = _ln_stats(c_ref[...])
        cn = xhat * g_ref[...] + b_ref[...]
        o_ref[...] = (cn * _sigmoid(cn)).astype(BF16)

    tile = pl.BlockSpec((TOKEN_TILE, d), lambda i: (i, 0))
    return pl.pallas_call(
        body, name=name, out_shape=jax.ShapeDtypeStruct((s, d), BF16), grid=(s // TOKEN_TILE,),
        in_specs=[tile, _row_spec(d), _row_spec(d)], out_specs=tile,
        compiler_params=_params(("parallel",)))(c, ln_g, ln_b)


def ln_silu_bwd(c, dy, ln_g, ln_b, name):
    s, d = c.shape

    def body(c_ref, dy_ref, g_ref, b_ref, dc_ref, dg_ref, db_ref):
        first = pl.program_id(0) == 0
        xhat, rstd = _ln_stats(c_ref[...])
        gv = g_ref[...]
        cn = xhat * gv + b_ref[...]
        sg = _sigmoid(cn)
        dcn = dy_ref[...].astype(F32) * sg * (1.0 + cn * (1.0 - sg))
        _accumulate(dg_ref, jnp.sum(dcn * xhat, axis=0, keepdims=True), first)
        _accumulate(db_ref, jnp.sum(dcn, axis=0, keepdims=True), first)
        dc_ref[...] = _ln_bwd(dcn, xhat, rstd, gv)

    tile = pl.BlockSpec((TOKEN_TILE, d), lambda i: (i, 0))
    vec_shape = jax.ShapeDtypeStruct((1, d), F32)
    return pl.pallas_call(
        body, name=name, out_shape=(jax.ShapeDtypeStruct((s, d), F32), vec_shape, vec_shape),
        grid=(s // TOKEN_TILE,), in_specs=[tile, tile, _row_spec(d), _row_spec(d)],
        out_specs=(tile, _row_spec(d), _row_spec(d)), compiler_params=_params(("arbitrary",)))(c, dy, ln_g, ln_b)


_D_BLOCK = 2816 // LANES


def _conv3(win, w):
    acc = jnp.zeros((CONV_ROWS, LANES), F32)
    for k in range(SC_KERNEL):
        acc = acc + w[k:k + 1, :] * _shift_rows(win, HALO - (SC_KERNEL - 1) + k, CONV_ROWS)
    return acc


def conv_d_fwd(proj, w_pad, name):
    s = proj.shape[0]

    def body(bg_ref, cg_ref, h_ref, w_ref, o_ref, ppad):
        ppad[0:HALO, :] = jnp.zeros((HALO, LANES), F32)
        ppad[HALO:, :] = cg_ref[...].astype(F32) * h_ref[...].astype(F32)
        w = w_ref[...]

        def block(bi, carry):
            start = pl.multiple_of(bi * CONV_ROWS, CONV_ROWS)
            cv = _conv3(ppad[pl.ds(start, CONV_ROWS + HALO), :], w)
            o_ref[pl.ds(start, CONV_ROWS), :] = (bg_ref[pl.ds(start, CONV_ROWS), :].astype(F32) * cv).astype(BF16)
            return carry

        lax.fori_loop(0, s // CONV_ROWS, block, 0)

    return pl.pallas_call(
        body, name=name, out_shape=jax.ShapeDtypeStruct((s, SC_WIDTH), BF16), grid=(_CH_TILES,),
        in_specs=[_col_spec(s, _D_BLOCK), _col_spec(s, _D_BLOCK + _CH_TILES), _col_spec(s, _D_BLOCK + 2 * _CH_TILES),
                  _col_spec(SUBLANES, 0)],
        out_specs=_col_spec(s, 0), scratch_shapes=[pltpu.VMEM((s + HALO, LANES), F32)],
        compiler_params=_params(("parallel",)))(proj, proj, proj, w_pad)


def conv_d_bwd(proj, w_pad, dy, name):
    s = proj.shape[0]

    def body(bg_ref, cg_ref, h_ref, w_ref, dy_ref, dbg_ref, dcg_ref, dh_ref, dw_ref, ppad, dpad, dw_scr):
        ppad[0:HALO, :] = jnp.zeros((HALO, LANES), F32)
        ppad[HALO:, :] = cg_ref[...].astype(F32) * h_ref[...].astype(F32)
        dpad[0:s, :] = dy_ref[...].astype(F32) * bg_ref[...].astype(F32)
        dpad[s:, :] = jnp.zeros((HALO, LANES), F32)
        dw_scr[...] = jnp.zeros((SUBLANES, LANES), F32)
        w = w_ref[...]

        def block(bi, carry):
            start = pl.multiple_of(bi * CONV_ROWS, CONV_ROWS)
            rows = pl.ds(start, CONV_ROWS)
            pwin = ppad[pl.ds(start, CONV_ROWS + HALO), :]
            dwin = dpad[pl.ds(start, CONV_ROWS + HALO), :]
            dcvb = dwin[:CONV_ROWS]
            dbg_ref[rows, :] = (dy_ref[rows, :].astype(F32) * _conv3(pwin, w)).astype(BF16)
            dp = jnp.zeros((CONV_ROWS, LANES), F32)
            for k in range(SC_KERNEL):
                dp = dp + w[k:k + 1, :] * _shift_rows(dwin, SC_KERNEL - 1 - k, CONV_ROWS)
                psh = _shift_rows(pwin, HALO - (SC_KERNEL - 1) + k, CONV_ROWS)
                dw_scr[k:k + 1, :] += jnp.sum(dcvb * psh, axis=0, keepdims=True)
            dcg_ref[rows, :] = (dp * h_ref[rows, :].astype(F32)).astype(BF16)
            dh_ref[rows, :] = (dp * cg_ref[rows, :].astype(F32)).astype(BF16)
            return carry

        lax.fori_loop(0, s // CONV_ROWS, block, 0)
        dw_ref[...] = dw_scr[...]

    act = jax.ShapeDtypeStruct((s, SC_WIDTH), BF16)
    return pl.pallas_call(
        body, name=name, out_shape=(act, act, act, jax.ShapeDtypeStruct((SUBLANES, SC_WIDTH), F32)),
        grid=(_CH_TILES,),
        in_specs=[_col_spec(s, _D_BLOCK), _col_spec(s, _D_BLOCK + _CH_TILES), _col_spec(s, _D_BLOCK + 2 * _CH_TILES),
                  _col_spec(SUBLANES, 0), _col_spec(s, 0)],
        out_specs=(_col_spec(s, 0), _col_spec(s, 0), _col_spec(s, 0), _col_spec(SUBLANES, 0)),
        scratch_shapes=[pltpu.VMEM((s + HALO, LANES), F32), pltpu.VMEM((s + HALO, LANES), F32),
                        pltpu.VMEM((SUBLANES, LANES), F32)],
        compiler_params=_params(("parallel",)))(proj, proj, proj, w_pad, dy)


_QK_BLOCK = 2048 // LANES
_QK_BLOCKS = (Q_WIDTH + KV_WIDTH) // LANES


def _swap_halves(t):
    lane = lax.broadcasted_iota(jnp.int32, t.shape, 1)
    low = (lane % HEAD_DIM) < (HEAD_DIM // 2)
    return jnp.where(low, pltpu.roll(t, LANES - HEAD_DIM // 2, axis=1), pltpu.roll(t, HEAD_DIM // 2, axis=1))


def rope_fwd(proj, cos_t, sin_t, name):
    s = proj.shape[0]

    def body(t_ref, c_ref, s_ref, o_ref):
        t = t_ref[...].astype(F32)
        o_ref[...] = (t * c_ref[...] + _swap_halves(t) * s_ref[...]).astype(BF16)

    tab = pl.BlockSpec((TOKEN_TILE, LANES), lambda i, j: (i, 0))
    return pl.pallas_call(
        body, name=name, out_shape=jax.ShapeDtypeStruct((s, Q_WIDTH + KV_WIDTH), BF16),
        grid=(s // TOKEN_TILE, _QK_BLOCKS),
        in_specs=[pl.BlockSpec((TOKEN_TILE, LANES), lambda i, j: (i, _QK_BLOCK + j)), tab, tab],
        out_specs=pl.BlockSpec((TOKEN_TILE, LANES), lambda i, j: (i, j)),
        compiler_params=_params(("parallel", "parallel")))(proj, cos_t, sin_t)


def rope_bwd(d_cur, d_prev, cos_t, sin_t, name):
    s, w = d_cur.shape

    def body(a_ref, b_ref, c_ref, s_ref, o_ref):
        d = a_ref[...] + b_ref[...]
        o_ref[...] = (d * c_ref[...] + _swap_halves(d) * s_ref[...]).astype(BF16)

    tab = pl.BlockSpec((TOKEN_TILE, LANES), lambda i, j: (i, 0))
    blk = pl.BlockSpec((TOKEN_TILE, LANES), lambda i, j: (i, j))
    return pl.pallas_call(
        body, name=name, out_shape=jax.ShapeDtypeStruct((s, w), BF16), grid=(s // TOKEN_TILE, w // LANES),
        in_specs=[blk, blk, tab, tab], out_specs=blk,
        compiler_params=_params(("parallel", "parallel")))(d_cur, d_prev, cos_t, sin_t)


_GROUP = N_Q_HEADS // N_KV_HEADS
_NEG = -1e30


def _attn_specs():
    q_spec = pl.BlockSpec((_GROUP, WINDOW, HEAD_DIM), lambda h, n: (h, n, 0))
    cur = pl.BlockSpec((1, WINDOW, HEAD_DIM), lambda h, n: (h, n, 0))
    prev = pl.BlockSpec((1, WINDOW, HEAD_DIM), lambda h, n: (h, jnp.maximum(n - 1, 0), 0))
    sink = pl.BlockSpec((_GROUP, 1, LANES), lambda h, n: (h, 0, 0))
    return q_spec, cur, prev, sink


def _attn_valid(n):
    qi = lax.broadcasted_iota(jnp.int32, (WINDOW, 2 * WINDOW), 0)
    kj = lax.broadcasted_iota(jnp.int32, (WINDOW, 2 * WINDOW), 1)
    delta = qi + WINDOW - kj
    return (delta >= 0) & (delta < WINDOW) & ((kj >= WINDOW) | (n > 0))


def _attn_probs(q, kcat, valid, sink_row):
    sc = lax.dot_general(q, kcat, _DIMS["nt"], preferred_element_type=F32) * (HEAD_DIM ** -0.5)
    sc = jnp.where(valid, sc, _NEG)
    sink = jnp.max(sink_row, axis=-1, keepdims=True)
    m = jnp.maximum(jnp.max(sc, axis=-1, keepdims=True), sink)
    p = jnp.where(valid, jnp.exp(sc - m), 0.0)
    es = jnp.exp(sink - m)
    inv = 1.0 / (jnp.sum(p, axis=-1, keepdims=True) + es)
    return p * inv, es * inv


def attention_fwd(qh, kh, vh, sinks_b, name):
    s = qh.shape[1]

    def body(q_ref, kc_ref, kp_ref, vc_ref, vp_ref, sk_ref, o_ref):
        valid = _attn_valid(pl.program_id(1))
        kcat = jnp.concatenate([kp_ref[0], kc_ref[0]], axis=0)
        vcat = jnp.concatenate([vp_ref[0], vc_ref[0]], axis=0)
        for g in range(_GROUP):
            probs, _ = _attn_probs(q_ref[g], kcat, valid, sk_ref[g])
            o_ref[g] = jnp.dot(probs.astype(BF16), vcat, preferred_element_type=F32).astype(BF16)

    q_spec, cur, prev, sink = _attn_specs()
    return pl.pallas_call(
        body, name=name, out_shape=jax.ShapeDtypeStruct(qh.shape, BF16), grid=(N_KV_HEADS, s // WINDOW),
        in_specs=[q_spec, cur, prev, cur, prev, sink], out_specs=q_spec,
        compiler_params=_params(("parallel", "parallel")))(qh, kh, kh, vh, vh, sinks_b)


def attention_bwd(qh, kh, vh, sinks_b, doh, name):
    s = qh.shape[1]

    def body(q_ref, kc_ref, kp_ref, vc_ref, vp_ref, sk_ref, do_ref, dq_ref, dkc_ref, dkp_ref, dvc_ref, dvp_ref, ds_ref):
        n = pl.program_id(1)
        valid = _attn_valid(n)
        kcat = jnp.concatenate([kp_ref[0], kc_ref[0]], axis=0)
        vcat = jnp.concatenate([vp_ref[0], vc_ref[0]], axis=0)
        dk = jnp.zeros((2 * WINDOW, HEAD_DIM), F32)
        dv = jnp.zeros((2 * WINDOW, HEAD_DIM), F32)
        for g in range(_GROUP):
            q = q_ref[g]
            do = do_ref[g]
            probs, ps = _attn_probs(q, kcat, valid, sk_ref[g])
            dprobs = lax.dot_general(do, vcat, _DIMS["nt"], preferred_element_type=F32)
            dv = dv + lax.dot_general(probs.astype(BF16), do, _DIMS["tn"], preferred_element_type=F32)
            rs = jnp.sum(probs * dprobs, axis=-1, keepdims=True)
            dsb = (probs * (dprobs - rs) * (HEAD_DIM ** -0.5)).astype(BF16)
            dq_ref[g] = jnp.dot(dsb, kcat, preferred_element_type=F32)
            dk = dk + lax.dot_general(dsb, q, _DIMS["tn"], preferred_element_type=F32)
            dsink = jnp.broadcast_to(-jnp.sum(ps * rs, axis=0, keepdims=True), (1, LANES))

            @pl.when(n == 0)
            def _():
                ds_ref[g] = dsink

            @pl.when(n > 0)
            def _():
                ds_ref[g] += dsink

        dkp_ref[0] = dk[:WINDOW]
        dkc_ref[0] = dk[WINDOW:]
        dvp_ref[0] = dv[:WINDOW]
        dvc_ref[0] = dv[WINDOW:]

    q_spec, cur, prev, sink = _attn_specs()
    kv_shape = jax.ShapeDtypeStruct(kh.shape, F32)
    return pl.pallas_call(
        body, name=name,
        out_shape=(jax.ShapeDtypeStruct(qh.shape, F32), kv_shape, kv_shape, kv_shape, kv_shape,
                   jax.ShapeDtypeStruct(sinks_b.shape, F32)),
        grid=(N_KV_HEADS, s // WINDOW), in_specs=[q_spec, cur, prev, cur, prev, sink, q_spec],
        out_specs=(q_spec, cur, cur, cur, cur, sink),
        compiler_params=_params(("parallel", "arbitrary")))(qh, kh, kh, vh, vh, sinks_b, doh)


def _to_heads(t, heads):
    return t.reshape(t.shape[0], heads, HEAD_DIM).transpose(1, 0, 2)


def _from_heads(t):
    return t.transpose(1, 0, 2).reshape(t.shape[1], t.shape[0] * HEAD_DIM)


def _shift_window(t):
    return jnp.concatenate([t[:, WINDOW:], jnp.zeros_like(t[:, :WINDOW])], axis=1)


def merge_fwd(zg, branches, name):
    s = zg.shape[0]

    def body(zg_ref, b0, b1, b2, b3, o_ref):
        acc = jnp.zeros((TOKEN_TILE, D_MODEL), F32)
        for n, b_ref in enumerate((b0, b1, b2, b3)):
            gate = _sigmoid(zg_ref[:, n * D_MODEL:(n + 1) * D_MODEL].astype(F32))
            acc = acc + gate * b_ref[...].astype(F32)
        o_ref[...] = acc.astype(BF16)

    tile = pl.BlockSpec((TOKEN_TILE, D_MODEL), lambda i: (i, 0))
    wide = pl.BlockSpec((TOKEN_TILE, N_BRANCH * D_MODEL), lambda i: (i, 0))
    return pl.pallas_call(
        body, name=name, out_shape=jax.ShapeDtypeStruct((s, D_MODEL), BF16), grid=(s // TOKEN_TILE,),
        in_specs=[wide, tile, tile, tile, tile], out_specs=tile,
        compiler_params=_params(("parallel",)))(zg, *branches)


def merge_bwd(zg, branches, dm, name):
    s = zg.shape[0]

    def body(zg_ref, b0, b1, b2, b3, dm_ref, dzg_ref, d0, d1, d2, d3):
        dmv = dm_ref[...].astype(F32)
        for n, (b_ref, d_ref) in enumerate(((b0, d0), (b1, d1), (b2, d2), (b3, d3))):
            cols = slice(n * D_MODEL, (n + 1) * D_MODEL)
            gate = _sigmoid(zg_ref[:, cols].astype(F32))
            d_ref[...] = (gate * dmv).astype(BF16)
            dzg_ref[:, cols] = (dmv * b_ref[...].astype(F32) * gate * (1.0 - gate)).astype(BF16)

    tile = pl.BlockSpec((TOKEN_TILE, D_MODEL), lambda i: (i, 0))
    wide = pl.BlockSpec((TOKEN_TILE, N_BRANCH * D_MODEL), lambda i: (i, 0))
    act = jax.ShapeDtypeStruct((s, D_MODEL), BF16)
    return pl.pallas_call(
        body, name=name, out_shape=(jax.ShapeDtypeStruct((s, N_BRANCH * D_MODEL), BF16), act, act, act, act),
        grid=(s // TOKEN_TILE,), in_specs=[wide, tile, tile, tile, tile, tile],
        out_specs=(wide, tile, tile, tile, tile), compiler_params=_params(("parallel",)))(zg, *branches, dm)


def swiglu_fwd(gu, name):
    s = gu.shape[0]

    def body(g_ref, u_ref, o_ref):
        gate = g_ref[...].astype(F32)
        o_ref[...] = (gate * _sigmoid(gate) * u_ref[...].astype(F32)).astype(BF16)

    return pl.pallas_call(
        body, name=name, out_shape=jax.ShapeDtypeStruct((s, D_FF), BF16), grid=(s // TOKEN_TILE,),
        in_specs=[pl.BlockSpec((TOKEN_TILE, D_FF), lambda i: (i, 0)), pl.BlockSpec((TOKEN_TILE, D_FF), lambda i: (i, 1))],
        out_specs=pl.BlockSpec((TOKEN_TILE, D_FF), lambda i: (i, 0)), compiler_params=_params(("parallel",)))(gu, gu)


def swiglu_bwd(gu, dact, name):
    s = gu.shape[0]

    def body(g_ref, u_ref, da_ref, o_ref):
        gate = g_ref[...].astype(F32)
        sg = _sigmoid(gate)
        da = da_ref[...].astype(F32)
        o_ref[:, :D_FF] = (da * u_ref[...].astype(F32) * sg * (1.0 + gate * (1.0 - sg))).astype(BF16)
        o_ref[:, D_FF:] = (da * gate * sg).astype(BF16)

    half = pl.BlockSpec((TOKEN_TILE, D_FF), lambda i: (i, 0))
    return pl.pallas_call(
        body, name=name, out_shape=jax.ShapeDtypeStruct((s, 2 * D_FF), BF16), grid=(s // TOKEN_TILE,),
        in_specs=[half, pl.BlockSpec((TOKEN_TILE, D_FF), lambda i: (i, 1)), half],
        out_specs=pl.BlockSpec((TOKEN_TILE, 2 * D_FF), lambda i: (i, 0)),
        compiler_params=_params(("parallel",)))(gu, gu, dact)


ADAMW_BLOCK_BYTES = 1 << 20


def adamw(parts, w, m, v, name):
    r, c = w.shape
    n_parts = len(parts)
    rows = r // n_parts
    tr = _divisor_tile(rows, max(SUBLANES, ADAMW_BLOCK_BYTES // (4 * c)), SUBLANES)
    tiles = rows // tr

    def part_spec(j):
        return pl.BlockSpec((N_DEV, tr, c), lambda i: (0, jnp.clip(i - j * tiles, 0, tiles - 1), 0))

    def body(*refs):
        p_refs = refs[:n_parts]
        w_ref, m_ref, v_ref, g_ref, d_ref, nm_ref, nv_ref = refs[n_parts:]
        which = pl.program_id(0) // tiles
        g = None
        for j, p_ref in enumerate(p_refs):
            gj = p_ref[0].astype(F32)
            for i in range(1, N_DEV):
                gj = gj + p_ref[i].astype(F32)
            g = gj if g is None else jnp.where(which == j, gj, g)
        nm = ADAM_B1 * m_ref[...] + (1.0 - ADAM_B1) * g
        nv = ADAM_B2 * v_ref[...] + (1.0 - ADAM_B2) * (g * g)
        m_hat = nm / (1.0 - ADAM_B1 ** ADAM_STEP)
        v_hat = nv / (1.0 - ADAM_B2 ** ADAM_STEP)
        g_ref[...] = g
        d_ref[...] = -ADAM_LR * (m_hat / (jnp.sqrt(v_hat) + ADAM_EPS) + ADAM_WD * w_ref[...])
        nm_ref[...] = nm
        nv_ref[...] = nv

    tile = pl.BlockSpec((tr, c), lambda i: (i, 0))
    shape = jax.ShapeDtypeStruct((r, c), F32)
    return pl.pallas_call(
        body, name=name, out_shape=(shape, shape, shape, shape), grid=(r // tr,),
        in_specs=[part_spec(j) for j in range(n_parts)] + [tile, tile, tile],
        out_specs=(tile, tile, tile, tile), compiler_params=_params(("parallel",)))(*parts, w, m, v)


_RELATIONS = [(a, b, e) for a in (0, 1) for b in (0, 1) for e in (0, 1)][1:]


def exchange(arrays, scatter, name):
    n = len(arrays)
    n_rel = len(_RELATIONS)

    def body(*refs):
        ins, outs = refs[:n], refs[n:2 * n]
        send_sems, recv_sems, local_sems = refs[2 * n:]
        x, y, c = lax.axis_index("x"), lax.axis_index("y"), lax.axis_index("c")
        me = 4 * x + 2 * y + c
        copies = []
        for t in range(n):
            own = ins[t].at[me] if scatter[t] else ins[t]
            local = pltpu.make_async_copy(own, outs[t].at[me], local_sems.at[t])
            local.start()
            copies.append(local)
            for k, (a, b, e) in enumerate(_RELATIONS):
                px, py, pc = (x + a) % 2, (y + b) % 2, (c + e) % 2
                src = ins[t].at[4 * px + 2 * py + pc] if scatter[t] else ins[t]
                remote = pltpu.make_async_remote_copy(
                    src_ref=src, dst_ref=outs[t].at[me], send_sem=send_sems.at[t, k], recv_sem=recv_sems.at[t, k],
                    device_id=(px, py, pc), device_id_type=pl.DeviceIdType.MESH)
                remote.start()
                copies.append(remote)
        for cp in copies:
            cp.wait()

    out_shape = tuple(
        jax.ShapeDtypeStruct(a.shape if scatter[t] else (N_DEV,) + a.shape, a.dtype) for t, a in enumerate(arrays))
    any_spec = pl.BlockSpec(memory_space=pl.ANY)
    return pl.pallas_call(
        body, name=name, out_shape=out_shape, in_specs=[any_spec] * n, out_specs=tuple([any_spec] * n),
        scratch_shapes=[pltpu.SemaphoreType.DMA((n, n_rel)), pltpu.SemaphoreType.DMA((n, n_rel)),
                        pltpu.SemaphoreType.DMA((n,))],
        compiler_params=pltpu.CompilerParams(has_side_effects=True))(*arrays)


_HBM_SPEC = pl.BlockSpec(memory_space=pltpu.HBM)
_SEM_SPEC = pl.BlockSpec(memory_space=pltpu.SEMAPHORE)
_ANY_SPEC = pl.BlockSpec(memory_space=pl.ANY)
_DATAFLOW = pltpu.SideEffectType.DATAFLOW_SIDE_EFFECTING


def _remote_copies(ins, lands, send_sems, recv_sems, scatter):
    x, y, c = lax.axis_index("x"), lax.axis_index("y"), lax.axis_index("c")
    me = 4 * x + 2 * y + c
    copies = []
    for t in range(len(ins)):
        for k, (a, b, e) in enumerate(_RELATIONS):
            px, py, pc = (x + a) % 2, (y + b) % 2, (c + e) % 2
            src = ins[t].at[4 * px + 2 * py + pc] if scatter[t] else ins[t]
            copies.append(pltpu.make_async_remote_copy(
                src_ref=src, dst_ref=lands[t].at[me], send_sem=send_sems.at[t * len(_RELATIONS) + k],
                recv_sem=recv_sems.at[t * len(_RELATIONS) + k],
                device_id=(px, py, pc), device_id_type=pl.DeviceIdType.MESH))
    return copies


def exchange_start(arrays, scatter, name, after=None):
    n = len(arrays)
    n_rel = len(_RELATIONS)
    land_shapes = [a.shape if scatter[t] else (N_DEV,) + a.shape for t, a in enumerate(arrays)]

    def body(*refs):
        ins, lands = refs[:n], refs[n:2 * n]
        send_sems, recv_sems = refs[-2 * n - 3], refs[-2 * n - 2]
        token = refs[-1]
        for cp in _remote_copies(ins, lands, send_sems, recv_sems, scatter):
            cp.start()
        token[...] = jnp.zeros_like(token)

    sems = pltpu.SemaphoreType.DMA((n * n_rel,))
    out_shape = ((sems, sems) + tuple(pltpu.HBM(a.shape, a.dtype) for a in arrays)
                 + tuple(pltpu.HBM(s, a.dtype) for s, a in zip(land_shapes, arrays))
                 + (jax.ShapeDtypeStruct((SUBLANES, LANES), F32),))
    operands = [pltpu.with_memory_space_constraint(a, pltpu.HBM) for a in arrays]
    operands += [pltpu.with_memory_space_constraint(lax.empty(s, a.dtype), pltpu.HBM) for s, a in zip(land_shapes, arrays)]
    in_specs = [_HBM_SPEC] * (2 * n)
    if after is not None:
        operands.append(after)
        in_specs.append(_ANY_SPEC)
    res = pl.pallas_call(
        body, name=name, out_shape=out_shape, in_specs=in_specs,
        out_specs=(_SEM_SPEC, _SEM_SPEC) + (_HBM_SPEC,) * (2 * n) + (pl.BlockSpec(memory_space=pltpu.VMEM),),
        input_output_aliases={i: 2 + i for i in range(2 * n)},
        compiler_params=pltpu.CompilerParams(has_side_effects=_DATAFLOW))(*operands)
    handle = (res[0], res[1], res[2:2 + n], res[2 + n:2 + 2 * n], tuple(scatter))
    return handle, res[-1]


def exchange_wait(handle, after, name):
    send_sems, recv_sems, sources, lands, scatter = handle
    n = len(sources)

    def body(*refs):
        ins, lzs = refs[:n], refs[n:2 * n]
        send_ref, recv_ref = refs[2 * n], refs[2 * n + 1]
        for cp in _remote_copies(ins, lzs, send_ref, recv_ref, scatter):
            cp.wait_send()
            cp.wait_recv()

    out_shape = (tuple(pltpu.HBM(a.shape, a.dtype) for a in sources) + tuple(pltpu.HBM(a.shape, a.dtype) for a in lands))
    res = pl.pallas_call(
        body, name=name, out_shape=out_shape, in_specs=[_HBM_SPEC] * (2 * n) + [_SEM_SPEC, _SEM_SPEC, _ANY_SPEC],
        out_specs=(_HBM_SPEC,) * (2 * n), input_output_aliases={i: i for i in range(2 * n)},
        compiler_params=pltpu.CompilerParams(has_side_effects=_DATAFLOW))(*sources, *lands, send_sems, recv_sems, after)
    return res[n:]


def _place_own(landed, own, me):
    return lax.dynamic_update_index_in_dim(landed, own, me, 0)


_SMALL = ("norm_mix", "sg_ln_g", "sg_ln_b", "sg_w", "sg_b", "cv_b", "cv_ln_g", "cv_ln_b", "attn_sinks", "norm_ffn",
          "norm_final")
_PACK_UNIT = SUBLANES * LANES


def _pack(tensors):
    rows = []
    for t in tensors:
        flat = t.reshape(-1)
        pad = (-flat.shape[0]) % _PACK_UNIT
        rows.append(jnp.pad(flat, (0, pad)).reshape(-1, LANES))
    return jnp.concatenate(rows, axis=0)


def _unpack(packed, like):
    out, row = [], 0
    for t in like:
        size = 1
        for d in t.shape:
            size *= d
        rows = -(-size // _PACK_UNIT) * SUBLANES
        out.append(packed[row:row + rows].reshape(-1)[:size].reshape(t.shape))
        row += rows
    return out


def _layer_fwd(l, x, p):
    tag = f"l{l}_"
    xn = rmsnorm_fwd(x, p["norm_mix"], tag + "norm_mix")
    proj = matmul(xn, p["w_in_a"], "nn", BF16, tag + "proj_a")
    zg = matmul(xn, p["w_in_g"], "nn", BF16, tag + "proj_g")
    y_a = mixer_a_fwd(proj, p["sg_ln_g"], p["sg_ln_b"], p["sg_w"], p["sg_b"], tag + "mix_a")
    conv = conv_b_fwd(proj, p["cv_w"], p["cv_b"], tag + "conv_b")
    y_b = ln_silu_fwd(conv, p["cv_ln_g"], p["cv_ln_b"], tag + "ln_silu")
    qk = rope_fwd(proj, p["cos"], p["sin"], tag + "rope")
    qh = _to_heads(qk[:, :Q_WIDTH], N_Q_HEADS)
    kh = _to_heads(qk[:, Q_WIDTH:], N_KV_HEADS)
    vh = _to_heads(proj[:, 2688:2816], N_KV_HEADS)
    oh = attention_fwd(qh, kh, vh, p["sinks"], tag + "attn")
    y_c = _from_heads(oh)
    y_d = conv_d_fwd(proj, p["sc_w"], tag + "conv_d")
    ys = (y_a, y_b, y_c, y_d)
    branches = tuple(matmul(ys[n], p["w_branch"][n], "nn", BF16, tag + f"branch{n}") for n in range(N_BRANCH))
    merged = merge_fwd(zg, branches, tag + "merge")
    x_mid = matmul(merged, p["w_out"], "nn", F32, tag + "out", add=x)
    hn = rmsnorm_fwd(x_mid, p["norm_ffn"], tag + "norm_ffn")
    gu = matmul(hn, p["w_gate_up"], "nn", BF16, tag + "gate_up", tn_cap=512)
    act = swiglu_fwd(gu, tag + "swiglu")
    x_out = matmul(act, p["w_down"], "nn", F32, tag + "down", add=x_mid)
    saved = dict(x=x, xn=xn, proj=proj, zg=zg, conv=conv, qh=qh, kh=kh, vh=vh, ys=ys, branches=branches,
                 merged=merged, x_mid=x_mid, hn=hn, gu=gu, act=act)
    return x_out, saved


def _layer_bwd(l, dx_out, p, sv, after=None):
    tag = f"l{l}_b_"
    g = {}
    dact = matmul(dx_out, p["w_down"], "nt", BF16, tag + "dact", after=after)
    g["w_down"] = matmul(sv["act"], dx_out, "tn", BF16, tag + "dw_down")
    dgu = swiglu_bwd(sv["gu"], dact, tag + "swiglu")
    dhn = matmul(dgu, p["w_gate_up"], "nt", BF16, tag + "dhn", tm_cap=256, tn_cap=256)
    g["w_gate_up"] = matmul(sv["hn"], dgu, "tn", BF16, tag + "dw_gate_up")
    dx_mid, g["norm_ffn"] = rmsnorm_bwd(sv["x_mid"], p["norm_ffn"], dhn, dx_out, tag + "norm_ffn")
    dmerged = matmul(dx_mid, p["w_out"], "nt", BF16, tag + "dmerged")
    g["w_out"] = matmul(sv["merged"], dx_mid, "tn", BF16, tag + "dw_out")
    dzg, *dbranches = merge_bwd(sv["zg"], sv["branches"], dmerged, tag + "merge")
    dys = [matmul(dbranches[n], p["w_branch"][n], "nt", BF16, tag + f"dy{n}") for n in range(N_BRANCH)]
    g["w_branch"] = jnp.stack(
        [matmul(sv["ys"][n], dbranches[n], "tn", BF16, tag + f"dw_branch{n}") for n in range(N_BRANCH)])
    proj = sv["proj"]
    dz_a, g["sg_ln_g"], g["sg_ln_b"], g["sg_w"], dsb = mixer_a_bwd(
        proj, dys[0], p["sg_ln_g"], p["sg_ln_b"], p["sg_w"], p["sg_b"], tag + "mix_a")
    g["sg_b"] = dsb[:, :, 0]
    dconv, g["cv_ln_g"], g["cv_ln_b"] = ln_silu_bwd(sv["conv"], dys[1], p["cv_ln_g"], p["cv_ln_b"], tag + "ln_silu")
    da, dgate, dcw, g["cv_b"] = conv_b_bwd(proj, p["cv_w"], dconv, tag + "conv_b")
    g["cv_w"] = dcw[:CV_KERNEL]
    doh = _to_heads(dys[2], N_Q_HEADS)
    dqh, dkc, dkp, dvc, dvp, dsk = attention_bwd(sv["qh"], sv["kh"], sv["vh"], p["sinks"], doh, tag + "attn")
    g["attn_sinks"] = dsk[:, 0, 0]
    dqk_cur = jnp.concatenate([_from_heads(dqh), _from_heads(dkc)], axis=1)
    dqk_prev = jnp.concatenate([jnp.zeros((SEQ, Q_WIDTH), F32), _from_heads(_shift_window(dkp))], axis=1)
    dqk = rope_bwd(dqk_cur, dqk_prev, p["cos"], -p["sin"], tag + "rope")
    dv = (_from_heads(dvc) + _from_heads(_shift_window(dvp))).astype(BF16)
    dbg, dcg, dh, dsw = conv_d_bwd(proj, p["sc_w"], dys[3], tag + "conv_d")
    g["sc_w"] = dsw[:SC_KERNEL]
    dproj = jnp.concatenate([dz_a, da, dgate, dqk, dv, dbg, dcg, dh], axis=1)
    dxn = matmul(dproj, p["w_in_a"], "nt", F32, tag + "dxn_a", tm_cap=256, tn_cap=256)
    dxn = matmul(dzg, p["w_in_g"], "nt", F32, tag + "dxn_g", add=dxn, tm_cap=256, tn_cap=256)
    g["w_in"] = jnp.concatenate(
        [matmul(sv["xn"], dproj, "tn", BF16, tag + "dw_in_a"), matmul(sv["xn"], dzg, "tn", BF16, tag + "dw_in_g")],
        axis=1)
    dx_in, g["norm_mix"] = rmsnorm_bwd(sv["x"], p["norm_mix"], dxn, dx_mid, tag + "norm_mix")
    return dx_in, g


_BIG = ("w_in", "cv_w", "sc_w", "w_branch", "w_out", "w_gate_up", "w_down")


def _full_weight(name, t):
    if name in ("w_out", "w_down"):
        return t.reshape(-1, t.shape[-1])
    if name == "w_branch":
        return t.transpose(1, 2, 0, 3).reshape(N_BRANCH, SG_WIDTH, D_MODEL)
    return t.transpose(1, 0, 2).reshape(t.shape[1], -1)


def _to_blocks(name, full):
    if name in ("w_out", "w_down"):
        return full.reshape(N_DEV, -1, full.shape[-1])
    if name == "w_branch":
        return full.reshape(N_BRANCH, SG_WIDTH, N_DEV, -1).transpose(2, 0, 1, 3)
    return full.reshape(full.shape[0], N_DEV, -1).transpose(1, 0, 2)


def _rope_tables():
    pos = jnp.arange(SEQ, dtype=F32)
    inv_freq = 1.0 / (ROPE_THETA ** (jnp.arange(0, HEAD_DIM, 2, dtype=F32) / HEAD_DIM))
    ang = pos[:, None] * inv_freq[None, :]
    cos, sin = jnp.cos(ang), jnp.sin(ang)
    reps = LANES // HEAD_DIM
    return jnp.tile(jnp.concatenate([cos, cos], axis=1), (1, reps)), jnp.tile(jnp.concatenate([-sin, sin], axis=1), (1, reps))


def kernel(x, norm_mix, w_in, sg_ln_g, sg_ln_b, sg_w, sg_b, cv_w, cv_b, cv_ln_g, cv_ln_b, attn_sinks, sc_w, w_branch, w_out, norm_ffn, w_gate_up, w_down, norm_final, loss_target, m_norm_mix, m_w_in, m_sg_ln_g, m_sg_ln_b, m_sg_w, m_sg_b, m_cv_w, m_cv_b, m_cv_ln_g, m_cv_ln_b, m_attn_sinks, m_sc_w, m_w_branch, m_w_out, m_norm_ffn, m_w_gate_up, m_w_down, m_norm_final, v_norm_mix, v_w_in, v_sg_ln_g, v_sg_ln_b, v_sg_w, v_sg_b, v_cv_w, v_cv_b, v_cv_ln_g, v_cv_ln_b, v_attn_sinks, v_sc_w, v_w_branch, v_w_out, v_norm_ffn, v_w_gate_up, v_w_down, v_norm_final):
    names = ("norm_mix", "w_in", "sg_ln_g", "sg_ln_b", "sg_w", "sg_b", "cv_w", "cv_b", "cv_ln_g", "cv_ln_b",
             "attn_sinks", "sc_w", "w_branch", "w_out", "norm_ffn", "w_gate_up", "w_down", "norm_final")
    w = dict(zip(names, (norm_mix, w_in, sg_ln_g, sg_ln_b, sg_w, sg_b, cv_w, cv_b, cv_ln_g, cv_ln_b, attn_sinks,
                         sc_w, w_branch, w_out, norm_ffn, w_gate_up, w_down, norm_final)))
    m = dict(zip(names, (m_norm_mix, m_w_in, m_sg_ln_g, m_sg_ln_b, m_sg_w, m_sg_b, m_cv_w, m_cv_b, m_cv_ln_g,
                         m_cv_ln_b, m_attn_sinks, m_sc_w, m_w_branch, m_w_out, m_norm_ffn, m_w_gate_up, m_w_down,
                         m_norm_final)))
    v = dict(zip(names, (v_norm_mix, v_w_in, v_sg_ln_g, v_sg_ln_b, v_sg_w, v_sg_b, v_cv_w, v_cv_b, v_cv_ln_g,
                         v_cv_ln_b, v_attn_sinks, v_sc_w, v_w_branch, v_w_out, v_norm_ffn, v_w_gate_up, v_w_down,
                         v_norm_final)))

    me = 4 * lax.axis_index("x") + 2 * lax.axis_index("y") + lax.axis_index("c")
    gather = [False] * len(_BIG)

    shards = [[w[n][l].astype(BF16) for n in _BIG] for l in range(DEPTH)]
    handles, token = [], None
    for l in range(DEPTH):
        handle, token = exchange_start(shards[l], gather, f"gather_start{l}", after=token)
        handles.append(handle)

    def landed_weights(l, after):
        landed = exchange_wait(handles[l], after, f"gather_wait{l}")
        return {n: _place_own(t, own, me) for n, t, own in zip(_BIG, landed, shards[l])}

    cos_t, sin_t = _rope_tables()

    def layer_params(l, gathered):
        w_in_full = _full_weight("w_in", gathered["w_in"])
        return dict(
            norm_mix=w["norm_mix"][l][None], norm_ffn=w["norm_ffn"][l][None],
            w_in_a=w_in_full[:, :PROJ_A], w_in_g=w_in_full[:, PROJ_A:],
            sg_ln_g=w["sg_ln_g"][l][None], sg_ln_b=w["sg_ln_b"][l][None], sg_w=w["sg_w"][l],
            sg_b=jnp.broadcast_to(w["sg_b"][l][:, :, None], (SG_GROUPS, SG_CHUNK, LANES)),
            cv_w=jnp.pad(_full_weight("cv_w", gathered["cv_w"]).astype(F32), ((0, HALO - CV_KERNEL), (0, 0))),
            cv_b=w["cv_b"][l][None], cv_ln_g=w["cv_ln_g"][l][None], cv_ln_b=w["cv_ln_b"][l][None],
            sinks=jnp.broadcast_to(w["attn_sinks"][l][:, None, None], (N_Q_HEADS, 1, LANES)),
            sc_w=jnp.pad(_full_weight("sc_w", gathered["sc_w"]).astype(F32), ((0, SUBLANES - SC_KERNEL), (0, 0))),
            w_branch=_full_weight("w_branch", gathered["w_branch"]),
            w_out=_full_weight("w_out", gathered["w_out"]),
            w_gate_up=_full_weight("w_gate_up", gathered["w_gate_up"]),
            w_down=_full_weight("w_down", gathered["w_down"]),
            cos=cos_t, sin=sin_t)

    params, saved = [None] * DEPTH, [None] * DEPTH
    h = x[0]
    after = token
    for l in range(DEPTH):
        params[l] = layer_params(l, landed_weights(l, after))
        h, saved[l] = _layer_fwd(l, h, params[l])
        after = h
    loss_row, dh, d_norm_final = loss_head(h, w["norm_final"][None], loss_target[0], "loss_head")

    scatter = [True] * len(_BIG)
    grads = [None] * DEPTH
    dh, grads[1] = _layer_bwd(1, dh, params[1], saved[1])
    send1 = [_to_blocks(n, grads[1][n].astype(BF16)) for n in _BIG]
    handle1, token1 = exchange_start(send1, scatter, "grads_start1")
    dh, grads[0] = _layer_bwd(0, dh, params[0], saved[0], after=token1)
    grad_x = dh[None]

    stacked = {n: jnp.stack([grads[l][n] for l in range(DEPTH)]) for n in _SMALL if n != "norm_final"}
    for n in ("norm_mix", "norm_ffn", "sg_ln_g", "sg_ln_b", "cv_b", "cv_ln_g", "cv_ln_b"):
        stacked[n] = stacked[n][:, 0]
    stacked["norm_final"] = d_norm_final[0]
    send0 = [_to_blocks(n, grads[0][n].astype(BF16)) for n in _BIG]
    small_like = [w[n] for n in _SMALL]
    small_part = _pack([stacked[n] for n in _SMALL])
    received0 = exchange(send0 + [small_part], scatter + [False], "exchange_grads0")
    landed1 = exchange_wait(handle1, received0[-1], "grads_wait1")
    received1 = [_place_own(t, lax.dynamic_index_in_dim(s, me, 0, keepdims=False), me) for t, s in zip(landed1, send1)]

    out_g, out_d, out_m, out_v = {}, {}, {}, {}
    for i, n in enumerate(_BIG):
        shape = w[n].shape
        cols = shape[-1]
        flat = (w[n].size // cols, cols)
        layer_flat = (N_DEV, flat[0] // DEPTH, cols)
        if layer_flat[1] % SUBLANES == 0:
            parts = [received0[i].reshape(layer_flat), received1[i].reshape(layer_flat)]
        else:
            parts = [jnp.stack([received0[i], received1[i]], axis=1).reshape((N_DEV,) + flat)]
        res = adamw(parts, w[n].reshape(flat), m[n].reshape(flat), v[n].reshape(flat), "adamw_" + n)
        out_g[n], out_d[n], out_m[n], out_v[n] = (t.reshape(shape) for t in res)
    res = adamw([received0[-1]], _pack(small_like), _pack([m[n] for n in _SMALL]), _pack([v[n] for n in _SMALL]),
                "adamw_small")
    for store, packed in zip((out_g, out_d, out_m, out_v), res):
        for n, t in zip(_SMALL, _unpack(packed, small_like)):
            store[n] = t

    loss = lax.psum(loss_row[0, 0], ("x", "y", "c"))
    return (loss, grad_x, *[out_g[n] for n in names], *[out_d[n] for n in names], *[out_m[n] for n in names],
            *[out_v[n] for n in names])
```

```python
import jax
import jax.numpy as jnp
from jax import lax
from jax.experimental import pallas as pl
from jax.experimental.pallas import tpu as pltpu

F32 = jnp.float32
BF16 = jnp.bfloat16

SEQ = 2048
D_MODEL = 1024
DEPTH = 2
SG_WIDTH = 512
SG_CHUNK = 128
SG_GROUPS = 4
CV_WIDTH = 512
CV_KERNEL = 31
HEAD_DIM = 64
N_Q_HEADS = 8
N_KV_HEADS = 2
Q_WIDTH = 512
KV_WIDTH = 128
WINDOW = 128
SC_WIDTH = 512
SC_KERNEL = 3
N_BRANCH = 4
D_FF = 2816
EPS = 1e-6
ROPE_THETA = 10000.0
PROJ_A = 4352
PROJ_WIDTH = 8448
N_DEV = 8

ADAM_LR = 0.001
ADAM_B1 = 0.9
ADAM_B2 = 0.999
ADAM_EPS = 1e-08
ADAM_WD = 0.01
ADAM_STEP = 10

LANES = 128
SUBLANES = 8
VMEM_LIMIT_BYTES = 48 * 1024 * 1024
HALO = 32
CONV_ROWS = 256
TOKEN_TILE = 256

_SQRT_HALF = 0.7071067811865476
_INV_SQRT_2PI = 0.3989422804014327


def _params(semantics=None):
    return pltpu.CompilerParams(dimension_semantics=semantics, vmem_limit_bytes=VMEM_LIMIT_BYTES)


def _divisor_tile(n, cap, unit):
    best = None
    for t in range(unit, min(n, cap) + 1, unit):
        if n % t == 0:
            best = t
    return best if best is not None else n


_DIMS = {"nn": (((1,), (0,)), ((), ())), "nt": (((1,), (1,)), ((), ())), "tn": (((0,), (0,)), ((), ()))}


def matmul(a, b, mode, out_dtype, name, add=None, tm_cap=512, tn_cap=512, after=None):
    if mode == "nn":
        (m, k), (_, n) = a.shape, b.shape
    elif mode == "nt":
        (m, k), (n, _) = a.shape, b.shape
    else:
        (k, m), (_, n) = a.shape, b.shape
    tm = _divisor_tile(m, tm_cap, LANES)
    tn = _divisor_tile(n, tn_cap, LANES)
    a_spec = pl.BlockSpec((k, tm), lambda i, j: (0, i)) if mode == "tn" else pl.BlockSpec((tm, k), lambda i, j: (i, 0))
    b_spec = pl.BlockSpec((tn, k), lambda i, j: (j, 0)) if mode == "nt" else pl.BlockSpec((k, tn), lambda i, j: (0, j))
    o_spec = pl.BlockSpec((tm, tn), lambda i, j: (i, j))
    dims = _DIMS[mode]

    def body(*refs):
        a_ref, b_ref = refs[0], refs[1]
        o_ref = refs[-1]
        acc = lax.dot_general(a_ref[...].astype(BF16), b_ref[...].astype(BF16), dims, preferred_element_type=F32)
        if add is not None:
            acc = acc + refs[2][...].astype(F32)
        o_ref[...] = acc.astype(out_dtype)

    operands = (a, b) + (() if add is None else (add,)) + (() if after is None else (after,))
    in_specs = [a_spec, b_spec] + ([o_spec] if add is not None else [])
    in_specs += [pl.BlockSpec(memory_space=pl.ANY)] if after is not None else []
    return pl.pallas_call(
        body, name=name, out_shape=jax.ShapeDtypeStruct((m, n), out_dtype), grid=(m // tm, n // tn),
        in_specs=in_specs, out_specs=o_spec, compiler_params=_params(("parallel", "parallel")))(*operands)


def _sigmoid(x):
    return 1.0 / (1.0 + jnp.exp(-x))


def _gelu(x):
    return 0.5 * x * (1.0 + lax.erf(x * _SQRT_HALF))


def _gelu_grad(x):
    return 0.5 * (1.0 + lax.erf(x * _SQRT_HALF)) + x * _INV_SQRT_2PI * jnp.exp(-0.5 * x * x)


def _rms_stats(x):
    r = lax.rsqrt(jnp.mean(x * x, axis=-1, keepdims=True) + EPS)
    return x * r, r


def _rms_bwd(dxn, xhat, r, g):
    h = dxn * g
    return r * (h - xhat * jnp.mean(h * xhat, axis=-1, keepdims=True))


def _ln_stats(x):
    mu = jnp.mean(x, axis=-1, keepdims=True)
    xc = x - mu
    rstd = lax.rsqrt(jnp.mean(xc * xc, axis=-1, keepdims=True) + EPS)
    return xc * rstd, rstd


def _ln_bwd(dy, xhat, rstd, g):
    dxhat = dy * g
    return rstd * (dxhat - jnp.mean(dxhat, axis=-1, keepdims=True)
                   - xhat * jnp.mean(dxhat * xhat, axis=-1, keepdims=True))


def _accumulate(ref, value, first):
    @pl.when(first)
    def _():
        ref[...] = value

    @pl.when(jnp.logical_not(first))
    def _():
        ref[...] += value


def _shift_rows(win, shift, n_out):
    n = win.shape[0]
    if shift % n == 0:
        return win[:n_out]
    return pltpu.roll(win, n - shift, axis=0)[:n_out]


def _row_spec(width):
    return pl.BlockSpec((1, width), lambda i: (0, 0))


def rmsnorm_fwd(x, g, name):
    s, d = x.shape

    def body(x_ref, g_ref, o_ref):
        xhat, _ = _rms_stats(x_ref[...])
        o_ref[...] = (xhat * g_ref[...]).astype(BF16)

    tile = pl.BlockSpec((TOKEN_TILE, d), lambda i: (i, 0))
    return pl.pallas_call(
        body, name=name, out_shape=jax.ShapeDtypeStruct((s, d), BF16), grid=(s // TOKEN_TILE,),
        in_specs=[tile, _row_spec(d)], out_specs=tile, compiler_params=_params(("parallel",)))(x, g)


def rmsnorm_bwd(x, g, dxn, dres, name):
    s, d = x.shape

    def body(x_ref, g_ref, dxn_ref, dres_ref, dx_ref, dg_ref):
        xhat, r = _rms_stats(x_ref[...])
        dxn_v = dxn_ref[...].astype(F32)
        dx_ref[...] = dres_ref[...] + _rms_bwd(dxn_v, xhat, r, g_ref[...])
        _accumulate(dg_ref, jnp.sum(dxn_v * xhat, axis=0, keepdims=True), pl.program_id(0) == 0)

    tile = pl.BlockSpec((TOKEN_TILE, d), lambda i: (i, 0))
    return pl.pallas_call(
        body, name=name, out_shape=(jax.ShapeDtypeStruct((s, d), F32), jax.ShapeDtypeStruct((1, d), F32)),
        grid=(s // TOKEN_TILE,), in_specs=[tile, _row_spec(d), tile, tile], out_specs=(tile, _row_spec(d)),
        compiler_params=_params(("arbitrary",)))(x, g, dxn, dres)


def loss_head(x, g, target, name):
    s, d = x.shape

    def body(x_ref, g_ref, t_ref, loss_ref, dx_ref, dg_ref):
        first = pl.program_id(0) == 0
        xhat, r = _rms_stats(x_ref[...])
        gv = g_ref[...]
        err = xhat * gv - t_ref[...]
        part = 0.5 * jnp.sum(jnp.sum(err * err, axis=-1, keepdims=True), axis=0, keepdims=True) / d
        _accumulate(loss_ref, jnp.broadcast_to(part, (1, LANES)), first)
        dy = err / d
        dx_ref[...] = _rms_bwd(dy, xhat, r, gv)
        _accumulate(dg_ref, jnp.sum(dy * xhat, axis=0, keepdims=True), first)

    tile = pl.BlockSpec((TOKEN_TILE, d), lambda i: (i, 0))
    return pl.pallas_call(
        body, name=name,
        out_shape=(jax.ShapeDtypeStruct((1, LANES), F32), jax.ShapeDtypeStruct((s, d), F32),
                   jax.ShapeDtypeStruct((1, d), F32)),
        grid=(s // TOKEN_TILE,), in_specs=[tile, _row_spec(d), tile],
        out_specs=(_row_spec(LANES), tile, _row_spec(d)), compiler_params=_params(("arbitrary",)))(x, g, target)


def _tril_mask():
    row = lax.broadcasted_iota(jnp.int32, (SG_CHUNK, SG_CHUNK), 0)
    col = lax.broadcasted_iota(jnp.int32, (SG_CHUNK, SG_CHUNK), 1)
    return row >= col


def _sg_specs():
    vec = _row_spec(SG_WIDTH)
    mat = pl.BlockSpec((SG_GROUPS, SG_CHUNK, SG_CHUNK), lambda i: (0, 0, 0))
    return vec, mat


def mixer_a_fwd(proj, ln_g, ln_b, w_s, b_s, name):
    s = proj.shape[0]
    chunks = TOKEN_TILE // SG_CHUNK

    def body(z_ref, lg_ref, lb_ref, w_ref, b_ref, o_ref):
        ge = _gelu(z_ref[...].astype(F32))
        u = ge[:, :SG_WIDTH]
        xhat, _ = _ln_stats(ge[:, SG_WIDTH:])
        vn = xhat * lg_ref[...] + lb_ref[...]
        tril = _tril_mask()
        for ci in range(chunks):
            rows = slice(ci * SG_CHUNK, (ci + 1) * SG_CHUNK)
            for g in range(SG_GROUPS):
                cols = slice(g * LANES, (g + 1) * LANES)
                wm = jnp.where(tril, w_ref[g], 0.0).astype(BF16)
                mixed = jnp.dot(wm, vn[rows, cols].astype(BF16), preferred_element_type=F32) + b_ref[g]
                o_ref[rows, cols] = (u[rows, cols] * mixed).astype(BF16)

    vec, mat = _sg_specs()
    return pl.pallas_call(
        body, name=name, out_shape=jax.ShapeDtypeStruct((s, SG_WIDTH), BF16), grid=(s // TOKEN_TILE,),
        in_specs=[pl.BlockSpec((TOKEN_TILE, 2 * SG_WIDTH), lambda i: (i, 0)), vec, vec, mat, mat],
        out_specs=pl.BlockSpec((TOKEN_TILE, SG_WIDTH), lambda i: (i, 0)),
        compiler_params=_params(("parallel",)))(proj, ln_g, ln_b, w_s, b_s)


def mixer_a_bwd(proj, dy, ln_g, ln_b, w_s, b_s, name):
    s = proj.shape[0]
    chunks = TOKEN_TILE // SG_CHUNK

    def body(z_ref, dy_ref, lg_ref, lb_ref, w_ref, b_ref, dz_ref, dlg_ref, dlb_ref, dw_ref, db_ref, du_scr, dvn_scr):
        first = pl.program_id(0) == 0

        @pl.when(first)
        def _():
            dw_ref[...] = jnp.zeros_like(dw_ref)
            db_ref[...] = jnp.zeros_like(db_ref)

        z = z_ref[...].astype(F32)
        ge = _gelu(z)
        u = ge[:, :SG_WIDTH]
        xhat, rstd = _ln_stats(ge[:, SG_WIDTH:])
        lg = lg_ref[...]
        vn = xhat * lg + lb_ref[...]
        dyv = dy_ref[...].astype(F32)
        tril = _tril_mask()
        for ci in range(chunks):
            rows = slice(ci * SG_CHUNK, (ci + 1) * SG_CHUNK)
            for g in range(SG_GROUPS):
                cols = slice(g * LANES, (g + 1) * LANES)
                wm = jnp.where(tril, w_ref[g], 0.0).astype(BF16)
                vg = vn[rows, cols].astype(BF16)
                mixed = jnp.dot(wm, vg, preferred_element_type=F32) + b_ref[g]
                dyb = dyv[rows, cols]
                du_scr[rows, cols] = dyb * mixed
                dmix = dyb * u[rows, cols]
                db_ref[g] += jnp.broadcast_to(jnp.sum(dmix, axis=1, keepdims=True), (SG_CHUNK, LANES))
                dmb = dmix.astype(BF16)
                dwg = lax.dot_general(dmb, vg, _DIMS["nt"], preferred_element_type=F32)
                dw_ref[g] += jnp.where(tril, dwg, 0.0)
                dvn_scr[rows, cols] = lax.dot_general(wm, dmb, _DIMS["tn"], preferred_element_type=F32)
        dvn = dvn_scr[...]
        _accumulate(dlg_ref, jnp.sum(dvn * xhat, axis=0, keepdims=True), first)
        _accumulate(dlb_ref, jnp.sum(dvn, axis=0, keepdims=True), first)
        dvv = _ln_bwd(dvn, xhat, rstd, lg)
        gg = _gelu_grad(z)
        dz_ref[:, :SG_WIDTH] = (du_scr[...] * gg[:, :SG_WIDTH]).astype(BF16)
        dz_ref[:, SG_WIDTH:] = (dvv * gg[:, SG_WIDTH:]).astype(BF16)

    vec, mat = _sg_specs()
    mat_shape = jax.ShapeDtypeStruct((SG_GROUPS, SG_CHUNK, SG_CHUNK), F32)
    vec_shape = jax.ShapeDtypeStruct((1, SG_WIDTH), F32)
    return pl.pallas_call(
        body, name=name,
        out_shape=(jax.ShapeDtypeStruct((s, 2 * SG_WIDTH), BF16), vec_shape, vec_shape, mat_shape, mat_shape),
        grid=(s // TOKEN_TILE,),
        in_specs=[pl.BlockSpec((TOKEN_TILE, 2 * SG_WIDTH), lambda i: (i, 0)),
                  pl.BlockSpec((TOKEN_TILE, SG_WIDTH), lambda i: (i, 0)), vec, vec, mat, mat],
        out_specs=(pl.BlockSpec((TOKEN_TILE, 2 * SG_WIDTH), lambda i: (i, 0)), vec, vec, mat, mat),
        scratch_shapes=[pltpu.VMEM((TOKEN_TILE, SG_WIDTH), F32), pltpu.VMEM((TOKEN_TILE, SG_WIDTH), F32)],
        compiler_params=_params(("arbitrary",)))(proj, dy, ln_g, ln_b, w_s, b_s)


_B_A_BLOCK = 1024 // LANES
_B_G_BLOCK = 1536 // LANES
_CH_TILES = CV_WIDTH // LANES


def _col_spec(s, first_block):
    return pl.BlockSpec((s, LANES), lambda j: (0, first_block + j))


def conv_b_fwd(proj, w_pad, bias, name):
    s = proj.shape[0]

    def body(a_ref, g_ref, w_ref, b_ref, c_ref, upad):
        upad[0:HALO, :] = jnp.zeros((HALO, LANES), F32)
        upad[HALO:, :] = a_ref[...].astype(F32) * _sigmoid(g_ref[...].astype(F32))
        w = w_ref[...]
        bv = b_ref[...]

        def block(bi, carry):
            start = pl.multiple_of(bi * CONV_ROWS, CONV_ROWS)
            win = upad[pl.ds(start, CONV_ROWS + HALO), :]
            acc = jnp.zeros((CONV_ROWS, LANES), F32)
            for k in range(CV_KERNEL):
                acc = acc + w[k:k + 1, :] * _shift_rows(win, HALO - (CV_KERNEL - 1) + k, CONV_ROWS)
            c_ref[pl.ds(start, CONV_ROWS), :] = acc + bv
            return carry

        lax.fori_loop(0, s // CONV_ROWS, block, 0)

    return pl.pallas_call(
        body, name=name, out_shape=jax.ShapeDtypeStruct((s, CV_WIDTH), F32), grid=(_CH_TILES,),
        in_specs=[_col_spec(s, _B_A_BLOCK), _col_spec(s, _B_G_BLOCK), _col_spec(HALO, 0), _col_spec(1, 0)],
        out_specs=_col_spec(s, 0), scratch_shapes=[pltpu.VMEM((s + HALO, LANES), F32)],
        compiler_params=_params(("parallel",)))(proj, proj, w_pad, bias)


def conv_b_bwd(proj, w_pad, dc, name):
    s = proj.shape[0]

    def body(a_ref, g_ref, w_ref, dc_ref, da_ref, dg_ref, dw_ref, db_ref, upad, dpad, dw_scr):
        upad[0:HALO, :] = jnp.zeros((HALO, LANES), F32)
        upad[HALO:, :] = a_ref[...].astype(F32) * _sigmoid(g_ref[...].astype(F32))
        dcv = dc_ref[...]
        dpad[0:s, :] = dcv
        dpad[s:, :] = jnp.zeros((HALO, LANES), F32)
        db_ref[...] = jnp.sum(dcv, axis=0, keepdims=True)
        dw_scr[...] = jnp.zeros((HALO, LANES), F32)
        w = w_ref[...]

        def block(bi, carry):
            start = pl.multiple_of(bi * CONV_ROWS, CONV_ROWS)
            uwin = upad[pl.ds(start, CONV_ROWS + HALO), :]
            dwin = dpad[pl.ds(start, CONV_ROWS + HALO), :]
            dcb = dwin[:CONV_ROWS]
            du = jnp.zeros((CONV_ROWS, LANES), F32)
            for k in range(CV_KERNEL):
                du = du + w[k:k + 1, :] * _shift_rows(dwin, CV_KERNEL - 1 - k, CONV_ROWS)
                ush = _shift_rows(uwin, HALO - (CV_KERNEL - 1) + k, CONV_ROWS)
                dw_scr[k:k + 1, :] += jnp.sum(dcb * ush, axis=0, keepdims=True)
            av = a_ref[pl.ds(start, CONV_ROWS), :].astype(F32)
            sg = _sigmoid(g_ref[pl.ds(start, CONV_ROWS), :].astype(F32))
            da_ref[pl.ds(start, CONV_ROWS), :] = (du * sg).astype(BF16)
            dg_ref[pl.ds(start, CONV_ROWS), :] = (du * av * sg * (1.0 - sg)).astype(BF16)
            return carry

        lax.fori_loop(0, s // CONV_ROWS, block, 0)
        dw_ref[...] = dw_scr[...]

    act = jax.ShapeDtypeStruct((s, CV_WIDTH), BF16)
    return pl.pallas_call(
        body, name=name,
        out_shape=(act, act, jax.ShapeDtypeStruct((HALO, CV_WIDTH), F32), jax.ShapeDtypeStruct((1, CV_WIDTH), F32)),
        grid=(_CH_TILES,),
        in_specs=[_col_spec(s, _B_A_BLOCK), _col_spec(s, _B_G_BLOCK), _col_spec(HALO, 0), _col_spec(s, 0)],
        out_specs=(_col_spec(s, 0), _col_spec(s, 0), _col_spec(HALO, 0), _col_spec(1, 0)),
        scratch_shapes=[pltpu.VMEM((s + HALO, LANES), F32), pltpu.VMEM((s + HALO, LANES), F32),
                        pltpu.VMEM((HALO, LANES), F32)],
        compiler_params=_params(("parallel",)))(proj, proj, w_pad, dc)


def ln_silu_fwd(c, ln_g, ln_b, name):
    s, d = c.shape

    def body(c_ref, g_ref, b_ref, o_ref):
        xhat, _ = _ln_stats(c_ref[...])
        cn = xhat * g_ref[...] + b_ref[...]
        o_ref[...] = (cn * _sigmoid(cn)).astype(BF16)

    tile = pl.BlockSpec((TOKEN_TILE, d), lambda i: (i, 0))
    return pl.pallas_call(
        body, name=name, out_shape=jax.ShapeDtypeStruct((s, d), BF16), grid=(s // TOKEN_TILE,),
        in_specs=[tile, _row_spec(d), _row_spec(d)], out_specs=tile,
        compiler_params=_params(("parallel",)))(c, ln_g, ln_b)


def ln_silu_bwd(c, dy, ln_g, ln_b, name):
    s, d = c.shape

    def body(c_ref, dy_ref, g_ref, b_ref, dc_ref, dg_ref, db_ref):
        first = pl.program_id(0) == 0
        xhat, rstd = _ln_stats(c_ref[...])
        gv = g_ref[...]
        cn = xhat * gv + b_ref[...]
        sg = _sigmoid(cn)
        dcn = dy_ref[...].astype(F32) * sg * (1.0 + cn * (1.0 - sg))
        _accumulate(dg_ref, jnp.sum(dcn * xhat, axis=0, keepdims=True), first)
        _accumulate(db_ref, jnp.sum(dcn, axis=0, keepdims=True), first)
        dc_ref[...] = _ln_bwd(dcn, xhat, rstd, gv)

    tile = pl.BlockSpec((TOKEN_TILE, d), lambda i: (i, 0))
    vec_shape = jax.ShapeDtypeStruct((1, d), F32)
    return pl.pallas_call(
        body, name=name, out_shape=(jax.ShapeDtypeStruct((s, d), F32), vec_shape, vec_shape),
        grid=(s // TOKEN_TILE,), in_specs=[tile, tile, _row_spec(d), _row_spec(d)],
        out_specs=(tile, _row_spec(d), _row_spec(d)), compiler_params=_params(("arbitrary",)))(c, dy, ln_g, ln_b)


_D_BLOCK = 2816 // LANES


def _conv3(win, w):
    acc = jnp.zeros((CONV_ROWS, LANES), F32)
    for k in range(SC_KERNEL):
        acc = acc + w[k:k + 1, :] * _shift_rows(win, HALO - (SC_KERNEL - 1) + k, CONV_ROWS)
    return acc


def conv_d_fwd(proj, w_pad, name):
    s = proj.shape[0]

    def body(bg_ref, cg_ref, h_ref, w_ref, o_ref, ppad):
        ppad[0:HALO, :] = jnp.zeros((HALO, LANES), F32)
        ppad[HALO:, :] = cg_ref[...].astype(F32) * h_ref[...].astype(F32)
        w = w_ref[...]

        def block(bi, carry):
            start = pl.multiple_of(bi * CONV_ROWS, CONV_ROWS)
            cv = _conv3(ppad[pl.ds(start, CONV_ROWS + HALO), :], w)
            o_ref[pl.ds(start, CONV_ROWS), :] = (bg_ref[pl.ds(start, CONV_ROWS), :].astype(F32) * cv).astype(BF16)
            return carry

        lax.fori_loop(0, s // CONV_ROWS, block, 0)

    return pl.pallas_call(
        body, name=name, out_shape=jax.ShapeDtypeStruct((s, SC_WIDTH), BF16), grid=(_CH_TILES,),
        in_specs=[_col_spec(s, _D_BLOCK), _col_spec(s, _D_BLOCK + _CH_TILES), _col_spec(s, _D_BLOCK + 2 * _CH_TILES),
                  _col_spec(SUBLANES, 0)],
        out_specs=_col_spec(s, 0), scratch_shapes=[pltpu.VMEM((s + HALO, LANES), F32)],
        compiler_params=_params(("parallel",)))(proj, proj, proj, w_pad)


def conv_d_bwd(proj, w_pad, dy, name):
    s = proj.shape[0]

    def body(bg_ref, cg_ref, h_ref, w_ref, dy_ref, dbg_ref, dcg_ref, dh_ref, dw_ref, ppad, dpad, dw_scr):
        ppad[0:HALO, :] = jnp.zeros((HALO, LANES), F32)
        ppad[HALO:, :] = cg_ref[...].astype(F32) * h_ref[...].astype(F32)
        dpad[0:s, :] = dy_ref[...].astype(F32) * bg_ref[...].astype(F32)
        dpad[s:, :] = jnp.zeros((HALO, LANES), F32)
        dw_scr[...] = jnp.zeros((SUBLANES, LANES), F32)
        w = w_ref[...]

        def block(bi, carry):
            start = pl.multiple_of(bi * CONV_ROWS, CONV_ROWS)
            rows = pl.ds(start, CONV_ROWS)
            pwin = ppad[pl.ds(start, CONV_ROWS + HALO), :]
            dwin = dpad[pl.ds(start, CONV_ROWS + HALO), :]
            dcvb = dwin[:CONV_ROWS]
            dbg_ref[rows, :] = (dy_ref[rows, :].astype(F32) * _conv3(pwin, w)).astype(BF16)
            dp = jnp.zeros((CONV_ROWS, LANES), F32)
            for k in range(SC_KERNEL):
                dp = dp + w[k:k + 1, :] * _shift_rows(dwin, SC_KERNEL - 1 - k, CONV_ROWS)
                psh = _shift_rows(pwin, HALO - (SC_KERNEL - 1) + k, CONV_ROWS)
                dw_scr[k:k + 1, :] += jnp.sum(dcvb * psh, axis=0, keepdims=True)
            dcg_ref[rows, :] = (dp * h_ref[rows, :].astype(F32)).astype(BF16)
            dh_ref[rows, :] = (dp * cg_ref[rows, :].astype(F32)).astype(BF16)
            return carry

        lax.fori_loop(0, s // CONV_ROWS, block, 0)
        dw_ref[...] = dw_scr[...]

    act = jax.ShapeDtypeStruct((s, SC_WIDTH), BF16)
    return pl.pallas_call(
        body, name=name, out_shape=(act, act, act, jax.ShapeDtypeStruct((SUBLANES, SC_WIDTH), F32)),
        grid=(_CH_TILES,),
        in_specs=[_col_spec(s, _D_BLOCK), _col_spec(s, _D_BLOCK + _CH_TILES), _col_spec(s, _D_BLOCK + 2 * _CH_TILES),
                  _col_spec(SUBLANES, 0), _col_spec(s, 0)],
        out_specs=(_col_spec(s, 0), _col_spec(s, 0), _col_spec(s, 0), _col_spec(SUBLANES, 0)),
        scratch_shapes=[pltpu.VMEM((s + HALO, LANES), F32), pltpu.VMEM((s + HALO, LANES), F32),
                        pltpu.VMEM((SUBLANES, LANES), F32)],
        compiler_params=_params(("parallel",)))(proj, proj, proj, w_pad, dy)


_QK_BLOCK = 2048 // LANES
_QK_BLOCKS = (Q_WIDTH + KV_WIDTH) // LANES


def _swap_halves(t):
    lane = lax.broadcasted_iota(jnp.int32, t.shape, 1)
    low = (lane % HEAD_DIM) < (HEAD_DIM // 2)
    return jnp.where(low, pltpu.roll(t, LANES - HEAD_DIM // 2, axis=1), pltpu.roll(t, HEAD_DIM // 2, axis=1))


def rope_fwd(proj, cos_t, sin_t, name):
    s = proj.shape[0]

    def body(t_ref, c_ref, s_ref, o_ref):
        t = t_ref[...].astype(F32)
        o_ref[...] = (t * c_ref[...] + _swap_halves(t) * s_ref[...]).astype(BF16)

    tab = pl.BlockSpec((TOKEN_TILE, LANES), lambda i, j: (i, 0))
    return pl.pallas_call(
        body, name=name, out_shape=jax.ShapeDtypeStruct((s, Q_WIDTH + KV_WIDTH), BF16),
        grid=(s // TOKEN_TILE, _QK_BLOCKS),
        in_specs=[pl.BlockSpec((TOKEN_TILE, LANES), lambda i, j: (i, _QK_BLOCK + j)), tab, tab],
        out_specs=pl.BlockSpec((TOKEN_TILE, LANES), lambda i, j: (i, j)),
        compiler_params=_params(("parallel", "parallel")))(proj, cos_t, sin_t)


def rope_bwd(d_cur, d_prev, cos_t, sin_t, name):
    s, w = d_cur.shape

    def body(a_ref, b_ref, c_ref, s_ref, o_ref):
        d = a_ref[...] + b_ref[...]
        o_ref[...] = (d * c_ref[...] + _swap_halves(d) * s_ref[...]).astype(BF16)

    tab = pl.BlockSpec((TOKEN_TILE, LANES), lambda i, j: (i, 0))
    blk = pl.BlockSpec((TOKEN_TILE, LANES), lambda i, j: (i, j))
    return pl.pallas_call(
        body, name=name, out_shape=jax.ShapeDtypeStruct((s, w), BF16), grid=(s // TOKEN_TILE, w // LANES),
        in_specs=[blk, blk, tab, tab], out_specs=blk,
        compiler_params=_params(("parallel", "parallel")))(d_cur, d_prev, cos_t, sin_t)


_GROUP = N_Q_HEADS // N_KV_HEADS
_NEG = -1e30


def _attn_specs():
    q_spec = pl.BlockSpec((_GROUP, WINDOW, HEAD_DIM), lambda h, n: (h, n, 0))
    cur = pl.BlockSpec((1, WINDOW, HEAD_DIM), lambda h, n: (h, n, 0))
    prev = pl.BlockSpec((1, WINDOW, HEAD_DIM), lambda h, n: (h, jnp.maximum(n - 1, 0), 0))
    sink = pl.BlockSpec((_GROUP, 1, LANES), lambda h, n: (h, 0, 0))
    return q_spec, cur, prev, sink


def _attn_valid(n):
    qi = lax.broadcasted_iota(jnp.int32, (WINDOW, 2 * WINDOW), 0)
    kj = lax.broadcasted_iota(jnp.int32, (WINDOW, 2 * WINDOW), 1)
    delta = qi + WINDOW - kj
    return (delta >= 0) & (delta < WINDOW) & ((kj >= WINDOW) | (n > 0))


def _attn_probs(q, kcat, valid, sink_row):
    sc = lax.dot_general(q, kcat, _DIMS["nt"], preferred_element_type=F32) * (HEAD_DIM ** -0.5)
    sc = jnp.where(valid, sc, _NEG)
    sink = jnp.max(sink_row, axis=-1, keepdims=True)
    m = jnp.maximum(jnp.max(sc, axis=-1, keepdims=True), sink)
    p = jnp.where(valid, jnp.exp(sc - m), 0.0)
    es = jnp.exp(sink - m)
    inv = 1.0 / (jnp.sum(p, axis=-1, keepdims=True) + es)
    return p * inv, es * inv


def attention_fwd(qh, kh, vh, sinks_b, name):
    s = qh.shape[1]

    def body(q_ref, kc_ref, kp_ref, vc_ref, vp_ref, sk_ref, o_ref):
        valid = _attn_valid(pl.program_id(1))
        kcat = jnp.concatenate([kp_ref[0], kc_ref[0]], axis=0)
        vcat = jnp.concatenate([vp_ref[0], vc_ref[0]], axis=0)
        for g in range(_GROUP):
            probs, _ = _attn_probs(q_ref[g], kcat, valid, sk_ref[g])
            o_ref[g] = jnp.dot(probs.astype(BF16), vcat, preferred_element_type=F32).astype(BF16)

    q_spec, cur, prev, sink = _attn_specs()
    return pl.pallas_call(
        body, name=name, out_shape=jax.ShapeDtypeStruct(qh.shape, BF16), grid=(N_KV_HEADS, s // WINDOW),
        in_specs=[q_spec, cur, prev, cur, prev, sink], out_specs=q_spec,
        compiler_params=_params(("parallel", "parallel")))(qh, kh, kh, vh, vh, sinks_b)


def attention_bwd(qh, kh, vh, sinks_b, doh, name):
    s = qh.shape[1]

    def body(q_ref, kc_ref, kp_ref, vc_ref, vp_ref, sk_ref, do_ref, dq_ref, dkc_ref, dkp_ref, dvc_ref, dvp_ref, ds_ref):
        n = pl.program_id(1)
        valid = _attn_valid(n)
        kcat = jnp.concatenate([kp_ref[0], kc_ref[0]], axis=0)
        vcat = jnp.concatenate([vp_ref[0], vc_ref[0]], axis=0)
        dk = jnp.zeros((2 * WINDOW, HEAD_DIM), F32)
        dv = jnp.zeros((2 * WINDOW, HEAD_DIM), F32)
        for g in range(_GROUP):
            q = q_ref[g]
            do = do_ref[g]
            probs, ps = _attn_probs(q, kcat, valid, sk_ref[g])
            dprobs = lax.dot_general(do, vcat, _DIMS["nt"], preferred_element_type=F32)
            dv = dv + lax.dot_general(probs.astype(BF16), do, _DIMS["tn"], preferred_element_type=F32)
            rs = jnp.sum(probs * dprobs, axis=-1, keepdims=True)
            dsb = (probs * (dprobs - rs) * (HEAD_DIM ** -0.5)).astype(BF16)
            dq_ref[g] = jnp.dot(dsb, kcat, preferred_element_type=F32)
            dk = dk + lax.dot_general(dsb, q, _DIMS["tn"], preferred_element_type=F32)
            dsink = jnp.broadcast_to(-jnp.sum(ps * rs, axis=0, keepdims=True), (1, LANES))

            @pl.when(n == 0)
            def _():
                ds_ref[g] = dsink

            @pl.when(n > 0)
            def _():
                ds_ref[g] += dsink

        dkp_ref[0] = dk[:WINDOW]
        dkc_ref[0] = dk[WINDOW:]
        dvp_ref[0] = dv[:WINDOW]
        dvc_ref[0] = dv[WINDOW:]

    q_spec, cur, prev, sink = _attn_specs()
    kv_shape = jax.ShapeDtypeStruct(kh.shape, F32)
    return pl.pallas_call(
        body, name=name,
        out_shape=(jax.ShapeDtypeStruct(qh.shape, F32), kv_shape, kv_shape, kv_shape, kv_shape,
                   jax.ShapeDtypeStruct(sinks_b.shape, F32)),
        grid=(N_KV_HEADS, s // WINDOW), in_specs=[q_spec, cur, prev, cur, prev, sink, q_spec],
        out_specs=(q_spec, cur, cur, cur, cur, sink),
        compiler_params=_params(("parallel", "arbitrary")))(qh, kh, kh, vh, vh, sinks_b, doh)


def _to_heads(t, heads):
    return t.reshape(t.shape[0], heads, HEAD_DIM).transpose(1, 0, 2)


def _from_heads(t):
    return t.transpose(1, 0, 2).reshape(t.shape[1], t.shape[0] * HEAD_DIM)


def _shift_window(t):
    return jnp.concatenate([t[:, WINDOW:], jnp.zeros_like(t[:, :WINDOW])], axis=1)


def merge_fwd(zg, branches, name):
    s = zg.shape[0]

    def body(zg_ref, b0, b1, b2, b3, o_ref):
        acc = jnp.zeros((TOKEN_TILE, D_MODEL), F32)
        for n, b_ref in enumerate((b0, b1, b2, b3)):
            gate = _sigmoid(zg_ref[:, n * D_MODEL:(n + 1) * D_MODEL].astype(F32))
            acc = acc + gate * b_ref[...].astype(F32)
        o_ref[...] = acc.astype(BF16)

    tile = pl.BlockSpec((TOKEN_TILE, D_MODEL), lambda i: (i, 0))
    wide = pl.BlockSpec((TOKEN_TILE, N_BRANCH * D_MODEL), lambda i: (i, 0))
    return pl.pallas_call(
        body, name=name, out_shape=jax.ShapeDtypeStruct((s, D_MODEL), BF16), grid=(s // TOKEN_TILE,),
        in_specs=[wide, tile, tile, tile, tile], out_specs=tile,
        compiler_params=_params(("parallel",)))(zg, *branches)


def merge_bwd(zg, branches, dm, name):
    s = zg.shape[0]

    def body(zg_ref, b0, b1, b2, b3, dm_ref, dzg_ref, d0, d1, d2, d3):
        dmv = dm_ref[...].astype(F32)
        for n, (b_ref, d_ref) in enumerate(((b0, d0), (b1, d1), (b2, d2), (b3, d3))):
            cols = slice(n * D_MODEL, (n + 1) * D_MODEL)
            gate = _sigmoid(zg_ref[:, cols].astype(F32))
            d_ref[...] = (gate * dmv).astype(BF16)
            dzg_ref[:, cols] = (dmv * b_ref[...].astype(F32) * gate * (1.0 - gate)).astype(BF16)

    tile = pl.BlockSpec((TOKEN_TILE, D_MODEL), lambda i: (i, 0))
    wide = pl.BlockSpec((TOKEN_TILE, N_BRANCH * D_MODEL), lambda i: (i, 0))
    act = jax.ShapeDtypeStruct((s, D_MODEL), BF16)
    return pl.pallas_call(
        body, name=name, out_shape=(jax.ShapeDtypeStruct((s, N_BRANCH * D_MODEL), BF16), act, act, act, act),
        grid=(s // TOKEN_TILE,), in_specs=[wide, tile, tile, tile, tile, tile],
        out_specs=(wide, tile, tile, tile, tile), compiler_params=_params(("parallel",)))(zg, *branches, dm)


def swiglu_fwd(gu, name):
    s = gu.shape[0]

    def body(g_ref, u_ref, o_ref):
        gate = g_ref[...].astype(F32)
        o_ref[...] = (gate * _sigmoid(gate) * u_ref[...].astype(F32)).astype(BF16)

    return pl.pallas_call(
        body, name=name, out_shape=jax.ShapeDtypeStruct((s, D_FF), BF16), grid=(s // TOKEN_TILE,),
        in_specs=[pl.BlockSpec((TOKEN_TILE, D_FF), lambda i: (i, 0)), pl.BlockSpec((TOKEN_TILE, D_FF), lambda i: (i, 1))],
        out_specs=pl.BlockSpec((TOKEN_TILE, D_FF), lambda i: (i, 0)), compiler_params=_params(("parallel",)))(gu, gu)


def swiglu_bwd(gu, dact, name):
    s = gu.shape[0]

    def body(g_ref, u_ref, da_ref, o_ref):
        gate = g_ref[...].astype(F32)
        sg = _sigmoid(gate)
        da = da_ref[...].astype(F32)
        o_ref[:, :D_FF] = (da * u_ref[...].astype(F32) * sg * (1.0 + gate * (1.0 - sg))).astype(BF16)
        o_ref[:, D_FF:] = (da * gate * sg).astype(BF16)

    half = pl.BlockSpec((TOKEN_TILE, D_FF), lambda i: (i, 0))
    return pl.pallas_call(
        body, name=name, out_shape=jax.ShapeDtypeStruct((s, 2 * D_FF), BF16), grid=(s // TOKEN_TILE,),
        in_specs=[half, pl.BlockSpec((TOKEN_TILE, D_FF), lambda i: (i, 1)), half],
        out_specs=pl.BlockSpec((TOKEN_TILE, 2 * D_FF), lambda i: (i, 0)),
        compiler_params=_params(("parallel",)))(gu, gu, dact)


ADAMW_BLOCK_BYTES = 1 << 20


def adamw(parts, w, m, v, name):
    r, c = w.shape
    n_parts = len(parts)
    rows = r // n_parts
    tr = _divisor_tile(rows, max(SUBLANES, ADAMW_BLOCK_BYTES // (4 * c)), SUBLANES)
    tiles = rows // tr

    def part_spec(j):
        return pl.BlockSpec((N_DEV, tr, c), lambda i: (0, jnp.clip(i - j * tiles, 0, tiles - 1), 0))

    def body(*refs):
        p_refs = refs[:n_parts]
        w_ref, m_ref, v_ref, g_ref, d_ref, nm_ref, nv_ref = refs[n_parts:]
        which = pl.program_id(0) // tiles
        g = None
        for j, p_ref in enumerate(p_refs):
            gj = p_ref[0].astype(F32)
            for i in range(1, N_DEV):
                gj = gj + p_ref[i].astype(F32)
            g = gj if g is None else jnp.where(which == j, gj, g)
        nm = ADAM_B1 * m_ref[...] + (1.0 - ADAM_B1) * g
        nv = ADAM_B2 * v_ref[...] + (1.0 - ADAM_B2) * (g * g)
        m_hat = nm / (1.0 - ADAM_B1 ** ADAM_STEP)
        v_hat = nv / (1.0 - ADAM_B2 ** ADAM_STEP)
        g_ref[...] = g
        d_ref[...] = -ADAM_LR * (m_hat / (jnp.sqrt(v_hat) + ADAM_EPS) + ADAM_WD * w_ref[...])
        nm_ref[...] = nm
        nv_ref[...] = nv

    tile = pl.BlockSpec((tr, c), lambda i: (i, 0))
    shape = jax.ShapeDtypeStruct((r, c), F32)
    return pl.pallas_call(
        body, name=name, out_shape=(shape, shape, shape, shape), grid=(r // tr,),
        in_specs=[part_spec(j) for j in range(n_parts)] + [tile, tile, tile],
        out_specs=(tile, tile, tile, tile), compiler_params=_params(("parallel",)))(*parts, w, m, v)


_RELATIONS = [(a, b, e) for a in (0, 1) for b in (0, 1) for e in (0, 1)][1:]


_HBM_SPEC = pl.BlockSpec(memory_space=pltpu.HBM)
_SEM_SPEC = pl.BlockSpec(memory_space=pltpu.SEMAPHORE)
_ANY_SPEC = pl.BlockSpec(memory_space=pl.ANY)
_DATAFLOW = pltpu.SideEffectType.DATAFLOW_SIDE_EFFECTING


def _remote_copies(ins, lands, send_sems, recv_sems, scatter):
    x, y, c = lax.axis_index("x"), lax.axis_index("y"), lax.axis_index("c")
    me = 4 * x + 2 * y + c
    copies = []
    for t in range(len(ins)):
        for k, (a, b, e) in enumerate(_RELATIONS):
            px, py, pc = (x + a) % 2, (y + b) % 2, (c + e) % 2
            src = ins[t].at[4 * px + 2 * py + pc] if scatter[t] else ins[t]
            copies.append(pltpu.make_async_remote_copy(
                src_ref=src, dst_ref=lands[t].at[me], send_sem=send_sems.at[t * len(_RELATIONS) + k],
                recv_sem=recv_sems.at[t * len(_RELATIONS) + k],
                device_id=(px, py, pc), device_id_type=pl.DeviceIdType.MESH))
    return copies


def exchange_start(arrays, scatter, name, after=None):
    n = len(arrays)
    n_rel = len(_RELATIONS)
    land_shapes = [a.shape if scatter[t] else (N_DEV,) + a.shape for t, a in enumerate(arrays)]

    def body(*refs):
        ins, lands = refs[:n], refs[n:2 * n]
        send_sems, recv_sems = refs[-2 * n - 3], refs[-2 * n - 2]
        token = refs[-1]
        for cp in _remote_copies(ins, lands, send_sems, recv_sems, scatter):
            cp.start()
        token[...] = jnp.zeros_like(token)

    sems = pltpu.SemaphoreType.DMA((n * n_rel,))
    out_shape = ((sems, sems) + tuple(pltpu.HBM(a.shape, a.dtype) for a in arrays)
                 + tuple(pltpu.HBM(s, a.dtype) for s, a in zip(land_shapes, arrays))
                 + (jax.ShapeDtypeStruct((SUBLANES, LANES), F32),))
    operands = [pltpu.with_memory_space_constraint(a, pltpu.HBM) for a in arrays]
    operands += [pltpu.with_memory_space_constraint(lax.empty(s, a.dtype), pltpu.HBM) for s, a in zip(land_shapes, arrays)]
    in_specs = [_HBM_SPEC] * (2 * n)
    if after is not None:
        operands.append(after)
        in_specs.append(_ANY_SPEC)
    res = pl.pallas_call(
        body, name=name, out_shape=out_shape, in_specs=in_specs,
        out_specs=(_SEM_SPEC, _SEM_SPEC) + (_HBM_SPEC,) * (2 * n) + (pl.BlockSpec(memory_space=pltpu.VMEM),),
        input_output_aliases={i: 2 + i for i in range(2 * n)},
        compiler_params=pltpu.CompilerParams(has_side_effects=_DATAFLOW))(*operands)
    handle = (res[0], res[1], res[2:2 + n], res[2 + n:2 + 2 * n], tuple(scatter))
    return handle, res[-1]


def exchange_wait(handle, after, name):
    send_sems, recv_sems, sources, lands, scatter = handle
    n = len(sources)

    def body(*refs):
        ins, lzs = refs[:n], refs[n:2 * n]
        send_ref, recv_ref = refs[2 * n], refs[2 * n + 1]
        for cp in _remote_copies(ins, lzs, send_ref, recv_ref, scatter):
            cp.wait_send()
            cp.wait_recv()

    out_shape = (tuple(pltpu.HBM(a.shape, a.dtype) for a in sources) + tuple(pltpu.HBM(a.shape, a.dtype) for a in lands))
    res = pl.pallas_call(
        body, name=name, out_shape=out_shape, in_specs=[_HBM_SPEC] * (2 * n) + [_SEM_SPEC, _SEM_SPEC, _ANY_SPEC],
        out_specs=(_HBM_SPEC,) * (2 * n), input_output_aliases={i: i for i in range(2 * n)},
        compiler_params=pltpu.CompilerParams(has_side_effects=_DATAFLOW))(*sources, *lands, send_sems, recv_sems, after)
    return res[n:]


def _place_own(landed, own, me):
    return lax.dynamic_update_index_in_dim(landed, own, me, 0)


_SMALL = ("norm_mix", "sg_ln_g", "sg_ln_b", "sg_w", "sg_b", "cv_b", "cv_ln_g", "cv_ln_b", "attn_sinks", "norm_ffn",
          "norm_final")
_PACK_UNIT = SUBLANES * LANES


def _pack(tensors):
    rows = []
    for t in tensors:
        flat = t.reshape(-1)
        pad = (-flat.shape[0]) % _PACK_UNIT
        rows.append(jnp.pad(flat, (0, pad)).reshape(-1, LANES))
    return jnp.concatenate(rows, axis=0)


def _unpack(packed, like):
    out, row = [], 0
    for t in like:
        size = 1
        for d in t.shape:
            size *= d
        rows = -(-size // _PACK_UNIT) * SUBLANES
        out.append(packed[row:row + rows].reshape(-1)[:size].reshape(t.shape))
        row += rows
    return out


def _layer_fwd(l, x, p, late_params):
    tag = f"l{l}_"
    xn = rmsnorm_fwd(x, p["norm_mix"], tag + "norm_mix")
    proj = matmul(xn, p["w_in_a"], "nn", BF16, tag + "proj_a", tm_cap=1024, tn_cap=2176)
    zg = matmul(xn, p["w_in_g"], "nn", BF16, tag + "proj_g", tm_cap=1024, tn_cap=2048)
    y_a = mixer_a_fwd(proj, p["sg_ln_g"], p["sg_ln_b"], p["sg_w"], p["sg_b"], tag + "mix_a")
    conv = conv_b_fwd(proj, p["cv_w"], p["cv_b"], tag + "conv_b")
    y_b = ln_silu_fwd(conv, p["cv_ln_g"], p["cv_ln_b"], tag + "ln_silu")
    qk = rope_fwd(proj, p["cos"], p["sin"], tag + "rope")
    qh = _to_heads(qk[:, :Q_WIDTH], N_Q_HEADS)
    kh = _to_heads(qk[:, Q_WIDTH:], N_KV_HEADS)
    vh = _to_heads(proj[:, 2688:2816], N_KV_HEADS)
    oh = attention_fwd(qh, kh, vh, p["sinks"], tag + "attn")
    y_c = _from_heads(oh)
    y_d = conv_d_fwd(proj, p["sc_w"], tag + "conv_d")
    ys = (y_a, y_b, y_c, y_d)
    p = {**p, **late_params(y_d)}
    branches = tuple(matmul(ys[n], p["w_branch"][n], "nn", BF16, tag + f"branch{n}", tm_cap=1024, tn_cap=1024)
                     for n in range(N_BRANCH))
    merged = merge_fwd(zg, branches, tag + "merge")
    x_mid = matmul(merged, p["w_out"], "nn", F32, tag + "out", add=x, tm_cap=1024, tn_cap=1024)
    hn = rmsnorm_fwd(x_mid, p["norm_ffn"], tag + "norm_ffn")
    gu = matmul(hn, p["w_gate_up"], "nn", BF16, tag + "gate_up", tm_cap=512, tn_cap=2816)
    act = swiglu_fwd(gu, tag + "swiglu")
    x_out = matmul(act, p["w_down"], "nn", F32, tag + "down", add=x_mid, tm_cap=512, tn_cap=1024)
    saved = dict(x=x, xn=xn, proj=proj, zg=zg, conv=conv, qh=qh, kh=kh, vh=vh, ys=ys, branches=branches,
                 merged=merged, x_mid=x_mid, hn=hn, gu=gu, act=act)
    return x_out, saved, p


def _layer_bwd(l, dx_out, p, sv, emit, after=None):
    tag = f"l{l}_b_"
    g = {}
    dact = matmul(dx_out, p["w_down"], "nt", BF16, tag + "dact", after=after, tm_cap=512, tn_cap=2816)
    dw_down = matmul(sv["act"], dx_out, "tn", BF16, tag + "dw_down", tm_cap=1408, tn_cap=512)
    dgu = swiglu_bwd(sv["gu"], dact, tag + "swiglu")
    dhn = matmul(dgu, p["w_gate_up"], "nt", BF16, tag + "dhn", tm_cap=512, tn_cap=512)
    dw_gate_up = matmul(sv["hn"], dgu, "tn", BF16, tag + "dw_gate_up", tm_cap=1024, tn_cap=1408)
    token = emit("a", {"w_gate_up": dw_gate_up, "w_down": dw_down})
    dx_mid, g["norm_ffn"] = rmsnorm_bwd(sv["x_mid"], p["norm_ffn"], dhn, dx_out, tag + "norm_ffn")
    dmerged = matmul(dx_mid, p["w_out"], "nt", BF16, tag + "dmerged", after=token, tm_cap=1024, tn_cap=1024)
    dw_out = matmul(sv["merged"], dx_mid, "tn", BF16, tag + "dw_out", tm_cap=1024, tn_cap=512)
    dzg, *dbranches = merge_bwd(sv["zg"], sv["branches"], dmerged, tag + "merge")
    dys = [matmul(dbranches[n], p["w_branch"][n], "nt", BF16, tag + f"dy{n}", tm_cap=1024, tn_cap=512)
           for n in range(N_BRANCH)]
    dw_branch = jnp.stack(
        [matmul(sv["ys"][n], dbranches[n], "tn", BF16, tag + f"dw_branch{n}", tm_cap=512, tn_cap=1024)
         for n in range(N_BRANCH)])
    token = emit("b", {"w_branch": dw_branch, "w_out": dw_out})
    proj = sv["proj"]
    dz_a, g["sg_ln_g"], g["sg_ln_b"], g["sg_w"], dsb = mixer_a_bwd(
        proj, dys[0], p["sg_ln_g"], p["sg_ln_b"], p["sg_w"], p["sg_b"], tag + "mix_a")
    g["sg_b"] = dsb[:, :, 0]
    dconv, g["cv_ln_g"], g["cv_ln_b"] = ln_silu_bwd(sv["conv"], dys[1], p["cv_ln_g"], p["cv_ln_b"], tag + "ln_silu")
    da, dgate, dcw, g["cv_b"] = conv_b_bwd(proj, p["cv_w"], dconv, tag + "conv_b")
    g["cv_w"] = dcw[:CV_KERNEL]
    doh = _to_heads(dys[2], N_Q_HEADS)
    dqh, dkc, dkp, dvc, dvp, dsk = attention_bwd(sv["qh"], sv["kh"], sv["vh"], p["sinks"], doh, tag + "attn")
    g["attn_sinks"] = dsk[:, 0, 0]
    dqk_cur = jnp.concatenate([_from_heads(dqh), _from_heads(dkc)], axis=1)
    dqk_prev = jnp.concatenate([jnp.zeros((SEQ, Q_WIDTH), F32), _from_heads(_shift_window(dkp))], axis=1)
    dqk = rope_bwd(dqk_cur, dqk_prev, p["cos"], -p["sin"], tag + "rope")
    dv = (_from_heads(dvc) + _from_heads(_shift_window(dvp))).astype(BF16)
    dbg, dcg, dh, dsw = conv_d_bwd(proj, p["sc_w"], dys[3], tag + "conv_d")
    g["sc_w"] = dsw[:SC_KERNEL]
    dproj = jnp.concatenate([dz_a, da, dgate, dqk, dv, dbg, dcg, dh], axis=1)
    dxn = matmul(dproj, p["w_in_a"], "nt", F32, tag + "dxn_a", after=token, tm_cap=512, tn_cap=512)
    dxn = matmul(dzg, p["w_in_g"], "nt", F32, tag + "dxn_g", add=dxn, tm_cap=512, tn_cap=512)
    g["w_in"] = jnp.concatenate(
        [matmul(sv["xn"], dproj, "tn", BF16, tag + "dw_in_a", tm_cap=512, tn_cap=2176),
         matmul(sv["xn"], dzg, "tn", BF16, tag + "dw_in_g", tm_cap=1024, tn_cap=1024)], axis=1)
    dx_in, g["norm_mix"] = rmsnorm_bwd(sv["x"], p["norm_mix"], dxn, dx_mid, tag + "norm_mix")
    return dx_in, g


_EARLY = ("w_in", "cv_w", "sc_w")
_LATE = ("w_branch", "w_out", "w_gate_up", "w_down")


def _full_weight(name, t):
    if name in ("w_out", "w_down"):
        return t.reshape(-1, t.shape[-1])
    if name == "w_branch":
        return t.transpose(1, 2, 0, 3).reshape(N_BRANCH, SG_WIDTH, D_MODEL)
    return t.transpose(1, 0, 2).reshape(t.shape[1], -1)


def _to_blocks(name, full):
    if name in ("w_out", "w_down"):
        return full.reshape(N_DEV, -1, full.shape[-1])
    if name == "w_branch":
        return full.reshape(N_BRANCH, SG_WIDTH, N_DEV, -1).transpose(2, 0, 1, 3)
    return full.reshape(full.shape[0], N_DEV, -1).transpose(1, 0, 2)


def _rope_tables():
    pos = jnp.arange(SEQ, dtype=F32)
    inv_freq = 1.0 / (ROPE_THETA ** (jnp.arange(0, HEAD_DIM, 2, dtype=F32) / HEAD_DIM))
    ang = pos[:, None] * inv_freq[None, :]
    cos, sin = jnp.cos(ang), jnp.sin(ang)
    reps = LANES // HEAD_DIM
    return jnp.tile(jnp.concatenate([cos, cos], axis=1), (1, reps)), jnp.tile(jnp.concatenate([-sin, sin], axis=1), (1, reps))


def kernel(x, norm_mix, w_in, sg_ln_g, sg_ln_b, sg_w, sg_b, cv_w, cv_b, cv_ln_g, cv_ln_b, attn_sinks, sc_w, w_branch, w_out, norm_ffn, w_gate_up, w_down, norm_final, loss_target, m_norm_mix, m_w_in, m_sg_ln_g, m_sg_ln_b, m_sg_w, m_sg_b, m_cv_w, m_cv_b, m_cv_ln_g, m_cv_ln_b, m_attn_sinks, m_sc_w, m_w_branch, m_w_out, m_norm_ffn, m_w_gate_up, m_w_down, m_norm_final, v_norm_mix, v_w_in, v_sg_ln_g, v_sg_ln_b, v_sg_w, v_sg_b, v_cv_w, v_cv_b, v_cv_ln_g, v_cv_ln_b, v_attn_sinks, v_sc_w, v_w_branch, v_w_out, v_norm_ffn, v_w_gate_up, v_w_down, v_norm_final):
    names = ("norm_mix", "w_in", "sg_ln_g", "sg_ln_b", "sg_w", "sg_b", "cv_w", "cv_b", "cv_ln_g", "cv_ln_b",
             "attn_sinks", "sc_w", "w_branch", "w_out", "norm_ffn", "w_gate_up", "w_down", "norm_final")
    w = dict(zip(names, (norm_mix, w_in, sg_ln_g, sg_ln_b, sg_w, sg_b, cv_w, cv_b, cv_ln_g, cv_ln_b, attn_sinks,
                         sc_w, w_branch, w_out, norm_ffn, w_gate_up, w_down, norm_final)))
    m = dict(zip(names, (m_norm_mix, m_w_in, m_sg_ln_g, m_sg_ln_b, m_sg_w, m_sg_b, m_cv_w, m_cv_b, m_cv_ln_g,
                         m_cv_ln_b, m_attn_sinks, m_sc_w, m_w_branch, m_w_out, m_norm_ffn, m_w_gate_up, m_w_down,
                         m_norm_final)))
    v = dict(zip(names, (v_norm_mix, v_w_in, v_sg_ln_g, v_sg_ln_b, v_sg_w, v_sg_b, v_cv_w, v_cv_b, v_cv_ln_g,
                         v_cv_ln_b, v_attn_sinks, v_sc_w, v_w_branch, v_w_out, v_norm_ffn, v_w_gate_up, v_w_down,
                         v_norm_final)))

    me = 4 * lax.axis_index("x") + 2 * lax.axis_index("y") + lax.axis_index("c")

    gathers, token = {}, None
    for l in range(DEPTH):
        for group in (_EARLY, _LATE):
            shards = [w[n][l].astype(BF16) for n in group]
            handle, token = exchange_start(shards, [False] * len(group), f"gather_start{l}_{group[0]}", after=token)
            gathers[(l, group)] = (handle, shards)

    def landed_weights(l, group, after):
        handle, shards = gathers[(l, group)]
        landed = exchange_wait(handle, after, f"gather_wait{l}_{group[0]}")
        return {n: _full_weight(n, _place_own(t, own, me)) for n, t, own in zip(group, landed, shards)}

    cos_t, sin_t = _rope_tables()

    def early_params(l, after):
        full = landed_weights(l, _EARLY, after)
        return dict(
            norm_mix=w["norm_mix"][l][None], norm_ffn=w["norm_ffn"][l][None],
            w_in_a=full["w_in"][:, :PROJ_A], w_in_g=full["w_in"][:, PROJ_A:],
            sg_ln_g=w["sg_ln_g"][l][None], sg_ln_b=w["sg_ln_b"][l][None], sg_w=w["sg_w"][l],
            sg_b=jnp.broadcast_to(w["sg_b"][l][:, :, None], (SG_GROUPS, SG_CHUNK, LANES)),
            cv_w=jnp.pad(full["cv_w"].astype(F32), ((0, HALO - CV_KERNEL), (0, 0))),
            cv_b=w["cv_b"][l][None], cv_ln_g=w["cv_ln_g"][l][None], cv_ln_b=w["cv_ln_b"][l][None],
            sinks=jnp.broadcast_to(w["attn_sinks"][l][:, None, None], (N_Q_HEADS, 1, LANES)),
            sc_w=jnp.pad(full["sc_w"].astype(F32), ((0, SUBLANES - SC_KERNEL), (0, 0))),
            cos=cos_t, sin=sin_t)

    params, saved = [None] * DEPTH, [None] * DEPTH
    h = x[0]
    after = token
    for l in range(DEPTH):
        h, saved[l], params[l] = _layer_fwd(
            l, h, early_params(l, after), lambda behind, l=l: landed_weights(l, _LATE, behind))
        after = h
    loss_row, dh, d_norm_final = loss_head(h, w["norm_final"][None], loss_target[0], "loss_head")

    sent = {}

    def emitter(l):
        def emit(group, grads_of):
            send = [_to_blocks(n, grads_of[n].astype(BF16)) for n in grads_of]
            handle, tok = exchange_start(send, [True] * len(send), f"grads_start{l}{group}")
            sent[(l, group)] = (handle, send, tuple(grads_of))
            return tok
        return emit

    grads = [None] * DEPTH
    dh, grads[1] = _layer_bwd(1, dh, params[1], saved[1], emitter(1))
    token = emitter(1)("c", {n: grads[1][n] for n in _EARLY})
    dh, grads[0] = _layer_bwd(0, dh, params[0], saved[0], emitter(0), after=token)
    grad_x = dh[None]

    stacked = {n: jnp.stack([grads[l][n] for l in range(DEPTH)]) for n in _SMALL if n != "norm_final"}
    for n in ("norm_mix", "norm_ffn", "sg_ln_g", "sg_ln_b", "cv_b", "cv_ln_g", "cv_ln_b"):
        stacked[n] = stacked[n][:, 0]
    stacked["norm_final"] = d_norm_final[0]
    small_like = [w[n] for n in _SMALL]
    small_part = _pack([stacked[n] for n in _SMALL])
    send_last = [_to_blocks(n, grads[0][n].astype(BF16)) for n in _EARLY]
    handle_last, token = exchange_start(send_last + [small_part], [True] * len(_EARLY) + [False], "grads_start0c")

    def received(l, group, after):
        handle, send, group_names = sent[(l, group)]
        landed = exchange_wait(handle, after, f"grads_wait{l}{group}")
        return {n: _place_own(t, lax.dynamic_index_in_dim(s, me, 0, keepdims=False), me)
                for n, t, s in zip(group_names, landed, send)}

    out_g, out_d, out_m, out_v = {}, {}, {}, {}

    def update(n, by_layer):
        shape = w[n].shape
        cols = shape[-1]
        flat = (w[n].size // cols, cols)
        layer_flat = (N_DEV, flat[0] // DEPTH, cols)
        if layer_flat[1] % SUBLANES == 0:
            parts = [t.reshape(layer_flat) for t in by_layer]
        else:
            parts = [jnp.stack(by_layer, axis=1).reshape((N_DEV,) + flat)]
        res = adamw(parts, w[n].reshape(flat), m[n].reshape(flat), v[n].reshape(flat), "adamw_" + n)
        out_g[n], out_d[n], out_m[n], out_v[n] = (t.reshape(shape) for t in res)
        return res[0]

    behind = token
    for group in ("a", "b"):
        r1 = received(1, group, behind)
        r0 = received(0, group, next(iter(r1.values())))
        for n in r0:
            behind = update(n, [r0[n], r1[n]])
    r1 = received(1, "c", behind)
    landed = exchange_wait(handle_last, next(iter(r1.values())), "grads_wait0c")
    for n, t, s in zip(_EARLY, landed, send_last):
        update(n, [_place_own(t, lax.dynamic_index_in_dim(s, me, 0, keepdims=False), me), r1[n]])
    res = adamw([_place_own(landed[-1], small_part, me)], _pack(small_like), _pack([m[n] for n in _SMALL]),
                _pack([v[n] for n in _SMALL]), "adamw_small")
    for store, packed in zip((out_g, out_d, out_m, out_v), res):
        for n, t in zip(_SMALL, _unpack(packed, small_like)):
            store[n] = t

    loss = lax.psum(loss_row[0, 0], ("x", "y", "c"))
    return (loss, grad_x, *[out_g[n] for n in names], *[out_d[n] for n in names], *[out_m[n] for n in names],
            *[out_v[n] for n in names])
```

```python
import jax
import jax.numpy as jnp
from jax import lax
from jax.experimental import pallas as pl
from jax.experimental.pallas import tpu as pltpu

F32 = jnp.float32
BF16 = jnp.bfloat16

SEQ = 2048
D_MODEL = 1024
DEPTH = 2
SG_WIDTH = 512
SG_CHUNK = 128
SG_GROUPS = 4
CV_WIDTH = 512
CV_KERNEL = 31
HEAD_DIM = 64
N_Q_HEADS = 8
N_KV_HEADS = 2
Q_WIDTH = 512
KV_WIDTH = 128
WINDOW = 128
SC_WIDTH = 512
SC_KERNEL = 3
N_BRANCH = 4
D_FF = 2816
EPS = 1e-6
ROPE_THETA = 10000.0
PROJ_A = 4352
PROJ_WIDTH = 8448
N_DEV = 8

ADAM_LR = 0.001
ADAM_B1 = 0.9
ADAM_B2 = 0.999
ADAM_EPS = 1e-08
ADAM_WD = 0.01
ADAM_STEP = 10

LANES = 128
SUBLANES = 8
VMEM_LIMIT_BYTES = 48 * 1024 * 1024
HALO = 32
CONV_ROWS = 256
TOKEN_TILE = 256
NORM_TILE = 512
ROPE_TILE = 1024

_SQRT_HALF = 0.7071067811865476
_INV_SQRT_2PI = 0.3989422804014327


def _params(semantics=None):
    return pltpu.CompilerParams(dimension_semantics=semantics, vmem_limit_bytes=VMEM_LIMIT_BYTES)


def _divisor_tile(n, cap, unit):
    best = None
    for t in range(unit, min(n, cap) + 1, unit):
        if n % t == 0:
            best = t
    return best if best is not None else n


_DIMS = {"nn": (((1,), (0,)), ((), ())), "nt": (((1,), (1,)), ((), ())), "tn": (((0,), (0,)), ((), ()))}


def matmul(a, b, mode, out_dtype, name, add=None, tm_cap=512, tn_cap=512, after=None):
    if mode == "nn":
        (m, k), (_, n) = a.shape, b.shape
    elif mode == "nt":
        (m, k), (n, _) = a.shape, b.shape
    else:
        (k, m), (_, n) = a.shape, b.shape
    tm = _divisor_tile(m, tm_cap, LANES)
    tn = _divisor_tile(n, tn_cap, LANES)
    a_spec = pl.BlockSpec((k, tm), lambda i, j: (0, i)) if mode == "tn" else pl.BlockSpec((tm, k), lambda i, j: (i, 0))
    b_spec = pl.BlockSpec((tn, k), lambda i, j: (j, 0)) if mode == "nt" else pl.BlockSpec((k, tn), lambda i, j: (0, j))
    o_spec = pl.BlockSpec((tm, tn), lambda i, j: (i, j))
    dims = _DIMS[mode]

    def body(*refs):
        a_ref, b_ref = refs[0], refs[1]
        o_ref = refs[-1]
        acc = lax.dot_general(a_ref[...].astype(BF16), b_ref[...].astype(BF16), dims, preferred_element_type=F32)
        if add is not None:
            acc = acc + refs[2][...].astype(F32)
        o_ref[...] = acc.astype(out_dtype)

    operands = (a, b) + (() if add is None else (add,)) + (() if after is None else (after,))
    in_specs = [a_spec, b_spec] + ([o_spec] if add is not None else [])
    in_specs += [pl.BlockSpec(memory_space=pl.ANY)] if after is not None else []
    return pl.pallas_call(
        body, name=name, out_shape=jax.ShapeDtypeStruct((m, n), out_dtype), grid=(m // tm, n // tn),
        in_specs=in_specs, out_specs=o_spec, compiler_params=_params(("parallel", "parallel")))(*operands)


def _sigmoid(x):
    return 1.0 / (1.0 + jnp.exp(-x))


def _gelu(x):
    return 0.5 * x * (1.0 + lax.erf(x * _SQRT_HALF))


def _gelu_grad(x):
    return 0.5 * (1.0 + lax.erf(x * _SQRT_HALF)) + x * _INV_SQRT_2PI * jnp.exp(-0.5 * x * x)


def _rms_stats(x):
    r = lax.rsqrt(jnp.mean(x * x, axis=-1, keepdims=True) + EPS)
    return x * r, r


def _rms_bwd(dxn, xhat, r, g):
    h = dxn * g
    return r * (h - xhat * jnp.mean(h * xhat, axis=-1, keepdims=True))


def _ln_stats(x):
    mu = jnp.mean(x, axis=-1, keepdims=True)
    xc = x - mu
    rstd = lax.rsqrt(jnp.mean(xc * xc, axis=-1, keepdims=True) + EPS)
    return xc * rstd, rstd


def _ln_bwd(dy, xhat, rstd, g):
    dxhat = dy * g
    return rstd * (dxhat - jnp.mean(dxhat, axis=-1, keepdims=True)
                   - xhat * jnp.mean(dxhat * xhat, axis=-1, keepdims=True))


def _accumulate(ref, value, first):
    @pl.when(first)
    def _():
        ref[...] = value

    @pl.when(jnp.logical_not(first))
    def _():
        ref[...] += value


def _shift_rows(win, shift, n_out):
    n = win.shape[0]
    if shift % n == 0:
        return win[:n_out]
    return pltpu.roll(win, n - shift, axis=0)[:n_out]


def _row_spec(width):
    return pl.BlockSpec((1, width), lambda i: (0, 0))


def rmsnorm_fwd(x, g, name):
    s, d = x.shape

    def body(x_ref, g_ref, o_ref):
        xhat, _ = _rms_stats(x_ref[...])
        o_ref[...] = (xhat * g_ref[...]).astype(BF16)

    tile = pl.BlockSpec((NORM_TILE, d), lambda i: (i, 0))
    return pl.pallas_call(
        body, name=name, out_shape=jax.ShapeDtypeStruct((s, d), BF16), grid=(s // NORM_TILE,),
        in_specs=[tile, _row_spec(d)], out_specs=tile, compiler_params=_params(("parallel",)))(x, g)


def rmsnorm_bwd(x, g, dxn, dres, name):
    s, d = x.shape

    def body(x_ref, g_ref, dxn_ref, dres_ref, dx_ref, dg_ref):
        xhat, r = _rms_stats(x_ref[...])
        dxn_v = dxn_ref[...].astype(F32)
        dx_ref[...] = dres_ref[...] + _rms_bwd(dxn_v, xhat, r, g_ref[...])
        _accumulate(dg_ref, jnp.sum(dxn_v * xhat, axis=0, keepdims=True), pl.program_id(0) == 0)

    tile = pl.BlockSpec((NORM_TILE, d), lambda i: (i, 0))
    return pl.pallas_call(
        body, name=name, out_shape=(jax.ShapeDtypeStruct((s, d), F32), jax.ShapeDtypeStruct((1, d), F32)),
        grid=(s // NORM_TILE,), in_specs=[tile, _row_spec(d), tile, tile], out_specs=(tile, _row_spec(d)),
        compiler_params=_params(("arbitrary",)))(x, g, dxn, dres)


def loss_head(x, g, target, name):
    s, d = x.shape

    def body(x_ref, g_ref, t_ref, loss_ref, dx_ref, dg_ref):
        first = pl.program_id(0) == 0
        xhat, r = _rms_stats(x_ref[...])
        gv = g_ref[...]
        err = xhat * gv - t_ref[...]
        part = 0.5 * jnp.sum(jnp.sum(err * err, axis=-1, keepdims=True), axis=0, keepdims=True) / d
        _accumulate(loss_ref, jnp.broadcast_to(part, (1, LANES)), first)
        dy = err / d
        dx_ref[...] = _rms_bwd(dy, xhat, r, gv)
        _accumulate(dg_ref, jnp.sum(dy * xhat, axis=0, keepdims=True), first)

    tile = pl.BlockSpec((NORM_TILE, d), lambda i: (i, 0))
    return pl.pallas_call(
        body, name=name,
        out_shape=(jax.ShapeDtypeStruct((1, LANES), F32), jax.ShapeDtypeStruct((s, d), F32),
                   jax.ShapeDtypeStruct((1, d), F32)),
        grid=(s // NORM_TILE,), in_specs=[tile, _row_spec(d), tile],
        out_specs=(_row_spec(LANES), tile, _row_spec(d)), compiler_params=_params(("arbitrary",)))(x, g, target)


def _tril_mask():
    row = lax.broadcasted_iota(jnp.int32, (SG_CHUNK, SG_CHUNK), 0)
    col = lax.broadcasted_iota(jnp.int32, (SG_CHUNK, SG_CHUNK), 1)
    return row >= col


def _sg_specs():
    vec = _row_spec(SG_WIDTH)
    mat = pl.BlockSpec((SG_GROUPS, SG_CHUNK, SG_CHUNK), lambda i: (0, 0, 0))
    return vec, mat


def mixer_a_fwd(proj, ln_g, ln_b, w_s, b_s, name):
    s = proj.shape[0]
    chunks = TOKEN_TILE // SG_CHUNK

    def body(z_ref, lg_ref, lb_ref, w_ref, b_ref, o_ref):
        ge = _gelu(z_ref[...].astype(F32))
        u = ge[:, :SG_WIDTH]
        xhat, _ = _ln_stats(ge[:, SG_WIDTH:])
        vn = xhat * lg_ref[...] + lb_ref[...]
        tril = _tril_mask()
        for ci in range(chunks):
            rows = slice(ci * SG_CHUNK, (ci + 1) * SG_CHUNK)
            for g in range(SG_GROUPS):
                cols = slice(g * LANES, (g + 1) * LANES)
                wm = jnp.where(tril, w_ref[g], 0.0).astype(BF16)
                mixed = jnp.dot(wm, vn[rows, cols].astype(BF16), preferred_element_type=F32) + b_ref[g]
                o_ref[rows, cols] = (u[rows, cols] * mixed).astype(BF16)

    vec, mat = _sg_specs()
    return pl.pallas_call(
        body, name=name, out_shape=jax.ShapeDtypeStruct((s, SG_WIDTH), BF16), grid=(s // TOKEN_TILE,),
        in_specs=[pl.BlockSpec((TOKEN_TILE, 2 * SG_WIDTH), lambda i: (i, 0)), vec, vec, mat, mat],
        out_specs=pl.BlockSpec((TOKEN_TILE, SG_WIDTH), lambda i: (i, 0)),
        compiler_params=_params(("parallel",)))(proj, ln_g, ln_b, w_s, b_s)


def mixer_a_bwd(proj, dy, ln_g, ln_b, w_s, b_s, name):
    s = proj.shape[0]
    chunks = TOKEN_TILE // SG_CHUNK

    def body(z_ref, dy_ref, lg_ref, lb_ref, w_ref, b_ref, dz_ref, dlg_ref, dlb_ref, dw_ref, db_ref, du_scr, dvn_scr):
        first = pl.program_id(0) == 0

        @pl.when(first)
        def _():
            dw_ref[...] = jnp.zeros_like(dw_ref)
            db_ref[...] = jnp.zeros_like(db_ref)

        z = z_ref[...].astype(F32)
        ge = _gelu(z)
        u = ge[:, :SG_WIDTH]
        xhat, rstd = _ln_stats(ge[:, SG_WIDTH:])
        lg = lg_ref[...]
        vn = xhat * lg + lb_ref[...]
        dyv = dy_ref[...].astype(F32)
        tril = _tril_mask()
        for ci in range(chunks):
            rows = slice(ci * SG_CHUNK, (ci + 1) * SG_CHUNK)
            for g in range(SG_GROUPS):
                cols = slice(g * LANES, (g + 1) * LANES)
                wm = jnp.where(tril, w_ref[g], 0.0).astype(BF16)
                vg = vn[rows, cols].astype(BF16)
                mixed = jnp.dot(wm, vg, preferred_element_type=F32) + b_ref[g]
                dyb = dyv[rows, cols]
                du_scr[rows, cols] = dyb * mixed
                dmix = dyb * u[rows, cols]
                db_ref[g] += jnp.broadcast_to(jnp.sum(dmix, axis=1, keepdims=True), (SG_CHUNK, LANES))
                dmb = dmix.astype(BF16)
                dwg = lax.dot_general(dmb, vg, _DIMS["nt"], preferred_element_type=F32)
                dw_ref[g] += jnp.where(tril, dwg, 0.0)
                dvn_scr[rows, cols] = lax.dot_general(wm, dmb, _DIMS["tn"], preferred_element_type=F32)
        dvn = dvn_scr[...]
        _accumulate(dlg_ref, jnp.sum(dvn * xhat, axis=0, keepdims=True), first)
        _accumulate(dlb_ref, jnp.sum(dvn, axis=0, keepdims=True), first)
        dvv = _ln_bwd(dvn, xhat, rstd, lg)
        gg = _gelu_grad(z)
        dz_ref[:, :SG_WIDTH] = (du_scr[...] * gg[:, :SG_WIDTH]).astype(BF16)
        dz_ref[:, SG_WIDTH:] = (dvv * gg[:, SG_WIDTH:]).astype(BF16)

    vec, mat = _sg_specs()
    mat_shape = jax.ShapeDtypeStruct((SG_GROUPS, SG_CHUNK, SG_CHUNK), F32)
    vec_shape = jax.ShapeDtypeStruct((1, SG_WIDTH), F32)
    return pl.pallas_call(
        body, name=name,
        out_shape=(jax.ShapeDtypeStruct((s, 2 * SG_WIDTH), BF16), vec_shape, vec_shape, mat_shape, mat_shape),
        grid=(s // TOKEN_TILE,),
        in_specs=[pl.BlockSpec((TOKEN_TILE, 2 * SG_WIDTH), lambda i: (i, 0)),
                  pl.BlockSpec((TOKEN_TILE, SG_WIDTH), lambda i: (i, 0)), vec, vec, mat, mat],
        out_specs=(pl.BlockSpec((TOKEN_TILE, 2 * SG_WIDTH), lambda i: (i, 0)), vec, vec, mat, mat),
        scratch_shapes=[pltpu.VMEM((TOKEN_TILE, SG_WIDTH), F32), pltpu.VMEM((TOKEN_TILE, SG_WIDTH), F32)],
        compiler_params=_params(("arbitrary",)))(proj, dy, ln_g, ln_b, w_s, b_s)


_B_A_BLOCK = 1024 // LANES
_B_G_BLOCK = 1536 // LANES
_CH_TILES = CV_WIDTH // LANES


def _col_spec(s, first_block):
    return pl.BlockSpec((s, LANES), lambda j: (0, first_block + j))


def conv_b_fwd(proj, w_pad, bias, name):
    s = proj.shape[0]

    def body(a_ref, g_ref, w_ref, b_ref, c_ref, upad):
        upad[0:HALO, :] = jnp.zeros((HALO, LANES), F32)
        upad[HALO:, :] = a_ref[...].astype(F32) * _sigmoid(g_ref[...].astype(F32))
        w = w_ref[...]
        bv = b_ref[...]

        def block(bi, carry):
            start = pl.multiple_of(bi * CONV_ROWS, CONV_ROWS)
            win = upad[pl.ds(start, CONV_ROWS + HALO), :]
            acc = jnp.zeros((CONV_ROWS, LANES), F32)
            for k in range(CV_KERNEL):
                acc = acc + w[k:k + 1, :] * _shift_rows(win, HALO - (CV_KERNEL - 1) + k, CONV_ROWS)
            c_ref[pl.ds(start, CONV_ROWS), :] = acc + bv
            return carry

        lax.fori_loop(0, s // CONV_ROWS, block, 0)

    return pl.pallas_call(
        body, name=name, out_shape=jax.ShapeDtypeStruct((s, CV_WIDTH), F32), grid=(_CH_TILES,),
        in_specs=[_col_spec(s, _B_A_BLOCK), _col_spec(s, _B_G_BLOCK), _col_spec(HALO, 0), _col_spec(1, 0)],
        out_specs=_col_spec(s, 0), scratch_shapes=[pltpu.VMEM((s + HALO, LANES), F32)],
        compiler_params=_params(("parallel",)))(proj, proj, w_pad, bias)


def conv_b_bwd(proj, w_pad, dc, name):
    s = proj.shape[0]

    def body(a_ref, g_ref, w_ref, dc_ref, da_ref, dg_ref, dw_ref, db_ref, upad, dpad, dw_scr):
        upad[0:HALO, :] = jnp.zeros((HALO, LANES), F32)
        upad[HALO:, :] = a_ref[...].astype(F32) * _sigmoid(g_ref[...].astype(F32))
        dcv = dc_ref[...]
        dpad[0:s, :] = dcv
        dpad[s:, :] = jnp.zeros((HALO, LANES), F32)
        db_ref[...] = jnp.sum(dcv, axis=0, keepdims=True)
        dw_scr[...] = jnp.zeros((HALO, LANES), F32)
        w = w_ref[...]

        def block(bi, carry):
            start = pl.multiple_of(bi * CONV_ROWS, CONV_ROWS)
            uwin = upad[pl.ds(start, CONV_ROWS + HALO), :]
            dwin = dpad[pl.ds(start, CONV_ROWS + HALO), :]
            dcb = dwin[:CONV_ROWS]
            du = jnp.zeros((CONV_ROWS, LANES), F32)
            for k in range(CV_KERNEL):
                du = du + w[k:k + 1, :] * _shift_rows(dwin, CV_KERNEL - 1 - k, CONV_ROWS)
                ush = _shift_rows(uwin, HALO - (CV_KERNEL - 1) + k, CONV_ROWS)
                dw_scr[k:k + 1, :] += jnp.sum(dcb * ush, axis=0, keepdims=True)
            av = a_ref[pl.ds(start, CONV_ROWS), :].astype(F32)
            sg = _sigmoid(g_ref[pl.ds(start, CONV_ROWS), :].astype(F32))
            da_ref[pl.ds(start, CONV_ROWS), :] = (du * sg).astype(BF16)
            dg_ref[pl.ds(start, CONV_ROWS), :] = (du * av * sg * (1.0 - sg)).astype(BF16)
            return carry

        lax.fori_loop(0, s // CONV_ROWS, block, 0)
        dw_ref[...] = dw_scr[...]

    act = jax.ShapeDtypeStruct((s, CV_WIDTH), BF16)
    return pl.pallas_call(
        body, name=name,
        out_shape=(act, act, jax.ShapeDtypeStruct((HALO, CV_WIDTH), F32), jax.ShapeDtypeStruct((1, CV_WIDTH), F32)),
        grid=(_CH_TILES,),
        in_specs=[_col_spec(s, _B_A_BLOCK), _col_spec(s, _B_G_BLOCK), _col_spec(HALO, 0), _col_spec(s, 0)],
        out_specs=(_col_spec(s, 0), _col_spec(s, 0), _col_spec(HALO, 0), _col_spec(1, 0)),
        scratch_shapes=[pltpu.VMEM((s + HALO, LANES), F32), pltpu.VMEM((s + HALO, LANES), F32),
                        pltpu.VMEM((HALO, LANES), F32)],
        compiler_params=_params(("parallel",)))(proj, proj, w_pad, dc)


def ln_silu_fwd(c, ln_g, ln_b, name):
    s, d = c.shape

    def body(c_ref, g_ref, b_ref, o_ref):
        xhat, _ = _ln_stats(c_ref[...])
        cn = xhat * g_ref[...] + b_ref[...]
        o_ref[...] = (cn * _sigmoid(cn)).astype(BF16)

    tile = pl.BlockSpec((NORM_TILE, d), lambda i: (i, 0))
    return pl.pallas_call(
        body, name=name, out_shape=jax.ShapeDtypeStruct((s, d), BF16), grid=(s // NORM_TILE,),
        in_specs=[tile, _row_spec(d), _row_spec(d)], out_specs=tile,
        compiler_params=_params(("parallel",)))(c, ln_g, ln_b)


def ln_silu_bwd(c, dy, ln_g, ln_b, name):
    s, d = c.shape

    def body(c_ref, dy_ref, g_ref, b_ref, dc_ref, dg_ref, db_ref):
        first = pl.program_id(0) == 0
        xhat, rstd = _ln_stats(c_ref[...])
        gv = g_ref[...]
        cn = xhat * gv + b_ref[...]
        sg = _sigmoid(cn)
        dcn = dy_ref[...].astype(F32) * sg * (1.0 + cn * (1.0 - sg))
        _accumulate(dg_ref, jnp.sum(dcn * xhat, axis=0, keepdims=True), first)
        _accumulate(db_ref, jnp.sum(dcn, axis=0, keepdims=True), first)
        dc_ref[...] = _ln_bwd(dcn, xhat, rstd, gv)

    tile = pl.BlockSpec((NORM_TILE, d), lambda i: (i, 0))
    vec_shape = jax.ShapeDtypeStruct((1, d), F32)
    return pl.pallas_call(
        body, name=name, out_shape=(jax.ShapeDtypeStruct((s, d), F32), vec_shape, vec_shape),
        grid=(s // NORM_TILE,), in_specs=[tile, tile, _row_spec(d), _row_spec(d)],
        out_specs=(tile, _row_spec(d), _row_spec(d)), compiler_params=_params(("arbitrary",)))(c, dy, ln_g, ln_b)


_D_BLOCK = 2816 // LANES


def _conv3(win, w):
    acc = jnp.zeros((CONV_ROWS, LANES), F32)
    for k in range(SC_KERNEL):
        acc = acc + w[k:k + 1, :] * _shift_rows(win, HALO - (SC_KERNEL - 1) + k, CONV_ROWS)
    return acc


def conv_d_fwd(proj, w_pad, name):
    s = proj.shape[0]

    def body(bg_ref, cg_ref, h_ref, w_ref, o_ref, ppad):
        ppad[0:HALO, :] = jnp.zeros((HALO, LANES), F32)
        ppad[HALO:, :] = cg_ref[...].astype(F32) * h_ref[...].astype(F32)
        w = w_ref[...]

        def block(bi, carry):
            start = pl.multiple_of(bi * CONV_ROWS, CONV_ROWS)
            cv = _conv3(ppad[pl.ds(start, CONV_ROWS + HALO), :], w)
            o_ref[pl.ds(start, CONV_ROWS), :] = (bg_ref[pl.ds(start, CONV_ROWS), :].astype(F32) * cv).astype(BF16)
            return carry

        lax.fori_loop(0, s // CONV_ROWS, block, 0)

    return pl.pallas_call(
        body, name=name, out_shape=jax.ShapeDtypeStruct((s, SC_WIDTH), BF16), grid=(_CH_TILES,),
        in_specs=[_col_spec(s, _D_BLOCK), _col_spec(s, _D_BLOCK + _CH_TILES), _col_spec(s, _D_BLOCK + 2 * _CH_TILES),
                  _col_spec(SUBLANES, 0)],
        out_specs=_col_spec(s, 0), scratch_shapes=[pltpu.VMEM((s + HALO, LANES), F32)],
        compiler_params=_params(("parallel",)))(proj, proj, proj, w_pad)


def conv_d_bwd(proj, w_pad, dy, name):
    s = proj.shape[0]

    def body(bg_ref, cg_ref, h_ref, w_ref, dy_ref, dbg_ref, dcg_ref, dh_ref, dw_ref, ppad, dpad, dw_scr):
        ppad[0:HALO, :] = jnp.zeros((HALO, LANES), F32)
        ppad[HALO:, :] = cg_ref[...].astype(F32) * h_ref[...].astype(F32)
        dpad[0:s, :] = dy_ref[...].astype(F32) * bg_ref[...].astype(F32)
        dpad[s:, :] = jnp.zeros((HALO, LANES), F32)
        dw_scr[...] = jnp.zeros((SUBLANES, LANES), F32)
        w = w_ref[...]

        def block(bi, carry):
            start = pl.multiple_of(bi * CONV_ROWS, CONV_ROWS)
            rows = pl.ds(start, CONV_ROWS)
            pwin = ppad[pl.ds(start, CONV_ROWS + HALO), :]
            dwin = dpad[pl.ds(start, CONV_ROWS + HALO), :]
            dcvb = dwin[:CONV_ROWS]
            dbg_ref[rows, :] = (dy_ref[rows, :].astype(F32) * _conv3(pwin, w)).astype(BF16)
            dp = jnp.zeros((CONV_ROWS, LANES), F32)
            for k in range(SC_KERNEL):
                dp = dp + w[k:k + 1, :] * _shift_rows(dwin, SC_KERNEL - 1 - k, CONV_ROWS)
                psh = _shift_rows(pwin, HALO - (SC_KERNEL - 1) + k, CONV_ROWS)
                dw_scr[k:k + 1, :] += jnp.sum(dcvb * psh, axis=0, keepdims=True)
            dcg_ref[rows, :] = (dp * h_ref[rows, :].astype(F32)).astype(BF16)
            dh_ref[rows, :] = (dp * cg_ref[rows, :].astype(F32)).astype(BF16)
            return carry

        lax.fori_loop(0, s // CONV_ROWS, block, 0)
        dw_ref[...] = dw_scr[...]

    act = jax.ShapeDtypeStruct((s, SC_WIDTH), BF16)
    return pl.pallas_call(
        body, name=name, out_shape=(act, act, act, jax.ShapeDtypeStruct((SUBLANES, SC_WIDTH), F32)),
        grid=(_CH_TILES,),
        in_specs=[_col_spec(s, _D_BLOCK), _col_spec(s, _D_BLOCK + _CH_TILES), _col_spec(s, _D_BLOCK + 2 * _CH_TILES),
                  _col_spec(SUBLANES, 0), _col_spec(s, 0)],
        out_specs=(_col_spec(s, 0), _col_spec(s, 0), _col_spec(s, 0), _col_spec(SUBLANES, 0)),
        scratch_shapes=[pltpu.VMEM((s + HALO, LANES), F32), pltpu.VMEM((s + HALO, LANES), F32),
                        pltpu.VMEM((SUBLANES, LANES), F32)],
        compiler_params=_params(("parallel",)))(proj, proj, proj, w_pad, dy)


_QK_BLOCK = 2048 // LANES
_QK_BLOCKS = (Q_WIDTH + KV_WIDTH) // LANES


def _swap_halves(t):
    lane = lax.broadcasted_iota(jnp.int32, t.shape, 1)
    low = (lane % HEAD_DIM) < (HEAD_DIM // 2)
    return jnp.where(low, pltpu.roll(t, LANES - HEAD_DIM // 2, axis=1), pltpu.roll(t, HEAD_DIM // 2, axis=1))


def rope_fwd(proj, cos_t, sin_t, name):
    s = proj.shape[0]

    def body(t_ref, c_ref, s_ref, o_ref):
        t = t_ref[...].astype(F32)
        o_ref[...] = (t * c_ref[...] + _swap_halves(t) * s_ref[...]).astype(BF16)

    tr = min(ROPE_TILE, s)
    tab = pl.BlockSpec((tr, LANES), lambda i, j: (i, 0))
    return pl.pallas_call(
        body, name=name, out_shape=jax.ShapeDtypeStruct((s, Q_WIDTH + KV_WIDTH), BF16),
        grid=(s // tr, _QK_BLOCKS),
        in_specs=[pl.BlockSpec((tr, LANES), lambda i, j: (i, _QK_BLOCK + j)), tab, tab],
        out_specs=pl.BlockSpec((tr, LANES), lambda i, j: (i, j)),
        compiler_params=_params(("parallel", "parallel")))(proj, cos_t, sin_t)


def rope_bwd(d_cur, d_prev, cos_t, sin_t, name):
    s, w = d_cur.shape

    def body(a_ref, b_ref, c_ref, s_ref, o_ref):
        d = a_ref[...] + b_ref[...]
        o_ref[...] = (d * c_ref[...] + _swap_halves(d) * s_ref[...]).astype(BF16)

    tr = min(ROPE_TILE, s)
    tab = pl.BlockSpec((tr, LANES), lambda i, j: (i, 0))
    blk = pl.BlockSpec((tr, LANES), lambda i, j: (i, j))
    return pl.pallas_call(
        body, name=name, out_shape=jax.ShapeDtypeStruct((s, w), BF16), grid=(s // tr, w // LANES),
        in_specs=[blk, blk, tab, tab], out_specs=blk,
        compiler_params=_params(("parallel", "parallel")))(d_cur, d_prev, cos_t, sin_t)


_GROUP = N_Q_HEADS // N_KV_HEADS
_NEG = -1e30


def _attn_specs():
    q_spec = pl.BlockSpec((_GROUP, WINDOW, HEAD_DIM), lambda h, n: (h, n, 0))
    cur = pl.BlockSpec((1, WINDOW, HEAD_DIM), lambda h, n: (h, n, 0))
    prev = pl.BlockSpec((1, WINDOW, HEAD_DIM), lambda h, n: (h, jnp.maximum(n - 1, 0), 0))
    sink = pl.BlockSpec((_GROUP, 1, LANES), lambda h, n: (h, 0, 0))
    return q_spec, cur, prev, sink


def _attn_valid(n):
    qi = lax.broadcasted_iota(jnp.int32, (WINDOW, 2 * WINDOW), 0)
    kj = lax.broadcasted_iota(jnp.int32, (WINDOW, 2 * WINDOW), 1)
    delta = qi + WINDOW - kj
    return (delta >= 0) & (delta < WINDOW) & ((kj >= WINDOW) | (n > 0))


def _attn_probs(q, kcat, valid, sink_row):
    sc = lax.dot_general(q, kcat, _DIMS["nt"], preferred_element_type=F32) * (HEAD_DIM ** -0.5)
    sc = jnp.where(valid, sc, _NEG)
    sink = jnp.max(sink_row, axis=-1, keepdims=True)
    m = jnp.maximum(jnp.max(sc, axis=-1, keepdims=True), sink)
    p = jnp.where(valid, jnp.exp(sc - m), 0.0)
    es = jnp.exp(sink - m)
    inv = 1.0 / (jnp.sum(p, axis=-1, keepdims=True) + es)
    return p * inv, es * inv


def attention_fwd(qh, kh, vh, sinks_b, name):
    s = qh.shape[1]

    def body(q_ref, kc_ref, kp_ref, vc_ref, vp_ref, sk_ref, o_ref):
        valid = _attn_valid(pl.program_id(1))
        kcat = jnp.concatenate([kp_ref[0], kc_ref[0]], axis=0)
        vcat = jnp.concatenate([vp_ref[0], vc_ref[0]], axis=0)
        for g in range(_GROUP):
            probs, _ = _attn_probs(q_ref[g], kcat, valid, sk_ref[g])
            o_ref[g] = jnp.dot(probs.astype(BF16), vcat, preferred_element_type=F32).astype(BF16)

    q_spec, cur, prev, sink = _attn_specs()
    return pl.pallas_call(
        body, name=name, out_shape=jax.ShapeDtypeStruct(qh.shape, BF16), grid=(N_KV_HEADS, s // WINDOW),
        in_specs=[q_spec, cur, prev, cur, prev, sink], out_specs=q_spec,
        compiler_params=_params(("parallel", "parallel")))(qh, kh, kh, vh, vh, sinks_b)


def attention_bwd(qh, kh, vh, sinks_b, doh, name):
    s = qh.shape[1]

    def body(q_ref, kc_ref, kp_ref, vc_ref, vp_ref, sk_ref, do_ref, dq_ref, dkc_ref, dkp_ref, dvc_ref, dvp_ref, ds_ref):
        n = pl.program_id(1)
        valid = _attn_valid(n)
        kcat = jnp.concatenate([kp_ref[0], kc_ref[0]], axis=0)
        vcat = jnp.concatenate([vp_ref[0], vc_ref[0]], axis=0)
        dk = jnp.zeros((2 * WINDOW, HEAD_DIM), F32)
        dv = jnp.zeros((2 * WINDOW, HEAD_DIM), F32)
        for g in range(_GROUP):
            q = q_ref[g]
            do = do_ref[g]
            probs, ps = _attn_probs(q, kcat, valid, sk_ref[g])
            dprobs = lax.dot_general(do, vcat, _DIMS["nt"], preferred_element_type=F32)
            dv = dv + lax.dot_general(probs.astype(BF16), do, _DIMS["tn"], preferred_element_type=F32)
            rs = jnp.sum(probs * dprobs, axis=-1, keepdims=True)
            dsb = (probs * (dprobs - rs) * (HEAD_DIM ** -0.5)).astype(BF16)
            dq_ref[g] = jnp.dot(dsb, kcat, preferred_element_type=F32)
            dk = dk + lax.dot_general(dsb, q, _DIMS["tn"], preferred_element_type=F32)
            dsink = jnp.broadcast_to(-jnp.sum(ps * rs, axis=0, keepdims=True), (1, LANES))

            @pl.when(n == 0)
            def _():
                ds_ref[g] = dsink

            @pl.when(n > 0)
            def _():
                ds_ref[g] += dsink

        dkp_ref[0] = dk[:WINDOW]
        dkc_ref[0] = dk[WINDOW:]
        dvp_ref[0] = dv[:WINDOW]
        dvc_ref[0] = dv[WINDOW:]

    q_spec, cur, prev, sink = _attn_specs()
    kv_shape = jax.ShapeDtypeStruct(kh.shape, F32)
    return pl.pallas_call(
        body, name=name,
        out_shape=(jax.ShapeDtypeStruct(qh.shape, F32), kv_shape, kv_shape, kv_shape, kv_shape,
                   jax.ShapeDtypeStruct(sinks_b.shape, F32)),
        grid=(N_KV_HEADS, s // WINDOW), in_specs=[q_spec, cur, prev, cur, prev, sink, q_spec],
        out_specs=(q_spec, cur, cur, cur, cur, sink),
        compiler_params=_params(("parallel", "arbitrary")))(qh, kh, kh, vh, vh, sinks_b, doh)


def _to_heads(t, heads):
    return t.reshape(t.shape[0], heads, HEAD_DIM).transpose(1, 0, 2)


def _from_heads(t):
    return t.transpose(1, 0, 2).reshape(t.shape[1], t.shape[0] * HEAD_DIM)


def _shift_window(t):
    return jnp.concatenate([t[:, WINDOW:], jnp.zeros_like(t[:, :WINDOW])], axis=1)


def merge_fwd(zg, branches, name):
    s = zg.shape[0]

    def body(zg_ref, b0, b1, b2, b3, o_ref):
        acc = jnp.zeros((TOKEN_TILE, D_MODEL), F32)
        for n, b_ref in enumerate((b0, b1, b2, b3)):
            gate = _sigmoid(zg_ref[:, n * D_MODEL:(n + 1) * D_MODEL].astype(F32))
            acc = acc + gate * b_ref[...].astype(F32)
        o_ref[...] = acc.astype(BF16)

    tile = pl.BlockSpec((TOKEN_TILE, D_MODEL), lambda i: (i, 0))
    wide = pl.BlockSpec((TOKEN_TILE, N_BRANCH * D_MODEL), lambda i: (i, 0))
    return pl.pallas_call(
        body, name=name, out_shape=jax.ShapeDtypeStruct((s, D_MODEL), BF16), grid=(s // TOKEN_TILE,),
        in_specs=[wide, tile, tile, tile, tile], out_specs=tile,
        compiler_params=_params(("parallel",)))(zg, *branches)


def merge_bwd(zg, branches, dm, name):
    s = zg.shape[0]

    def body(zg_ref, b0, b1, b2, b3, dm_ref, dzg_ref, d0, d1, d2, d3):
        dmv = dm_ref[...].astype(F32)
        for n, (b_ref, d_ref) in enumerate(((b0, d0), (b1, d1), (b2, d2), (b3, d3))):
            cols = slice(n * D_MODEL, (n + 1) * D_MODEL)
            gate = _sigmoid(zg_ref[:, cols].astype(F32))
            d_ref[...] = (gate * dmv).astype(BF16)
            dzg_ref[:, cols] = (dmv * b_ref[...].astype(F32) * gate * (1.0 - gate)).astype(BF16)

    tile = pl.BlockSpec((TOKEN_TILE, D_MODEL), lambda i: (i, 0))
    wide = pl.BlockSpec((TOKEN_TILE, N_BRANCH * D_MODEL), lambda i: (i, 0))
    act = jax.ShapeDtypeStruct((s, D_MODEL), BF16)
    return pl.pallas_call(
        body, name=name, out_shape=(jax.ShapeDtypeStruct((s, N_BRANCH * D_MODEL), BF16), act, act, act, act),
        grid=(s // TOKEN_TILE,), in_specs=[wide, tile, tile, tile, tile, tile],
        out_specs=(wide, tile, tile, tile, tile), compiler_params=_params(("parallel",)))(zg, *branches, dm)


def swiglu_fwd(gu, name):
    s = gu.shape[0]

    def body(g_ref, u_ref, o_ref):
        gate = g_ref[...].astype(F32)
        o_ref[...] = (gate * _sigmoid(gate) * u_ref[...].astype(F32)).astype(BF16)

    return pl.pallas_call(
        body, name=name, out_shape=jax.ShapeDtypeStruct((s, D_FF), BF16), grid=(s // TOKEN_TILE,),
        in_specs=[pl.BlockSpec((TOKEN_TILE, D_FF), lambda i: (i, 0)), pl.BlockSpec((TOKEN_TILE, D_FF), lambda i: (i, 1))],
        out_specs=pl.BlockSpec((TOKEN_TILE, D_FF), lambda i: (i, 0)), compiler_params=_params(("parallel",)))(gu, gu)


def swiglu_bwd(gu, dact, name):
    s = gu.shape[0]

    def body(g_ref, u_ref, da_ref, o_ref):
        gate = g_ref[...].astype(F32)
        sg = _sigmoid(gate)
        da = da_ref[...].astype(F32)
        o_ref[:, :D_FF] = (da * u_ref[...].astype(F32) * sg * (1.0 + gate * (1.0 - sg))).astype(BF16)
        o_ref[:, D_FF:] = (da * gate * sg).astype(BF16)

    half = pl.BlockSpec((TOKEN_TILE, D_FF), lambda i: (i, 0))
    return pl.pallas_call(
        body, name=name, out_shape=jax.ShapeDtypeStruct((s, 2 * D_FF), BF16), grid=(s // TOKEN_TILE,),
        in_specs=[half, pl.BlockSpec((TOKEN_TILE, D_FF), lambda i: (i, 1)), half],
        out_specs=pl.BlockSpec((TOKEN_TILE, 2 * D_FF), lambda i: (i, 0)),
        compiler_params=_params(("parallel",)))(gu, gu, dact)


ADAMW_BLOCK_BYTES = 1 << 20


def adamw(parts, w, m, v, name):
    n_parts, r, c = w.shape
    tr = _divisor_tile(r, max(SUBLANES, ADAMW_BLOCK_BYTES // (4 * c)), SUBLANES)
    tiles = r // tr

    def part_spec(j):
        return pl.BlockSpec((N_DEV, tr, c), lambda i: (0, jnp.clip(i - j * tiles, 0, tiles - 1), 0))

    def body(*refs):
        p_refs = refs[:n_parts]
        w_ref, m_ref, v_ref, g_ref, d_ref, nm_ref, nv_ref = refs[n_parts:]
        which = pl.program_id(0) // tiles
        g = None
        for j, p_ref in enumerate(p_refs):
            gj = p_ref[0].astype(F32)
            for i in range(1, N_DEV):
                gj = gj + p_ref[i].astype(F32)
            g = gj if g is None else jnp.where(which == j, gj, g)
        nm = ADAM_B1 * m_ref[0] + (1.0 - ADAM_B1) * g
        nv = ADAM_B2 * v_ref[0] + (1.0 - ADAM_B2) * (g * g)
        m_hat = nm / (1.0 - ADAM_B1 ** ADAM_STEP)
        v_hat = nv / (1.0 - ADAM_B2 ** ADAM_STEP)
        g_ref[0] = g
        d_ref[0] = -ADAM_LR * (m_hat / (jnp.sqrt(v_hat) + ADAM_EPS) + ADAM_WD * w_ref[0])
        nm_ref[0] = nm
        nv_ref[0] = nv

    tile = pl.BlockSpec((1, tr, c), lambda i: (i // tiles, i % tiles, 0))
    shape = jax.ShapeDtypeStruct(w.shape, F32)
    return pl.pallas_call(
        body, name=name, out_shape=(shape, shape, shape, shape), grid=(n_parts * tiles,),
        in_specs=[part_spec(j) for j in range(n_parts)] + [tile, tile, tile],
        out_specs=(tile, tile, tile, tile), compiler_params=_params(("parallel",)))(*parts, w, m, v)


_RELATIONS = [(a, b, e) for a in (0, 1) for b in (0, 1) for e in (0, 1)][1:]


_HBM_SPEC = pl.BlockSpec(memory_space=pltpu.HBM)
_SEM_SPEC = pl.BlockSpec(memory_space=pltpu.SEMAPHORE)
_ANY_SPEC = pl.BlockSpec(memory_space=pl.ANY)
_DATAFLOW = pltpu.SideEffectType.DATAFLOW_SIDE_EFFECTING


def _remote_copies(ins, lands, send_sems, recv_sems, scatter):
    x, y, c = lax.axis_index("x"), lax.axis_index("y"), lax.axis_index("c")
    me = 4 * x + 2 * y + c
    copies = []
    for t in range(len(ins)):
        for k, (a, b, e) in enumerate(_RELATIONS):
            px, py, pc = (x + a) % 2, (y + b) % 2, (c + e) % 2
            src = ins[t].at[4 * px + 2 * py + pc] if scatter[t] else ins[t]
            copies.append(pltpu.make_async_remote_copy(
                src_ref=src, dst_ref=lands[t].at[me], send_sem=send_sems.at[t * len(_RELATIONS) + k],
                recv_sem=recv_sems.at[t * len(_RELATIONS) + k],
                device_id=(px, py, pc), device_id_type=pl.DeviceIdType.MESH))
    return copies


def exchange_start(arrays, scatter, name, after=None):
    n = len(arrays)
    n_rel = len(_RELATIONS)
    land_shapes = [a.shape if scatter[t] else (N_DEV,) + a.shape for t, a in enumerate(arrays)]

    def body(*refs):
        ins, lands = refs[:n], refs[n:2 * n]
        send_sems, recv_sems = refs[-2 * n - 3], refs[-2 * n - 2]
        token = refs[-1]
        for cp in _remote_copies(ins, lands, send_sems, recv_sems, scatter):
            cp.start()
        token[...] = jnp.zeros_like(token)

    sems = pltpu.SemaphoreType.DMA((n * n_rel,))
    out_shape = ((sems, sems) + tuple(pltpu.HBM(a.shape, a.dtype) for a in arrays)
                 + tuple(pltpu.HBM(s, a.dtype) for s, a in zip(land_shapes, arrays))
                 + (jax.ShapeDtypeStruct((SUBLANES, LANES), F32),))
    operands = [pltpu.with_memory_space_constraint(a, pltpu.HBM) for a in arrays]
    operands += [pltpu.with_memory_space_constraint(lax.empty(s, a.dtype), pltpu.HBM) for s, a in zip(land_shapes, arrays)]
    in_specs = [_HBM_SPEC] * (2 * n)
    if after is not None:
        operands.append(after)
        in_specs.append(_ANY_SPEC)
    res = pl.pallas_call(
        body, name=name, out_shape=out_shape, in_specs=in_specs,
        out_specs=(_SEM_SPEC, _SEM_SPEC) + (_HBM_SPEC,) * (2 * n) + (pl.BlockSpec(memory_space=pltpu.VMEM),),
        input_output_aliases={i: 2 + i for i in range(2 * n)},
        compiler_params=pltpu.CompilerParams(has_side_effects=_DATAFLOW))(*operands)
    handle = (res[0], res[1], res[2:2 + n], res[2 + n:2 + 2 * n], tuple(scatter))
    return handle, res[-1]


def exchange_wait(handle, after, name):
    send_sems, recv_sems, sources, lands, scatter = handle
    n = len(sources)

    def body(*refs):
        ins, lzs = refs[:n], refs[n:2 * n]
        send_ref, recv_ref = refs[2 * n], refs[2 * n + 1]
        for cp in _remote_copies(ins, lzs, send_ref, recv_ref, scatter):
            cp.wait_send()
            cp.wait_recv()

    out_shape = (tuple(pltpu.HBM(a.shape, a.dtype) for a in sources) + tuple(pltpu.HBM(a.shape, a.dtype) for a in lands))
    res = pl.pallas_call(
        body, name=name, out_shape=out_shape, in_specs=[_HBM_SPEC] * (2 * n) + [_SEM_SPEC, _SEM_SPEC, _ANY_SPEC],
        out_specs=(_HBM_SPEC,) * (2 * n), input_output_aliases={i: i for i in range(2 * n)},
        compiler_params=pltpu.CompilerParams(has_side_effects=_DATAFLOW))(*sources, *lands, send_sems, recv_sems, after)
    return res[n:]


def _place_own(landed, own, me):
    return lax.dynamic_update_index_in_dim(landed, own, me, 0)


_SMALL = ("norm_mix", "sg_ln_g", "sg_ln_b", "sg_w", "sg_b", "cv_b", "cv_ln_g", "cv_ln_b", "attn_sinks", "norm_ffn",
          "norm_final")
_PACK_UNIT = SUBLANES * LANES


def _pack(tensors):
    rows = []
    for t in tensors:
        flat = t.reshape(-1)
        pad = (-flat.shape[0]) % _PACK_UNIT
        rows.append(jnp.pad(flat, (0, pad)).reshape(-1, LANES))
    return jnp.concatenate(rows, axis=0)


def _unpack(packed, like):
    out, row = [], 0
    for t in like:
        size = 1
        for d in t.shape:
            size *= d
        rows = -(-size // _PACK_UNIT) * SUBLANES
        out.append(packed[row:row + rows].reshape(-1)[:size].reshape(t.shape))
        row += rows
    return out


def _layer_fwd(l, x, p, late_params):
    tag = f"l{l}_"
    xn = rmsnorm_fwd(x, p["norm_mix"], tag + "norm_mix")
    proj = matmul(xn, p["w_in_a"], "nn", BF16, tag + "proj_a", tm_cap=1024, tn_cap=2176)
    zg = matmul(xn, p["w_in_g"], "nn", BF16, tag + "proj_g", tm_cap=1024, tn_cap=2048)
    y_a = mixer_a_fwd(proj, p["sg_ln_g"], p["sg_ln_b"], p["sg_w"], p["sg_b"], tag + "mix_a")
    conv = conv_b_fwd(proj, p["cv_w"], p["cv_b"], tag + "conv_b")
    y_b = ln_silu_fwd(conv, p["cv_ln_g"], p["cv_ln_b"], tag + "ln_silu")
    qk = rope_fwd(proj, p["cos"], p["sin"], tag + "rope")
    qh = _to_heads(qk[:, :Q_WIDTH], N_Q_HEADS)
    kh = _to_heads(qk[:, Q_WIDTH:], N_KV_HEADS)
    vh = _to_heads(proj[:, 2688:2816], N_KV_HEADS)
    oh = attention_fwd(qh, kh, vh, p["sinks"], tag + "attn")
    y_c = _from_heads(oh)
    y_d = conv_d_fwd(proj, p["sc_w"], tag + "conv_d")
    ys = (y_a, y_b, y_c, y_d)
    p = {**p, **late_params(y_d)}
    branches = tuple(matmul(ys[n], p["w_branch"][n], "nn", BF16, tag + f"branch{n}", tm_cap=1024, tn_cap=1024)
                     for n in range(N_BRANCH))
    merged = merge_fwd(zg, branches, tag + "merge")
    x_mid = matmul(merged, p["w_out"], "nn", F32, tag + "out", add=x, tm_cap=1024, tn_cap=1024)
    hn = rmsnorm_fwd(x_mid, p["norm_ffn"], tag + "norm_ffn")
    gu = matmul(hn, p["w_gate_up"], "nn", BF16, tag + "gate_up", tm_cap=512, tn_cap=2816)
    act = swiglu_fwd(gu, tag + "swiglu")
    x_out = matmul(act, p["w_down"], "nn", F32, tag + "down", add=x_mid, tm_cap=512, tn_cap=1024)
    saved = dict(x=x, xn=xn, proj=proj, zg=zg, conv=conv, qh=qh, kh=kh, vh=vh, ys=ys, branches=branches,
                 merged=merged, x_mid=x_mid, hn=hn, gu=gu, act=act)
    return x_out, saved, p


def _layer_bwd(l, dx_out, p, sv, emit, after=None):
    tag = f"l{l}_b_"
    g = {}
    dact = matmul(dx_out, p["w_down"], "nt", BF16, tag + "dact", after=after, tm_cap=512, tn_cap=2816)
    dw_down = matmul(sv["act"], dx_out, "tn", BF16, tag + "dw_down", tm_cap=1408, tn_cap=512)
    dgu = swiglu_bwd(sv["gu"], dact, tag + "swiglu")
    dhn = matmul(dgu, p["w_gate_up"], "nt", BF16, tag + "dhn", tm_cap=512, tn_cap=512)
    dw_gate_up = matmul(sv["hn"], dgu, "tn", BF16, tag + "dw_gate_up", tm_cap=1024, tn_cap=1408)
    token = emit("a", {"w_gate_up": dw_gate_up, "w_down": dw_down})
    dx_mid, g["norm_ffn"] = rmsnorm_bwd(sv["x_mid"], p["norm_ffn"], dhn, dx_out, tag + "norm_ffn")
    dmerged = matmul(dx_mid, p["w_out"], "nt", BF16, tag + "dmerged", after=token, tm_cap=1024, tn_cap=1024)
    dw_out = matmul(sv["merged"], dx_mid, "tn", BF16, tag + "dw_out", tm_cap=1024, tn_cap=512)
    dzg, *dbranches = merge_bwd(sv["zg"], sv["branches"], dmerged, tag + "merge")
    dys = [matmul(dbranches[n], p["w_branch"][n], "nt", BF16, tag + f"dy{n}", tm_cap=1024, tn_cap=512)
           for n in range(N_BRANCH)]
    dw_branch = jnp.stack(
        [matmul(sv["ys"][n], dbranches[n], "tn", BF16, tag + f"dw_branch{n}", tm_cap=512, tn_cap=1024)
         for n in range(N_BRANCH)])
    token = emit("b", {"w_branch": dw_branch, "w_out": dw_out})
    proj = sv["proj"]
    dz_a, g["sg_ln_g"], g["sg_ln_b"], g["sg_w"], dsb = mixer_a_bwd(
        proj, dys[0], p["sg_ln_g"], p["sg_ln_b"], p["sg_w"], p["sg_b"], tag + "mix_a")
    g["sg_b"] = dsb[:, :, 0]
    dconv, g["cv_ln_g"], g["cv_ln_b"] = ln_silu_bwd(sv["conv"], dys[1], p["cv_ln_g"], p["cv_ln_b"], tag + "ln_silu")
    da, dgate, dcw, g["cv_b"] = conv_b_bwd(proj, p["cv_w"], dconv, tag + "conv_b")
    g["cv_w"] = dcw[:CV_KERNEL]
    doh = _to_heads(dys[2], N_Q_HEADS)
    dqh, dkc, dkp, dvc, dvp, dsk = attention_bwd(sv["qh"], sv["kh"], sv["vh"], p["sinks"], doh, tag + "attn")
    g["attn_sinks"] = dsk[:, 0, 0]
    dqk_cur = jnp.concatenate([_from_heads(dqh), _from_heads(dkc)], axis=1)
    dqk_prev = jnp.concatenate([jnp.zeros((SEQ, Q_WIDTH), F32), _from_heads(_shift_window(dkp))], axis=1)
    dqk = rope_bwd(dqk_cur, dqk_prev, p["cos"], -p["sin"], tag + "rope")
    dv = (_from_heads(dvc) + _from_heads(_shift_window(dvp))).astype(BF16)
    dbg, dcg, dh, dsw = conv_d_bwd(proj, p["sc_w"], dys[3], tag + "conv_d")
    g["sc_w"] = dsw[:SC_KERNEL]
    dproj = jnp.concatenate([dz_a, da, dgate, dqk, dv, dbg, dcg, dh], axis=1)
    dxn = matmul(dproj, p["w_in_a"], "nt", F32, tag + "dxn_a", after=token, tm_cap=512, tn_cap=512)
    dxn = matmul(dzg, p["w_in_g"], "nt", F32, tag + "dxn_g", add=dxn, tm_cap=512, tn_cap=512)
    g["w_in"] = jnp.concatenate(
        [matmul(sv["xn"], dproj, "tn", BF16, tag + "dw_in_a", tm_cap=512, tn_cap=2176),
         matmul(sv["xn"], dzg, "tn", BF16, tag + "dw_in_g", tm_cap=1024, tn_cap=1024)], axis=1)
    dx_in, g["norm_mix"] = rmsnorm_bwd(sv["x"], p["norm_mix"], dxn, dx_mid, tag + "norm_mix")
    return dx_in, g


_EARLY = ("w_in", "cv_w", "sc_w")
_LATE = ("w_branch", "w_out", "w_gate_up", "w_down")


def _full_weight(name, t):
    if name in ("w_out", "w_down"):
        return t.reshape(-1, t.shape[-1])
    if name == "w_branch":
        return t.transpose(1, 2, 0, 3).reshape(N_BRANCH, SG_WIDTH, D_MODEL)
    return t.transpose(1, 0, 2).reshape(t.shape[1], -1)


def _to_blocks(name, full):
    if name in ("w_out", "w_down"):
        return full.reshape(N_DEV, -1, full.shape[-1])
    if name == "w_branch":
        return full.reshape(N_BRANCH, SG_WIDTH, N_DEV, -1).transpose(2, 0, 1, 3)
    return full.reshape(full.shape[0], N_DEV, -1).transpose(1, 0, 2)


def _rope_tables():
    pos = jnp.arange(SEQ, dtype=F32)
    inv_freq = 1.0 / (ROPE_THETA ** (jnp.arange(0, HEAD_DIM, 2, dtype=F32) / HEAD_DIM))
    ang = pos[:, None] * inv_freq[None, :]
    cos, sin = jnp.cos(ang), jnp.sin(ang)
    reps = LANES // HEAD_DIM
    return jnp.tile(jnp.concatenate([cos, cos], axis=1), (1, reps)), jnp.tile(jnp.concatenate([-sin, sin], axis=1), (1, reps))


def kernel(x, norm_mix, w_in, sg_ln_g, sg_ln_b, sg_w, sg_b, cv_w, cv_b, cv_ln_g, cv_ln_b, attn_sinks, sc_w, w_branch, w_out, norm_ffn, w_gate_up, w_down, norm_final, loss_target, m_norm_mix, m_w_in, m_sg_ln_g, m_sg_ln_b, m_sg_w, m_sg_b, m_cv_w, m_cv_b, m_cv_ln_g, m_cv_ln_b, m_attn_sinks, m_sc_w, m_w_branch, m_w_out, m_norm_ffn, m_w_gate_up, m_w_down, m_norm_final, v_norm_mix, v_w_in, v_sg_ln_g, v_sg_ln_b, v_sg_w, v_sg_b, v_cv_w, v_cv_b, v_cv_ln_g, v_cv_ln_b, v_attn_sinks, v_sc_w, v_w_branch, v_w_out, v_norm_ffn, v_w_gate_up, v_w_down, v_norm_final):
    names = ("norm_mix", "w_in", "sg_ln_g", "sg_ln_b", "sg_w", "sg_b", "cv_w", "cv_b", "cv_ln_g", "cv_ln_b",
             "attn_sinks", "sc_w", "w_branch", "w_out", "norm_ffn", "w_gate_up", "w_down", "norm_final")
    w = dict(zip(names, (norm_mix, w_in, sg_ln_g, sg_ln_b, sg_w, sg_b, cv_w, cv_b, cv_ln_g, cv_ln_b, attn_sinks,
                         sc_w, w_branch, w_out, norm_ffn, w_gate_up, w_down, norm_final)))
    m = dict(zip(names, (m_norm_mix, m_w_in, m_sg_ln_g, m_sg_ln_b, m_sg_w, m_sg_b, m_cv_w, m_cv_b, m_cv_ln_g,
                         m_cv_ln_b, m_attn_sinks, m_sc_w, m_w_branch, m_w_out, m_norm_ffn, m_w_gate_up, m_w_down,
                         m_norm_final)))
    v = dict(zip(names, (v_norm_mix, v_w_in, v_sg_ln_g, v_sg_ln_b, v_sg_w, v_sg_b, v_cv_w, v_cv_b, v_cv_ln_g,
                         v_cv_ln_b, v_attn_sinks, v_sc_w, v_w_branch, v_w_out, v_norm_ffn, v_w_gate_up, v_w_down,
                         v_norm_final)))

    me = 4 * lax.axis_index("x") + 2 * lax.axis_index("y") + lax.axis_index("c")

    gathers, token = {}, None
    for l in range(DEPTH):
        for group in (_EARLY, _LATE):
            shards = [w[n][l].astype(BF16) for n in group]
            handle, token = exchange_start(shards, [False] * len(group), f"gather_start{l}_{group[0]}", after=token)
            gathers[(l, group)] = (handle, shards)

    def landed_weights(l, group, after):
        handle, shards = gathers[(l, group)]
        landed = exchange_wait(handle, after, f"gather_wait{l}_{group[0]}")
        return {n: _full_weight(n, _place_own(t, own, me)) for n, t, own in zip(group, landed, shards)}

    cos_t, sin_t = _rope_tables()

    def early_params(l, after):
        full = landed_weights(l, _EARLY, after)
        return dict(
            norm_mix=w["norm_mix"][l][None], norm_ffn=w["norm_ffn"][l][None],
            w_in_a=full["w_in"][:, :PROJ_A], w_in_g=full["w_in"][:, PROJ_A:],
            sg_ln_g=w["sg_ln_g"][l][None], sg_ln_b=w["sg_ln_b"][l][None], sg_w=w["sg_w"][l],
            sg_b=jnp.broadcast_to(w["sg_b"][l][:, :, None], (SG_GROUPS, SG_CHUNK, LANES)),
            cv_w=jnp.pad(full["cv_w"].astype(F32), ((0, HALO - CV_KERNEL), (0, 0))),
            cv_b=w["cv_b"][l][None], cv_ln_g=w["cv_ln_g"][l][None], cv_ln_b=w["cv_ln_b"][l][None],
            sinks=jnp.broadcast_to(w["attn_sinks"][l][:, None, None], (N_Q_HEADS, 1, LANES)),
            sc_w=jnp.pad(full["sc_w"].astype(F32), ((0, SUBLANES - SC_KERNEL), (0, 0))),
            cos=cos_t, sin=sin_t)

    params, saved = [None] * DEPTH, [None] * DEPTH
    h = x[0]
    after = token
    for l in range(DEPTH):
        h, saved[l], params[l] = _layer_fwd(
            l, h, early_params(l, after), lambda behind, l=l: landed_weights(l, _LATE, behind))
        after = h
    loss_row, dh, d_norm_final = loss_head(h, w["norm_final"][None], loss_target[0], "loss_head")

    sent = {}

    def emitter(l):
        def emit(group, grads_of):
            send = [_to_blocks(n, grads_of[n].astype(BF16)) for n in grads_of]
            handle, tok = exchange_start(send, [True] * len(send), f"grads_start{l}{group}")
            sent[(l, group)] = (handle, send, tuple(grads_of))
            return tok
        return emit

    grads = [None] * DEPTH
    dh, grads[1] = _layer_bwd(1, dh, params[1], saved[1], emitter(1))
    token = emitter(1)("c", {n: grads[1][n] for n in _EARLY})
    dh, grads[0] = _layer_bwd(0, dh, params[0], saved[0], emitter(0), after=token)
    grad_x = dh[None]

    stacked = {n: jnp.stack([grads[l][n] for l in range(DEPTH)]) for n in _SMALL if n != "norm_final"}
    for n in ("norm_mix", "norm_ffn", "sg_ln_g", "sg_ln_b", "cv_b", "cv_ln_g", "cv_ln_b"):
        stacked[n] = stacked[n][:, 0]
    stacked["norm_final"] = d_norm_final[0]
    no_state = jnp.zeros((1,), F32)
    small_like = [w[n] for n in _SMALL] + [no_state]
    small_part = _pack([stacked[n] for n in _SMALL] + [loss_row[0, :1]])
    send_last = [_to_blocks(n, grads[0][n].astype(BF16)) for n in _EARLY]
    handle_last, token = exchange_start(send_last + [small_part], [True] * len(_EARLY) + [False], "grads_start0c")

    def received(l, group, after):
        handle, send, group_names = sent[(l, group)]
        landed = exchange_wait(handle, after, f"grads_wait{l}{group}")
        return {n: _place_own(t, lax.dynamic_index_in_dim(s, me, 0, keepdims=False), me)
                for n, t, s in zip(group_names, landed, send)}

    out_g, out_d, out_m, out_v = {}, {}, {}, {}

    def update(n, by_layer):
        shape = w[n].shape
        view = (DEPTH, w[n].size // (DEPTH * shape[-1]), shape[-1])
        parts = [t.reshape((N_DEV,) + view[1:]) for t in by_layer]
        res = adamw(parts, w[n].reshape(view), m[n].reshape(view), v[n].reshape(view), "adamw_" + n)
        out_g[n], out_d[n], out_m[n], out_v[n] = (t.reshape(shape) for t in res)
        return res[0]

    behind = token
    for group in ("a", "b"):
        r1 = received(1, group, behind)
        r0 = received(0, group, next(iter(r1.values())))
        for n in r0:
            behind = update(n, [r0[n], r1[n]])
    r1 = received(1, "c", behind)
    landed = exchange_wait(handle_last, next(iter(r1.values())), "grads_wait0c")
    for n, t, s in zip(_EARLY, landed, send_last):
        update(n, [_place_own(t, lax.dynamic_index_in_dim(s, me, 0, keepdims=False), me), r1[n]])
    res = adamw([_place_own(landed[-1], small_part, me)], _pack(small_like)[None],
                _pack([m[n] for n in _SMALL] + [no_state])[None], _pack([v[n] for n in _SMALL] + [no_state])[None],
                "adamw_small")
    for store, packed in zip((out_g, out_d, out_m, out_v), res):
        for n, t in zip(_SMALL + ("loss",), _unpack(packed[0], small_like)):
            store[n] = t

    loss = out_g["loss"][0]
    return (loss, grad_x, *[out_g[n] for n in names], *[out_d[n] for n in names], *[out_m[n] for n in names],
            *[out_v[n] for n in names])
```

```python
import jax
import jax.numpy as jnp
from jax import lax
from jax.experimental import pallas as pl
from jax.experimental.pallas import tpu as pltpu

F32 = jnp.float32
BF16 = jnp.bfloat16

SEQ = 2048
D_MODEL = 1024
DEPTH = 2
SG_WIDTH = 512
SG_CHUNK = 128
SG_GROUPS = 4
CV_WIDTH = 512
CV_KERNEL = 31
HEAD_DIM = 64
N_Q_HEADS = 8
N_KV_HEADS = 2
Q_WIDTH = 512
KV_WIDTH = 128
WINDOW = 128
SC_WIDTH = 512
SC_KERNEL = 3
N_BRANCH = 4
D_FF = 2816
EPS = 1e-6
ROPE_THETA = 10000.0
PROJ_A = 4352
PROJ_WIDTH = 8448
N_DEV = 8

ADAM_LR = 0.001
ADAM_B1 = 0.9
ADAM_B2 = 0.999
ADAM_EPS = 1e-08
ADAM_WD = 0.01
ADAM_STEP = 10

LANES = 128
SUBLANES = 8
VMEM_LIMIT_BYTES = 48 * 1024 * 1024
HALO = 32
CONV_ROWS = 256
TOKEN_TILE = 256
NORM_TILE = 512
ROPE_TILE = 1024

_SQRT_HALF = 0.7071067811865476
_INV_SQRT_2PI = 0.3989422804014327


def _params(semantics=None):
    return pltpu.CompilerParams(dimension_semantics=semantics, vmem_limit_bytes=VMEM_LIMIT_BYTES)


def _divisor_tile(n, cap, unit):
    best = None
    for t in range(unit, min(n, cap) + 1, unit):
        if n % t == 0:
            best = t
    return best if best is not None else n


_DIMS = {"nn": (((1,), (0,)), ((), ())), "nt": (((1,), (1,)), ((), ())), "tn": (((0,), (0,)), ((), ()))}


def matmul(a, b, mode, out_dtype, name, add=None, tm_cap=512, tn_cap=512, after=None):
    if mode == "nn":
        (m, k), (_, n) = a.shape, b.shape
    elif mode == "nt":
        (m, k), (n, _) = a.shape, b.shape
    else:
        (k, m), (_, n) = a.shape, b.shape
    tm = _divisor_tile(m, tm_cap, LANES)
    tn = _divisor_tile(n, tn_cap, LANES)
    a_spec = pl.BlockSpec((k, tm), lambda i, j: (0, i)) if mode == "tn" else pl.BlockSpec((tm, k), lambda i, j: (i, 0))
    b_spec = pl.BlockSpec((tn, k), lambda i, j: (j, 0)) if mode == "nt" else pl.BlockSpec((k, tn), lambda i, j: (0, j))
    o_spec = pl.BlockSpec((tm, tn), lambda i, j: (i, j))
    dims = _DIMS[mode]

    def body(*refs):
        a_ref, b_ref = refs[0], refs[1]
        o_ref = refs[-1]
        acc = lax.dot_general(a_ref[...].astype(BF16), b_ref[...].astype(BF16), dims, preferred_element_type=F32)
        if add is not None:
            acc = acc + refs[2][...].astype(F32)
        o_ref[...] = acc.astype(out_dtype)

    operands = (a, b) + (() if add is None else (add,)) + (() if after is None else (after,))
    in_specs = [a_spec, b_spec] + ([o_spec] if add is not None else [])
    in_specs += [pl.BlockSpec(memory_space=pl.ANY)] if after is not None else []
    return pl.pallas_call(
        body, name=name, out_shape=jax.ShapeDtypeStruct((m, n), out_dtype), grid=(m // tm, n // tn),
        in_specs=in_specs, out_specs=o_spec, compiler_params=_params(("parallel", "parallel")))(*operands)


def _sigmoid(x):
    return 1.0 / (1.0 + jnp.exp(-x))


def _gelu(x):
    return 0.5 * x * (1.0 + lax.erf(x * _SQRT_HALF))


def _gelu_grad(x):
    return 0.5 * (1.0 + lax.erf(x * _SQRT_HALF)) + x * _INV_SQRT_2PI * jnp.exp(-0.5 * x * x)


def _rms_stats(x):
    r = lax.rsqrt(jnp.mean(x * x, axis=-1, keepdims=True) + EPS)
    return x * r, r


def _rms_bwd(dxn, xhat, r, g):
    h = dxn * g
    return r * (h - xhat * jnp.mean(h * xhat, axis=-1, keepdims=True))


def _ln_stats(x):
    mu = jnp.mean(x, axis=-1, keepdims=True)
    xc = x - mu
    rstd = lax.rsqrt(jnp.mean(xc * xc, axis=-1, keepdims=True) + EPS)
    return xc * rstd, rstd


def _ln_bwd(dy, xhat, rstd, g):
    dxhat = dy * g
    return rstd * (dxhat - jnp.mean(dxhat, axis=-1, keepdims=True)
                   - xhat * jnp.mean(dxhat * xhat, axis=-1, keepdims=True))


def _accumulate(ref, value, first):
    @pl.when(first)
    def _():
        ref[...] = value

    @pl.when(jnp.logical_not(first))
    def _():
        ref[...] += value


def _shift_rows(win, shift, n_out):
    n = win.shape[0]
    if shift % n == 0:
        return win[:n_out]
    return pltpu.roll(win, n - shift, axis=0)[:n_out]


def _row_spec(width):
    return pl.BlockSpec((1, width), lambda i: (0, 0))


def rmsnorm_fwd(x, g, name):
    s, d = x.shape

    def body(x_ref, g_ref, o_ref):
        xhat, _ = _rms_stats(x_ref[...])
        o_ref[...] = (xhat * g_ref[...]).astype(BF16)

    tile = pl.BlockSpec((NORM_TILE, d), lambda i: (i, 0))
    return pl.pallas_call(
        body, name=name, out_shape=jax.ShapeDtypeStruct((s, d), BF16), grid=(s // NORM_TILE,),
        in_specs=[tile, _row_spec(d)], out_specs=tile, compiler_params=_params(("parallel",)))(x, g)


def rmsnorm_bwd(x, g, dxn, dres, name):
    s, d = x.shape

    def body(x_ref, g_ref, dxn_ref, dres_ref, dx_ref, dg_ref):
        xhat, r = _rms_stats(x_ref[...])
        dxn_v = dxn_ref[...].astype(F32)
        dx_ref[...] = dres_ref[...] + _rms_bwd(dxn_v, xhat, r, g_ref[...])
        _accumulate(dg_ref, jnp.sum(dxn_v * xhat, axis=0, keepdims=True), pl.program_id(0) == 0)

    tile = pl.BlockSpec((NORM_TILE, d), lambda i: (i, 0))
    return pl.pallas_call(
        body, name=name, out_shape=(jax.ShapeDtypeStruct((s, d), F32), jax.ShapeDtypeStruct((1, d), F32)),
        grid=(s // NORM_TILE,), in_specs=[tile, _row_spec(d), tile, tile], out_specs=(tile, _row_spec(d)),
        compiler_params=_params(("arbitrary",)))(x, g, dxn, dres)


def loss_head(x, g, target, name):
    s, d = x.shape

    def body(x_ref, g_ref, t_ref, loss_ref, dx_ref, dg_ref):
        first = pl.program_id(0) == 0
        xhat, r = _rms_stats(x_ref[...])
        gv = g_ref[...]
        err = xhat * gv - t_ref[...]
        part = 0.5 * jnp.sum(jnp.sum(err * err, axis=-1, keepdims=True), axis=0, keepdims=True) / d
        _accumulate(loss_ref, jnp.broadcast_to(part, (1, LANES)), first)
        dy = err / d
        dx_ref[...] = _rms_bwd(dy, xhat, r, gv)
        _accumulate(dg_ref, jnp.sum(dy * xhat, axis=0, keepdims=True), first)

    tile = pl.BlockSpec((NORM_TILE, d), lambda i: (i, 0))
    return pl.pallas_call(
        body, name=name,
        out_shape=(jax.ShapeDtypeStruct((1, LANES), F32), jax.ShapeDtypeStruct((s, d), F32),
                   jax.ShapeDtypeStruct((1, d), F32)),
        grid=(s // NORM_TILE,), in_specs=[tile, _row_spec(d), tile],
        out_specs=(_row_spec(LANES), tile, _row_spec(d)), compiler_params=_params(("arbitrary",)))(x, g, target)


def _tril_mask():
    row = lax.broadcasted_iota(jnp.int32, (SG_CHUNK, SG_CHUNK), 0)
    col = lax.broadcasted_iota(jnp.int32, (SG_CHUNK, SG_CHUNK), 1)
    return row >= col


def _sg_specs():
    vec = _row_spec(SG_WIDTH)
    mat = pl.BlockSpec((SG_GROUPS, SG_CHUNK, SG_CHUNK), lambda i: (0, 0, 0))
    return vec, mat


def mixer_a_fwd(proj, ln_g, ln_b, w_s, b_s, name):
    s = proj.shape[0]
    chunks = TOKEN_TILE // SG_CHUNK

    def body(z_ref, lg_ref, lb_ref, w_ref, b_ref, o_ref):
        ge = _gelu(z_ref[...].astype(F32))
        u = ge[:, :SG_WIDTH]
        xhat, _ = _ln_stats(ge[:, SG_WIDTH:])
        vn = xhat * lg_ref[...] + lb_ref[...]
        tril = _tril_mask()
        for ci in range(chunks):
            rows = slice(ci * SG_CHUNK, (ci + 1) * SG_CHUNK)
            for g in range(SG_GROUPS):
                cols = slice(g * LANES, (g + 1) * LANES)
                wm = jnp.where(tril, w_ref[g], 0.0).astype(BF16)
                mixed = jnp.dot(wm, vn[rows, cols].astype(BF16), preferred_element_type=F32) + b_ref[g]
                o_ref[rows, cols] = (u[rows, cols] * mixed).astype(BF16)

    vec, mat = _sg_specs()
    return pl.pallas_call(
        body, name=name, out_shape=jax.ShapeDtypeStruct((s, SG_WIDTH), BF16), grid=(s // TOKEN_TILE,),
        in_specs=[pl.BlockSpec((TOKEN_TILE, 2 * SG_WIDTH), lambda i: (i, 0)), vec, vec, mat, mat],
        out_specs=pl.BlockSpec((TOKEN_TILE, SG_WIDTH), lambda i: (i, 0)),
        compiler_params=_params(("parallel",)))(proj, ln_g, ln_b, w_s, b_s)


def mixer_a_bwd(proj, dy, ln_g, ln_b, w_s, b_s, name):
    s = proj.shape[0]
    chunks = TOKEN_TILE // SG_CHUNK

    def body(z_ref, dy_ref, lg_ref, lb_ref, w_ref, b_ref, dz_ref, dlg_ref, dlb_ref, dw_ref, db_ref, du_scr, dvn_scr):
        first = pl.program_id(0) == 0

        @pl.when(first)
        def _():
            dw_ref[...] = jnp.zeros_like(dw_ref)
            db_ref[...] = jnp.zeros_like(db_ref)

        z = z_ref[...].astype(F32)
        ge = _gelu(z)
        u = ge[:, :SG_WIDTH]
        xhat, rstd = _ln_stats(ge[:, SG_WIDTH:])
        lg = lg_ref[...]
        vn = xhat * lg + lb_ref[...]
        dyv = dy_ref[...].astype(F32)
        tril = _tril_mask()
        for ci in range(chunks):
            rows = slice(ci * SG_CHUNK, (ci + 1) * SG_CHUNK)
            for g in range(SG_GROUPS):
                cols = slice(g * LANES, (g + 1) * LANES)
                wm = jnp.where(tril, w_ref[g], 0.0).astype(BF16)
                vg = vn[rows, cols].astype(BF16)
                mixed = jnp.dot(wm, vg, preferred_element_type=F32) + b_ref[g]
                dyb = dyv[rows, cols]
                du_scr[rows, cols] = dyb * mixed
                dmix = dyb * u[rows, cols]
                db_ref[g] += jnp.broadcast_to(jnp.sum(dmix, axis=1, keepdims=True), (SG_CHUNK, LANES))
                dmb = dmix.astype(BF16)
                dwg = lax.dot_general(dmb, vg, _DIMS["nt"], preferred_element_type=F32)
                dw_ref[g] += jnp.where(tril, dwg, 0.0)
                dvn_scr[rows, cols] = lax.dot_general(wm, dmb, _DIMS["tn"], preferred_element_type=F32)
        dvn = dvn_scr[...]
        _accumulate(dlg_ref, jnp.sum(dvn * xhat, axis=0, keepdims=True), first)
        _accumulate(dlb_ref, jnp.sum(dvn, axis=0, keepdims=True), first)
        dvv = _ln_bwd(dvn, xhat, rstd, lg)
        gg = _gelu_grad(z)
        dz_ref[:, :SG_WIDTH] = (du_scr[...] * gg[:, :SG_WIDTH]).astype(BF16)
        dz_ref[:, SG_WIDTH:] = (dvv * gg[:, SG_WIDTH:]).astype(BF16)

    vec, mat = _sg_specs()
    mat_shape = jax.ShapeDtypeStruct((SG_GROUPS, SG_CHUNK, SG_CHUNK), F32)
    vec_shape = jax.ShapeDtypeStruct((1, SG_WIDTH), F32)
    return pl.pallas_call(
        body, name=name,
        out_shape=(jax.ShapeDtypeStruct((s, 2 * SG_WIDTH), BF16), vec_shape, vec_shape, mat_shape, mat_shape),
        grid=(s // TOKEN_TILE,),
        in_specs=[pl.BlockSpec((TOKEN_TILE, 2 * SG_WIDTH), lambda i: (i, 0)),
                  pl.BlockSpec((TOKEN_TILE, SG_WIDTH), lambda i: (i, 0)), vec, vec, mat, mat],
        out_specs=(pl.BlockSpec((TOKEN_TILE, 2 * SG_WIDTH), lambda i: (i, 0)), vec, vec, mat, mat),
        scratch_shapes=[pltpu.VMEM((TOKEN_TILE, SG_WIDTH), F32), pltpu.VMEM((TOKEN_TILE, SG_WIDTH), F32)],
        compiler_params=_params(("arbitrary",)))(proj, dy, ln_g, ln_b, w_s, b_s)


_B_A_BLOCK = 1024 // LANES
_B_G_BLOCK = 1536 // LANES
_CH_TILES = CV_WIDTH // LANES


def _col_spec(s, first_block):
    return pl.BlockSpec((s, LANES), lambda j: (0, first_block + j))


def conv_b_fwd(proj, w_pad, bias, name):
    s = proj.shape[0]

    def body(a_ref, g_ref, w_ref, b_ref, c_ref, upad):
        upad[0:HALO, :] = jnp.zeros((HALO, LANES), F32)
        upad[HALO:, :] = a_ref[...].astype(F32) * _sigmoid(g_ref[...].astype(F32))
        w = w_ref[...]
        bv = b_ref[...]

        def block(bi, carry):
            start = pl.multiple_of(bi * CONV_ROWS, CONV_ROWS)
            win = upad[pl.ds(start, CONV_ROWS + HALO), :]
            acc = jnp.zeros((CONV_ROWS, LANES), F32)
            for k in range(CV_KERNEL):
                acc = acc + w[k:k + 1, :] * _shift_rows(win, HALO - (CV_KERNEL - 1) + k, CONV_ROWS)
            c_ref[pl.ds(start, CONV_ROWS), :] = acc + bv
            return carry

        lax.fori_loop(0, s // CONV_ROWS, block, 0)

    return pl.pallas_call(
        body, name=name, out_shape=jax.ShapeDtypeStruct((s, CV_WIDTH), F32), grid=(_CH_TILES,),
        in_specs=[_col_spec(s, _B_A_BLOCK), _col_spec(s, _B_G_BLOCK), _col_spec(HALO, 0), _col_spec(1, 0)],
        out_specs=_col_spec(s, 0), scratch_shapes=[pltpu.VMEM((s + HALO, LANES), F32)],
        compiler_params=_params(("parallel",)))(proj, proj, w_pad, bias)


def conv_b_bwd(proj, w_pad, dc, name):
    s = proj.shape[0]

    def body(a_ref, g_ref, w_ref, dc_ref, da_ref, dg_ref, dw_ref, db_ref, upad, dpad, dw_scr):
        upad[0:HALO, :] = jnp.zeros((HALO, LANES), F32)
        upad[HALO:, :] = a_ref[...].astype(F32) * _sigmoid(g_ref[...].astype(F32))
        dcv = dc_ref[...]
        dpad[0:s, :] = dcv
        dpad[s:, :] = jnp.zeros((HALO, LANES), F32)
        db_ref[...] = jnp.sum(dcv, axis=0, keepdims=True)
        dw_scr[...] = jnp.zeros((HALO, LANES), F32)
        w = w_ref[...]

        def block(bi, carry):
            start = pl.multiple_of(bi * CONV_ROWS, CONV_ROWS)
            uwin = upad[pl.ds(start, CONV_ROWS + HALO), :]
            dwin = dpad[pl.ds(start, CONV_ROWS + HALO), :]
            dcb = dwin[:CONV_ROWS]
            du = jnp.zeros((CONV_ROWS, LANES), F32)
            for k in range(CV_KERNEL):
                du = du + w[k:k + 1, :] * _shift_rows(dwin, CV_KERNEL - 1 - k, CONV_ROWS)
                ush = _shift_rows(uwin, HALO - (CV_KERNEL - 1) + k, CONV_ROWS)
                dw_scr[k:k + 1, :] += jnp.sum(dcb * ush, axis=0, keepdims=True)
            av = a_ref[pl.ds(start, CONV_ROWS), :].astype(F32)
            sg = _sigmoid(g_ref[pl.ds(start, CONV_ROWS), :].astype(F32))
            da_ref[pl.ds(start, CONV_ROWS), :] = (du * sg).astype(BF16)
            dg_ref[pl.ds(start, CONV_ROWS), :] = (du * av * sg * (1.0 - sg)).astype(BF16)
            return carry

        lax.fori_loop(0, s // CONV_ROWS, block, 0)
        dw_ref[...] = dw_scr[...]

    act = jax.ShapeDtypeStruct((s, CV_WIDTH), BF16)
    return pl.pallas_call(
        body, name=name,
        out_shape=(act, act, jax.ShapeDtypeStruct((HALO, CV_WIDTH), F32), jax.ShapeDtypeStruct((1, CV_WIDTH), F32)),
        grid=(_CH_TILES,),
        in_specs=[_col_spec(s, _B_A_BLOCK), _col_spec(s, _B_G_BLOCK), _col_spec(HALO, 0), _col_spec(s, 0)],
        out_specs=(_col_spec(s, 0), _col_spec(s, 0), _col_spec(HALO, 0), _col_spec(1, 0)),
        scratch_shapes=[pltpu.VMEM((s + HALO, LANES), F32), pltpu.VMEM((s + HALO, LANES), F32),
                        pltpu.VMEM((HALO, LANES), F32)],
        compiler_params=_params(("parallel",)))(proj, proj, w_pad, dc)


def ln_silu_fwd(c, ln_g, ln_b, name):
    s, d = c.shape

    def body(c_ref, g_ref, b_ref, o_ref):
        xhat, _ = _ln_stats(c_ref[...])
        cn = xhat * g_ref[...] + b_ref[...]
        o_ref[...] = (cn * _sigmoid(cn)).astype(BF16)

    tile = pl.BlockSpec((NORM_TILE, d), lambda i: (i, 0))
    return pl.pallas_call(
        body, name=name, out_shape=jax.ShapeDtypeStruct((s, d), BF16), grid=(s // NORM_TILE,),
        in_specs=[tile, _row_spec(d), _row_spec(d)], out_specs=tile,
        compiler_params=_params(("parallel",)))(c, ln_g, ln_b)


def ln_silu_bwd(c, dy, ln_g, ln_b, name):
    s, d = c.shape

    def body(c_ref, dy_ref, g_ref, b_ref, dc_ref, dg_ref, db_ref):
        first = pl.program_id(0) == 0
        xhat, rstd = _ln_stats(c_ref[...])
        gv = g_ref[...]
        cn = xhat * gv + b_ref[...]
        sg = _sigmoid(cn)
        dcn = dy_ref[...].astype(F32) * sg * (1.0 + cn * (1.0 - sg))
        _accumulate(dg_ref, jnp.sum(dcn * xhat, axis=0, keepdims=True), first)
        _accumulate(db_ref, jnp.sum(dcn, axis=0, keepdims=True), first)
        dc_ref[...] = _ln_bwd(dcn, xhat, rstd, gv)

    tile = pl.BlockSpec((NORM_TILE, d), lambda i: (i, 0))
    vec_shape = jax.ShapeDtypeStruct((1, d), F32)
    return pl.pallas_call(
        body, name=name, out_shape=(jax.ShapeDtypeStruct((s, d), F32), vec_shape, vec_shape),
        grid=(s // NORM_TILE,), in_specs=[tile, tile, _row_spec(d), _row_spec(d)],
        out_specs=(tile, _row_spec(d), _row_spec(d)), compiler_params=_params(("arbitrary",)))(c, dy, ln_g, ln_b)


_D_BLOCK = 2816 // LANES


def _conv3(win, w):
    acc = jnp.zeros((CONV_ROWS, LANES), F32)
    for k in range(SC_KERNEL):
        acc = acc + w[k:k + 1, :] * _shift_rows(win, HALO - (SC_KERNEL - 1) + k, CONV_ROWS)
    return acc


def conv_d_fwd(proj, w_pad, name):
    s = proj.shape[0]

    def body(bg_ref, cg_ref, h_ref, w_ref, o_ref, ppad):
        ppad[0:HALO, :] = jnp.zeros((HALO, LANES), F32)
        ppad[HALO:, :] = cg_ref[...].astype(F32) * h_ref[...].astype(F32)
        w = w_ref[...]

        def block(bi, carry):
            start = pl.multiple_of(bi * CONV_ROWS, CONV_ROWS)
            cv = _conv3(ppad[pl.ds(start, CONV_ROWS + HALO), :], w)
            o_ref[pl.ds(start, CONV_ROWS), :] = (bg_ref[pl.ds(start, CONV_ROWS), :].astype(F32) * cv).astype(BF16)
            return carry

        lax.fori_loop(0, s // CONV_ROWS, block, 0)

    return pl.pallas_call(
        body, name=name, out_shape=jax.ShapeDtypeStruct((s, SC_WIDTH), BF16), grid=(_CH_TILES,),
        in_specs=[_col_spec(s, _D_BLOCK), _col_spec(s, _D_BLOCK + _CH_TILES), _col_spec(s, _D_BLOCK + 2 * _CH_TILES),
                  _col_spec(SUBLANES, 0)],
        out_specs=_col_spec(s, 0), scratch_shapes=[pltpu.VMEM((s + HALO, LANES), F32)],
        compiler_params=_params(("parallel",)))(proj, proj, proj, w_pad)


def conv_d_bwd(proj, w_pad, dy, name):
    s = proj.shape[0]

    def body(bg_ref, cg_ref, h_ref, w_ref, dy_ref, dbg_ref, dcg_ref, dh_ref, dw_ref, ppad, dpad, dw_scr):
        ppad[0:HALO, :] = jnp.zeros((HALO, LANES), F32)
        ppad[HALO:, :] = cg_ref[...].astype(F32) * h_ref[...].astype(F32)
        dpad[0:s, :] = dy_ref[...].astype(F32) * bg_ref[...].astype(F32)
        dpad[s:, :] = jnp.zeros((HALO, LANES), F32)
        dw_scr[...] = jnp.zeros((SUBLANES, LANES), F32)
        w = w_ref[...]

        def block(bi, carry):
            start = pl.multiple_of(bi * CONV_ROWS, CONV_ROWS)
            rows = pl.ds(start, CONV_ROWS)
            pwin = ppad[pl.ds(start, CONV_ROWS + HALO), :]
            dwin = dpad[pl.ds(start, CONV_ROWS + HALO), :]
            dcvb = dwin[:CONV_ROWS]
            dbg_ref[rows, :] = (dy_ref[rows, :].astype(F32) * _conv3(pwin, w)).astype(BF16)
            dp = jnp.zeros((CONV_ROWS, LANES), F32)
            for k in range(SC_KERNEL):
                dp = dp + w[k:k + 1, :] * _shift_rows(dwin, SC_KERNEL - 1 - k, CONV_ROWS)
                psh = _shift_rows(pwin, HALO - (SC_KERNEL - 1) + k, CONV_ROWS)
                dw_scr[k:k + 1, :] += jnp.sum(dcvb * psh, axis=0, keepdims=True)
            dcg_ref[rows, :] = (dp * h_ref[rows, :].astype(F32)).astype(BF16)
            dh_ref[rows, :] = (dp * cg_ref[rows, :].astype(F32)).astype(BF16)
            return carry

        lax.fori_loop(0, s // CONV_ROWS, block, 0)
        dw_ref[...] = dw_scr[...]

    act = jax.ShapeDtypeStruct((s, SC_WIDTH), BF16)
    return pl.pallas_call(
        body, name=name, out_shape=(act, act, act, jax.ShapeDtypeStruct((SUBLANES, SC_WIDTH), F32)),
        grid=(_CH_TILES,),
        in_specs=[_col_spec(s, _D_BLOCK), _col_spec(s, _D_BLOCK + _CH_TILES), _col_spec(s, _D_BLOCK + 2 * _CH_TILES),
                  _col_spec(SUBLANES, 0), _col_spec(s, 0)],
        out_specs=(_col_spec(s, 0), _col_spec(s, 0), _col_spec(s, 0), _col_spec(SUBLANES, 0)),
        scratch_shapes=[pltpu.VMEM((s + HALO, LANES), F32), pltpu.VMEM((s + HALO, LANES), F32),
                        pltpu.VMEM((SUBLANES, LANES), F32)],
        compiler_params=_params(("parallel",)))(proj, proj, proj, w_pad, dy)


_QK_BLOCK = 2048 // LANES
_QK_BLOCKS = (Q_WIDTH + KV_WIDTH) // LANES


def _swap_halves(t):
    lane = lax.broadcasted_iota(jnp.int32, t.shape, 1)
    low = (lane % HEAD_DIM) < (HEAD_DIM // 2)
    return jnp.where(low, pltpu.roll(t, LANES - HEAD_DIM // 2, axis=1), pltpu.roll(t, HEAD_DIM // 2, axis=1))


def rope_fwd(proj, cos_t, sin_t, name, after=None):
    s = proj.shape[0]

    def body(t_ref, c_ref, s_ref, *rest):
        t = t_ref[...].astype(F32)
        rest[-1][...] = (t * c_ref[...] + _swap_halves(t) * s_ref[...]).astype(BF16)

    tr = min(ROPE_TILE, s)
    tab = pl.BlockSpec((tr, LANES), lambda i, j: (i, 0))
    extra = () if after is None else (after,)
    return pl.pallas_call(
        body, name=name, out_shape=jax.ShapeDtypeStruct((s, Q_WIDTH + KV_WIDTH), BF16),
        grid=(s // tr, _QK_BLOCKS),
        in_specs=[pl.BlockSpec((tr, LANES), lambda i, j: (i, _QK_BLOCK + j)), tab, tab]
        + [pl.BlockSpec(memory_space=pl.ANY) for _ in extra],
        out_specs=pl.BlockSpec((tr, LANES), lambda i, j: (i, j)),
        compiler_params=_params(("parallel", "parallel")))(proj, cos_t, sin_t, *extra)


def rope_bwd(d_cur, d_prev, cos_t, sin_t, name):
    s, w = d_cur.shape

    def body(a_ref, b_ref, c_ref, s_ref, o_ref):
        d = a_ref[...] + b_ref[...]
        o_ref[...] = (d * c_ref[...] + _swap_halves(d) * s_ref[...]).astype(BF16)

    tr = min(ROPE_TILE, s)
    tab = pl.BlockSpec((tr, LANES), lambda i, j: (i, 0))
    blk = pl.BlockSpec((tr, LANES), lambda i, j: (i, j))
    return pl.pallas_call(
        body, name=name, out_shape=jax.ShapeDtypeStruct((s, w), BF16), grid=(s // tr, w // LANES),
        in_specs=[blk, blk, tab, tab], out_specs=blk,
        compiler_params=_params(("parallel", "parallel")))(d_cur, d_prev, cos_t, sin_t)


_GROUP = N_Q_HEADS // N_KV_HEADS
_NEG = -1e30


def _attn_specs():
    q_spec = pl.BlockSpec((_GROUP, WINDOW, HEAD_DIM), lambda h, n: (h, n, 0))
    cur = pl.BlockSpec((1, WINDOW, HEAD_DIM), lambda h, n: (h, n, 0))
    prev = pl.BlockSpec((1, WINDOW, HEAD_DIM), lambda h, n: (h, jnp.maximum(n - 1, 0), 0))
    sink = pl.BlockSpec((_GROUP, 1, LANES), lambda h, n: (h, 0, 0))
    return q_spec, cur, prev, sink


def _attn_valid(n):
    qi = lax.broadcasted_iota(jnp.int32, (WINDOW, 2 * WINDOW), 0)
    kj = lax.broadcasted_iota(jnp.int32, (WINDOW, 2 * WINDOW), 1)
    delta = qi + WINDOW - kj
    return (delta >= 0) & (delta < WINDOW) & ((kj >= WINDOW) | (n > 0))


def _attn_probs(q, kcat, valid, sink_row):
    sc = lax.dot_general(q, kcat, _DIMS["nt"], preferred_element_type=F32) * (HEAD_DIM ** -0.5)
    sc = jnp.where(valid, sc, _NEG)
    sink = jnp.max(sink_row, axis=-1, keepdims=True)
    m = jnp.maximum(jnp.max(sc, axis=-1, keepdims=True), sink)
    p = jnp.where(valid, jnp.exp(sc - m), 0.0)
    es = jnp.exp(sink - m)
    inv = 1.0 / (jnp.sum(p, axis=-1, keepdims=True) + es)
    return p * inv, es * inv


def attention_fwd(qh, kh, vh, sinks_b, name):
    s = qh.shape[1]

    def body(q_ref, kc_ref, kp_ref, vc_ref, vp_ref, sk_ref, o_ref):
        valid = _attn_valid(pl.program_id(1))
        kcat = jnp.concatenate([kp_ref[0], kc_ref[0]], axis=0)
        vcat = jnp.concatenate([vp_ref[0], vc_ref[0]], axis=0)
        for g in range(_GROUP):
            probs, _ = _attn_probs(q_ref[g], kcat, valid, sk_ref[g])
            o_ref[g] = jnp.dot(probs.astype(BF16), vcat, preferred_element_type=F32).astype(BF16)

    q_spec, cur, prev, sink = _attn_specs()
    return pl.pallas_call(
        body, name=name, out_shape=jax.ShapeDtypeStruct(qh.shape, BF16), grid=(N_KV_HEADS, s // WINDOW),
        in_specs=[q_spec, cur, prev, cur, prev, sink], out_specs=q_spec,
        compiler_params=_params(("parallel", "parallel")))(qh, kh, kh, vh, vh, sinks_b)


def attention_bwd(qh, kh, vh, sinks_b, doh, name):
    s = qh.shape[1]

    def body(q_ref, kc_ref, kp_ref, vc_ref, vp_ref, sk_ref, do_ref, dq_ref, dkc_ref, dkp_ref, dvc_ref, dvp_ref, ds_ref):
        n = pl.program_id(1)
        valid = _attn_valid(n)
        kcat = jnp.concatenate([kp_ref[0], kc_ref[0]], axis=0)
        vcat = jnp.concatenate([vp_ref[0], vc_ref[0]], axis=0)
        dk = jnp.zeros((2 * WINDOW, HEAD_DIM), F32)
        dv = jnp.zeros((2 * WINDOW, HEAD_DIM), F32)
        for g in range(_GROUP):
            q = q_ref[g]
            do = do_ref[g]
            probs, ps = _attn_probs(q, kcat, valid, sk_ref[g])
            dprobs = lax.dot_general(do, vcat, _DIMS["nt"], preferred_element_type=F32)
            dv = dv + lax.dot_general(probs.astype(BF16), do, _DIMS["tn"], preferred_element_type=F32)
            rs = jnp.sum(probs * dprobs, axis=-1, keepdims=True)
            dsb = (probs * (dprobs - rs) * (HEAD_DIM ** -0.5)).astype(BF16)
            dq_ref[g] = jnp.dot(dsb, kcat, preferred_element_type=F32)
            dk = dk + lax.dot_general(dsb, q, _DIMS["tn"], preferred_element_type=F32)
            dsink = jnp.broadcast_to(-jnp.sum(ps * rs, axis=0, keepdims=True), (1, LANES))

            @pl.when(n == 0)
            def _():
                ds_ref[g] = dsink

            @pl.when(n > 0)
            def _():
                ds_ref[g] += dsink

        dkp_ref[0] = dk[:WINDOW]
        dkc_ref[0] = dk[WINDOW:]
        dvp_ref[0] = dv[:WINDOW]
        dvc_ref[0] = dv[WINDOW:]

    q_spec, cur, prev, sink = _attn_specs()
    kv_shape = jax.ShapeDtypeStruct(kh.shape, F32)
    return pl.pallas_call(
        body, name=name,
        out_shape=(jax.ShapeDtypeStruct(qh.shape, F32), kv_shape, kv_shape, kv_shape, kv_shape,
                   jax.ShapeDtypeStruct(sinks_b.shape, F32)),
        grid=(N_KV_HEADS, s // WINDOW), in_specs=[q_spec, cur, prev, cur, prev, sink, q_spec],
        out_specs=(q_spec, cur, cur, cur, cur, sink),
        compiler_params=_params(("parallel", "arbitrary")))(qh, kh, kh, vh, vh, sinks_b, doh)


def _to_heads(t, heads):
    return t.reshape(t.shape[0], heads, HEAD_DIM).transpose(1, 0, 2)


def _from_heads(t):
    return t.transpose(1, 0, 2).reshape(t.shape[1], t.shape[0] * HEAD_DIM)


def _shift_window(t):
    return jnp.concatenate([t[:, WINDOW:], jnp.zeros_like(t[:, :WINDOW])], axis=1)


def merge_fwd(zg, branches, name):
    s = zg.shape[0]

    def body(zg_ref, b0, b1, b2, b3, o_ref):
        acc = jnp.zeros((TOKEN_TILE, D_MODEL), F32)
        for n, b_ref in enumerate((b0, b1, b2, b3)):
            gate = _sigmoid(zg_ref[:, n * D_MODEL:(n + 1) * D_MODEL].astype(F32))
            acc = acc + gate * b_ref[...].astype(F32)
        o_ref[...] = acc.astype(BF16)

    tile = pl.BlockSpec((TOKEN_TILE, D_MODEL), lambda i: (i, 0))
    wide = pl.BlockSpec((TOKEN_TILE, N_BRANCH * D_MODEL), lambda i: (i, 0))
    return pl.pallas_call(
        body, name=name, out_shape=jax.ShapeDtypeStruct((s, D_MODEL), BF16), grid=(s // TOKEN_TILE,),
        in_specs=[wide, tile, tile, tile, tile], out_specs=tile,
        compiler_params=_params(("parallel",)))(zg, *branches)


def merge_bwd(zg, branches, dm, name):
    s = zg.shape[0]

    def body(zg_ref, b0, b1, b2, b3, dm_ref, dzg_ref, d0, d1, d2, d3):
        dmv = dm_ref[...].astype(F32)
        for n, (b_ref, d_ref) in enumerate(((b0, d0), (b1, d1), (b2, d2), (b3, d3))):
            cols = slice(n * D_MODEL, (n + 1) * D_MODEL)
            gate = _sigmoid(zg_ref[:, cols].astype(F32))
            d_ref[...] = (gate * dmv).astype(BF16)
            dzg_ref[:, cols] = (dmv * b_ref[...].astype(F32) * gate * (1.0 - gate)).astype(BF16)

    tile = pl.BlockSpec((TOKEN_TILE, D_MODEL), lambda i: (i, 0))
    wide = pl.BlockSpec((TOKEN_TILE, N_BRANCH * D_MODEL), lambda i: (i, 0))
    act = jax.ShapeDtypeStruct((s, D_MODEL), BF16)
    return pl.pallas_call(
        body, name=name, out_shape=(jax.ShapeDtypeStruct((s, N_BRANCH * D_MODEL), BF16), act, act, act, act),
        grid=(s // TOKEN_TILE,), in_specs=[wide, tile, tile, tile, tile, tile],
        out_specs=(wide, tile, tile, tile, tile), compiler_params=_params(("parallel",)))(zg, *branches, dm)


def swiglu_fwd(gu, name):
    s = gu.shape[0]

    def body(g_ref, u_ref, o_ref):
        gate = g_ref[...].astype(F32)
        o_ref[...] = (gate * _sigmoid(gate) * u_ref[...].astype(F32)).astype(BF16)

    return pl.pallas_call(
        body, name=name, out_shape=jax.ShapeDtypeStruct((s, D_FF), BF16), grid=(s // TOKEN_TILE,),
        in_specs=[pl.BlockSpec((TOKEN_TILE, D_FF), lambda i: (i, 0)), pl.BlockSpec((TOKEN_TILE, D_FF), lambda i: (i, 1))],
        out_specs=pl.BlockSpec((TOKEN_TILE, D_FF), lambda i: (i, 0)), compiler_params=_params(("parallel",)))(gu, gu)


def swiglu_bwd(gu, dact, name):
    s = gu.shape[0]

    def body(g_ref, u_ref, da_ref, o_ref):
        gate = g_ref[...].astype(F32)
        sg = _sigmoid(gate)
        da = da_ref[...].astype(F32)
        o_ref[:, :D_FF] = (da * u_ref[...].astype(F32) * sg * (1.0 + gate * (1.0 - sg))).astype(BF16)
        o_ref[:, D_FF:] = (da * gate * sg).astype(BF16)

    half = pl.BlockSpec((TOKEN_TILE, D_FF), lambda i: (i, 0))
    return pl.pallas_call(
        body, name=name, out_shape=jax.ShapeDtypeStruct((s, 2 * D_FF), BF16), grid=(s // TOKEN_TILE,),
        in_specs=[half, pl.BlockSpec((TOKEN_TILE, D_FF), lambda i: (i, 1)), half],
        out_specs=pl.BlockSpec((TOKEN_TILE, 2 * D_FF), lambda i: (i, 0)),
        compiler_params=_params(("parallel",)))(gu, gu, dact)


ADAMW_BLOCK_BYTES = 1 << 20


def adamw(parts, w, m, v, name):
    n_parts, r, c = w.shape
    tr = _divisor_tile(r, max(SUBLANES, ADAMW_BLOCK_BYTES // (4 * c)), SUBLANES)
    tiles = r // tr

    def part_spec(j):
        return pl.BlockSpec((N_DEV, tr, c), lambda i: (0, jnp.clip(i - j * tiles, 0, tiles - 1), 0))

    def body(*refs):
        p_refs = refs[:n_parts]
        w_ref, m_ref, v_ref, g_ref, d_ref, nm_ref, nv_ref = refs[n_parts:]
        which = pl.program_id(0) // tiles
        g = None
        for j, p_ref in enumerate(p_refs):
            gj = p_ref[0].astype(F32)
            for i in range(1, N_DEV):
                gj = gj + p_ref[i].astype(F32)
            g = gj if g is None else jnp.where(which == j, gj, g)
        nm = ADAM_B1 * m_ref[0] + (1.0 - ADAM_B1) * g
        nv = ADAM_B2 * v_ref[0] + (1.0 - ADAM_B2) * (g * g)
        m_hat = nm / (1.0 - ADAM_B1 ** ADAM_STEP)
        v_hat = nv / (1.0 - ADAM_B2 ** ADAM_STEP)
        g_ref[0] = g
        d_ref[0] = -ADAM_LR * (m_hat / (jnp.sqrt(v_hat) + ADAM_EPS) + ADAM_WD * w_ref[0])
        nm_ref[0] = nm
        nv_ref[0] = nv

    tile = pl.BlockSpec((1, tr, c), lambda i: (i // tiles, i % tiles, 0))
    shape = jax.ShapeDtypeStruct(w.shape, F32)
    return pl.pallas_call(
        body, name=name, out_shape=(shape, shape, shape, shape), grid=(n_parts * tiles,),
        in_specs=[part_spec(j) for j in range(n_parts)] + [tile, tile, tile],
        out_specs=(tile, tile, tile, tile), compiler_params=_params(("parallel",)))(*parts, w, m, v)


_RELATIONS = [(a, b, e) for a in (0, 1) for b in (0, 1) for e in (0, 1)][1:]


_HBM_SPEC = pl.BlockSpec(memory_space=pltpu.HBM)
_SEM_SPEC = pl.BlockSpec(memory_space=pltpu.SEMAPHORE)
_ANY_SPEC = pl.BlockSpec(memory_space=pl.ANY)
_DATAFLOW = pltpu.SideEffectType.DATAFLOW_SIDE_EFFECTING


_OTHER_CHIPS = [(1, 0), (0, 1), (1, 1)]
_FIRST_LEVEL = [(0, 0, 1)] + [(a, b, 0) for a, b in _OTHER_CHIPS]


def _remote_copies(ins, lands, send_sems, recv_sems, scatter, relations):
    x, y, c = lax.axis_index("x"), lax.axis_index("y"), lax.axis_index("c")
    me = 4 * x + 2 * y + c
    copies = []
    for t in range(len(ins)):
        for k, (a, b, e) in enumerate(relations):
            px, py, pc = (x + a) % 2, (y + b) % 2, (c + e) % 2
            src = ins[t].at[4 * px + 2 * py + pc] if scatter[t] else ins[t]
            copies.append(pltpu.make_async_remote_copy(
                src_ref=src, dst_ref=lands[t].at[me], send_sem=send_sems.at[t * len(relations) + k],
                recv_sem=recv_sems.at[t * len(relations) + k],
                device_id=(px, py, pc), device_id_type=pl.DeviceIdType.MESH))
    return copies


def _forward_copies(lands, send_sems, recv_sems):
    x, y, c = lax.axis_index("x"), lax.axis_index("y"), lax.axis_index("c")
    copies = []
    for t in range(len(lands)):
        for k, (a, b) in enumerate(_OTHER_CHIPS):
            slot = lands[t].at[4 * ((x + a) % 2) + 2 * ((y + b) % 2) + c]
            copies.append(pltpu.make_async_remote_copy(
                src_ref=slot, dst_ref=slot, send_sem=send_sems.at[t * len(_OTHER_CHIPS) + k],
                recv_sem=recv_sems.at[t * len(_OTHER_CHIPS) + k],
                device_id=(x, y, 1 - c), device_id_type=pl.DeviceIdType.MESH))
    return copies


def exchange_start(arrays, scatter, name, after=None, relations=_RELATIONS):
    n = len(arrays)
    n_rel = len(relations)
    land_shapes = [a.shape if scatter[t] else (N_DEV,) + a.shape for t, a in enumerate(arrays)]

    def body(*refs):
        ins, lands = refs[:n], refs[n:2 * n]
        send_sems, recv_sems = refs[-2 * n - 3], refs[-2 * n - 2]
        token = refs[-1]
        for cp in _remote_copies(ins, lands, send_sems, recv_sems, scatter, relations):
            cp.start()
        token[...] = jnp.zeros_like(token)

    sems = pltpu.SemaphoreType.DMA((n * n_rel,))
    out_shape = ((sems, sems) + tuple(pltpu.HBM(a.shape, a.dtype) for a in arrays)
                 + tuple(pltpu.HBM(s, a.dtype) for s, a in zip(land_shapes, arrays))
                 + (jax.ShapeDtypeStruct((SUBLANES, LANES), F32),))
    operands = [pltpu.with_memory_space_constraint(a, pltpu.HBM) for a in arrays]
    operands += [pltpu.with_memory_space_constraint(lax.empty(s, a.dtype), pltpu.HBM) for s, a in zip(land_shapes, arrays)]
    in_specs = [_HBM_SPEC] * (2 * n)
    if after is not None:
        operands.append(after)
        in_specs.append(_ANY_SPEC)
    res = pl.pallas_call(
        body, name=name, out_shape=out_shape, in_specs=in_specs,
        out_specs=(_SEM_SPEC, _SEM_SPEC) + (_HBM_SPEC,) * (2 * n) + (pl.BlockSpec(memory_space=pltpu.VMEM),),
        input_output_aliases={i: 2 + i for i in range(2 * n)},
        compiler_params=pltpu.CompilerParams(has_side_effects=_DATAFLOW))(*operands)
    handle = (res[0], res[1], res[2:2 + n], res[2 + n:2 + 2 * n], tuple(scatter), relations)
    return handle, res[-1]


def exchange_wait(handle, after, name):
    send_sems, recv_sems, sources, lands, scatter, relations = handle
    n = len(sources)

    def body(*refs):
        ins, lzs = refs[:n], refs[n:2 * n]
        send_ref, recv_ref = refs[2 * n], refs[2 * n + 1]
        for cp in _remote_copies(ins, lzs, send_ref, recv_ref, scatter, relations):
            cp.wait_send()
            cp.wait_recv()

    out_shape = (tuple(pltpu.HBM(a.shape, a.dtype) for a in sources) + tuple(pltpu.HBM(a.shape, a.dtype) for a in lands))
    res = pl.pallas_call(
        body, name=name, out_shape=out_shape, in_specs=[_HBM_SPEC] * (2 * n) + [_SEM_SPEC, _SEM_SPEC, _ANY_SPEC],
        out_specs=(_HBM_SPEC,) * (2 * n), input_output_aliases={i: i for i in range(2 * n)},
        compiler_params=pltpu.CompilerParams(has_side_effects=_DATAFLOW))(*sources, *lands, send_sems, recv_sems, after)
    return res[n:]


def forward_start(lands, name):
    n = len(lands)

    def body(*refs):
        send_sems, recv_sems, token = refs[n], refs[n + 1], refs[-1]
        for cp in _forward_copies(refs[:n], send_sems, recv_sems):
            cp.start()
        token[...] = jnp.zeros_like(token)

    sems = pltpu.SemaphoreType.DMA((n * len(_OTHER_CHIPS),))
    res = pl.pallas_call(
        body, name=name,
        out_shape=(sems, sems) + tuple(pltpu.HBM(a.shape, a.dtype) for a in lands)
        + (jax.ShapeDtypeStruct((SUBLANES, LANES), F32),),
        in_specs=[_HBM_SPEC] * n,
        out_specs=(_SEM_SPEC, _SEM_SPEC) + (_HBM_SPEC,) * n + (pl.BlockSpec(memory_space=pltpu.VMEM),),
        input_output_aliases={i: 2 + i for i in range(n)},
        compiler_params=pltpu.CompilerParams(has_side_effects=_DATAFLOW))(*lands)
    return (res[0], res[1], res[2:2 + n]), res[-1]


def forward_wait(handle, after, name):
    send_sems, recv_sems, lands = handle
    n = len(lands)

    def body(*refs):
        for cp in _forward_copies(refs[:n], refs[n], refs[n + 1]):
            cp.wait_send()
            cp.wait_recv()

    return pl.pallas_call(
        body, name=name, out_shape=tuple(pltpu.HBM(a.shape, a.dtype) for a in lands),
        in_specs=[_HBM_SPEC] * n + [_SEM_SPEC, _SEM_SPEC, _ANY_SPEC], out_specs=(_HBM_SPEC,) * n,
        input_output_aliases={i: i for i in range(n)},
        compiler_params=pltpu.CompilerParams(has_side_effects=_DATAFLOW))(*lands, send_sems, recv_sems, after)


def _place_own(landed, own, me):
    return lax.dynamic_update_index_in_dim(landed, own, me, 0)


_SMALL = ("norm_mix", "sg_ln_g", "sg_ln_b", "sg_w", "sg_b", "cv_b", "cv_ln_g", "cv_ln_b", "attn_sinks", "norm_ffn",
          "norm_final")
_PACK_UNIT = SUBLANES * LANES


def _pack(tensors):
    rows = []
    for t in tensors:
        flat = t.reshape(-1)
        pad = (-flat.shape[0]) % _PACK_UNIT
        rows.append(jnp.pad(flat, (0, pad)).reshape(-1, LANES))
    return jnp.concatenate(rows, axis=0)


def _unpack(packed, like):
    out, row = [], 0
    for t in like:
        size = 1
        for d in t.shape:
            size *= d
        rows = -(-size // _PACK_UNIT) * SUBLANES
        out.append(packed[row:row + rows].reshape(-1)[:size].reshape(t.shape))
        row += rows
    return out


def _layer_fwd(l, x, p, late_params, mid_hook=None, ffn_hook=None):
    tag = f"l{l}_"
    xn = rmsnorm_fwd(x, p["norm_mix"], tag + "norm_mix")
    proj = matmul(xn, p["w_in_a"], "nn", BF16, tag + "proj_a", tm_cap=1024, tn_cap=2176)
    zg = matmul(xn, p["w_in_g"], "nn", BF16, tag + "proj_g", tm_cap=1024, tn_cap=2048)
    y_a = mixer_a_fwd(proj, p["sg_ln_g"], p["sg_ln_b"], p["sg_w"], p["sg_b"], tag + "mix_a")
    conv = conv_b_fwd(proj, p["cv_w"], p["cv_b"], tag + "conv_b")
    y_b = ln_silu_fwd(conv, p["cv_ln_g"], p["cv_ln_b"], tag + "ln_silu")
    token = mid_hook(y_b) if mid_hook is not None else None
    qk = rope_fwd(proj, p["cos"], p["sin"], tag + "rope", after=token)
    qh = _to_heads(qk[:, :Q_WIDTH], N_Q_HEADS)
    kh = _to_heads(qk[:, Q_WIDTH:], N_KV_HEADS)
    vh = _to_heads(proj[:, 2688:2816], N_KV_HEADS)
    oh = attention_fwd(qh, kh, vh, p["sinks"], tag + "attn")
    y_c = _from_heads(oh)
    y_d = conv_d_fwd(proj, p["sc_w"], tag + "conv_d")
    ys = (y_a, y_b, y_c, y_d)
    p = {**p, **late_params(y_d)}
    branches = tuple(matmul(ys[n], p["w_branch"][n], "nn", BF16, tag + f"branch{n}", tm_cap=1024, tn_cap=1024)
                     for n in range(N_BRANCH))
    merged = merge_fwd(zg, branches, tag + "merge")
    x_mid = matmul(merged, p["w_out"], "nn", F32, tag + "out", add=x, tm_cap=1024, tn_cap=1024)
    token = ffn_hook(x_mid) if ffn_hook is not None else None
    hn = rmsnorm_fwd(x_mid, p["norm_ffn"], tag + "norm_ffn")
    gu = matmul(hn, p["w_gate_up"], "nn", BF16, tag + "gate_up", tm_cap=512, tn_cap=2816, after=token)
    act = swiglu_fwd(gu, tag + "swiglu")
    x_out = matmul(act, p["w_down"], "nn", F32, tag + "down", add=x_mid, tm_cap=512, tn_cap=1024)
    saved = dict(x=x, xn=xn, proj=proj, zg=zg, conv=conv, qh=qh, kh=kh, vh=vh, ys=ys, branches=branches,
                 merged=merged, x_mid=x_mid, hn=hn, gu=gu, act=act)
    return x_out, saved, p


def _layer_bwd(l, dx_out, p, sv, emit, after=None):
    tag = f"l{l}_b_"
    g = {}
    dact = matmul(dx_out, p["w_down"], "nt", BF16, tag + "dact", after=after, tm_cap=512, tn_cap=2816)
    dw_down = matmul(sv["act"], dx_out, "tn", BF16, tag + "dw_down", tm_cap=1408, tn_cap=512)
    dgu = swiglu_bwd(sv["gu"], dact, tag + "swiglu")
    dhn = matmul(dgu, p["w_gate_up"], "nt", BF16, tag + "dhn", tm_cap=512, tn_cap=512)
    dw_gate_up = matmul(sv["hn"], dgu, "tn", BF16, tag + "dw_gate_up", tm_cap=1024, tn_cap=1408)
    token = emit("a", {"w_gate_up": dw_gate_up, "w_down": dw_down})
    dx_mid, g["norm_ffn"] = rmsnorm_bwd(sv["x_mid"], p["norm_ffn"], dhn, dx_out, tag + "norm_ffn")
    dmerged = matmul(dx_mid, p["w_out"], "nt", BF16, tag + "dmerged", after=token, tm_cap=1024, tn_cap=1024)
    dw_out = matmul(sv["merged"], dx_mid, "tn", BF16, tag + "dw_out", tm_cap=1024, tn_cap=512)
    dzg, *dbranches = merge_bwd(sv["zg"], sv["branches"], dmerged, tag + "merge")
    dys = [matmul(dbranches[n], p["w_branch"][n], "nt", BF16, tag + f"dy{n}", tm_cap=1024, tn_cap=512)
           for n in range(N_BRANCH)]
    dw_branch = jnp.stack(
        [matmul(sv["ys"][n], dbranches[n], "tn", BF16, tag + f"dw_branch{n}", tm_cap=512, tn_cap=1024)
         for n in range(N_BRANCH)])
    token = emit("b", {"w_branch": dw_branch, "w_out": dw_out})
    proj = sv["proj"]
    dz_a, g["sg_ln_g"], g["sg_ln_b"], g["sg_w"], dsb = mixer_a_bwd(
        proj, dys[0], p["sg_ln_g"], p["sg_ln_b"], p["sg_w"], p["sg_b"], tag + "mix_a")
    g["sg_b"] = dsb[:, :, 0]
    dconv, g["cv_ln_g"], g["cv_ln_b"] = ln_silu_bwd(sv["conv"], dys[1], p["cv_ln_g"], p["cv_ln_b"], tag + "ln_silu")
    da, dgate, dcw, g["cv_b"] = conv_b_bwd(proj, p["cv_w"], dconv, tag + "conv_b")
    g["cv_w"] = dcw[:CV_KERNEL]
    doh = _to_heads(dys[2], N_Q_HEADS)
    dqh, dkc, dkp, dvc, dvp, dsk = attention_bwd(sv["qh"], sv["kh"], sv["vh"], p["sinks"], doh, tag + "attn")
    g["attn_sinks"] = dsk[:, 0, 0]
    dqk_cur = jnp.concatenate([_from_heads(dqh), _from_heads(dkc)], axis=1)
    dqk_prev = jnp.concatenate([jnp.zeros((SEQ, Q_WIDTH), F32), _from_heads(_shift_window(dkp))], axis=1)
    dqk = rope_bwd(dqk_cur, dqk_prev, p["cos"], -p["sin"], tag + "rope")
    dv = (_from_heads(dvc) + _from_heads(_shift_window(dvp))).astype(BF16)
    dbg, dcg, dh, dsw = conv_d_bwd(proj, p["sc_w"], dys[3], tag + "conv_d")
    g["sc_w"] = dsw[:SC_KERNEL]
    dproj = jnp.concatenate([dz_a, da, dgate, dqk, dv, dbg, dcg, dh], axis=1)
    dxn = matmul(dproj, p["w_in_a"], "nt", F32, tag + "dxn_a", after=token, tm_cap=512, tn_cap=512)
    dxn = matmul(dzg, p["w_in_g"], "nt", F32, tag + "dxn_g", add=dxn, tm_cap=512, tn_cap=512)
    g["w_in"] = jnp.concatenate(
        [matmul(sv["xn"], dproj, "tn", BF16, tag + "dw_in_a", tm_cap=512, tn_cap=2176),
         matmul(sv["xn"], dzg, "tn", BF16, tag + "dw_in_g", tm_cap=1024, tn_cap=1024)], axis=1)
    dx_in, g["norm_mix"] = rmsnorm_bwd(sv["x"], p["norm_mix"], dxn, dx_mid, tag + "norm_mix")
    return dx_in, g


_EARLY = ("w_in", "cv_w", "sc_w")
_LATE = ("w_branch", "w_out", "w_gate_up", "w_down")


def _full_weight(name, t):
    if name in ("w_out", "w_down"):
        return t.reshape(-1, t.shape[-1])
    if name == "w_branch":
        return t.transpose(1, 2, 0, 3).reshape(N_BRANCH, SG_WIDTH, D_MODEL)
    return t.transpose(1, 0, 2).reshape(t.shape[1], -1)


def _to_blocks(name, full):
    if name in ("w_out", "w_down"):
        return full.reshape(N_DEV, -1, full.shape[-1])
    if name == "w_branch":
        return full.reshape(N_BRANCH, SG_WIDTH, N_DEV, -1).transpose(2, 0, 1, 3)
    return full.reshape(full.shape[0], N_DEV, -1).transpose(1, 0, 2)


def _rope_tables():
    pos = jnp.arange(SEQ, dtype=F32)
    inv_freq = 1.0 / (ROPE_THETA ** (jnp.arange(0, HEAD_DIM, 2, dtype=F32) / HEAD_DIM))
    ang = pos[:, None] * inv_freq[None, :]
    cos, sin = jnp.cos(ang), jnp.sin(ang)
    reps = LANES // HEAD_DIM
    return jnp.tile(jnp.concatenate([cos, cos], axis=1), (1, reps)), jnp.tile(jnp.concatenate([-sin, sin], axis=1), (1, reps))


def kernel(x, norm_mix, w_in, sg_ln_g, sg_ln_b, sg_w, sg_b, cv_w, cv_b, cv_ln_g, cv_ln_b, attn_sinks, sc_w, w_branch, w_out, norm_ffn, w_gate_up, w_down, norm_final, loss_target, m_norm_mix, m_w_in, m_sg_ln_g, m_sg_ln_b, m_sg_w, m_sg_b, m_cv_w, m_cv_b, m_cv_ln_g, m_cv_ln_b, m_attn_sinks, m_sc_w, m_w_branch, m_w_out, m_norm_ffn, m_w_gate_up, m_w_down, m_norm_final, v_norm_mix, v_w_in, v_sg_ln_g, v_sg_ln_b, v_sg_w, v_sg_b, v_cv_w, v_cv_b, v_cv_ln_g, v_cv_ln_b, v_attn_sinks, v_sc_w, v_w_branch, v_w_out, v_norm_ffn, v_w_gate_up, v_w_down, v_norm_final):
    names = ("norm_mix", "w_in", "sg_ln_g", "sg_ln_b", "sg_w", "sg_b", "cv_w", "cv_b", "cv_ln_g", "cv_ln_b",
             "attn_sinks", "sc_w", "w_branch", "w_out", "norm_ffn", "w_gate_up", "w_down", "norm_final")
    w = dict(zip(names, (norm_mix, w_in, sg_ln_g, sg_ln_b, sg_w, sg_b, cv_w, cv_b, cv_ln_g, cv_ln_b, attn_sinks,
                         sc_w, w_branch, w_out, norm_ffn, w_gate_up, w_down, norm_final)))
    m = dict(zip(names, (m_norm_mix, m_w_in, m_sg_ln_g, m_sg_ln_b, m_sg_w, m_sg_b, m_cv_w, m_cv_b, m_cv_ln_g,
                         m_cv_ln_b, m_attn_sinks, m_sc_w, m_w_branch, m_w_out, m_norm_ffn, m_w_gate_up, m_w_down,
                         m_norm_final)))
    v = dict(zip(names, (v_norm_mix, v_w_in, v_sg_ln_g, v_sg_ln_b, v_sg_w, v_sg_b, v_cv_w, v_cv_b, v_cv_ln_g,
                         v_cv_ln_b, v_attn_sinks, v_sc_w, v_w_branch, v_w_out, v_norm_ffn, v_w_gate_up, v_w_down,
                         v_norm_final)))

    me = 4 * lax.axis_index("x") + 2 * lax.axis_index("y") + lax.axis_index("c")

    gathers, forwards, token = {}, {}, None
    for l in range(DEPTH):
        for group in (_EARLY, _LATE):
            shards = [w[n][l].astype(BF16) for n in group]
            handle, token = exchange_start(shards, [False] * len(group), f"gather_start{l}_{group[0]}", after=token,
                                           relations=_FIRST_LEVEL)
            gathers[(l, group)] = (handle, shards)

    def begin_forward(l, group, after):
        landed = exchange_wait(gathers[(l, group)][0], after, f"gather_wait{l}_{group[0]}")
        forwards[(l, group)], tok = forward_start(landed, f"forward_start{l}_{group[0]}")
        return tok

    def landed_weights(l, group, after):
        landed = forward_wait(forwards[(l, group)], after, f"forward_wait{l}_{group[0]}")
        return {n: _full_weight(n, _place_own(t, own, me)) for n, t, own in zip(group, landed, gathers[(l, group)][1])}

    cos_t, sin_t = _rope_tables()

    def early_params(l, after):
        full = landed_weights(l, _EARLY, after)
        return dict(
            norm_mix=w["norm_mix"][l][None], norm_ffn=w["norm_ffn"][l][None],
            w_in_a=full["w_in"][:, :PROJ_A], w_in_g=full["w_in"][:, PROJ_A:],
            sg_ln_g=w["sg_ln_g"][l][None], sg_ln_b=w["sg_ln_b"][l][None], sg_w=w["sg_w"][l],
            sg_b=jnp.broadcast_to(w["sg_b"][l][:, :, None], (SG_GROUPS, SG_CHUNK, LANES)),
            cv_w=jnp.pad(full["cv_w"].astype(F32), ((0, HALO - CV_KERNEL), (0, 0))),
            cv_b=w["cv_b"][l][None], cv_ln_g=w["cv_ln_g"][l][None], cv_ln_b=w["cv_ln_b"][l][None],
            sinks=jnp.broadcast_to(w["attn_sinks"][l][:, None, None], (N_Q_HEADS, 1, LANES)),
            sc_w=jnp.pad(full["sc_w"].astype(F32), ((0, SUBLANES - SC_KERNEL), (0, 0))),
            cos=cos_t, sin=sin_t)

    params, saved = [None] * DEPTH, [None] * DEPTH
    h = x[0]
    after = begin_forward(0, _EARLY, token)
    for l in range(DEPTH):
        h, saved[l], params[l] = _layer_fwd(
            l, h, early_params(l, after), lambda behind, l=l: landed_weights(l, _LATE, behind),
            mid_hook=lambda behind, l=l: begin_forward(l, _LATE, behind),
            ffn_hook=(lambda behind, l=l: begin_forward(l + 1, _EARLY, behind)) if l + 1 < DEPTH else None)
        after = h
    loss_row, dh, d_norm_final = loss_head(h, w["norm_final"][None], loss_target[0], "loss_head")

    sent = {}

    def emitter(l):
        def emit(group, grads_of):
            send = [_to_blocks(n, grads_of[n].astype(BF16)) for n in grads_of]
            handle, tok = exchange_start(send, [True] * len(send), f"grads_start{l}{group}")
            sent[(l, group)] = (handle, send, tuple(grads_of))
            return tok
        return emit

    grads = [None] * DEPTH
    dh, grads[1] = _layer_bwd(1, dh, params[1], saved[1], emitter(1))
    token = emitter(1)("c", {n: grads[1][n] for n in _EARLY})
    dh, grads[0] = _layer_bwd(0, dh, params[0], saved[0], emitter(0), after=token)
    grad_x = dh[None]

    stacked = {n: jnp.stack([grads[l][n] for l in range(DEPTH)]) for n in _SMALL if n != "norm_final"}
    for n in ("norm_mix", "norm_ffn", "sg_ln_g", "sg_ln_b", "cv_b", "cv_ln_g", "cv_ln_b"):
        stacked[n] = stacked[n][:, 0]
    stacked["norm_final"] = d_norm_final[0]
    no_state = jnp.zeros((1,), F32)
    small_like = [w[n] for n in _SMALL] + [no_state]
    small_part = _pack([stacked[n] for n in _SMALL] + [loss_row[0, :1]])
    send_last = [_to_blocks(n, grads[0][n].astype(BF16)) for n in _EARLY]
    handle_last, token = exchange_start(send_last + [small_part], [True] * len(_EARLY) + [False], "grads_start0c")

    def received(l, group, after):
        handle, send, group_names = sent[(l, group)]
        landed = exchange_wait(handle, after, f"grads_wait{l}{group}")
        return {n: _place_own(t, lax.dynamic_index_in_dim(s, me, 0, keepdims=False), me)
                for n, t, s in zip(group_names, landed, send)}

    out_g, out_d, out_m, out_v = {}, {}, {}, {}

    def update(n, by_layer):
        shape = w[n].shape
        view = (DEPTH, w[n].size // (DEPTH * shape[-1]), shape[-1])
        parts = [t.reshape((N_DEV,) + view[1:]) for t in by_layer]
        res = adamw(parts, w[n].reshape(view), m[n].reshape(view), v[n].reshape(view), "adamw_" + n)
        out_g[n], out_d[n], out_m[n], out_v[n] = (t.reshape(shape) for t in res)
        return res[0]

    behind = token
    for group in ("a", "b"):
        r1 = received(1, group, behind)
        r0 = received(0, group, next(iter(r1.values())))
        for n in r0:
            behind = update(n, [r0[n], r1[n]])
    r1 = received(1, "c", behind)
    landed = exchange_wait(handle_last, next(iter(r1.values())), "grads_wait0c")
    for n, t, s in zip(_EARLY, landed, send_last):
        update(n, [_place_own(t, lax.dynamic_index_in_dim(s, me, 0, keepdims=False), me), r1[n]])
    res = adamw([_place_own(landed[-1], small_part, me)], _pack(small_like)[None],
                _pack([m[n] for n in _SMALL] + [no_state])[None], _pack([v[n] for n in _SMALL] + [no_state])[None],
                "adamw_small")
    for store, packed in zip((out_g, out_d, out_m, out_v), res):
        for n, t in zip(_SMALL + ("loss",), _unpack(packed[0], small_like)):
            store[n] = t

    loss = out_g["loss"][0]
    return (loss, grad_x, *[out_g[n] for n in names], *[out_d[n] for n in names], *[out_m[n] for n in names],
            *[out_v[n] for n in names])
```

```python
import jax
import jax.numpy as jnp
from jax import lax
from jax.experimental import pallas as pl
from jax.experimental.pallas import tpu as pltpu

F32 = jnp.float32
BF16 = jnp.bfloat16

SEQ = 2048
D_MODEL = 1024
DEPTH = 2
SG_WIDTH = 512
SG_CHUNK = 128
SG_GROUPS = 4
CV_WIDTH = 512
CV_KERNEL = 31
HEAD_DIM = 64
N_Q_HEADS = 8
N_KV_HEADS = 2
Q_WIDTH = 512
KV_WIDTH = 128
WINDOW = 128
SC_WIDTH = 512
SC_KERNEL = 3
N_BRANCH = 4
D_FF = 2816
EPS = 1e-6
ROPE_THETA = 10000.0
PROJ_A = 4352
PROJ_WIDTH = 8448
N_DEV = 8

ADAM_LR = 0.001
ADAM_B1 = 0.9
ADAM_B2 = 0.999
ADAM_EPS = 1e-08
ADAM_WD = 0.01
ADAM_STEP = 10

LANES = 128
SUBLANES = 8
VMEM_LIMIT_BYTES = 48 * 1024 * 1024
HALO = 32
CONV_ROWS = 256
TOKEN_TILE = 256
NORM_TILE = 512
ROPE_TILE = 1024

_SQRT_HALF = 0.7071067811865476
_INV_SQRT_2PI = 0.3989422804014327


def _params(semantics=None):
    return pltpu.CompilerParams(dimension_semantics=semantics, vmem_limit_bytes=VMEM_LIMIT_BYTES)


def _divisor_tile(n, cap, unit):
    best = None
    for t in range(unit, min(n, cap) + 1, unit):
        if n % t == 0:
            best = t
    return best if best is not None else n


_DIMS = {"nn": (((1,), (0,)), ((), ())), "nt": (((1,), (1,)), ((), ())), "tn": (((0,), (0,)), ((), ()))}


def matmul(a, b, mode, out_dtype, name, add=None, tm_cap=512, tn_cap=512, after=None):
    if mode == "nn":
        (m, k), (_, n) = a.shape, b.shape
    elif mode == "nt":
        (m, k), (n, _) = a.shape, b.shape
    else:
        (k, m), (_, n) = a.shape, b.shape
    tm = _divisor_tile(m, tm_cap, LANES)
    tn = _divisor_tile(n, tn_cap, LANES)
    a_spec = pl.BlockSpec((k, tm), lambda i, j: (0, i)) if mode == "tn" else pl.BlockSpec((tm, k), lambda i, j: (i, 0))
    b_spec = pl.BlockSpec((tn, k), lambda i, j: (j, 0)) if mode == "nt" else pl.BlockSpec((k, tn), lambda i, j: (0, j))
    o_spec = pl.BlockSpec((tm, tn), lambda i, j: (i, j))
    dims = _DIMS[mode]

    def body(*refs):
        a_ref, b_ref = refs[0], refs[1]
        o_ref = refs[-1]
        acc = lax.dot_general(a_ref[...].astype(BF16), b_ref[...].astype(BF16), dims, preferred_element_type=F32)
        if add is not None:
            acc = acc + refs[2][...].astype(F32)
        o_ref[...] = acc.astype(out_dtype)

    operands = (a, b) + (() if add is None else (add,)) + (() if after is None else (after,))
    in_specs = [a_spec, b_spec] + ([o_spec] if add is not None else [])
    in_specs += [pl.BlockSpec(memory_space=pl.ANY)] if after is not None else []
    return pl.pallas_call(
        body, name=name, out_shape=jax.ShapeDtypeStruct((m, n), out_dtype), grid=(m // tm, n // tn),
        in_specs=in_specs, out_specs=o_spec, compiler_params=_params(("parallel", "parallel")))(*operands)


def _sigmoid(x):
    return 1.0 / (1.0 + jnp.exp(-x))


def _gelu(x):
    return 0.5 * x * (1.0 + lax.erf(x * _SQRT_HALF))


def _gelu_grad(x):
    return 0.5 * (1.0 + lax.erf(x * _SQRT_HALF)) + x * _INV_SQRT_2PI * jnp.exp(-0.5 * x * x)


def _rms_stats(x):
    r = lax.rsqrt(jnp.mean(x * x, axis=-1, keepdims=True) + EPS)
    return x * r, r


def _rms_bwd(dxn, xhat, r, g):
    h = dxn * g
    return r * (h - xhat * jnp.mean(h * xhat, axis=-1, keepdims=True))


def _ln_stats(x):
    mu = jnp.mean(x, axis=-1, keepdims=True)
    xc = x - mu
    rstd = lax.rsqrt(jnp.mean(xc * xc, axis=-1, keepdims=True) + EPS)
    return xc * rstd, rstd


def _ln_bwd(dy, xhat, rstd, g):
    dxhat = dy * g
    return rstd * (dxhat - jnp.mean(dxhat, axis=-1, keepdims=True)
                   - xhat * jnp.mean(dxhat * xhat, axis=-1, keepdims=True))


def _accumulate(ref, value, first):
    @pl.when(first)
    def _():
        ref[...] = value

    @pl.when(jnp.logical_not(first))
    def _():
        ref[...] += value


def _shift_rows(win, shift, n_out):
    n = win.shape[0]
    if shift % n == 0:
        return win[:n_out]
    return pltpu.roll(win, n - shift, axis=0)[:n_out]


def _row_spec(width):
    return pl.BlockSpec((1, width), lambda i: (0, 0))


def rmsnorm_fwd(x, g, name):
    s, d = x.shape

    def body(x_ref, g_ref, o_ref):
        xhat, _ = _rms_stats(x_ref[...])
        o_ref[...] = (xhat * g_ref[...]).astype(BF16)

    tile = pl.BlockSpec((NORM_TILE, d), lambda i: (i, 0))
    return pl.pallas_call(
        body, name=name, out_shape=jax.ShapeDtypeStruct((s, d), BF16), grid=(s // NORM_TILE,),
        in_specs=[tile, _row_spec(d)], out_specs=tile, compiler_params=_params(("parallel",)))(x, g)


def rmsnorm_bwd(x, g, dxn, dres, name):
    s, d = x.shape

    def body(x_ref, g_ref, dxn_ref, dres_ref, dx_ref, dg_ref):
        xhat, r = _rms_stats(x_ref[...])
        dxn_v = dxn_ref[...].astype(F32)
        dx_ref[...] = dres_ref[...] + _rms_bwd(dxn_v, xhat, r, g_ref[...])
        _accumulate(dg_ref, jnp.sum(dxn_v * xhat, axis=0, keepdims=True), pl.program_id(0) == 0)

    tile = pl.BlockSpec((NORM_TILE, d), lambda i: (i, 0))
    return pl.pallas_call(
        body, name=name, out_shape=(jax.ShapeDtypeStruct((s, d), F32), jax.ShapeDtypeStruct((1, d), F32)),
        grid=(s // NORM_TILE,), in_specs=[tile, _row_spec(d), tile, tile], out_specs=(tile, _row_spec(d)),
        compiler_params=_params(("arbitrary",)))(x, g, dxn, dres)


def loss_head(x, g, target, name):
    s, d = x.shape

    def body(x_ref, g_ref, t_ref, loss_ref, dx_ref, dg_ref):
        first = pl.program_id(0) == 0
        xhat, r = _rms_stats(x_ref[...])
        gv = g_ref[...]
        err = xhat * gv - t_ref[...]
        part = 0.5 * jnp.sum(jnp.sum(err * err, axis=-1, keepdims=True), axis=0, keepdims=True) / d
        _accumulate(loss_ref, jnp.broadcast_to(part, (1, LANES)), first)
        dy = err / d
        dx_ref[...] = _rms_bwd(dy, xhat, r, gv)
        _accumulate(dg_ref, jnp.sum(dy * xhat, axis=0, keepdims=True), first)

    tile = pl.BlockSpec((NORM_TILE, d), lambda i: (i, 0))
    return pl.pallas_call(
        body, name=name,
        out_shape=(jax.ShapeDtypeStruct((1, LANES), F32), jax.ShapeDtypeStruct((s, d), F32),
                   jax.ShapeDtypeStruct((1, d), F32)),
        grid=(s // NORM_TILE,), in_specs=[tile, _row_spec(d), tile],
        out_specs=(_row_spec(LANES), tile, _row_spec(d)), compiler_params=_params(("arbitrary",)))(x, g, target)


def _tril_mask():
    row = lax.broadcasted_iota(jnp.int32, (SG_CHUNK, SG_CHUNK), 0)
    col = lax.broadcasted_iota(jnp.int32, (SG_CHUNK, SG_CHUNK), 1)
    return row >= col


def _sg_specs():
    vec = _row_spec(SG_WIDTH)
    mat = pl.BlockSpec((SG_GROUPS, SG_CHUNK, SG_CHUNK), lambda i: (0, 0, 0))
    return vec, mat


def mixer_a_fwd(proj, ln_g, ln_b, w_s, b_s, name):
    s = proj.shape[0]
    chunks = TOKEN_TILE // SG_CHUNK

    def body(z_ref, lg_ref, lb_ref, w_ref, b_ref, o_ref):
        ge = _gelu(z_ref[...].astype(F32))
        u = ge[:, :SG_WIDTH]
        xhat, _ = _ln_stats(ge[:, SG_WIDTH:])
        vn = xhat * lg_ref[...] + lb_ref[...]
        tril = _tril_mask()
        for ci in range(chunks):
            rows = slice(ci * SG_CHUNK, (ci + 1) * SG_CHUNK)
            for g in range(SG_GROUPS):
                cols = slice(g * LANES, (g + 1) * LANES)
                wm = jnp.where(tril, w_ref[g], 0.0).astype(BF16)
                mixed = jnp.dot(wm, vn[rows, cols].astype(BF16), preferred_element_type=F32) + b_ref[g]
                o_ref[rows, cols] = (u[rows, cols] * mixed).astype(BF16)

    vec, mat = _sg_specs()
    return pl.pallas_call(
        body, name=name, out_shape=jax.ShapeDtypeStruct((s, SG_WIDTH), BF16), grid=(s // TOKEN_TILE,),
        in_specs=[pl.BlockSpec((TOKEN_TILE, 2 * SG_WIDTH), lambda i: (i, 0)), vec, vec, mat, mat],
        out_specs=pl.BlockSpec((TOKEN_TILE, SG_WIDTH), lambda i: (i, 0)),
        compiler_params=_params(("parallel",)))(proj, ln_g, ln_b, w_s, b_s)


def mixer_a_bwd(proj, dy, ln_g, ln_b, w_s, b_s, name):
    s = proj.shape[0]
    chunks = TOKEN_TILE // SG_CHUNK

    def body(z_ref, dy_ref, lg_ref, lb_ref, w_ref, b_ref, dz_ref, dlg_ref, dlb_ref, dw_ref, db_ref, du_scr, dvn_scr):
        first = pl.program_id(0) == 0

        @pl.when(first)
        def _():
            dw_ref[...] = jnp.zeros_like(dw_ref)
            db_ref[...] = jnp.zeros_like(db_ref)

        z = z_ref[...].astype(F32)
        ge = _gelu(z)
        u = ge[:, :SG_WIDTH]
        xhat, rstd = _ln_stats(ge[:, SG_WIDTH:])
        lg = lg_ref[...]
        vn = xhat * lg + lb_ref[...]
        dyv = dy_ref[...].astype(F32)
        tril = _tril_mask()
        for ci in range(chunks):
            rows = slice(ci * SG_CHUNK, (ci + 1) * SG_CHUNK)
            for g in range(SG_GROUPS):
                cols = slice(g * LANES, (g + 1) * LANES)
                wm = jnp.where(tril, w_ref[g], 0.0).astype(BF16)
                vg = vn[rows, cols].astype(BF16)
                mixed = jnp.dot(wm, vg, preferred_element_type=F32) + b_ref[g]
                dyb = dyv[rows, cols]
                du_scr[rows, cols] = dyb * mixed
                dmix = dyb * u[rows, cols]
                db_ref[g] += jnp.broadcast_to(jnp.sum(dmix, axis=1, keepdims=True), (SG_CHUNK, LANES))
                dmb = dmix.astype(BF16)
                dwg = lax.dot_general(dmb, vg, _DIMS["nt"], preferred_element_type=F32)
                dw_ref[g] += jnp.where(tril, dwg, 0.0)
                dvn_scr[rows, cols] = lax.dot_general(wm, dmb, _DIMS["tn"], preferred_element_type=F32)
        dvn = dvn_scr[...]
        _accumulate(dlg_ref, jnp.sum(dvn * xhat, axis=0, keepdims=True), first)
        _accumulate(dlb_ref, jnp.sum(dvn, axis=0, keepdims=True), first)
        dvv = _ln_bwd(dvn, xhat, rstd, lg)
        gg = _gelu_grad(z)
        dz_ref[:, :SG_WIDTH] = (du_scr[...] * gg[:, :SG_WIDTH]).astype(BF16)
        dz_ref[:, SG_WIDTH:] = (dvv * gg[:, SG_WIDTH:]).astype(BF16)

    vec, mat = _sg_specs()
    mat_shape = jax.ShapeDtypeStruct((SG_GROUPS, SG_CHUNK, SG_CHUNK), F32)
    vec_shape = jax.ShapeDtypeStruct((1, SG_WIDTH), F32)
    return pl.pallas_call(
        body, name=name,
        out_shape=(jax.ShapeDtypeStruct((s, 2 * SG_WIDTH), BF16), vec_shape, vec_shape, mat_shape, mat_shape),
        grid=(s // TOKEN_TILE,),
        in_specs=[pl.BlockSpec((TOKEN_TILE, 2 * SG_WIDTH), lambda i: (i, 0)),
                  pl.BlockSpec((TOKEN_TILE, SG_WIDTH), lambda i: (i, 0)), vec, vec, mat, mat],
        out_specs=(pl.BlockSpec((TOKEN_TILE, 2 * SG_WIDTH), lambda i: (i, 0)), vec, vec, mat, mat),
        scratch_shapes=[pltpu.VMEM((TOKEN_TILE, SG_WIDTH), F32), pltpu.VMEM((TOKEN_TILE, SG_WIDTH), F32)],
        compiler_params=_params(("arbitrary",)))(proj, dy, ln_g, ln_b, w_s, b_s)


_B_A_BLOCK = 1024 // LANES
_B_G_BLOCK = 1536 // LANES
_CH_TILES = CV_WIDTH // LANES


def _col_spec(s, first_block):
    return pl.BlockSpec((s, LANES), lambda j: (0, first_block + j))


def conv_b_fwd(proj, w_pad, bias, name):
    s = proj.shape[0]

    def body(a_ref, g_ref, w_ref, b_ref, c_ref, upad):
        upad[0:HALO, :] = jnp.zeros((HALO, LANES), F32)
        upad[HALO:, :] = a_ref[...].astype(F32) * _sigmoid(g_ref[...].astype(F32))
        w = w_ref[...]
        bv = b_ref[...]

        def block(bi, carry):
            start = pl.multiple_of(bi * CONV_ROWS, CONV_ROWS)
            win = upad[pl.ds(start, CONV_ROWS + HALO), :]
            acc = jnp.zeros((CONV_ROWS, LANES), F32)
            for k in range(CV_KERNEL):
                acc = acc + w[k:k + 1, :] * _shift_rows(win, HALO - (CV_KERNEL - 1) + k, CONV_ROWS)
            c_ref[pl.ds(start, CONV_ROWS), :] = acc + bv
            return carry

        lax.fori_loop(0, s // CONV_ROWS, block, 0)

    return pl.pallas_call(
        body, name=name, out_shape=jax.ShapeDtypeStruct((s, CV_WIDTH), F32), grid=(_CH_TILES,),
        in_specs=[_col_spec(s, _B_A_BLOCK), _col_spec(s, _B_G_BLOCK), _col_spec(HALO, 0), _col_spec(1, 0)],
        out_specs=_col_spec(s, 0), scratch_shapes=[pltpu.VMEM((s + HALO, LANES), F32)],
        compiler_params=_params(("parallel",)))(proj, proj, w_pad, bias)


def conv_b_bwd(proj, w_pad, dc, name):
    s = proj.shape[0]

    def body(a_ref, g_ref, w_ref, dc_ref, da_ref, dg_ref, dw_ref, db_ref, upad, dpad, dw_scr):
        upad[0:HALO, :] = jnp.zeros((HALO, LANES), F32)
        upad[HALO:, :] = a_ref[...].astype(F32) * _sigmoid(g_ref[...].astype(F32))
        dcv = dc_ref[...]
        dpad[0:s, :] = dcv
        dpad[s:, :] = jnp.zeros((HALO, LANES), F32)
        db_ref[...] = jnp.sum(dcv, axis=0, keepdims=True)
        dw_scr[...] = jnp.zeros((HALO, LANES), F32)
        w = w_ref[...]

        def block(bi, carry):
            start = pl.multiple_of(bi * CONV_ROWS, CONV_ROWS)
            uwin = upad[pl.ds(start, CONV_ROWS + HALO), :]
            dwin = dpad[pl.ds(start, CONV_ROWS + HALO), :]
            dcb = dwin[:CONV_ROWS]
            du = jnp.zeros((CONV_ROWS, LANES), F32)
            for k in range(CV_KERNEL):
                du = du + w[k:k + 1, :] * _shift_rows(dwin, CV_KERNEL - 1 - k, CONV_ROWS)
                ush = _shift_rows(uwin, HALO - (CV_KERNEL - 1) + k, CONV_ROWS)
                dw_scr[k:k + 1, :] += jnp.sum(dcb * ush, axis=0, keepdims=True)
            av = a_ref[pl.ds(start, CONV_ROWS), :].astype(F32)
            sg = _sigmoid(g_ref[pl.ds(start, CONV_ROWS), :].astype(F32))
            da_ref[pl.ds(start, CONV_ROWS), :] = (du * sg).astype(BF16)
            dg_ref[pl.ds(start, CONV_ROWS), :] = (du * av * sg * (1.0 - sg)).astype(BF16)
            return carry

        lax.fori_loop(0, s // CONV_ROWS, block, 0)
        dw_ref[...] = dw_scr[...]

    act = jax.ShapeDtypeStruct((s, CV_WIDTH), BF16)
    return pl.pallas_call(
        body, name=name,
        out_shape=(act, act, jax.ShapeDtypeStruct((HALO, CV_WIDTH), F32), jax.ShapeDtypeStruct((1, CV_WIDTH), F32)),
        grid=(_CH_TILES,),
        in_specs=[_col_spec(s, _B_A_BLOCK), _col_spec(s, _B_G_BLOCK), _col_spec(HALO, 0), _col_spec(s, 0)],
        out_specs=(_col_spec(s, 0), _col_spec(s, 0), _col_spec(HALO, 0), _col_spec(1, 0)),
        scratch_shapes=[pltpu.VMEM((s + HALO, LANES), F32), pltpu.VMEM((s + HALO, LANES), F32),
                        pltpu.VMEM((HALO, LANES), F32)],
        compiler_params=_params(("parallel",)))(proj, proj, w_pad, dc)


def ln_silu_fwd(c, ln_g, ln_b, name):
    s, d = c.shape

    def body(c_ref, g_ref, b_ref, o_ref):
        xhat, _ = _ln_stats(c_ref[...])
        cn = xhat * g_ref[...] + b_ref[...]
        o_ref[...] = (cn * _sigmoid(cn)).astype(BF16)

    tile = pl.BlockSpec((NORM_TILE, d), lambda i: (i, 0))
    return pl.pallas_call(
        body, name=name, out_shape=jax.ShapeDtypeStruct((s, d), BF16), grid=(s // NORM_TILE,),
        in_specs=[tile, _row_spec(d), _row_spec(d)], out_specs=tile,
        compiler_params=_params(("parallel",)))(c, ln_g, ln_b)


def ln_silu_bwd(c, dy, ln_g, ln_b, name):
    s, d = c.shape

    def body(c_ref, dy_ref, g_ref, b_ref, dc_ref, dg_ref, db_ref):
        first = pl.program_id(0) == 0
        xhat, rstd = _ln_stats(c_ref[...])
        gv = g_ref[...]
        cn = xhat * gv + b_ref[...]
        sg = _sigmoid(cn)
        dcn = dy_ref[...].astype(F32) * sg * (1.0 + cn * (1.0 - sg))
        _accumulate(dg_ref, jnp.sum(dcn * xhat, axis=0, keepdims=True), first)
        _accumulate(db_ref, jnp.sum(dcn, axis=0, keepdims=True), first)
        dc_ref[...] = _ln_bwd(dcn, xhat, rstd, gv)

    tile = pl.BlockSpec((NORM_TILE, d), lambda i: (i, 0))
    vec_shape = jax.ShapeDtypeStruct((1, d), F32)
    return pl.pallas_call(
        body, name=name, out_shape=(jax.ShapeDtypeStruct((s, d), F32), vec_shape, vec_shape),
        grid=(s // NORM_TILE,), in_specs=[tile, tile, _row_spec(d), _row_spec(d)],
        out_specs=(tile, _row_spec(d), _row_spec(d)), compiler_params=_params(("arbitrary",)))(c, dy, ln_g, ln_b)


_D_BLOCK = 2816 // LANES


def _conv3(win, w):
    acc = jnp.zeros((CONV_ROWS, LANES), F32)
    for k in range(SC_KERNEL):
        acc = acc + w[k:k + 1, :] * _shift_rows(win, HALO - (SC_KERNEL - 1) + k, CONV_ROWS)
    return acc


def conv_d_fwd(proj, w_pad, name):
    s = proj.shape[0]

    def body(bg_ref, cg_ref, h_ref, w_ref, o_ref, ppad):
        ppad[0:HALO, :] = jnp.zeros((HALO, LANES), F32)
        ppad[HALO:, :] = cg_ref[...].astype(F32) * h_ref[...].astype(F32)
        w = w_ref[...]

        def block(bi, carry):
            start = pl.multiple_of(bi * CONV_ROWS, CONV_ROWS)
            cv = _conv3(ppad[pl.ds(start, CONV_ROWS + HALO), :], w)
            o_ref[pl.ds(start, CONV_ROWS), :] = (bg_ref[pl.ds(start, CONV_ROWS), :].astype(F32) * cv).astype(BF16)
            return carry

        lax.fori_loop(0, s // CONV_ROWS, block, 0)

    return pl.pallas_call(
        body, name=name, out_shape=jax.ShapeDtypeStruct((s, SC_WIDTH), BF16), grid=(_CH_TILES,),
        in_specs=[_col_spec(s, _D_BLOCK), _col_spec(s, _D_BLOCK + _CH_TILES), _col_spec(s, _D_BLOCK + 2 * _CH_TILES),
                  _col_spec(SUBLANES, 0)],
        out_specs=_col_spec(s, 0), scratch_shapes=[pltpu.VMEM((s + HALO, LANES), F32)],
        compiler_params=_params(("parallel",)))(proj, proj, proj, w_pad)


def conv_d_bwd(proj, w_pad, dy, name):
    s = proj.shape[0]

    def body(bg_ref, cg_ref, h_ref, w_ref, dy_ref, dbg_ref, dcg_ref, dh_ref, dw_ref, ppad, dpad, dw_scr):
        ppad[0:HALO, :] = jnp.zeros((HALO, LANES), F32)
        ppad[HALO:, :] = cg_ref[...].astype(F32) * h_ref[...].astype(F32)
        dpad[0:s, :] = dy_ref[...].astype(F32) * bg_ref[...].astype(F32)
        dpad[s:, :] = jnp.zeros((HALO, LANES), F32)
        dw_scr[...] = jnp.zeros((SUBLANES, LANES), F32)
        w = w_ref[...]

        def block(bi, carry):
            start = pl.multiple_of(bi * CONV_ROWS, CONV_ROWS)
            rows = pl.ds(start, CONV_ROWS)
            pwin = ppad[pl.ds(start, CONV_ROWS + HALO), :]
            dwin = dpad[pl.ds(start, CONV_ROWS + HALO), :]
            dcvb = dwin[:CONV_ROWS]
            dbg_ref[rows, :] = (dy_ref[rows, :].astype(F32) * _conv3(pwin, w)).astype(BF16)
            dp = jnp.zeros((CONV_ROWS, LANES), F32)
            for k in range(SC_KERNEL):
                dp = dp + w[k:k + 1, :] * _shift_rows(dwin, SC_KERNEL - 1 - k, CONV_ROWS)
                psh = _shift_rows(pwin, HALO - (SC_KERNEL - 1) + k, CONV_ROWS)
                dw_scr[k:k + 1, :] += jnp.sum(dcvb * psh, axis=0, keepdims=True)
            dcg_ref[rows, :] = (dp * h_ref[rows, :].astype(F32)).astype(BF16)
            dh_ref[rows, :] = (dp * cg_ref[rows, :].astype(F32)).astype(BF16)
            return carry

        lax.fori_loop(0, s // CONV_ROWS, block, 0)
        dw_ref[...] = dw_scr[...]

    act = jax.ShapeDtypeStruct((s, SC_WIDTH), BF16)
    return pl.pallas_call(
        body, name=name, out_shape=(act, act, act, jax.ShapeDtypeStruct((SUBLANES, SC_WIDTH), F32)),
        grid=(_CH_TILES,),
        in_specs=[_col_spec(s, _D_BLOCK), _col_spec(s, _D_BLOCK + _CH_TILES), _col_spec(s, _D_BLOCK + 2 * _CH_TILES),
                  _col_spec(SUBLANES, 0), _col_spec(s, 0)],
        out_specs=(_col_spec(s, 0), _col_spec(s, 0), _col_spec(s, 0), _col_spec(SUBLANES, 0)),
        scratch_shapes=[pltpu.VMEM((s + HALO, LANES), F32), pltpu.VMEM((s + HALO, LANES), F32),
                        pltpu.VMEM((SUBLANES, LANES), F32)],
        compiler_params=_params(("parallel",)))(proj, proj, proj, w_pad, dy)


_QK_BLOCK = 2048 // LANES
_QK_BLOCKS = (Q_WIDTH + KV_WIDTH) // LANES


def _swap_halves(t):
    lane = lax.broadcasted_iota(jnp.int32, t.shape, 1)
    low = (lane % HEAD_DIM) < (HEAD_DIM // 2)
    return jnp.where(low, pltpu.roll(t, LANES - HEAD_DIM // 2, axis=1), pltpu.roll(t, HEAD_DIM // 2, axis=1))


def rope_fwd(proj, cos_t, sin_t, name, after=None):
    s = proj.shape[0]

    def body(t_ref, c_ref, s_ref, *rest):
        t = t_ref[...].astype(F32)
        rest[-1][...] = (t * c_ref[...] + _swap_halves(t) * s_ref[...]).astype(BF16)

    tr = min(ROPE_TILE, s)
    tab = pl.BlockSpec((tr, LANES), lambda i, j: (i, 0))
    extra = () if after is None else (after,)
    return pl.pallas_call(
        body, name=name, out_shape=jax.ShapeDtypeStruct((s, Q_WIDTH + KV_WIDTH), BF16),
        grid=(s // tr, _QK_BLOCKS),
        in_specs=[pl.BlockSpec((tr, LANES), lambda i, j: (i, _QK_BLOCK + j)), tab, tab]
        + [pl.BlockSpec(memory_space=pl.ANY) for _ in extra],
        out_specs=pl.BlockSpec((tr, LANES), lambda i, j: (i, j)),
        compiler_params=_params(("parallel", "parallel")))(proj, cos_t, sin_t, *extra)


def rope_bwd(d_cur, d_prev, cos_t, sin_t, name):
    s, w = d_cur.shape

    def body(a_ref, b_ref, c_ref, s_ref, o_ref):
        d = a_ref[...] + b_ref[...]
        o_ref[...] = (d * c_ref[...] + _swap_halves(d) * s_ref[...]).astype(BF16)

    tr = min(ROPE_TILE, s)
    tab = pl.BlockSpec((tr, LANES), lambda i, j: (i, 0))
    blk = pl.BlockSpec((tr, LANES), lambda i, j: (i, j))
    return pl.pallas_call(
        body, name=name, out_shape=jax.ShapeDtypeStruct((s, w), BF16), grid=(s // tr, w // LANES),
        in_specs=[blk, blk, tab, tab], out_specs=blk,
        compiler_params=_params(("parallel", "parallel")))(d_cur, d_prev, cos_t, sin_t)


_GROUP = N_Q_HEADS // N_KV_HEADS
_NEG = -1e30


def _attn_specs():
    q_spec = pl.BlockSpec((_GROUP, WINDOW, HEAD_DIM), lambda h, n: (h, n, 0))
    cur = pl.BlockSpec((1, WINDOW, HEAD_DIM), lambda h, n: (h, n, 0))
    prev = pl.BlockSpec((1, WINDOW, HEAD_DIM), lambda h, n: (h, jnp.maximum(n - 1, 0), 0))
    sink = pl.BlockSpec((_GROUP, 1, LANES), lambda h, n: (h, 0, 0))
    return q_spec, cur, prev, sink


def _attn_valid(n):
    qi = lax.broadcasted_iota(jnp.int32, (WINDOW, 2 * WINDOW), 0)
    kj = lax.broadcasted_iota(jnp.int32, (WINDOW, 2 * WINDOW), 1)
    delta = qi + WINDOW - kj
    return (delta >= 0) & (delta < WINDOW) & ((kj >= WINDOW) | (n > 0))


def _attn_probs(q, kcat, valid, sink_row):
    sc = lax.dot_general(q, kcat, _DIMS["nt"], preferred_element_type=F32) * (HEAD_DIM ** -0.5)
    sc = jnp.where(valid, sc, _NEG)
    sink = jnp.max(sink_row, axis=-1, keepdims=True)
    m = jnp.maximum(jnp.max(sc, axis=-1, keepdims=True), sink)
    p = jnp.where(valid, jnp.exp(sc - m), 0.0)
    es = jnp.exp(sink - m)
    inv = 1.0 / (jnp.sum(p, axis=-1, keepdims=True) + es)
    return p * inv, es * inv


def attention_fwd(qh, kh, vh, sinks_b, name):
    s = qh.shape[1]

    def body(q_ref, kc_ref, kp_ref, vc_ref, vp_ref, sk_ref, o_ref):
        valid = _attn_valid(pl.program_id(1))
        kcat = jnp.concatenate([kp_ref[0], kc_ref[0]], axis=0)
        vcat = jnp.concatenate([vp_ref[0], vc_ref[0]], axis=0)
        for g in range(_GROUP):
            probs, _ = _attn_probs(q_ref[g], kcat, valid, sk_ref[g])
            o_ref[g] = jnp.dot(probs.astype(BF16), vcat, preferred_element_type=F32).astype(BF16)

    q_spec, cur, prev, sink = _attn_specs()
    return pl.pallas_call(
        body, name=name, out_shape=jax.ShapeDtypeStruct(qh.shape, BF16), grid=(N_KV_HEADS, s // WINDOW),
        in_specs=[q_spec, cur, prev, cur, prev, sink], out_specs=q_spec,
        compiler_params=_params(("parallel", "parallel")))(qh, kh, kh, vh, vh, sinks_b)


def attention_bwd(qh, kh, vh, sinks_b, doh, name):
    s = qh.shape[1]

    def body(q_ref, kc_ref, kp_ref, vc_ref, vp_ref, sk_ref, do_ref, dq_ref, dkc_ref, dkp_ref, dvc_ref, dvp_ref, ds_ref):
        n = pl.program_id(1)
        valid = _attn_valid(n)
        kcat = jnp.concatenate([kp_ref[0], kc_ref[0]], axis=0)
        vcat = jnp.concatenate([vp_ref[0], vc_ref[0]], axis=0)
        dk = jnp.zeros((2 * WINDOW, HEAD_DIM), F32)
        dv = jnp.zeros((2 * WINDOW, HEAD_DIM), F32)
        for g in range(_GROUP):
            q = q_ref[g]
            do = do_ref[g]
            probs, ps = _attn_probs(q, kcat, valid, sk_ref[g])
            dprobs = lax.dot_general(do, vcat, _DIMS["nt"], preferred_element_type=F32)
            dv = dv + lax.dot_general(probs.astype(BF16), do, _DIMS["tn"], preferred_element_type=F32)
            rs = jnp.sum(probs * dprobs, axis=-1, keepdims=True)
            dsb = (probs * (dprobs - rs) * (HEAD_DIM ** -0.5)).astype(BF16)
            dq_ref[g] = jnp.dot(dsb, kcat, preferred_element_type=F32)
            dk = dk + lax.dot_general(dsb, q, _DIMS["tn"], preferred_element_type=F32)
            dsink = jnp.broadcast_to(-jnp.sum(ps * rs, axis=0, keepdims=True), (1, LANES))

            @pl.when(n == 0)
            def _():
                ds_ref[g] = dsink

            @pl.when(n > 0)
            def _():
                ds_ref[g] += dsink

        dkp_ref[0] = dk[:WINDOW]
        dkc_ref[0] = dk[WINDOW:]
        dvp_ref[0] = dv[:WINDOW]
        dvc_ref[0] = dv[WINDOW:]

    q_spec, cur, prev, sink = _attn_specs()
    kv_shape = jax.ShapeDtypeStruct(kh.shape, F32)
    return pl.pallas_call(
        body, name=name,
        out_shape=(jax.ShapeDtypeStruct(qh.shape, F32), kv_shape, kv_shape, kv_shape, kv_shape,
                   jax.ShapeDtypeStruct(sinks_b.shape, F32)),
        grid=(N_KV_HEADS, s // WINDOW), in_specs=[q_spec, cur, prev, cur, prev, sink, q_spec],
        out_specs=(q_spec, cur, cur, cur, cur, sink),
        compiler_params=_params(("parallel", "arbitrary")))(qh, kh, kh, vh, vh, sinks_b, doh)


def _to_heads(t, heads):
    return t.reshape(t.shape[0], heads, HEAD_DIM).transpose(1, 0, 2)


def _from_heads(t):
    return t.transpose(1, 0, 2).reshape(t.shape[1], t.shape[0] * HEAD_DIM)


def _shift_window(t):
    return jnp.concatenate([t[:, WINDOW:], jnp.zeros_like(t[:, :WINDOW])], axis=1)


def merge_fwd(zg, branches, name):
    s = zg.shape[0]

    def body(zg_ref, b0, b1, b2, b3, o_ref):
        acc = jnp.zeros((TOKEN_TILE, D_MODEL), F32)
        for n, b_ref in enumerate((b0, b1, b2, b3)):
            gate = _sigmoid(zg_ref[:, n * D_MODEL:(n + 1) * D_MODEL].astype(F32))
            acc = acc + gate * b_ref[...].astype(F32)
        o_ref[...] = acc.astype(BF16)

    tile = pl.BlockSpec((TOKEN_TILE, D_MODEL), lambda i: (i, 0))
    wide = pl.BlockSpec((TOKEN_TILE, N_BRANCH * D_MODEL), lambda i: (i, 0))
    return pl.pallas_call(
        body, name=name, out_shape=jax.ShapeDtypeStruct((s, D_MODEL), BF16), grid=(s // TOKEN_TILE,),
        in_specs=[wide, tile, tile, tile, tile], out_specs=tile,
        compiler_params=_params(("parallel",)))(zg, *branches)


def merge_bwd(zg, branches, dm, name):
    s = zg.shape[0]

    def body(zg_ref, b0, b1, b2, b3, dm_ref, dzg_ref, d0, d1, d2, d3):
        dmv = dm_ref[...].astype(F32)
        for n, (b_ref, d_ref) in enumerate(((b0, d0), (b1, d1), (b2, d2), (b3, d3))):
            cols = slice(n * D_MODEL, (n + 1) * D_MODEL)
            gate = _sigmoid(zg_ref[:, cols].astype(F32))
            d_ref[...] = (gate * dmv).astype(BF16)
            dzg_ref[:, cols] = (dmv * b_ref[...].astype(F32) * gate * (1.0 - gate)).astype(BF16)

    tile = pl.BlockSpec((TOKEN_TILE, D_MODEL), lambda i: (i, 0))
    wide = pl.BlockSpec((TOKEN_TILE, N_BRANCH * D_MODEL), lambda i: (i, 0))
    act = jax.ShapeDtypeStruct((s, D_MODEL), BF16)
    return pl.pallas_call(
        body, name=name, out_shape=(jax.ShapeDtypeStruct((s, N_BRANCH * D_MODEL), BF16), act, act, act, act),
        grid=(s // TOKEN_TILE,), in_specs=[wide, tile, tile, tile, tile, tile],
        out_specs=(wide, tile, tile, tile, tile), compiler_params=_params(("parallel",)))(zg, *branches, dm)


def swiglu_fwd(gu, name):
    s = gu.shape[0]

    def body(g_ref, u_ref, o_ref):
        gate = g_ref[...].astype(F32)
        o_ref[...] = (gate * _sigmoid(gate) * u_ref[...].astype(F32)).astype(BF16)

    return pl.pallas_call(
        body, name=name, out_shape=jax.ShapeDtypeStruct((s, D_FF), BF16), grid=(s // TOKEN_TILE,),
        in_specs=[pl.BlockSpec((TOKEN_TILE, D_FF), lambda i: (i, 0)), pl.BlockSpec((TOKEN_TILE, D_FF), lambda i: (i, 1))],
        out_specs=pl.BlockSpec((TOKEN_TILE, D_FF), lambda i: (i, 0)), compiler_params=_params(("parallel",)))(gu, gu)


def swiglu_bwd(gu, dact, name):
    s = gu.shape[0]

    def body(g_ref, u_ref, da_ref, o_ref):
        gate = g_ref[...].astype(F32)
        sg = _sigmoid(gate)
        da = da_ref[...].astype(F32)
        o_ref[:, :D_FF] = (da * u_ref[...].astype(F32) * sg * (1.0 + gate * (1.0 - sg))).astype(BF16)
        o_ref[:, D_FF:] = (da * gate * sg).astype(BF16)

    half = pl.BlockSpec((TOKEN_TILE, D_FF), lambda i: (i, 0))
    return pl.pallas_call(
        body, name=name, out_shape=jax.ShapeDtypeStruct((s, 2 * D_FF), BF16), grid=(s // TOKEN_TILE,),
        in_specs=[half, pl.BlockSpec((TOKEN_TILE, D_FF), lambda i: (i, 1)), half],
        out_specs=pl.BlockSpec((TOKEN_TILE, 2 * D_FF), lambda i: (i, 0)),
        compiler_params=_params(("parallel",)))(gu, gu, dact)


ADAMW_BLOCK_BYTES = 1 << 20


def adamw(parts, w, m, v, name):
    n_parts, r, c = w.shape
    tr = _divisor_tile(r, max(SUBLANES, ADAMW_BLOCK_BYTES // (4 * c)), SUBLANES)
    tiles = r // tr

    def part_spec(j):
        return pl.BlockSpec((N_DEV, tr, c), lambda i: (0, jnp.clip(i - j * tiles, 0, tiles - 1), 0))

    def body(*refs):
        p_refs = refs[:n_parts]
        w_ref, m_ref, v_ref, g_ref, d_ref, nm_ref, nv_ref = refs[n_parts:]
        which = pl.program_id(0) // tiles
        g = None
        for j, p_ref in enumerate(p_refs):
            gj = p_ref[0].astype(F32)
            for i in range(1, N_DEV):
                gj = gj + p_ref[i].astype(F32)
            g = gj if g is None else jnp.where(which == j, gj, g)
        nm = ADAM_B1 * m_ref[0] + (1.0 - ADAM_B1) * g
        nv = ADAM_B2 * v_ref[0] + (1.0 - ADAM_B2) * (g * g)
        m_hat = nm / (1.0 - ADAM_B1 ** ADAM_STEP)
        v_hat = nv / (1.0 - ADAM_B2 ** ADAM_STEP)
        g_ref[0] = g
        d_ref[0] = -ADAM_LR * (m_hat / (jnp.sqrt(v_hat) + ADAM_EPS) + ADAM_WD * w_ref[0])
        nm_ref[0] = nm
        nv_ref[0] = nv

    tile = pl.BlockSpec((1, tr, c), lambda i: (i // tiles, i % tiles, 0))
    shape = jax.ShapeDtypeStruct(w.shape, F32)
    return pl.pallas_call(
        body, name=name, out_shape=(shape, shape, shape, shape), grid=(n_parts * tiles,),
        in_specs=[part_spec(j) for j in range(n_parts)] + [tile, tile, tile],
        out_specs=(tile, tile, tile, tile), compiler_params=_params(("parallel",)))(*parts, w, m, v)


_RELATIONS = [(a, b, e) for a in (0, 1) for b in (0, 1) for e in (0, 1)][1:]


_HBM_SPEC = pl.BlockSpec(memory_space=pltpu.HBM)
_SEM_SPEC = pl.BlockSpec(memory_space=pltpu.SEMAPHORE)
_ANY_SPEC = pl.BlockSpec(memory_space=pl.ANY)
_DATAFLOW = pltpu.SideEffectType.DATAFLOW_SIDE_EFFECTING


_OTHER_CHIPS = [(1, 0), (0, 1), (1, 1)]
_FIRST_LEVEL = [(0, 0, 1)] + [(a, b, 0) for a, b in _OTHER_CHIPS]


def _remote_copies(ins, lands, send_sems, recv_sems, scatter, relations):
    x, y, c = lax.axis_index("x"), lax.axis_index("y"), lax.axis_index("c")
    me = 4 * x + 2 * y + c
    copies = []
    for t in range(len(ins)):
        for k, (a, b, e) in enumerate(relations):
            px, py, pc = (x + a) % 2, (y + b) % 2, (c + e) % 2
            src = ins[t].at[4 * px + 2 * py + pc] if scatter[t] else ins[t]
            copies.append(pltpu.make_async_remote_copy(
                src_ref=src, dst_ref=lands[t].at[me], send_sem=send_sems.at[t * len(relations) + k],
                recv_sem=recv_sems.at[t * len(relations) + k],
                device_id=(px, py, pc), device_id_type=pl.DeviceIdType.MESH))
    return copies


def _forward_copies(lands, send_sems, recv_sems):
    x, y, c = lax.axis_index("x"), lax.axis_index("y"), lax.axis_index("c")
    copies = []
    for t in range(len(lands)):
        for k, (a, b) in enumerate(_OTHER_CHIPS):
            slot = lands[t].at[4 * ((x + a) % 2) + 2 * ((y + b) % 2) + c]
            copies.append(pltpu.make_async_remote_copy(
                src_ref=slot, dst_ref=slot, send_sem=send_sems.at[t * len(_OTHER_CHIPS) + k],
                recv_sem=recv_sems.at[t * len(_OTHER_CHIPS) + k],
                device_id=(x, y, 1 - c), device_id_type=pl.DeviceIdType.MESH))
    return copies


def exchange_start(arrays, scatter, name, after=None, relations=_RELATIONS):
    n = len(arrays)
    n_rel = len(relations)
    land_shapes = [a.shape if scatter[t] else (N_DEV,) + a.shape for t, a in enumerate(arrays)]

    def body(*refs):
        ins, lands = refs[:n], refs[n:2 * n]
        send_sems, recv_sems = refs[-2 * n - 3], refs[-2 * n - 2]
        token = refs[-1]
        for cp in _remote_copies(ins, lands, send_sems, recv_sems, scatter, relations):
            cp.start()
        token[...] = jnp.zeros_like(token)

    sems = pltpu.SemaphoreType.DMA((n * n_rel,))
    out_shape = ((sems, sems) + tuple(pltpu.HBM(a.shape, a.dtype) for a in arrays)
                 + tuple(pltpu.HBM(s, a.dtype) for s, a in zip(land_shapes, arrays))
                 + (jax.ShapeDtypeStruct((SUBLANES, LANES), F32),))
    operands = [pltpu.with_memory_space_constraint(a, pltpu.HBM) for a in arrays]
    operands += [pltpu.with_memory_space_constraint(lax.empty(s, a.dtype), pltpu.HBM) for s, a in zip(land_shapes, arrays)]
    in_specs = [_HBM_SPEC] * (2 * n)
    if after is not None:
        operands.append(after)
        in_specs.append(_ANY_SPEC)
    res = pl.pallas_call(
        body, name=name, out_shape=out_shape, in_specs=in_specs,
        out_specs=(_SEM_SPEC, _SEM_SPEC) + (_HBM_SPEC,) * (2 * n) + (pl.BlockSpec(memory_space=pltpu.VMEM),),
        input_output_aliases={i: 2 + i for i in range(2 * n)},
        compiler_params=pltpu.CompilerParams(has_side_effects=_DATAFLOW))(*operands)
    handle = (res[0], res[1], res[2:2 + n], res[2 + n:2 + 2 * n], tuple(scatter), relations)
    return handle, res[-1]


def exchange_wait(handle, after, name):
    send_sems, recv_sems, sources, lands, scatter, relations = handle
    n = len(sources)

    def body(*refs):
        ins, lzs = refs[:n], refs[n:2 * n]
        send_ref, recv_ref = refs[2 * n], refs[2 * n + 1]
        for cp in _remote_copies(ins, lzs, send_ref, recv_ref, scatter, relations):
            cp.wait_send()
            cp.wait_recv()

    out_shape = (tuple(pltpu.HBM(a.shape, a.dtype) for a in sources) + tuple(pltpu.HBM(a.shape, a.dtype) for a in lands))
    res = pl.pallas_call(
        body, name=name, out_shape=out_shape, in_specs=[_HBM_SPEC] * (2 * n) + [_SEM_SPEC, _SEM_SPEC, _ANY_SPEC],
        out_specs=(_HBM_SPEC,) * (2 * n), input_output_aliases={i: i for i in range(2 * n)},
        compiler_params=pltpu.CompilerParams(has_side_effects=_DATAFLOW))(*sources, *lands, send_sems, recv_sems, after)
    return res[n:]


def forward_start(lands, name):
    n = len(lands)

    def body(*refs):
        send_sems, recv_sems, token = refs[n], refs[n + 1], refs[-1]
        for cp in _forward_copies(refs[:n], send_sems, recv_sems):
            cp.start()
        token[...] = jnp.zeros_like(token)

    sems = pltpu.SemaphoreType.DMA((n * len(_OTHER_CHIPS),))
    res = pl.pallas_call(
        body, name=name,
        out_shape=(sems, sems) + tuple(pltpu.HBM(a.shape, a.dtype) for a in lands)
        + (jax.ShapeDtypeStruct((SUBLANES, LANES), F32),),
        in_specs=[_HBM_SPEC] * n,
        out_specs=(_SEM_SPEC, _SEM_SPEC) + (_HBM_SPEC,) * n + (pl.BlockSpec(memory_space=pltpu.VMEM),),
        input_output_aliases={i: 2 + i for i in range(n)},
        compiler_params=pltpu.CompilerParams(has_side_effects=_DATAFLOW))(*lands)
    return (res[0], res[1], res[2:2 + n]), res[-1]


def forward_wait(handle, after, name):
    send_sems, recv_sems, lands = handle
    n = len(lands)

    def body(*refs):
        for cp in _forward_copies(refs[:n], refs[n], refs[n + 1]):
            cp.wait_send()
            cp.wait_recv()

    return pl.pallas_call(
        body, name=name, out_shape=tuple(pltpu.HBM(a.shape, a.dtype) for a in lands),
        in_specs=[_HBM_SPEC] * n + [_SEM_SPEC, _SEM_SPEC, _ANY_SPEC], out_specs=(_HBM_SPEC,) * n,
        input_output_aliases={i: i for i in range(n)},
        compiler_params=pltpu.CompilerParams(has_side_effects=_DATAFLOW))(*lands, send_sems, recv_sems, after)


def _place_own(landed, own, me):
    return lax.dynamic_update_index_in_dim(landed, own, me, 0)


_SMALL = ("norm_mix", "sg_ln_g", "sg_ln_b", "sg_w", "sg_b", "cv_b", "cv_ln_g", "cv_ln_b", "attn_sinks", "norm_ffn",
          "norm_final")
_PACK_UNIT = SUBLANES * LANES


def _pack(tensors):
    rows = []
    for t in tensors:
        flat = t.reshape(-1)
        pad = (-flat.shape[0]) % _PACK_UNIT
        rows.append(jnp.pad(flat, (0, pad)).reshape(-1, LANES))
    return jnp.concatenate(rows, axis=0)


def _unpack(packed, like):
    out, row = [], 0
    for t in like:
        size = 1
        for d in t.shape:
            size *= d
        rows = -(-size // _PACK_UNIT) * SUBLANES
        out.append(packed[row:row + rows].reshape(-1)[:size].reshape(t.shape))
        row += rows
    return out


def _layer_fwd(l, x, p, late_params, mid_hook=None, ffn_hook=None):
    tag = f"l{l}_"
    xn = rmsnorm_fwd(x, p["norm_mix"], tag + "norm_mix")
    proj = matmul(xn, p["w_in_t_a"], "nt", BF16, tag + "proj_a", tm_cap=1024, tn_cap=2176)
    zg = matmul(xn, p["w_in_t_g"], "nt", BF16, tag + "proj_g", tm_cap=1024, tn_cap=2048)
    y_a = mixer_a_fwd(proj, p["sg_ln_g"], p["sg_ln_b"], p["sg_w"], p["sg_b"], tag + "mix_a")
    conv = conv_b_fwd(proj, p["cv_w"], p["cv_b"], tag + "conv_b")
    y_b = ln_silu_fwd(conv, p["cv_ln_g"], p["cv_ln_b"], tag + "ln_silu")
    token = mid_hook(y_b) if mid_hook is not None else None
    qk = rope_fwd(proj, p["cos"], p["sin"], tag + "rope", after=token)
    qh = _to_heads(qk[:, :Q_WIDTH], N_Q_HEADS)
    kh = _to_heads(qk[:, Q_WIDTH:], N_KV_HEADS)
    vh = _to_heads(proj[:, 2688:2816], N_KV_HEADS)
    oh = attention_fwd(qh, kh, vh, p["sinks"], tag + "attn")
    y_c = _from_heads(oh)
    y_d = conv_d_fwd(proj, p["sc_w"], tag + "conv_d")
    ys = (y_a, y_b, y_c, y_d)
    p = {**p, **late_params(y_d)}
    branches = tuple(matmul(ys[n], p["w_branch"][n], "nn", BF16, tag + f"branch{n}", tm_cap=1024, tn_cap=1024)
                     for n in range(N_BRANCH))
    merged = merge_fwd(zg, branches, tag + "merge")
    x_mid = matmul(merged, p["w_out"], "nn", F32, tag + "out", add=x, tm_cap=1024, tn_cap=1024)
    token = ffn_hook(x_mid) if ffn_hook is not None else None
    hn = rmsnorm_fwd(x_mid, p["norm_ffn"], tag + "norm_ffn")
    gu = matmul(hn, p["w_gate_up_t"], "nt", BF16, tag + "gate_up", tm_cap=512, tn_cap=2816, after=token)
    act = swiglu_fwd(gu, tag + "swiglu")
    x_out = matmul(act, p["w_down"], "nn", F32, tag + "down", add=x_mid, tm_cap=512, tn_cap=1024)
    saved = dict(x=x, xn=xn, proj=proj, zg=zg, conv=conv, qh=qh, kh=kh, vh=vh, ys=ys, branches=branches,
                 merged=merged, x_mid=x_mid, hn=hn, gu=gu, act=act)
    return x_out, saved, p


def _layer_bwd(l, dx_out, p, sv, emit, after=None):
    tag = f"l{l}_b_"
    g = {}
    dact = matmul(dx_out, p["w_down"], "nt", BF16, tag + "dact", after=after, tm_cap=512, tn_cap=2816)
    dw_down = matmul(sv["act"], dx_out, "tn", BF16, tag + "dw_down", tm_cap=1408, tn_cap=512)
    dgu = swiglu_bwd(sv["gu"], dact, tag + "swiglu")
    dhn = matmul(dgu, p["w_gate_up_t"], "nn", BF16, tag + "dhn", tm_cap=512, tn_cap=512)
    dw_gate_up = matmul(dgu, sv["hn"], "tn", BF16, tag + "dw_gate_up", tm_cap=1408, tn_cap=1024)
    token = emit("a", {"w_gate_up": dw_gate_up, "w_down": dw_down})
    dx_mid, g["norm_ffn"] = rmsnorm_bwd(sv["x_mid"], p["norm_ffn"], dhn, dx_out, tag + "norm_ffn")
    dmerged = matmul(dx_mid, p["w_out"], "nt", BF16, tag + "dmerged", after=token, tm_cap=1024, tn_cap=1024)
    dw_out = matmul(sv["merged"], dx_mid, "tn", BF16, tag + "dw_out", tm_cap=1024, tn_cap=512)
    dzg, *dbranches = merge_bwd(sv["zg"], sv["branches"], dmerged, tag + "merge")
    dys = [matmul(dbranches[n], p["w_branch"][n], "nt", BF16, tag + f"dy{n}", tm_cap=1024, tn_cap=512)
           for n in range(N_BRANCH)]
    dw_branch = jnp.stack(
        [matmul(sv["ys"][n], dbranches[n], "tn", BF16, tag + f"dw_branch{n}", tm_cap=512, tn_cap=1024)
         for n in range(N_BRANCH)])
    token = emit("b", {"w_branch": dw_branch, "w_out": dw_out})
    proj = sv["proj"]
    dz_a, g["sg_ln_g"], g["sg_ln_b"], g["sg_w"], dsb = mixer_a_bwd(
        proj, dys[0], p["sg_ln_g"], p["sg_ln_b"], p["sg_w"], p["sg_b"], tag + "mix_a")
    g["sg_b"] = dsb[:, :, 0]
    dconv, g["cv_ln_g"], g["cv_ln_b"] = ln_silu_bwd(sv["conv"], dys[1], p["cv_ln_g"], p["cv_ln_b"], tag + "ln_silu")
    da, dgate, dcw, g["cv_b"] = conv_b_bwd(proj, p["cv_w"], dconv, tag + "conv_b")
    g["cv_w"] = dcw[:CV_KERNEL]
    doh = _to_heads(dys[2], N_Q_HEADS)
    dqh, dkc, dkp, dvc, dvp, dsk = attention_bwd(sv["qh"], sv["kh"], sv["vh"], p["sinks"], doh, tag + "attn")
    g["attn_sinks"] = dsk[:, 0, 0]
    dqk_cur = jnp.concatenate([_from_heads(dqh), _from_heads(dkc)], axis=1)
    dqk_prev = jnp.concatenate([jnp.zeros((SEQ, Q_WIDTH), F32), _from_heads(_shift_window(dkp))], axis=1)
    dqk = rope_bwd(dqk_cur, dqk_prev, p["cos"], -p["sin"], tag + "rope")
    dv = (_from_heads(dvc) + _from_heads(_shift_window(dvp))).astype(BF16)
    dbg, dcg, dh, dsw = conv_d_bwd(proj, p["sc_w"], dys[3], tag + "conv_d")
    g["sc_w"] = dsw[:SC_KERNEL]
    dproj = jnp.concatenate([dz_a, da, dgate, dqk, dv, dbg, dcg, dh], axis=1)
    dxn = matmul(dproj, p["w_in_t_a"], "nn", F32, tag + "dxn_a", after=token, tm_cap=512, tn_cap=512)
    dxn = matmul(dzg, p["w_in_t_g"], "nn", F32, tag + "dxn_g", add=dxn, tm_cap=512, tn_cap=512)
    g["w_in"] = jnp.concatenate(
        [matmul(dproj, sv["xn"], "tn", BF16, tag + "dw_in_a", tm_cap=2176, tn_cap=512),
         matmul(dzg, sv["xn"], "tn", BF16, tag + "dw_in_g", tm_cap=1024, tn_cap=1024)], axis=0)
    dx_in, g["norm_mix"] = rmsnorm_bwd(sv["x"], p["norm_mix"], dxn, dx_mid, tag + "norm_mix")
    return dx_in, g


_EARLY = ("w_in", "cv_w", "sc_w")
_LATE = ("w_branch", "w_out", "w_gate_up", "w_down")


_TRANSPOSED = ("w_in", "w_gate_up")


def _shard_view(name, t):
    return jnp.swapaxes(t, 1, 2) if name in _TRANSPOSED else t


def _full_weight(name, t):
    if name in ("w_out", "w_down") + _TRANSPOSED:
        return t.reshape(-1, t.shape[-1])
    if name == "w_branch":
        return t.transpose(1, 2, 0, 3).reshape(N_BRANCH, SG_WIDTH, D_MODEL)
    return t.transpose(1, 0, 2).reshape(t.shape[1], -1)


def _to_blocks(name, full):
    if name in ("w_out", "w_down") + _TRANSPOSED:
        return full.reshape(N_DEV, -1, full.shape[-1])
    if name == "w_branch":
        return full.reshape(N_BRANCH, SG_WIDTH, N_DEV, -1).transpose(2, 0, 1, 3)
    return full.reshape(full.shape[0], N_DEV, -1).transpose(1, 0, 2)


def _rope_tables():
    pos = jnp.arange(SEQ, dtype=F32)
    inv_freq = 1.0 / (ROPE_THETA ** (jnp.arange(0, HEAD_DIM, 2, dtype=F32) / HEAD_DIM))
    ang = pos[:, None] * inv_freq[None, :]
    cos, sin = jnp.cos(ang), jnp.sin(ang)
    reps = LANES // HEAD_DIM
    return jnp.tile(jnp.concatenate([cos, cos], axis=1), (1, reps)), jnp.tile(jnp.concatenate([-sin, sin], axis=1), (1, reps))


def kernel(x, norm_mix, w_in, sg_ln_g, sg_ln_b, sg_w, sg_b, cv_w, cv_b, cv_ln_g, cv_ln_b, attn_sinks, sc_w, w_branch, w_out, norm_ffn, w_gate_up, w_down, norm_final, loss_target, m_norm_mix, m_w_in, m_sg_ln_g, m_sg_ln_b, m_sg_w, m_sg_b, m_cv_w, m_cv_b, m_cv_ln_g, m_cv_ln_b, m_attn_sinks, m_sc_w, m_w_branch, m_w_out, m_norm_ffn, m_w_gate_up, m_w_down, m_norm_final, v_norm_mix, v_w_in, v_sg_ln_g, v_sg_ln_b, v_sg_w, v_sg_b, v_cv_w, v_cv_b, v_cv_ln_g, v_cv_ln_b, v_attn_sinks, v_sc_w, v_w_branch, v_w_out, v_norm_ffn, v_w_gate_up, v_w_down, v_norm_final):
    names = ("norm_mix", "w_in", "sg_ln_g", "sg_ln_b", "sg_w", "sg_b", "cv_w", "cv_b", "cv_ln_g", "cv_ln_b",
             "attn_sinks", "sc_w", "w_branch", "w_out", "norm_ffn", "w_gate_up", "w_down", "norm_final")
    w = dict(zip(names, (norm_mix, w_in, sg_ln_g, sg_ln_b, sg_w, sg_b, cv_w, cv_b, cv_ln_g, cv_ln_b, attn_sinks,
                         sc_w, w_branch, w_out, norm_ffn, w_gate_up, w_down, norm_final)))
    m = dict(zip(names, (m_norm_mix, m_w_in, m_sg_ln_g, m_sg_ln_b, m_sg_w, m_sg_b, m_cv_w, m_cv_b, m_cv_ln_g,
                         m_cv_ln_b, m_attn_sinks, m_sc_w, m_w_branch, m_w_out, m_norm_ffn, m_w_gate_up, m_w_down,
                         m_norm_final)))
    v = dict(zip(names, (v_norm_mix, v_w_in, v_sg_ln_g, v_sg_ln_b, v_sg_w, v_sg_b, v_cv_w, v_cv_b, v_cv_ln_g,
                         v_cv_ln_b, v_attn_sinks, v_sc_w, v_w_branch, v_w_out, v_norm_ffn, v_w_gate_up, v_w_down,
                         v_norm_final)))

    me = 4 * lax.axis_index("x") + 2 * lax.axis_index("y") + lax.axis_index("c")

    gathers, forwards, token = {}, {}, None
    for l in range(DEPTH):
        for group in (_EARLY, _LATE):
            shards = [_shard_view(n, w[n])[l].astype(BF16) for n in group]
            handle, token = exchange_start(shards, [False] * len(group), f"gather_start{l}_{group[0]}", after=token,
                                           relations=_FIRST_LEVEL)
            gathers[(l, group)] = (handle, shards)

    def begin_forward(l, group, after):
        landed = exchange_wait(gathers[(l, group)][0], after, f"gather_wait{l}_{group[0]}")
        forwards[(l, group)], tok = forward_start(landed, f"forward_start{l}_{group[0]}")
        return tok

    def landed_weights(l, group, after):
        landed = forward_wait(forwards[(l, group)], after, f"forward_wait{l}_{group[0]}")
        return {n + "_t" if n in _TRANSPOSED else n: _full_weight(n, _place_own(t, own, me))
                for n, t, own in zip(group, landed, gathers[(l, group)][1])}

    cos_t, sin_t = _rope_tables()

    def early_params(l, after):
        full = landed_weights(l, _EARLY, after)
        return dict(
            norm_mix=w["norm_mix"][l][None], norm_ffn=w["norm_ffn"][l][None],
            w_in_t_a=full["w_in_t"][:PROJ_A], w_in_t_g=full["w_in_t"][PROJ_A:],
            sg_ln_g=w["sg_ln_g"][l][None], sg_ln_b=w["sg_ln_b"][l][None], sg_w=w["sg_w"][l],
            sg_b=jnp.broadcast_to(w["sg_b"][l][:, :, None], (SG_GROUPS, SG_CHUNK, LANES)),
            cv_w=jnp.pad(full["cv_w"].astype(F32), ((0, HALO - CV_KERNEL), (0, 0))),
            cv_b=w["cv_b"][l][None], cv_ln_g=w["cv_ln_g"][l][None], cv_ln_b=w["cv_ln_b"][l][None],
            sinks=jnp.broadcast_to(w["attn_sinks"][l][:, None, None], (N_Q_HEADS, 1, LANES)),
            sc_w=jnp.pad(full["sc_w"].astype(F32), ((0, SUBLANES - SC_KERNEL), (0, 0))),
            cos=cos_t, sin=sin_t)

    params, saved = [None] * DEPTH, [None] * DEPTH
    h = x[0]
    after = begin_forward(0, _EARLY, token)
    for l in range(DEPTH):
        h, saved[l], params[l] = _layer_fwd(
            l, h, early_params(l, after), lambda behind, l=l: landed_weights(l, _LATE, behind),
            mid_hook=lambda behind, l=l: begin_forward(l, _LATE, behind),
            ffn_hook=(lambda behind, l=l: begin_forward(l + 1, _EARLY, behind)) if l + 1 < DEPTH else None)
        after = h
    loss_row, dh, d_norm_final = loss_head(h, w["norm_final"][None], loss_target[0], "loss_head")

    sent = {}

    def emitter(l):
        def emit(group, grads_of):
            send = [_to_blocks(n, grads_of[n].astype(BF16)) for n in grads_of]
            handle, tok = exchange_start(send, [True] * len(send), f"grads_start{l}{group}")
            sent[(l, group)] = (handle, send, tuple(grads_of))
            return tok
        return emit

    grads = [None] * DEPTH
    dh, grads[1] = _layer_bwd(1, dh, params[1], saved[1], emitter(1))
    token = emitter(1)("c", {n: grads[1][n] for n in _EARLY})
    dh, grads[0] = _layer_bwd(0, dh, params[0], saved[0], emitter(0), after=token)
    grad_x = dh[None]

    stacked = {n: jnp.stack([grads[l][n] for l in range(DEPTH)]) for n in _SMALL if n != "norm_final"}
    for n in ("norm_mix", "norm_ffn", "sg_ln_g", "sg_ln_b", "cv_b", "cv_ln_g", "cv_ln_b"):
        stacked[n] = stacked[n][:, 0]
    stacked["norm_final"] = d_norm_final[0]
    no_state = jnp.zeros((1,), F32)
    small_like = [w[n] for n in _SMALL] + [no_state]
    small_part = _pack([stacked[n] for n in _SMALL] + [loss_row[0, :1]])
    send_last = [_to_blocks(n, grads[0][n].astype(BF16)) for n in _EARLY]
    handle_last, token = exchange_start(send_last + [small_part], [True] * len(_EARLY) + [False], "grads_start0c")

    def received(l, group, after):
        handle, send, group_names = sent[(l, group)]
        landed = exchange_wait(handle, after, f"grads_wait{l}{group}")
        return {n: _place_own(t, lax.dynamic_index_in_dim(s, me, 0, keepdims=False), me)
                for n, t, s in zip(group_names, landed, send)}

    out_g, out_d, out_m, out_v = {}, {}, {}, {}

    def update(n, by_layer):
        shape = _shard_view(n, w[n]).shape
        view = (DEPTH, w[n].size // (DEPTH * shape[-1]), shape[-1])
        parts = [t.reshape((N_DEV,) + view[1:]) for t in by_layer]
        res = adamw(parts, *[_shard_view(n, t).reshape(view) for t in (w[n], m[n], v[n])], "adamw_" + n)
        out_g[n], out_d[n], out_m[n], out_v[n] = (_shard_view(n, t.reshape(shape)) for t in res)
        return res[0]

    behind = token
    for group in ("a", "b"):
        r1 = received(1, group, behind)
        r0 = received(0, group, next(iter(r1.values())))
        for n in r0:
            behind = update(n, [r0[n], r1[n]])
    r1 = received(1, "c", behind)
    landed = exchange_wait(handle_last, next(iter(r1.values())), "grads_wait0c")
    for n, t, s in zip(_EARLY, landed, send_last):
        update(n, [_place_own(t, lax.dynamic_index_in_dim(s, me, 0, keepdims=False), me), r1[n]])
    res = adamw([_place_own(landed[-1], small_part, me)], _pack(small_like)[None],
                _pack([m[n] for n in _SMALL] + [no_state])[None], _pack([v[n] for n in _SMALL] + [no_state])[None],
                "adamw_small")
    for store, packed in zip((out_g, out_d, out_m, out_v), res):
        for n, t in zip(_SMALL + ("loss",), _unpack(packed[0], small_like)):
            store[n] = t

    loss = out_g["loss"][0]
    return (loss, grad_x, *[out_g[n] for n in names], *[out_d[n] for n in names], *[out_m[n] for n in names],
            *[out_v[n] for n in names])
```

```python
import jax
import jax.numpy as jnp
from jax import lax
from jax.experimental import pallas as pl
from jax.experimental.pallas import tpu as pltpu

F32 = jnp.float32
BF16 = jnp.bfloat16

SEQ = 2048
D_MODEL = 1024
DEPTH = 2
SG_WIDTH = 512
SG_CHUNK = 128
SG_GROUPS = 4
CV_WIDTH = 512
CV_KERNEL = 31
HEAD_DIM = 64
N_Q_HEADS = 8
N_KV_HEADS = 2
Q_WIDTH = 512
KV_WIDTH = 128
WINDOW = 128
SC_WIDTH = 512
SC_KERNEL = 3
N_BRANCH = 4
D_FF = 2816
EPS = 1e-6
ROPE_THETA = 10000.0
PROJ_A = 4352
PROJ_WIDTH = 8448
N_DEV = 8

ADAM_LR = 0.001
ADAM_B1 = 0.9
ADAM_B2 = 0.999
ADAM_EPS = 1e-08
ADAM_WD = 0.01
ADAM_STEP = 10

LANES = 128
SUBLANES = 8
VMEM_LIMIT_BYTES = 48 * 1024 * 1024
HALO = 32
CONV_ROWS = 256
TOKEN_TILE = 256
NORM_TILE = 512
ROPE_TILE = 1024

_SQRT_HALF = 0.7071067811865476
_INV_SQRT_2PI = 0.3989422804014327


def _params(semantics=None):
    return pltpu.CompilerParams(dimension_semantics=semantics, vmem_limit_bytes=VMEM_LIMIT_BYTES)


def _divisor_tile(n, cap, unit):
    best = None
    for t in range(unit, min(n, cap) + 1, unit):
        if n % t == 0:
            best = t
    return best if best is not None else n


_DIMS = {"nn": (((1,), (0,)), ((), ())), "nt": (((1,), (1,)), ((), ())), "tn": (((0,), (0,)), ((), ()))}


def matmul(a, b, mode, out_dtype, name, add=None, tm_cap=512, tn_cap=512, after=None, b_rows=None, out_rows=None,
           into=None, into_row=0):
    if mode == "nn":
        (m, k), n = a.shape, b.shape[1]
        k = b_rows if b_rows is not None else k
    elif mode == "nt":
        (m, k), n = a.shape, (b_rows if b_rows is not None else b.shape[0])
    else:
        (k, m), n = a.shape, b.shape[1]
    tm = _divisor_tile(m, tm_cap, LANES)
    tn = _divisor_tile(n, tn_cap, LANES)
    row0 = into_row // tm
    assert row0 * tm == into_row
    a_spec = pl.BlockSpec((k, tm), lambda i, j: (0, i)) if mode == "tn" else pl.BlockSpec((tm, k), lambda i, j: (i, 0))
    b_spec = pl.BlockSpec((tn, k), lambda i, j: (j, 0)) if mode == "nt" else pl.BlockSpec((k, tn), lambda i, j: (0, j))
    o_spec = pl.BlockSpec((tm, tn), lambda i, j: (i + row0, j))
    dims = _DIMS[mode]

    def body(*refs):
        a_ref, b_ref = refs[0], refs[1]
        o_ref = refs[-1]
        acc = lax.dot_general(a_ref[...].astype(BF16), b_ref[...].astype(BF16), dims, preferred_element_type=F32)
        if add is not None:
            acc = acc + refs[2][...].astype(F32)
        o_ref[...] = acc.astype(out_dtype)

    unread = tuple(t for t in (after, into) if t is not None)
    operands = (a, b) + (() if add is None else (add,)) + unread
    in_specs = [a_spec, b_spec] + ([o_spec] if add is not None else [])
    in_specs += [pl.BlockSpec(memory_space=pl.ANY)] * len(unread)
    aliases = {len(operands) - 1: 0} if into is not None else {}
    return pl.pallas_call(
        body, name=name,
        out_shape=jax.ShapeDtypeStruct((into.shape[0] if into is not None else out_rows or m, n), out_dtype),
        grid=(m // tm, n // tn),
        in_specs=in_specs, out_specs=o_spec, input_output_aliases=aliases,
        compiler_params=_params(("parallel", "parallel")))(*operands)


def _sigmoid(x):
    return 1.0 / (1.0 + jnp.exp(-x))


def _gelu(x):
    return 0.5 * x * (1.0 + lax.erf(x * _SQRT_HALF))


def _gelu_grad(x):
    return 0.5 * (1.0 + lax.erf(x * _SQRT_HALF)) + x * _INV_SQRT_2PI * jnp.exp(-0.5 * x * x)


def _rms_stats(x):
    r = lax.rsqrt(jnp.mean(x * x, axis=-1, keepdims=True) + EPS)
    return x * r, r


def _rms_bwd(dxn, xhat, r, g):
    h = dxn * g
    return r * (h - xhat * jnp.mean(h * xhat, axis=-1, keepdims=True))


def _ln_stats(x):
    mu = jnp.mean(x, axis=-1, keepdims=True)
    xc = x - mu
    rstd = lax.rsqrt(jnp.mean(xc * xc, axis=-1, keepdims=True) + EPS)
    return xc * rstd, rstd


def _ln_bwd(dy, xhat, rstd, g):
    dxhat = dy * g
    return rstd * (dxhat - jnp.mean(dxhat, axis=-1, keepdims=True)
                   - xhat * jnp.mean(dxhat * xhat, axis=-1, keepdims=True))


def _accumulate(ref, value, first):
    @pl.when(first)
    def _():
        ref[...] = value

    @pl.when(jnp.logical_not(first))
    def _():
        ref[...] += value


def _shift_rows(win, shift, n_out):
    n = win.shape[0]
    if shift % n == 0:
        return win[:n_out]
    return pltpu.roll(win, n - shift, axis=0)[:n_out]


def _row_spec(width):
    return pl.BlockSpec((1, width), lambda i: (0, 0))


def rmsnorm_fwd(x, g, name):
    s, d = x.shape

    def body(x_ref, g_ref, o_ref):
        xhat, _ = _rms_stats(x_ref[...])
        o_ref[...] = (xhat * g_ref[...]).astype(BF16)

    tile = pl.BlockSpec((NORM_TILE, d), lambda i: (i, 0))
    return pl.pallas_call(
        body, name=name, out_shape=jax.ShapeDtypeStruct((s, d), BF16), grid=(s // NORM_TILE,),
        in_specs=[tile, _row_spec(d)], out_specs=tile, compiler_params=_params(("parallel",)))(x, g)


def rmsnorm_bwd(x, g, dxn, dres, name):
    s, d = x.shape

    def body(x_ref, g_ref, dxn_ref, dres_ref, dx_ref, dg_ref):
        xhat, r = _rms_stats(x_ref[...])
        dxn_v = dxn_ref[...].astype(F32)
        dx_ref[...] = dres_ref[...] + _rms_bwd(dxn_v, xhat, r, g_ref[...])
        _accumulate(dg_ref, jnp.sum(dxn_v * xhat, axis=0, keepdims=True), pl.program_id(0) == 0)

    tile = pl.BlockSpec((NORM_TILE, d), lambda i: (i, 0))
    return pl.pallas_call(
        body, name=name, out_shape=(jax.ShapeDtypeStruct((s, d), F32), jax.ShapeDtypeStruct((1, d), F32)),
        grid=(s // NORM_TILE,), in_specs=[tile, _row_spec(d), tile, tile], out_specs=(tile, _row_spec(d)),
        compiler_params=_params(("arbitrary",)))(x, g, dxn, dres)


def loss_head(x, g, target, name):
    s, d = x.shape

    def body(x_ref, g_ref, t_ref, loss_ref, dx_ref, dg_ref):
        first = pl.program_id(0) == 0
        xhat, r = _rms_stats(x_ref[...])
        gv = g_ref[...]
        err = xhat * gv - t_ref[...]
        part = 0.5 * jnp.sum(jnp.sum(err * err, axis=-1, keepdims=True), axis=0, keepdims=True) / d
        _accumulate(loss_ref, jnp.broadcast_to(part, (1, LANES)), first)
        dy = err / d
        dx_ref[...] = _rms_bwd(dy, xhat, r, gv)
        _accumulate(dg_ref, jnp.sum(dy * xhat, axis=0, keepdims=True), first)

    tile = pl.BlockSpec((NORM_TILE, d), lambda i: (i, 0))
    return pl.pallas_call(
        body, name=name,
        out_shape=(jax.ShapeDtypeStruct((1, LANES), F32), jax.ShapeDtypeStruct((s, d), F32),
                   jax.ShapeDtypeStruct((1, d), F32)),
        grid=(s // NORM_TILE,), in_specs=[tile, _row_spec(d), tile],
        out_specs=(_row_spec(LANES), tile, _row_spec(d)), compiler_params=_params(("arbitrary",)))(x, g, target)


def _tril_mask():
    row = lax.broadcasted_iota(jnp.int32, (SG_CHUNK, SG_CHUNK), 0)
    col = lax.broadcasted_iota(jnp.int32, (SG_CHUNK, SG_CHUNK), 1)
    return row >= col


def _sg_specs():
    vec = _row_spec(SG_WIDTH)
    mat = pl.BlockSpec((SG_GROUPS, SG_CHUNK, SG_CHUNK), lambda i: (0, 0, 0))
    return vec, mat


def mixer_a_fwd(proj, ln_g, ln_b, w_s, b_s, name):
    s = proj.shape[0]
    chunks = TOKEN_TILE // SG_CHUNK

    def body(z_ref, lg_ref, lb_ref, w_ref, b_ref, o_ref):
        ge = _gelu(z_ref[...].astype(F32))
        u = ge[:, :SG_WIDTH]
        xhat, _ = _ln_stats(ge[:, SG_WIDTH:])
        vn = xhat * lg_ref[...] + lb_ref[...]
        tril = _tril_mask()
        for ci in range(chunks):
            rows = slice(ci * SG_CHUNK, (ci + 1) * SG_CHUNK)
            for g in range(SG_GROUPS):
                cols = slice(g * LANES, (g + 1) * LANES)
                wm = jnp.where(tril, w_ref[g], 0.0).astype(BF16)
                mixed = jnp.dot(wm, vn[rows, cols].astype(BF16), preferred_element_type=F32) + b_ref[g]
                o_ref[rows, cols] = (u[rows, cols] * mixed).astype(BF16)

    vec, mat = _sg_specs()
    return pl.pallas_call(
        body, name=name, out_shape=jax.ShapeDtypeStruct((s, SG_WIDTH), BF16), grid=(s // TOKEN_TILE,),
        in_specs=[pl.BlockSpec((TOKEN_TILE, 2 * SG_WIDTH), lambda i: (i, 0)), vec, vec, mat, mat],
        out_specs=pl.BlockSpec((TOKEN_TILE, SG_WIDTH), lambda i: (i, 0)),
        compiler_params=_params(("parallel",)))(proj, ln_g, ln_b, w_s, b_s)


def mixer_a_bwd(proj, dy, ln_g, ln_b, w_s, b_s, name):
    s = proj.shape[0]
    chunks = TOKEN_TILE // SG_CHUNK

    def body(z_ref, dy_ref, lg_ref, lb_ref, w_ref, b_ref, dz_ref, dlg_ref, dlb_ref, dw_ref, db_ref, du_scr, dvn_scr):
        first = pl.program_id(0) == 0

        @pl.when(first)
        def _():
            dw_ref[...] = jnp.zeros_like(dw_ref)
            db_ref[...] = jnp.zeros_like(db_ref)

        z = z_ref[...].astype(F32)
        ge = _gelu(z)
        u = ge[:, :SG_WIDTH]
        xhat, rstd = _ln_stats(ge[:, SG_WIDTH:])
        lg = lg_ref[...]
        vn = xhat * lg + lb_ref[...]
        dyv = dy_ref[...].astype(F32)
        tril = _tril_mask()
        for ci in range(chunks):
            rows = slice(ci * SG_CHUNK, (ci + 1) * SG_CHUNK)
            for g in range(SG_GROUPS):
                cols = slice(g * LANES, (g + 1) * LANES)
                wm = jnp.where(tril, w_ref[g], 0.0).astype(BF16)
                vg = vn[rows, cols].astype(BF16)
                mixed = jnp.dot(wm, vg, preferred_element_type=F32) + b_ref[g]
                dyb = dyv[rows, cols]
                du_scr[rows, cols] = dyb * mixed
                dmix = dyb * u[rows, cols]
                db_ref[g] += jnp.broadcast_to(jnp.sum(dmix, axis=1, keepdims=True), (SG_CHUNK, LANES))
                dmb = dmix.astype(BF16)
                dwg = lax.dot_general(dmb, vg, _DIMS["nt"], preferred_element_type=F32)
                dw_ref[g] += jnp.where(tril, dwg, 0.0)
                dvn_scr[rows, cols] = lax.dot_general(wm, dmb, _DIMS["tn"], preferred_element_type=F32)
        dvn = dvn_scr[...]
        _accumulate(dlg_ref, jnp.sum(dvn * xhat, axis=0, keepdims=True), first)
        _accumulate(dlb_ref, jnp.sum(dvn, axis=0, keepdims=True), first)
        dvv = _ln_bwd(dvn, xhat, rstd, lg)
        gg = _gelu_grad(z)
        dz_ref[:, :SG_WIDTH] = (du_scr[...] * gg[:, :SG_WIDTH]).astype(BF16)
        dz_ref[:, SG_WIDTH:] = (dvv * gg[:, SG_WIDTH:]).astype(BF16)

    vec, mat = _sg_specs()
    mat_shape = jax.ShapeDtypeStruct((SG_GROUPS, SG_CHUNK, SG_CHUNK), F32)
    vec_shape = jax.ShapeDtypeStruct((1, SG_WIDTH), F32)
    return pl.pallas_call(
        body, name=name,
        out_shape=(jax.ShapeDtypeStruct((s, 2 * SG_WIDTH), BF16), vec_shape, vec_shape, mat_shape, mat_shape),
        grid=(s // TOKEN_TILE,),
        in_specs=[pl.BlockSpec((TOKEN_TILE, 2 * SG_WIDTH), lambda i: (i, 0)),
                  pl.BlockSpec((TOKEN_TILE, SG_WIDTH), lambda i: (i, 0)), vec, vec, mat, mat],
        out_specs=(pl.BlockSpec((TOKEN_TILE, 2 * SG_WIDTH), lambda i: (i, 0)), vec, vec, mat, mat),
        scratch_shapes=[pltpu.VMEM((TOKEN_TILE, SG_WIDTH), F32), pltpu.VMEM((TOKEN_TILE, SG_WIDTH), F32)],
        compiler_params=_params(("arbitrary",)))(proj, dy, ln_g, ln_b, w_s, b_s)


_B_A_BLOCK = 1024 // LANES
_B_G_BLOCK = 1536 // LANES
_CH_TILES = CV_WIDTH // LANES


def _col_spec(s, first_block):
    return pl.BlockSpec((s, LANES), lambda j: (0, first_block + j))


def conv_b_fwd(proj, w_pad, bias, name):
    s = proj.shape[0]

    def body(a_ref, g_ref, w_ref, b_ref, c_ref, upad):
        upad[0:HALO, :] = jnp.zeros((HALO, LANES), F32)
        upad[HALO:, :] = a_ref[...].astype(F32) * _sigmoid(g_ref[...].astype(F32))
        w = w_ref[...]
        bv = b_ref[...]

        def block(bi, carry):
            start = pl.multiple_of(bi * CONV_ROWS, CONV_ROWS)
            win = upad[pl.ds(start, CONV_ROWS + HALO), :]
            acc = jnp.zeros((CONV_ROWS, LANES), F32)
            for k in range(CV_KERNEL):
                acc = acc + w[k:k + 1, :] * _shift_rows(win, HALO - (CV_KERNEL - 1) + k, CONV_ROWS)
            c_ref[pl.ds(start, CONV_ROWS), :] = acc + bv
            return carry

        lax.fori_loop(0, s // CONV_ROWS, block, 0)

    return pl.pallas_call(
        body, name=name, out_shape=jax.ShapeDtypeStruct((s, CV_WIDTH), F32), grid=(_CH_TILES,),
        in_specs=[_col_spec(s, _B_A_BLOCK), _col_spec(s, _B_G_BLOCK), _col_spec(HALO, 0), _col_spec(1, 0)],
        out_specs=_col_spec(s, 0), scratch_shapes=[pltpu.VMEM((s + HALO, LANES), F32)],
        compiler_params=_params(("parallel",)))(proj, proj, w_pad, bias)


def conv_b_bwd(proj, w_pad, dc, name):
    s = proj.shape[0]

    def body(a_ref, g_ref, w_ref, dc_ref, da_ref, dg_ref, dw_ref, db_ref, upad, dpad, dw_scr):
        upad[0:HALO, :] = jnp.zeros((HALO, LANES), F32)
        upad[HALO:, :] = a_ref[...].astype(F32) * _sigmoid(g_ref[...].astype(F32))
        dcv = dc_ref[...]
        dpad[0:s, :] = dcv
        dpad[s:, :] = jnp.zeros((HALO, LANES), F32)
        db_ref[...] = jnp.sum(dcv, axis=0, keepdims=True)
        dw_scr[...] = jnp.zeros((HALO, LANES), F32)
        w = w_ref[...]

        def block(bi, carry):
            start = pl.multiple_of(bi * CONV_ROWS, CONV_ROWS)
            uwin = upad[pl.ds(start, CONV_ROWS + HALO), :]
            dwin = dpad[pl.ds(start, CONV_ROWS + HALO), :]
            dcb = dwin[:CONV_ROWS]
            du = jnp.zeros((CONV_ROWS, LANES), F32)
            for k in range(CV_KERNEL):
                du = du + w[k:k + 1, :] * _shift_rows(dwin, CV_KERNEL - 1 - k, CONV_ROWS)
                ush = _shift_rows(uwin, HALO - (CV_KERNEL - 1) + k, CONV_ROWS)
                dw_scr[k:k + 1, :] += jnp.sum(dcb * ush, axis=0, keepdims=True)
            av = a_ref[pl.ds(start, CONV_ROWS), :].astype(F32)
            sg = _sigmoid(g_ref[pl.ds(start, CONV_ROWS), :].astype(F32))
            da_ref[pl.ds(start, CONV_ROWS), :] = (du * sg).astype(BF16)
            dg_ref[pl.ds(start, CONV_ROWS), :] = (du * av * sg * (1.0 - sg)).astype(BF16)
            return carry

        lax.fori_loop(0, s // CONV_ROWS, block, 0)
        dw_ref[...] = dw_scr[...]

    act = jax.ShapeDtypeStruct((s, CV_WIDTH), BF16)
    return pl.pallas_call(
        body, name=name,
        out_shape=(act, act, jax.ShapeDtypeStruct((HALO, CV_WIDTH), F32), jax.ShapeDtypeStruct((1, CV_WIDTH), F32)),
        grid=(_CH_TILES,),
        in_specs=[_col_spec(s, _B_A_BLOCK), _col_spec(s, _B_G_BLOCK), _col_spec(HALO, 0), _col_spec(s, 0)],
        out_specs=(_col_spec(s, 0), _col_spec(s, 0), _col_spec(HALO, 0), _col_spec(1, 0)),
        scratch_shapes=[pltpu.VMEM((s + HALO, LANES), F32), pltpu.VMEM((s + HALO, LANES), F32),
                        pltpu.VMEM((HALO, LANES), F32)],
        compiler_params=_params(("parallel",)))(proj, proj, w_pad, dc)


def ln_silu_fwd(c, ln_g, ln_b, name):
    s, d = c.shape

    def body(c_ref, g_ref, b_ref, o_ref):
        xhat, _ = _ln_stats(c_ref[...])
        cn = xhat * g_ref[...] + b_ref[...]
        o_ref[...] = (cn * _sigmoid(cn)).astype(BF16)

    tile = pl.BlockSpec((NORM_TILE, d), lambda i: (i, 0))
    return pl.pallas_call(
        body, name=name, out_shape=jax.ShapeDtypeStruct((s, d), BF16), grid=(s // NORM_TILE,),
        in_specs=[tile, _row_spec(d), _row_spec(d)], out_specs=tile,
        compiler_params=_params(("parallel",)))(c, ln_g, ln_b)


def ln_silu_bwd(c, dy, ln_g, ln_b, name):
    s, d = c.shape

    def body(c_ref, dy_ref, g_ref, b_ref, dc_ref, dg_ref, db_ref):
        first = pl.program_id(0) == 0
        xhat, rstd = _ln_stats(c_ref[...])
        gv = g_ref[...]
        cn = xhat * gv + b_ref[...]
        sg = _sigmoid(cn)
        dcn = dy_ref[...].astype(F32) * sg * (1.0 + cn * (1.0 - sg))
        _accumulate(dg_ref, jnp.sum(dcn * xhat, axis=0, keepdims=True), first)
        _accumulate(db_ref, jnp.sum(dcn, axis=0, keepdims=True), first)
        dc_ref[...] = _ln_bwd(dcn, xhat, rstd, gv)

    tile = pl.BlockSpec((NORM_TILE, d), lambda i: (i, 0))
    vec_shape = jax.ShapeDtypeStruct((1, d), F32)
    return pl.pallas_call(
        body, name=name, out_shape=(jax.ShapeDtypeStruct((s, d), F32), vec_shape, vec_shape),
        grid=(s // NORM_TILE,), in_specs=[tile, tile, _row_spec(d), _row_spec(d)],
        out_specs=(tile, _row_spec(d), _row_spec(d)), compiler_params=_params(("arbitrary",)))(c, dy, ln_g, ln_b)


_D_BLOCK = 2816 // LANES


def _conv3(win, w):
    acc = jnp.zeros((CONV_ROWS, LANES), F32)
    for k in range(SC_KERNEL):
        acc = acc + w[k:k + 1, :] * _shift_rows(win, HALO - (SC_KERNEL - 1) + k, CONV_ROWS)
    return acc


def conv_d_fwd(proj, w_pad, name):
    s = proj.shape[0]

    def body(bg_ref, cg_ref, h_ref, w_ref, o_ref, ppad):
        ppad[0:HALO, :] = jnp.zeros((HALO, LANES), F32)
        ppad[HALO:, :] = cg_ref[...].astype(F32) * h_ref[...].astype(F32)
        w = w_ref[...]

        def block(bi, carry):
            start = pl.multiple_of(bi * CONV_ROWS, CONV_ROWS)
            cv = _conv3(ppad[pl.ds(start, CONV_ROWS + HALO), :], w)
            o_ref[pl.ds(start, CONV_ROWS), :] = (bg_ref[pl.ds(start, CONV_ROWS), :].astype(F32) * cv).astype(BF16)
            return carry

        lax.fori_loop(0, s // CONV_ROWS, block, 0)

    return pl.pallas_call(
        body, name=name, out_shape=jax.ShapeDtypeStruct((s, SC_WIDTH), BF16), grid=(_CH_TILES,),
        in_specs=[_col_spec(s, _D_BLOCK), _col_spec(s, _D_BLOCK + _CH_TILES), _col_spec(s, _D_BLOCK + 2 * _CH_TILES),
                  _col_spec(SUBLANES, 0)],
        out_specs=_col_spec(s, 0), scratch_shapes=[pltpu.VMEM((s + HALO, LANES), F32)],
        compiler_params=_params(("parallel",)))(proj, proj, proj, w_pad)


def conv_d_bwd(proj, w_pad, dy, name):
    s = proj.shape[0]

    def body(bg_ref, cg_ref, h_ref, w_ref, dy_ref, dbg_ref, dcg_ref, dh_ref, dw_ref, ppad, dpad, dw_scr):
        ppad[0:HALO, :] = jnp.zeros((HALO, LANES), F32)
        ppad[HALO:, :] = cg_ref[...].astype(F32) * h_ref[...].astype(F32)
        dpad[0:s, :] = dy_ref[...].astype(F32) * bg_ref[...].astype(F32)
        dpad[s:, :] = jnp.zeros((HALO, LANES), F32)
        dw_scr[...] = jnp.zeros((SUBLANES, LANES), F32)
        w = w_ref[...]

        def block(bi, carry):
            start = pl.multiple_of(bi * CONV_ROWS, CONV_ROWS)
            rows = pl.ds(start, CONV_ROWS)
            pwin = ppad[pl.ds(start, CONV_ROWS + HALO), :]
            dwin = dpad[pl.ds(start, CONV_ROWS + HALO), :]
            dcvb = dwin[:CONV_ROWS]
            dbg_ref[rows, :] = (dy_ref[rows, :].astype(F32) * _conv3(pwin, w)).astype(BF16)
            dp = jnp.zeros((CONV_ROWS, LANES), F32)
            for k in range(SC_KERNEL):
                dp = dp + w[k:k + 1, :] * _shift_rows(dwin, SC_KERNEL - 1 - k, CONV_ROWS)
                psh = _shift_rows(pwin, HALO - (SC_KERNEL - 1) + k, CONV_ROWS)
                dw_scr[k:k + 1, :] += jnp.sum(dcvb * psh, axis=0, keepdims=True)
            dcg_ref[rows, :] = (dp * h_ref[rows, :].astype(F32)).astype(BF16)
            dh_ref[rows, :] = (dp * cg_ref[rows, :].astype(F32)).astype(BF16)
            return carry

        lax.fori_loop(0, s // CONV_ROWS, block, 0)
        dw_ref[...] = dw_scr[...]

    act = jax.ShapeDtypeStruct((s, SC_WIDTH), BF16)
    return pl.pallas_call(
        body, name=name, out_shape=(act, act, act, jax.ShapeDtypeStruct((SUBLANES, SC_WIDTH), F32)),
        grid=(_CH_TILES,),
        in_specs=[_col_spec(s, _D_BLOCK), _col_spec(s, _D_BLOCK + _CH_TILES), _col_spec(s, _D_BLOCK + 2 * _CH_TILES),
                  _col_spec(SUBLANES, 0), _col_spec(s, 0)],
        out_specs=(_col_spec(s, 0), _col_spec(s, 0), _col_spec(s, 0), _col_spec(SUBLANES, 0)),
        scratch_shapes=[pltpu.VMEM((s + HALO, LANES), F32), pltpu.VMEM((s + HALO, LANES), F32),
                        pltpu.VMEM((SUBLANES, LANES), F32)],
        compiler_params=_params(("parallel",)))(proj, proj, proj, w_pad, dy)


_QK_BLOCK = 2048 // LANES
_QK_BLOCKS = (Q_WIDTH + KV_WIDTH) // LANES


def _swap_halves(t):
    lane = lax.broadcasted_iota(jnp.int32, t.shape, 1)
    low = (lane % HEAD_DIM) < (HEAD_DIM // 2)
    return jnp.where(low, pltpu.roll(t, LANES - HEAD_DIM // 2, axis=1), pltpu.roll(t, HEAD_DIM // 2, axis=1))


def rope_fwd(proj, cos_t, sin_t, name, after=None):
    s = proj.shape[0]

    def body(t_ref, c_ref, s_ref, *rest):
        t = t_ref[...].astype(F32)
        rest[-1][...] = (t * c_ref[...] + _swap_halves(t) * s_ref[...]).astype(BF16)

    tr = min(ROPE_TILE, s)
    tab = pl.BlockSpec((tr, LANES), lambda i, j: (i, 0))
    extra = () if after is None else (after,)
    return pl.pallas_call(
        body, name=name, out_shape=jax.ShapeDtypeStruct((s, Q_WIDTH + KV_WIDTH), BF16),
        grid=(s // tr, _QK_BLOCKS),
        in_specs=[pl.BlockSpec((tr, LANES), lambda i, j: (i, _QK_BLOCK + j)), tab, tab]
        + [pl.BlockSpec(memory_space=pl.ANY) for _ in extra],
        out_specs=pl.BlockSpec((tr, LANES), lambda i, j: (i, j)),
        compiler_params=_params(("parallel", "parallel")))(proj, cos_t, sin_t, *extra)


def rope_bwd(d_cur, d_prev, cos_t, sin_t, name):
    s, w = d_cur.shape

    def body(a_ref, b_ref, c_ref, s_ref, o_ref):
        d = a_ref[...] + b_ref[...]
        o_ref[...] = (d * c_ref[...] + _swap_halves(d) * s_ref[...]).astype(BF16)

    tr = min(ROPE_TILE, s)
    tab = pl.BlockSpec((tr, LANES), lambda i, j: (i, 0))
    blk = pl.BlockSpec((tr, LANES), lambda i, j: (i, j))
    return pl.pallas_call(
        body, name=name, out_shape=jax.ShapeDtypeStruct((s, w), BF16), grid=(s // tr, w // LANES),
        in_specs=[blk, blk, tab, tab], out_specs=blk,
        compiler_params=_params(("parallel", "parallel")))(d_cur, d_prev, cos_t, sin_t)


_GROUP = N_Q_HEADS // N_KV_HEADS
_NEG = -1e30


def _attn_specs():
    q_spec = pl.BlockSpec((_GROUP, WINDOW, HEAD_DIM), lambda h, n: (h, n, 0))
    cur = pl.BlockSpec((1, WINDOW, HEAD_DIM), lambda h, n: (h, n, 0))
    prev = pl.BlockSpec((1, WINDOW, HEAD_DIM), lambda h, n: (h, jnp.maximum(n - 1, 0), 0))
    sink = pl.BlockSpec((_GROUP, 1, LANES), lambda h, n: (h, 0, 0))
    return q_spec, cur, prev, sink


def _attn_valid(n):
    qi = lax.broadcasted_iota(jnp.int32, (WINDOW, 2 * WINDOW), 0)
    kj = lax.broadcasted_iota(jnp.int32, (WINDOW, 2 * WINDOW), 1)
    delta = qi + WINDOW - kj
    return (delta >= 0) & (delta < WINDOW) & ((kj >= WINDOW) | (n > 0))


def _attn_probs(q, kcat, valid, sink_row):
    sc = lax.dot_general(q, kcat, _DIMS["nt"], preferred_element_type=F32) * (HEAD_DIM ** -0.5)
    sc = jnp.where(valid, sc, _NEG)
    sink = jnp.max(sink_row, axis=-1, keepdims=True)
    m = jnp.maximum(jnp.max(sc, axis=-1, keepdims=True), sink)
    p = jnp.where(valid, jnp.exp(sc - m), 0.0)
    es = jnp.exp(sink - m)
    inv = 1.0 / (jnp.sum(p, axis=-1, keepdims=True) + es)
    return p * inv, es * inv


def attention_fwd(qh, kh, vh, sinks_b, name):
    s = qh.shape[1]

    def body(q_ref, kc_ref, kp_ref, vc_ref, vp_ref, sk_ref, o_ref):
        valid = _attn_valid(pl.program_id(1))
        kcat = jnp.concatenate([kp_ref[0], kc_ref[0]], axis=0)
        vcat = jnp.concatenate([vp_ref[0], vc_ref[0]], axis=0)
        for g in range(_GROUP):
            probs, _ = _attn_probs(q_ref[g], kcat, valid, sk_ref[g])
            o_ref[g] = jnp.dot(probs.astype(BF16), vcat, preferred_element_type=F32).astype(BF16)

    q_spec, cur, prev, sink = _attn_specs()
    return pl.pallas_call(
        body, name=name, out_shape=jax.ShapeDtypeStruct(qh.shape, BF16), grid=(N_KV_HEADS, s // WINDOW),
        in_specs=[q_spec, cur, prev, cur, prev, sink], out_specs=q_spec,
        compiler_params=_params(("parallel", "parallel")))(qh, kh, kh, vh, vh, sinks_b)


def attention_bwd(qh, kh, vh, sinks_b, doh, name):
    s = qh.shape[1]

    def body(q_ref, kc_ref, kp_ref, vc_ref, vp_ref, sk_ref, do_ref, dq_ref, dkc_ref, dkp_ref, dvc_ref, dvp_ref, ds_ref):
        n = pl.program_id(1)
        valid = _attn_valid(n)
        kcat = jnp.concatenate([kp_ref[0], kc_ref[0]], axis=0)
        vcat = jnp.concatenate([vp_ref[0], vc_ref[0]], axis=0)
        dk = jnp.zeros((2 * WINDOW, HEAD_DIM), F32)
        dv = jnp.zeros((2 * WINDOW, HEAD_DIM), F32)
        for g in range(_GROUP):
            q = q_ref[g]
            do = do_ref[g]
            probs, ps = _attn_probs(q, kcat, valid, sk_ref[g])
            dprobs = lax.dot_general(do, vcat, _DIMS["nt"], preferred_element_type=F32)
            dv = dv + lax.dot_general(probs.astype(BF16), do, _DIMS["tn"], preferred_element_type=F32)
            rs = jnp.sum(probs * dprobs, axis=-1, keepdims=True)
            dsb = (probs * (dprobs - rs) * (HEAD_DIM ** -0.5)).astype(BF16)
            dq_ref[g] = jnp.dot(dsb, kcat, preferred_element_type=F32)
            dk = dk + lax.dot_general(dsb, q, _DIMS["tn"], preferred_element_type=F32)
            dsink = jnp.broadcast_to(-jnp.sum(ps * rs, axis=0, keepdims=True), (1, LANES))

            @pl.when(n == 0)
            def _():
                ds_ref[g] = dsink

            @pl.when(n > 0)
            def _():
                ds_ref[g] += dsink

        dkp_ref[0] = dk[:WINDOW]
        dkc_ref[0] = dk[WINDOW:]
        dvp_ref[0] = dv[:WINDOW]
        dvc_ref[0] = dv[WINDOW:]

    q_spec, cur, prev, sink = _attn_specs()
    kv_shape = jax.ShapeDtypeStruct(kh.shape, F32)
    return pl.pallas_call(
        body, name=name,
        out_shape=(jax.ShapeDtypeStruct(qh.shape, F32), kv_shape, kv_shape, kv_shape, kv_shape,
                   jax.ShapeDtypeStruct(sinks_b.shape, F32)),
        grid=(N_KV_HEADS, s // WINDOW), in_specs=[q_spec, cur, prev, cur, prev, sink, q_spec],
        out_specs=(q_spec, cur, cur, cur, cur, sink),
        compiler_params=_params(("parallel", "arbitrary")))(qh, kh, kh, vh, vh, sinks_b, doh)


def _to_heads(t, heads):
    return t.reshape(t.shape[0], heads, HEAD_DIM).transpose(1, 0, 2)


def _from_heads(t):
    return t.transpose(1, 0, 2).reshape(t.shape[1], t.shape[0] * HEAD_DIM)


def _shift_window(t):
    return jnp.concatenate([t[:, WINDOW:], jnp.zeros_like(t[:, :WINDOW])], axis=1)


def merge_fwd(zg, branches, name):
    s = zg.shape[0]

    def body(zg_ref, b0, b1, b2, b3, o_ref):
        acc = jnp.zeros((TOKEN_TILE, D_MODEL), F32)
        for n, b_ref in enumerate((b0, b1, b2, b3)):
            gate = _sigmoid(zg_ref[:, n * D_MODEL:(n + 1) * D_MODEL].astype(F32))
            acc = acc + gate * b_ref[...].astype(F32)
        o_ref[...] = acc.astype(BF16)

    tile = pl.BlockSpec((TOKEN_TILE, D_MODEL), lambda i: (i, 0))
    wide = pl.BlockSpec((TOKEN_TILE, N_BRANCH * D_MODEL), lambda i: (i, 0))
    return pl.pallas_call(
        body, name=name, out_shape=jax.ShapeDtypeStruct((s, D_MODEL), BF16), grid=(s // TOKEN_TILE,),
        in_specs=[wide, tile, tile, tile, tile], out_specs=tile,
        compiler_params=_params(("parallel",)))(zg, *branches)


def merge_bwd(zg, branches, dm, name):
    s = zg.shape[0]

    def body(zg_ref, b0, b1, b2, b3, dm_ref, dzg_ref, d0, d1, d2, d3):
        dmv = dm_ref[...].astype(F32)
        for n, (b_ref, d_ref) in enumerate(((b0, d0), (b1, d1), (b2, d2), (b3, d3))):
            cols = slice(n * D_MODEL, (n + 1) * D_MODEL)
            gate = _sigmoid(zg_ref[:, cols].astype(F32))
            d_ref[...] = (gate * dmv).astype(BF16)
            dzg_ref[:, cols] = (dmv * b_ref[...].astype(F32) * gate * (1.0 - gate)).astype(BF16)

    tile = pl.BlockSpec((TOKEN_TILE, D_MODEL), lambda i: (i, 0))
    wide = pl.BlockSpec((TOKEN_TILE, N_BRANCH * D_MODEL), lambda i: (i, 0))
    act = jax.ShapeDtypeStruct((s, D_MODEL), BF16)
    return pl.pallas_call(
        body, name=name, out_shape=(jax.ShapeDtypeStruct((s, N_BRANCH * D_MODEL), BF16), act, act, act, act),
        grid=(s // TOKEN_TILE,), in_specs=[wide, tile, tile, tile, tile, tile],
        out_specs=(wide, tile, tile, tile, tile), compiler_params=_params(("parallel",)))(zg, *branches, dm)


def swiglu_fwd(gu, name):
    s = gu.shape[0]

    def body(g_ref, u_ref, o_ref):
        gate = g_ref[...].astype(F32)
        o_ref[...] = (gate * _sigmoid(gate) * u_ref[...].astype(F32)).astype(BF16)

    return pl.pallas_call(
        body, name=name, out_shape=jax.ShapeDtypeStruct((s, D_FF), BF16), grid=(s // TOKEN_TILE,),
        in_specs=[pl.BlockSpec((TOKEN_TILE, D_FF), lambda i: (i, 0)), pl.BlockSpec((TOKEN_TILE, D_FF), lambda i: (i, 1))],
        out_specs=pl.BlockSpec((TOKEN_TILE, D_FF), lambda i: (i, 0)), compiler_params=_params(("parallel",)))(gu, gu)


def swiglu_bwd(gu, dact, name):
    s = gu.shape[0]

    def body(g_ref, u_ref, da_ref, o_ref):
        gate = g_ref[...].astype(F32)
        sg = _sigmoid(gate)
        da = da_ref[...].astype(F32)
        o_ref[:, :D_FF] = (da * u_ref[...].astype(F32) * sg * (1.0 + gate * (1.0 - sg))).astype(BF16)
        o_ref[:, D_FF:] = (da * gate * sg).astype(BF16)

    half = pl.BlockSpec((TOKEN_TILE, D_FF), lambda i: (i, 0))
    return pl.pallas_call(
        body, name=name, out_shape=jax.ShapeDtypeStruct((s, 2 * D_FF), BF16), grid=(s // TOKEN_TILE,),
        in_specs=[half, pl.BlockSpec((TOKEN_TILE, D_FF), lambda i: (i, 1)), half],
        out_specs=pl.BlockSpec((TOKEN_TILE, 2 * D_FF), lambda i: (i, 0)),
        compiler_params=_params(("parallel",)))(gu, gu, dact)


ADAMW_BLOCK_BYTES = 1 << 20


def adamw(parts, w, m, v, name):
    n_parts, r, c = w.shape
    tr = _divisor_tile(r, max(SUBLANES, ADAMW_BLOCK_BYTES // (4 * c)), SUBLANES)
    tiles = r // tr

    def part_spec(j):
        return pl.BlockSpec((N_DEV, tr, c), lambda i: (0, jnp.clip(i - j * tiles, 0, tiles - 1), 0))

    def body(*refs):
        p_refs = refs[:n_parts]
        w_ref, m_ref, v_ref, g_ref, d_ref, nm_ref, nv_ref = refs[n_parts:]
        which = pl.program_id(0) // tiles
        g = None
        for j, p_ref in enumerate(p_refs):
            gj = p_ref[0].astype(F32)
            for i in range(1, N_DEV):
                gj = gj + p_ref[i].astype(F32)
            g = gj if g is None else jnp.where(which == j, gj, g)
        nm = ADAM_B1 * m_ref[0] + (1.0 - ADAM_B1) * g
        nv = ADAM_B2 * v_ref[0] + (1.0 - ADAM_B2) * (g * g)
        m_hat = nm / (1.0 - ADAM_B1 ** ADAM_STEP)
        v_hat = nv / (1.0 - ADAM_B2 ** ADAM_STEP)
        g_ref[0] = g
        d_ref[0] = -ADAM_LR * (m_hat / (jnp.sqrt(v_hat) + ADAM_EPS) + ADAM_WD * w_ref[0])
        nm_ref[0] = nm
        nv_ref[0] = nv

    tile = pl.BlockSpec((1, tr, c), lambda i: (i // tiles, i % tiles, 0))
    shape = jax.ShapeDtypeStruct(w.shape, F32)
    return pl.pallas_call(
        body, name=name, out_shape=(shape, shape, shape, shape), grid=(n_parts * tiles,),
        in_specs=[part_spec(j) for j in range(n_parts)] + [tile, tile, tile],
        out_specs=(tile, tile, tile, tile), compiler_params=_params(("parallel",)))(*parts, w, m, v)


_RELATIONS = [(a, b, e) for a in (0, 1) for b in (0, 1) for e in (0, 1)][1:]


_HBM_SPEC = pl.BlockSpec(memory_space=pltpu.HBM)
_SEM_SPEC = pl.BlockSpec(memory_space=pltpu.SEMAPHORE)
_ANY_SPEC = pl.BlockSpec(memory_space=pl.ANY)
_DATAFLOW = pltpu.SideEffectType.DATAFLOW_SIDE_EFFECTING


_OTHER_CHIPS = [(1, 0), (0, 1), (1, 1)]
_FIRST_LEVEL = [(0, 0, 1)] + [(a, b, 0) for a, b in _OTHER_CHIPS]


def _remote_copies(ins, lands, send_sems, recv_sems, scatter, relations):
    x, y, c = lax.axis_index("x"), lax.axis_index("y"), lax.axis_index("c")
    me = 4 * x + 2 * y + c
    copies = []
    for t in range(len(ins)):
        for k, (a, b, e) in enumerate(relations):
            px, py, pc = (x + a) % 2, (y + b) % 2, (c + e) % 2
            src = ins[t].at[4 * px + 2 * py + pc] if scatter[t] else ins[t]
            copies.append(pltpu.make_async_remote_copy(
                src_ref=src, dst_ref=lands[t].at[me], send_sem=send_sems.at[t * len(relations) + k],
                recv_sem=recv_sems.at[t * len(relations) + k],
                device_id=(px, py, pc), device_id_type=pl.DeviceIdType.MESH))
    return copies


def _forward_copies(lands, send_sems, recv_sems):
    x, y, c = lax.axis_index("x"), lax.axis_index("y"), lax.axis_index("c")
    copies = []
    for t in range(len(lands)):
        for k, (a, b) in enumerate(_OTHER_CHIPS):
            slot = lands[t].at[4 * ((x + a) % 2) + 2 * ((y + b) % 2) + c]
            copies.append(pltpu.make_async_remote_copy(
                src_ref=slot, dst_ref=slot, send_sem=send_sems.at[t * len(_OTHER_CHIPS) + k],
                recv_sem=recv_sems.at[t * len(_OTHER_CHIPS) + k],
                device_id=(x, y, 1 - c), device_id_type=pl.DeviceIdType.MESH))
    return copies


def exchange_start(arrays, scatter, name, after=None, relations=_RELATIONS):
    n = len(arrays)
    n_rel = len(relations)
    land_shapes = [a.shape if scatter[t] else (N_DEV,) + a.shape for t, a in enumerate(arrays)]

    def body(*refs):
        ins, lands = refs[:n], refs[n:2 * n]
        send_sems, recv_sems = refs[-2 * n - 3], refs[-2 * n - 2]
        token = refs[-1]
        for cp in _remote_copies(ins, lands, send_sems, recv_sems, scatter, relations):
            cp.start()
        token[...] = jnp.zeros_like(token)

    sems = pltpu.SemaphoreType.DMA((n * n_rel,))
    out_shape = ((sems, sems) + tuple(pltpu.HBM(a.shape, a.dtype) for a in arrays)
                 + tuple(pltpu.HBM(s, a.dtype) for s, a in zip(land_shapes, arrays))
                 + (jax.ShapeDtypeStruct((SUBLANES, LANES), F32),))
    operands = [pltpu.with_memory_space_constraint(a, pltpu.HBM) for a in arrays]
    operands += [pltpu.with_memory_space_constraint(lax.empty(s, a.dtype), pltpu.HBM) for s, a in zip(land_shapes, arrays)]
    in_specs = [_HBM_SPEC] * (2 * n)
    if after is not None:
        operands.append(after)
        in_specs.append(_ANY_SPEC)
    res = pl.pallas_call(
        body, name=name, out_shape=out_shape, in_specs=in_specs,
        out_specs=(_SEM_SPEC, _SEM_SPEC) + (_HBM_SPEC,) * (2 * n) + (pl.BlockSpec(memory_space=pltpu.VMEM),),
        input_output_aliases={i: 2 + i for i in range(2 * n)},
        compiler_params=pltpu.CompilerParams(has_side_effects=_DATAFLOW))(*operands)
    handle = (res[0], res[1], res[2:2 + n], res[2 + n:2 + 2 * n], tuple(scatter), relations)
    return handle, res[-1]


def exchange_wait(handle, after, name):
    send_sems, recv_sems, sources, lands, scatter, relations = handle
    n = len(sources)

    def body(*refs):
        ins, lzs = refs[:n], refs[n:2 * n]
        send_ref, recv_ref = refs[2 * n], refs[2 * n + 1]
        for cp in _remote_copies(ins, lzs, send_ref, recv_ref, scatter, relations):
            cp.wait_send()
            cp.wait_recv()

    out_shape = (tuple(pltpu.HBM(a.shape, a.dtype) for a in sources) + tuple(pltpu.HBM(a.shape, a.dtype) for a in lands))
    res = pl.pallas_call(
        body, name=name, out_shape=out_shape, in_specs=[_HBM_SPEC] * (2 * n) + [_SEM_SPEC, _SEM_SPEC, _ANY_SPEC],
        out_specs=(_HBM_SPEC,) * (2 * n), input_output_aliases={i: i for i in range(2 * n)},
        compiler_params=pltpu.CompilerParams(has_side_effects=_DATAFLOW))(*sources, *lands, send_sems, recv_sems, after)
    return res[n:]


def forward_start(lands, name):
    n = len(lands)

    def body(*refs):
        send_sems, recv_sems, token = refs[n], refs[n + 1], refs[-1]
        for cp in _forward_copies(refs[:n], send_sems, recv_sems):
            cp.start()
        token[...] = jnp.zeros_like(token)

    sems = pltpu.SemaphoreType.DMA((n * len(_OTHER_CHIPS),))
    res = pl.pallas_call(
        body, name=name,
        out_shape=(sems, sems) + tuple(pltpu.HBM(a.shape, a.dtype) for a in lands)
        + (jax.ShapeDtypeStruct((SUBLANES, LANES), F32),),
        in_specs=[_HBM_SPEC] * n,
        out_specs=(_SEM_SPEC, _SEM_SPEC) + (_HBM_SPEC,) * n + (pl.BlockSpec(memory_space=pltpu.VMEM),),
        input_output_aliases={i: 2 + i for i in range(n)},
        compiler_params=pltpu.CompilerParams(has_side_effects=_DATAFLOW))(*lands)
    return (res[0], res[1], res[2:2 + n]), res[-1]


def forward_wait(handle, after, name):
    send_sems, recv_sems, lands = handle
    n = len(lands)

    def body(*refs):
        for cp in _forward_copies(refs[:n], refs[n], refs[n + 1]):
            cp.wait_send()
            cp.wait_recv()

    return pl.pallas_call(
        body, name=name, out_shape=tuple(pltpu.HBM(a.shape, a.dtype) for a in lands),
        in_specs=[_HBM_SPEC] * n + [_SEM_SPEC, _SEM_SPEC, _ANY_SPEC], out_specs=(_HBM_SPEC,) * n,
        input_output_aliases={i: i for i in range(n)},
        compiler_params=pltpu.CompilerParams(has_side_effects=_DATAFLOW))(*lands, send_sems, recv_sems, after)


def _place_own(landed, own, me):
    return lax.dynamic_update_index_in_dim(landed, own, me, 0)


_SMALL = ("norm_mix", "sg_ln_g", "sg_ln_b", "sg_b", "cv_b", "cv_ln_g", "cv_ln_b", "attn_sinks", "norm_ffn",
          "norm_final")
_PACK_UNIT = SUBLANES * LANES


def _pack(tensors):
    rows = []
    for t in tensors:
        flat = t.reshape(-1)
        pad = (-flat.shape[0]) % _PACK_UNIT
        rows.append(jnp.pad(flat, (0, pad)).reshape(-1, LANES))
    return jnp.concatenate(rows, axis=0)


def _unpack(packed, like):
    out, row = [], 0
    for t in like:
        size = 1
        for d in t.shape:
            size *= d
        rows = -(-size // _PACK_UNIT) * SUBLANES
        out.append(packed[row:row + rows].reshape(-1)[:size].reshape(t.shape))
        row += rows
    return out


def _layer_fwd(l, x, p, late_params, mid_hook=None, ffn_hook=None):
    tag = f"l{l}_"
    xn = rmsnorm_fwd(x, p["norm_mix"], tag + "norm_mix")
    proj = matmul(xn, p["w_in_t"], "nt", BF16, tag + "proj_a", tm_cap=1024, tn_cap=2176, b_rows=PROJ_A)
    zg = matmul(xn, p["w_in_t_g"], "nt", BF16, tag + "proj_g", tm_cap=1024, tn_cap=2048)
    y_a = mixer_a_fwd(proj, p["sg_ln_g"], p["sg_ln_b"], p["sg_w"], p["sg_b"], tag + "mix_a")
    conv = conv_b_fwd(proj, p["cv_w"], p["cv_b"], tag + "conv_b")
    y_b = ln_silu_fwd(conv, p["cv_ln_g"], p["cv_ln_b"], tag + "ln_silu")
    token = mid_hook(y_b) if mid_hook is not None else None
    qk = rope_fwd(proj, p["cos"], p["sin"], tag + "rope", after=token)
    qh = _to_heads(qk[:, :Q_WIDTH], N_Q_HEADS)
    kh = _to_heads(qk[:, Q_WIDTH:], N_KV_HEADS)
    vh = _to_heads(proj[:, 2688:2816], N_KV_HEADS)
    oh = attention_fwd(qh, kh, vh, p["sinks"], tag + "attn")
    y_c = _from_heads(oh)
    y_d = conv_d_fwd(proj, p["sc_w"], tag + "conv_d")
    ys = (y_a, y_b, y_c, y_d)
    p = {**p, **late_params(y_d)}
    branches = tuple(matmul(ys[n], p["w_branch"][n], "nn", BF16, tag + f"branch{n}", tm_cap=1024, tn_cap=1024)
                     for n in range(N_BRANCH))
    merged = merge_fwd(zg, branches, tag + "merge")
    x_mid = matmul(merged, p["w_out"], "nn", F32, tag + "out", add=x, tm_cap=1024, tn_cap=1024)
    token = ffn_hook(x_mid) if ffn_hook is not None else None
    hn = rmsnorm_fwd(x_mid, p["norm_ffn"], tag + "norm_ffn")
    gu = matmul(hn, p["w_gate_up_t"], "nt", BF16, tag + "gate_up", tm_cap=512, tn_cap=2816, after=token)
    act = swiglu_fwd(gu, tag + "swiglu")
    x_out = matmul(act, p["w_down"], "nn", F32, tag + "down", add=x_mid, tm_cap=512, tn_cap=1024)
    saved = dict(x=x, xn=xn, proj=proj, zg=zg, conv=conv, qh=qh, kh=kh, vh=vh, ys=ys, branches=branches,
                 merged=merged, x_mid=x_mid, hn=hn, gu=gu, act=act)
    return x_out, saved, p


def _layer_bwd(l, dx_out, p, sv, emit, after=None):
    tag = f"l{l}_b_"
    g = {}
    dact = matmul(dx_out, p["w_down"], "nt", BF16, tag + "dact", after=after, tm_cap=512, tn_cap=2816)
    dw_down = matmul(sv["act"], dx_out, "tn", BF16, tag + "dw_down", tm_cap=1408, tn_cap=512)
    dgu = swiglu_bwd(sv["gu"], dact, tag + "swiglu")
    dhn = matmul(dgu, p["w_gate_up_t"], "nn", BF16, tag + "dhn", tm_cap=512, tn_cap=512)
    dw_gate_up = matmul(dgu, sv["hn"], "tn", BF16, tag + "dw_gate_up", tm_cap=1408, tn_cap=1024)
    token = emit("a", {"w_gate_up": dw_gate_up, "w_down": dw_down})
    dx_mid, g["norm_ffn"] = rmsnorm_bwd(sv["x_mid"], p["norm_ffn"], dhn, dx_out, tag + "norm_ffn")
    dmerged = matmul(dx_mid, p["w_out"], "nt", BF16, tag + "dmerged", after=token, tm_cap=1024, tn_cap=1024)
    dw_out = matmul(sv["merged"], dx_mid, "tn", BF16, tag + "dw_out", tm_cap=1024, tn_cap=512)
    dzg, *dbranches = merge_bwd(sv["zg"], sv["branches"], dmerged, tag + "merge")
    dys = [matmul(dbranches[n], p["w_branch"][n], "nt", BF16, tag + f"dy{n}", tm_cap=1024, tn_cap=512)
           for n in range(N_BRANCH)]
    dw_branch = jnp.stack(
        [matmul(sv["ys"][n], dbranches[n], "tn", BF16, tag + f"dw_branch{n}", tm_cap=512, tn_cap=1024)
         for n in range(N_BRANCH)])
    token = emit("b", {"w_branch": dw_branch, "w_out": dw_out})
    proj = sv["proj"]
    dz_a, g["sg_ln_g"], g["sg_ln_b"], g["sg_w"], dsb = mixer_a_bwd(
        proj, dys[0], p["sg_ln_g"], p["sg_ln_b"], p["sg_w"], p["sg_b"], tag + "mix_a")
    g["sg_b"] = dsb[:, :, 0]
    dconv, g["cv_ln_g"], g["cv_ln_b"] = ln_silu_bwd(sv["conv"], dys[1], p["cv_ln_g"], p["cv_ln_b"], tag + "ln_silu")
    da, dgate, dcw, g["cv_b"] = conv_b_bwd(proj, p["cv_w"], dconv, tag + "conv_b")
    g["cv_w"] = dcw[:CV_KERNEL]
    doh = _to_heads(dys[2], N_Q_HEADS)
    dqh, dkc, dkp, dvc, dvp, dsk = attention_bwd(sv["qh"], sv["kh"], sv["vh"], p["sinks"], doh, tag + "attn")
    g["attn_sinks"] = dsk[:, 0, 0]
    dqk_cur = jnp.concatenate([_from_heads(dqh), _from_heads(dkc)], axis=1)
    dqk_prev = jnp.concatenate([jnp.zeros((SEQ, Q_WIDTH), F32), _from_heads(_shift_window(dkp))], axis=1)
    dqk = rope_bwd(dqk_cur, dqk_prev, p["cos"], -p["sin"], tag + "rope")
    dv = (_from_heads(dvc) + _from_heads(_shift_window(dvp))).astype(BF16)
    dbg, dcg, dh, dsw = conv_d_bwd(proj, p["sc_w"], dys[3], tag + "conv_d")
    g["sc_w"] = dsw[:SC_KERNEL]
    dproj = jnp.concatenate([dz_a, da, dgate, dqk, dv, dbg, dcg, dh], axis=1)
    dw_in = matmul(dproj, sv["xn"], "tn", BF16, tag + "dw_in_a", after=token, tm_cap=2176, tn_cap=512,
                   out_rows=PROJ_WIDTH)
    g["w_in"] = matmul(dzg, sv["xn"], "tn", BF16, tag + "dw_in_g", tm_cap=256, tn_cap=1024, into=dw_in,
                       into_row=PROJ_A)
    token = emit("c", {n: g.pop(n) for n in _EARLY}, {"sg_w": g.pop("sg_w")})
    dxn = matmul(dproj, p["w_in_t"], "nn", F32, tag + "dxn_a", after=token, tm_cap=512, tn_cap=512, b_rows=PROJ_A)
    dxn = matmul(dzg, p["w_in_t_g"], "nn", F32, tag + "dxn_g", add=dxn, tm_cap=512, tn_cap=512)
    dx_in, g["norm_mix"] = rmsnorm_bwd(sv["x"], p["norm_mix"], dxn, dx_mid, tag + "norm_mix")
    return dx_in, g, token


_EARLY = ("w_in", "cv_w", "sc_w")
_LATE = ("w_branch", "w_out", "w_gate_up", "w_down")


_TRANSPOSED = ("w_in", "w_gate_up")


def _shard_view(name, t):
    return jnp.swapaxes(t, 1, 2) if name in _TRANSPOSED else t


def _full_weight(name, t):
    if name in ("w_out", "w_down") + _TRANSPOSED:
        return t.reshape(-1, t.shape[-1])
    if name == "w_branch":
        return t.transpose(1, 2, 0, 3).reshape(N_BRANCH, SG_WIDTH, D_MODEL)
    return t.transpose(1, 0, 2).reshape(t.shape[1], -1)


def _to_blocks(name, full):
    if name in ("w_out", "w_down") + _TRANSPOSED:
        return full.reshape(N_DEV, -1, full.shape[-1])
    if name == "w_branch":
        return full.reshape(N_BRANCH, SG_WIDTH, N_DEV, -1).transpose(2, 0, 1, 3)
    return full.reshape(full.shape[0], N_DEV, -1).transpose(1, 0, 2)


def _rope_tables():
    pos = jnp.arange(SEQ, dtype=F32)
    inv_freq = 1.0 / (ROPE_THETA ** (jnp.arange(0, HEAD_DIM, 2, dtype=F32) / HEAD_DIM))
    ang = pos[:, None] * inv_freq[None, :]
    cos, sin = jnp.cos(ang), jnp.sin(ang)
    reps = LANES // HEAD_DIM
    return jnp.tile(jnp.concatenate([cos, cos], axis=1), (1, reps)), jnp.tile(jnp.concatenate([-sin, sin], axis=1), (1, reps))


def kernel(x, norm_mix, w_in, sg_ln_g, sg_ln_b, sg_w, sg_b, cv_w, cv_b, cv_ln_g, cv_ln_b, attn_sinks, sc_w, w_branch, w_out, norm_ffn, w_gate_up, w_down, norm_final, loss_target, m_norm_mix, m_w_in, m_sg_ln_g, m_sg_ln_b, m_sg_w, m_sg_b, m_cv_w, m_cv_b, m_cv_ln_g, m_cv_ln_b, m_attn_sinks, m_sc_w, m_w_branch, m_w_out, m_norm_ffn, m_w_gate_up, m_w_down, m_norm_final, v_norm_mix, v_w_in, v_sg_ln_g, v_sg_ln_b, v_sg_w, v_sg_b, v_cv_w, v_cv_b, v_cv_ln_g, v_cv_ln_b, v_attn_sinks, v_sc_w, v_w_branch, v_w_out, v_norm_ffn, v_w_gate_up, v_w_down, v_norm_final):
    names = ("norm_mix", "w_in", "sg_ln_g", "sg_ln_b", "sg_w", "sg_b", "cv_w", "cv_b", "cv_ln_g", "cv_ln_b",
             "attn_sinks", "sc_w", "w_branch", "w_out", "norm_ffn", "w_gate_up", "w_down", "norm_final")
    w = dict(zip(names, (norm_mix, w_in, sg_ln_g, sg_ln_b, sg_w, sg_b, cv_w, cv_b, cv_ln_g, cv_ln_b, attn_sinks,
                         sc_w, w_branch, w_out, norm_ffn, w_gate_up, w_down, norm_final)))
    m = dict(zip(names, (m_norm_mix, m_w_in, m_sg_ln_g, m_sg_ln_b, m_sg_w, m_sg_b, m_cv_w, m_cv_b, m_cv_ln_g,
                         m_cv_ln_b, m_attn_sinks, m_sc_w, m_w_branch, m_w_out, m_norm_ffn, m_w_gate_up, m_w_down,
                         m_norm_final)))
    v = dict(zip(names, (v_norm_mix, v_w_in, v_sg_ln_g, v_sg_ln_b, v_sg_w, v_sg_b, v_cv_w, v_cv_b, v_cv_ln_g,
                         v_cv_ln_b, v_attn_sinks, v_sc_w, v_w_branch, v_w_out, v_norm_ffn, v_w_gate_up, v_w_down,
                         v_norm_final)))

    me = 4 * lax.axis_index("x") + 2 * lax.axis_index("y") + lax.axis_index("c")

    gathers, forwards, token = {}, {}, None
    for l in range(DEPTH):
        for group in (_EARLY, _LATE):
            shards = [_shard_view(n, w[n])[l].astype(BF16) for n in group]
            handle, token = exchange_start(shards, [False] * len(group), f"gather_start{l}_{group[0]}", after=token,
                                           relations=_FIRST_LEVEL)
            gathers[(l, group)] = (handle, shards)

    def begin_forward(l, group, after):
        landed = exchange_wait(gathers[(l, group)][0], after, f"gather_wait{l}_{group[0]}")
        forwards[(l, group)], tok = forward_start(landed, f"forward_start{l}_{group[0]}")
        return tok

    def landed_weights(l, group, after):
        landed = forward_wait(forwards[(l, group)], after, f"forward_wait{l}_{group[0]}")
        return {n + "_t" if n in _TRANSPOSED else n: _full_weight(n, _place_own(t, own, me))
                for n, t, own in zip(group, landed, gathers[(l, group)][1])}

    cos_t, sin_t = _rope_tables()

    def early_params(l, after):
        full = landed_weights(l, _EARLY, after)
        return dict(
            norm_mix=w["norm_mix"][l][None], norm_ffn=w["norm_ffn"][l][None],
            w_in_t=full["w_in_t"], w_in_t_g=full["w_in_t"][PROJ_A:],
            sg_ln_g=w["sg_ln_g"][l][None], sg_ln_b=w["sg_ln_b"][l][None], sg_w=w["sg_w"][l],
            sg_b=jnp.broadcast_to(w["sg_b"][l][:, :, None], (SG_GROUPS, SG_CHUNK, LANES)),
            cv_w=jnp.pad(full["cv_w"].astype(F32), ((0, HALO - CV_KERNEL), (0, 0))),
            cv_b=w["cv_b"][l][None], cv_ln_g=w["cv_ln_g"][l][None], cv_ln_b=w["cv_ln_b"][l][None],
            sinks=jnp.broadcast_to(w["attn_sinks"][l][:, None, None], (N_Q_HEADS, 1, LANES)),
            sc_w=jnp.pad(full["sc_w"].astype(F32), ((0, SUBLANES - SC_KERNEL), (0, 0))),
            cos=cos_t, sin=sin_t)

    params, saved = [None] * DEPTH, [None] * DEPTH
    h = x[0]
    after = begin_forward(0, _EARLY, token)
    for l in range(DEPTH):
        h, saved[l], params[l] = _layer_fwd(
            l, h, early_params(l, after), lambda behind, l=l: landed_weights(l, _LATE, behind),
            mid_hook=lambda behind, l=l: begin_forward(l, _LATE, behind),
            ffn_hook=(lambda behind, l=l: begin_forward(l + 1, _EARLY, behind)) if l + 1 < DEPTH else None)
        after = h
    loss_row, dh, d_norm_final = loss_head(h, w["norm_final"][None], loss_target[0], "loss_head")

    sent = {}

    def emitter(l):
        def emit(group, grads_of, replicated=None):
            replicated = replicated or {}
            send = [_to_blocks(n, grads_of[n].astype(BF16)) for n in grads_of] + list(replicated.values())
            flags = [True] * len(grads_of) + [False] * len(replicated)
            handle, tok = exchange_start(send, flags, f"grads_start{l}{group}")
            sent[(l, group)] = (handle, send, tuple(grads_of) + tuple(replicated), flags)
            return tok
        return emit

    grads = [None] * DEPTH
    token = None
    for l in reversed(range(DEPTH)):
        dh, grads[l], token = _layer_bwd(l, dh, params[l], saved[l], emitter(l), after=token)
    grad_x = dh[None]

    stacked = {n: jnp.stack([grads[l][n] for l in range(DEPTH)]) for n in _SMALL if n != "norm_final"}
    for n in ("norm_mix", "norm_ffn", "sg_ln_g", "sg_ln_b", "cv_b", "cv_ln_g", "cv_ln_b"):
        stacked[n] = stacked[n][:, 0]
    stacked["norm_final"] = d_norm_final[0]
    no_state = jnp.zeros((1,), F32)
    small_like = [w[n] for n in _SMALL] + [no_state]
    small_part = _pack([stacked[n] for n in _SMALL] + [loss_row[0, :1]])
    handle_small, token = exchange_start([small_part], [False], "grads_start_small", after=token)

    def received(l, group, after):
        handle, send, group_names, flags = sent[(l, group)]
        landed = exchange_wait(handle, after, f"grads_wait{l}{group}")
        return {n: _place_own(t, lax.dynamic_index_in_dim(s, me, 0, keepdims=False) if scattered else s, me)
                for n, t, s, scattered in zip(group_names, landed, send, flags)}

    out_g, out_d, out_m, out_v = {}, {}, {}, {}

    def update(n, by_layer):
        shape = _shard_view(n, w[n]).shape
        view = (DEPTH, w[n].size // (DEPTH * shape[-1]), shape[-1])
        parts = [t.reshape((N_DEV,) + view[1:]) for t in by_layer]
        res = adamw(parts, *[_shard_view(n, t).reshape(view) for t in (w[n], m[n], v[n])], "adamw_" + n)
        out_g[n], out_d[n], out_m[n], out_v[n] = (_shard_view(n, t.reshape(shape)) for t in res)
        return res[0]

    behind = token
    for group in ("a", "b", "c"):
        r1 = received(1, group, behind)
        r0 = received(0, group, next(iter(r1.values())))
        for n in r0:
            behind = update(n, [r0[n], r1[n]])
    landed = exchange_wait(handle_small, behind, "grads_wait_small")
    res = adamw([_place_own(landed[0], small_part, me)], _pack(small_like)[None],
                _pack([m[n] for n in _SMALL] + [no_state])[None], _pack([v[n] for n in _SMALL] + [no_state])[None],
                "adamw_small")
    for store, packed in zip((out_g, out_d, out_m, out_v), res):
        for n, t in zip(_SMALL + ("loss",), _unpack(packed[0], small_like)):
            store[n] = t

    loss = out_g["loss"][0]
    return (loss, grad_x, *[out_g[n] for n in names], *[out_d[n] for n in names], *[out_m[n] for n in names],
            *[out_v[n] for n in names])
```

```python
import jax
import jax.numpy as jnp
from jax import lax
from jax.experimental import pallas as pl
from jax.experimental.pallas import tpu as pltpu

F32 = jnp.float32
BF16 = jnp.bfloat16

SEQ = 2048
D_MODEL = 1024
DEPTH = 2
SG_WIDTH = 512
SG_CHUNK = 128
SG_GROUPS = 4
CV_WIDTH = 512
CV_KERNEL = 31
HEAD_DIM = 64
N_Q_HEADS = 8
N_KV_HEADS = 2
Q_WIDTH = 512
KV_WIDTH = 128
WINDOW = 128
SC_WIDTH = 512
SC_KERNEL = 3
N_BRANCH = 4
D_FF = 2816
EPS = 1e-6
ROPE_THETA = 10000.0
PROJ_A = 4352
PROJ_WIDTH = 8448
N_DEV = 8

ADAM_LR = 0.001
ADAM_B1 = 0.9
ADAM_B2 = 0.999
ADAM_EPS = 1e-08
ADAM_WD = 0.01
ADAM_STEP = 10

LANES = 128
SUBLANES = 8
VMEM_LIMIT_BYTES = 48 * 1024 * 1024
HALO = 32
CONV_ROWS = 256
TOKEN_TILE = 256
NORM_TILE = 512
ROPE_TILE = 1024

_SQRT_HALF = 0.7071067811865476
_INV_SQRT_2PI = 0.3989422804014327


def _params(semantics=None):
    return pltpu.CompilerParams(dimension_semantics=semantics, vmem_limit_bytes=VMEM_LIMIT_BYTES)


def _divisor_tile(n, cap, unit):
    best = None
    for t in range(unit, min(n, cap) + 1, unit):
        if n % t == 0:
            best = t
    return best if best is not None else n


_DIMS = {"nn": (((1,), (0,)), ((), ())), "nt": (((1,), (1,)), ((), ())), "tn": (((0,), (0,)), ((), ()))}


def matmul(a, b, mode, out_dtype, name, add=None, tm_cap=512, tn_cap=512, after=None, b_rows=None, out_rows=None,
           into=None, into_row=0):
    if mode == "nn":
        (m, k), n = a.shape, b.shape[1]
        k = b_rows if b_rows is not None else k
    elif mode == "nt":
        (m, k), n = a.shape, (b_rows if b_rows is not None else b.shape[0])
    else:
        (k, m), n = a.shape, b.shape[1]
    tm = _divisor_tile(m, tm_cap, LANES)
    tn = _divisor_tile(n, tn_cap, LANES)
    row0 = into_row // tm
    assert row0 * tm == into_row
    a_spec = pl.BlockSpec((k, tm), lambda i, j: (0, i)) if mode == "tn" else pl.BlockSpec((tm, k), lambda i, j: (i, 0))
    b_spec = pl.BlockSpec((tn, k), lambda i, j: (j, 0)) if mode == "nt" else pl.BlockSpec((k, tn), lambda i, j: (0, j))
    o_spec = pl.BlockSpec((tm, tn), lambda i, j: (i + row0, j))
    dims = _DIMS[mode]

    def body(*refs):
        a_ref, b_ref = refs[0], refs[1]
        o_ref = refs[-1]
        acc = lax.dot_general(a_ref[...].astype(BF16), b_ref[...].astype(BF16), dims, preferred_element_type=F32)
        if add is not None:
            acc = acc + refs[2][...].astype(F32)
        o_ref[...] = acc.astype(out_dtype)

    unread = tuple(t for t in (after, into) if t is not None)
    operands = (a, b) + (() if add is None else (add,)) + unread
    in_specs = [a_spec, b_spec] + ([o_spec] if add is not None else [])
    in_specs += [pl.BlockSpec(memory_space=pl.ANY)] * len(unread)
    aliases = {len(operands) - 1: 0} if into is not None else {}
    return pl.pallas_call(
        body, name=name,
        out_shape=jax.ShapeDtypeStruct((into.shape[0] if into is not None else out_rows or m, n), out_dtype),
        grid=(m // tm, n // tn),
        in_specs=in_specs, out_specs=o_spec, input_output_aliases=aliases,
        compiler_params=_params(("parallel", "parallel")))(*operands)


def _sigmoid(x):
    return 1.0 / (1.0 + jnp.exp(-x))


def _gelu(x):
    return 0.5 * x * (1.0 + lax.erf(x * _SQRT_HALF))


def _gelu_grad(x):
    return 0.5 * (1.0 + lax.erf(x * _SQRT_HALF)) + x * _INV_SQRT_2PI * jnp.exp(-0.5 * x * x)


def _rms_stats(x):
    r = lax.rsqrt(jnp.mean(x * x, axis=-1, keepdims=True) + EPS)
    return x * r, r


def _rms_bwd(dxn, xhat, r, g):
    h = dxn * g
    return r * (h - xhat * jnp.mean(h * xhat, axis=-1, keepdims=True))


def _ln_stats(x):
    mu = jnp.mean(x, axis=-1, keepdims=True)
    xc = x - mu
    rstd = lax.rsqrt(jnp.mean(xc * xc, axis=-1, keepdims=True) + EPS)
    return xc * rstd, rstd


def _ln_bwd(dy, xhat, rstd, g):
    dxhat = dy * g
    return rstd * (dxhat - jnp.mean(dxhat, axis=-1, keepdims=True)
                   - xhat * jnp.mean(dxhat * xhat, axis=-1, keepdims=True))


def _accumulate(ref, value, first):
    @pl.when(first)
    def _():
        ref[...] = value

    @pl.when(jnp.logical_not(first))
    def _():
        ref[...] += value


def _shift_rows(win, shift, n_out):
    n = win.shape[0]
    if shift % n == 0:
        return win[:n_out]
    return pltpu.roll(win, n - shift, axis=0)[:n_out]


def _row_spec(width):
    return pl.BlockSpec((1, width), lambda i: (0, 0))


def rmsnorm_fwd(x, g, name):
    s, d = x.shape

    def body(x_ref, g_ref, o_ref):
        xhat, _ = _rms_stats(x_ref[...])
        o_ref[...] = (xhat * g_ref[...]).astype(BF16)

    tile = pl.BlockSpec((NORM_TILE, d), lambda i: (i, 0))
    return pl.pallas_call(
        body, name=name, out_shape=jax.ShapeDtypeStruct((s, d), BF16), grid=(s // NORM_TILE,),
        in_specs=[tile, _row_spec(d)], out_specs=tile, compiler_params=_params(("parallel",)))(x, g)


def rmsnorm_bwd(x, g, dxn, dres, name):
    s, d = x.shape

    def body(x_ref, g_ref, dxn_ref, dres_ref, dx_ref, dg_ref):
        xhat, r = _rms_stats(x_ref[...])
        dxn_v = dxn_ref[...].astype(F32)
        dx_ref[...] = dres_ref[...] + _rms_bwd(dxn_v, xhat, r, g_ref[...])
        _accumulate(dg_ref, jnp.sum(dxn_v * xhat, axis=0, keepdims=True), pl.program_id(0) == 0)

    tile = pl.BlockSpec((NORM_TILE, d), lambda i: (i, 0))
    return pl.pallas_call(
        body, name=name, out_shape=(jax.ShapeDtypeStruct((s, d), F32), jax.ShapeDtypeStruct((1, d), F32)),
        grid=(s // NORM_TILE,), in_specs=[tile, _row_spec(d), tile, tile], out_specs=(tile, _row_spec(d)),
        compiler_params=_params(("arbitrary",)))(x, g, dxn, dres)


def loss_head(x, g, target, name):
    s, d = x.shape

    def body(x_ref, g_ref, t_ref, loss_ref, dx_ref, dg_ref):
        first = pl.program_id(0) == 0
        xhat, r = _rms_stats(x_ref[...])
        gv = g_ref[...]
        err = xhat * gv - t_ref[...]
        part = 0.5 * jnp.sum(jnp.sum(err * err, axis=-1, keepdims=True), axis=0, keepdims=True) / d
        _accumulate(loss_ref, jnp.broadcast_to(part, (1, LANES)), first)
        dy = err / d
        dx_ref[...] = _rms_bwd(dy, xhat, r, gv)
        _accumulate(dg_ref, jnp.sum(dy * xhat, axis=0, keepdims=True), first)

    tile = pl.BlockSpec((NORM_TILE, d), lambda i: (i, 0))
    return pl.pallas_call(
        body, name=name,
        out_shape=(jax.ShapeDtypeStruct((1, LANES), F32), jax.ShapeDtypeStruct((s, d), F32),
                   jax.ShapeDtypeStruct((1, d), F32)),
        grid=(s // NORM_TILE,), in_specs=[tile, _row_spec(d), tile],
        out_specs=(_row_spec(LANES), tile, _row_spec(d)), compiler_params=_params(("arbitrary",)))(x, g, target)


def _tril_mask():
    row = lax.broadcasted_iota(jnp.int32, (SG_CHUNK, SG_CHUNK), 0)
    col = lax.broadcasted_iota(jnp.int32, (SG_CHUNK, SG_CHUNK), 1)
    return row >= col


def _sg_specs():
    vec = _row_spec(SG_WIDTH)
    mat = pl.BlockSpec((SG_GROUPS, SG_CHUNK, SG_CHUNK), lambda i: (0, 0, 0))
    return vec, mat


def mixer_a_fwd(proj, ln_g, ln_b, w_s, b_s, name):
    s = proj.shape[0]
    chunks = TOKEN_TILE // SG_CHUNK

    def body(z_ref, lg_ref, lb_ref, w_ref, b_ref, o_ref):
        ge = _gelu(z_ref[...].astype(F32))
        u = ge[:, :SG_WIDTH]
        xhat, _ = _ln_stats(ge[:, SG_WIDTH:])
        vn = xhat * lg_ref[...] + lb_ref[...]
        tril = _tril_mask()
        for ci in range(chunks):
            rows = slice(ci * SG_CHUNK, (ci + 1) * SG_CHUNK)
            for g in range(SG_GROUPS):
                cols = slice(g * LANES, (g + 1) * LANES)
                wm = jnp.where(tril, w_ref[g], 0.0).astype(BF16)
                mixed = jnp.dot(wm, vn[rows, cols].astype(BF16), preferred_element_type=F32) + b_ref[g]
                o_ref[rows, cols] = (u[rows, cols] * mixed).astype(BF16)

    vec, mat = _sg_specs()
    return pl.pallas_call(
        body, name=name, out_shape=jax.ShapeDtypeStruct((s, SG_WIDTH), BF16), grid=(s // TOKEN_TILE,),
        in_specs=[pl.BlockSpec((TOKEN_TILE, 2 * SG_WIDTH), lambda i: (i, 0)), vec, vec, mat, mat],
        out_specs=pl.BlockSpec((TOKEN_TILE, SG_WIDTH), lambda i: (i, 0)),
        compiler_params=_params(("parallel",)))(proj, ln_g, ln_b, w_s, b_s)


def mixer_a_bwd(proj, dy, ln_g, ln_b, w_s, b_s, name):
    s = proj.shape[0]
    chunks = TOKEN_TILE // SG_CHUNK

    def body(z_ref, dy_ref, lg_ref, lb_ref, w_ref, b_ref, dz_ref, dlg_ref, dlb_ref, dw_ref, db_ref, du_scr, dvn_scr):
        first = pl.program_id(0) == 0

        @pl.when(first)
        def _():
            dw_ref[...] = jnp.zeros_like(dw_ref)
            db_ref[...] = jnp.zeros_like(db_ref)

        z = z_ref[...].astype(F32)
        ge = _gelu(z)
        u = ge[:, :SG_WIDTH]
        xhat, rstd = _ln_stats(ge[:, SG_WIDTH:])
        lg = lg_ref[...]
        vn = xhat * lg + lb_ref[...]
        dyv = dy_ref[...].astype(F32)
        tril = _tril_mask()
        for ci in range(chunks):
            rows = slice(ci * SG_CHUNK, (ci + 1) * SG_CHUNK)
            for g in range(SG_GROUPS):
                cols = slice(g * LANES, (g + 1) * LANES)
                wm = jnp.where(tril, w_ref[g], 0.0).astype(BF16)
                vg = vn[rows, cols].astype(BF16)
                mixed = jnp.dot(wm, vg, preferred_element_type=F32) + b_ref[g]
                dyb = dyv[rows, cols]
                du_scr[rows, cols] = dyb * mixed
                dmix = dyb * u[rows, cols]
                db_ref[g] += jnp.broadcast_to(jnp.sum(dmix, axis=1, keepdims=True), (SG_CHUNK, LANES))
                dmb = dmix.astype(BF16)
                dwg = lax.dot_general(dmb, vg, _DIMS["nt"], preferred_element_type=F32)
                dw_ref[g] += jnp.where(tril, dwg, 0.0)
                dvn_scr[rows, cols] = lax.dot_general(wm, dmb, _DIMS["tn"], preferred_element_type=F32)
        dvn = dvn_scr[...]
        _accumulate(dlg_ref, jnp.sum(dvn * xhat, axis=0, keepdims=True), first)
        _accumulate(dlb_ref, jnp.sum(dvn, axis=0, keepdims=True), first)
        dvv = _ln_bwd(dvn, xhat, rstd, lg)
        gg = _gelu_grad(z)
        dz_ref[:, :SG_WIDTH] = (du_scr[...] * gg[:, :SG_WIDTH]).astype(BF16)
        dz_ref[:, SG_WIDTH:] = (dvv * gg[:, SG_WIDTH:]).astype(BF16)

    vec, mat = _sg_specs()
    mat_shape = jax.ShapeDtypeStruct((SG_GROUPS, SG_CHUNK, SG_CHUNK), F32)
    vec_shape = jax.ShapeDtypeStruct((1, SG_WIDTH), F32)
    return pl.pallas_call(
        body, name=name,
        out_shape=(jax.ShapeDtypeStruct((s, 2 * SG_WIDTH), BF16), vec_shape, vec_shape, mat_shape, mat_shape),
        grid=(s // TOKEN_TILE,),
        in_specs=[pl.BlockSpec((TOKEN_TILE, 2 * SG_WIDTH), lambda i: (i, 0)),
                  pl.BlockSpec((TOKEN_TILE, SG_WIDTH), lambda i: (i, 0)), vec, vec, mat, mat],
        out_specs=(pl.BlockSpec((TOKEN_TILE, 2 * SG_WIDTH), lambda i: (i, 0)), vec, vec, mat, mat),
        scratch_shapes=[pltpu.VMEM((TOKEN_TILE, SG_WIDTH), F32), pltpu.VMEM((TOKEN_TILE, SG_WIDTH), F32)],
        compiler_params=_params(("arbitrary",)))(proj, dy, ln_g, ln_b, w_s, b_s)


_B_A_BLOCK = 1024 // LANES
_B_G_BLOCK = 1536 // LANES
_CH_TILES = CV_WIDTH // LANES


def _col_spec(s, first_block):
    return pl.BlockSpec((s, LANES), lambda j: (0, first_block + j))


def conv_b_fwd(proj, w_pad, bias, name):
    s = proj.shape[0]

    def body(a_ref, g_ref, w_ref, b_ref, c_ref, upad):
        upad[0:HALO, :] = jnp.zeros((HALO, LANES), F32)
        upad[HALO:, :] = a_ref[...].astype(F32) * _sigmoid(g_ref[...].astype(F32))
        w = w_ref[...]
        bv = b_ref[...]

        def block(bi, carry):
            start = pl.multiple_of(bi * CONV_ROWS, CONV_ROWS)
            win = upad[pl.ds(start, CONV_ROWS + HALO), :]
            acc = jnp.zeros((CONV_ROWS, LANES), F32)
            for k in range(CV_KERNEL):
                acc = acc + w[k:k + 1, :] * _shift_rows(win, HALO - (CV_KERNEL - 1) + k, CONV_ROWS)
            c_ref[pl.ds(start, CONV_ROWS), :] = acc + bv
            return carry

        lax.fori_loop(0, s // CONV_ROWS, block, 0)

    return pl.pallas_call(
        body, name=name, out_shape=jax.ShapeDtypeStruct((s, CV_WIDTH), F32), grid=(_CH_TILES,),
        in_specs=[_col_spec(s, _B_A_BLOCK), _col_spec(s, _B_G_BLOCK), _col_spec(HALO, 0), _col_spec(1, 0)],
        out_specs=_col_spec(s, 0), scratch_shapes=[pltpu.VMEM((s + HALO, LANES), F32)],
        compiler_params=_params(("parallel",)))(proj, proj, w_pad, bias)


def conv_b_bwd(proj, w_pad, dc, name):
    s = proj.shape[0]

    def body(a_ref, g_ref, w_ref, dc_ref, da_ref, dg_ref, dw_ref, db_ref, upad, dpad, dw_scr):
        upad[0:HALO, :] = jnp.zeros((HALO, LANES), F32)
        upad[HALO:, :] = a_ref[...].astype(F32) * _sigmoid(g_ref[...].astype(F32))
        dcv = dc_ref[...]
        dpad[0:s, :] = dcv
        dpad[s:, :] = jnp.zeros((HALO, LANES), F32)
        db_ref[...] = jnp.sum(dcv, axis=0, keepdims=True)
        dw_scr[...] = jnp.zeros((HALO, LANES), F32)
        w = w_ref[...]

        def block(bi, carry):
            start = pl.multiple_of(bi * CONV_ROWS, CONV_ROWS)
            uwin = upad[pl.ds(start, CONV_ROWS + HALO), :]
            dwin = dpad[pl.ds(start, CONV_ROWS + HALO), :]
            dcb = dwin[:CONV_ROWS]
            du = jnp.zeros((CONV_ROWS, LANES), F32)
            for k in range(CV_KERNEL):
                du = du + w[k:k + 1, :] * _shift_rows(dwin, CV_KERNEL - 1 - k, CONV_ROWS)
                ush = _shift_rows(uwin, HALO - (CV_KERNEL - 1) + k, CONV_ROWS)
                dw_scr[k:k + 1, :] += jnp.sum(dcb * ush, axis=0, keepdims=True)
            av = a_ref[pl.ds(start, CONV_ROWS), :].astype(F32)
            sg = _sigmoid(g_ref[pl.ds(start, CONV_ROWS), :].astype(F32))
            da_ref[pl.ds(start, CONV_ROWS), :] = (du * sg).astype(BF16)
            dg_ref[pl.ds(start, CONV_ROWS), :] = (du * av * sg * (1.0 - sg)).astype(BF16)
            return carry

        lax.fori_loop(0, s // CONV_ROWS, block, 0)
        dw_ref[...] = dw_scr[...]

    act = jax.ShapeDtypeStruct((s, CV_WIDTH), BF16)
    return pl.pallas_call(
        body, name=name,
        out_shape=(act, act, jax.ShapeDtypeStruct((HALO, CV_WIDTH), F32), jax.ShapeDtypeStruct((1, CV_WIDTH), F32)),
        grid=(_CH_TILES,),
        in_specs=[_col_spec(s, _B_A_BLOCK), _col_spec(s, _B_G_BLOCK), _col_spec(HALO, 0), _col_spec(s, 0)],
        out_specs=(_col_spec(s, 0), _col_spec(s, 0), _col_spec(HALO, 0), _col_spec(1, 0)),
        scratch_shapes=[pltpu.VMEM((s + HALO, LANES), F32), pltpu.VMEM((s + HALO, LANES), F32),
                        pltpu.VMEM((HALO, LANES), F32)],
        compiler_params=_params(("parallel",)))(proj, proj, w_pad, dc)


def ln_silu_fwd(c, ln_g, ln_b, name):
    s, d = c.shape

    def body(c_ref, g_ref, b_ref, o_ref):
        xhat, _ = _ln_stats(c_ref[...])
        cn = xhat * g_ref[...] + b_ref[...]
        o_ref[...] = (cn * _sigmoid(cn)).astype(BF16)

    tile = pl.BlockSpec((NORM_TILE, d), lambda i: (i, 0))
    return pl.pallas_call(
        body, name=name, out_shape=jax.ShapeDtypeStruct((s, d), BF16), grid=(s // NORM_TILE,),
        in_specs=[tile, _row_spec(d), _row_spec(d)], out_specs=tile,
        compiler_params=_params(("parallel",)))(c, ln_g, ln_b)


def ln_silu_bwd(c, dy, ln_g, ln_b, name):
    s, d = c.shape

    def body(c_ref, dy_ref, g_ref, b_ref, dc_ref, dg_ref, db_ref):
        first = pl.program_id(0) == 0
        xhat, rstd = _ln_stats(c_ref[...])
        gv = g_ref[...]
        cn = xhat * gv + b_ref[...]
        sg = _sigmoid(cn)
        dcn = dy_ref[...].astype(F32) * sg * (1.0 + cn * (1.0 - sg))
        _accumulate(dg_ref, jnp.sum(dcn * xhat, axis=0, keepdims=True), first)
        _accumulate(db_ref, jnp.sum(dcn, axis=0, keepdims=True), first)
        dc_ref[...] = _ln_bwd(dcn, xhat, rstd, gv)

    tile = pl.BlockSpec((NORM_TILE, d), lambda i: (i, 0))
    vec_shape = jax.ShapeDtypeStruct((1, d), F32)
    return pl.pallas_call(
        body, name=name, out_shape=(jax.ShapeDtypeStruct((s, d), F32), vec_shape, vec_shape),
        grid=(s // NORM_TILE,), in_specs=[tile, tile, _row_spec(d), _row_spec(d)],
        out_specs=(tile, _row_spec(d), _row_spec(d)), compiler_params=_params(("arbitrary",)))(c, dy, ln_g, ln_b)


_D_BLOCK = 2816 // LANES


def _conv3(win, w):
    acc = jnp.zeros((CONV_ROWS, LANES), F32)
    for k in range(SC_KERNEL):
        acc = acc + w[k:k + 1, :] * _shift_rows(win, HALO - (SC_KERNEL - 1) + k, CONV_ROWS)
    return acc


def conv_d_fwd(proj, w_pad, name):
    s = proj.shape[0]

    def body(bg_ref, cg_ref, h_ref, w_ref, o_ref, ppad):
        ppad[0:HALO, :] = jnp.zeros((HALO, LANES), F32)
        ppad[HALO:, :] = cg_ref[...].astype(F32) * h_ref[...].astype(F32)
        w = w_ref[...]

        def block(bi, carry):
            start = pl.multiple_of(bi * CONV_ROWS, CONV_ROWS)
            cv = _conv3(ppad[pl.ds(start, CONV_ROWS + HALO), :], w)
            o_ref[pl.ds(start, CONV_ROWS), :] = (bg_ref[pl.ds(start, CONV_ROWS), :].astype(F32) * cv).astype(BF16)
            return carry

        lax.fori_loop(0, s // CONV_ROWS, block, 0)

    return pl.pallas_call(
        body, name=name, out_shape=jax.ShapeDtypeStruct((s, SC_WIDTH), BF16), grid=(_CH_TILES,),
        in_specs=[_col_spec(s, _D_BLOCK), _col_spec(s, _D_BLOCK + _CH_TILES), _col_spec(s, _D_BLOCK + 2 * _CH_TILES),
                  _col_spec(SUBLANES, 0)],
        out_specs=_col_spec(s, 0), scratch_shapes=[pltpu.VMEM((s + HALO, LANES), F32)],
        compiler_params=_params(("parallel",)))(proj, proj, proj, w_pad)


def conv_d_bwd(proj, w_pad, dy, name):
    s = proj.shape[0]

    def body(bg_ref, cg_ref, h_ref, w_ref, dy_ref, dbg_ref, dcg_ref, dh_ref, dw_ref, ppad, dpad, dw_scr):
        ppad[0:HALO, :] = jnp.zeros((HALO, LANES), F32)
        ppad[HALO:, :] = cg_ref[...].astype(F32) * h_ref[...].astype(F32)
        dpad[0:s, :] = dy_ref[...].astype(F32) * bg_ref[...].astype(F32)
        dpad[s:, :] = jnp.zeros((HALO, LANES), F32)
        dw_scr[...] = jnp.zeros((SUBLANES, LANES), F32)
        w = w_ref[...]

        def block(bi, carry):
            start = pl.multiple_of(bi * CONV_ROWS, CONV_ROWS)
            rows = pl.ds(start, CONV_ROWS)
            pwin = ppad[pl.ds(start, CONV_ROWS + HALO), :]
            dwin = dpad[pl.ds(start, CONV_ROWS + HALO), :]
            dcvb = dwin[:CONV_ROWS]
            dbg_ref[rows, :] = (dy_ref[rows, :].astype(F32) * _conv3(pwin, w)).astype(BF16)
            dp = jnp.zeros((CONV_ROWS, LANES), F32)
            for k in range(SC_KERNEL):
                dp = dp + w[k:k + 1, :] * _shift_rows(dwin, SC_KERNEL - 1 - k, CONV_ROWS)
                psh = _shift_rows(pwin, HALO - (SC_KERNEL - 1) + k, CONV_ROWS)
                dw_scr[k:k + 1, :] += jnp.sum(dcvb * psh, axis=0, keepdims=True)
            dcg_ref[rows, :] = (dp * h_ref[rows, :].astype(F32)).astype(BF16)
            dh_ref[rows, :] = (dp * cg_ref[rows, :].astype(F32)).astype(BF16)
            return carry

        lax.fori_loop(0, s // CONV_ROWS, block, 0)
        dw_ref[...] = dw_scr[...]

    act = jax.ShapeDtypeStruct((s, SC_WIDTH), BF16)
    return pl.pallas_call(
        body, name=name, out_shape=(act, act, act, jax.ShapeDtypeStruct((SUBLANES, SC_WIDTH), F32)),
        grid=(_CH_TILES,),
        in_specs=[_col_spec(s, _D_BLOCK), _col_spec(s, _D_BLOCK + _CH_TILES), _col_spec(s, _D_BLOCK + 2 * _CH_TILES),
                  _col_spec(SUBLANES, 0), _col_spec(s, 0)],
        out_specs=(_col_spec(s, 0), _col_spec(s, 0), _col_spec(s, 0), _col_spec(SUBLANES, 0)),
        scratch_shapes=[pltpu.VMEM((s + HALO, LANES), F32), pltpu.VMEM((s + HALO, LANES), F32),
                        pltpu.VMEM((SUBLANES, LANES), F32)],
        compiler_params=_params(("parallel",)))(proj, proj, proj, w_pad, dy)


_QK_BLOCK = 2048 // LANES
_QK_BLOCKS = (Q_WIDTH + KV_WIDTH) // LANES


def _swap_halves(t):
    lane = lax.broadcasted_iota(jnp.int32, t.shape, 1)
    low = (lane % HEAD_DIM) < (HEAD_DIM // 2)
    return jnp.where(low, pltpu.roll(t, LANES - HEAD_DIM // 2, axis=1), pltpu.roll(t, HEAD_DIM // 2, axis=1))


def rope_fwd(proj, cos_t, sin_t, name, after=None):
    s = proj.shape[0]

    def body(t_ref, c_ref, s_ref, *rest):
        t = t_ref[...].astype(F32)
        rest[-1][...] = (t * c_ref[...] + _swap_halves(t) * s_ref[...]).astype(BF16)

    tr = min(ROPE_TILE, s)
    tab = pl.BlockSpec((tr, LANES), lambda i, j: (i, 0))
    extra = () if after is None else (after,)
    return pl.pallas_call(
        body, name=name, out_shape=jax.ShapeDtypeStruct((s, Q_WIDTH + KV_WIDTH), BF16),
        grid=(s // tr, _QK_BLOCKS),
        in_specs=[pl.BlockSpec((tr, LANES), lambda i, j: (i, _QK_BLOCK + j)), tab, tab]
        + [pl.BlockSpec(memory_space=pl.ANY) for _ in extra],
        out_specs=pl.BlockSpec((tr, LANES), lambda i, j: (i, j)),
        compiler_params=_params(("parallel", "parallel")))(proj, cos_t, sin_t, *extra)


def rope_bwd(d_cur, d_prev, cos_t, sin_t, name):
    s, w = d_cur.shape

    def body(a_ref, b_ref, c_ref, s_ref, o_ref):
        d = a_ref[...] + b_ref[...]
        o_ref[...] = (d * c_ref[...] + _swap_halves(d) * s_ref[...]).astype(BF16)

    tr = min(ROPE_TILE, s)
    tab = pl.BlockSpec((tr, LANES), lambda i, j: (i, 0))
    blk = pl.BlockSpec((tr, LANES), lambda i, j: (i, j))
    return pl.pallas_call(
        body, name=name, out_shape=jax.ShapeDtypeStruct((s, w), BF16), grid=(s // tr, w // LANES),
        in_specs=[blk, blk, tab, tab], out_specs=blk,
        compiler_params=_params(("parallel", "parallel")))(d_cur, d_prev, cos_t, sin_t)


_GROUP = N_Q_HEADS // N_KV_HEADS
_NEG = -1e30


def _attn_specs():
    q_spec = pl.BlockSpec((_GROUP, WINDOW, HEAD_DIM), lambda h, n: (h, n, 0))
    cur = pl.BlockSpec((1, WINDOW, HEAD_DIM), lambda h, n: (h, n, 0))
    prev = pl.BlockSpec((1, WINDOW, HEAD_DIM), lambda h, n: (h, jnp.maximum(n - 1, 0), 0))
    sink = pl.BlockSpec((_GROUP, 1, LANES), lambda h, n: (h, 0, 0))
    return q_spec, cur, prev, sink


def _attn_valid(n):
    qi = lax.broadcasted_iota(jnp.int32, (WINDOW, 2 * WINDOW), 0)
    kj = lax.broadcasted_iota(jnp.int32, (WINDOW, 2 * WINDOW), 1)
    delta = qi + WINDOW - kj
    return (delta >= 0) & (delta < WINDOW) & ((kj >= WINDOW) | (n > 0))


def _attn_probs(q, kcat, valid, sink_row):
    sc = lax.dot_general(q, kcat, _DIMS["nt"], preferred_element_type=F32) * (HEAD_DIM ** -0.5)
    sc = jnp.where(valid, sc, _NEG)
    sink = jnp.max(sink_row, axis=-1, keepdims=True)
    m = jnp.maximum(jnp.max(sc, axis=-1, keepdims=True), sink)
    p = jnp.where(valid, jnp.exp(sc - m), 0.0)
    es = jnp.exp(sink - m)
    inv = 1.0 / (jnp.sum(p, axis=-1, keepdims=True) + es)
    return p * inv, es * inv


def attention_fwd(qh, kh, vh, sinks_b, name):
    s = qh.shape[1]

    def body(q_ref, kc_ref, kp_ref, vc_ref, vp_ref, sk_ref, o_ref):
        valid = _attn_valid(pl.program_id(1))
        kcat = jnp.concatenate([kp_ref[0], kc_ref[0]], axis=0)
        vcat = jnp.concatenate([vp_ref[0], vc_ref[0]], axis=0)
        for g in range(_GROUP):
            probs, _ = _attn_probs(q_ref[g], kcat, valid, sk_ref[g])
            o_ref[g] = jnp.dot(probs.astype(BF16), vcat, preferred_element_type=F32).astype(BF16)

    q_spec, cur, prev, sink = _attn_specs()
    return pl.pallas_call(
        body, name=name, out_shape=jax.ShapeDtypeStruct(qh.shape, BF16), grid=(N_KV_HEADS, s // WINDOW),
        in_specs=[q_spec, cur, prev, cur, prev, sink], out_specs=q_spec,
        compiler_params=_params(("parallel", "parallel")))(qh, kh, kh, vh, vh, sinks_b)


def attention_bwd(qh, kh, vh, sinks_b, doh, name):
    s = qh.shape[1]

    def body(q_ref, kc_ref, kp_ref, vc_ref, vp_ref, sk_ref, do_ref, dq_ref, dkc_ref, dkp_ref, dvc_ref, dvp_ref, ds_ref):
        n = pl.program_id(1)
        valid = _attn_valid(n)
        kcat = jnp.concatenate([kp_ref[0], kc_ref[0]], axis=0)
        vcat = jnp.concatenate([vp_ref[0], vc_ref[0]], axis=0)
        dk = jnp.zeros((2 * WINDOW, HEAD_DIM), F32)
        dv = jnp.zeros((2 * WINDOW, HEAD_DIM), F32)
        for g in range(_GROUP):
            q = q_ref[g]
            do = do_ref[g]
            probs, ps = _attn_probs(q, kcat, valid, sk_ref[g])
            dprobs = lax.dot_general(do, vcat, _DIMS["nt"], preferred_element_type=F32)
            dv = dv + lax.dot_general(probs.astype(BF16), do, _DIMS["tn"], preferred_element_type=F32)
            rs = jnp.sum(probs * dprobs, axis=-1, keepdims=True)
            dsb = (probs * (dprobs - rs) * (HEAD_DIM ** -0.5)).astype(BF16)
            dq_ref[g] = jnp.dot(dsb, kcat, preferred_element_type=F32)
            dk = dk + lax.dot_general(dsb, q, _DIMS["tn"], preferred_element_type=F32)
            dsink = jnp.broadcast_to(-jnp.sum(ps * rs, axis=0, keepdims=True), (1, LANES))

            @pl.when(n == 0)
            def _():
                ds_ref[g] = dsink

            @pl.when(n > 0)
            def _():
                ds_ref[g] += dsink

        dkp_ref[0] = dk[:WINDOW]
        dkc_ref[0] = dk[WINDOW:]
        dvp_ref[0] = dv[:WINDOW]
        dvc_ref[0] = dv[WINDOW:]

    q_spec, cur, prev, sink = _attn_specs()
    kv_shape = jax.ShapeDtypeStruct(kh.shape, F32)
    return pl.pallas_call(
        body, name=name,
        out_shape=(jax.ShapeDtypeStruct(qh.shape, F32), kv_shape, kv_shape, kv_shape, kv_shape,
                   jax.ShapeDtypeStruct(sinks_b.shape, F32)),
        grid=(N_KV_HEADS, s // WINDOW), in_specs=[q_spec, cur, prev, cur, prev, sink, q_spec],
        out_specs=(q_spec, cur, cur, cur, cur, sink),
        compiler_params=_params(("parallel", "arbitrary")))(qh, kh, kh, vh, vh, sinks_b, doh)


def _to_heads(t, heads):
    return t.reshape(t.shape[0], heads, HEAD_DIM).transpose(1, 0, 2)


def _from_heads(t):
    return t.transpose(1, 0, 2).reshape(t.shape[1], t.shape[0] * HEAD_DIM)


def _shift_window(t):
    return jnp.concatenate([t[:, WINDOW:], jnp.zeros_like(t[:, :WINDOW])], axis=1)


def merge_fwd(zg, branches, name):
    s = zg.shape[0]

    def body(zg_ref, b0, b1, b2, b3, o_ref):
        acc = jnp.zeros((TOKEN_TILE, D_MODEL), F32)
        for n, b_ref in enumerate((b0, b1, b2, b3)):
            gate = _sigmoid(zg_ref[:, n * D_MODEL:(n + 1) * D_MODEL].astype(F32))
            acc = acc + gate * b_ref[...].astype(F32)
        o_ref[...] = acc.astype(BF16)

    tile = pl.BlockSpec((TOKEN_TILE, D_MODEL), lambda i: (i, 0))
    wide = pl.BlockSpec((TOKEN_TILE, N_BRANCH * D_MODEL), lambda i: (i, 0))
    return pl.pallas_call(
        body, name=name, out_shape=jax.ShapeDtypeStruct((s, D_MODEL), BF16), grid=(s // TOKEN_TILE,),
        in_specs=[wide, tile, tile, tile, tile], out_specs=tile,
        compiler_params=_params(("parallel",)))(zg, *branches)


def merge_bwd(zg, branches, dm, name):
    s = zg.shape[0]

    def body(zg_ref, b0, b1, b2, b3, dm_ref, dzg_ref, d0, d1, d2, d3):
        dmv = dm_ref[...].astype(F32)
        for n, (b_ref, d_ref) in enumerate(((b0, d0), (b1, d1), (b2, d2), (b3, d3))):
            cols = slice(n * D_MODEL, (n + 1) * D_MODEL)
            gate = _sigmoid(zg_ref[:, cols].astype(F32))
            d_ref[...] = (gate * dmv).astype(BF16)
            dzg_ref[:, cols] = (dmv * b_ref[...].astype(F32) * gate * (1.0 - gate)).astype(BF16)

    tile = pl.BlockSpec((TOKEN_TILE, D_MODEL), lambda i: (i, 0))
    wide = pl.BlockSpec((TOKEN_TILE, N_BRANCH * D_MODEL), lambda i: (i, 0))
    act = jax.ShapeDtypeStruct((s, D_MODEL), BF16)
    return pl.pallas_call(
        body, name=name, out_shape=(jax.ShapeDtypeStruct((s, N_BRANCH * D_MODEL), BF16), act, act, act, act),
        grid=(s // TOKEN_TILE,), in_specs=[wide, tile, tile, tile, tile, tile],
        out_specs=(wide, tile, tile, tile, tile), compiler_params=_params(("parallel",)))(zg, *branches, dm)


def swiglu_fwd(gu, name):
    s = gu.shape[0]

    def body(g_ref, u_ref, o_ref):
        gate = g_ref[...].astype(F32)
        o_ref[...] = (gate * _sigmoid(gate) * u_ref[...].astype(F32)).astype(BF16)

    return pl.pallas_call(
        body, name=name, out_shape=jax.ShapeDtypeStruct((s, D_FF), BF16), grid=(s // TOKEN_TILE,),
        in_specs=[pl.BlockSpec((TOKEN_TILE, D_FF), lambda i: (i, 0)), pl.BlockSpec((TOKEN_TILE, D_FF), lambda i: (i, 1))],
        out_specs=pl.BlockSpec((TOKEN_TILE, D_FF), lambda i: (i, 0)), compiler_params=_params(("parallel",)))(gu, gu)


def swiglu_bwd(gu, dact, name):
    s = gu.shape[0]

    def body(g_ref, u_ref, da_ref, o_ref):
        gate = g_ref[...].astype(F32)
        sg = _sigmoid(gate)
        da = da_ref[...].astype(F32)
        o_ref[:, :D_FF] = (da * u_ref[...].astype(F32) * sg * (1.0 + gate * (1.0 - sg))).astype(BF16)
        o_ref[:, D_FF:] = (da * gate * sg).astype(BF16)

    half = pl.BlockSpec((TOKEN_TILE, D_FF), lambda i: (i, 0))
    return pl.pallas_call(
        body, name=name, out_shape=jax.ShapeDtypeStruct((s, 2 * D_FF), BF16), grid=(s // TOKEN_TILE,),
        in_specs=[half, pl.BlockSpec((TOKEN_TILE, D_FF), lambda i: (i, 1)), half],
        out_specs=pl.BlockSpec((TOKEN_TILE, 2 * D_FF), lambda i: (i, 0)),
        compiler_params=_params(("parallel",)))(gu, gu, dact)


ADAMW_BLOCK_BYTES = 1 << 20


def adamw(parts, w, m, v, name):
    n_parts, r, c = w.shape
    tr = _divisor_tile(r, max(SUBLANES, ADAMW_BLOCK_BYTES // (4 * c)), SUBLANES)
    tiles = r // tr

    def part_spec(j):
        return pl.BlockSpec((N_DEV, tr, c), lambda i: (0, jnp.clip(i - j * tiles, 0, tiles - 1), 0))

    def body(*refs):
        p_refs = refs[:n_parts]
        w_ref, m_ref, v_ref, g_ref, d_ref, nm_ref, nv_ref = refs[n_parts:]
        which = pl.program_id(0) // tiles
        g = None
        for j, p_ref in enumerate(p_refs):
            gj = p_ref[0].astype(F32)
            for i in range(1, N_DEV):
                gj = gj + p_ref[i].astype(F32)
            g = gj if g is None else jnp.where(which == j, gj, g)
        nm = ADAM_B1 * m_ref[0] + (1.0 - ADAM_B1) * g
        nv = ADAM_B2 * v_ref[0] + (1.0 - ADAM_B2) * (g * g)
        m_hat = nm / (1.0 - ADAM_B1 ** ADAM_STEP)
        v_hat = nv / (1.0 - ADAM_B2 ** ADAM_STEP)
        g_ref[0] = g
        d_ref[0] = -ADAM_LR * (m_hat / (jnp.sqrt(v_hat) + ADAM_EPS) + ADAM_WD * w_ref[0])
        nm_ref[0] = nm
        nv_ref[0] = nv

    tile = pl.BlockSpec((1, tr, c), lambda i: (i // tiles, i % tiles, 0))
    shape = jax.ShapeDtypeStruct(w.shape, F32)
    return pl.pallas_call(
        body, name=name, out_shape=(shape, shape, shape, shape), grid=(n_parts * tiles,),
        in_specs=[part_spec(j) for j in range(n_parts)] + [tile, tile, tile],
        out_specs=(tile, tile, tile, tile), compiler_params=_params(("parallel",)))(*parts, w, m, v)


_RELATIONS = [(a, b, e) for a in (0, 1) for b in (0, 1) for e in (0, 1)][1:]


_HBM_SPEC = pl.BlockSpec(memory_space=pltpu.HBM)
_SEM_SPEC = pl.BlockSpec(memory_space=pltpu.SEMAPHORE)
_ANY_SPEC = pl.BlockSpec(memory_space=pl.ANY)
_DATAFLOW = pltpu.SideEffectType.DATAFLOW_SIDE_EFFECTING


_OTHER_CHIPS = [(1, 0), (0, 1), (1, 1)]
_FIRST_LEVEL = [(0, 0, 1)] + [(a, b, 0) for a, b in _OTHER_CHIPS]


def _remote_copies(ins, lands, send_sems, recv_sems, scatter, relations):
    x, y, c = lax.axis_index("x"), lax.axis_index("y"), lax.axis_index("c")
    me = 4 * x + 2 * y + c
    copies = []
    for t in range(len(ins)):
        for k, (a, b, e) in enumerate(relations):
            px, py, pc = (x + a) % 2, (y + b) % 2, (c + e) % 2
            src = ins[t].at[4 * px + 2 * py + pc] if scatter[t] else ins[t]
            copies.append(pltpu.make_async_remote_copy(
                src_ref=src, dst_ref=lands[t].at[me], send_sem=send_sems.at[t * len(relations) + k],
                recv_sem=recv_sems.at[t * len(relations) + k],
                device_id=(px, py, pc), device_id_type=pl.DeviceIdType.MESH))
    return copies


def _forward_copies(lands, send_sems, recv_sems):
    x, y, c = lax.axis_index("x"), lax.axis_index("y"), lax.axis_index("c")
    copies = []
    for t in range(len(lands)):
        for k, (a, b) in enumerate(_OTHER_CHIPS):
            slot = lands[t].at[4 * ((x + a) % 2) + 2 * ((y + b) % 2) + c]
            copies.append(pltpu.make_async_remote_copy(
                src_ref=slot, dst_ref=slot, send_sem=send_sems.at[t * len(_OTHER_CHIPS) + k],
                recv_sem=recv_sems.at[t * len(_OTHER_CHIPS) + k],
                device_id=(x, y, 1 - c), device_id_type=pl.DeviceIdType.MESH))
    return copies


def exchange_start(arrays, scatter, name, after=None, relations=_RELATIONS):
    n = len(arrays)
    n_rel = len(relations)
    land_shapes = [a.shape if scatter[t] else (N_DEV,) + a.shape for t, a in enumerate(arrays)]

    def body(*refs):
        ins, lands = refs[:n], refs[n:2 * n]
        send_sems, recv_sems = refs[-2 * n - 4], refs[-2 * n - 3]
        token, local_sems = refs[-2], refs[-1]
        for cp in _remote_copies(ins, lands, send_sems, recv_sems, scatter, relations):
            cp.start()
        me = 4 * lax.axis_index("x") + 2 * lax.axis_index("y") + lax.axis_index("c")
        own = [pltpu.make_async_copy(ins[t].at[me] if scatter[t] else ins[t], lands[t].at[me], local_sems.at[t])
               for t in range(n)]
        for cp in own:
            cp.start()
        for cp in own:
            cp.wait()
        token[...] = jnp.zeros_like(token)

    sems = pltpu.SemaphoreType.DMA((n * n_rel,))
    out_shape = ((sems, sems) + tuple(pltpu.HBM(a.shape, a.dtype) for a in arrays)
                 + tuple(pltpu.HBM(s, a.dtype) for s, a in zip(land_shapes, arrays))
                 + (jax.ShapeDtypeStruct((SUBLANES, LANES), F32),))
    operands = [pltpu.with_memory_space_constraint(a, pltpu.HBM) for a in arrays]
    operands += [pltpu.with_memory_space_constraint(lax.empty(s, a.dtype), pltpu.HBM) for s, a in zip(land_shapes, arrays)]
    in_specs = [_HBM_SPEC] * (2 * n)
    if after is not None:
        operands.append(after)
        in_specs.append(_ANY_SPEC)
    res = pl.pallas_call(
        body, name=name, out_shape=out_shape, in_specs=in_specs,
        out_specs=(_SEM_SPEC, _SEM_SPEC) + (_HBM_SPEC,) * (2 * n) + (pl.BlockSpec(memory_space=pltpu.VMEM),),
        input_output_aliases={i: 2 + i for i in range(2 * n)}, scratch_shapes=[pltpu.SemaphoreType.DMA((n,))],
        compiler_params=pltpu.CompilerParams(has_side_effects=_DATAFLOW))(*operands)
    handle = (res[0], res[1], res[2:2 + n], res[2 + n:2 + 2 * n], tuple(scatter), relations)
    return handle, res[-1]


def exchange_wait(handle, after, name):
    send_sems, recv_sems, sources, lands, scatter, relations = handle
    n = len(sources)

    def body(*refs):
        ins, lzs = refs[:n], refs[n:2 * n]
        send_ref, recv_ref = refs[2 * n], refs[2 * n + 1]
        for cp in _remote_copies(ins, lzs, send_ref, recv_ref, scatter, relations):
            cp.wait_send()
            cp.wait_recv()

    out_shape = (tuple(pltpu.HBM(a.shape, a.dtype) for a in sources) + tuple(pltpu.HBM(a.shape, a.dtype) for a in lands))
    res = pl.pallas_call(
        body, name=name, out_shape=out_shape, in_specs=[_HBM_SPEC] * (2 * n) + [_SEM_SPEC, _SEM_SPEC, _ANY_SPEC],
        out_specs=(_HBM_SPEC,) * (2 * n), input_output_aliases={i: i for i in range(2 * n)},
        compiler_params=pltpu.CompilerParams(has_side_effects=_DATAFLOW))(*sources, *lands, send_sems, recv_sems, after)
    return res[n:]


def forward_start(lands, name):
    n = len(lands)

    def body(*refs):
        send_sems, recv_sems, token = refs[n], refs[n + 1], refs[-1]
        for cp in _forward_copies(refs[:n], send_sems, recv_sems):
            cp.start()
        token[...] = jnp.zeros_like(token)

    sems = pltpu.SemaphoreType.DMA((n * len(_OTHER_CHIPS),))
    res = pl.pallas_call(
        body, name=name,
        out_shape=(sems, sems) + tuple(pltpu.HBM(a.shape, a.dtype) for a in lands)
        + (jax.ShapeDtypeStruct((SUBLANES, LANES), F32),),
        in_specs=[_HBM_SPEC] * n,
        out_specs=(_SEM_SPEC, _SEM_SPEC) + (_HBM_SPEC,) * n + (pl.BlockSpec(memory_space=pltpu.VMEM),),
        input_output_aliases={i: 2 + i for i in range(n)},
        compiler_params=pltpu.CompilerParams(has_side_effects=_DATAFLOW))(*lands)
    return (res[0], res[1], res[2:2 + n]), res[-1]


def forward_wait(handle, after, name):
    send_sems, recv_sems, lands = handle
    n = len(lands)

    def body(*refs):
        for cp in _forward_copies(refs[:n], refs[n], refs[n + 1]):
            cp.wait_send()
            cp.wait_recv()

    return pl.pallas_call(
        body, name=name, out_shape=tuple(pltpu.HBM(a.shape, a.dtype) for a in lands),
        in_specs=[_HBM_SPEC] * n + [_SEM_SPEC, _SEM_SPEC, _ANY_SPEC], out_specs=(_HBM_SPEC,) * n,
        input_output_aliases={i: i for i in range(n)},
        compiler_params=pltpu.CompilerParams(has_side_effects=_DATAFLOW))(*lands, send_sems, recv_sems, after)


_SMALL = ("norm_mix", "sg_ln_g", "sg_ln_b", "sg_b", "cv_b", "cv_ln_g", "cv_ln_b", "attn_sinks", "norm_ffn",
          "norm_final")
_PACK_UNIT = SUBLANES * LANES


def _pack(tensors):
    rows = []
    for t in tensors:
        flat = t.reshape(-1)
        pad = (-flat.shape[0]) % _PACK_UNIT
        rows.append(jnp.pad(flat, (0, pad)).reshape(-1, LANES))
    return jnp.concatenate(rows, axis=0)


def _unpack(packed, like):
    out, row = [], 0
    for t in like:
        size = 1
        for d in t.shape:
            size *= d
        rows = -(-size // _PACK_UNIT) * SUBLANES
        out.append(packed[row:row + rows].reshape(-1)[:size].reshape(t.shape))
        row += rows
    return out


def _layer_fwd(l, x, p, late_params, mid_hook=None, ffn_hook=None):
    tag = f"l{l}_"
    xn = rmsnorm_fwd(x, p["norm_mix"], tag + "norm_mix")
    proj = matmul(xn, p["w_in_t"], "nt", BF16, tag + "proj_a", tm_cap=1024, tn_cap=2176, b_rows=PROJ_A)
    zg = matmul(xn, p["w_in_t_g"], "nt", BF16, tag + "proj_g", tm_cap=1024, tn_cap=2048)
    y_a = mixer_a_fwd(proj, p["sg_ln_g"], p["sg_ln_b"], p["sg_w"], p["sg_b"], tag + "mix_a")
    conv = conv_b_fwd(proj, p["cv_w"], p["cv_b"], tag + "conv_b")
    y_b = ln_silu_fwd(conv, p["cv_ln_g"], p["cv_ln_b"], tag + "ln_silu")
    token = mid_hook(y_b) if mid_hook is not None else None
    qk = rope_fwd(proj, p["cos"], p["sin"], tag + "rope", after=token)
    qh = _to_heads(qk[:, :Q_WIDTH], N_Q_HEADS)
    kh = _to_heads(qk[:, Q_WIDTH:], N_KV_HEADS)
    vh = _to_heads(proj[:, 2688:2816], N_KV_HEADS)
    oh = attention_fwd(qh, kh, vh, p["sinks"], tag + "attn")
    y_c = _from_heads(oh)
    y_d = conv_d_fwd(proj, p["sc_w"], tag + "conv_d")
    ys = (y_a, y_b, y_c, y_d)
    p = {**p, **late_params(y_d)}
    branches = tuple(matmul(ys[n], p["w_branch"][n], "nn", BF16, tag + f"branch{n}", tm_cap=1024, tn_cap=1024)
                     for n in range(N_BRANCH))
    merged = merge_fwd(zg, branches, tag + "merge")
    x_mid = matmul(merged, p["w_out"], "nn", F32, tag + "out", add=x, tm_cap=1024, tn_cap=1024)
    token = ffn_hook(x_mid) if ffn_hook is not None else None
    hn = rmsnorm_fwd(x_mid, p["norm_ffn"], tag + "norm_ffn")
    gu = matmul(hn, p["w_gate_up_t"], "nt", BF16, tag + "gate_up", tm_cap=512, tn_cap=2816, after=token)
    act = swiglu_fwd(gu, tag + "swiglu")
    x_out = matmul(act, p["w_down"], "nn", F32, tag + "down", add=x_mid, tm_cap=512, tn_cap=1024)
    saved = dict(x=x, xn=xn, proj=proj, zg=zg, conv=conv, qh=qh, kh=kh, vh=vh, ys=ys, branches=branches,
                 merged=merged, x_mid=x_mid, hn=hn, gu=gu, act=act)
    return x_out, saved, p


def _layer_bwd(l, dx_out, p, sv, emit, after=None):
    tag = f"l{l}_b_"
    g = {}
    dact = matmul(dx_out, p["w_down"], "nt", BF16, tag + "dact", after=after, tm_cap=512, tn_cap=2816)
    dw_down = matmul(sv["act"], dx_out, "tn", BF16, tag + "dw_down", tm_cap=1408, tn_cap=512)
    dgu = swiglu_bwd(sv["gu"], dact, tag + "swiglu")
    dhn = matmul(dgu, p["w_gate_up_t"], "nn", BF16, tag + "dhn", tm_cap=512, tn_cap=512)
    dw_gate_up = matmul(dgu, sv["hn"], "tn", BF16, tag + "dw_gate_up", tm_cap=1408, tn_cap=1024)
    token = emit("a", {"w_gate_up": dw_gate_up, "w_down": dw_down})
    dx_mid, g["norm_ffn"] = rmsnorm_bwd(sv["x_mid"], p["norm_ffn"], dhn, dx_out, tag + "norm_ffn")
    dmerged = matmul(dx_mid, p["w_out"], "nt", BF16, tag + "dmerged", after=token, tm_cap=1024, tn_cap=1024)
    dw_out = matmul(sv["merged"], dx_mid, "tn", BF16, tag + "dw_out", tm_cap=1024, tn_cap=512)
    dzg, *dbranches = merge_bwd(sv["zg"], sv["branches"], dmerged, tag + "merge")
    dys = [matmul(dbranches[n], p["w_branch"][n], "nt", BF16, tag + f"dy{n}", tm_cap=1024, tn_cap=512)
           for n in range(N_BRANCH)]
    dw_branch = jnp.stack(
        [matmul(sv["ys"][n], dbranches[n], "tn", BF16, tag + f"dw_branch{n}", tm_cap=512, tn_cap=1024)
         for n in range(N_BRANCH)])
    token = emit("b", {"w_branch": dw_branch, "w_out": dw_out})
    proj = sv["proj"]
    dz_a, g["sg_ln_g"], g["sg_ln_b"], g["sg_w"], dsb = mixer_a_bwd(
        proj, dys[0], p["sg_ln_g"], p["sg_ln_b"], p["sg_w"], p["sg_b"], tag + "mix_a")
    g["sg_b"] = dsb[:, :, 0]
    dconv, g["cv_ln_g"], g["cv_ln_b"] = ln_silu_bwd(sv["conv"], dys[1], p["cv_ln_g"], p["cv_ln_b"], tag + "ln_silu")
    da, dgate, dcw, g["cv_b"] = conv_b_bwd(proj, p["cv_w"], dconv, tag + "conv_b")
    g["cv_w"] = dcw[:CV_KERNEL]
    doh = _to_heads(dys[2], N_Q_HEADS)
    dqh, dkc, dkp, dvc, dvp, dsk = attention_bwd(sv["qh"], sv["kh"], sv["vh"], p["sinks"], doh, tag + "attn")
    g["attn_sinks"] = dsk[:, 0, 0]
    dqk_cur = jnp.concatenate([_from_heads(dqh), _from_heads(dkc)], axis=1)
    dqk_prev = jnp.concatenate([jnp.zeros((SEQ, Q_WIDTH), F32), _from_heads(_shift_window(dkp))], axis=1)
    dqk = rope_bwd(dqk_cur, dqk_prev, p["cos"], -p["sin"], tag + "rope")
    dv = (_from_heads(dvc) + _from_heads(_shift_window(dvp))).astype(BF16)
    dbg, dcg, dh, dsw = conv_d_bwd(proj, p["sc_w"], dys[3], tag + "conv_d")
    g["sc_w"] = dsw[:SC_KERNEL]
    dproj = jnp.concatenate([dz_a, da, dgate, dqk, dv, dbg, dcg, dh], axis=1)
    dw_in = matmul(dproj, sv["xn"], "tn", BF16, tag + "dw_in_a", after=token, tm_cap=2176, tn_cap=512,
                   out_rows=PROJ_WIDTH)
    g["w_in"] = matmul(dzg, sv["xn"], "tn", BF16, tag + "dw_in_g", tm_cap=256, tn_cap=1024, into=dw_in,
                       into_row=PROJ_A)
    token = emit("c", {n: g.pop(n) for n in _EARLY}, {"sg_w": g.pop("sg_w")})
    dxn = matmul(dproj, p["w_in_t"], "nn", F32, tag + "dxn_a", after=token, tm_cap=512, tn_cap=512, b_rows=PROJ_A)
    dxn = matmul(dzg, p["w_in_t_g"], "nn", F32, tag + "dxn_g", add=dxn, tm_cap=512, tn_cap=512)
    dx_in, g["norm_mix"] = rmsnorm_bwd(sv["x"], p["norm_mix"], dxn, dx_mid, tag + "norm_mix")
    return dx_in, g, token


_EARLY = ("w_in", "cv_w", "sc_w")
_LATE = ("w_branch", "w_out", "w_gate_up", "w_down")


_TRANSPOSED = ("w_in", "w_gate_up")


def _shard_view(name, t):
    return jnp.swapaxes(t, 1, 2) if name in _TRANSPOSED else t


def _full_weight(name, t):
    if name in ("w_out", "w_down") + _TRANSPOSED:
        return t.reshape(-1, t.shape[-1])
    if name == "w_branch":
        return t.transpose(1, 2, 0, 3).reshape(N_BRANCH, SG_WIDTH, D_MODEL)
    return t.transpose(1, 0, 2).reshape(t.shape[1], -1)


def _to_blocks(name, full):
    if name in ("w_out", "w_down") + _TRANSPOSED:
        return full.reshape(N_DEV, -1, full.shape[-1])
    if name == "w_branch":
        return full.reshape(N_BRANCH, SG_WIDTH, N_DEV, -1).transpose(2, 0, 1, 3)
    return full.reshape(full.shape[0], N_DEV, -1).transpose(1, 0, 2)


def _rope_tables():
    pos = jnp.arange(SEQ, dtype=F32)
    inv_freq = 1.0 / (ROPE_THETA ** (jnp.arange(0, HEAD_DIM, 2, dtype=F32) / HEAD_DIM))
    ang = pos[:, None] * inv_freq[None, :]
    cos, sin = jnp.cos(ang), jnp.sin(ang)
    reps = LANES // HEAD_DIM
    return jnp.tile(jnp.concatenate([cos, cos], axis=1), (1, reps)), jnp.tile(jnp.concatenate([-sin, sin], axis=1), (1, reps))


def kernel(x, norm_mix, w_in, sg_ln_g, sg_ln_b, sg_w, sg_b, cv_w, cv_b, cv_ln_g, cv_ln_b, attn_sinks, sc_w, w_branch, w_out, norm_ffn, w_gate_up, w_down, norm_final, loss_target, m_norm_mix, m_w_in, m_sg_ln_g, m_sg_ln_b, m_sg_w, m_sg_b, m_cv_w, m_cv_b, m_cv_ln_g, m_cv_ln_b, m_attn_sinks, m_sc_w, m_w_branch, m_w_out, m_norm_ffn, m_w_gate_up, m_w_down, m_norm_final, v_norm_mix, v_w_in, v_sg_ln_g, v_sg_ln_b, v_sg_w, v_sg_b, v_cv_w, v_cv_b, v_cv_ln_g, v_cv_ln_b, v_attn_sinks, v_sc_w, v_w_branch, v_w_out, v_norm_ffn, v_w_gate_up, v_w_down, v_norm_final):
    names = ("norm_mix", "w_in", "sg_ln_g", "sg_ln_b", "sg_w", "sg_b", "cv_w", "cv_b", "cv_ln_g", "cv_ln_b",
             "attn_sinks", "sc_w", "w_branch", "w_out", "norm_ffn", "w_gate_up", "w_down", "norm_final")
    w = dict(zip(names, (norm_mix, w_in, sg_ln_g, sg_ln_b, sg_w, sg_b, cv_w, cv_b, cv_ln_g, cv_ln_b, attn_sinks,
                         sc_w, w_branch, w_out, norm_ffn, w_gate_up, w_down, norm_final)))
    m = dict(zip(names, (m_norm_mix, m_w_in, m_sg_ln_g, m_sg_ln_b, m_sg_w, m_sg_b, m_cv_w, m_cv_b, m_cv_ln_g,
                         m_cv_ln_b, m_attn_sinks, m_sc_w, m_w_branch, m_w_out, m_norm_ffn, m_w_gate_up, m_w_down,
                         m_norm_final)))
    v = dict(zip(names, (v_norm_mix, v_w_in, v_sg_ln_g, v_sg_ln_b, v_sg_w, v_sg_b, v_cv_w, v_cv_b, v_cv_ln_g,
                         v_cv_ln_b, v_attn_sinks, v_sc_w, v_w_branch, v_w_out, v_norm_ffn, v_w_gate_up, v_w_down,
                         v_norm_final)))

    gathers, forwards, token = {}, {}, None
    for l in range(DEPTH):
        for group in (_EARLY, _LATE):
            shards = [_shard_view(n, w[n])[l].astype(BF16) for n in group]
            handle, token = exchange_start(shards, [False] * len(group), f"gather_start{l}_{group[0]}", after=token,
                                           relations=_FIRST_LEVEL)
            gathers[(l, group)] = (handle, shards)

    def begin_forward(l, group, after):
        landed = exchange_wait(gathers[(l, group)][0], after, f"gather_wait{l}_{group[0]}")
        forwards[(l, group)], tok = forward_start(landed, f"forward_start{l}_{group[0]}")
        return tok

    def landed_weights(l, group, after):
        landed = forward_wait(forwards[(l, group)], after, f"forward_wait{l}_{group[0]}")
        return {n + "_t" if n in _TRANSPOSED else n: _full_weight(n, t) for n, t in zip(group, landed)}

    cos_t, sin_t = _rope_tables()

    def early_params(l, after):
        full = landed_weights(l, _EARLY, after)
        return dict(
            norm_mix=w["norm_mix"][l][None], norm_ffn=w["norm_ffn"][l][None],
            w_in_t=full["w_in_t"], w_in_t_g=full["w_in_t"][PROJ_A:],
            sg_ln_g=w["sg_ln_g"][l][None], sg_ln_b=w["sg_ln_b"][l][None], sg_w=w["sg_w"][l],
            sg_b=jnp.broadcast_to(w["sg_b"][l][:, :, None], (SG_GROUPS, SG_CHUNK, LANES)),
            cv_w=jnp.pad(full["cv_w"].astype(F32), ((0, HALO - CV_KERNEL), (0, 0))),
            cv_b=w["cv_b"][l][None], cv_ln_g=w["cv_ln_g"][l][None], cv_ln_b=w["cv_ln_b"][l][None],
            sinks=jnp.broadcast_to(w["attn_sinks"][l][:, None, None], (N_Q_HEADS, 1, LANES)),
            sc_w=jnp.pad(full["sc_w"].astype(F32), ((0, SUBLANES - SC_KERNEL), (0, 0))),
            cos=cos_t, sin=sin_t)

    params, saved = [None] * DEPTH, [None] * DEPTH
    h = x[0]
    after = begin_forward(0, _EARLY, token)
    for l in range(DEPTH):
        h, saved[l], params[l] = _layer_fwd(
            l, h, early_params(l, after), lambda behind, l=l: landed_weights(l, _LATE, behind),
            mid_hook=lambda behind, l=l: begin_forward(l, _LATE, behind),
            ffn_hook=(lambda behind, l=l: begin_forward(l + 1, _EARLY, behind)) if l + 1 < DEPTH else None)
        after = h
    loss_row, dh, d_norm_final = loss_head(h, w["norm_final"][None], loss_target[0], "loss_head")

    sent = {}

    def emitter(l):
        def emit(group, grads_of, replicated=None):
            replicated = replicated or {}
            send = [_to_blocks(n, grads_of[n].astype(BF16)) for n in grads_of] + list(replicated.values())
            flags = [True] * len(grads_of) + [False] * len(replicated)
            handle, tok = exchange_start(send, flags, f"grads_start{l}{group}")
            sent[(l, group)] = (handle, tuple(grads_of) + tuple(replicated))
            return tok
        return emit

    grads = [None] * DEPTH
    token = None
    for l in reversed(range(DEPTH)):
        dh, grads[l], token = _layer_bwd(l, dh, params[l], saved[l], emitter(l), after=token)
    grad_x = dh[None]

    stacked = {n: jnp.stack([grads[l][n] for l in range(DEPTH)]) for n in _SMALL if n != "norm_final"}
    for n in ("norm_mix", "norm_ffn", "sg_ln_g", "sg_ln_b", "cv_b", "cv_ln_g", "cv_ln_b"):
        stacked[n] = stacked[n][:, 0]
    stacked["norm_final"] = d_norm_final[0]
    no_state = jnp.zeros((1,), F32)
    small_like = [w[n] for n in _SMALL] + [no_state]
    small_part = _pack([stacked[n] for n in _SMALL] + [loss_row[0, :1]])
    handle_small, token = exchange_start([small_part], [False], "grads_start_small", after=token)

    def received(l, group, after):
        handle, group_names = sent[(l, group)]
        return dict(zip(group_names, exchange_wait(handle, after, f"grads_wait{l}{group}")))

    out_g, out_d, out_m, out_v = {}, {}, {}, {}

    def update(n, by_layer):
        shape = _shard_view(n, w[n]).shape
        view = (DEPTH, w[n].size // (DEPTH * shape[-1]), shape[-1])
        parts = [t.reshape((N_DEV,) + view[1:]) for t in by_layer]
        res = adamw(parts, *[_shard_view(n, t).reshape(view) for t in (w[n], m[n], v[n])], "adamw_" + n)
        out_g[n], out_d[n], out_m[n], out_v[n] = (_shard_view(n, t.reshape(shape)) for t in res)
        return res[0]

    behind = token
    for group in ("a", "b", "c"):
        r1 = received(1, group, behind)
        r0 = received(0, group, next(iter(r1.values())))
        for n in r0:
            behind = update(n, [r0[n], r1[n]])
    landed = exchange_wait(handle_small, behind, "grads_wait_small")
    res = adamw([landed[0]], _pack(small_like)[None],
                _pack([m[n] for n in _SMALL] + [no_state])[None], _pack([v[n] for n in _SMALL] + [no_state])[None],
                "adamw_small")
    for store, packed in zip((out_g, out_d, out_m, out_v), res):
        for n, t in zip(_SMALL + ("loss",), _unpack(packed[0], small_like)):
            store[n] = t

    loss = out_g["loss"][0]
    return (loss, grad_x, *[out_g[n] for n in names], *[out_d[n] for n in names], *[out_m[n] for n in names],
            *[out_v[n] for n in names])
```

```python
import jax
import jax.numpy as jnp
from jax import lax
from jax.experimental import pallas as pl
from jax.experimental.pallas import tpu as pltpu

F32 = jnp.float32
BF16 = jnp.bfloat16

SEQ = 2048
D_MODEL = 1024
DEPTH = 2
SG_WIDTH = 512
SG_CHUNK = 128
SG_GROUPS = 4
CV_WIDTH = 512
CV_KERNEL = 31
HEAD_DIM = 64
N_Q_HEADS = 8
N_KV_HEADS = 2
Q_WIDTH = 512
KV_WIDTH = 128
WINDOW = 128
SC_WIDTH = 512
SC_KERNEL = 3
N_BRANCH = 4
D_FF = 2816
EPS = 1e-6
ROPE_THETA = 10000.0
PROJ_A = 4352
PROJ_WIDTH = 8448
N_DEV = 8

ADAM_LR = 0.001
ADAM_B1 = 0.9
ADAM_B2 = 0.999
ADAM_EPS = 1e-08
ADAM_WD = 0.01
ADAM_STEP = 10

LANES = 128
SUBLANES = 8
VMEM_LIMIT_BYTES = 48 * 1024 * 1024
HALO = 32
CONV_ROWS = 256
TOKEN_TILE = 256
NORM_TILE = 512
ROPE_TILE = 1024

_SQRT_HALF = 0.7071067811865476
_INV_SQRT_2PI = 0.3989422804014327


def _params(semantics=None):
    return pltpu.CompilerParams(dimension_semantics=semantics, vmem_limit_bytes=VMEM_LIMIT_BYTES)


def _divisor_tile(n, cap, unit):
    best = None
    for t in range(unit, min(n, cap) + 1, unit):
        if n % t == 0:
            best = t
    return best if best is not None else n


_DIMS = {"nn": (((1,), (0,)), ((), ())), "nt": (((1,), (1,)), ((), ())), "tn": (((0,), (0,)), ((), ()))}


def matmul(a, b, mode, out_dtype, name, add=None, tm_cap=512, tn_cap=512, after=None, b_rows=None, out_rows=None,
           into=None, into_row=0):
    if mode == "nn":
        (m, k), n = a.shape, b.shape[1]
        k = b_rows if b_rows is not None else k
    elif mode == "nt":
        (m, k), n = a.shape, (b_rows if b_rows is not None else b.shape[0])
    else:
        (k, m), n = a.shape, b.shape[1]
    tm = _divisor_tile(m, tm_cap, LANES)
    tn = _divisor_tile(n, tn_cap, LANES)
    row0 = into_row // tm
    assert row0 * tm == into_row
    a_spec = pl.BlockSpec((k, tm), lambda i, j: (0, i)) if mode == "tn" else pl.BlockSpec((tm, k), lambda i, j: (i, 0))
    b_spec = pl.BlockSpec((tn, k), lambda i, j: (j, 0)) if mode == "nt" else pl.BlockSpec((k, tn), lambda i, j: (0, j))
    o_spec = pl.BlockSpec((tm, tn), lambda i, j: (i + row0, j))
    dims = _DIMS[mode]

    def body(*refs):
        a_ref, b_ref = refs[0], refs[1]
        o_ref = refs[-1]
        acc = lax.dot_general(a_ref[...].astype(BF16), b_ref[...].astype(BF16), dims, preferred_element_type=F32)
        if add is not None:
            acc = acc + refs[2][...].astype(F32)
        o_ref[...] = acc.astype(out_dtype)

    unread = tuple(t for t in (after, into) if t is not None)
    operands = (a, b) + (() if add is None else (add,)) + unread
    in_specs = [a_spec, b_spec] + ([o_spec] if add is not None else [])
    in_specs += [pl.BlockSpec(memory_space=pl.ANY)] * len(unread)
    aliases = {len(operands) - 1: 0} if into is not None else {}
    return pl.pallas_call(
        body, name=name,
        out_shape=jax.ShapeDtypeStruct((into.shape[0] if into is not None else out_rows or m, n), out_dtype),
        grid=(m // tm, n // tn),
        in_specs=in_specs, out_specs=o_spec, input_output_aliases=aliases,
        compiler_params=_params(("parallel", "parallel")))(*operands)


def _sigmoid(x):
    return 1.0 / (1.0 + jnp.exp(-x))


def _gelu(x):
    return 0.5 * x * (1.0 + lax.erf(x * _SQRT_HALF))


def _gelu_grad(x):
    return 0.5 * (1.0 + lax.erf(x * _SQRT_HALF)) + x * _INV_SQRT_2PI * jnp.exp(-0.5 * x * x)


def _rms_stats(x):
    r = lax.rsqrt(jnp.mean(x * x, axis=-1, keepdims=True) + EPS)
    return x * r, r


def _rms_bwd(dxn, xhat, r, g):
    h = dxn * g
    return r * (h - xhat * jnp.mean(h * xhat, axis=-1, keepdims=True))


def _ln_stats(x):
    mu = jnp.mean(x, axis=-1, keepdims=True)
    xc = x - mu
    rstd = lax.rsqrt(jnp.mean(xc * xc, axis=-1, keepdims=True) + EPS)
    return xc * rstd, rstd


def _ln_bwd(dy, xhat, rstd, g):
    dxhat = dy * g
    return rstd * (dxhat - jnp.mean(dxhat, axis=-1, keepdims=True)
                   - xhat * jnp.mean(dxhat * xhat, axis=-1, keepdims=True))


def _accumulate(ref, value, first):
    @pl.when(first)
    def _():
        ref[...] = value

    @pl.when(jnp.logical_not(first))
    def _():
        ref[...] += value


def _shift_rows(win, shift, n_out):
    n = win.shape[0]
    if shift % n == 0:
        return win[:n_out]
    return pltpu.roll(win, n - shift, axis=0)[:n_out]


def _row_spec(width):
    return pl.BlockSpec((1, width), lambda i: (0, 0))


def rmsnorm_fwd(x, g, name):
    s, d = x.shape

    def body(x_ref, g_ref, o_ref):
        xhat, _ = _rms_stats(x_ref[...])
        o_ref[...] = (xhat * g_ref[...]).astype(BF16)

    tile = pl.BlockSpec((NORM_TILE, d), lambda i: (i, 0))
    return pl.pallas_call(
        body, name=name, out_shape=jax.ShapeDtypeStruct((s, d), BF16), grid=(s // NORM_TILE,),
        in_specs=[tile, _row_spec(d)], out_specs=tile, compiler_params=_params(("parallel",)))(x, g)


def rmsnorm_bwd(x, g, dxn, dres, name):
    s, d = x.shape

    def body(x_ref, g_ref, dxn_ref, dres_ref, dx_ref, dg_ref):
        xhat, r = _rms_stats(x_ref[...])
        dxn_v = dxn_ref[...].astype(F32)
        dx_ref[...] = dres_ref[...] + _rms_bwd(dxn_v, xhat, r, g_ref[...])
        _accumulate(dg_ref, jnp.sum(dxn_v * xhat, axis=0, keepdims=True), pl.program_id(0) == 0)

    tile = pl.BlockSpec((NORM_TILE, d), lambda i: (i, 0))
    return pl.pallas_call(
        body, name=name, out_shape=(jax.ShapeDtypeStruct((s, d), F32), jax.ShapeDtypeStruct((1, d), F32)),
        grid=(s // NORM_TILE,), in_specs=[tile, _row_spec(d), tile, tile], out_specs=(tile, _row_spec(d)),
        compiler_params=_params(("arbitrary",)))(x, g, dxn, dres)


def loss_head(x, g, target, name):
    s, d = x.shape

    def body(x_ref, g_ref, t_ref, loss_ref, dx_ref, dg_ref):
        first = pl.program_id(0) == 0
        xhat, r = _rms_stats(x_ref[...])
        gv = g_ref[...]
        err = xhat * gv - t_ref[...]
        part = 0.5 * jnp.sum(jnp.sum(err * err, axis=-1, keepdims=True), axis=0, keepdims=True) / d
        _accumulate(loss_ref, jnp.broadcast_to(part, (1, LANES)), first)
        dy = err / d
        dx_ref[...] = _rms_bwd(dy, xhat, r, gv)
        _accumulate(dg_ref, jnp.sum(dy * xhat, axis=0, keepdims=True), first)

    tile = pl.BlockSpec((NORM_TILE, d), lambda i: (i, 0))
    return pl.pallas_call(
        body, name=name,
        out_shape=(jax.ShapeDtypeStruct((1, LANES), F32), jax.ShapeDtypeStruct((s, d), F32),
                   jax.ShapeDtypeStruct((1, d), F32)),
        grid=(s // NORM_TILE,), in_specs=[tile, _row_spec(d), tile],
        out_specs=(_row_spec(LANES), tile, _row_spec(d)), compiler_params=_params(("arbitrary",)))(x, g, target)


def _tril_mask():
    row = lax.broadcasted_iota(jnp.int32, (SG_CHUNK, SG_CHUNK), 0)
    col = lax.broadcasted_iota(jnp.int32, (SG_CHUNK, SG_CHUNK), 1)
    return row >= col


def _sg_specs():
    vec = _row_spec(SG_WIDTH)
    mat = pl.BlockSpec((SG_GROUPS, SG_CHUNK, SG_CHUNK), lambda i: (0, 0, 0))
    return vec, mat


def mixer_a_fwd(proj, ln_g, ln_b, w_s, b_s, name):
    s = proj.shape[0]
    chunks = TOKEN_TILE // SG_CHUNK

    def body(z_ref, lg_ref, lb_ref, w_ref, b_ref, o_ref):
        ge = _gelu(z_ref[...].astype(F32))
        u = ge[:, :SG_WIDTH]
        xhat, _ = _ln_stats(ge[:, SG_WIDTH:])
        vn = xhat * lg_ref[...] + lb_ref[...]
        tril = _tril_mask()
        for ci in range(chunks):
            rows = slice(ci * SG_CHUNK, (ci + 1) * SG_CHUNK)
            for g in range(SG_GROUPS):
                cols = slice(g * LANES, (g + 1) * LANES)
                wm = jnp.where(tril, w_ref[g], 0.0).astype(BF16)
                mixed = jnp.dot(wm, vn[rows, cols].astype(BF16), preferred_element_type=F32) + b_ref[g]
                o_ref[rows, cols] = (u[rows, cols] * mixed).astype(BF16)

    vec, mat = _sg_specs()
    return pl.pallas_call(
        body, name=name, out_shape=jax.ShapeDtypeStruct((s, SG_WIDTH), BF16), grid=(s // TOKEN_TILE,),
        in_specs=[pl.BlockSpec((TOKEN_TILE, 2 * SG_WIDTH), lambda i: (i, 0)), vec, vec, mat, mat],
        out_specs=pl.BlockSpec((TOKEN_TILE, SG_WIDTH), lambda i: (i, 0)),
        compiler_params=_params(("parallel",)))(proj, ln_g, ln_b, w_s, b_s)


def mixer_a_bwd(proj, dy, ln_g, ln_b, w_s, b_s, name):
    s = proj.shape[0]
    chunks = TOKEN_TILE // SG_CHUNK

    def body(z_ref, dy_ref, lg_ref, lb_ref, w_ref, b_ref, dz_ref, dlg_ref, dlb_ref, dw_ref, db_ref, du_scr, dvn_scr):
        first = pl.program_id(0) == 0

        @pl.when(first)
        def _():
            dw_ref[...] = jnp.zeros_like(dw_ref)
            db_ref[...] = jnp.zeros_like(db_ref)

        z = z_ref[...].astype(F32)
        ge = _gelu(z)
        u = ge[:, :SG_WIDTH]
        xhat, rstd = _ln_stats(ge[:, SG_WIDTH:])
        lg = lg_ref[...]
        vn = xhat * lg + lb_ref[...]
        dyv = dy_ref[...].astype(F32)
        tril = _tril_mask()
        for ci in range(chunks):
            rows = slice(ci * SG_CHUNK, (ci + 1) * SG_CHUNK)
            for g in range(SG_GROUPS):
                cols = slice(g * LANES, (g + 1) * LANES)
                wm = jnp.where(tril, w_ref[g], 0.0).astype(BF16)
                vg = vn[rows, cols].astype(BF16)
                mixed = jnp.dot(wm, vg, preferred_element_type=F32) + b_ref[g]
                dyb = dyv[rows, cols]
                du_scr[rows, cols] = dyb * mixed
                dmix = dyb * u[rows, cols]
                db_ref[g] += jnp.broadcast_to(jnp.sum(dmix, axis=1, keepdims=True), (SG_CHUNK, LANES))
                dmb = dmix.astype(BF16)
                dwg = lax.dot_general(dmb, vg, _DIMS["nt"], preferred_element_type=F32)
                dw_ref[g] += jnp.where(tril, dwg, 0.0)
                dvn_scr[rows, cols] = lax.dot_general(wm, dmb, _DIMS["tn"], preferred_element_type=F32)
        dvn = dvn_scr[...]
        _accumulate(dlg_ref, jnp.sum(dvn * xhat, axis=0, keepdims=True), first)
        _accumulate(dlb_ref, jnp.sum(dvn, axis=0, keepdims=True), first)
        dvv = _ln_bwd(dvn, xhat, rstd, lg)
        gg = _gelu_grad(z)
        dz_ref[:, :SG_WIDTH] = (du_scr[...] * gg[:, :SG_WIDTH]).astype(BF16)
        dz_ref[:, SG_WIDTH:] = (dvv * gg[:, SG_WIDTH:]).astype(BF16)

    vec, mat = _sg_specs()
    mat_shape = jax.ShapeDtypeStruct((SG_GROUPS, SG_CHUNK, SG_CHUNK), F32)
    vec_shape = jax.ShapeDtypeStruct((1, SG_WIDTH), F32)
    return pl.pallas_call(
        body, name=name,
        out_shape=(jax.ShapeDtypeStruct((s, 2 * SG_WIDTH), BF16), vec_shape, vec_shape, mat_shape, mat_shape),
        grid=(s // TOKEN_TILE,),
        in_specs=[pl.BlockSpec((TOKEN_TILE, 2 * SG_WIDTH), lambda i: (i, 0)),
                  pl.BlockSpec((TOKEN_TILE, SG_WIDTH), lambda i: (i, 0)), vec, vec, mat, mat],
        out_specs=(pl.BlockSpec((TOKEN_TILE, 2 * SG_WIDTH), lambda i: (i, 0)), vec, vec, mat, mat),
        scratch_shapes=[pltpu.VMEM((TOKEN_TILE, SG_WIDTH), F32), pltpu.VMEM((TOKEN_TILE, SG_WIDTH), F32)],
        compiler_params=_params(("arbitrary",)))(proj, dy, ln_g, ln_b, w_s, b_s)


_B_A_BLOCK = 1024 // LANES
_B_G_BLOCK = 1536 // LANES
_CH_TILES = CV_WIDTH // LANES


def _col_spec(s, first_block):
    return pl.BlockSpec((s, LANES), lambda j: (0, first_block + j))


def conv_b_fwd(proj, w_pad, bias, name):
    s = proj.shape[0]

    def body(a_ref, g_ref, w_ref, b_ref, c_ref, upad):
        upad[0:HALO, :] = jnp.zeros((HALO, LANES), F32)
        upad[HALO:, :] = a_ref[...].astype(F32) * _sigmoid(g_ref[...].astype(F32))
        w = w_ref[...]
        bv = b_ref[...]

        def block(bi, carry):
            start = pl.multiple_of(bi * CONV_ROWS, CONV_ROWS)
            win = upad[pl.ds(start, CONV_ROWS + HALO), :]
            acc = jnp.zeros((CONV_ROWS, LANES), F32)
            for k in range(CV_KERNEL):
                acc = acc + w[k:k + 1, :] * _shift_rows(win, HALO - (CV_KERNEL - 1) + k, CONV_ROWS)
            c_ref[pl.ds(start, CONV_ROWS), :] = acc + bv
            return carry

        lax.fori_loop(0, s // CONV_ROWS, block, 0)

    return pl.pallas_call(
        body, name=name, out_shape=jax.ShapeDtypeStruct((s, CV_WIDTH), F32), grid=(_CH_TILES,),
        in_specs=[_col_spec(s, _B_A_BLOCK), _col_spec(s, _B_G_BLOCK), _col_spec(HALO, 0), _col_spec(1, 0)],
        out_specs=_col_spec(s, 0), scratch_shapes=[pltpu.VMEM((s + HALO, LANES), F32)],
        compiler_params=_params(("parallel",)))(proj, proj, w_pad, bias)


def conv_b_bwd(proj, w_pad, dc, name):
    s = proj.shape[0]

    def body(a_ref, g_ref, w_ref, dc_ref, da_ref, dg_ref, dw_ref, db_ref, upad, dpad, dw_scr):
        upad[0:HALO, :] = jnp.zeros((HALO, LANES), F32)
        upad[HALO:, :] = a_ref[...].astype(F32) * _sigmoid(g_ref[...].astype(F32))
        dcv = dc_ref[...]
        dpad[0:s, :] = dcv
        dpad[s:, :] = jnp.zeros((HALO, LANES), F32)
        db_ref[...] = jnp.sum(dcv, axis=0, keepdims=True)
        dw_scr[...] = jnp.zeros((HALO, LANES), F32)
        w = w_ref[...]

        def block(bi, carry):
            start = pl.multiple_of(bi * CONV_ROWS, CONV_ROWS)
            uwin = upad[pl.ds(start, CONV_ROWS + HALO), :]
            dwin = dpad[pl.ds(start, CONV_ROWS + HALO), :]
            dcb = dwin[:CONV_ROWS]
            du = jnp.zeros((CONV_ROWS, LANES), F32)
            for k in range(CV_KERNEL):
                du = du + w[k:k + 1, :] * _shift_rows(dwin, CV_KERNEL - 1 - k, CONV_ROWS)
                ush = _shift_rows(uwin, HALO - (CV_KERNEL - 1) + k, CONV_ROWS)
                dw_scr[k:k + 1, :] += jnp.sum(dcb * ush, axis=0, keepdims=True)
            av = a_ref[pl.ds(start, CONV_ROWS), :].astype(F32)
            sg = _sigmoid(g_ref[pl.ds(start, CONV_ROWS), :].astype(F32))
            da_ref[pl.ds(start, CONV_ROWS), :] = (du * sg).astype(BF16)
            dg_ref[pl.ds(start, CONV_ROWS), :] = (du * av * sg * (1.0 - sg)).astype(BF16)
            return carry

        lax.fori_loop(0, s // CONV_ROWS, block, 0)
        dw_ref[...] = dw_scr[...]

    act = jax.ShapeDtypeStruct((s, CV_WIDTH), BF16)
    return pl.pallas_call(
        body, name=name,
        out_shape=(act, act, jax.ShapeDtypeStruct((HALO, CV_WIDTH), F32), jax.ShapeDtypeStruct((1, CV_WIDTH), F32)),
        grid=(_CH_TILES,),
        in_specs=[_col_spec(s, _B_A_BLOCK), _col_spec(s, _B_G_BLOCK), _col_spec(HALO, 0), _col_spec(s, 0)],
        out_specs=(_col_spec(s, 0), _col_spec(s, 0), _col_spec(HALO, 0), _col_spec(1, 0)),
        scratch_shapes=[pltpu.VMEM((s + HALO, LANES), F32), pltpu.VMEM((s + HALO, LANES), F32),
                        pltpu.VMEM((HALO, LANES), F32)],
        compiler_params=_params(("parallel",)))(proj, proj, w_pad, dc)


def ln_silu_fwd(c, ln_g, ln_b, name):
    s, d = c.shape

    def body(c_ref, g_ref, b_ref, o_ref):
        xhat, _ = _ln_stats(c_ref[...])
        cn = xhat * g_ref[...] + b_ref[...]
        o_ref[...] = (cn * _sigmoid(cn)).astype(BF16)

    tile = pl.BlockSpec((NORM_TILE, d), lambda i: (i, 0))
    return pl.pallas_call(
        body, name=name, out_shape=jax.ShapeDtypeStruct((s, d), BF16), grid=(s // NORM_TILE,),
        in_specs=[tile, _row_spec(d), _row_spec(d)], out_specs=tile,
        compiler_params=_params(("parallel",)))(c, ln_g, ln_b)


def ln_silu_bwd(c, dy, ln_g, ln_b, name):
    s, d = c.shape

    def body(c_ref, dy_ref, g_ref, b_ref, dc_ref, dg_ref, db_ref):
        first = pl.program_id(0) == 0
        xhat, rstd = _ln_stats(c_ref[...])
        gv = g_ref[...]
        cn = xhat * gv + b_ref[...]
        sg = _sigmoid(cn)
        dcn = dy_ref[...].astype(F32) * sg * (1.0 + cn * (1.0 - sg))
        _accumulate(dg_ref, jnp.sum(dcn * xhat, axis=0, keepdims=True), first)
        _accumulate(db_ref, jnp.sum(dcn, axis=0, keepdims=True), first)
        dc_ref[...] = _ln_bwd(dcn, xhat, rstd, gv)

    tile = pl.BlockSpec((NORM_TILE, d), lambda i: (i, 0))
    vec_shape = jax.ShapeDtypeStruct((1, d), F32)
    return pl.pallas_call(
        body, name=name, out_shape=(jax.ShapeDtypeStruct((s, d), F32), vec_shape, vec_shape),
        grid=(s // NORM_TILE,), in_specs=[tile, tile, _row_spec(d), _row_spec(d)],
        out_specs=(tile, _row_spec(d), _row_spec(d)), compiler_params=_params(("arbitrary",)))(c, dy, ln_g, ln_b)


_D_BLOCK = 2816 // LANES


def _conv3(win, w):
    acc = jnp.zeros((CONV_ROWS, LANES), F32)
    for k in range(SC_KERNEL):
        acc = acc + w[k:k + 1, :] * _shift_rows(win, HALO - (SC_KERNEL - 1) + k, CONV_ROWS)
    return acc


def conv_d_fwd(proj, w_pad, name):
    s = proj.shape[0]

    def body(bg_ref, cg_ref, h_ref, w_ref, o_ref, ppad):
        ppad[0:HALO, :] = jnp.zeros((HALO, LANES), F32)
        ppad[HALO:, :] = cg_ref[...].astype(F32) * h_ref[...].astype(F32)
        w = w_ref[...]

        def block(bi, carry):
            start = pl.multiple_of(bi * CONV_ROWS, CONV_ROWS)
            cv = _conv3(ppad[pl.ds(start, CONV_ROWS + HALO), :], w)
            o_ref[pl.ds(start, CONV_ROWS), :] = (bg_ref[pl.ds(start, CONV_ROWS), :].astype(F32) * cv).astype(BF16)
            return carry

        lax.fori_loop(0, s // CONV_ROWS, block, 0)

    return pl.pallas_call(
        body, name=name, out_shape=jax.ShapeDtypeStruct((s, SC_WIDTH), BF16), grid=(_CH_TILES,),
        in_specs=[_col_spec(s, _D_BLOCK), _col_spec(s, _D_BLOCK + _CH_TILES), _col_spec(s, _D_BLOCK + 2 * _CH_TILES),
                  _col_spec(SUBLANES, 0)],
        out_specs=_col_spec(s, 0), scratch_shapes=[pltpu.VMEM((s + HALO, LANES), F32)],
        compiler_params=_params(("parallel",)))(proj, proj, proj, w_pad)


def conv_d_bwd(proj, w_pad, dy, name):
    s = proj.shape[0]

    def body(bg_ref, cg_ref, h_ref, w_ref, dy_ref, dbg_ref, dcg_ref, dh_ref, dw_ref, ppad, dpad, dw_scr):
        ppad[0:HALO, :] = jnp.zeros((HALO, LANES), F32)
        ppad[HALO:, :] = cg_ref[...].astype(F32) * h_ref[...].astype(F32)
        dpad[0:s, :] = dy_ref[...].astype(F32) * bg_ref[...].astype(F32)
        dpad[s:, :] = jnp.zeros((HALO, LANES), F32)
        dw_scr[...] = jnp.zeros((SUBLANES, LANES), F32)
        w = w_ref[...]

        def block(bi, carry):
            start = pl.multiple_of(bi * CONV_ROWS, CONV_ROWS)
            rows = pl.ds(start, CONV_ROWS)
            pwin = ppad[pl.ds(start, CONV_ROWS + HALO), :]
            dwin = dpad[pl.ds(start, CONV_ROWS + HALO), :]
            dcvb = dwin[:CONV_ROWS]
            dbg_ref[rows, :] = (dy_ref[rows, :].astype(F32) * _conv3(pwin, w)).astype(BF16)
            dp = jnp.zeros((CONV_ROWS, LANES), F32)
            for k in range(SC_KERNEL):
                dp = dp + w[k:k + 1, :] * _shift_rows(dwin, SC_KERNEL - 1 - k, CONV_ROWS)
                psh = _shift_rows(pwin, HALO - (SC_KERNEL - 1) + k, CONV_ROWS)
                dw_scr[k:k + 1, :] += jnp.sum(dcvb * psh, axis=0, keepdims=True)
            dcg_ref[rows, :] = (dp * h_ref[rows, :].astype(F32)).astype(BF16)
            dh_ref[rows, :] = (dp * cg_ref[rows, :].astype(F32)).astype(BF16)
            return carry

        lax.fori_loop(0, s // CONV_ROWS, block, 0)
        dw_ref[...] = dw_scr[...]

    act = jax.ShapeDtypeStruct((s, SC_WIDTH), BF16)
    return pl.pallas_call(
        body, name=name, out_shape=(act, act, act, jax.ShapeDtypeStruct((SUBLANES, SC_WIDTH), F32)),
        grid=(_CH_TILES,),
        in_specs=[_col_spec(s, _D_BLOCK), _col_spec(s, _D_BLOCK + _CH_TILES), _col_spec(s, _D_BLOCK + 2 * _CH_TILES),
                  _col_spec(SUBLANES, 0), _col_spec(s, 0)],
        out_specs=(_col_spec(s, 0), _col_spec(s, 0), _col_spec(s, 0), _col_spec(SUBLANES, 0)),
        scratch_shapes=[pltpu.VMEM((s + HALO, LANES), F32), pltpu.VMEM((s + HALO, LANES), F32),
                        pltpu.VMEM((SUBLANES, LANES), F32)],
        compiler_params=_params(("parallel",)))(proj, proj, proj, w_pad, dy)


_QK_BLOCK = 2048 // LANES
_QK_BLOCKS = (Q_WIDTH + KV_WIDTH) // LANES


def _swap_halves(t):
    lane = lax.broadcasted_iota(jnp.int32, t.shape, 1)
    low = (lane % HEAD_DIM) < (HEAD_DIM // 2)
    return jnp.where(low, pltpu.roll(t, LANES - HEAD_DIM // 2, axis=1), pltpu.roll(t, HEAD_DIM // 2, axis=1))


def rope_fwd(proj, cos_t, sin_t, name, after=None):
    s = proj.shape[0]

    def body(t_ref, c_ref, s_ref, *rest):
        t = t_ref[...].astype(F32)
        rest[-1][...] = (t * c_ref[...] + _swap_halves(t) * s_ref[...]).astype(BF16)

    tr = min(ROPE_TILE, s)
    tab = pl.BlockSpec((tr, LANES), lambda i, j: (i, 0))
    extra = () if after is None else (after,)
    return pl.pallas_call(
        body, name=name, out_shape=jax.ShapeDtypeStruct((s, Q_WIDTH + KV_WIDTH), BF16),
        grid=(s // tr, _QK_BLOCKS),
        in_specs=[pl.BlockSpec((tr, LANES), lambda i, j: (i, _QK_BLOCK + j)), tab, tab]
        + [pl.BlockSpec(memory_space=pl.ANY) for _ in extra],
        out_specs=pl.BlockSpec((tr, LANES), lambda i, j: (i, j)),
        compiler_params=_params(("parallel", "parallel")))(proj, cos_t, sin_t, *extra)


def rope_bwd(d_cur, d_prev, cos_t, sin_t, name):
    s, w = d_cur.shape

    def body(a_ref, b_ref, c_ref, s_ref, o_ref):
        d = a_ref[...] + b_ref[...]
        o_ref[...] = (d * c_ref[...] + _swap_halves(d) * s_ref[...]).astype(BF16)

    tr = min(ROPE_TILE, s)
    tab = pl.BlockSpec((tr, LANES), lambda i, j: (i, 0))
    blk = pl.BlockSpec((tr, LANES), lambda i, j: (i, j))
    return pl.pallas_call(
        body, name=name, out_shape=jax.ShapeDtypeStruct((s, w), BF16), grid=(s // tr, w // LANES),
        in_specs=[blk, blk, tab, tab], out_specs=blk,
        compiler_params=_params(("parallel", "parallel")))(d_cur, d_prev, cos_t, sin_t)


_GROUP = N_Q_HEADS // N_KV_HEADS
_NEG = -1e30


def _attn_specs():
    q_spec = pl.BlockSpec((_GROUP, WINDOW, HEAD_DIM), lambda h, n: (h, n, 0))
    cur = pl.BlockSpec((1, WINDOW, HEAD_DIM), lambda h, n: (h, n, 0))
    prev = pl.BlockSpec((1, WINDOW, HEAD_DIM), lambda h, n: (h, jnp.maximum(n - 1, 0), 0))
    sink = pl.BlockSpec((_GROUP, 1, LANES), lambda h, n: (h, 0, 0))
    return q_spec, cur, prev, sink


def _attn_valid(n):
    qi = lax.broadcasted_iota(jnp.int32, (WINDOW, 2 * WINDOW), 0)
    kj = lax.broadcasted_iota(jnp.int32, (WINDOW, 2 * WINDOW), 1)
    delta = qi + WINDOW - kj
    return (delta >= 0) & (delta < WINDOW) & ((kj >= WINDOW) | (n > 0))


def _attn_probs(q, kcat, valid, sink_row):
    sc = lax.dot_general(q, kcat, _DIMS["nt"], preferred_element_type=F32) * (HEAD_DIM ** -0.5)
    sc = jnp.where(valid, sc, _NEG)
    sink = jnp.max(sink_row, axis=-1, keepdims=True)
    m = jnp.maximum(jnp.max(sc, axis=-1, keepdims=True), sink)
    p = jnp.where(valid, jnp.exp(sc - m), 0.0)
    es = jnp.exp(sink - m)
    inv = 1.0 / (jnp.sum(p, axis=-1, keepdims=True) + es)
    return p * inv, es * inv


def attention_fwd(qh, kh, vh, sinks_b, name):
    s = qh.shape[1]

    def body(q_ref, kc_ref, kp_ref, vc_ref, vp_ref, sk_ref, o_ref):
        valid = _attn_valid(pl.program_id(1))
        kcat = jnp.concatenate([kp_ref[0], kc_ref[0]], axis=0)
        vcat = jnp.concatenate([vp_ref[0], vc_ref[0]], axis=0)
        for g in range(_GROUP):
            probs, _ = _attn_probs(q_ref[g], kcat, valid, sk_ref[g])
            o_ref[g] = jnp.dot(probs.astype(BF16), vcat, preferred_element_type=F32).astype(BF16)

    q_spec, cur, prev, sink = _attn_specs()
    return pl.pallas_call(
        body, name=name, out_shape=jax.ShapeDtypeStruct(qh.shape, BF16), grid=(N_KV_HEADS, s // WINDOW),
        in_specs=[q_spec, cur, prev, cur, prev, sink], out_specs=q_spec,
        compiler_params=_params(("parallel", "parallel")))(qh, kh, kh, vh, vh, sinks_b)


def attention_bwd(qh, kh, vh, sinks_b, doh, name):
    s = qh.shape[1]

    def body(q_ref, kc_ref, kp_ref, vc_ref, vp_ref, sk_ref, do_ref, dq_ref, dkc_ref, dkp_ref, dvc_ref, dvp_ref, ds_ref):
        n = pl.program_id(1)
        valid = _attn_valid(n)
        kcat = jnp.concatenate([kp_ref[0], kc_ref[0]], axis=0)
        vcat = jnp.concatenate([vp_ref[0], vc_ref[0]], axis=0)
        dk = jnp.zeros((2 * WINDOW, HEAD_DIM), F32)
        dv = jnp.zeros((2 * WINDOW, HEAD_DIM), F32)
        for g in range(_GROUP):
            q = q_ref[g]
            do = do_ref[g]
            probs, ps = _attn_probs(q, kcat, valid, sk_ref[g])
            dprobs = lax.dot_general(do, vcat, _DIMS["nt"], preferred_element_type=F32)
            dv = dv + lax.dot_general(probs.astype(BF16), do, _DIMS["tn"], preferred_element_type=F32)
            rs = jnp.sum(probs * dprobs, axis=-1, keepdims=True)
            dsb = (probs * (dprobs - rs) * (HEAD_DIM ** -0.5)).astype(BF16)
            dq_ref[g] = jnp.dot(dsb, kcat, preferred_element_type=F32)
            dk = dk + lax.dot_general(dsb, q, _DIMS["tn"], preferred_element_type=F32)
            dsink = jnp.broadcast_to(-jnp.sum(ps * rs, axis=0, keepdims=True), (1, LANES))

            @pl.when(n == 0)
            def _():
                ds_ref[g] = dsink

            @pl.when(n > 0)
            def _():
                ds_ref[g] += dsink

        dkp_ref[0] = dk[:WINDOW]
        dkc_ref[0] = dk[WINDOW:]
        dvp_ref[0] = dv[:WINDOW]
        dvc_ref[0] = dv[WINDOW:]

    q_spec, cur, prev, sink = _attn_specs()
    kv_shape = jax.ShapeDtypeStruct(kh.shape, F32)
    return pl.pallas_call(
        body, name=name,
        out_shape=(jax.ShapeDtypeStruct(qh.shape, F32), kv_shape, kv_shape, kv_shape, kv_shape,
                   jax.ShapeDtypeStruct(sinks_b.shape, F32)),
        grid=(N_KV_HEADS, s // WINDOW), in_specs=[q_spec, cur, prev, cur, prev, sink, q_spec],
        out_specs=(q_spec, cur, cur, cur, cur, sink),
        compiler_params=_params(("parallel", "arbitrary")))(qh, kh, kh, vh, vh, sinks_b, doh)


def _to_heads(t, heads):
    return t.reshape(t.shape[0], heads, HEAD_DIM).transpose(1, 0, 2)


def _from_heads(t):
    return t.transpose(1, 0, 2).reshape(t.shape[1], t.shape[0] * HEAD_DIM)


def _shift_window(t):
    return jnp.concatenate([t[:, WINDOW:], jnp.zeros_like(t[:, :WINDOW])], axis=1)


def merge_fwd(zg, branches, name):
    s = zg.shape[0]

    def body(zg_ref, b0, b1, b2, b3, o_ref):
        acc = jnp.zeros((TOKEN_TILE, D_MODEL), F32)
        for n, b_ref in enumerate((b0, b1, b2, b3)):
            gate = _sigmoid(zg_ref[:, n * D_MODEL:(n + 1) * D_MODEL].astype(F32))
            acc = acc + gate * b_ref[...].astype(F32)
        o_ref[...] = acc.astype(BF16)

    tile = pl.BlockSpec((TOKEN_TILE, D_MODEL), lambda i: (i, 0))
    wide = pl.BlockSpec((TOKEN_TILE, N_BRANCH * D_MODEL), lambda i: (i, 0))
    return pl.pallas_call(
        body, name=name, out_shape=jax.ShapeDtypeStruct((s, D_MODEL), BF16), grid=(s // TOKEN_TILE,),
        in_specs=[wide, tile, tile, tile, tile], out_specs=tile,
        compiler_params=_params(("parallel",)))(zg, *branches)


def merge_bwd(zg, branches, dm, name):
    s = zg.shape[0]

    def body(zg_ref, b0, b1, b2, b3, dm_ref, dzg_ref, d0, d1, d2, d3):
        dmv = dm_ref[...].astype(F32)
        for n, (b_ref, d_ref) in enumerate(((b0, d0), (b1, d1), (b2, d2), (b3, d3))):
            cols = slice(n * D_MODEL, (n + 1) * D_MODEL)
            gate = _sigmoid(zg_ref[:, cols].astype(F32))
            d_ref[...] = (gate * dmv).astype(BF16)
            dzg_ref[:, cols] = (dmv * b_ref[...].astype(F32) * gate * (1.0 - gate)).astype(BF16)

    tile = pl.BlockSpec((TOKEN_TILE, D_MODEL), lambda i: (i, 0))
    wide = pl.BlockSpec((TOKEN_TILE, N_BRANCH * D_MODEL), lambda i: (i, 0))
    act = jax.ShapeDtypeStruct((s, D_MODEL), BF16)
    return pl.pallas_call(
        body, name=name, out_shape=(jax.ShapeDtypeStruct((s, N_BRANCH * D_MODEL), BF16), act, act, act, act),
        grid=(s // TOKEN_TILE,), in_specs=[wide, tile, tile, tile, tile, tile],
        out_specs=(wide, tile, tile, tile, tile), compiler_params=_params(("parallel",)))(zg, *branches, dm)


def swiglu_fwd(gu, name):
    s = gu.shape[0]

    def body(g_ref, u_ref, o_ref):
        gate = g_ref[...].astype(F32)
        o_ref[...] = (gate * _sigmoid(gate) * u_ref[...].astype(F32)).astype(BF16)

    return pl.pallas_call(
        body, name=name, out_shape=jax.ShapeDtypeStruct((s, D_FF), BF16), grid=(s // TOKEN_TILE,),
        in_specs=[pl.BlockSpec((TOKEN_TILE, D_FF), lambda i: (i, 0)), pl.BlockSpec((TOKEN_TILE, D_FF), lambda i: (i, 1))],
        out_specs=pl.BlockSpec((TOKEN_TILE, D_FF), lambda i: (i, 0)), compiler_params=_params(("parallel",)))(gu, gu)


def swiglu_bwd(gu, dact, name):
    s = gu.shape[0]

    def body(g_ref, u_ref, da_ref, o_ref):
        gate = g_ref[...].astype(F32)
        sg = _sigmoid(gate)
        da = da_ref[...].astype(F32)
        o_ref[:, :D_FF] = (da * u_ref[...].astype(F32) * sg * (1.0 + gate * (1.0 - sg))).astype(BF16)
        o_ref[:, D_FF:] = (da * gate * sg).astype(BF16)

    half = pl.BlockSpec((TOKEN_TILE, D_FF), lambda i: (i, 0))
    return pl.pallas_call(
        body, name=name, out_shape=jax.ShapeDtypeStruct((s, 2 * D_FF), BF16), grid=(s // TOKEN_TILE,),
        in_specs=[half, pl.BlockSpec((TOKEN_TILE, D_FF), lambda i: (i, 1)), half],
        out_specs=pl.BlockSpec((TOKEN_TILE, 2 * D_FF), lambda i: (i, 0)),
        compiler_params=_params(("parallel",)))(gu, gu, dact)


ADAMW_BLOCK_BYTES = 1 << 20


def adamw(parts, w, m, v, name):
    n_parts, r, c = w.shape
    tr = _divisor_tile(r, max(SUBLANES, ADAMW_BLOCK_BYTES // (4 * c)), SUBLANES)
    tiles = r // tr

    def part_spec(j):
        return pl.BlockSpec((N_DEV, tr, c), lambda i: (0, jnp.clip(i - j * tiles, 0, tiles - 1), 0))

    def body(*refs):
        p_refs = refs[:n_parts]
        w_ref, m_ref, v_ref, g_ref, d_ref, nm_ref, nv_ref = refs[n_parts:]
        which = pl.program_id(0) // tiles
        g = None
        for j, p_ref in enumerate(p_refs):
            gj = p_ref[0].astype(F32)
            for i in range(1, N_DEV):
                gj = gj + p_ref[i].astype(F32)
            g = gj if g is None else jnp.where(which == j, gj, g)
        nm = ADAM_B1 * m_ref[0] + (1.0 - ADAM_B1) * g
        nv = ADAM_B2 * v_ref[0] + (1.0 - ADAM_B2) * (g * g)
        m_hat = nm / (1.0 - ADAM_B1 ** ADAM_STEP)
        v_hat = nv / (1.0 - ADAM_B2 ** ADAM_STEP)
        g_ref[0] = g
        d_ref[0] = -ADAM_LR * (m_hat / (jnp.sqrt(v_hat) + ADAM_EPS) + ADAM_WD * w_ref[0])
        nm_ref[0] = nm
        nv_ref[0] = nv

    tile = pl.BlockSpec((1, tr, c), lambda i: (i // tiles, i % tiles, 0))
    shape = jax.ShapeDtypeStruct(w.shape, F32)
    return pl.pallas_call(
        body, name=name, out_shape=(shape, shape, shape, shape), grid=(n_parts * tiles,),
        in_specs=[part_spec(j) for j in range(n_parts)] + [tile, tile, tile],
        out_specs=(tile, tile, tile, tile), compiler_params=_params(("parallel",)))(*parts, w, m, v)


_RELATIONS = [(a, b, e) for a in (0, 1) for b in (0, 1) for e in (0, 1)][1:]


_HBM_SPEC = pl.BlockSpec(memory_space=pltpu.HBM)
_SEM_SPEC = pl.BlockSpec(memory_space=pltpu.SEMAPHORE)
_ANY_SPEC = pl.BlockSpec(memory_space=pl.ANY)
_DATAFLOW = pltpu.SideEffectType.DATAFLOW_SIDE_EFFECTING


_OTHER_CHIPS = [(1, 0), (0, 1), (1, 1)]
_FIRST_LEVEL = [(0, 0, 1)] + [(a, b, 0) for a, b in _OTHER_CHIPS]


def _remote_copies(ins, lands, send_sems, recv_sems, scatter, relations):
    x, y, c = lax.axis_index("x"), lax.axis_index("y"), lax.axis_index("c")
    me = 4 * x + 2 * y + c
    copies = []
    for t in range(len(ins)):
        for k, (a, b, e) in enumerate(relations):
            px, py, pc = (x + a) % 2, (y + b) % 2, (c + e) % 2
            src = ins[t].at[4 * px + 2 * py + pc] if scatter[t] else ins[t]
            copies.append(pltpu.make_async_remote_copy(
                src_ref=src, dst_ref=lands[t].at[me], send_sem=send_sems.at[t * len(relations) + k],
                recv_sem=recv_sems.at[t * len(relations) + k],
                device_id=(px, py, pc), device_id_type=pl.DeviceIdType.MESH))
    return copies


def _forward_copies(lands, send_sems, recv_sems):
    x, y, c = lax.axis_index("x"), lax.axis_index("y"), lax.axis_index("c")
    copies = []
    for t in range(len(lands)):
        for k, (a, b) in enumerate(_OTHER_CHIPS):
            slot = lands[t].at[4 * ((x + a) % 2) + 2 * ((y + b) % 2) + c]
            copies.append(pltpu.make_async_remote_copy(
                src_ref=slot, dst_ref=slot, send_sem=send_sems.at[t * len(_OTHER_CHIPS) + k],
                recv_sem=recv_sems.at[t * len(_OTHER_CHIPS) + k],
                device_id=(x, y, 1 - c), device_id_type=pl.DeviceIdType.MESH))
    return copies


def gather_zones(shards, name):
    n = len(shards)

    def body(*refs):
        me = 4 * lax.axis_index("x") + 2 * lax.axis_index("y") + lax.axis_index("c")
        copies = [pltpu.make_async_copy(refs[t], refs[n + t].at[me], refs[2 * n].at[t]) for t in range(n)]
        for cp in copies:
            cp.start()
        for cp in copies:
            cp.wait()

    return pl.pallas_call(
        body, name=name, out_shape=tuple(jax.ShapeDtypeStruct((N_DEV,) + a.shape, a.dtype) for a in shards),
        in_specs=[_ANY_SPEC] * n, out_specs=(_ANY_SPEC,) * n, scratch_shapes=[pltpu.SemaphoreType.DMA((n,))])(*shards)


def _place_own(landed, own, me):
    return lax.dynamic_update_index_in_dim(landed, own, me, 0)


def exchange_start(arrays, scatter, name, after=None, relations=_RELATIONS, zones=None):
    n = len(arrays)
    n_rel = len(relations)
    land_shapes = [a.shape if scatter[t] else (N_DEV,) + a.shape for t, a in enumerate(arrays)]
    if zones is None:
        zones = [lax.empty(s, a.dtype) for s, a in zip(land_shapes, arrays)]

    def body(*refs):
        ins, lands = refs[:n], refs[n:2 * n]
        send_sems, recv_sems = refs[-2 * n - 3], refs[-2 * n - 2]
        token = refs[-1]
        for cp in _remote_copies(ins, lands, send_sems, recv_sems, scatter, relations):
            cp.start()
        token[...] = jnp.zeros_like(token)

    sems = pltpu.SemaphoreType.DMA((n * n_rel,))
    out_shape = ((sems, sems) + tuple(pltpu.HBM(a.shape, a.dtype) for a in arrays)
                 + tuple(pltpu.HBM(s, a.dtype) for s, a in zip(land_shapes, arrays))
                 + (jax.ShapeDtypeStruct((SUBLANES, LANES), F32),))
    operands = [pltpu.with_memory_space_constraint(a, pltpu.HBM) for a in arrays]
    operands += [pltpu.with_memory_space_constraint(z, pltpu.HBM) for z in zones]
    in_specs = [_HBM_SPEC] * (2 * n)
    if after is not None:
        operands.append(after)
        in_specs.append(_ANY_SPEC)
    res = pl.pallas_call(
        body, name=name, out_shape=out_shape, in_specs=in_specs,
        out_specs=(_SEM_SPEC, _SEM_SPEC) + (_HBM_SPEC,) * (2 * n) + (pl.BlockSpec(memory_space=pltpu.VMEM),),
        input_output_aliases={i: 2 + i for i in range(2 * n)},
        compiler_params=pltpu.CompilerParams(has_side_effects=_DATAFLOW))(*operands)
    handle = (res[0], res[1], res[2:2 + n], res[2 + n:2 + 2 * n], tuple(scatter), relations)
    return handle, res[-1]


def exchange_wait(handle, after, name):
    send_sems, recv_sems, sources, lands, scatter, relations = handle
    n = len(sources)

    def body(*refs):
        ins, lzs = refs[:n], refs[n:2 * n]
        send_ref, recv_ref = refs[2 * n], refs[2 * n + 1]
        for cp in _remote_copies(ins, lzs, send_ref, recv_ref, scatter, relations):
            cp.wait_send()
            cp.wait_recv()

    out_shape = (tuple(pltpu.HBM(a.shape, a.dtype) for a in sources) + tuple(pltpu.HBM(a.shape, a.dtype) for a in lands))
    res = pl.pallas_call(
        body, name=name, out_shape=out_shape, in_specs=[_HBM_SPEC] * (2 * n) + [_SEM_SPEC, _SEM_SPEC, _ANY_SPEC],
        out_specs=(_HBM_SPEC,) * (2 * n), input_output_aliases={i: i for i in range(2 * n)},
        compiler_params=pltpu.CompilerParams(has_side_effects=_DATAFLOW))(*sources, *lands, send_sems, recv_sems, after)
    return res[n:], res[:n]


def forward_start(lands, name):
    n = len(lands)

    def body(*refs):
        send_sems, recv_sems, token = refs[n], refs[n + 1], refs[-1]
        for cp in _forward_copies(refs[:n], send_sems, recv_sems):
            cp.start()
        token[...] = jnp.zeros_like(token)

    sems = pltpu.SemaphoreType.DMA((n * len(_OTHER_CHIPS),))
    res = pl.pallas_call(
        body, name=name,
        out_shape=(sems, sems) + tuple(pltpu.HBM(a.shape, a.dtype) for a in lands)
        + (jax.ShapeDtypeStruct((SUBLANES, LANES), F32),),
        in_specs=[_HBM_SPEC] * n,
        out_specs=(_SEM_SPEC, _SEM_SPEC) + (_HBM_SPEC,) * n + (pl.BlockSpec(memory_space=pltpu.VMEM),),
        input_output_aliases={i: 2 + i for i in range(n)},
        compiler_params=pltpu.CompilerParams(has_side_effects=_DATAFLOW))(*lands)
    return (res[0], res[1], res[2:2 + n]), res[-1]


def forward_wait(handle, after, name):
    send_sems, recv_sems, lands = handle
    n = len(lands)

    def body(*refs):
        for cp in _forward_copies(refs[:n], refs[n], refs[n + 1]):
            cp.wait_send()
            cp.wait_recv()

    return pl.pallas_call(
        body, name=name, out_shape=tuple(pltpu.HBM(a.shape, a.dtype) for a in lands),
        in_specs=[_HBM_SPEC] * n + [_SEM_SPEC, _SEM_SPEC, _ANY_SPEC], out_specs=(_HBM_SPEC,) * n,
        input_output_aliases={i: i for i in range(n)},
        compiler_params=pltpu.CompilerParams(has_side_effects=_DATAFLOW))(*lands, send_sems, recv_sems, after)


_SMALL = ("norm_mix", "sg_ln_g", "sg_ln_b", "sg_b", "cv_b", "cv_ln_g", "cv_ln_b", "attn_sinks", "norm_ffn",
          "norm_final")
_PACK_UNIT = SUBLANES * LANES


def _pack(tensors):
    rows = []
    for t in tensors:
        flat = t.reshape(-1)
        pad = (-flat.shape[0]) % _PACK_UNIT
        rows.append(jnp.pad(flat, (0, pad)).reshape(-1, LANES))
    return jnp.concatenate(rows, axis=0)


def _unpack(packed, like):
    out, row = [], 0
    for t in like:
        size = 1
        for d in t.shape:
            size *= d
        rows = -(-size // _PACK_UNIT) * SUBLANES
        out.append(packed[row:row + rows].reshape(-1)[:size].reshape(t.shape))
        row += rows
    return out


def _layer_fwd(l, x, p, late_params, mid_hook=None, ffn_hook=None):
    tag = f"l{l}_"
    xn = rmsnorm_fwd(x, p["norm_mix"], tag + "norm_mix")
    proj = matmul(xn, p["w_in_t"], "nt", BF16, tag + "proj_a", tm_cap=1024, tn_cap=2176, b_rows=PROJ_A)
    zg = matmul(xn, p["w_in_t_g"], "nt", BF16, tag + "proj_g", tm_cap=1024, tn_cap=2048)
    y_a = mixer_a_fwd(proj, p["sg_ln_g"], p["sg_ln_b"], p["sg_w"], p["sg_b"], tag + "mix_a")
    conv = conv_b_fwd(proj, p["cv_w"], p["cv_b"], tag + "conv_b")
    y_b = ln_silu_fwd(conv, p["cv_ln_g"], p["cv_ln_b"], tag + "ln_silu")
    token = mid_hook(y_b) if mid_hook is not None else None
    qk = rope_fwd(proj, p["cos"], p["sin"], tag + "rope", after=token)
    qh = _to_heads(qk[:, :Q_WIDTH], N_Q_HEADS)
    kh = _to_heads(qk[:, Q_WIDTH:], N_KV_HEADS)
    vh = _to_heads(proj[:, 2688:2816], N_KV_HEADS)
    oh = attention_fwd(qh, kh, vh, p["sinks"], tag + "attn")
    y_c = _from_heads(oh)
    y_d = conv_d_fwd(proj, p["sc_w"], tag + "conv_d")
    ys = (y_a, y_b, y_c, y_d)
    p = {**p, **late_params(y_d)}
    branches = tuple(matmul(ys[n], p["w_branch"][n], "nn", BF16, tag + f"branch{n}", tm_cap=1024, tn_cap=1024)
                     for n in range(N_BRANCH))
    merged = merge_fwd(zg, branches, tag + "merge")
    x_mid = matmul(merged, p["w_out"], "nn", F32, tag + "out", add=x, tm_cap=1024, tn_cap=1024)
    token = ffn_hook(x_mid) if ffn_hook is not None else None
    hn = rmsnorm_fwd(x_mid, p["norm_ffn"], tag + "norm_ffn")
    gu = matmul(hn, p["w_gate_up_t"], "nt", BF16, tag + "gate_up", tm_cap=512, tn_cap=2816, after=token)
    act = swiglu_fwd(gu, tag + "swiglu")
    x_out = matmul(act, p["w_down"], "nn", F32, tag + "down", add=x_mid, tm_cap=512, tn_cap=1024)
    saved = dict(x=x, xn=xn, proj=proj, zg=zg, conv=conv, qh=qh, kh=kh, vh=vh, ys=ys, branches=branches,
                 merged=merged, x_mid=x_mid, hn=hn, gu=gu, act=act)
    return x_out, saved, p


def _layer_bwd(l, dx_out, p, sv, emit, after=None):
    tag = f"l{l}_b_"
    g = {}
    dact = matmul(dx_out, p["w_down"], "nt", BF16, tag + "dact", after=after, tm_cap=512, tn_cap=2816)
    dw_down = matmul(sv["act"], dx_out, "tn", BF16, tag + "dw_down", tm_cap=1408, tn_cap=512)
    dgu = swiglu_bwd(sv["gu"], dact, tag + "swiglu")
    dhn = matmul(dgu, p["w_gate_up_t"], "nn", BF16, tag + "dhn", tm_cap=512, tn_cap=512)
    dw_gate_up = matmul(dgu, sv["hn"], "tn", BF16, tag + "dw_gate_up", tm_cap=1408, tn_cap=1024)
    token = emit("a", {"w_gate_up": dw_gate_up, "w_down": dw_down})
    dx_mid, g["norm_ffn"] = rmsnorm_bwd(sv["x_mid"], p["norm_ffn"], dhn, dx_out, tag + "norm_ffn")
    dmerged = matmul(dx_mid, p["w_out"], "nt", BF16, tag + "dmerged", after=token, tm_cap=1024, tn_cap=1024)
    dw_out = matmul(sv["merged"], dx_mid, "tn", BF16, tag + "dw_out", tm_cap=1024, tn_cap=512)
    dzg, *dbranches = merge_bwd(sv["zg"], sv["branches"], dmerged, tag + "merge")
    dys = [matmul(dbranches[n], p["w_branch"][n], "nt", BF16, tag + f"dy{n}", tm_cap=1024, tn_cap=512)
           for n in range(N_BRANCH)]
    dw_branch = jnp.stack(
        [matmul(sv["ys"][n], dbranches[n], "tn", BF16, tag + f"dw_branch{n}", tm_cap=512, tn_cap=1024)
         for n in range(N_BRANCH)])
    token = emit("b", {"w_branch": dw_branch, "w_out": dw_out})
    proj = sv["proj"]
    dz_a, g["sg_ln_g"], g["sg_ln_b"], g["sg_w"], dsb = mixer_a_bwd(
        proj, dys[0], p["sg_ln_g"], p["sg_ln_b"], p["sg_w"], p["sg_b"], tag + "mix_a")
    g["sg_b"] = dsb[:, :, 0]
    dconv, g["cv_ln_g"], g["cv_ln_b"] = ln_silu_bwd(sv["conv"], dys[1], p["cv_ln_g"], p["cv_ln_b"], tag + "ln_silu")
    da, dgate, dcw, g["cv_b"] = conv_b_bwd(proj, p["cv_w"], dconv, tag + "conv_b")
    g["cv_w"] = dcw[:CV_KERNEL]
    doh = _to_heads(dys[2], N_Q_HEADS)
    dqh, dkc, dkp, dvc, dvp, dsk = attention_bwd(sv["qh"], sv["kh"], sv["vh"], p["sinks"], doh, tag + "attn")
    g["attn_sinks"] = dsk[:, 0, 0]
    dqk_cur = jnp.concatenate([_from_heads(dqh), _from_heads(dkc)], axis=1)
    dqk_prev = jnp.concatenate([jnp.zeros((SEQ, Q_WIDTH), F32), _from_heads(_shift_window(dkp))], axis=1)
    dqk = rope_bwd(dqk_cur, dqk_prev, p["cos"], -p["sin"], tag + "rope")
    dv = (_from_heads(dvc) + _from_heads(_shift_window(dvp))).astype(BF16)
    dbg, dcg, dh, dsw = conv_d_bwd(proj, p["sc_w"], dys[3], tag + "conv_d")
    g["sc_w"] = dsw[:SC_KERNEL]
    dproj = jnp.concatenate([dz_a, da, dgate, dqk, dv, dbg, dcg, dh], axis=1)
    dw_in = matmul(dproj, sv["xn"], "tn", BF16, tag + "dw_in_a", after=token, tm_cap=2176, tn_cap=512,
                   out_rows=PROJ_WIDTH)
    g["w_in"] = matmul(dzg, sv["xn"], "tn", BF16, tag + "dw_in_g", tm_cap=256, tn_cap=1024, into=dw_in,
                       into_row=PROJ_A)
    token = emit("c", {n: g.pop(n) for n in _EARLY}, {"sg_w": g.pop("sg_w")})
    dxn = matmul(dproj, p["w_in_t"], "nn", F32, tag + "dxn_a", after=token, tm_cap=512, tn_cap=512, b_rows=PROJ_A)
    dxn = matmul(dzg, p["w_in_t_g"], "nn", F32, tag + "dxn_g", add=dxn, tm_cap=512, tn_cap=512)
    dx_in, g["norm_mix"] = rmsnorm_bwd(sv["x"], p["norm_mix"], dxn, dx_mid, tag + "norm_mix")
    return dx_in, g, token


_EARLY = ("w_in", "cv_w", "sc_w")
_LATE = ("w_branch", "w_out", "w_gate_up", "w_down")


_TRANSPOSED = ("w_in", "w_gate_up")


def _shard_view(name, t):
    return jnp.swapaxes(t, 1, 2) if name in _TRANSPOSED else t


def _full_weight(name, t):
    if name in ("w_out", "w_down") + _TRANSPOSED:
        return t.reshape(-1, t.shape[-1])
    if name == "w_branch":
        return t.transpose(1, 2, 0, 3).reshape(N_BRANCH, SG_WIDTH, D_MODEL)
    return t.transpose(1, 0, 2).reshape(t.shape[1], -1)


def _to_blocks(name, full):
    if name in ("w_out", "w_down") + _TRANSPOSED:
        return full.reshape(N_DEV, -1, full.shape[-1])
    if name == "w_branch":
        return full.reshape(N_BRANCH, SG_WIDTH, N_DEV, -1).transpose(2, 0, 1, 3)
    return full.reshape(full.shape[0], N_DEV, -1).transpose(1, 0, 2)


def _rope_tables():
    pos = jnp.arange(SEQ, dtype=F32)
    inv_freq = 1.0 / (ROPE_THETA ** (jnp.arange(0, HEAD_DIM, 2, dtype=F32) / HEAD_DIM))
    ang = pos[:, None] * inv_freq[None, :]
    cos, sin = jnp.cos(ang), jnp.sin(ang)
    reps = LANES // HEAD_DIM
    return jnp.tile(jnp.concatenate([cos, cos], axis=1), (1, reps)), jnp.tile(jnp.concatenate([-sin, sin], axis=1), (1, reps))


def kernel(x, norm_mix, w_in, sg_ln_g, sg_ln_b, sg_w, sg_b, cv_w, cv_b, cv_ln_g, cv_ln_b, attn_sinks, sc_w, w_branch, w_out, norm_ffn, w_gate_up, w_down, norm_final, loss_target, m_norm_mix, m_w_in, m_sg_ln_g, m_sg_ln_b, m_sg_w, m_sg_b, m_cv_w, m_cv_b, m_cv_ln_g, m_cv_ln_b, m_attn_sinks, m_sc_w, m_w_branch, m_w_out, m_norm_ffn, m_w_gate_up, m_w_down, m_norm_final, v_norm_mix, v_w_in, v_sg_ln_g, v_sg_ln_b, v_sg_w, v_sg_b, v_cv_w, v_cv_b, v_cv_ln_g, v_cv_ln_b, v_attn_sinks, v_sc_w, v_w_branch, v_w_out, v_norm_ffn, v_w_gate_up, v_w_down, v_norm_final):
    names = ("norm_mix", "w_in", "sg_ln_g", "sg_ln_b", "sg_w", "sg_b", "cv_w", "cv_b", "cv_ln_g", "cv_ln_b",
             "attn_sinks", "sc_w", "w_branch", "w_out", "norm_ffn", "w_gate_up", "w_down", "norm_final")
    w = dict(zip(names, (norm_mix, w_in, sg_ln_g, sg_ln_b, sg_w, sg_b, cv_w, cv_b, cv_ln_g, cv_ln_b, attn_sinks,
                         sc_w, w_branch, w_out, norm_ffn, w_gate_up, w_down, norm_final)))
    m = dict(zip(names, (m_norm_mix, m_w_in, m_sg_ln_g, m_sg_ln_b, m_sg_w, m_sg_b, m_cv_w, m_cv_b, m_cv_ln_g,
                         m_cv_ln_b, m_attn_sinks, m_sc_w, m_w_branch, m_w_out, m_norm_ffn, m_w_gate_up, m_w_down,
                         m_norm_final)))
    v = dict(zip(names, (v_norm_mix, v_w_in, v_sg_ln_g, v_sg_ln_b, v_sg_w, v_sg_b, v_cv_w, v_cv_b, v_cv_ln_g,
                         v_cv_ln_b, v_attn_sinks, v_sc_w, v_w_branch, v_w_out, v_norm_ffn, v_w_gate_up, v_w_down,
                         v_norm_final)))

    me = 4 * lax.axis_index("x") + 2 * lax.axis_index("y") + lax.axis_index("c")
    groups = [(l, group) for l in range(DEPTH) for group in (_EARLY, _LATE)]
    shards = {(l, group): [_shard_view(n, w[n])[l].astype(BF16) for n in group] for l, group in groups}
    zones = list(gather_zones([s for key in groups for s in shards[key]], "gather_zones"))
    gathers, forwards, token = {}, {}, None
    for l, group in groups:
        gathers[(l, group)], token = exchange_start(
            shards[(l, group)], [False] * len(group), f"gather_start{l}_{group[0]}", after=token,
            relations=_FIRST_LEVEL, zones=[zones.pop(0) for _ in group])

    def begin_forward(l, group, after):
        landed, _ = exchange_wait(gathers[(l, group)], after, f"gather_wait{l}_{group[0]}")
        forwards[(l, group)], tok = forward_start(landed, f"forward_start{l}_{group[0]}")
        return tok

    def landed_weights(l, group, after):
        landed = forward_wait(forwards[(l, group)], after, f"forward_wait{l}_{group[0]}")
        return {n + "_t" if n in _TRANSPOSED else n: _full_weight(n, t) for n, t in zip(group, landed)}

    cos_t, sin_t = _rope_tables()

    def early_params(l, after):
        full = landed_weights(l, _EARLY, after)
        return dict(
            norm_mix=w["norm_mix"][l][None], norm_ffn=w["norm_ffn"][l][None],
            w_in_t=full["w_in_t"], w_in_t_g=full["w_in_t"][PROJ_A:],
            sg_ln_g=w["sg_ln_g"][l][None], sg_ln_b=w["sg_ln_b"][l][None], sg_w=w["sg_w"][l],
            sg_b=jnp.broadcast_to(w["sg_b"][l][:, :, None], (SG_GROUPS, SG_CHUNK, LANES)),
            cv_w=jnp.pad(full["cv_w"].astype(F32), ((0, HALO - CV_KERNEL), (0, 0))),
            cv_b=w["cv_b"][l][None], cv_ln_g=w["cv_ln_g"][l][None], cv_ln_b=w["cv_ln_b"][l][None],
            sinks=jnp.broadcast_to(w["attn_sinks"][l][:, None, None], (N_Q_HEADS, 1, LANES)),
            sc_w=jnp.pad(full["sc_w"].astype(F32), ((0, SUBLANES - SC_KERNEL), (0, 0))),
            cos=cos_t, sin=sin_t)

    params, saved = [None] * DEPTH, [None] * DEPTH
    h = x[0]
    after = begin_forward(0, _EARLY, token)
    for l in range(DEPTH):
        h, saved[l], params[l] = _layer_fwd(
            l, h, early_params(l, after), lambda behind, l=l: landed_weights(l, _LATE, behind),
            mid_hook=lambda behind, l=l: begin_forward(l, _LATE, behind),
            ffn_hook=(lambda behind, l=l: begin_forward(l + 1, _EARLY, behind)) if l + 1 < DEPTH else None)
        after = h
    loss_row, dh, d_norm_final = loss_head(h, w["norm_final"][None], loss_target[0], "loss_head")

    sent = {}

    def emitter(l):
        def emit(group, grads_of, replicated=None):
            replicated = replicated or {}
            send = [_to_blocks(n, grads_of[n].astype(BF16)) for n in grads_of] + list(replicated.values())
            flags = [True] * len(grads_of) + [False] * len(replicated)
            handle, tok = exchange_start(send, flags, f"grads_start{l}{group}")
            sent[(l, group)] = (handle, tuple(grads_of) + tuple(replicated), flags)
            return tok
        return emit

    grads = [None] * DEPTH
    token = None
    for l in reversed(range(DEPTH)):
        dh, grads[l], token = _layer_bwd(l, dh, params[l], saved[l], emitter(l), after=token)
    grad_x = dh[None]

    stacked = {n: jnp.stack([grads[l][n] for l in range(DEPTH)]) for n in _SMALL if n != "norm_final"}
    for n in ("norm_mix", "norm_ffn", "sg_ln_g", "sg_ln_b", "cv_b", "cv_ln_g", "cv_ln_b"):
        stacked[n] = stacked[n][:, 0]
    stacked["norm_final"] = d_norm_final[0]
    no_state = jnp.zeros((1,), F32)
    small_like = [w[n] for n in _SMALL] + [no_state]
    small_part = _pack([stacked[n] for n in _SMALL] + [loss_row[0, :1]])
    handle_small, token = exchange_start([small_part], [False], "grads_start_small", after=token)

    def received(l, group, after):
        handle, group_names, flags = sent[(l, group)]
        landed, sources = exchange_wait(handle, after, f"grads_wait{l}{group}")
        return {n: _place_own(t, lax.dynamic_index_in_dim(s, me, 0, keepdims=False) if scattered else s, me)
                for n, t, s, scattered in zip(group_names, landed, sources, flags)}

    out_g, out_d, out_m, out_v = {}, {}, {}, {}

    def update(n, by_layer):
        shape = _shard_view(n, w[n]).shape
        view = (DEPTH, w[n].size // (DEPTH * shape[-1]), shape[-1])
        parts = [t.reshape((N_DEV,) + view[1:]) for t in by_layer]
        res = adamw(parts, *[_shard_view(n, t).reshape(view) for t in (w[n], m[n], v[n])], "adamw_" + n)
        out_g[n], out_d[n], out_m[n], out_v[n] = (_shard_view(n, t.reshape(shape)) for t in res)
        return res[0]

    behind = token
    for group in ("a", "b", "c"):
        r1 = received(1, group, behind)
        r0 = received(0, group, next(iter(r1.values())))
        for n in r0:
            behind = update(n, [r0[n], r1[n]])
    landed, sources = exchange_wait(handle_small, behind, "grads_wait_small")
    res = adamw([_place_own(landed[0], sources[0], me)], _pack(small_like)[None],
                _pack([m[n] for n in _SMALL] + [no_state])[None], _pack([v[n] for n in _SMALL] + [no_state])[None],
                "adamw_small")
    for store, packed in zip((out_g, out_d, out_m, out_v), res):
        for n, t in zip(_SMALL + ("loss",), _unpack(packed[0], small_like)):
            store[n] = t

    loss = out_g["loss"][0]
    return (loss, grad_x, *[out_g[n] for n in names], *[out_d[n] for n in names], *[out_m[n] for n in names],
            *[out_v[n] for n in names])
```

```python
import jax
import jax.numpy as jnp
from jax import lax
from jax.experimental import pallas as pl
from jax.experimental.pallas import tpu as pltpu

F32 = jnp.float32
BF16 = jnp.bfloat16

SEQ = 2048
D_MODEL = 1024
DEPTH = 2
SG_WIDTH = 512
SG_CHUNK = 128
SG_GROUPS = 4
CV_WIDTH = 512
CV_KERNEL = 31
HEAD_DIM = 64
N_Q_HEADS = 8
N_KV_HEADS = 2
Q_WIDTH = 512
KV_WIDTH = 128
WINDOW = 128
SC_WIDTH = 512
SC_KERNEL = 3
N_BRANCH = 4
D_FF = 2816
EPS = 1e-6
ROPE_THETA = 10000.0
PROJ_A = 4352
PROJ_WIDTH = 8448
N_DEV = 8

ADAM_LR = 0.001
ADAM_B1 = 0.9
ADAM_B2 = 0.999
ADAM_EPS = 1e-08
ADAM_WD = 0.01
ADAM_STEP = 10

LANES = 128
SUBLANES = 8
VMEM_LIMIT_BYTES = 48 * 1024 * 1024
HALO = 32
CONV_ROWS = 256
TOKEN_TILE = 256
NORM_TILE = 512
ROPE_TILE = 1024

_SQRT_HALF = 0.7071067811865476
_INV_SQRT_2PI = 0.3989422804014327


def _params(semantics=None):
    return pltpu.CompilerParams(dimension_semantics=semantics, vmem_limit_bytes=VMEM_LIMIT_BYTES)


def _divisor_tile(n, cap, unit):
    best = None
    for t in range(unit, min(n, cap) + 1, unit):
        if n % t == 0:
            best = t
    return best if best is not None else n


_DIMS = {"nn": (((1,), (0,)), ((), ())), "nt": (((1,), (1,)), ((), ())), "tn": (((0,), (0,)), ((), ()))}


def matmul(a, b, mode, out_dtype, name, add=None, tm_cap=512, tn_cap=512, after=None, b_rows=None, out_rows=None,
           into=None, into_row=0):
    if mode == "nn":
        (m, k), n = a.shape, b.shape[1]
        k = b_rows if b_rows is not None else k
    elif mode == "nt":
        (m, k), n = a.shape, (b_rows if b_rows is not None else b.shape[0])
    else:
        (k, m), n = a.shape, b.shape[1]
    tm = _divisor_tile(m, tm_cap, LANES)
    tn = _divisor_tile(n, tn_cap, LANES)
    row0 = into_row // tm
    assert row0 * tm == into_row
    a_spec = pl.BlockSpec((k, tm), lambda i, j: (0, i)) if mode == "tn" else pl.BlockSpec((tm, k), lambda i, j: (i, 0))
    b_spec = pl.BlockSpec((tn, k), lambda i, j: (j, 0)) if mode == "nt" else pl.BlockSpec((k, tn), lambda i, j: (0, j))
    o_spec = pl.BlockSpec((tm, tn), lambda i, j: (i + row0, j))
    dims = _DIMS[mode]

    def body(*refs):
        a_ref, b_ref = refs[0], refs[1]
        o_ref = refs[-1]
        acc = lax.dot_general(a_ref[...].astype(BF16), b_ref[...].astype(BF16), dims, preferred_element_type=F32)
        if add is not None:
            acc = acc + refs[2][...].astype(F32)
        o_ref[...] = acc.astype(out_dtype)

    unread = tuple(t for t in (after, into) if t is not None)
    operands = (a, b) + (() if add is None else (add,)) + unread
    in_specs = [a_spec, b_spec] + ([o_spec] if add is not None else [])
    in_specs += [pl.BlockSpec(memory_space=pl.ANY)] * len(unread)
    aliases = {len(operands) - 1: 0} if into is not None else {}
    return pl.pallas_call(
        body, name=name,
        out_shape=jax.ShapeDtypeStruct((into.shape[0] if into is not None else out_rows or m, n), out_dtype),
        grid=(m // tm, n // tn),
        in_specs=in_specs, out_specs=o_spec, input_output_aliases=aliases,
        compiler_params=_params(("parallel", "parallel")))(*operands)


def _sigmoid(x):
    return 1.0 / (1.0 + jnp.exp(-x))


def _gelu(x):
    return 0.5 * x * (1.0 + lax.erf(x * _SQRT_HALF))


def _gelu_grad(x):
    return 0.5 * (1.0 + lax.erf(x * _SQRT_HALF)) + x * _INV_SQRT_2PI * jnp.exp(-0.5 * x * x)


def _rms_stats(x):
    r = lax.rsqrt(jnp.mean(x * x, axis=-1, keepdims=True) + EPS)
    return x * r, r


def _rms_bwd(dxn, xhat, r, g):
    h = dxn * g
    return r * (h - xhat * jnp.mean(h * xhat, axis=-1, keepdims=True))


def _ln_stats(x):
    mu = jnp.mean(x, axis=-1, keepdims=True)
    xc = x - mu
    rstd = lax.rsqrt(jnp.mean(xc * xc, axis=-1, keepdims=True) + EPS)
    return xc * rstd, rstd


def _ln_bwd(dy, xhat, rstd, g):
    dxhat = dy * g
    return rstd * (dxhat - jnp.mean(dxhat, axis=-1, keepdims=True)
                   - xhat * jnp.mean(dxhat * xhat, axis=-1, keepdims=True))


def _accumulate(ref, value, first):
    @pl.when(first)
    def _():
        ref[...] = value

    @pl.when(jnp.logical_not(first))
    def _():
        ref[...] += value


def _shift_rows(win, shift, n_out):
    n = win.shape[0]
    if shift % n == 0:
        return win[:n_out]
    return pltpu.roll(win, n - shift, axis=0)[:n_out]


def _row_spec(width):
    return pl.BlockSpec((1, width), lambda i: (0, 0))


def rmsnorm_fwd(x, g, name):
    s, d = x.shape

    def body(x_ref, g_ref, o_ref):
        xhat, _ = _rms_stats(x_ref[...])
        o_ref[...] = (xhat * g_ref[...]).astype(BF16)

    tile = pl.BlockSpec((NORM_TILE, d), lambda i: (i, 0))
    return pl.pallas_call(
        body, name=name, out_shape=jax.ShapeDtypeStruct((s, d), BF16), grid=(s // NORM_TILE,),
        in_specs=[tile, _row_spec(d)], out_specs=tile, compiler_params=_params(("parallel",)))(x, g)


def rmsnorm_bwd(x, g, dxn, dres, name):
    s, d = x.shape

    def body(x_ref, g_ref, dxn_ref, dres_ref, dx_ref, dg_ref):
        xhat, r = _rms_stats(x_ref[...])
        dxn_v = dxn_ref[...].astype(F32)
        dx_ref[...] = dres_ref[...] + _rms_bwd(dxn_v, xhat, r, g_ref[...])
        _accumulate(dg_ref, jnp.sum(dxn_v * xhat, axis=0, keepdims=True), pl.program_id(0) == 0)

    tile = pl.BlockSpec((NORM_TILE, d), lambda i: (i, 0))
    return pl.pallas_call(
        body, name=name, out_shape=(jax.ShapeDtypeStruct((s, d), F32), jax.ShapeDtypeStruct((1, d), F32)),
        grid=(s // NORM_TILE,), in_specs=[tile, _row_spec(d), tile, tile], out_specs=(tile, _row_spec(d)),
        compiler_params=_params(("arbitrary",)))(x, g, dxn, dres)


def loss_head(x, g, target, name):
    s, d = x.shape

    def body(x_ref, g_ref, t_ref, loss_ref, dx_ref, dg_ref):
        first = pl.program_id(0) == 0
        xhat, r = _rms_stats(x_ref[...])
        gv = g_ref[...]
        err = xhat * gv - t_ref[...]
        part = 0.5 * jnp.sum(jnp.sum(err * err, axis=-1, keepdims=True), axis=0, keepdims=True) / d
        _accumulate(loss_ref, jnp.broadcast_to(part, (1, LANES)), first)
        dy = err / d
        dx_ref[...] = _rms_bwd(dy, xhat, r, gv)
        _accumulate(dg_ref, jnp.sum(dy * xhat, axis=0, keepdims=True), first)

    tile = pl.BlockSpec((NORM_TILE, d), lambda i: (i, 0))
    return pl.pallas_call(
        body, name=name,
        out_shape=(jax.ShapeDtypeStruct((1, LANES), F32), jax.ShapeDtypeStruct((s, d), F32),
                   jax.ShapeDtypeStruct((1, d), F32)),
        grid=(s // NORM_TILE,), in_specs=[tile, _row_spec(d), tile],
        out_specs=(_row_spec(LANES), tile, _row_spec(d)), compiler_params=_params(("arbitrary",)))(x, g, target)


def _tril_mask():
    row = lax.broadcasted_iota(jnp.int32, (SG_CHUNK, SG_CHUNK), 0)
    col = lax.broadcasted_iota(jnp.int32, (SG_CHUNK, SG_CHUNK), 1)
    return row >= col


def _sg_specs():
    vec = _row_spec(SG_WIDTH)
    mat = pl.BlockSpec((SG_GROUPS, SG_CHUNK, SG_CHUNK), lambda i: (0, 0, 0))
    return vec, mat


def mixer_a_fwd(proj, ln_g, ln_b, w_s, b_s, name):
    s = proj.shape[0]
    chunks = TOKEN_TILE // SG_CHUNK

    def body(z_ref, lg_ref, lb_ref, w_ref, b_ref, o_ref):
        ge = _gelu(z_ref[...].astype(F32))
        u = ge[:, :SG_WIDTH]
        xhat, _ = _ln_stats(ge[:, SG_WIDTH:])
        vn = xhat * lg_ref[...] + lb_ref[...]
        tril = _tril_mask()
        for ci in range(chunks):
            rows = slice(ci * SG_CHUNK, (ci + 1) * SG_CHUNK)
            for g in range(SG_GROUPS):
                cols = slice(g * LANES, (g + 1) * LANES)
                wm = jnp.where(tril, w_ref[g], 0.0).astype(BF16)
                mixed = jnp.dot(wm, vn[rows, cols].astype(BF16), preferred_element_type=F32) + b_ref[g]
                o_ref[rows, cols] = (u[rows, cols] * mixed).astype(BF16)

    vec, mat = _sg_specs()
    return pl.pallas_call(
        body, name=name, out_shape=jax.ShapeDtypeStruct((s, SG_WIDTH), BF16), grid=(s // TOKEN_TILE,),
        in_specs=[pl.BlockSpec((TOKEN_TILE, 2 * SG_WIDTH), lambda i: (i, 0)), vec, vec, mat, mat],
        out_specs=pl.BlockSpec((TOKEN_TILE, SG_WIDTH), lambda i: (i, 0)),
        compiler_params=_params(("parallel",)))(proj, ln_g, ln_b, w_s, b_s)


def mixer_a_bwd(proj, dy, ln_g, ln_b, w_s, b_s, name):
    s = proj.shape[0]
    chunks = TOKEN_TILE // SG_CHUNK

    def body(z_ref, dy_ref, lg_ref, lb_ref, w_ref, b_ref, dz_ref, dlg_ref, dlb_ref, dw_ref, db_ref, du_scr, dvn_scr):
        first = pl.program_id(0) == 0

        @pl.when(first)
        def _():
            dw_ref[...] = jnp.zeros_like(dw_ref)
            db_ref[...] = jnp.zeros_like(db_ref)

        z = z_ref[...].astype(F32)
        ge = _gelu(z)
        u = ge[:, :SG_WIDTH]
        xhat, rstd = _ln_stats(ge[:, SG_WIDTH:])
        lg = lg_ref[...]
        vn = xhat * lg + lb_ref[...]
        dyv = dy_ref[...].astype(F32)
        tril = _tril_mask()
        for ci in range(chunks):
            rows = slice(ci * SG_CHUNK, (ci + 1) * SG_CHUNK)
            for g in range(SG_GROUPS):
                cols = slice(g * LANES, (g + 1) * LANES)
                wm = jnp.where(tril, w_ref[g], 0.0).astype(BF16)
                vg = vn[rows, cols].astype(BF16)
                mixed = jnp.dot(wm, vg, preferred_element_type=F32) + b_ref[g]
                dyb = dyv[rows, cols]
                du_scr[rows, cols] = dyb * mixed
                dmix = dyb * u[rows, cols]
                db_ref[g] += jnp.broadcast_to(jnp.sum(dmix, axis=1, keepdims=True), (SG_CHUNK, LANES))
                dmb = dmix.astype(BF16)
                dwg = lax.dot_general(dmb, vg, _DIMS["nt"], preferred_element_type=F32)
                dw_ref[g] += jnp.where(tril, dwg, 0.0)
                dvn_scr[rows, cols] = lax.dot_general(wm, dmb, _DIMS["tn"], preferred_element_type=F32)
        dvn = dvn_scr[...]
        _accumulate(dlg_ref, jnp.sum(dvn * xhat, axis=0, keepdims=True), first)
        _accumulate(dlb_ref, jnp.sum(dvn, axis=0, keepdims=True), first)
        dvv = _ln_bwd(dvn, xhat, rstd, lg)
        gg = _gelu_grad(z)
        dz_ref[:, :SG_WIDTH] = (du_scr[...] * gg[:, :SG_WIDTH]).astype(BF16)
        dz_ref[:, SG_WIDTH:] = (dvv * gg[:, SG_WIDTH:]).astype(BF16)

    vec, mat = _sg_specs()
    mat_shape = jax.ShapeDtypeStruct((SG_GROUPS, SG_CHUNK, SG_CHUNK), F32)
    vec_shape = jax.ShapeDtypeStruct((1, SG_WIDTH), F32)
    return pl.pallas_call(
        body, name=name,
        out_shape=(jax.ShapeDtypeStruct((s, 2 * SG_WIDTH), BF16), vec_shape, vec_shape, mat_shape, mat_shape),
        grid=(s // TOKEN_TILE,),
        in_specs=[pl.BlockSpec((TOKEN_TILE, 2 * SG_WIDTH), lambda i: (i, 0)),
                  pl.BlockSpec((TOKEN_TILE, SG_WIDTH), lambda i: (i, 0)), vec, vec, mat, mat],
        out_specs=(pl.BlockSpec((TOKEN_TILE, 2 * SG_WIDTH), lambda i: (i, 0)), vec, vec, mat, mat),
        scratch_shapes=[pltpu.VMEM((TOKEN_TILE, SG_WIDTH), F32), pltpu.VMEM((TOKEN_TILE, SG_WIDTH), F32)],
        compiler_params=_params(("arbitrary",)))(proj, dy, ln_g, ln_b, w_s, b_s)


_B_A_BLOCK = 1024 // LANES
_B_G_BLOCK = 1536 // LANES
_CH_TILES = CV_WIDTH // LANES


def _col_spec(s, first_block):
    return pl.BlockSpec((s, LANES), lambda j: (0, first_block + j))


def conv_b_fwd(proj, w_pad, bias, name):
    s = proj.shape[0]

    def body(a_ref, g_ref, w_ref, b_ref, c_ref, upad):
        upad[0:HALO, :] = jnp.zeros((HALO, LANES), F32)
        upad[HALO:, :] = a_ref[...].astype(F32) * _sigmoid(g_ref[...].astype(F32))
        w = w_ref[...]
        bv = b_ref[...]

        def block(bi, carry):
            start = pl.multiple_of(bi * CONV_ROWS, CONV_ROWS)
            win = upad[pl.ds(start, CONV_ROWS + HALO), :]
            acc = jnp.zeros((CONV_ROWS, LANES), F32)
            for k in range(CV_KERNEL):
                acc = acc + w[k:k + 1, :] * _shift_rows(win, HALO - (CV_KERNEL - 1) + k, CONV_ROWS)
            c_ref[pl.ds(start, CONV_ROWS), :] = acc + bv
            return carry

        lax.fori_loop(0, s // CONV_ROWS, block, 0)

    return pl.pallas_call(
        body, name=name, out_shape=jax.ShapeDtypeStruct((s, CV_WIDTH), F32), grid=(_CH_TILES,),
        in_specs=[_col_spec(s, _B_A_BLOCK), _col_spec(s, _B_G_BLOCK), _col_spec(HALO, 0), _col_spec(1, 0)],
        out_specs=_col_spec(s, 0), scratch_shapes=[pltpu.VMEM((s + HALO, LANES), F32)],
        compiler_params=_params(("parallel",)))(proj, proj, w_pad, bias)


def conv_b_bwd(proj, w_pad, dc, name):
    s = proj.shape[0]

    def body(a_ref, g_ref, w_ref, dc_ref, da_ref, dg_ref, dw_ref, db_ref, upad, dpad, dw_scr):
        upad[0:HALO, :] = jnp.zeros((HALO, LANES), F32)
        upad[HALO:, :] = a_ref[...].astype(F32) * _sigmoid(g_ref[...].astype(F32))
        dcv = dc_ref[...]
        dpad[0:s, :] = dcv
        dpad[s:, :] = jnp.zeros((HALO, LANES), F32)
        db_ref[...] = jnp.sum(dcv, axis=0, keepdims=True)
        dw_scr[...] = jnp.zeros((HALO, LANES), F32)
        w = w_ref[...]

        def block(bi, carry):
            start = pl.multiple_of(bi * CONV_ROWS, CONV_ROWS)
            uwin = upad[pl.ds(start, CONV_ROWS + HALO), :]
            dwin = dpad[pl.ds(start, CONV_ROWS + HALO), :]
            dcb = dwin[:CONV_ROWS]
            du = jnp.zeros((CONV_ROWS, LANES), F32)
            for k in range(CV_KERNEL):
                du = du + w[k:k + 1, :] * _shift_rows(dwin, CV_KERNEL - 1 - k, CONV_ROWS)
                ush = _shift_rows(uwin, HALO - (CV_KERNEL - 1) + k, CONV_ROWS)
                dw_scr[k:k + 1, :] += jnp.sum(dcb * ush, axis=0, keepdims=True)
            av = a_ref[pl.ds(start, CONV_ROWS), :].astype(F32)
            sg = _sigmoid(g_ref[pl.ds(start, CONV_ROWS), :].astype(F32))
            da_ref[pl.ds(start, CONV_ROWS), :] = (du * sg).astype(BF16)
            dg_ref[pl.ds(start, CONV_ROWS), :] = (du * av * sg * (1.0 - sg)).astype(BF16)
            return carry

        lax.fori_loop(0, s // CONV_ROWS, block, 0)
        dw_ref[...] = dw_scr[...]

    act = jax.ShapeDtypeStruct((s, CV_WIDTH), BF16)
    return pl.pallas_call(
        body, name=name,
        out_shape=(act, act, jax.ShapeDtypeStruct((HALO, CV_WIDTH), F32), jax.ShapeDtypeStruct((1, CV_WIDTH), F32)),
        grid=(_CH_TILES,),
        in_specs=[_col_spec(s, _B_A_BLOCK), _col_spec(s, _B_G_BLOCK), _col_spec(HALO, 0), _col_spec(s, 0)],
        out_specs=(_col_spec(s, 0), _col_spec(s, 0), _col_spec(HALO, 0), _col_spec(1, 0)),
        scratch_shapes=[pltpu.VMEM((s + HALO, LANES), F32), pltpu.VMEM((s + HALO, LANES), F32),
                        pltpu.VMEM((HALO, LANES), F32)],
        compiler_params=_params(("parallel",)))(proj, proj, w_pad, dc)


def ln_silu_fwd(c, ln_g, ln_b, name):
    s, d = c.shape

    def body(c_ref, g_ref, b_ref, o_ref):
        xhat, _ = _ln_stats(c_ref[...])
        cn = xhat * g_ref[...] + b_ref[...]
        o_ref[...] = (cn * _sigmoid(cn)).astype(BF16)

    tile = pl.BlockSpec((NORM_TILE, d), lambda i: (i, 0))
    return pl.pallas_call(
        body, name=name, out_shape=jax.ShapeDtypeStruct((s, d), BF16), grid=(s // NORM_TILE,),
        in_specs=[tile, _row_spec(d), _row_spec(d)], out_specs=tile,
        compiler_params=_params(("parallel",)))(c, ln_g, ln_b)


def ln_silu_bwd(c, dy, ln_g, ln_b, name):
    s, d = c.shape

    def body(c_ref, dy_ref, g_ref, b_ref, dc_ref, dg_ref, db_ref):
        first = pl.program_id(0) == 0
        xhat, rstd = _ln_stats(c_ref[...])
        gv = g_ref[...]
        cn = xhat * gv + b_ref[...]
        sg = _sigmoid(cn)
        dcn = dy_ref[...].astype(F32) * sg * (1.0 + cn * (1.0 - sg))
        _accumulate(dg_ref, jnp.sum(dcn * xhat, axis=0, keepdims=True), first)
        _accumulate(db_ref, jnp.sum(dcn, axis=0, keepdims=True), first)
        dc_ref[...] = _ln_bwd(dcn, xhat, rstd, gv)

    tile = pl.BlockSpec((NORM_TILE, d), lambda i: (i, 0))
    vec_shape = jax.ShapeDtypeStruct((1, d), F32)
    return pl.pallas_call(
        body, name=name, out_shape=(jax.ShapeDtypeStruct((s, d), F32), vec_shape, vec_shape),
        grid=(s // NORM_TILE,), in_specs=[tile, tile, _row_spec(d), _row_spec(d)],
        out_specs=(tile, _row_spec(d), _row_spec(d)), compiler_params=_params(("arbitrary",)))(c, dy, ln_g, ln_b)


_D_BLOCK = 2816 // LANES


def _conv3(win, w):
    acc = jnp.zeros((CONV_ROWS, LANES), F32)
    for k in range(SC_KERNEL):
        acc = acc + w[k:k + 1, :] * _shift_rows(win, HALO - (SC_KERNEL - 1) + k, CONV_ROWS)
    return acc


def conv_d_fwd(proj, w_pad, name):
    s = proj.shape[0]

    def body(bg_ref, cg_ref, h_ref, w_ref, o_ref, ppad):
        ppad[0:HALO, :] = jnp.zeros((HALO, LANES), F32)
        ppad[HALO:, :] = cg_ref[...].astype(F32) * h_ref[...].astype(F32)
        w = w_ref[...]

        def block(bi, carry):
            start = pl.multiple_of(bi * CONV_ROWS, CONV_ROWS)
            cv = _conv3(ppad[pl.ds(start, CONV_ROWS + HALO), :], w)
            o_ref[pl.ds(start, CONV_ROWS), :] = (bg_ref[pl.ds(start, CONV_ROWS), :].astype(F32) * cv).astype(BF16)
            return carry

        lax.fori_loop(0, s // CONV_ROWS, block, 0)

    return pl.pallas_call(
        body, name=name, out_shape=jax.ShapeDtypeStruct((s, SC_WIDTH), BF16), grid=(_CH_TILES,),
        in_specs=[_col_spec(s, _D_BLOCK), _col_spec(s, _D_BLOCK + _CH_TILES), _col_spec(s, _D_BLOCK + 2 * _CH_TILES),
                  _col_spec(SUBLANES, 0)],
        out_specs=_col_spec(s, 0), scratch_shapes=[pltpu.VMEM((s + HALO, LANES), F32)],
        compiler_params=_params(("parallel",)))(proj, proj, proj, w_pad)


def conv_d_bwd(proj, w_pad, dy, name):
    s = proj.shape[0]

    def body(bg_ref, cg_ref, h_ref, w_ref, dy_ref, dbg_ref, dcg_ref, dh_ref, dw_ref, ppad, dpad, dw_scr):
        ppad[0:HALO, :] = jnp.zeros((HALO, LANES), F32)
        ppad[HALO:, :] = cg_ref[...].astype(F32) * h_ref[...].astype(F32)
        dpad[0:s, :] = dy_ref[...].astype(F32) * bg_ref[...].astype(F32)
        dpad[s:, :] = jnp.zeros((HALO, LANES), F32)
        dw_scr[...] = jnp.zeros((SUBLANES, LANES), F32)
        w = w_ref[...]

        def block(bi, carry):
            start = pl.multiple_of(bi * CONV_ROWS, CONV_ROWS)
            rows = pl.ds(start, CONV_ROWS)
            pwin = ppad[pl.ds(start, CONV_ROWS + HALO), :]
            dwin = dpad[pl.ds(start, CONV_ROWS + HALO), :]
            dcvb = dwin[:CONV_ROWS]
            dbg_ref[rows, :] = (dy_ref[rows, :].astype(F32) * _conv3(pwin, w)).astype(BF16)
            dp = jnp.zeros((CONV_ROWS, LANES), F32)
            for k in range(SC_KERNEL):
                dp = dp + w[k:k + 1, :] * _shift_rows(dwin, SC_KERNEL - 1 - k, CONV_ROWS)
                psh = _shift_rows(pwin, HALO - (SC_KERNEL - 1) + k, CONV_ROWS)
                dw_scr[k:k + 1, :] += jnp.sum(dcvb * psh, axis=0, keepdims=True)
            dcg_ref[rows, :] = (dp * h_ref[rows, :].astype(F32)).astype(BF16)
            dh_ref[rows, :] = (dp * cg_ref[rows, :].astype(F32)).astype(BF16)
            return carry

        lax.fori_loop(0, s // CONV_ROWS, block, 0)
        dw_ref[...] = dw_scr[...]

    act = jax.ShapeDtypeStruct((s, SC_WIDTH), BF16)
    return pl.pallas_call(
        body, name=name, out_shape=(act, act, act, jax.ShapeDtypeStruct((SUBLANES, SC_WIDTH), F32)),
        grid=(_CH_TILES,),
        in_specs=[_col_spec(s, _D_BLOCK), _col_spec(s, _D_BLOCK + _CH_TILES), _col_spec(s, _D_BLOCK + 2 * _CH_TILES),
                  _col_spec(SUBLANES, 0), _col_spec(s, 0)],
        out_specs=(_col_spec(s, 0), _col_spec(s, 0), _col_spec(s, 0), _col_spec(SUBLANES, 0)),
        scratch_shapes=[pltpu.VMEM((s + HALO, LANES), F32), pltpu.VMEM((s + HALO, LANES), F32),
                        pltpu.VMEM((SUBLANES, LANES), F32)],
        compiler_params=_params(("parallel",)))(proj, proj, proj, w_pad, dy)


_QK_BLOCK = 2048 // LANES
_QK_BLOCKS = (Q_WIDTH + KV_WIDTH) // LANES


def _swap_halves(t):
    lane = lax.broadcasted_iota(jnp.int32, t.shape, 1)
    low = (lane % HEAD_DIM) < (HEAD_DIM // 2)
    return jnp.where(low, pltpu.roll(t, LANES - HEAD_DIM // 2, axis=1), pltpu.roll(t, HEAD_DIM // 2, axis=1))


def rope_fwd(proj, cos_t, sin_t, name, after=None):
    s = proj.shape[0]

    def body(t_ref, c_ref, s_ref, *rest):
        t = t_ref[...].astype(F32)
        rest[-1][...] = (t * c_ref[...] + _swap_halves(t) * s_ref[...]).astype(BF16)

    tr = min(ROPE_TILE, s)
    tab = pl.BlockSpec((tr, LANES), lambda i, j: (i, 0))
    extra = () if after is None else (after,)
    return pl.pallas_call(
        body, name=name, out_shape=jax.ShapeDtypeStruct((s, Q_WIDTH + KV_WIDTH), BF16),
        grid=(s // tr, _QK_BLOCKS),
        in_specs=[pl.BlockSpec((tr, LANES), lambda i, j: (i, _QK_BLOCK + j)), tab, tab]
        + [pl.BlockSpec(memory_space=pl.ANY) for _ in extra],
        out_specs=pl.BlockSpec((tr, LANES), lambda i, j: (i, j)),
        compiler_params=_params(("parallel", "parallel")))(proj, cos_t, sin_t, *extra)


def rope_bwd(d_cur, d_prev, cos_t, sin_t, name):
    s, w = d_cur.shape

    def body(a_ref, b_ref, c_ref, s_ref, o_ref):
        d = a_ref[...] + b_ref[...]
        o_ref[...] = (d * c_ref[...] + _swap_halves(d) * s_ref[...]).astype(BF16)

    tr = min(ROPE_TILE, s)
    tab = pl.BlockSpec((tr, LANES), lambda i, j: (i, 0))
    blk = pl.BlockSpec((tr, LANES), lambda i, j: (i, j))
    return pl.pallas_call(
        body, name=name, out_shape=jax.ShapeDtypeStruct((s, w), BF16), grid=(s // tr, w // LANES),
        in_specs=[blk, blk, tab, tab], out_specs=blk,
        compiler_params=_params(("parallel", "parallel")))(d_cur, d_prev, cos_t, sin_t)


_GROUP = N_Q_HEADS // N_KV_HEADS
_NEG = -1e30


def _attn_specs():
    q_spec = pl.BlockSpec((_GROUP, WINDOW, HEAD_DIM), lambda h, n: (h, n, 0))
    cur = pl.BlockSpec((1, WINDOW, HEAD_DIM), lambda h, n: (h, n, 0))
    prev = pl.BlockSpec((1, WINDOW, HEAD_DIM), lambda h, n: (h, jnp.maximum(n - 1, 0), 0))
    sink = pl.BlockSpec((_GROUP, 1, LANES), lambda h, n: (h, 0, 0))
    return q_spec, cur, prev, sink


def _attn_valid(n):
    qi = lax.broadcasted_iota(jnp.int32, (WINDOW, 2 * WINDOW), 0)
    kj = lax.broadcasted_iota(jnp.int32, (WINDOW, 2 * WINDOW), 1)
    delta = qi + WINDOW - kj
    return (delta >= 0) & (delta < WINDOW) & ((kj >= WINDOW) | (n > 0))


def _attn_probs(q, kcat, valid, sink_row):
    sc = lax.dot_general(q, kcat, _DIMS["nt"], preferred_element_type=F32) * (HEAD_DIM ** -0.5)
    sc = jnp.where(valid, sc, _NEG)
    sink = jnp.max(sink_row, axis=-1, keepdims=True)
    m = jnp.maximum(jnp.max(sc, axis=-1, keepdims=True), sink)
    p = jnp.where(valid, jnp.exp(sc - m), 0.0)
    es = jnp.exp(sink - m)
    inv = 1.0 / (jnp.sum(p, axis=-1, keepdims=True) + es)
    return p * inv, es * inv


def attention_fwd(qh, kh, vh, sinks_b, name, after=None):
    s = qh.shape[1]
    unread = () if after is None else (after,)

    def body(q_ref, kc_ref, kp_ref, vc_ref, vp_ref, sk_ref, *rest):
        o_ref = rest[-1]
        valid = _attn_valid(pl.program_id(1))
        kcat = jnp.concatenate([kp_ref[0], kc_ref[0]], axis=0)
        vcat = jnp.concatenate([vp_ref[0], vc_ref[0]], axis=0)
        for g in range(_GROUP):
            probs, _ = _attn_probs(q_ref[g], kcat, valid, sk_ref[g])
            o_ref[g] = jnp.dot(probs.astype(BF16), vcat, preferred_element_type=F32).astype(BF16)

    q_spec, cur, prev, sink = _attn_specs()
    return pl.pallas_call(
        body, name=name, out_shape=jax.ShapeDtypeStruct(qh.shape, BF16), grid=(N_KV_HEADS, s // WINDOW),
        in_specs=[q_spec, cur, prev, cur, prev, sink] + [pl.BlockSpec(memory_space=pl.ANY)] * len(unread),
        out_specs=q_spec, compiler_params=_params(("parallel", "parallel")))(qh, kh, kh, vh, vh, sinks_b, *unread)


def attention_bwd(qh, kh, vh, sinks_b, doh, name):
    s = qh.shape[1]

    def body(q_ref, kc_ref, kp_ref, vc_ref, vp_ref, sk_ref, do_ref, dq_ref, dkc_ref, dkp_ref, dvc_ref, dvp_ref, ds_ref):
        n = pl.program_id(1)
        valid = _attn_valid(n)
        kcat = jnp.concatenate([kp_ref[0], kc_ref[0]], axis=0)
        vcat = jnp.concatenate([vp_ref[0], vc_ref[0]], axis=0)
        dk = jnp.zeros((2 * WINDOW, HEAD_DIM), F32)
        dv = jnp.zeros((2 * WINDOW, HEAD_DIM), F32)
        for g in range(_GROUP):
            q = q_ref[g]
            do = do_ref[g]
            probs, ps = _attn_probs(q, kcat, valid, sk_ref[g])
            dprobs = lax.dot_general(do, vcat, _DIMS["nt"], preferred_element_type=F32)
            dv = dv + lax.dot_general(probs.astype(BF16), do, _DIMS["tn"], preferred_element_type=F32)
            rs = jnp.sum(probs * dprobs, axis=-1, keepdims=True)
            dsb = (probs * (dprobs - rs) * (HEAD_DIM ** -0.5)).astype(BF16)
            dq_ref[g] = jnp.dot(dsb, kcat, preferred_element_type=F32)
            dk = dk + lax.dot_general(dsb, q, _DIMS["tn"], preferred_element_type=F32)
            dsink = jnp.broadcast_to(-jnp.sum(ps * rs, axis=0, keepdims=True), (1, LANES))

            @pl.when(n == 0)
            def _():
                ds_ref[g] = dsink

            @pl.when(n > 0)
            def _():
                ds_ref[g] += dsink

        dkp_ref[0] = dk[:WINDOW]
        dkc_ref[0] = dk[WINDOW:]
        dvp_ref[0] = dv[:WINDOW]
        dvc_ref[0] = dv[WINDOW:]

    q_spec, cur, prev, sink = _attn_specs()
    kv_shape = jax.ShapeDtypeStruct(kh.shape, F32)
    return pl.pallas_call(
        body, name=name,
        out_shape=(jax.ShapeDtypeStruct(qh.shape, F32), kv_shape, kv_shape, kv_shape, kv_shape,
                   jax.ShapeDtypeStruct(sinks_b.shape, F32)),
        grid=(N_KV_HEADS, s // WINDOW), in_specs=[q_spec, cur, prev, cur, prev, sink, q_spec],
        out_specs=(q_spec, cur, cur, cur, cur, sink),
        compiler_params=_params(("parallel", "arbitrary")))(qh, kh, kh, vh, vh, sinks_b, doh)


def _to_heads(t, heads):
    return t.reshape(t.shape[0], heads, HEAD_DIM).transpose(1, 0, 2)


def _from_heads(t):
    return t.transpose(1, 0, 2).reshape(t.shape[1], t.shape[0] * HEAD_DIM)


def _shift_window(t):
    return jnp.concatenate([t[:, WINDOW:], jnp.zeros_like(t[:, :WINDOW])], axis=1)


def merge_fwd(zg, branches, name):
    s = zg.shape[0]

    def body(zg_ref, b0, b1, b2, b3, o_ref):
        acc = jnp.zeros((TOKEN_TILE, D_MODEL), F32)
        for n, b_ref in enumerate((b0, b1, b2, b3)):
            gate = _sigmoid(zg_ref[:, n * D_MODEL:(n + 1) * D_MODEL].astype(F32))
            acc = acc + gate * b_ref[...].astype(F32)
        o_ref[...] = acc.astype(BF16)

    tile = pl.BlockSpec((TOKEN_TILE, D_MODEL), lambda i: (i, 0))
    wide = pl.BlockSpec((TOKEN_TILE, N_BRANCH * D_MODEL), lambda i: (i, 0))
    return pl.pallas_call(
        body, name=name, out_shape=jax.ShapeDtypeStruct((s, D_MODEL), BF16), grid=(s // TOKEN_TILE,),
        in_specs=[wide, tile, tile, tile, tile], out_specs=tile,
        compiler_params=_params(("parallel",)))(zg, *branches)


def merge_bwd(zg, branches, dm, name):
    s = zg.shape[0]

    def body(zg_ref, b0, b1, b2, b3, dm_ref, dzg_ref, d0, d1, d2, d3):
        dmv = dm_ref[...].astype(F32)
        for n, (b_ref, d_ref) in enumerate(((b0, d0), (b1, d1), (b2, d2), (b3, d3))):
            cols = slice(n * D_MODEL, (n + 1) * D_MODEL)
            gate = _sigmoid(zg_ref[:, cols].astype(F32))
            d_ref[...] = (gate * dmv).astype(BF16)
            dzg_ref[:, cols] = (dmv * b_ref[...].astype(F32) * gate * (1.0 - gate)).astype(BF16)

    tile = pl.BlockSpec((TOKEN_TILE, D_MODEL), lambda i: (i, 0))
    wide = pl.BlockSpec((TOKEN_TILE, N_BRANCH * D_MODEL), lambda i: (i, 0))
    act = jax.ShapeDtypeStruct((s, D_MODEL), BF16)
    return pl.pallas_call(
        body, name=name, out_shape=(jax.ShapeDtypeStruct((s, N_BRANCH * D_MODEL), BF16), act, act, act, act),
        grid=(s // TOKEN_TILE,), in_specs=[wide, tile, tile, tile, tile, tile],
        out_specs=(wide, tile, tile, tile, tile), compiler_params=_params(("parallel",)))(zg, *branches, dm)


def swiglu_fwd(gu, name):
    s = gu.shape[0]

    def body(g_ref, u_ref, o_ref):
        gate = g_ref[...].astype(F32)
        o_ref[...] = (gate * _sigmoid(gate) * u_ref[...].astype(F32)).astype(BF16)

    return pl.pallas_call(
        body, name=name, out_shape=jax.ShapeDtypeStruct((s, D_FF), BF16), grid=(s // TOKEN_TILE,),
        in_specs=[pl.BlockSpec((TOKEN_TILE, D_FF), lambda i: (i, 0)), pl.BlockSpec((TOKEN_TILE, D_FF), lambda i: (i, 1))],
        out_specs=pl.BlockSpec((TOKEN_TILE, D_FF), lambda i: (i, 0)), compiler_params=_params(("parallel",)))(gu, gu)


def swiglu_bwd(gu, dact, name):
    s = gu.shape[0]

    def body(g_ref, u_ref, da_ref, o_ref):
        gate = g_ref[...].astype(F32)
        sg = _sigmoid(gate)
        da = da_ref[...].astype(F32)
        o_ref[:, :D_FF] = (da * u_ref[...].astype(F32) * sg * (1.0 + gate * (1.0 - sg))).astype(BF16)
        o_ref[:, D_FF:] = (da * gate * sg).astype(BF16)

    half = pl.BlockSpec((TOKEN_TILE, D_FF), lambda i: (i, 0))
    return pl.pallas_call(
        body, name=name, out_shape=jax.ShapeDtypeStruct((s, 2 * D_FF), BF16), grid=(s // TOKEN_TILE,),
        in_specs=[half, pl.BlockSpec((TOKEN_TILE, D_FF), lambda i: (i, 1)), half],
        out_specs=pl.BlockSpec((TOKEN_TILE, 2 * D_FF), lambda i: (i, 0)),
        compiler_params=_params(("parallel",)))(gu, gu, dact)


ADAMW_BLOCK_BYTES = 1 << 20


def adamw(parts, w, m, v, name):
    n_parts, r, c = w.shape
    tr = _divisor_tile(r, max(SUBLANES, ADAMW_BLOCK_BYTES // (4 * c)), SUBLANES)
    tiles = r // tr

    def part_spec(j):
        return pl.BlockSpec((N_DEV, tr, c), lambda i: (0, jnp.clip(i - j * tiles, 0, tiles - 1), 0))

    def body(*refs):
        p_refs = refs[:n_parts]
        w_ref, m_ref, v_ref, g_ref, d_ref, nm_ref, nv_ref = refs[n_parts:]
        which = pl.program_id(0) // tiles
        g = None
        for j, p_ref in enumerate(p_refs):
            gj = p_ref[0].astype(F32)
            for i in range(1, N_DEV):
                gj = gj + p_ref[i].astype(F32)
            g = gj if g is None else jnp.where(which == j, gj, g)
        nm = ADAM_B1 * m_ref[0] + (1.0 - ADAM_B1) * g
        nv = ADAM_B2 * v_ref[0] + (1.0 - ADAM_B2) * (g * g)
        m_hat = nm / (1.0 - ADAM_B1 ** ADAM_STEP)
        v_hat = nv / (1.0 - ADAM_B2 ** ADAM_STEP)
        g_ref[0] = g
        d_ref[0] = -ADAM_LR * (m_hat / (jnp.sqrt(v_hat) + ADAM_EPS) + ADAM_WD * w_ref[0])
        nm_ref[0] = nm
        nv_ref[0] = nv

    tile = pl.BlockSpec((1, tr, c), lambda i: (i // tiles, i % tiles, 0))
    shape = jax.ShapeDtypeStruct(w.shape, F32)
    return pl.pallas_call(
        body, name=name, out_shape=(shape, shape, shape, shape), grid=(n_parts * tiles,),
        in_specs=[part_spec(j) for j in range(n_parts)] + [tile, tile, tile],
        out_specs=(tile, tile, tile, tile), compiler_params=_params(("parallel",)))(*parts, w, m, v)


_RELATIONS = [(a, b, e) for a in (0, 1) for b in (0, 1) for e in (0, 1)][1:]


_HBM_SPEC = pl.BlockSpec(memory_space=pltpu.HBM)
_SEM_SPEC = pl.BlockSpec(memory_space=pltpu.SEMAPHORE)
_ANY_SPEC = pl.BlockSpec(memory_space=pl.ANY)
_DATAFLOW = pltpu.SideEffectType.DATAFLOW_SIDE_EFFECTING


_OTHER_CHIPS = [(1, 0), (0, 1), (1, 1)]
_FIRST_LEVEL = [(0, 0, 1)] + [(a, b, 0) for a, b in _OTHER_CHIPS]


def _remote_copies(ins, lands, send_sems, recv_sems, scatter, relations):
    x, y, c = lax.axis_index("x"), lax.axis_index("y"), lax.axis_index("c")
    me = 4 * x + 2 * y + c
    copies = []
    for t in range(len(ins)):
        for k, (a, b, e) in enumerate(relations):
            px, py, pc = (x + a) % 2, (y + b) % 2, (c + e) % 2
            src = ins[t].at[4 * px + 2 * py + pc] if scatter[t] else ins[t]
            copies.append(pltpu.make_async_remote_copy(
                src_ref=src, dst_ref=lands[t].at[me], send_sem=send_sems.at[t * len(relations) + k],
                recv_sem=recv_sems.at[t * len(relations) + k],
                device_id=(px, py, pc), device_id_type=pl.DeviceIdType.MESH))
    return copies


def _forward_copies(lands, send_sems, recv_sems):
    x, y, c = lax.axis_index("x"), lax.axis_index("y"), lax.axis_index("c")
    copies = []
    for t in range(len(lands)):
        for k, (a, b) in enumerate(_OTHER_CHIPS):
            slot = lands[t].at[4 * ((x + a) % 2) + 2 * ((y + b) % 2) + c]
            copies.append(pltpu.make_async_remote_copy(
                src_ref=slot, dst_ref=slot, send_sem=send_sems.at[t * len(_OTHER_CHIPS) + k],
                recv_sem=recv_sems.at[t * len(_OTHER_CHIPS) + k],
                device_id=(x, y, 1 - c), device_id_type=pl.DeviceIdType.MESH))
    return copies


def _place_own(landed, own, me):
    return lax.dynamic_update_index_in_dim(landed, own, me, 0)


def exchange_start(arrays, scatter, name, after=None, relations=_RELATIONS):
    n = len(arrays)
    n_rel = len(relations)
    land_shapes = [a.shape if scatter[t] else (N_DEV,) + a.shape for t, a in enumerate(arrays)]
    zones = [lax.empty(s, a.dtype) for s, a in zip(land_shapes, arrays)]

    def body(*refs):
        ins, lands = refs[:n], refs[n:2 * n]
        send_sems, recv_sems = refs[-2 * n - 3], refs[-2 * n - 2]
        token = refs[-1]
        for cp in _remote_copies(ins, lands, send_sems, recv_sems, scatter, relations):
            cp.start()
        token[...] = jnp.zeros_like(token)

    sems = pltpu.SemaphoreType.DMA((n * n_rel,))
    out_shape = ((sems, sems) + tuple(pltpu.HBM(a.shape, a.dtype) for a in arrays)
                 + tuple(pltpu.HBM(s, a.dtype) for s, a in zip(land_shapes, arrays))
                 + (jax.ShapeDtypeStruct((SUBLANES, LANES), F32),))
    operands = [pltpu.with_memory_space_constraint(a, pltpu.HBM) for a in arrays]
    operands += [pltpu.with_memory_space_constraint(z, pltpu.HBM) for z in zones]
    in_specs = [_HBM_SPEC] * (2 * n)
    if after is not None:
        operands.append(after)
        in_specs.append(_ANY_SPEC)
    res = pl.pallas_call(
        body, name=name, out_shape=out_shape, in_specs=in_specs,
        out_specs=(_SEM_SPEC, _SEM_SPEC) + (_HBM_SPEC,) * (2 * n) + (pl.BlockSpec(memory_space=pltpu.VMEM),),
        input_output_aliases={i: 2 + i for i in range(2 * n)},
        compiler_params=pltpu.CompilerParams(has_side_effects=_DATAFLOW))(*operands)
    handle = (res[0], res[1], res[2:2 + n], res[2 + n:2 + 2 * n], tuple(scatter), relations)
    return handle, res[-1]


def exchange_wait(handle, after, name):
    send_sems, recv_sems, sources, lands, scatter, relations = handle
    n = len(sources)

    def body(*refs):
        ins, lzs = refs[:n], refs[n:2 * n]
        send_ref, recv_ref = refs[2 * n], refs[2 * n + 1]
        for cp in _remote_copies(ins, lzs, send_ref, recv_ref, scatter, relations):
            cp.wait_send()
            cp.wait_recv()

    out_shape = (tuple(pltpu.HBM(a.shape, a.dtype) for a in sources) + tuple(pltpu.HBM(a.shape, a.dtype) for a in lands))
    res = pl.pallas_call(
        body, name=name, out_shape=out_shape, in_specs=[_HBM_SPEC] * (2 * n) + [_SEM_SPEC, _SEM_SPEC, _ANY_SPEC],
        out_specs=(_HBM_SPEC,) * (2 * n), input_output_aliases={i: i for i in range(2 * n)},
        compiler_params=pltpu.CompilerParams(has_side_effects=_DATAFLOW))(*sources, *lands, send_sems, recv_sems, after)
    return res[n:], res[:n]


def forward_start(lands, name):
    n = len(lands)

    def body(*refs):
        send_sems, recv_sems, token = refs[n], refs[n + 1], refs[-1]
        for cp in _forward_copies(refs[:n], send_sems, recv_sems):
            cp.start()
        token[...] = jnp.zeros_like(token)

    sems = pltpu.SemaphoreType.DMA((n * len(_OTHER_CHIPS),))
    res = pl.pallas_call(
        body, name=name,
        out_shape=(sems, sems) + tuple(pltpu.HBM(a.shape, a.dtype) for a in lands)
        + (jax.ShapeDtypeStruct((SUBLANES, LANES), F32),),
        in_specs=[_HBM_SPEC] * n,
        out_specs=(_SEM_SPEC, _SEM_SPEC) + (_HBM_SPEC,) * n + (pl.BlockSpec(memory_space=pltpu.VMEM),),
        input_output_aliases={i: 2 + i for i in range(n)},
        compiler_params=pltpu.CompilerParams(has_side_effects=_DATAFLOW))(*lands)
    return (res[0], res[1], res[2:2 + n]), res[-1]


def forward_wait(handle, after, name):
    send_sems, recv_sems, lands = handle
    n = len(lands)

    def body(*refs):
        for cp in _forward_copies(refs[:n], refs[n], refs[n + 1]):
            cp.wait_send()
            cp.wait_recv()

    return pl.pallas_call(
        body, name=name, out_shape=tuple(pltpu.HBM(a.shape, a.dtype) for a in lands),
        in_specs=[_HBM_SPEC] * n + [_SEM_SPEC, _SEM_SPEC, _ANY_SPEC], out_specs=(_HBM_SPEC,) * n,
        input_output_aliases={i: i for i in range(n)},
        compiler_params=pltpu.CompilerParams(has_side_effects=_DATAFLOW))(*lands, send_sems, recv_sems, after)


_SMALL = ("norm_mix", "sg_ln_g", "sg_ln_b", "sg_b", "cv_b", "cv_ln_g", "cv_ln_b", "attn_sinks", "norm_ffn",
          "norm_final")
_PACK_UNIT = SUBLANES * LANES


def _pack(tensors):
    rows = []
    for t in tensors:
        flat = t.reshape(-1)
        pad = (-flat.shape[0]) % _PACK_UNIT
        rows.append(jnp.pad(flat, (0, pad)).reshape(-1, LANES))
    return jnp.concatenate(rows, axis=0)


def _unpack(packed, like):
    out, row = [], 0
    for t in like:
        size = 1
        for d in t.shape:
            size *= d
        rows = -(-size // _PACK_UNIT) * SUBLANES
        out.append(packed[row:row + rows].reshape(-1)[:size].reshape(t.shape))
        row += rows
    return out


def _layer_fwd(l, x, p, late_params, mid_hook=None, ffn_hook=None):
    tag = f"l{l}_"
    xn = rmsnorm_fwd(x, p["norm_mix"], tag + "norm_mix")
    proj = matmul(xn, p["w_in_t"], "nt", BF16, tag + "proj_a", tm_cap=1024, tn_cap=2176, b_rows=PROJ_A)
    zg = matmul(xn, p["w_in_t_g"], "nt", BF16, tag + "proj_g", tm_cap=1024, tn_cap=2048)
    y_a = mixer_a_fwd(proj, p["sg_ln_g"], p["sg_ln_b"], p["sg_w"], p["sg_b"], tag + "mix_a")
    conv = conv_b_fwd(proj, p["cv_w"], p["cv_b"], tag + "conv_b")
    y_b = ln_silu_fwd(conv, p["cv_ln_g"], p["cv_ln_b"], tag + "ln_silu")
    y_d = conv_d_fwd(proj, p["sc_w"], tag + "conv_d")
    qk = rope_fwd(proj, p["cos"], p["sin"], tag + "rope")
    qh = _to_heads(qk[:, :Q_WIDTH], N_Q_HEADS)
    kh = _to_heads(qk[:, Q_WIDTH:], N_KV_HEADS)
    vh = _to_heads(proj[:, 2688:2816], N_KV_HEADS)
    token = mid_hook(qh) if mid_hook is not None else None
    oh = attention_fwd(qh, kh, vh, p["sinks"], tag + "attn", after=token)
    y_c = _from_heads(oh)
    ys = (y_a, y_b, y_c, y_d)
    p = {**p, **late_params(y_c)}
    branches = tuple(matmul(ys[n], p["w_branch"][n], "nn", BF16, tag + f"branch{n}", tm_cap=1024, tn_cap=1024)
                     for n in range(N_BRANCH))
    merged = merge_fwd(zg, branches, tag + "merge")
    x_mid = matmul(merged, p["w_out"], "nn", F32, tag + "out", add=x, tm_cap=1024, tn_cap=1024)
    token = ffn_hook(x_mid) if ffn_hook is not None else None
    hn = rmsnorm_fwd(x_mid, p["norm_ffn"], tag + "norm_ffn")
    gu = matmul(hn, p["w_gate_up_t"], "nt", BF16, tag + "gate_up", tm_cap=512, tn_cap=2816, after=token)
    act = swiglu_fwd(gu, tag + "swiglu")
    x_out = matmul(act, p["w_down"], "nn", F32, tag + "down", add=x_mid, tm_cap=512, tn_cap=1024)
    saved = dict(x=x, xn=xn, proj=proj, zg=zg, conv=conv, qh=qh, kh=kh, vh=vh, ys=ys, branches=branches,
                 merged=merged, x_mid=x_mid, hn=hn, gu=gu, act=act)
    return x_out, saved, p


def _layer_bwd(l, dx_out, p, sv, emit, after=None):
    tag = f"l{l}_b_"
    g = {}
    dact = matmul(dx_out, p["w_down"], "nt", BF16, tag + "dact", after=after, tm_cap=512, tn_cap=2816)
    dw_down = matmul(sv["act"], dx_out, "tn", BF16, tag + "dw_down", tm_cap=1408, tn_cap=512)
    dgu = swiglu_bwd(sv["gu"], dact, tag + "swiglu")
    dhn = matmul(dgu, p["w_gate_up_t"], "nn", BF16, tag + "dhn", tm_cap=512, tn_cap=512)
    dw_gate_up = matmul(dgu, sv["hn"], "tn", BF16, tag + "dw_gate_up", tm_cap=1408, tn_cap=1024)
    token = emit("a", {"w_gate_up": dw_gate_up, "w_down": dw_down})
    dx_mid, g["norm_ffn"] = rmsnorm_bwd(sv["x_mid"], p["norm_ffn"], dhn, dx_out, tag + "norm_ffn")
    dmerged = matmul(dx_mid, p["w_out"], "nt", BF16, tag + "dmerged", after=token, tm_cap=1024, tn_cap=1024)
    dw_out = matmul(sv["merged"], dx_mid, "tn", BF16, tag + "dw_out", tm_cap=1024, tn_cap=512)
    dzg, *dbranches = merge_bwd(sv["zg"], sv["branches"], dmerged, tag + "merge")
    dys = [matmul(dbranches[n], p["w_branch"][n], "nt", BF16, tag + f"dy{n}", tm_cap=1024, tn_cap=512)
           for n in range(N_BRANCH)]
    dw_branch = jnp.stack(
        [matmul(sv["ys"][n], dbranches[n], "tn", BF16, tag + f"dw_branch{n}", tm_cap=512, tn_cap=1024)
         for n in range(N_BRANCH)])
    token = emit("b", {"w_branch": dw_branch, "w_out": dw_out})
    proj = sv["proj"]
    dz_a, g["sg_ln_g"], g["sg_ln_b"], g["sg_w"], dsb = mixer_a_bwd(
        proj, dys[0], p["sg_ln_g"], p["sg_ln_b"], p["sg_w"], p["sg_b"], tag + "mix_a")
    g["sg_b"] = dsb[:, :, 0]
    dconv, g["cv_ln_g"], g["cv_ln_b"] = ln_silu_bwd(sv["conv"], dys[1], p["cv_ln_g"], p["cv_ln_b"], tag + "ln_silu")
    da, dgate, dcw, g["cv_b"] = conv_b_bwd(proj, p["cv_w"], dconv, tag + "conv_b")
    g["cv_w"] = dcw[:CV_KERNEL]
    doh = _to_heads(dys[2], N_Q_HEADS)
    dqh, dkc, dkp, dvc, dvp, dsk = attention_bwd(sv["qh"], sv["kh"], sv["vh"], p["sinks"], doh, tag + "attn")
    g["attn_sinks"] = dsk[:, 0, 0]
    dqk_cur = jnp.concatenate([_from_heads(dqh), _from_heads(dkc)], axis=1)
    dqk_prev = jnp.concatenate([jnp.zeros((SEQ, Q_WIDTH), F32), _from_heads(_shift_window(dkp))], axis=1)
    dqk = rope_bwd(dqk_cur, dqk_prev, p["cos"], -p["sin"], tag + "rope")
    dv = (_from_heads(dvc) + _from_heads(_shift_window(dvp))).astype(BF16)
    dbg, dcg, dh, dsw = conv_d_bwd(proj, p["sc_w"], dys[3], tag + "conv_d")
    g["sc_w"] = dsw[:SC_KERNEL]
    dproj = jnp.concatenate([dz_a, da, dgate, dqk, dv, dbg, dcg, dh], axis=1)
    dw_in = matmul(dproj, sv["xn"], "tn", BF16, tag + "dw_in_a", after=token, tm_cap=2176, tn_cap=512,
                   out_rows=PROJ_WIDTH)
    g["w_in"] = matmul(dzg, sv["xn"], "tn", BF16, tag + "dw_in_g", tm_cap=256, tn_cap=1024, into=dw_in,
                       into_row=PROJ_A)
    token = emit("c", {n: g.pop(n) for n in _EARLY}, {"sg_w": g.pop("sg_w")})
    dxn = matmul(dproj, p["w_in_t"], "nn", F32, tag + "dxn_a", after=token, tm_cap=512, tn_cap=512, b_rows=PROJ_A)
    dxn = matmul(dzg, p["w_in_t_g"], "nn", F32, tag + "dxn_g", add=dxn, tm_cap=512, tn_cap=512)
    dx_in, g["norm_mix"] = rmsnorm_bwd(sv["x"], p["norm_mix"], dxn, dx_mid, tag + "norm_mix")
    return dx_in, g, token


_EARLY = ("w_in", "cv_w", "sc_w")
_LATE = ("w_branch", "w_out", "w_gate_up", "w_down")


_TRANSPOSED = ("w_in", "w_gate_up")


def _shard_view(name, t):
    return jnp.swapaxes(t, 1, 2) if name in _TRANSPOSED else t


def _full_weight(name, t):
    if name in ("w_out", "w_down") + _TRANSPOSED:
        return t.reshape(-1, t.shape[-1])
    if name == "w_branch":
        return t.transpose(1, 2, 0, 3).reshape(N_BRANCH, SG_WIDTH, D_MODEL)
    return t.transpose(1, 0, 2).reshape(t.shape[1], -1)


def _to_blocks(name, full):
    if name in ("w_out", "w_down") + _TRANSPOSED:
        return full.reshape(N_DEV, -1, full.shape[-1])
    if name == "w_branch":
        return full.reshape(N_BRANCH, SG_WIDTH, N_DEV, -1).transpose(2, 0, 1, 3)
    return full.reshape(full.shape[0], N_DEV, -1).transpose(1, 0, 2)


def _rope_tables():
    pos = jnp.arange(SEQ, dtype=F32)
    inv_freq = 1.0 / (ROPE_THETA ** (jnp.arange(0, HEAD_DIM, 2, dtype=F32) / HEAD_DIM))
    ang = pos[:, None] * inv_freq[None, :]
    cos, sin = jnp.cos(ang), jnp.sin(ang)
    reps = LANES // HEAD_DIM
    return jnp.tile(jnp.concatenate([cos, cos], axis=1), (1, reps)), jnp.tile(jnp.concatenate([-sin, sin], axis=1), (1, reps))


def kernel(x, norm_mix, w_in, sg_ln_g, sg_ln_b, sg_w, sg_b, cv_w, cv_b, cv_ln_g, cv_ln_b, attn_sinks, sc_w, w_branch, w_out, norm_ffn, w_gate_up, w_down, norm_final, loss_target, m_norm_mix, m_w_in, m_sg_ln_g, m_sg_ln_b, m_sg_w, m_sg_b, m_cv_w, m_cv_b, m_cv_ln_g, m_cv_ln_b, m_attn_sinks, m_sc_w, m_w_branch, m_w_out, m_norm_ffn, m_w_gate_up, m_w_down, m_norm_final, v_norm_mix, v_w_in, v_sg_ln_g, v_sg_ln_b, v_sg_w, v_sg_b, v_cv_w, v_cv_b, v_cv_ln_g, v_cv_ln_b, v_attn_sinks, v_sc_w, v_w_branch, v_w_out, v_norm_ffn, v_w_gate_up, v_w_down, v_norm_final):
    names = ("norm_mix", "w_in", "sg_ln_g", "sg_ln_b", "sg_w", "sg_b", "cv_w", "cv_b", "cv_ln_g", "cv_ln_b",
             "attn_sinks", "sc_w", "w_branch", "w_out", "norm_ffn", "w_gate_up", "w_down", "norm_final")
    w = dict(zip(names, (norm_mix, w_in, sg_ln_g, sg_ln_b, sg_w, sg_b, cv_w, cv_b, cv_ln_g, cv_ln_b, attn_sinks,
                         sc_w, w_branch, w_out, norm_ffn, w_gate_up, w_down, norm_final)))
    m = dict(zip(names, (m_norm_mix, m_w_in, m_sg_ln_g, m_sg_ln_b, m_sg_w, m_sg_b, m_cv_w, m_cv_b, m_cv_ln_g,
                         m_cv_ln_b, m_attn_sinks, m_sc_w, m_w_branch, m_w_out, m_norm_ffn, m_w_gate_up, m_w_down,
                         m_norm_final)))
    v = dict(zip(names, (v_norm_mix, v_w_in, v_sg_ln_g, v_sg_ln_b, v_sg_w, v_sg_b, v_cv_w, v_cv_b, v_cv_ln_g,
                         v_cv_ln_b, v_attn_sinks, v_sc_w, v_w_branch, v_w_out, v_norm_ffn, v_w_gate_up, v_w_down,
                         v_norm_final)))

    me = 4 * lax.axis_index("x") + 2 * lax.axis_index("y") + lax.axis_index("c")
    groups = [(l, group) for l in range(DEPTH) for group in (_EARLY, _LATE)]
    shards = {(l, group): [_shard_view(n, w[n])[l].astype(BF16) for n in group] for l, group in groups}
    gathers, forwards, own_shards, token = {}, {}, {}, None
    for l, group in groups:
        gathers[(l, group)], token = exchange_start(
            shards[(l, group)], [False] * len(group), f"gather_start{l}_{group[0]}", after=token,
            relations=_FIRST_LEVEL)

    def begin_forward(l, group, after):
        landed, own_shards[(l, group)] = exchange_wait(gathers[(l, group)], after, f"gather_wait{l}_{group[0]}")
        forwards[(l, group)], tok = forward_start(landed, f"forward_start{l}_{group[0]}")
        return tok

    def landed_weights(l, group, after):
        landed = forward_wait(forwards[(l, group)], after, f"forward_wait{l}_{group[0]}")
        return {n + "_t" if n in _TRANSPOSED else n: _full_weight(n, _place_own(t, own, me))
                for n, t, own in zip(group, landed, own_shards[(l, group)])}

    cos_t, sin_t = _rope_tables()

    def early_params(l, after):
        full = landed_weights(l, _EARLY, after)
        return dict(
            norm_mix=w["norm_mix"][l][None], norm_ffn=w["norm_ffn"][l][None],
            w_in_t=full["w_in_t"], w_in_t_g=full["w_in_t"][PROJ_A:],
            sg_ln_g=w["sg_ln_g"][l][None], sg_ln_b=w["sg_ln_b"][l][None], sg_w=w["sg_w"][l],
            sg_b=jnp.broadcast_to(w["sg_b"][l][:, :, None], (SG_GROUPS, SG_CHUNK, LANES)),
            cv_w=jnp.pad(full["cv_w"].astype(F32), ((0, HALO - CV_KERNEL), (0, 0))),
            cv_b=w["cv_b"][l][None], cv_ln_g=w["cv_ln_g"][l][None], cv_ln_b=w["cv_ln_b"][l][None],
            sinks=jnp.broadcast_to(w["attn_sinks"][l][:, None, None], (N_Q_HEADS, 1, LANES)),
            sc_w=jnp.pad(full["sc_w"].astype(F32), ((0, SUBLANES - SC_KERNEL), (0, 0))),
            cos=cos_t, sin=sin_t)

    params, saved = [None] * DEPTH, [None] * DEPTH
    h = x[0]
    after = begin_forward(0, _EARLY, token)
    for l in range(DEPTH):
        h, saved[l], params[l] = _layer_fwd(
            l, h, early_params(l, after), lambda behind, l=l: landed_weights(l, _LATE, behind),
            mid_hook=lambda behind, l=l: begin_forward(l, _LATE, behind),
            ffn_hook=(lambda behind, l=l: begin_forward(l + 1, _EARLY, behind)) if l + 1 < DEPTH else None)
        after = h
    loss_row, dh, d_norm_final = loss_head(h, w["norm_final"][None], loss_target[0], "loss_head")

    sent = {}

    def emitter(l):
        def emit(group, grads_of, replicated=None):
            replicated = replicated or {}
            send = [_to_blocks(n, grads_of[n].astype(BF16)) for n in grads_of] + list(replicated.values())
            flags = [True] * len(grads_of) + [False] * len(replicated)
            handle, tok = exchange_start(send, flags, f"grads_start{l}{group}")
            sent[(l, group)] = (handle, tuple(grads_of) + tuple(replicated), flags)
            return tok
        return emit

    grads = [None] * DEPTH
    token = None
    for l in reversed(range(DEPTH)):
        dh, grads[l], token = _layer_bwd(l, dh, params[l], saved[l], emitter(l), after=token)
    grad_x = dh[None]

    stacked = {n: jnp.stack([grads[l][n] for l in range(DEPTH)]) for n in _SMALL if n != "norm_final"}
    for n in ("norm_mix", "norm_ffn", "sg_ln_g", "sg_ln_b", "cv_b", "cv_ln_g", "cv_ln_b"):
        stacked[n] = stacked[n][:, 0]
    stacked["norm_final"] = d_norm_final[0]
    no_state = jnp.zeros((1,), F32)
    small_like = [w[n] for n in _SMALL] + [no_state]
    small_part = _pack([stacked[n] for n in _SMALL] + [loss_row[0, :1]])
    handle_small, token = exchange_start([small_part], [False], "grads_start_small", after=token)

    def received(l, group, after):
        handle, group_names, flags = sent[(l, group)]
        landed, sources = exchange_wait(handle, after, f"grads_wait{l}{group}")
        return {n: _place_own(t, lax.dynamic_index_in_dim(s, me, 0, keepdims=False) if scattered else s, me)
                for n, t, s, scattered in zip(group_names, landed, sources, flags)}

    out_g, out_d, out_m, out_v = {}, {}, {}, {}

    def update(n, by_layer):
        shape = _shard_view(n, w[n]).shape
        view = (DEPTH, w[n].size // (DEPTH * shape[-1]), shape[-1])
        parts = [t.reshape((N_DEV,) + view[1:]) for t in by_layer]
        res = adamw(parts, *[_shard_view(n, t).reshape(view) for t in (w[n], m[n], v[n])], "adamw_" + n)
        out_g[n], out_d[n], out_m[n], out_v[n] = (_shard_view(n, t.reshape(shape)) for t in res)
        return res[0]

    behind = token
    for group in ("a", "b", "c"):
        r1 = received(1, group, behind)
        r0 = received(0, group, next(iter(r1.values())))
        for n in r0:
            behind = update(n, [r0[n], r1[n]])
    landed, sources = exchange_wait(handle_small, behind, "grads_wait_small")
    res = adamw([_place_own(landed[0], sources[0], me)], _pack(small_like)[None],
                _pack([m[n] for n in _SMALL] + [no_state])[None], _pack([v[n] for n in _SMALL] + [no_state])[None],
                "adamw_small")
    for store, packed in zip((out_g, out_d, out_m, out_v), res):
        for n, t in zip(_SMALL + ("loss",), _unpack(packed[0], small_like)):
            store[n] = t

    loss = out_g["loss"][0]
    return (loss, grad_x, *[out_g[n] for n in names], *[out_d[n] for n in names], *[out_m[n] for n in names],
            *[out_v[n] for n in names])
```

```python
import jax
import jax.numpy as jnp
from jax import lax
from jax.experimental import pallas as pl
from jax.experimental.pallas import tpu as pltpu

F32 = jnp.float32
BF16 = jnp.bfloat16

SEQ = 2048
D_MODEL = 1024
DEPTH = 2
SG_WIDTH = 512
SG_CHUNK = 128
SG_GROUPS = 4
CV_WIDTH = 512
CV_KERNEL = 31
HEAD_DIM = 64
N_Q_HEADS = 8
N_KV_HEADS = 2
Q_WIDTH = 512
KV_WIDTH = 128
WINDOW = 128
SC_WIDTH = 512
SC_KERNEL = 3
N_BRANCH = 4
D_FF = 2816
EPS = 1e-6
ROPE_THETA = 10000.0
PROJ_A = 4352
PROJ_WIDTH = 8448
N_DEV = 8

ADAM_LR = 0.001
ADAM_B1 = 0.9
ADAM_B2 = 0.999
ADAM_EPS = 1e-08
ADAM_WD = 0.01
ADAM_STEP = 10

LANES = 128
SUBLANES = 8
VMEM_LIMIT_BYTES = 48 * 1024 * 1024
HALO = 32
CONV_ROWS = 256
TOKEN_TILE = 256
NORM_TILE = 512
ROPE_TILE = 1024

_SQRT_HALF = 0.7071067811865476
_INV_SQRT_2PI = 0.3989422804014327


def _params(semantics=None):
    return pltpu.CompilerParams(dimension_semantics=semantics, vmem_limit_bytes=VMEM_LIMIT_BYTES)


def _divisor_tile(n, cap, unit):
    best = None
    for t in range(unit, min(n, cap) + 1, unit):
        if n % t == 0:
            best = t
    return best if best is not None else n


_DIMS = {"nn": (((1,), (0,)), ((), ())), "nt": (((1,), (1,)), ((), ())), "tn": (((0,), (0,)), ((), ()))}


def matmul(a, b, mode, out_dtype, name, add=None, tm_cap=512, tn_cap=512, after=None, b_rows=None, out_rows=None,
           into=None, into_row=0):
    if mode == "nn":
        (m, k), n = a.shape, b.shape[1]
        k = b_rows if b_rows is not None else k
    elif mode == "nt":
        (m, k), n = a.shape, (b_rows if b_rows is not None else b.shape[0])
    else:
        (k, m), n = a.shape, b.shape[1]
    tm = _divisor_tile(m, tm_cap, LANES)
    tn = _divisor_tile(n, tn_cap, LANES)
    row0 = into_row // tm
    assert row0 * tm == into_row
    a_spec = pl.BlockSpec((k, tm), lambda i, j: (0, i)) if mode == "tn" else pl.BlockSpec((tm, k), lambda i, j: (i, 0))
    b_spec = pl.BlockSpec((tn, k), lambda i, j: (j, 0)) if mode == "nt" else pl.BlockSpec((k, tn), lambda i, j: (0, j))
    o_spec = pl.BlockSpec((tm, tn), lambda i, j: (i + row0, j))
    dims = _DIMS[mode]

    def body(*refs):
        a_ref, b_ref = refs[0], refs[1]
        o_ref = refs[-1]
        acc = lax.dot_general(a_ref[...].astype(BF16), b_ref[...].astype(BF16), dims, preferred_element_type=F32)
        if add is not None:
            acc = acc + refs[2][...].astype(F32)
        o_ref[...] = acc.astype(out_dtype)

    unread = tuple(t for t in (after, into) if t is not None)
    operands = (a, b) + (() if add is None else (add,)) + unread
    in_specs = [a_spec, b_spec] + ([o_spec] if add is not None else [])
    in_specs += [pl.BlockSpec(memory_space=pl.ANY)] * len(unread)
    aliases = {len(operands) - 1: 0} if into is not None else {}
    return pl.pallas_call(
        body, name=name,
        out_shape=jax.ShapeDtypeStruct((into.shape[0] if into is not None else out_rows or m, n), out_dtype),
        grid=(m // tm, n // tn),
        in_specs=in_specs, out_specs=o_spec, input_output_aliases=aliases,
        compiler_params=_params(("parallel", "parallel")))(*operands)


def _sigmoid(x):
    return 1.0 / (1.0 + jnp.exp(-x))


def _gelu(x):
    return 0.5 * x * (1.0 + lax.erf(x * _SQRT_HALF))


def _gelu_grad(x):
    return 0.5 * (1.0 + lax.erf(x * _SQRT_HALF)) + x * _INV_SQRT_2PI * jnp.exp(-0.5 * x * x)


def _rms_stats(x):
    r = lax.rsqrt(jnp.mean(x * x, axis=-1, keepdims=True) + EPS)
    return x * r, r


def _rms_bwd(dxn, xhat, r, g):
    h = dxn * g
    return r * (h - xhat * jnp.mean(h * xhat, axis=-1, keepdims=True))


def _ln_stats(x):
    mu = jnp.mean(x, axis=-1, keepdims=True)
    xc = x - mu
    rstd = lax.rsqrt(jnp.mean(xc * xc, axis=-1, keepdims=True) + EPS)
    return xc * rstd, rstd


def _ln_bwd(dy, xhat, rstd, g):
    dxhat = dy * g
    return rstd * (dxhat - jnp.mean(dxhat, axis=-1, keepdims=True)
                   - xhat * jnp.mean(dxhat * xhat, axis=-1, keepdims=True))


def _accumulate(ref, value, first):
    @pl.when(first)
    def _():
        ref[...] = value

    @pl.when(jnp.logical_not(first))
    def _():
        ref[...] += value


def _shift_rows(win, shift, n_out):
    n = win.shape[0]
    if shift % n == 0:
        return win[:n_out]
    return pltpu.roll(win, n - shift, axis=0)[:n_out]


def _row_spec(width):
    return pl.BlockSpec((1, width), lambda i: (0, 0))


def rmsnorm_fwd(x, g, name):
    s, d = x.shape

    def body(x_ref, g_ref, o_ref):
        xhat, _ = _rms_stats(x_ref[...])
        o_ref[...] = (xhat * g_ref[...]).astype(BF16)

    tile = pl.BlockSpec((NORM_TILE, d), lambda i: (i, 0))
    return pl.pallas_call(
        body, name=name, out_shape=jax.ShapeDtypeStruct((s, d), BF16), grid=(s // NORM_TILE,),
        in_specs=[tile, _row_spec(d)], out_specs=tile, compiler_params=_params(("parallel",)))(x, g)


def rmsnorm_bwd(x, g, dxn, dres, name):
    s, d = x.shape

    def body(x_ref, g_ref, dxn_ref, dres_ref, dx_ref, dg_ref):
        xhat, r = _rms_stats(x_ref[...])
        dxn_v = dxn_ref[...].astype(F32)
        dx_ref[...] = dres_ref[...] + _rms_bwd(dxn_v, xhat, r, g_ref[...])
        _accumulate(dg_ref, jnp.sum(dxn_v * xhat, axis=0, keepdims=True), pl.program_id(0) == 0)

    tile = pl.BlockSpec((NORM_TILE, d), lambda i: (i, 0))
    return pl.pallas_call(
        body, name=name, out_shape=(jax.ShapeDtypeStruct((s, d), F32), jax.ShapeDtypeStruct((1, d), F32)),
        grid=(s // NORM_TILE,), in_specs=[tile, _row_spec(d), tile, tile], out_specs=(tile, _row_spec(d)),
        compiler_params=_params(("arbitrary",)))(x, g, dxn, dres)


def loss_head(x, g, target, name):
    s, d = x.shape

    def body(x_ref, g_ref, t_ref, loss_ref, dx_ref, dg_ref):
        first = pl.program_id(0) == 0
        xhat, r = _rms_stats(x_ref[...])
        gv = g_ref[...]
        err = xhat * gv - t_ref[...]
        part = 0.5 * jnp.sum(jnp.sum(err * err, axis=-1, keepdims=True), axis=0, keepdims=True) / d
        _accumulate(loss_ref, jnp.broadcast_to(part, (1, LANES)), first)
        dy = err / d
        dx_ref[...] = _rms_bwd(dy, xhat, r, gv)
        _accumulate(dg_ref, jnp.sum(dy * xhat, axis=0, keepdims=True), first)

    tile = pl.BlockSpec((NORM_TILE, d), lambda i: (i, 0))
    return pl.pallas_call(
        body, name=name,
        out_shape=(jax.ShapeDtypeStruct((1, LANES), F32), jax.ShapeDtypeStruct((s, d), F32),
                   jax.ShapeDtypeStruct((1, d), F32)),
        grid=(s // NORM_TILE,), in_specs=[tile, _row_spec(d), tile],
        out_specs=(_row_spec(LANES), tile, _row_spec(d)), compiler_params=_params(("arbitrary",)))(x, g, target)


def _tril_mask():
    row = lax.broadcasted_iota(jnp.int32, (SG_CHUNK, SG_CHUNK), 0)
    col = lax.broadcasted_iota(jnp.int32, (SG_CHUNK, SG_CHUNK), 1)
    return row >= col


def _sg_specs():
    vec = _row_spec(SG_WIDTH)
    mat = pl.BlockSpec((SG_GROUPS, SG_CHUNK, SG_CHUNK), lambda i: (0, 0, 0))
    return vec, mat


def mixer_a_fwd(proj, ln_g, ln_b, w_s, b_s, name):
    s = proj.shape[0]
    chunks = TOKEN_TILE // SG_CHUNK

    def body(z_ref, lg_ref, lb_ref, w_ref, b_ref, o_ref):
        ge = _gelu(z_ref[...].astype(F32))
        u = ge[:, :SG_WIDTH]
        xhat, _ = _ln_stats(ge[:, SG_WIDTH:])
        vn = xhat * lg_ref[...] + lb_ref[...]
        tril = _tril_mask()
        for ci in range(chunks):
            rows = slice(ci * SG_CHUNK, (ci + 1) * SG_CHUNK)
            for g in range(SG_GROUPS):
                cols = slice(g * LANES, (g + 1) * LANES)
                wm = jnp.where(tril, w_ref[g], 0.0).astype(BF16)
                mixed = jnp.dot(wm, vn[rows, cols].astype(BF16), preferred_element_type=F32) + b_ref[g]
                o_ref[rows, cols] = (u[rows, cols] * mixed).astype(BF16)

    vec, mat = _sg_specs()
    return pl.pallas_call(
        body, name=name, out_shape=jax.ShapeDtypeStruct((s, SG_WIDTH), BF16), grid=(s // TOKEN_TILE,),
        in_specs=[pl.BlockSpec((TOKEN_TILE, 2 * SG_WIDTH), lambda i: (i, 0)), vec, vec, mat, mat],
        out_specs=pl.BlockSpec((TOKEN_TILE, SG_WIDTH), lambda i: (i, 0)),
        compiler_params=_params(("parallel",)))(proj, ln_g, ln_b, w_s, b_s)


def mixer_a_bwd(proj, dy, ln_g, ln_b, w_s, b_s, name):
    s = proj.shape[0]
    chunks = TOKEN_TILE // SG_CHUNK

    def body(z_ref, dy_ref, lg_ref, lb_ref, w_ref, b_ref, dz_ref, dlg_ref, dlb_ref, dw_ref, db_ref, du_scr, dvn_scr):
        first = pl.program_id(0) == 0

        @pl.when(first)
        def _():
            dw_ref[...] = jnp.zeros_like(dw_ref)
            db_ref[...] = jnp.zeros_like(db_ref)

        z = z_ref[...].astype(F32)
        ge = _gelu(z)
        u = ge[:, :SG_WIDTH]
        xhat, rstd = _ln_stats(ge[:, SG_WIDTH:])
        lg = lg_ref[...]
        vn = xhat * lg + lb_ref[...]
        dyv = dy_ref[...].astype(F32)
        tril = _tril_mask()
        for ci in range(chunks):
            rows = slice(ci * SG_CHUNK, (ci + 1) * SG_CHUNK)
            for g in range(SG_GROUPS):
                cols = slice(g * LANES, (g + 1) * LANES)
                wm = jnp.where(tril, w_ref[g], 0.0).astype(BF16)
                vg = vn[rows, cols].astype(BF16)
                mixed = jnp.dot(wm, vg, preferred_element_type=F32) + b_ref[g]
                dyb = dyv[rows, cols]
                du_scr[rows, cols] = dyb * mixed
                dmix = dyb * u[rows, cols]
                db_ref[g] += jnp.broadcast_to(jnp.sum(dmix, axis=1, keepdims=True), (SG_CHUNK, LANES))
                dmb = dmix.astype(BF16)
                dwg = lax.dot_general(dmb, vg, _DIMS["nt"], preferred_element_type=F32)
                dw_ref[g] += jnp.where(tril, dwg, 0.0)
                dvn_scr[rows, cols] = lax.dot_general(wm, dmb, _DIMS["tn"], preferred_element_type=F32)
        dvn = dvn_scr[...]
        _accumulate(dlg_ref, jnp.sum(dvn * xhat, axis=0, keepdims=True), first)
        _accumulate(dlb_ref, jnp.sum(dvn, axis=0, keepdims=True), first)
        dvv = _ln_bwd(dvn, xhat, rstd, lg)
        gg = _gelu_grad(z)
        dz_ref[:, :SG_WIDTH] = (du_scr[...] * gg[:, :SG_WIDTH]).astype(BF16)
        dz_ref[:, SG_WIDTH:] = (dvv * gg[:, SG_WIDTH:]).astype(BF16)

    vec, mat = _sg_specs()
    mat_shape = jax.ShapeDtypeStruct((SG_GROUPS, SG_CHUNK, SG_CHUNK), F32)
    vec_shape = jax.ShapeDtypeStruct((1, SG_WIDTH), F32)
    return pl.pallas_call(
        body, name=name,
        out_shape=(jax.ShapeDtypeStruct((s, 2 * SG_WIDTH), BF16), vec_shape, vec_shape, mat_shape, mat_shape),
        grid=(s // TOKEN_TILE,),
        in_specs=[pl.BlockSpec((TOKEN_TILE, 2 * SG_WIDTH), lambda i: (i, 0)),
                  pl.BlockSpec((TOKEN_TILE, SG_WIDTH), lambda i: (i, 0)), vec, vec, mat, mat],
        out_specs=(pl.BlockSpec((TOKEN_TILE, 2 * SG_WIDTH), lambda i: (i, 0)), vec, vec, mat, mat),
        scratch_shapes=[pltpu.VMEM((TOKEN_TILE, SG_WIDTH), F32), pltpu.VMEM((TOKEN_TILE, SG_WIDTH), F32)],
        compiler_params=_params(("arbitrary",)))(proj, dy, ln_g, ln_b, w_s, b_s)


_B_A_BLOCK = 1024 // LANES
_B_G_BLOCK = 1536 // LANES
_CH_TILES = CV_WIDTH // LANES


def _col_spec(s, first_block):
    return pl.BlockSpec((s, LANES), lambda j: (0, first_block + j))


def conv_b_fwd(proj, w_pad, bias, name):
    s = proj.shape[0]

    def body(a_ref, g_ref, w_ref, b_ref, c_ref, upad):
        upad[0:HALO, :] = jnp.zeros((HALO, LANES), F32)
        upad[HALO:, :] = a_ref[...].astype(F32) * _sigmoid(g_ref[...].astype(F32))
        w = w_ref[...]
        bv = b_ref[...]

        def block(bi, carry):
            start = pl.multiple_of(bi * CONV_ROWS, CONV_ROWS)
            win = upad[pl.ds(start, CONV_ROWS + HALO), :]
            acc = jnp.zeros((CONV_ROWS, LANES), F32)
            for k in range(CV_KERNEL):
                acc = acc + w[k:k + 1, :] * _shift_rows(win, HALO - (CV_KERNEL - 1) + k, CONV_ROWS)
            c_ref[pl.ds(start, CONV_ROWS), :] = acc + bv
            return carry

        lax.fori_loop(0, s // CONV_ROWS, block, 0)

    return pl.pallas_call(
        body, name=name, out_shape=jax.ShapeDtypeStruct((s, CV_WIDTH), F32), grid=(_CH_TILES,),
        in_specs=[_col_spec(s, _B_A_BLOCK), _col_spec(s, _B_G_BLOCK), _col_spec(HALO, 0), _col_spec(1, 0)],
        out_specs=_col_spec(s, 0), scratch_shapes=[pltpu.VMEM((s + HALO, LANES), F32)],
        compiler_params=_params(("parallel",)))(proj, proj, w_pad, bias)


def conv_b_bwd(proj, w_pad, dc, name):
    s = proj.shape[0]

    def body(a_ref, g_ref, w_ref, dc_ref, da_ref, dg_ref, dw_ref, db_ref, upad, dpad, dw_scr):
        upad[0:HALO, :] = jnp.zeros((HALO, LANES), F32)
        upad[HALO:, :] = a_ref[...].astype(F32) * _sigmoid(g_ref[...].astype(F32))
        dcv = dc_ref[...]
        dpad[0:s, :] = dcv
        dpad[s:, :] = jnp.zeros((HALO, LANES), F32)
        db_ref[...] = jnp.sum(dcv, axis=0, keepdims=True)
        dw_scr[...] = jnp.zeros((HALO, LANES), F32)
        w = w_ref[...]

        def block(bi, carry):
            start = pl.multiple_of(bi * CONV_ROWS, CONV_ROWS)
            uwin = upad[pl.ds(start, CONV_ROWS + HALO), :]
            dwin = dpad[pl.ds(start, CONV_ROWS + HALO), :]
            dcb = dwin[:CONV_ROWS]
            du = jnp.zeros((CONV_ROWS, LANES), F32)
            for k in range(CV_KERNEL):
                du = du + w[k:k + 1, :] * _shift_rows(dwin, CV_KERNEL - 1 - k, CONV_ROWS)
                ush = _shift_rows(uwin, HALO - (CV_KERNEL - 1) + k, CONV_ROWS)
                dw_scr[k:k + 1, :] += jnp.sum(dcb * ush, axis=0, keepdims=True)
            av = a_ref[pl.ds(start, CONV_ROWS), :].astype(F32)
            sg = _sigmoid(g_ref[pl.ds(start, CONV_ROWS), :].astype(F32))
            da_ref[pl.ds(start, CONV_ROWS), :] = (du * sg).astype(BF16)
            dg_ref[pl.ds(start, CONV_ROWS), :] = (du * av * sg * (1.0 - sg)).astype(BF16)
            return carry

        lax.fori_loop(0, s // CONV_ROWS, block, 0)
        dw_ref[...] = dw_scr[...]

    act = jax.ShapeDtypeStruct((s, CV_WIDTH), BF16)
    return pl.pallas_call(
        body, name=name,
        out_shape=(act, act, jax.ShapeDtypeStruct((HALO, CV_WIDTH), F32), jax.ShapeDtypeStruct((1, CV_WIDTH), F32)),
        grid=(_CH_TILES,),
        in_specs=[_col_spec(s, _B_A_BLOCK), _col_spec(s, _B_G_BLOCK), _col_spec(HALO, 0), _col_spec(s, 0)],
        out_specs=(_col_spec(s, 0), _col_spec(s, 0), _col_spec(HALO, 0), _col_spec(1, 0)),
        scratch_shapes=[pltpu.VMEM((s + HALO, LANES), F32), pltpu.VMEM((s + HALO, LANES), F32),
                        pltpu.VMEM((HALO, LANES), F32)],
        compiler_params=_params(("parallel",)))(proj, proj, w_pad, dc)


def ln_silu_fwd(c, ln_g, ln_b, name):
    s, d = c.shape

    def body(c_ref, g_ref, b_ref, o_ref):
        xhat, _ = _ln_stats(c_ref[...])
        cn = xhat * g_ref[...] + b_ref[...]
        o_ref[...] = (cn * _sigmoid(cn)).astype(BF16)

    tile = pl.BlockSpec((NORM_TILE, d), lambda i: (i, 0))
    return pl.pallas_call(
        body, name=name, out_shape=jax.ShapeDtypeStruct((s, d), BF16), grid=(s // NORM_TILE,),
        in_specs=[tile, _row_spec(d), _row_spec(d)], out_specs=tile,
        compiler_params=_params(("parallel",)))(c, ln_g, ln_b)


def ln_silu_bwd(c, dy, ln_g, ln_b, name, after=None):
    s, d = c.shape
    unread = () if after is None else (after,)

    def body(c_ref, dy_ref, g_ref, b_ref, *rest):
        dc_ref, dg_ref, db_ref = rest[-3:]
        first = pl.program_id(0) == 0
        xhat, rstd = _ln_stats(c_ref[...])
        gv = g_ref[...]
        cn = xhat * gv + b_ref[...]
        sg = _sigmoid(cn)
        dcn = dy_ref[...].astype(F32) * sg * (1.0 + cn * (1.0 - sg))
        _accumulate(dg_ref, jnp.sum(dcn * xhat, axis=0, keepdims=True), first)
        _accumulate(db_ref, jnp.sum(dcn, axis=0, keepdims=True), first)
        dc_ref[...] = _ln_bwd(dcn, xhat, rstd, gv)

    tile = pl.BlockSpec((NORM_TILE, d), lambda i: (i, 0))
    vec_shape = jax.ShapeDtypeStruct((1, d), F32)
    return pl.pallas_call(
        body, name=name, out_shape=(jax.ShapeDtypeStruct((s, d), F32), vec_shape, vec_shape),
        grid=(s // NORM_TILE,),
        in_specs=[tile, tile, _row_spec(d), _row_spec(d)] + [pl.BlockSpec(memory_space=pl.ANY)] * len(unread),
        out_specs=(tile, _row_spec(d), _row_spec(d)),
        compiler_params=_params(("arbitrary",)))(c, dy, ln_g, ln_b, *unread)


_D_BLOCK = 2816 // LANES


def _conv3(win, w):
    acc = jnp.zeros((CONV_ROWS, LANES), F32)
    for k in range(SC_KERNEL):
        acc = acc + w[k:k + 1, :] * _shift_rows(win, HALO - (SC_KERNEL - 1) + k, CONV_ROWS)
    return acc


def conv_d_fwd(proj, w_pad, name):
    s = proj.shape[0]

    def body(bg_ref, cg_ref, h_ref, w_ref, o_ref, ppad):
        ppad[0:HALO, :] = jnp.zeros((HALO, LANES), F32)
        ppad[HALO:, :] = cg_ref[...].astype(F32) * h_ref[...].astype(F32)
        w = w_ref[...]

        def block(bi, carry):
            start = pl.multiple_of(bi * CONV_ROWS, CONV_ROWS)
            cv = _conv3(ppad[pl.ds(start, CONV_ROWS + HALO), :], w)
            o_ref[pl.ds(start, CONV_ROWS), :] = (bg_ref[pl.ds(start, CONV_ROWS), :].astype(F32) * cv).astype(BF16)
            return carry

        lax.fori_loop(0, s // CONV_ROWS, block, 0)

    return pl.pallas_call(
        body, name=name, out_shape=jax.ShapeDtypeStruct((s, SC_WIDTH), BF16), grid=(_CH_TILES,),
        in_specs=[_col_spec(s, _D_BLOCK), _col_spec(s, _D_BLOCK + _CH_TILES), _col_spec(s, _D_BLOCK + 2 * _CH_TILES),
                  _col_spec(SUBLANES, 0)],
        out_specs=_col_spec(s, 0), scratch_shapes=[pltpu.VMEM((s + HALO, LANES), F32)],
        compiler_params=_params(("parallel",)))(proj, proj, proj, w_pad)


def conv_d_bwd(proj, w_pad, dy, name):
    s = proj.shape[0]

    def body(bg_ref, cg_ref, h_ref, w_ref, dy_ref, dbg_ref, dcg_ref, dh_ref, dw_ref, ppad, dpad, dw_scr):
        ppad[0:HALO, :] = jnp.zeros((HALO, LANES), F32)
        ppad[HALO:, :] = cg_ref[...].astype(F32) * h_ref[...].astype(F32)
        dpad[0:s, :] = dy_ref[...].astype(F32) * bg_ref[...].astype(F32)
        dpad[s:, :] = jnp.zeros((HALO, LANES), F32)
        dw_scr[...] = jnp.zeros((SUBLANES, LANES), F32)
        w = w_ref[...]

        def block(bi, carry):
            start = pl.multiple_of(bi * CONV_ROWS, CONV_ROWS)
            rows = pl.ds(start, CONV_ROWS)
            pwin = ppad[pl.ds(start, CONV_ROWS + HALO), :]
            dwin = dpad[pl.ds(start, CONV_ROWS + HALO), :]
            dcvb = dwin[:CONV_ROWS]
            dbg_ref[rows, :] = (dy_ref[rows, :].astype(F32) * _conv3(pwin, w)).astype(BF16)
            dp = jnp.zeros((CONV_ROWS, LANES), F32)
            for k in range(SC_KERNEL):
                dp = dp + w[k:k + 1, :] * _shift_rows(dwin, SC_KERNEL - 1 - k, CONV_ROWS)
                psh = _shift_rows(pwin, HALO - (SC_KERNEL - 1) + k, CONV_ROWS)
                dw_scr[k:k + 1, :] += jnp.sum(dcvb * psh, axis=0, keepdims=True)
            dcg_ref[rows, :] = (dp * h_ref[rows, :].astype(F32)).astype(BF16)
            dh_ref[rows, :] = (dp * cg_ref[rows, :].astype(F32)).astype(BF16)
            return carry

        lax.fori_loop(0, s // CONV_ROWS, block, 0)
        dw_ref[...] = dw_scr[...]

    act = jax.ShapeDtypeStruct((s, SC_WIDTH), BF16)
    return pl.pallas_call(
        body, name=name, out_shape=(act, act, act, jax.ShapeDtypeStruct((SUBLANES, SC_WIDTH), F32)),
        grid=(_CH_TILES,),
        in_specs=[_col_spec(s, _D_BLOCK), _col_spec(s, _D_BLOCK + _CH_TILES), _col_spec(s, _D_BLOCK + 2 * _CH_TILES),
                  _col_spec(SUBLANES, 0), _col_spec(s, 0)],
        out_specs=(_col_spec(s, 0), _col_spec(s, 0), _col_spec(s, 0), _col_spec(SUBLANES, 0)),
        scratch_shapes=[pltpu.VMEM((s + HALO, LANES), F32), pltpu.VMEM((s + HALO, LANES), F32),
                        pltpu.VMEM((SUBLANES, LANES), F32)],
        compiler_params=_params(("parallel",)))(proj, proj, proj, w_pad, dy)


_QK_BLOCK = 2048 // LANES
_QK_BLOCKS = (Q_WIDTH + KV_WIDTH) // LANES


def _swap_halves(t):
    lane = lax.broadcasted_iota(jnp.int32, t.shape, 1)
    low = (lane % HEAD_DIM) < (HEAD_DIM // 2)
    return jnp.where(low, pltpu.roll(t, LANES - HEAD_DIM // 2, axis=1), pltpu.roll(t, HEAD_DIM // 2, axis=1))


def rope_fwd(proj, cos_t, sin_t, name, after=None):
    s = proj.shape[0]

    def body(t_ref, c_ref, s_ref, *rest):
        t = t_ref[...].astype(F32)
        rest[-1][...] = (t * c_ref[...] + _swap_halves(t) * s_ref[...]).astype(BF16)

    tr = min(ROPE_TILE, s)
    tab = pl.BlockSpec((tr, LANES), lambda i, j: (i, 0))
    extra = () if after is None else (after,)
    return pl.pallas_call(
        body, name=name, out_shape=jax.ShapeDtypeStruct((s, Q_WIDTH + KV_WIDTH), BF16),
        grid=(s // tr, _QK_BLOCKS),
        in_specs=[pl.BlockSpec((tr, LANES), lambda i, j: (i, _QK_BLOCK + j)), tab, tab]
        + [pl.BlockSpec(memory_space=pl.ANY) for _ in extra],
        out_specs=pl.BlockSpec((tr, LANES), lambda i, j: (i, j)),
        compiler_params=_params(("parallel", "parallel")))(proj, cos_t, sin_t, *extra)


def rope_bwd(d_cur, d_prev, cos_t, sin_t, name):
    s, w = d_cur.shape

    def body(a_ref, b_ref, c_ref, s_ref, o_ref):
        d = a_ref[...] + b_ref[...]
        o_ref[...] = (d * c_ref[...] + _swap_halves(d) * s_ref[...]).astype(BF16)

    tr = min(ROPE_TILE, s)
    tab = pl.BlockSpec((tr, LANES), lambda i, j: (i, 0))
    blk = pl.BlockSpec((tr, LANES), lambda i, j: (i, j))
    return pl.pallas_call(
        body, name=name, out_shape=jax.ShapeDtypeStruct((s, w), BF16), grid=(s // tr, w // LANES),
        in_specs=[blk, blk, tab, tab], out_specs=blk,
        compiler_params=_params(("parallel", "parallel")))(d_cur, d_prev, cos_t, sin_t)


_GROUP = N_Q_HEADS // N_KV_HEADS
_NEG = -1e30


def _attn_specs():
    q_spec = pl.BlockSpec((_GROUP, WINDOW, HEAD_DIM), lambda h, n: (h, n, 0))
    cur = pl.BlockSpec((1, WINDOW, HEAD_DIM), lambda h, n: (h, n, 0))
    prev = pl.BlockSpec((1, WINDOW, HEAD_DIM), lambda h, n: (h, jnp.maximum(n - 1, 0), 0))
    sink = pl.BlockSpec((_GROUP, 1, LANES), lambda h, n: (h, 0, 0))
    return q_spec, cur, prev, sink


def _attn_valid(n):
    qi = lax.broadcasted_iota(jnp.int32, (WINDOW, 2 * WINDOW), 0)
    kj = lax.broadcasted_iota(jnp.int32, (WINDOW, 2 * WINDOW), 1)
    delta = qi + WINDOW - kj
    return (delta >= 0) & (delta < WINDOW) & ((kj >= WINDOW) | (n > 0))


def _attn_probs(q, kcat, valid, sink_row):
    sc = lax.dot_general(q, kcat, _DIMS["nt"], preferred_element_type=F32) * (HEAD_DIM ** -0.5)
    sc = jnp.where(valid, sc, _NEG)
    sink = jnp.max(sink_row, axis=-1, keepdims=True)
    m = jnp.maximum(jnp.max(sc, axis=-1, keepdims=True), sink)
    p = jnp.where(valid, jnp.exp(sc - m), 0.0)
    es = jnp.exp(sink - m)
    inv = 1.0 / (jnp.sum(p, axis=-1, keepdims=True) + es)
    return p * inv, es * inv


def attention_fwd(qh, kh, vh, sinks_b, name, after=None):
    s = qh.shape[1]
    unread = () if after is None else (after,)

    def body(q_ref, kc_ref, kp_ref, vc_ref, vp_ref, sk_ref, *rest):
        o_ref = rest[-1]
        valid = _attn_valid(pl.program_id(1))
        kcat = jnp.concatenate([kp_ref[0], kc_ref[0]], axis=0)
        vcat = jnp.concatenate([vp_ref[0], vc_ref[0]], axis=0)
        for g in range(_GROUP):
            probs, _ = _attn_probs(q_ref[g], kcat, valid, sk_ref[g])
            o_ref[g] = jnp.dot(probs.astype(BF16), vcat, preferred_element_type=F32).astype(BF16)

    q_spec, cur, prev, sink = _attn_specs()
    return pl.pallas_call(
        body, name=name, out_shape=jax.ShapeDtypeStruct(qh.shape, BF16), grid=(N_KV_HEADS, s // WINDOW),
        in_specs=[q_spec, cur, prev, cur, prev, sink] + [pl.BlockSpec(memory_space=pl.ANY)] * len(unread),
        out_specs=q_spec, compiler_params=_params(("parallel", "parallel")))(qh, kh, kh, vh, vh, sinks_b, *unread)


def attention_bwd(qh, kh, vh, sinks_b, doh, name):
    s = qh.shape[1]

    def body(q_ref, kc_ref, kp_ref, vc_ref, vp_ref, sk_ref, do_ref, dq_ref, dkc_ref, dkp_ref, dvc_ref, dvp_ref, ds_ref):
        n = pl.program_id(1)
        valid = _attn_valid(n)
        kcat = jnp.concatenate([kp_ref[0], kc_ref[0]], axis=0)
        vcat = jnp.concatenate([vp_ref[0], vc_ref[0]], axis=0)
        dk = jnp.zeros((2 * WINDOW, HEAD_DIM), F32)
        dv = jnp.zeros((2 * WINDOW, HEAD_DIM), F32)
        for g in range(_GROUP):
            q = q_ref[g]
            do = do_ref[g]
            probs, ps = _attn_probs(q, kcat, valid, sk_ref[g])
            dprobs = lax.dot_general(do, vcat, _DIMS["nt"], preferred_element_type=F32)
            dv = dv + lax.dot_general(probs.astype(BF16), do, _DIMS["tn"], preferred_element_type=F32)
            rs = jnp.sum(probs * dprobs, axis=-1, keepdims=True)
            dsb = (probs * (dprobs - rs) * (HEAD_DIM ** -0.5)).astype(BF16)
            dq_ref[g] = jnp.dot(dsb, kcat, preferred_element_type=F32)
            dk = dk + lax.dot_general(dsb, q, _DIMS["tn"], preferred_element_type=F32)
            dsink = jnp.broadcast_to(-jnp.sum(ps * rs, axis=0, keepdims=True), (1, LANES))

            @pl.when(n == 0)
            def _():
                ds_ref[g] = dsink

            @pl.when(n > 0)
            def _():
                ds_ref[g] += dsink

        dkp_ref[0] = dk[:WINDOW]
        dkc_ref[0] = dk[WINDOW:]
        dvp_ref[0] = dv[:WINDOW]
        dvc_ref[0] = dv[WINDOW:]

    q_spec, cur, prev, sink = _attn_specs()
    kv_shape = jax.ShapeDtypeStruct(kh.shape, F32)
    return pl.pallas_call(
        body, name=name,
        out_shape=(jax.ShapeDtypeStruct(qh.shape, F32), kv_shape, kv_shape, kv_shape, kv_shape,
                   jax.ShapeDtypeStruct(sinks_b.shape, F32)),
        grid=(N_KV_HEADS, s // WINDOW), in_specs=[q_spec, cur, prev, cur, prev, sink, q_spec],
        out_specs=(q_spec, cur, cur, cur, cur, sink),
        compiler_params=_params(("parallel", "arbitrary")))(qh, kh, kh, vh, vh, sinks_b, doh)


def _to_heads(t, heads):
    return t.reshape(t.shape[0], heads, HEAD_DIM).transpose(1, 0, 2)


def _from_heads(t):
    return t.transpose(1, 0, 2).reshape(t.shape[1], t.shape[0] * HEAD_DIM)


def _shift_window(t):
    return jnp.concatenate([t[:, WINDOW:], jnp.zeros_like(t[:, :WINDOW])], axis=1)


def merge_fwd(zg, branches, name):
    s = zg.shape[0]

    def body(zg_ref, b0, b1, b2, b3, o_ref):
        acc = jnp.zeros((TOKEN_TILE, D_MODEL), F32)
        for n, b_ref in enumerate((b0, b1, b2, b3)):
            gate = _sigmoid(zg_ref[:, n * D_MODEL:(n + 1) * D_MODEL].astype(F32))
            acc = acc + gate * b_ref[...].astype(F32)
        o_ref[...] = acc.astype(BF16)

    tile = pl.BlockSpec((TOKEN_TILE, D_MODEL), lambda i: (i, 0))
    wide = pl.BlockSpec((TOKEN_TILE, N_BRANCH * D_MODEL), lambda i: (i, 0))
    return pl.pallas_call(
        body, name=name, out_shape=jax.ShapeDtypeStruct((s, D_MODEL), BF16), grid=(s // TOKEN_TILE,),
        in_specs=[wide, tile, tile, tile, tile], out_specs=tile,
        compiler_params=_params(("parallel",)))(zg, *branches)


def merge_bwd(zg, branches, dm, name):
    s = zg.shape[0]

    def body(zg_ref, b0, b1, b2, b3, dm_ref, dzg_ref, d0, d1, d2, d3):
        dmv = dm_ref[...].astype(F32)
        for n, (b_ref, d_ref) in enumerate(((b0, d0), (b1, d1), (b2, d2), (b3, d3))):
            cols = slice(n * D_MODEL, (n + 1) * D_MODEL)
            gate = _sigmoid(zg_ref[:, cols].astype(F32))
            d_ref[...] = (gate * dmv).astype(BF16)
            dzg_ref[:, cols] = (dmv * b_ref[...].astype(F32) * gate * (1.0 - gate)).astype(BF16)

    tile = pl.BlockSpec((TOKEN_TILE, D_MODEL), lambda i: (i, 0))
    wide = pl.BlockSpec((TOKEN_TILE, N_BRANCH * D_MODEL), lambda i: (i, 0))
    act = jax.ShapeDtypeStruct((s, D_MODEL), BF16)
    return pl.pallas_call(
        body, name=name, out_shape=(jax.ShapeDtypeStruct((s, N_BRANCH * D_MODEL), BF16), act, act, act, act),
        grid=(s // TOKEN_TILE,), in_specs=[wide, tile, tile, tile, tile, tile],
        out_specs=(wide, tile, tile, tile, tile), compiler_params=_params(("parallel",)))(zg, *branches, dm)


def swiglu_fwd(gu, name):
    s = gu.shape[0]

    def body(g_ref, u_ref, o_ref):
        gate = g_ref[...].astype(F32)
        o_ref[...] = (gate * _sigmoid(gate) * u_ref[...].astype(F32)).astype(BF16)

    return pl.pallas_call(
        body, name=name, out_shape=jax.ShapeDtypeStruct((s, D_FF), BF16), grid=(s // TOKEN_TILE,),
        in_specs=[pl.BlockSpec((TOKEN_TILE, D_FF), lambda i: (i, 0)), pl.BlockSpec((TOKEN_TILE, D_FF), lambda i: (i, 1))],
        out_specs=pl.BlockSpec((TOKEN_TILE, D_FF), lambda i: (i, 0)), compiler_params=_params(("parallel",)))(gu, gu)


def swiglu_bwd(gu, dact, name):
    s = gu.shape[0]

    def body(g_ref, u_ref, da_ref, o_ref):
        gate = g_ref[...].astype(F32)
        sg = _sigmoid(gate)
        da = da_ref[...].astype(F32)
        o_ref[:, :D_FF] = (da * u_ref[...].astype(F32) * sg * (1.0 + gate * (1.0 - sg))).astype(BF16)
        o_ref[:, D_FF:] = (da * gate * sg).astype(BF16)

    half = pl.BlockSpec((TOKEN_TILE, D_FF), lambda i: (i, 0))
    return pl.pallas_call(
        body, name=name, out_shape=jax.ShapeDtypeStruct((s, 2 * D_FF), BF16), grid=(s // TOKEN_TILE,),
        in_specs=[half, pl.BlockSpec((TOKEN_TILE, D_FF), lambda i: (i, 1)), half],
        out_specs=pl.BlockSpec((TOKEN_TILE, 2 * D_FF), lambda i: (i, 0)),
        compiler_params=_params(("parallel",)))(gu, gu, dact)


ADAMW_BLOCK_BYTES = 1 << 20


def adamw(parts, w, m, v, name):
    n_parts, r, c = w.shape
    tr = _divisor_tile(r, max(SUBLANES, ADAMW_BLOCK_BYTES // (4 * c)), SUBLANES)
    tiles = r // tr

    def part_spec(j):
        return pl.BlockSpec((N_DEV, tr, c), lambda i: (0, jnp.clip(i - j * tiles, 0, tiles - 1), 0))

    def body(*refs):
        p_refs = refs[:n_parts]
        w_ref, m_ref, v_ref, g_ref, d_ref, nm_ref, nv_ref = refs[n_parts:]
        which = pl.program_id(0) // tiles
        g = None
        for j, p_ref in enumerate(p_refs):
            gj = p_ref[0].astype(F32)
            for i in range(1, N_DEV):
                gj = gj + p_ref[i].astype(F32)
            g = gj if g is None else jnp.where(which == j, gj, g)
        nm = ADAM_B1 * m_ref[0] + (1.0 - ADAM_B1) * g
        nv = ADAM_B2 * v_ref[0] + (1.0 - ADAM_B2) * (g * g)
        m_hat = nm / (1.0 - ADAM_B1 ** ADAM_STEP)
        v_hat = nv / (1.0 - ADAM_B2 ** ADAM_STEP)
        g_ref[0] = g
        d_ref[0] = -ADAM_LR * (m_hat / (jnp.sqrt(v_hat) + ADAM_EPS) + ADAM_WD * w_ref[0])
        nm_ref[0] = nm
        nv_ref[0] = nv

    tile = pl.BlockSpec((1, tr, c), lambda i: (i // tiles, i % tiles, 0))
    shape = jax.ShapeDtypeStruct(w.shape, F32)
    return pl.pallas_call(
        body, name=name, out_shape=(shape, shape, shape, shape), grid=(n_parts * tiles,),
        in_specs=[part_spec(j) for j in range(n_parts)] + [tile, tile, tile],
        out_specs=(tile, tile, tile, tile), compiler_params=_params(("parallel",)))(*parts, w, m, v)


_RELATIONS = [(a, b, e) for a in (0, 1) for b in (0, 1) for e in (0, 1)][1:]


_HBM_SPEC = pl.BlockSpec(memory_space=pltpu.HBM)
_SEM_SPEC = pl.BlockSpec(memory_space=pltpu.SEMAPHORE)
_ANY_SPEC = pl.BlockSpec(memory_space=pl.ANY)
_DATAFLOW = pltpu.SideEffectType.DATAFLOW_SIDE_EFFECTING


_OTHER_CHIPS = [(1, 0), (0, 1), (1, 1)]
_FIRST_LEVEL = [(0, 0, 1)] + [(a, b, 0) for a, b in _OTHER_CHIPS]


def _remote_copies(ins, lands, send_sems, recv_sems, scatter, relations):
    x, y, c = lax.axis_index("x"), lax.axis_index("y"), lax.axis_index("c")
    me = 4 * x + 2 * y + c
    copies = []
    for t in range(len(ins)):
        for k, (a, b, e) in enumerate(relations):
            px, py, pc = (x + a) % 2, (y + b) % 2, (c + e) % 2
            src = ins[t].at[4 * px + 2 * py + pc] if scatter[t] else ins[t]
            copies.append(pltpu.make_async_remote_copy(
                src_ref=src, dst_ref=lands[t].at[me], send_sem=send_sems.at[t * len(relations) + k],
                recv_sem=recv_sems.at[t * len(relations) + k],
                device_id=(px, py, pc), device_id_type=pl.DeviceIdType.MESH))
    return copies


def _forward_copies(lands, send_sems, recv_sems):
    x, y, c = lax.axis_index("x"), lax.axis_index("y"), lax.axis_index("c")
    copies = []
    for t in range(len(lands)):
        for k, (a, b) in enumerate(_OTHER_CHIPS):
            slot = lands[t].at[4 * ((x + a) % 2) + 2 * ((y + b) % 2) + c]
            copies.append(pltpu.make_async_remote_copy(
                src_ref=slot, dst_ref=slot, send_sem=send_sems.at[t * len(_OTHER_CHIPS) + k],
                recv_sem=recv_sems.at[t * len(_OTHER_CHIPS) + k],
                device_id=(x, y, 1 - c), device_id_type=pl.DeviceIdType.MESH))
    return copies


def _place_own(landed, own, me):
    return lax.dynamic_update_index_in_dim(landed, own, me, 0)


def exchange_start(arrays, scatter, name, after=None, relations=_RELATIONS):
    n = len(arrays)
    n_rel = len(relations)
    land_shapes = [a.shape if scatter[t] else (N_DEV,) + a.shape for t, a in enumerate(arrays)]
    zones = [lax.empty(s, a.dtype) for s, a in zip(land_shapes, arrays)]

    def body(*refs):
        ins, lands = refs[:n], refs[n:2 * n]
        send_sems, recv_sems = refs[-2 * n - 3], refs[-2 * n - 2]
        token = refs[-1]
        for cp in _remote_copies(ins, lands, send_sems, recv_sems, scatter, relations):
            cp.start()
        token[...] = jnp.zeros_like(token)

    sems = pltpu.SemaphoreType.DMA((n * n_rel,))
    out_shape = ((sems, sems) + tuple(pltpu.HBM(a.shape, a.dtype) for a in arrays)
                 + tuple(pltpu.HBM(s, a.dtype) for s, a in zip(land_shapes, arrays))
                 + (jax.ShapeDtypeStruct((SUBLANES, LANES), F32),))
    operands = [pltpu.with_memory_space_constraint(a, pltpu.HBM) for a in arrays]
    operands += [pltpu.with_memory_space_constraint(z, pltpu.HBM) for z in zones]
    in_specs = [_HBM_SPEC] * (2 * n)
    if after is not None:
        operands.append(after)
        in_specs.append(_ANY_SPEC)
    res = pl.pallas_call(
        body, name=name, out_shape=out_shape, in_specs=in_specs,
        out_specs=(_SEM_SPEC, _SEM_SPEC) + (_HBM_SPEC,) * (2 * n) + (pl.BlockSpec(memory_space=pltpu.VMEM),),
        input_output_aliases={i: 2 + i for i in range(2 * n)},
        compiler_params=pltpu.CompilerParams(has_side_effects=_DATAFLOW))(*operands)
    handle = (res[0], res[1], res[2:2 + n], res[2 + n:2 + 2 * n], tuple(scatter), relations)
    return handle, res[-1]


def exchange_wait(handle, after, name):
    send_sems, recv_sems, sources, lands, scatter, relations = handle
    n = len(sources)

    def body(*refs):
        ins, lzs = refs[:n], refs[n:2 * n]
        send_ref, recv_ref = refs[2 * n], refs[2 * n + 1]
        for cp in _remote_copies(ins, lzs, send_ref, recv_ref, scatter, relations):
            cp.wait_send()
            cp.wait_recv()

    out_shape = (tuple(pltpu.HBM(a.shape, a.dtype) for a in sources) + tuple(pltpu.HBM(a.shape, a.dtype) for a in lands))
    res = pl.pallas_call(
        body, name=name, out_shape=out_shape, in_specs=[_HBM_SPEC] * (2 * n) + [_SEM_SPEC, _SEM_SPEC, _ANY_SPEC],
        out_specs=(_HBM_SPEC,) * (2 * n), input_output_aliases={i: i for i in range(2 * n)},
        compiler_params=pltpu.CompilerParams(has_side_effects=_DATAFLOW))(*sources, *lands, send_sems, recv_sems, after)
    return res[n:], res[:n]


def forward_start(lands, name):
    n = len(lands)

    def body(*refs):
        send_sems, recv_sems, token = refs[n], refs[n + 1], refs[-1]
        for cp in _forward_copies(refs[:n], send_sems, recv_sems):
            cp.start()
        token[...] = jnp.zeros_like(token)

    sems = pltpu.SemaphoreType.DMA((n * len(_OTHER_CHIPS),))
    res = pl.pallas_call(
        body, name=name,
        out_shape=(sems, sems) + tuple(pltpu.HBM(a.shape, a.dtype) for a in lands)
        + (jax.ShapeDtypeStruct((SUBLANES, LANES), F32),),
        in_specs=[_HBM_SPEC] * n,
        out_specs=(_SEM_SPEC, _SEM_SPEC) + (_HBM_SPEC,) * n + (pl.BlockSpec(memory_space=pltpu.VMEM),),
        input_output_aliases={i: 2 + i for i in range(n)},
        compiler_params=pltpu.CompilerParams(has_side_effects=_DATAFLOW))(*lands)
    return (res[0], res[1], res[2:2 + n]), res[-1]


def forward_wait(handle, after, name):
    send_sems, recv_sems, lands = handle
    n = len(lands)

    def body(*refs):
        for cp in _forward_copies(refs[:n], refs[n], refs[n + 1]):
            cp.wait_send()
            cp.wait_recv()

    return pl.pallas_call(
        body, name=name, out_shape=tuple(pltpu.HBM(a.shape, a.dtype) for a in lands),
        in_specs=[_HBM_SPEC] * n + [_SEM_SPEC, _SEM_SPEC, _ANY_SPEC], out_specs=(_HBM_SPEC,) * n,
        input_output_aliases={i: i for i in range(n)},
        compiler_params=pltpu.CompilerParams(has_side_effects=_DATAFLOW))(*lands, send_sems, recv_sems, after)


_SMALL = ("norm_mix", "sg_ln_g", "sg_ln_b", "sg_b", "cv_b", "cv_ln_g", "cv_ln_b", "attn_sinks", "norm_ffn",
          "norm_final")
_PACK_UNIT = SUBLANES * LANES


def _pack(entries):
    flat = []
    for entry in entries:
        parts = [t.reshape(-1) for t in (entry if isinstance(entry, (list, tuple)) else [entry])]
        size = sum(t.shape[0] for t in parts)
        flat += parts + [jnp.zeros(((-size) % _PACK_UNIT,), parts[0].dtype)]
    return jnp.concatenate(flat).reshape(-1, LANES)


def _unpack(packed, like):
    out, row = [], 0
    for t in like:
        size = 1
        for d in t.shape:
            size *= d
        rows = -(-size // _PACK_UNIT) * SUBLANES
        out.append(packed[row:row + rows].reshape(-1)[:size].reshape(t.shape))
        row += rows
    return out


def _layer_fwd(l, x, p, late_params, mid_hook=None, ffn_hook=None):
    tag = f"l{l}_"
    xn = rmsnorm_fwd(x, p["norm_mix"], tag + "norm_mix")
    proj = matmul(xn, p["w_in_t"], "nt", BF16, tag + "proj_a", tm_cap=1024, tn_cap=2176, b_rows=PROJ_A)
    zg = matmul(xn, p["w_in_t_g"], "nt", BF16, tag + "proj_g", tm_cap=1024, tn_cap=2048)
    y_a = mixer_a_fwd(proj, p["sg_ln_g"], p["sg_ln_b"], p["sg_w"], p["sg_b"], tag + "mix_a")
    conv = conv_b_fwd(proj, p["cv_w"], p["cv_b"], tag + "conv_b")
    y_b = ln_silu_fwd(conv, p["cv_ln_g"], p["cv_ln_b"], tag + "ln_silu")
    y_d = conv_d_fwd(proj, p["sc_w"], tag + "conv_d")
    qk = rope_fwd(proj, p["cos"], p["sin"], tag + "rope")
    qh = _to_heads(qk[:, :Q_WIDTH], N_Q_HEADS)
    kh = _to_heads(qk[:, Q_WIDTH:], N_KV_HEADS)
    vh = _to_heads(proj[:, 2688:2816], N_KV_HEADS)
    token = mid_hook(qh) if mid_hook is not None else None
    oh = attention_fwd(qh, kh, vh, p["sinks"], tag + "attn", after=token)
    y_c = _from_heads(oh)
    ys = (y_a, y_b, y_c, y_d)
    p = {**p, **late_params(y_c)}
    branches = tuple(matmul(ys[n], p["w_branch"][n], "nn", BF16, tag + f"branch{n}", tm_cap=1024, tn_cap=1024)
                     for n in range(N_BRANCH))
    merged = merge_fwd(zg, branches, tag + "merge")
    x_mid = matmul(merged, p["w_out"], "nn", F32, tag + "out", add=x, tm_cap=1024, tn_cap=1024)
    token = ffn_hook(x_mid) if ffn_hook is not None else None
    hn = rmsnorm_fwd(x_mid, p["norm_ffn"], tag + "norm_ffn")
    gu = matmul(hn, p["w_gate_up_t"], "nt", BF16, tag + "gate_up", tm_cap=512, tn_cap=2816, after=token)
    act = swiglu_fwd(gu, tag + "swiglu")
    x_out = matmul(act, p["w_down"], "nn", F32, tag + "down", add=x_mid, tm_cap=512, tn_cap=1024)
    saved = dict(x=x, xn=xn, proj=proj, zg=zg, conv=conv, qh=qh, kh=kh, vh=vh, ys=ys, branches=branches,
                 merged=merged, x_mid=x_mid, hn=hn, gu=gu, act=act)
    return x_out, saved, p


def _layer_bwd(l, dx_out, p, sv, emit, after=None):
    tag = f"l{l}_b_"
    g = {}
    dact = matmul(dx_out, p["w_down"], "nt", BF16, tag + "dact", after=after, tm_cap=512, tn_cap=2816)
    dw_down = matmul(sv["act"], dx_out, "tn", BF16, tag + "dw_down", tm_cap=1408, tn_cap=512)
    dgu = swiglu_bwd(sv["gu"], dact, tag + "swiglu")
    dhn = matmul(dgu, p["w_gate_up_t"], "nn", BF16, tag + "dhn", tm_cap=512, tn_cap=512)
    dw_gate_up = matmul(dgu, sv["hn"], "tn", BF16, tag + "dw_gate_up", tm_cap=1408, tn_cap=1024)
    token = emit("a", {"w_gate_up": dw_gate_up, "w_down": dw_down})
    dx_mid, g["norm_ffn"] = rmsnorm_bwd(sv["x_mid"], p["norm_ffn"], dhn, dx_out, tag + "norm_ffn")
    dmerged = matmul(dx_mid, p["w_out"], "nt", BF16, tag + "dmerged", after=token, tm_cap=1024, tn_cap=1024)
    dw_out = matmul(sv["merged"], dx_mid, "tn", BF16, tag + "dw_out", tm_cap=1024, tn_cap=512)
    dzg, *dbranches = merge_bwd(sv["zg"], sv["branches"], dmerged, tag + "merge")
    dys = [matmul(dbranches[n], p["w_branch"][n], "nt", BF16, tag + f"dy{n}", tm_cap=1024, tn_cap=512)
           for n in range(N_BRANCH)]
    dw_branch = jnp.stack(
        [matmul(sv["ys"][n], dbranches[n], "tn", BF16, tag + f"dw_branch{n}", tm_cap=512, tn_cap=1024)
         for n in range(N_BRANCH)])
    proj = sv["proj"]
    dz_a, g["sg_ln_g"], g["sg_ln_b"], dsw_a, dsb = mixer_a_bwd(
        proj, dys[0], p["sg_ln_g"], p["sg_ln_b"], p["sg_w"], p["sg_b"], tag + "mix_a")
    g["sg_b"] = dsb[:, :, 0]
    token = emit("b", {"w_branch": dw_branch, "w_out": dw_out}, {"sg_w": dsw_a})
    dconv, g["cv_ln_g"], g["cv_ln_b"] = ln_silu_bwd(sv["conv"], dys[1], p["cv_ln_g"], p["cv_ln_b"], tag + "ln_silu",
                                                    after=token)
    da, dgate, dcw, g["cv_b"] = conv_b_bwd(proj, p["cv_w"], dconv, tag + "conv_b")
    g["cv_w"] = dcw[:CV_KERNEL]
    doh = _to_heads(dys[2], N_Q_HEADS)
    dqh, dkc, dkp, dvc, dvp, dsk = attention_bwd(sv["qh"], sv["kh"], sv["vh"], p["sinks"], doh, tag + "attn")
    g["attn_sinks"] = dsk[:, 0, 0]
    dqk_cur = jnp.concatenate([_from_heads(dqh), _from_heads(dkc)], axis=1)
    dqk_prev = jnp.concatenate([jnp.zeros((SEQ, Q_WIDTH), F32), _from_heads(_shift_window(dkp))], axis=1)
    dqk = rope_bwd(dqk_cur, dqk_prev, p["cos"], -p["sin"], tag + "rope")
    dv = (_from_heads(dvc) + _from_heads(_shift_window(dvp))).astype(BF16)
    dbg, dcg, dh, dsw = conv_d_bwd(proj, p["sc_w"], dys[3], tag + "conv_d")
    g["sc_w"] = dsw[:SC_KERNEL]
    dproj = jnp.concatenate([dz_a, da, dgate, dqk, dv, dbg, dcg, dh], axis=1)
    dw_in = matmul(dproj, sv["xn"], "tn", BF16, tag + "dw_in_a", after=token, tm_cap=2176, tn_cap=512,
                   out_rows=PROJ_WIDTH)
    g["w_in"] = matmul(dzg, sv["xn"], "tn", BF16, tag + "dw_in_g", tm_cap=256, tn_cap=1024, into=dw_in,
                       into_row=PROJ_A)
    token = emit("c", {n: g.pop(n) for n in _EARLY})
    dxn = matmul(dproj, p["w_in_t"], "nn", F32, tag + "dxn_a", after=token, tm_cap=512, tn_cap=512, b_rows=PROJ_A)
    dxn = matmul(dzg, p["w_in_t_g"], "nn", F32, tag + "dxn_g", add=dxn, tm_cap=512, tn_cap=512)
    dx_in, g["norm_mix"] = rmsnorm_bwd(sv["x"], p["norm_mix"], dxn, dx_mid, tag + "norm_mix")
    return dx_in, g, token


_EARLY = ("w_in", "cv_w", "sc_w")
_LATE = ("w_branch", "w_out", "w_gate_up", "w_down")


_TRANSPOSED = ("w_in", "w_gate_up")


def _shard_view(name, t):
    return jnp.swapaxes(t, 1, 2) if name in _TRANSPOSED else t


def _full_weight(name, t):
    if name in ("w_out", "w_down") + _TRANSPOSED:
        return t.reshape(-1, t.shape[-1])
    if name == "w_branch":
        return t.transpose(1, 2, 0, 3).reshape(N_BRANCH, SG_WIDTH, D_MODEL)
    return t.transpose(1, 0, 2).reshape(t.shape[1], -1)


def _to_blocks(name, full):
    if name in ("w_out", "w_down") + _TRANSPOSED:
        return full.reshape(N_DEV, -1, full.shape[-1])
    if name == "w_branch":
        return full.reshape(N_BRANCH, SG_WIDTH, N_DEV, -1).transpose(2, 0, 1, 3)
    return full.reshape(full.shape[0], N_DEV, -1).transpose(1, 0, 2)


def _rope_tables():
    pos = jnp.arange(SEQ, dtype=F32)
    inv_freq = 1.0 / (ROPE_THETA ** (jnp.arange(0, HEAD_DIM, 2, dtype=F32) / HEAD_DIM))
    ang = pos[:, None] * inv_freq[None, :]
    cos, sin = jnp.cos(ang), jnp.sin(ang)
    reps = LANES // HEAD_DIM
    return jnp.tile(jnp.concatenate([cos, cos], axis=1), (1, reps)), jnp.tile(jnp.concatenate([-sin, sin], axis=1), (1, reps))


def kernel(x, norm_mix, w_in, sg_ln_g, sg_ln_b, sg_w, sg_b, cv_w, cv_b, cv_ln_g, cv_ln_b, attn_sinks, sc_w, w_branch, w_out, norm_ffn, w_gate_up, w_down, norm_final, loss_target, m_norm_mix, m_w_in, m_sg_ln_g, m_sg_ln_b, m_sg_w, m_sg_b, m_cv_w, m_cv_b, m_cv_ln_g, m_cv_ln_b, m_attn_sinks, m_sc_w, m_w_branch, m_w_out, m_norm_ffn, m_w_gate_up, m_w_down, m_norm_final, v_norm_mix, v_w_in, v_sg_ln_g, v_sg_ln_b, v_sg_w, v_sg_b, v_cv_w, v_cv_b, v_cv_ln_g, v_cv_ln_b, v_attn_sinks, v_sc_w, v_w_branch, v_w_out, v_norm_ffn, v_w_gate_up, v_w_down, v_norm_final):
    names = ("norm_mix", "w_in", "sg_ln_g", "sg_ln_b", "sg_w", "sg_b", "cv_w", "cv_b", "cv_ln_g", "cv_ln_b",
             "attn_sinks", "sc_w", "w_branch", "w_out", "norm_ffn", "w_gate_up", "w_down", "norm_final")
    w = dict(zip(names, (norm_mix, w_in, sg_ln_g, sg_ln_b, sg_w, sg_b, cv_w, cv_b, cv_ln_g, cv_ln_b, attn_sinks,
                         sc_w, w_branch, w_out, norm_ffn, w_gate_up, w_down, norm_final)))
    m = dict(zip(names, (m_norm_mix, m_w_in, m_sg_ln_g, m_sg_ln_b, m_sg_w, m_sg_b, m_cv_w, m_cv_b, m_cv_ln_g,
                         m_cv_ln_b, m_attn_sinks, m_sc_w, m_w_branch, m_w_out, m_norm_ffn, m_w_gate_up, m_w_down,
                         m_norm_final)))
    v = dict(zip(names, (v_norm_mix, v_w_in, v_sg_ln_g, v_sg_ln_b, v_sg_w, v_sg_b, v_cv_w, v_cv_b, v_cv_ln_g,
                         v_cv_ln_b, v_attn_sinks, v_sc_w, v_w_branch, v_w_out, v_norm_ffn, v_w_gate_up, v_w_down,
                         v_norm_final)))

    me = 4 * lax.axis_index("x") + 2 * lax.axis_index("y") + lax.axis_index("c")
    groups = [(l, group) for l in range(DEPTH) for group in (_EARLY, _LATE)]
    shards = {(l, group): [_shard_view(n, w[n])[l].astype(BF16) for n in group] for l, group in groups}
    gathers, forwards, own_shards, token = {}, {}, {}, None
    for l, group in groups:
        gathers[(l, group)], token = exchange_start(
            shards[(l, group)], [False] * len(group), f"gather_start{l}_{group[0]}", after=token,
            relations=_FIRST_LEVEL)

    def begin_forward(l, group, after):
        landed, own_shards[(l, group)] = exchange_wait(gathers[(l, group)], after, f"gather_wait{l}_{group[0]}")
        forwards[(l, group)], tok = forward_start(landed, f"forward_start{l}_{group[0]}")
        return tok

    def landed_weights(l, group, after):
        landed = forward_wait(forwards[(l, group)], after, f"forward_wait{l}_{group[0]}")
        return {n + "_t" if n in _TRANSPOSED else n: _full_weight(n, _place_own(t, own, me))
                for n, t, own in zip(group, landed, own_shards[(l, group)])}

    cos_t, sin_t = _rope_tables()

    def early_params(l, after):
        full = landed_weights(l, _EARLY, after)
        return dict(
            norm_mix=w["norm_mix"][l][None], norm_ffn=w["norm_ffn"][l][None],
            w_in_t=full["w_in_t"], w_in_t_g=full["w_in_t"][PROJ_A:],
            sg_ln_g=w["sg_ln_g"][l][None], sg_ln_b=w["sg_ln_b"][l][None], sg_w=w["sg_w"][l],
            sg_b=jnp.broadcast_to(w["sg_b"][l][:, :, None], (SG_GROUPS, SG_CHUNK, LANES)),
            cv_w=jnp.pad(full["cv_w"].astype(F32), ((0, HALO - CV_KERNEL), (0, 0))),
            cv_b=w["cv_b"][l][None], cv_ln_g=w["cv_ln_g"][l][None], cv_ln_b=w["cv_ln_b"][l][None],
            sinks=jnp.broadcast_to(w["attn_sinks"][l][:, None, None], (N_Q_HEADS, 1, LANES)),
            sc_w=jnp.pad(full["sc_w"].astype(F32), ((0, SUBLANES - SC_KERNEL), (0, 0))),
            cos=cos_t, sin=sin_t)

    params, saved = [None] * DEPTH, [None] * DEPTH
    h = x[0]
    after = begin_forward(0, _EARLY, token)
    for l in range(DEPTH):
        h, saved[l], params[l] = _layer_fwd(
            l, h, early_params(l, after), lambda behind, l=l: landed_weights(l, _LATE, behind),
            mid_hook=lambda behind, l=l: begin_forward(l, _LATE, behind),
            ffn_hook=(lambda behind, l=l: begin_forward(l + 1, _EARLY, behind)) if l + 1 < DEPTH else None)
        after = h
    loss_row, dh, d_norm_final = loss_head(h, w["norm_final"][None], loss_target[0], "loss_head")

    sent = {}

    def emitter(l):
        def emit(group, grads_of, replicated=None):
            replicated = replicated or {}
            send = [_to_blocks(n, grads_of[n].astype(BF16)) for n in grads_of] + list(replicated.values())
            flags = [True] * len(grads_of) + [False] * len(replicated)
            handle, tok = exchange_start(send, flags, f"grads_start{l}{group}")
            sent[(l, group)] = (handle, tuple(grads_of) + tuple(replicated), flags)
            return tok
        return emit

    grads = [None] * DEPTH
    token = None
    for l in reversed(range(DEPTH)):
        dh, grads[l], token = _layer_bwd(l, dh, params[l], saved[l], emitter(l), after=token)
    grad_x = dh[None]

    small_grads = [[d_norm_final] if n == "norm_final" else [grads[l][n] for l in range(DEPTH)] for n in _SMALL]
    no_state = jnp.zeros((1,), F32)
    small_like = [w[n] for n in _SMALL] + [no_state]
    small_part = _pack(small_grads + [loss_row[0, :1]])
    handle_small, token = exchange_start([small_part], [False], "grads_start_small", after=token)

    def received(l, group, after):
        handle, group_names, flags = sent[(l, group)]
        landed, sources = exchange_wait(handle, after, f"grads_wait{l}{group}")
        return {n: _place_own(t, lax.dynamic_index_in_dim(s, me, 0, keepdims=False) if scattered else s, me)
                for n, t, s, scattered in zip(group_names, landed, sources, flags)}

    out_g, out_d, out_m, out_v = {}, {}, {}, {}

    def update(n, by_layer):
        shape = _shard_view(n, w[n]).shape
        view = (DEPTH, w[n].size // (DEPTH * shape[-1]), shape[-1])
        parts = [t.reshape((N_DEV,) + view[1:]) for t in by_layer]
        res = adamw(parts, *[_shard_view(n, t).reshape(view) for t in (w[n], m[n], v[n])], "adamw_" + n)
        out_g[n], out_d[n], out_m[n], out_v[n] = (_shard_view(n, t.reshape(shape)) for t in res)
        return res[0]

    behind = token
    for group in ("a", "b", "c"):
        r1 = received(1, group, behind)
        r0 = received(0, group, next(iter(r1.values())))
        for n in r0:
            behind = update(n, [r0[n], r1[n]])
    landed, sources = exchange_wait(handle_small, behind, "grads_wait_small")
    res = adamw([_place_own(landed[0], sources[0], me)], _pack(small_like)[None],
                _pack([m[n] for n in _SMALL] + [no_state])[None], _pack([v[n] for n in _SMALL] + [no_state])[None],
                "adamw_small")
    for store, packed in zip((out_g, out_d, out_m, out_v), res):
        for n, t in zip(_SMALL + ("loss",), _unpack(packed[0], small_like)):
            store[n] = t

    loss = out_g["loss"][0]
    return (loss, grad_x, *[out_g[n] for n in names], *[out_d[n] for n in names], *[out_m[n] for n in names],
            *[out_v[n] for n in names])
```

```python
import jax
import jax.numpy as jnp
from jax import lax
from jax.experimental import pallas as pl
from jax.experimental.pallas import tpu as pltpu

F32 = jnp.float32
BF16 = jnp.bfloat16

SEQ = 2048
D_MODEL = 1024
DEPTH = 2
SG_WIDTH = 512
SG_CHUNK = 128
SG_GROUPS = 4
CV_WIDTH = 512
CV_KERNEL = 31
HEAD_DIM = 64
N_Q_HEADS = 8
N_KV_HEADS = 2
Q_WIDTH = 512
KV_WIDTH = 128
WINDOW = 128
SC_WIDTH = 512
SC_KERNEL = 3
N_BRANCH = 4
D_FF = 2816
EPS = 1e-6
ROPE_THETA = 10000.0
PROJ_A = 4352
PROJ_WIDTH = 8448
N_DEV = 8

ADAM_LR = 0.001
ADAM_B1 = 0.9
ADAM_B2 = 0.999
ADAM_EPS = 1e-08
ADAM_WD = 0.01
ADAM_STEP = 10

LANES = 128
SUBLANES = 8
VMEM_LIMIT_BYTES = 48 * 1024 * 1024
HALO = 32
CONV_ROWS = 256
TOKEN_TILE = 256
NORM_TILE = 512
ROPE_TILE = 1024

_SQRT_HALF = 0.7071067811865476
_INV_SQRT_2PI = 0.3989422804014327


def _params(semantics=None):
    return pltpu.CompilerParams(dimension_semantics=semantics, vmem_limit_bytes=VMEM_LIMIT_BYTES)


def _divisor_tile(n, cap, unit):
    best = None
    for t in range(unit, min(n, cap) + 1, unit):
        if n % t == 0:
            best = t
    return best if best is not None else n


_DIMS = {"nn": (((1,), (0,)), ((), ())), "nt": (((1,), (1,)), ((), ())), "tn": (((0,), (0,)), ((), ()))}


def matmul(a, b, mode, out_dtype, name, add=None, tm_cap=512, tn_cap=512, after=None, b_rows=None, out_rows=None,
           into=None, into_row=0):
    if mode == "nn":
        (m, k), n = a.shape, b.shape[1]
        k = b_rows if b_rows is not None else k
    elif mode == "nt":
        (m, k), n = a.shape, (b_rows if b_rows is not None else b.shape[0])
    else:
        (k, m), n = a.shape, b.shape[1]
    tm = _divisor_tile(m, tm_cap, LANES)
    tn = _divisor_tile(n, tn_cap, LANES)
    row0 = into_row // tm
    assert row0 * tm == into_row
    a_spec = pl.BlockSpec((k, tm), lambda i, j: (0, i)) if mode == "tn" else pl.BlockSpec((tm, k), lambda i, j: (i, 0))
    b_spec = pl.BlockSpec((tn, k), lambda i, j: (j, 0)) if mode == "nt" else pl.BlockSpec((k, tn), lambda i, j: (0, j))
    o_spec = pl.BlockSpec((tm, tn), lambda i, j: (i + row0, j))
    dims = _DIMS[mode]

    def body(*refs):
        a_ref, b_ref = refs[0], refs[1]
        o_ref = refs[-1]
        acc = lax.dot_general(a_ref[...].astype(BF16), b_ref[...].astype(BF16), dims, preferred_element_type=F32)
        if add is not None:
            acc = acc + refs[2][...].astype(F32)
        o_ref[...] = acc.astype(out_dtype)

    unread = tuple(t for t in (after, into) if t is not None)
    operands = (a, b) + (() if add is None else (add,)) + unread
    in_specs = [a_spec, b_spec] + ([o_spec] if add is not None else [])
    in_specs += [pl.BlockSpec(memory_space=pl.ANY)] * len(unread)
    aliases = {len(operands) - 1: 0} if into is not None else {}
    return pl.pallas_call(
        body, name=name,
        out_shape=jax.ShapeDtypeStruct((into.shape[0] if into is not None else out_rows or m, n), out_dtype),
        grid=(m // tm, n // tn),
        in_specs=in_specs, out_specs=o_spec, input_output_aliases=aliases,
        compiler_params=_params(("parallel", "parallel")))(*operands)


def _sigmoid(x):
    return 1.0 / (1.0 + jnp.exp(-x))


def _gelu(x):
    return 0.5 * x * (1.0 + lax.erf(x * _SQRT_HALF))


def _gelu_grad(x):
    return 0.5 * (1.0 + lax.erf(x * _SQRT_HALF)) + x * _INV_SQRT_2PI * jnp.exp(-0.5 * x * x)


def _rms_stats(x):
    r = lax.rsqrt(jnp.mean(x * x, axis=-1, keepdims=True) + EPS)
    return x * r, r


def _rms_bwd(dxn, xhat, r, g):
    h = dxn * g
    return r * (h - xhat * jnp.mean(h * xhat, axis=-1, keepdims=True))


def _ln_stats(x):
    mu = jnp.mean(x, axis=-1, keepdims=True)
    xc = x - mu
    rstd = lax.rsqrt(jnp.mean(xc * xc, axis=-1, keepdims=True) + EPS)
    return xc * rstd, rstd


def _ln_bwd(dy, xhat, rstd, g):
    dxhat = dy * g
    return rstd * (dxhat - jnp.mean(dxhat, axis=-1, keepdims=True)
                   - xhat * jnp.mean(dxhat * xhat, axis=-1, keepdims=True))


def _accumulate(ref, value, first):
    @pl.when(first)
    def _():
        ref[...] = value

    @pl.when(jnp.logical_not(first))
    def _():
        ref[...] += value


def _shift_rows(win, shift, n_out):
    n = win.shape[0]
    if shift % n == 0:
        return win[:n_out]
    return pltpu.roll(win, n - shift, axis=0)[:n_out]


def _row_spec(width):
    return pl.BlockSpec((1, width), lambda i: (0, 0))


def rmsnorm_fwd(x, g, name):
    s, d = x.shape

    def body(x_ref, g_ref, o_ref):
        xhat, _ = _rms_stats(x_ref[...])
        o_ref[...] = (xhat * g_ref[...]).astype(BF16)

    tile = pl.BlockSpec((NORM_TILE, d), lambda i: (i, 0))
    return pl.pallas_call(
        body, name=name, out_shape=jax.ShapeDtypeStruct((s, d), BF16), grid=(s // NORM_TILE,),
        in_specs=[tile, _row_spec(d)], out_specs=tile, compiler_params=_params(("parallel",)))(x, g)


def rmsnorm_bwd(x, g, dxn, dres, name):
    s, d = x.shape

    def body(x_ref, g_ref, dxn_ref, dres_ref, dx_ref, dg_ref):
        xhat, r = _rms_stats(x_ref[...])
        dxn_v = dxn_ref[...].astype(F32)
        dx_ref[...] = dres_ref[...] + _rms_bwd(dxn_v, xhat, r, g_ref[...])
        _accumulate(dg_ref, jnp.sum(dxn_v * xhat, axis=0, keepdims=True), pl.program_id(0) == 0)

    tile = pl.BlockSpec((NORM_TILE, d), lambda i: (i, 0))
    return pl.pallas_call(
        body, name=name, out_shape=(jax.ShapeDtypeStruct((s, d), F32), jax.ShapeDtypeStruct((1, d), F32)),
        grid=(s // NORM_TILE,), in_specs=[tile, _row_spec(d), tile, tile], out_specs=(tile, _row_spec(d)),
        compiler_params=_params(("arbitrary",)))(x, g, dxn, dres)


def loss_head(x, g, target, name):
    s, d = x.shape

    def body(x_ref, g_ref, t_ref, loss_ref, dx_ref, dg_ref):
        first = pl.program_id(0) == 0
        xhat, r = _rms_stats(x_ref[...])
        gv = g_ref[...]
        err = xhat * gv - t_ref[...]
        part = 0.5 * jnp.sum(jnp.sum(err * err, axis=-1, keepdims=True), axis=0, keepdims=True) / d
        _accumulate(loss_ref, jnp.broadcast_to(part, (1, LANES)), first)
        dy = err / d
        dx_ref[...] = _rms_bwd(dy, xhat, r, gv)
        _accumulate(dg_ref, jnp.sum(dy * xhat, axis=0, keepdims=True), first)

    tile = pl.BlockSpec((NORM_TILE, d), lambda i: (i, 0))
    return pl.pallas_call(
        body, name=name,
        out_shape=(jax.ShapeDtypeStruct((1, LANES), F32), jax.ShapeDtypeStruct((s, d), F32),
                   jax.ShapeDtypeStruct((1, d), F32)),
        grid=(s // NORM_TILE,), in_specs=[tile, _row_spec(d), tile],
        out_specs=(_row_spec(LANES), tile, _row_spec(d)), compiler_params=_params(("arbitrary",)))(x, g, target)


def _tril_mask():
    row = lax.broadcasted_iota(jnp.int32, (SG_CHUNK, SG_CHUNK), 0)
    col = lax.broadcasted_iota(jnp.int32, (SG_CHUNK, SG_CHUNK), 1)
    return row >= col


def _sg_specs():
    vec = _row_spec(SG_WIDTH)
    mat = pl.BlockSpec((SG_GROUPS, SG_CHUNK, SG_CHUNK), lambda i: (0, 0, 0))
    return vec, mat


def mixer_a_fwd(proj, ln_g, ln_b, w_s, b_s, name):
    s = proj.shape[0]
    chunks = TOKEN_TILE // SG_CHUNK

    def body(z_ref, lg_ref, lb_ref, w_ref, b_ref, o_ref):
        ge = _gelu(z_ref[...].astype(F32))
        u = ge[:, :SG_WIDTH]
        xhat, _ = _ln_stats(ge[:, SG_WIDTH:])
        vn = xhat * lg_ref[...] + lb_ref[...]
        tril = _tril_mask()
        for ci in range(chunks):
            rows = slice(ci * SG_CHUNK, (ci + 1) * SG_CHUNK)
            for g in range(SG_GROUPS):
                cols = slice(g * LANES, (g + 1) * LANES)
                wm = jnp.where(tril, w_ref[g], 0.0).astype(BF16)
                mixed = jnp.dot(wm, vn[rows, cols].astype(BF16), preferred_element_type=F32) + b_ref[g]
                o_ref[rows, cols] = (u[rows, cols] * mixed).astype(BF16)

    vec, mat = _sg_specs()
    return pl.pallas_call(
        body, name=name, out_shape=jax.ShapeDtypeStruct((s, SG_WIDTH), BF16), grid=(s // TOKEN_TILE,),
        in_specs=[pl.BlockSpec((TOKEN_TILE, 2 * SG_WIDTH), lambda i: (i, 0)), vec, vec, mat, mat],
        out_specs=pl.BlockSpec((TOKEN_TILE, SG_WIDTH), lambda i: (i, 0)),
        compiler_params=_params(("parallel",)))(proj, ln_g, ln_b, w_s, b_s)


def mixer_a_bwd(proj, dy, ln_g, ln_b, w_s, b_s, name):
    s = proj.shape[0]
    chunks = TOKEN_TILE // SG_CHUNK

    def body(z_ref, dy_ref, lg_ref, lb_ref, w_ref, b_ref, dz_ref, dlg_ref, dlb_ref, dw_ref, db_ref, du_scr, dvn_scr):
        first = pl.program_id(0) == 0

        @pl.when(first)
        def _():
            dw_ref[...] = jnp.zeros_like(dw_ref)
            db_ref[...] = jnp.zeros_like(db_ref)

        z = z_ref[...].astype(F32)
        ge = _gelu(z)
        u = ge[:, :SG_WIDTH]
        xhat, rstd = _ln_stats(ge[:, SG_WIDTH:])
        lg = lg_ref[...]
        vn = xhat * lg + lb_ref[...]
        dyv = dy_ref[...].astype(F32)
        tril = _tril_mask()
        for ci in range(chunks):
            rows = slice(ci * SG_CHUNK, (ci + 1) * SG_CHUNK)
            for g in range(SG_GROUPS):
                cols = slice(g * LANES, (g + 1) * LANES)
                wm = jnp.where(tril, w_ref[g], 0.0).astype(BF16)
                vg = vn[rows, cols].astype(BF16)
                mixed = jnp.dot(wm, vg, preferred_element_type=F32) + b_ref[g]
                dyb = dyv[rows, cols]
                du_scr[rows, cols] = dyb * mixed
                dmix = dyb * u[rows, cols]
                db_ref[g] += jnp.broadcast_to(jnp.sum(dmix, axis=1, keepdims=True), (SG_CHUNK, LANES))
                dmb = dmix.astype(BF16)
                dwg = lax.dot_general(dmb, vg, _DIMS["nt"], preferred_element_type=F32)
                dw_ref[g] += jnp.where(tril, dwg, 0.0)
                dvn_scr[rows, cols] = lax.dot_general(wm, dmb, _DIMS["tn"], preferred_element_type=F32)
        dvn = dvn_scr[...]
        _accumulate(dlg_ref, jnp.sum(dvn * xhat, axis=0, keepdims=True), first)
        _accumulate(dlb_ref, jnp.sum(dvn, axis=0, keepdims=True), first)
        dvv = _ln_bwd(dvn, xhat, rstd, lg)
        gg = _gelu_grad(z)
        dz_ref[:, :SG_WIDTH] = (du_scr[...] * gg[:, :SG_WIDTH]).astype(BF16)
        dz_ref[:, SG_WIDTH:] = (dvv * gg[:, SG_WIDTH:]).astype(BF16)

    vec, mat = _sg_specs()
    mat_shape = jax.ShapeDtypeStruct((SG_GROUPS, SG_CHUNK, SG_CHUNK), F32)
    vec_shape = jax.ShapeDtypeStruct((1, SG_WIDTH), F32)
    return pl.pallas_call(
        body, name=name,
        out_shape=(jax.ShapeDtypeStruct((s, 2 * SG_WIDTH), BF16), vec_shape, vec_shape, mat_shape, mat_shape),
        grid=(s // TOKEN_TILE,),
        in_specs=[pl.BlockSpec((TOKEN_TILE, 2 * SG_WIDTH), lambda i: (i, 0)),
                  pl.BlockSpec((TOKEN_TILE, SG_WIDTH), lambda i: (i, 0)), vec, vec, mat, mat],
        out_specs=(pl.BlockSpec((TOKEN_TILE, 2 * SG_WIDTH), lambda i: (i, 0)), vec, vec, mat, mat),
        scratch_shapes=[pltpu.VMEM((TOKEN_TILE, SG_WIDTH), F32), pltpu.VMEM((TOKEN_TILE, SG_WIDTH), F32)],
        compiler_params=_params(("arbitrary",)))(proj, dy, ln_g, ln_b, w_s, b_s)


_B_A_BLOCK = 1024 // LANES
_B_G_BLOCK = 1536 // LANES
_CH_TILES = CV_WIDTH // LANES


def _col_spec(s, first_block):
    return pl.BlockSpec((s, LANES), lambda j: (0, first_block + j))


def conv_b_fwd(proj, w_pad, bias, name):
    s = proj.shape[0]

    def body(a_ref, g_ref, w_ref, b_ref, c_ref, upad):
        upad[0:HALO, :] = jnp.zeros((HALO, LANES), F32)
        upad[HALO:, :] = a_ref[...].astype(F32) * _sigmoid(g_ref[...].astype(F32))
        w = w_ref[...]
        bv = b_ref[...]

        def block(bi, carry):
            start = pl.multiple_of(bi * CONV_ROWS, CONV_ROWS)
            win = upad[pl.ds(start, CONV_ROWS + HALO), :]
            acc = jnp.zeros((CONV_ROWS, LANES), F32)
            for k in range(CV_KERNEL):
                acc = acc + w[k:k + 1, :] * _shift_rows(win, HALO - (CV_KERNEL - 1) + k, CONV_ROWS)
            c_ref[pl.ds(start, CONV_ROWS), :] = acc + bv
            return carry

        lax.fori_loop(0, s // CONV_ROWS, block, 0)

    return pl.pallas_call(
        body, name=name, out_shape=jax.ShapeDtypeStruct((s, CV_WIDTH), F32), grid=(_CH_TILES,),
        in_specs=[_col_spec(s, _B_A_BLOCK), _col_spec(s, _B_G_BLOCK), _col_spec(HALO, 0), _col_spec(1, 0)],
        out_specs=_col_spec(s, 0), scratch_shapes=[pltpu.VMEM((s + HALO, LANES), F32)],
        compiler_params=_params(("parallel",)))(proj, proj, w_pad, bias)


def conv_b_bwd(proj, w_pad, dc, name):
    s = proj.shape[0]

    def body(a_ref, g_ref, w_ref, dc_ref, da_ref, dg_ref, dw_ref, db_ref, upad, dpad, dw_scr):
        upad[0:HALO, :] = jnp.zeros((HALO, LANES), F32)
        upad[HALO:, :] = a_ref[...].astype(F32) * _sigmoid(g_ref[...].astype(F32))
        dcv = dc_ref[...]
        dpad[0:s, :] = dcv
        dpad[s:, :] = jnp.zeros((HALO, LANES), F32)
        db_ref[...] = jnp.sum(dcv, axis=0, keepdims=True)
        dw_scr[...] = jnp.zeros((HALO, LANES), F32)
        w = w_ref[...]

        def block(bi, carry):
            start = pl.multiple_of(bi * CONV_ROWS, CONV_ROWS)
            uwin = upad[pl.ds(start, CONV_ROWS + HALO), :]
            dwin = dpad[pl.ds(start, CONV_ROWS + HALO), :]
            dcb = dwin[:CONV_ROWS]
            du = jnp.zeros((CONV_ROWS, LANES), F32)
            for k in range(CV_KERNEL):
                du = du + w[k:k + 1, :] * _shift_rows(dwin, CV_KERNEL - 1 - k, CONV_ROWS)
                ush = _shift_rows(uwin, HALO - (CV_KERNEL - 1) + k, CONV_ROWS)
                dw_scr[k:k + 1, :] += jnp.sum(dcb * ush, axis=0, keepdims=True)
            av = a_ref[pl.ds(start, CONV_ROWS), :].astype(F32)
            sg = _sigmoid(g_ref[pl.ds(start, CONV_ROWS), :].astype(F32))
            da_ref[pl.ds(start, CONV_ROWS), :] = (du * sg).astype(BF16)
            dg_ref[pl.ds(start, CONV_ROWS), :] = (du * av * sg * (1.0 - sg)).astype(BF16)
            return carry

        lax.fori_loop(0, s // CONV_ROWS, block, 0)
        dw_ref[...] = dw_scr[...]

    act = jax.ShapeDtypeStruct((s, CV_WIDTH), BF16)
    return pl.pallas_call(
        body, name=name,
        out_shape=(act, act, jax.ShapeDtypeStruct((HALO, CV_WIDTH), F32), jax.ShapeDtypeStruct((1, CV_WIDTH), F32)),
        grid=(_CH_TILES,),
        in_specs=[_col_spec(s, _B_A_BLOCK), _col_spec(s, _B_G_BLOCK), _col_spec(HALO, 0), _col_spec(s, 0)],
        out_specs=(_col_spec(s, 0), _col_spec(s, 0), _col_spec(HALO, 0), _col_spec(1, 0)),
        scratch_shapes=[pltpu.VMEM((s + HALO, LANES), F32), pltpu.VMEM((s + HALO, LANES), F32),
                        pltpu.VMEM((HALO, LANES), F32)],
        compiler_params=_params(("parallel",)))(proj, proj, w_pad, dc)


def ln_silu_fwd(c, ln_g, ln_b, name):
    s, d = c.shape

    def body(c_ref, g_ref, b_ref, o_ref):
        xhat, _ = _ln_stats(c_ref[...])
        cn = xhat * g_ref[...] + b_ref[...]
        o_ref[...] = (cn * _sigmoid(cn)).astype(BF16)

    tile = pl.BlockSpec((NORM_TILE, d), lambda i: (i, 0))
    return pl.pallas_call(
        body, name=name, out_shape=jax.ShapeDtypeStruct((s, d), BF16), grid=(s // NORM_TILE,),
        in_specs=[tile, _row_spec(d), _row_spec(d)], out_specs=tile,
        compiler_params=_params(("parallel",)))(c, ln_g, ln_b)


def ln_silu_bwd(c, dy, ln_g, ln_b, name, after=None):
    s, d = c.shape
    unread = () if after is None else (after,)

    def body(c_ref, dy_ref, g_ref, b_ref, *rest):
        dc_ref, dg_ref, db_ref = rest[-3:]
        first = pl.program_id(0) == 0
        xhat, rstd = _ln_stats(c_ref[...])
        gv = g_ref[...]
        cn = xhat * gv + b_ref[...]
        sg = _sigmoid(cn)
        dcn = dy_ref[...].astype(F32) * sg * (1.0 + cn * (1.0 - sg))
        _accumulate(dg_ref, jnp.sum(dcn * xhat, axis=0, keepdims=True), first)
        _accumulate(db_ref, jnp.sum(dcn, axis=0, keepdims=True), first)
        dc_ref[...] = _ln_bwd(dcn, xhat, rstd, gv)

    tile = pl.BlockSpec((NORM_TILE, d), lambda i: (i, 0))
    vec_shape = jax.ShapeDtypeStruct((1, d), F32)
    return pl.pallas_call(
        body, name=name, out_shape=(jax.ShapeDtypeStruct((s, d), F32), vec_shape, vec_shape),
        grid=(s // NORM_TILE,),
        in_specs=[tile, tile, _row_spec(d), _row_spec(d)] + [pl.BlockSpec(memory_space=pl.ANY)] * len(unread),
        out_specs=(tile, _row_spec(d), _row_spec(d)),
        compiler_params=_params(("arbitrary",)))(c, dy, ln_g, ln_b, *unread)


_D_BLOCK = 2816 // LANES


def _conv3(win, w):
    acc = jnp.zeros((CONV_ROWS, LANES), F32)
    for k in range(SC_KERNEL):
        acc = acc + w[k:k + 1, :] * _shift_rows(win, HALO - (SC_KERNEL - 1) + k, CONV_ROWS)
    return acc


def conv_d_fwd(proj, w_pad, name):
    s = proj.shape[0]

    def body(bg_ref, cg_ref, h_ref, w_ref, o_ref, ppad):
        ppad[0:HALO, :] = jnp.zeros((HALO, LANES), F32)
        ppad[HALO:, :] = cg_ref[...].astype(F32) * h_ref[...].astype(F32)
        w = w_ref[...]

        def block(bi, carry):
            start = pl.multiple_of(bi * CONV_ROWS, CONV_ROWS)
            cv = _conv3(ppad[pl.ds(start, CONV_ROWS + HALO), :], w)
            o_ref[pl.ds(start, CONV_ROWS), :] = (bg_ref[pl.ds(start, CONV_ROWS), :].astype(F32) * cv).astype(BF16)
            return carry

        lax.fori_loop(0, s // CONV_ROWS, block, 0)

    return pl.pallas_call(
        body, name=name, out_shape=jax.ShapeDtypeStruct((s, SC_WIDTH), BF16), grid=(_CH_TILES,),
        in_specs=[_col_spec(s, _D_BLOCK), _col_spec(s, _D_BLOCK + _CH_TILES), _col_spec(s, _D_BLOCK + 2 * _CH_TILES),
                  _col_spec(SUBLANES, 0)],
        out_specs=_col_spec(s, 0), scratch_shapes=[pltpu.VMEM((s + HALO, LANES), F32)],
        compiler_params=_params(("parallel",)))(proj, proj, proj, w_pad)


def conv_d_bwd(proj, w_pad, dy, name):
    s = proj.shape[0]

    def body(bg_ref, cg_ref, h_ref, w_ref, dy_ref, dbg_ref, dcg_ref, dh_ref, dw_ref, ppad, dpad, dw_scr):
        ppad[0:HALO, :] = jnp.zeros((HALO, LANES), F32)
        ppad[HALO:, :] = cg_ref[...].astype(F32) * h_ref[...].astype(F32)
        dpad[0:s, :] = dy_ref[...].astype(F32) * bg_ref[...].astype(F32)
        dpad[s:, :] = jnp.zeros((HALO, LANES), F32)
        dw_scr[...] = jnp.zeros((SUBLANES, LANES), F32)
        w = w_ref[...]

        def block(bi, carry):
            start = pl.multiple_of(bi * CONV_ROWS, CONV_ROWS)
            rows = pl.ds(start, CONV_ROWS)
            pwin = ppad[pl.ds(start, CONV_ROWS + HALO), :]
            dwin = dpad[pl.ds(start, CONV_ROWS + HALO), :]
            dcvb = dwin[:CONV_ROWS]
            dbg_ref[rows, :] = (dy_ref[rows, :].astype(F32) * _conv3(pwin, w)).astype(BF16)
            dp = jnp.zeros((CONV_ROWS, LANES), F32)
            for k in range(SC_KERNEL):
                dp = dp + w[k:k + 1, :] * _shift_rows(dwin, SC_KERNEL - 1 - k, CONV_ROWS)
                psh = _shift_rows(pwin, HALO - (SC_KERNEL - 1) + k, CONV_ROWS)
                dw_scr[k:k + 1, :] += jnp.sum(dcvb * psh, axis=0, keepdims=True)
            dcg_ref[rows, :] = (dp * h_ref[rows, :].astype(F32)).astype(BF16)
            dh_ref[rows, :] = (dp * cg_ref[rows, :].astype(F32)).astype(BF16)
            return carry

        lax.fori_loop(0, s // CONV_ROWS, block, 0)
        dw_ref[...] = dw_scr[...]

    act = jax.ShapeDtypeStruct((s, SC_WIDTH), BF16)
    return pl.pallas_call(
        body, name=name, out_shape=(act, act, act, jax.ShapeDtypeStruct((SUBLANES, SC_WIDTH), F32)),
        grid=(_CH_TILES,),
        in_specs=[_col_spec(s, _D_BLOCK), _col_spec(s, _D_BLOCK + _CH_TILES), _col_spec(s, _D_BLOCK + 2 * _CH_TILES),
                  _col_spec(SUBLANES, 0), _col_spec(s, 0)],
        out_specs=(_col_spec(s, 0), _col_spec(s, 0), _col_spec(s, 0), _col_spec(SUBLANES, 0)),
        scratch_shapes=[pltpu.VMEM((s + HALO, LANES), F32), pltpu.VMEM((s + HALO, LANES), F32),
                        pltpu.VMEM((SUBLANES, LANES), F32)],
        compiler_params=_params(("parallel",)))(proj, proj, proj, w_pad, dy)


_QK_BLOCK = 2048 // LANES
_QK_BLOCKS = (Q_WIDTH + KV_WIDTH) // LANES


def _swap_halves(t):
    lane = lax.broadcasted_iota(jnp.int32, t.shape, 1)
    low = (lane % HEAD_DIM) < (HEAD_DIM // 2)
    return jnp.where(low, pltpu.roll(t, LANES - HEAD_DIM // 2, axis=1), pltpu.roll(t, HEAD_DIM // 2, axis=1))


def rope_fwd(proj, cos_t, sin_t, name, after=None):
    s = proj.shape[0]

    def body(t_ref, c_ref, s_ref, *rest):
        t = t_ref[...].astype(F32)
        rest[-1][...] = (t * c_ref[...] + _swap_halves(t) * s_ref[...]).astype(BF16)

    tr = min(ROPE_TILE, s)
    tab = pl.BlockSpec((tr, LANES), lambda i, j: (i, 0))
    extra = () if after is None else (after,)
    return pl.pallas_call(
        body, name=name, out_shape=jax.ShapeDtypeStruct((s, Q_WIDTH + KV_WIDTH), BF16),
        grid=(s // tr, _QK_BLOCKS),
        in_specs=[pl.BlockSpec((tr, LANES), lambda i, j: (i, _QK_BLOCK + j)), tab, tab]
        + [pl.BlockSpec(memory_space=pl.ANY) for _ in extra],
        out_specs=pl.BlockSpec((tr, LANES), lambda i, j: (i, j)),
        compiler_params=_params(("parallel", "parallel")))(proj, cos_t, sin_t, *extra)


def rope_bwd(d_cur, d_prev, cos_t, sin_t, name):
    s, w = d_cur.shape

    def body(a_ref, b_ref, c_ref, s_ref, o_ref):
        d = a_ref[...] + b_ref[...]
        o_ref[...] = (d * c_ref[...] + _swap_halves(d) * s_ref[...]).astype(BF16)

    tr = min(ROPE_TILE, s)
    tab = pl.BlockSpec((tr, LANES), lambda i, j: (i, 0))
    blk = pl.BlockSpec((tr, LANES), lambda i, j: (i, j))
    return pl.pallas_call(
        body, name=name, out_shape=jax.ShapeDtypeStruct((s, w), BF16), grid=(s // tr, w // LANES),
        in_specs=[blk, blk, tab, tab], out_specs=blk,
        compiler_params=_params(("parallel", "parallel")))(d_cur, d_prev, cos_t, sin_t)


_GROUP = N_Q_HEADS // N_KV_HEADS
_NEG = -1e30


def _attn_specs():
    q_spec = pl.BlockSpec((_GROUP, WINDOW, HEAD_DIM), lambda h, n: (h, n, 0))
    cur = pl.BlockSpec((1, WINDOW, HEAD_DIM), lambda h, n: (h, n, 0))
    prev = pl.BlockSpec((1, WINDOW, HEAD_DIM), lambda h, n: (h, jnp.maximum(n - 1, 0), 0))
    sink = pl.BlockSpec((_GROUP, 1, LANES), lambda h, n: (h, 0, 0))
    return q_spec, cur, prev, sink


def _attn_valid(n):
    qi = lax.broadcasted_iota(jnp.int32, (WINDOW, 2 * WINDOW), 0)
    kj = lax.broadcasted_iota(jnp.int32, (WINDOW, 2 * WINDOW), 1)
    delta = qi + WINDOW - kj
    return (delta >= 0) & (delta < WINDOW) & ((kj >= WINDOW) | (n > 0))


def _attn_probs(q, kcat, valid, sink_row):
    sc = lax.dot_general(q, kcat, _DIMS["nt"], preferred_element_type=F32) * (HEAD_DIM ** -0.5)
    sc = jnp.where(valid, sc, _NEG)
    sink = jnp.max(sink_row, axis=-1, keepdims=True)
    m = jnp.maximum(jnp.max(sc, axis=-1, keepdims=True), sink)
    p = jnp.where(valid, jnp.exp(sc - m), 0.0)
    es = jnp.exp(sink - m)
    inv = 1.0 / (jnp.sum(p, axis=-1, keepdims=True) + es)
    return p * inv, es * inv


def attention_fwd(qh, kh, vh, sinks_b, name, after=None):
    s = qh.shape[1]
    unread = () if after is None else (after,)

    def body(q_ref, kc_ref, kp_ref, vc_ref, vp_ref, sk_ref, *rest):
        o_ref = rest[-1]
        valid = _attn_valid(pl.program_id(1))
        kcat = jnp.concatenate([kp_ref[0], kc_ref[0]], axis=0)
        vcat = jnp.concatenate([vp_ref[0], vc_ref[0]], axis=0)
        for g in range(_GROUP):
            probs, _ = _attn_probs(q_ref[g], kcat, valid, sk_ref[g])
            o_ref[g] = jnp.dot(probs.astype(BF16), vcat, preferred_element_type=F32).astype(BF16)

    q_spec, cur, prev, sink = _attn_specs()
    return pl.pallas_call(
        body, name=name, out_shape=jax.ShapeDtypeStruct(qh.shape, BF16), grid=(N_KV_HEADS, s // WINDOW),
        in_specs=[q_spec, cur, prev, cur, prev, sink] + [pl.BlockSpec(memory_space=pl.ANY)] * len(unread),
        out_specs=q_spec, compiler_params=_params(("parallel", "parallel")))(qh, kh, kh, vh, vh, sinks_b, *unread)


def attention_bwd(qh, kh, vh, sinks_b, doh, name):
    s = qh.shape[1]

    def body(q_ref, kc_ref, kp_ref, vc_ref, vp_ref, sk_ref, do_ref, dq_ref, dkc_ref, dkp_ref, dvc_ref, dvp_ref, ds_ref):
        n = pl.program_id(1)
        valid = _attn_valid(n)
        kcat = jnp.concatenate([kp_ref[0], kc_ref[0]], axis=0)
        vcat = jnp.concatenate([vp_ref[0], vc_ref[0]], axis=0)
        dk = jnp.zeros((2 * WINDOW, HEAD_DIM), F32)
        dv = jnp.zeros((2 * WINDOW, HEAD_DIM), F32)
        for g in range(_GROUP):
            q = q_ref[g]
            do = do_ref[g]
            probs, ps = _attn_probs(q, kcat, valid, sk_ref[g])
            dprobs = lax.dot_general(do, vcat, _DIMS["nt"], preferred_element_type=F32)
            dv = dv + lax.dot_general(probs.astype(BF16), do, _DIMS["tn"], preferred_element_type=F32)
            rs = jnp.sum(probs * dprobs, axis=-1, keepdims=True)
            dsb = (probs * (dprobs - rs) * (HEAD_DIM ** -0.5)).astype(BF16)
            dq_ref[g] = jnp.dot(dsb, kcat, preferred_element_type=F32)
            dk = dk + lax.dot_general(dsb, q, _DIMS["tn"], preferred_element_type=F32)
            dsink = jnp.broadcast_to(-jnp.sum(ps * rs, axis=0, keepdims=True), (1, LANES))

            @pl.when(n == 0)
            def _():
                ds_ref[g] = dsink

            @pl.when(n > 0)
            def _():
                ds_ref[g] += dsink

        dkp_ref[0] = dk[:WINDOW]
        dkc_ref[0] = dk[WINDOW:]
        dvp_ref[0] = dv[:WINDOW]
        dvc_ref[0] = dv[WINDOW:]

    q_spec, cur, prev, sink = _attn_specs()
    kv_shape = jax.ShapeDtypeStruct(kh.shape, F32)
    return pl.pallas_call(
        body, name=name,
        out_shape=(jax.ShapeDtypeStruct(qh.shape, F32), kv_shape, kv_shape, kv_shape, kv_shape,
                   jax.ShapeDtypeStruct(sinks_b.shape, F32)),
        grid=(N_KV_HEADS, s // WINDOW), in_specs=[q_spec, cur, prev, cur, prev, sink, q_spec],
        out_specs=(q_spec, cur, cur, cur, cur, sink),
        compiler_params=_params(("parallel", "arbitrary")))(qh, kh, kh, vh, vh, sinks_b, doh)


def _to_heads(t, heads):
    return t.reshape(t.shape[0], heads, HEAD_DIM).transpose(1, 0, 2)


def _from_heads(t):
    return t.transpose(1, 0, 2).reshape(t.shape[1], t.shape[0] * HEAD_DIM)


def _shift_window(t):
    return jnp.concatenate([t[:, WINDOW:], jnp.zeros_like(t[:, :WINDOW])], axis=1)


def merge_fwd(zg, branches, name):
    s = zg.shape[0]

    def body(zg_ref, b0, b1, b2, b3, o_ref):
        acc = jnp.zeros((TOKEN_TILE, D_MODEL), F32)
        for n, b_ref in enumerate((b0, b1, b2, b3)):
            gate = _sigmoid(zg_ref[:, n * D_MODEL:(n + 1) * D_MODEL].astype(F32))
            acc = acc + gate * b_ref[...].astype(F32)
        o_ref[...] = acc.astype(BF16)

    tile = pl.BlockSpec((TOKEN_TILE, D_MODEL), lambda i: (i, 0))
    wide = pl.BlockSpec((TOKEN_TILE, N_BRANCH * D_MODEL), lambda i: (i, 0))
    return pl.pallas_call(
        body, name=name, out_shape=jax.ShapeDtypeStruct((s, D_MODEL), BF16), grid=(s // TOKEN_TILE,),
        in_specs=[wide, tile, tile, tile, tile], out_specs=tile,
        compiler_params=_params(("parallel",)))(zg, *branches)


def merge_bwd(zg, branches, dm, name):
    s = zg.shape[0]

    def body(zg_ref, b0, b1, b2, b3, dm_ref, dzg_ref, d0, d1, d2, d3):
        dmv = dm_ref[...].astype(F32)
        for n, (b_ref, d_ref) in enumerate(((b0, d0), (b1, d1), (b2, d2), (b3, d3))):
            cols = slice(n * D_MODEL, (n + 1) * D_MODEL)
            gate = _sigmoid(zg_ref[:, cols].astype(F32))
            d_ref[...] = (gate * dmv).astype(BF16)
            dzg_ref[:, cols] = (dmv * b_ref[...].astype(F32) * gate * (1.0 - gate)).astype(BF16)

    tile = pl.BlockSpec((TOKEN_TILE, D_MODEL), lambda i: (i, 0))
    wide = pl.BlockSpec((TOKEN_TILE, N_BRANCH * D_MODEL), lambda i: (i, 0))
    act = jax.ShapeDtypeStruct((s, D_MODEL), BF16)
    return pl.pallas_call(
        body, name=name, out_shape=(jax.ShapeDtypeStruct((s, N_BRANCH * D_MODEL), BF16), act, act, act, act),
        grid=(s // TOKEN_TILE,), in_specs=[wide, tile, tile, tile, tile, tile],
        out_specs=(wide, tile, tile, tile, tile), compiler_params=_params(("parallel",)))(zg, *branches, dm)


def swiglu_fwd(gu, name):
    s = gu.shape[0]

    def body(g_ref, u_ref, o_ref):
        gate = g_ref[...].astype(F32)
        o_ref[...] = (gate * _sigmoid(gate) * u_ref[...].astype(F32)).astype(BF16)

    return pl.pallas_call(
        body, name=name, out_shape=jax.ShapeDtypeStruct((s, D_FF), BF16), grid=(s // TOKEN_TILE,),
        in_specs=[pl.BlockSpec((TOKEN_TILE, D_FF), lambda i: (i, 0)), pl.BlockSpec((TOKEN_TILE, D_FF), lambda i: (i, 1))],
        out_specs=pl.BlockSpec((TOKEN_TILE, D_FF), lambda i: (i, 0)), compiler_params=_params(("parallel",)))(gu, gu)


def swiglu_bwd(gu, dact, name):
    s = gu.shape[0]

    def body(g_ref, u_ref, da_ref, o_ref):
        gate = g_ref[...].astype(F32)
        sg = _sigmoid(gate)
        da = da_ref[...].astype(F32)
        o_ref[:, :D_FF] = (da * u_ref[...].astype(F32) * sg * (1.0 + gate * (1.0 - sg))).astype(BF16)
        o_ref[:, D_FF:] = (da * gate * sg).astype(BF16)

    half = pl.BlockSpec((TOKEN_TILE, D_FF), lambda i: (i, 0))
    return pl.pallas_call(
        body, name=name, out_shape=jax.ShapeDtypeStruct((s, 2 * D_FF), BF16), grid=(s // TOKEN_TILE,),
        in_specs=[half, pl.BlockSpec((TOKEN_TILE, D_FF), lambda i: (i, 1)), half],
        out_specs=pl.BlockSpec((TOKEN_TILE, 2 * D_FF), lambda i: (i, 0)),
        compiler_params=_params(("parallel",)))(gu, gu, dact)


ADAMW_BLOCK_BYTES = 1 << 20


def adamw(parts, w, m, v, name):
    n_parts, r, c = w.shape
    tr = _divisor_tile(r, max(SUBLANES, ADAMW_BLOCK_BYTES // (4 * c)), SUBLANES)
    tiles = r // tr

    def part_spec(j):
        return pl.BlockSpec((N_DEV, tr, c), lambda i: (0, jnp.clip(i - j * tiles, 0, tiles - 1), 0))

    def body(*refs):
        p_refs = refs[:n_parts]
        w_ref, m_ref, v_ref, g_ref, d_ref, nm_ref, nv_ref = refs[n_parts:]
        which = pl.program_id(0) // tiles
        g = None
        for j, p_ref in enumerate(p_refs):
            gj = p_ref[0].astype(F32)
            for i in range(1, N_DEV):
                gj = gj + p_ref[i].astype(F32)
            g = gj if g is None else jnp.where(which == j, gj, g)
        nm = ADAM_B1 * m_ref[0] + (1.0 - ADAM_B1) * g
        nv = ADAM_B2 * v_ref[0] + (1.0 - ADAM_B2) * (g * g)
        m_hat = nm / (1.0 - ADAM_B1 ** ADAM_STEP)
        v_hat = nv / (1.0 - ADAM_B2 ** ADAM_STEP)
        g_ref[0] = g
        d_ref[0] = -ADAM_LR * (m_hat / (jnp.sqrt(v_hat) + ADAM_EPS) + ADAM_WD * w_ref[0])
        nm_ref[0] = nm
        nv_ref[0] = nv

    tile = pl.BlockSpec((1, tr, c), lambda i: (i // tiles, i % tiles, 0))
    shape = jax.ShapeDtypeStruct(w.shape, F32)
    return pl.pallas_call(
        body, name=name, out_shape=(shape, shape, shape, shape), grid=(n_parts * tiles,),
        in_specs=[part_spec(j) for j in range(n_parts)] + [tile, tile, tile],
        out_specs=(tile, tile, tile, tile), compiler_params=_params(("parallel",)))(*parts, w, m, v)


_RELATIONS = [(a, b, e) for a in (0, 1) for b in (0, 1) for e in (0, 1)][1:]


_HBM_SPEC = pl.BlockSpec(memory_space=pltpu.HBM)
_SEM_SPEC = pl.BlockSpec(memory_space=pltpu.SEMAPHORE)
_ANY_SPEC = pl.BlockSpec(memory_space=pl.ANY)
_DATAFLOW = pltpu.SideEffectType.DATAFLOW_SIDE_EFFECTING


_OTHER_CHIPS = [(1, 0), (0, 1), (1, 1)]
_FIRST_LEVEL = [(0, 0, 1)] + [(a, b, 0) for a, b in _OTHER_CHIPS]


def _remote_copies(ins, lands, send_sems, recv_sems, scatter, relations):
    x, y, c = lax.axis_index("x"), lax.axis_index("y"), lax.axis_index("c")
    me = 4 * x + 2 * y + c
    copies = []
    for t in range(len(ins)):
        for k, (a, b, e) in enumerate(relations):
            px, py, pc = (x + a) % 2, (y + b) % 2, (c + e) % 2
            src = ins[t].at[4 * px + 2 * py + pc] if scatter[t] else ins[t]
            copies.append(pltpu.make_async_remote_copy(
                src_ref=src, dst_ref=lands[t].at[me], send_sem=send_sems.at[t * len(relations) + k],
                recv_sem=recv_sems.at[t * len(relations) + k],
                device_id=(px, py, pc), device_id_type=pl.DeviceIdType.MESH))
    return copies


def _forward_copies(lands, send_sems, recv_sems):
    x, y, c = lax.axis_index("x"), lax.axis_index("y"), lax.axis_index("c")
    copies = []
    for t in range(len(lands)):
        for k, (a, b) in enumerate(_OTHER_CHIPS):
            slot = lands[t].at[4 * ((x + a) % 2) + 2 * ((y + b) % 2) + c]
            copies.append(pltpu.make_async_remote_copy(
                src_ref=slot, dst_ref=slot, send_sem=send_sems.at[t * len(_OTHER_CHIPS) + k],
                recv_sem=recv_sems.at[t * len(_OTHER_CHIPS) + k],
                device_id=(x, y, 1 - c), device_id_type=pl.DeviceIdType.MESH))
    return copies


def _place_own(landed, own, me):
    return lax.dynamic_update_index_in_dim(landed, own, me, 0)


def exchange_start(arrays, scatter, name, after=None, relations=_RELATIONS):
    n = len(arrays)
    n_rel = len(relations)
    land_shapes = [a.shape if scatter[t] else (N_DEV,) + a.shape for t, a in enumerate(arrays)]
    zones = [lax.empty(s, a.dtype) for s, a in zip(land_shapes, arrays)]

    def body(*refs):
        ins, lands = refs[:n], refs[n:2 * n]
        send_sems, recv_sems = refs[-2 * n - 3], refs[-2 * n - 2]
        token = refs[-1]
        for cp in _remote_copies(ins, lands, send_sems, recv_sems, scatter, relations):
            cp.start()
        token[...] = jnp.zeros_like(token)

    sems = pltpu.SemaphoreType.DMA((n * n_rel,))
    out_shape = ((sems, sems) + tuple(pltpu.HBM(a.shape, a.dtype) for a in arrays)
                 + tuple(pltpu.HBM(s, a.dtype) for s, a in zip(land_shapes, arrays))
                 + (jax.ShapeDtypeStruct((SUBLANES, LANES), F32),))
    operands = [pltpu.with_memory_space_constraint(a, pltpu.HBM) for a in arrays]
    operands += [pltpu.with_memory_space_constraint(z, pltpu.HBM) for z in zones]
    in_specs = [_HBM_SPEC] * (2 * n)
    if after is not None:
        operands.append(after)
        in_specs.append(_ANY_SPEC)
    res = pl.pallas_call(
        body, name=name, out_shape=out_shape, in_specs=in_specs,
        out_specs=(_SEM_SPEC, _SEM_SPEC) + (_HBM_SPEC,) * (2 * n) + (pl.BlockSpec(memory_space=pltpu.VMEM),),
        input_output_aliases={i: 2 + i for i in range(2 * n)},
        compiler_params=pltpu.CompilerParams(has_side_effects=_DATAFLOW))(*operands)
    handle = (res[0], res[1], res[2:2 + n], res[2 + n:2 + 2 * n], tuple(scatter), relations)
    return handle, res[-1]


def exchange_wait(handle, after, name):
    send_sems, recv_sems, sources, lands, scatter, relations = handle
    n = len(sources)

    def body(*refs):
        ins, lzs = refs[:n], refs[n:2 * n]
        send_ref, recv_ref = refs[2 * n], refs[2 * n + 1]
        for cp in _remote_copies(ins, lzs, send_ref, recv_ref, scatter, relations):
            cp.wait_send()
            cp.wait_recv()

    out_shape = (tuple(pltpu.HBM(a.shape, a.dtype) for a in sources) + tuple(pltpu.HBM(a.shape, a.dtype) for a in lands))
    res = pl.pallas_call(
        body, name=name, out_shape=out_shape, in_specs=[_HBM_SPEC] * (2 * n) + [_SEM_SPEC, _SEM_SPEC, _ANY_SPEC],
        out_specs=(_HBM_SPEC,) * (2 * n), input_output_aliases={i: i for i in range(2 * n)},
        compiler_params=pltpu.CompilerParams(has_side_effects=_DATAFLOW))(*sources, *lands, send_sems, recv_sems, after)
    return res[n:], res[:n]


def forward_start(lands, name):
    n = len(lands)

    def body(*refs):
        send_sems, recv_sems, token = refs[n], refs[n + 1], refs[-1]
        for cp in _forward_copies(refs[:n], send_sems, recv_sems):
            cp.start()
        token[...] = jnp.zeros_like(token)

    sems = pltpu.SemaphoreType.DMA((n * len(_OTHER_CHIPS),))
    res = pl.pallas_call(
        body, name=name,
        out_shape=(sems, sems) + tuple(pltpu.HBM(a.shape, a.dtype) for a in lands)
        + (jax.ShapeDtypeStruct((SUBLANES, LANES), F32),),
        in_specs=[_HBM_SPEC] * n,
        out_specs=(_SEM_SPEC, _SEM_SPEC) + (_HBM_SPEC,) * n + (pl.BlockSpec(memory_space=pltpu.VMEM),),
        input_output_aliases={i: 2 + i for i in range(n)},
        compiler_params=pltpu.CompilerParams(has_side_effects=_DATAFLOW))(*lands)
    return (res[0], res[1], res[2:2 + n]), res[-1]


def forward_wait(handle, after, name):
    send_sems, recv_sems, lands = handle
    n = len(lands)

    def body(*refs):
        for cp in _forward_copies(refs[:n], refs[n], refs[n + 1]):
            cp.wait_send()
            cp.wait_recv()

    return pl.pallas_call(
        body, name=name, out_shape=tuple(pltpu.HBM(a.shape, a.dtype) for a in lands),
        in_specs=[_HBM_SPEC] * n + [_SEM_SPEC, _SEM_SPEC, _ANY_SPEC], out_specs=(_HBM_SPEC,) * n,
        input_output_aliases={i: i for i in range(n)},
        compiler_params=pltpu.CompilerParams(has_side_effects=_DATAFLOW))(*lands, send_sems, recv_sems, after)


_SMALL = ("norm_mix", "sg_ln_g", "sg_ln_b", "sg_b", "cv_b", "cv_ln_g", "cv_ln_b", "attn_sinks", "norm_ffn",
          "norm_final")
_SMALL_LATE = ("norm_mix",)
_PACK_UNIT = SUBLANES * LANES


def _pack(entries):
    flat = []
    for entry in entries:
        parts = [t.reshape(-1) for t in (entry if isinstance(entry, (list, tuple)) else [entry])]
        size = sum(t.shape[0] for t in parts)
        flat += parts + [jnp.zeros(((-size) % _PACK_UNIT,), parts[0].dtype)]
    return jnp.concatenate(flat).reshape(-1, LANES)


def _unpack(packed, like):
    out, row = [], 0
    for t in like:
        size = 1
        for d in t.shape:
            size *= d
        rows = -(-size // _PACK_UNIT) * SUBLANES
        out.append(packed[row:row + rows].reshape(-1)[:size].reshape(t.shape))
        row += rows
    return out


def _layer_fwd(l, x, p, late_params, mid_hook=None, ffn_hook=None):
    tag = f"l{l}_"
    xn = rmsnorm_fwd(x, p["norm_mix"], tag + "norm_mix")
    proj = matmul(xn, p["w_in_t"], "nt", BF16, tag + "proj_a", tm_cap=1024, tn_cap=2176, b_rows=PROJ_A)
    zg = matmul(xn, p["w_in_t_g"], "nt", BF16, tag + "proj_g", tm_cap=1024, tn_cap=2048)
    y_a = mixer_a_fwd(proj, p["sg_ln_g"], p["sg_ln_b"], p["sg_w"], p["sg_b"], tag + "mix_a")
    conv = conv_b_fwd(proj, p["cv_w"], p["cv_b"], tag + "conv_b")
    y_b = ln_silu_fwd(conv, p["cv_ln_g"], p["cv_ln_b"], tag + "ln_silu")
    y_d = conv_d_fwd(proj, p["sc_w"], tag + "conv_d")
    qk = rope_fwd(proj, p["cos"], p["sin"], tag + "rope")
    qh = _to_heads(qk[:, :Q_WIDTH], N_Q_HEADS)
    kh = _to_heads(qk[:, Q_WIDTH:], N_KV_HEADS)
    vh = _to_heads(proj[:, 2688:2816], N_KV_HEADS)
    token = mid_hook(qh) if mid_hook is not None else None
    oh = attention_fwd(qh, kh, vh, p["sinks"], tag + "attn", after=token)
    y_c = _from_heads(oh)
    ys = (y_a, y_b, y_c, y_d)
    p = {**p, **late_params(y_c)}
    branches = tuple(matmul(ys[n], p["w_branch"][n], "nn", BF16, tag + f"branch{n}", tm_cap=1024, tn_cap=1024)
                     for n in range(N_BRANCH))
    merged = merge_fwd(zg, branches, tag + "merge")
    x_mid = matmul(merged, p["w_out"], "nn", F32, tag + "out", add=x, tm_cap=1024, tn_cap=1024)
    token = ffn_hook(x_mid) if ffn_hook is not None else None
    hn = rmsnorm_fwd(x_mid, p["norm_ffn"], tag + "norm_ffn")
    gu = matmul(hn, p["w_gate_up_t"], "nt", BF16, tag + "gate_up", tm_cap=512, tn_cap=2816, after=token)
    act = swiglu_fwd(gu, tag + "swiglu")
    x_out = matmul(act, p["w_down"], "nn", F32, tag + "down", add=x_mid, tm_cap=512, tn_cap=1024)
    saved = dict(x=x, xn=xn, proj=proj, zg=zg, conv=conv, qh=qh, kh=kh, vh=vh, ys=ys, branches=branches,
                 merged=merged, x_mid=x_mid, hn=hn, gu=gu, act=act)
    return x_out, saved, p


def _layer_bwd(l, dx_out, p, sv, emit, after=None, before_last=None):
    tag = f"l{l}_b_"
    g = {}
    dact = matmul(dx_out, p["w_down"], "nt", BF16, tag + "dact", after=after, tm_cap=512, tn_cap=2816)
    dw_down = matmul(sv["act"], dx_out, "tn", BF16, tag + "dw_down", tm_cap=1408, tn_cap=512)
    dgu = swiglu_bwd(sv["gu"], dact, tag + "swiglu")
    dhn = matmul(dgu, p["w_gate_up_t"], "nn", BF16, tag + "dhn", tm_cap=512, tn_cap=512)
    dw_gate_up = matmul(dgu, sv["hn"], "tn", BF16, tag + "dw_gate_up", tm_cap=1408, tn_cap=1024)
    token = emit("a", {"w_gate_up": dw_gate_up, "w_down": dw_down})
    dx_mid, g["norm_ffn"] = rmsnorm_bwd(sv["x_mid"], p["norm_ffn"], dhn, dx_out, tag + "norm_ffn")
    dmerged = matmul(dx_mid, p["w_out"], "nt", BF16, tag + "dmerged", after=token, tm_cap=1024, tn_cap=1024)
    dw_out = matmul(sv["merged"], dx_mid, "tn", BF16, tag + "dw_out", tm_cap=1024, tn_cap=512)
    dzg, *dbranches = merge_bwd(sv["zg"], sv["branches"], dmerged, tag + "merge")
    dys = [matmul(dbranches[n], p["w_branch"][n], "nt", BF16, tag + f"dy{n}", tm_cap=1024, tn_cap=512)
           for n in range(N_BRANCH)]
    dw_branch = jnp.stack(
        [matmul(sv["ys"][n], dbranches[n], "tn", BF16, tag + f"dw_branch{n}", tm_cap=512, tn_cap=1024)
         for n in range(N_BRANCH)])
    proj = sv["proj"]
    dz_a, g["sg_ln_g"], g["sg_ln_b"], dsw_a, dsb = mixer_a_bwd(
        proj, dys[0], p["sg_ln_g"], p["sg_ln_b"], p["sg_w"], p["sg_b"], tag + "mix_a")
    g["sg_b"] = dsb[:, :, 0]
    token = emit("b", {"w_branch": dw_branch, "w_out": dw_out}, {"sg_w": dsw_a})
    dconv, g["cv_ln_g"], g["cv_ln_b"] = ln_silu_bwd(sv["conv"], dys[1], p["cv_ln_g"], p["cv_ln_b"], tag + "ln_silu",
                                                    after=token)
    da, dgate, dcw, g["cv_b"] = conv_b_bwd(proj, p["cv_w"], dconv, tag + "conv_b")
    g["cv_w"] = dcw[:CV_KERNEL]
    doh = _to_heads(dys[2], N_Q_HEADS)
    dqh, dkc, dkp, dvc, dvp, dsk = attention_bwd(sv["qh"], sv["kh"], sv["vh"], p["sinks"], doh, tag + "attn")
    g["attn_sinks"] = dsk[:, 0, 0]
    dqk_cur = jnp.concatenate([_from_heads(dqh), _from_heads(dkc)], axis=1)
    dqk_prev = jnp.concatenate([jnp.zeros((SEQ, Q_WIDTH), F32), _from_heads(_shift_window(dkp))], axis=1)
    dqk = rope_bwd(dqk_cur, dqk_prev, p["cos"], -p["sin"], tag + "rope")
    dv = (_from_heads(dvc) + _from_heads(_shift_window(dvp))).astype(BF16)
    dbg, dcg, dh, dsw = conv_d_bwd(proj, p["sc_w"], dys[3], tag + "conv_d")
    g["sc_w"] = dsw[:SC_KERNEL]
    dproj = jnp.concatenate([dz_a, da, dgate, dqk, dv, dbg, dcg, dh], axis=1)
    if before_last is not None:
        token = before_last(g)
    dw_in = matmul(dproj, sv["xn"], "tn", BF16, tag + "dw_in_a", after=token, tm_cap=2176, tn_cap=512,
                   out_rows=PROJ_WIDTH)
    g["w_in"] = matmul(dzg, sv["xn"], "tn", BF16, tag + "dw_in_g", tm_cap=256, tn_cap=1024, into=dw_in,
                       into_row=PROJ_A)
    token = emit("c", {n: g.pop(n) for n in _EARLY})
    dxn = matmul(dproj, p["w_in_t"], "nn", F32, tag + "dxn_a", after=token, tm_cap=512, tn_cap=512, b_rows=PROJ_A)
    dxn = matmul(dzg, p["w_in_t_g"], "nn", F32, tag + "dxn_g", add=dxn, tm_cap=512, tn_cap=512)
    dx_in, g["norm_mix"] = rmsnorm_bwd(sv["x"], p["norm_mix"], dxn, dx_mid, tag + "norm_mix")
    return dx_in, g, token


_EARLY = ("w_in", "cv_w", "sc_w")
_LATE = ("w_branch", "w_out", "w_gate_up", "w_down")


_TRANSPOSED = ("w_in", "w_gate_up")


def _shard_view(name, t):
    return jnp.swapaxes(t, 1, 2) if name in _TRANSPOSED else t


def _full_weight(name, t):
    if name in ("w_out", "w_down") + _TRANSPOSED:
        return t.reshape(-1, t.shape[-1])
    if name == "w_branch":
        return t.transpose(1, 2, 0, 3).reshape(N_BRANCH, SG_WIDTH, D_MODEL)
    return t.transpose(1, 0, 2).reshape(t.shape[1], -1)


def _to_blocks(name, full):
    if name in ("w_out", "w_down") + _TRANSPOSED:
        return full.reshape(N_DEV, -1, full.shape[-1])
    if name == "w_branch":
        return full.reshape(N_BRANCH, SG_WIDTH, N_DEV, -1).transpose(2, 0, 1, 3)
    return full.reshape(full.shape[0], N_DEV, -1).transpose(1, 0, 2)


def _rope_tables():
    pos = jnp.arange(SEQ, dtype=F32)
    inv_freq = 1.0 / (ROPE_THETA ** (jnp.arange(0, HEAD_DIM, 2, dtype=F32) / HEAD_DIM))
    ang = pos[:, None] * inv_freq[None, :]
    cos, sin = jnp.cos(ang), jnp.sin(ang)
    reps = LANES // HEAD_DIM
    return jnp.tile(jnp.concatenate([cos, cos], axis=1), (1, reps)), jnp.tile(jnp.concatenate([-sin, sin], axis=1), (1, reps))


def kernel(x, norm_mix, w_in, sg_ln_g, sg_ln_b, sg_w, sg_b, cv_w, cv_b, cv_ln_g, cv_ln_b, attn_sinks, sc_w, w_branch, w_out, norm_ffn, w_gate_up, w_down, norm_final, loss_target, m_norm_mix, m_w_in, m_sg_ln_g, m_sg_ln_b, m_sg_w, m_sg_b, m_cv_w, m_cv_b, m_cv_ln_g, m_cv_ln_b, m_attn_sinks, m_sc_w, m_w_branch, m_w_out, m_norm_ffn, m_w_gate_up, m_w_down, m_norm_final, v_norm_mix, v_w_in, v_sg_ln_g, v_sg_ln_b, v_sg_w, v_sg_b, v_cv_w, v_cv_b, v_cv_ln_g, v_cv_ln_b, v_attn_sinks, v_sc_w, v_w_branch, v_w_out, v_norm_ffn, v_w_gate_up, v_w_down, v_norm_final):
    names = ("norm_mix", "w_in", "sg_ln_g", "sg_ln_b", "sg_w", "sg_b", "cv_w", "cv_b", "cv_ln_g", "cv_ln_b",
             "attn_sinks", "sc_w", "w_branch", "w_out", "norm_ffn", "w_gate_up", "w_down", "norm_final")
    w = dict(zip(names, (norm_mix, w_in, sg_ln_g, sg_ln_b, sg_w, sg_b, cv_w, cv_b, cv_ln_g, cv_ln_b, attn_sinks,
                         sc_w, w_branch, w_out, norm_ffn, w_gate_up, w_down, norm_final)))
    m = dict(zip(names, (m_norm_mix, m_w_in, m_sg_ln_g, m_sg_ln_b, m_sg_w, m_sg_b, m_cv_w, m_cv_b, m_cv_ln_g,
                         m_cv_ln_b, m_attn_sinks, m_sc_w, m_w_branch, m_w_out, m_norm_ffn, m_w_gate_up, m_w_down,
                         m_norm_final)))
    v = dict(zip(names, (v_norm_mix, v_w_in, v_sg_ln_g, v_sg_ln_b, v_sg_w, v_sg_b, v_cv_w, v_cv_b, v_cv_ln_g,
                         v_cv_ln_b, v_attn_sinks, v_sc_w, v_w_branch, v_w_out, v_norm_ffn, v_w_gate_up, v_w_down,
                         v_norm_final)))

    me = 4 * lax.axis_index("x") + 2 * lax.axis_index("y") + lax.axis_index("c")
    groups = [(l, group) for l in range(DEPTH) for group in (_EARLY, _LATE)]
    shards = {(l, group): [_shard_view(n, w[n])[l].astype(BF16) for n in group] for l, group in groups}
    gathers, forwards, own_shards, token = {}, {}, {}, None
    for l, group in groups:
        gathers[(l, group)], token = exchange_start(
            shards[(l, group)], [False] * len(group), f"gather_start{l}_{group[0]}", after=token,
            relations=_FIRST_LEVEL)

    def begin_forward(l, group, after):
        landed, own_shards[(l, group)] = exchange_wait(gathers[(l, group)], after, f"gather_wait{l}_{group[0]}")
        forwards[(l, group)], tok = forward_start(landed, f"forward_start{l}_{group[0]}")
        return tok

    def landed_weights(l, group, after):
        landed = forward_wait(forwards[(l, group)], after, f"forward_wait{l}_{group[0]}")
        return {n + "_t" if n in _TRANSPOSED else n: _full_weight(n, _place_own(t, own, me))
                for n, t, own in zip(group, landed, own_shards[(l, group)])}

    cos_t, sin_t = _rope_tables()

    def early_params(l, after):
        full = landed_weights(l, _EARLY, after)
        return dict(
            norm_mix=w["norm_mix"][l][None], norm_ffn=w["norm_ffn"][l][None],
            w_in_t=full["w_in_t"], w_in_t_g=full["w_in_t"][PROJ_A:],
            sg_ln_g=w["sg_ln_g"][l][None], sg_ln_b=w["sg_ln_b"][l][None], sg_w=w["sg_w"][l],
            sg_b=jnp.broadcast_to(w["sg_b"][l][:, :, None], (SG_GROUPS, SG_CHUNK, LANES)),
            cv_w=jnp.pad(full["cv_w"].astype(F32), ((0, HALO - CV_KERNEL), (0, 0))),
            cv_b=w["cv_b"][l][None], cv_ln_g=w["cv_ln_g"][l][None], cv_ln_b=w["cv_ln_b"][l][None],
            sinks=jnp.broadcast_to(w["attn_sinks"][l][:, None, None], (N_Q_HEADS, 1, LANES)),
            sc_w=jnp.pad(full["sc_w"].astype(F32), ((0, SUBLANES - SC_KERNEL), (0, 0))),
            cos=cos_t, sin=sin_t)

    params, saved = [None] * DEPTH, [None] * DEPTH
    h = x[0]
    after = begin_forward(0, _EARLY, token)
    for l in range(DEPTH):
        h, saved[l], params[l] = _layer_fwd(
            l, h, early_params(l, after), lambda behind, l=l: landed_weights(l, _LATE, behind),
            mid_hook=lambda behind, l=l: begin_forward(l, _LATE, behind),
            ffn_hook=(lambda behind, l=l: begin_forward(l + 1, _EARLY, behind)) if l + 1 < DEPTH else None)
        after = h
    loss_row, dh, d_norm_final = loss_head(h, w["norm_final"][None], loss_target[0], "loss_head")

    sent = {}

    def emitter(l):
        def emit(group, grads_of, replicated=None):
            replicated = replicated or {}
            send = [_to_blocks(n, grads_of[n].astype(BF16)) for n in grads_of] + list(replicated.values())
            flags = [True] * len(grads_of) + [False] * len(replicated)
            handle, tok = exchange_start(send, flags, f"grads_start{l}{group}")
            sent[(l, group)] = (handle, tuple(grads_of) + tuple(replicated), flags)
            return tok
        return emit

    small_sent = {}
    no_state = jnp.zeros((1,), F32)
    small_early = tuple(n for n in _SMALL if n not in _SMALL_LATE) + ("loss",)

    def start_small(tag, tag_names, entries, after):
        small_sent[tag], tok = exchange_start([_pack(entries)], [False], f"grads_start_small_{tag}", after=after)
        small_sent[tag] = (small_sent[tag], tag_names)
        return tok

    grads = [None] * DEPTH

    def before_last(g0):
        by_name = {"norm_final": [d_norm_final], "loss": [loss_row[0, :1]]}
        return start_small("early", small_early, [by_name.get(n) or [g0[n], grads[1][n]] for n in small_early], None)

    token = None
    for l in reversed(range(DEPTH)):
        dh, grads[l], token = _layer_bwd(l, dh, params[l], saved[l], emitter(l), after=token,
                                         before_last=before_last if l == 0 else None)
    grad_x = dh[None]

    token = start_small("late", _SMALL_LATE, [[grads[l][n] for l in range(DEPTH)] for n in _SMALL_LATE], token)

    def received(l, group, after):
        handle, group_names, flags = sent[(l, group)]
        landed, sources = exchange_wait(handle, after, f"grads_wait{l}{group}")
        return {n: _place_own(t, lax.dynamic_index_in_dim(s, me, 0, keepdims=False) if scattered else s, me)
                for n, t, s, scattered in zip(group_names, landed, sources, flags)}

    out_g, out_d, out_m, out_v = {}, {}, {}, {}

    def update(n, by_layer):
        shape = _shard_view(n, w[n]).shape
        view = (DEPTH, w[n].size // (DEPTH * shape[-1]), shape[-1])
        parts = [t.reshape((N_DEV,) + view[1:]) for t in by_layer]
        res = adamw(parts, *[_shard_view(n, t).reshape(view) for t in (w[n], m[n], v[n])], "adamw_" + n)
        out_g[n], out_d[n], out_m[n], out_v[n] = (_shard_view(n, t.reshape(shape)) for t in res)
        return res[0]

    def finish_small(tag, after):
        handle, tag_names = small_sent[tag]
        landed, sources = exchange_wait(handle, after, f"grads_wait_small_{tag}")
        like, m_like, v_like = ([no_state if n == "loss" else t[n] for n in tag_names] for t in (w, m, v))
        res = adamw([_place_own(landed[0], sources[0], me)], _pack(like)[None], _pack(m_like)[None],
                    _pack(v_like)[None], f"adamw_small_{tag}")
        for store, packed in zip((out_g, out_d, out_m, out_v), res):
            for n, t in zip(tag_names, _unpack(packed[0], like)):
                store[n] = t
        return res[0]

    behind = token
    for group in ("a", "b", "c"):
        if group == "c":
            behind = finish_small("early", behind)
        r1 = received(1, group, behind)
        r0 = received(0, group, next(iter(r1.values())))
        for n in r0:
            behind = update(n, [r0[n], r1[n]])
    finish_small("late", behind)

    loss = out_g["loss"][0]
    return (loss, grad_x, *[out_g[n] for n in names], *[out_d[n] for n in names], *[out_m[n] for n in names],
            *[out_v[n] for n in names])
```

```python
import jax
import jax.numpy as jnp
from jax import lax
from jax.experimental import pallas as pl
from jax.experimental.pallas import tpu as pltpu

F32 = jnp.float32
BF16 = jnp.bfloat16

SEQ = 2048
D_MODEL = 1024
DEPTH = 2
SG_WIDTH = 512
SG_CHUNK = 128
SG_GROUPS = 4
CV_WIDTH = 512
CV_KERNEL = 31
HEAD_DIM = 64
N_Q_HEADS = 8
N_KV_HEADS = 2
Q_WIDTH = 512
KV_WIDTH = 128
WINDOW = 128
SC_WIDTH = 512
SC_KERNEL = 3
N_BRANCH = 4
D_FF = 2816
EPS = 1e-6
ROPE_THETA = 10000.0
PROJ_A = 4352
PROJ_WIDTH = 8448
N_DEV = 8

ADAM_LR = 0.001
ADAM_B1 = 0.9
ADAM_B2 = 0.999
ADAM_EPS = 1e-08
ADAM_WD = 0.01
ADAM_STEP = 10

LANES = 128
SUBLANES = 8
VMEM_LIMIT_BYTES = 48 * 1024 * 1024
HALO = 32
CONV_ROWS = 256
TOKEN_TILE = 256
NORM_TILE = 512
ROPE_TILE = 1024

_SQRT_HALF = 0.7071067811865476
_INV_SQRT_2PI = 0.3989422804014327


def _params(semantics=None):
    return pltpu.CompilerParams(dimension_semantics=semantics, vmem_limit_bytes=VMEM_LIMIT_BYTES)


def _divisor_tile(n, cap, unit):
    best = None
    for t in range(unit, min(n, cap) + 1, unit):
        if n % t == 0:
            best = t
    return best if best is not None else n


_DIMS = {"nn": (((1,), (0,)), ((), ())), "nt": (((1,), (1,)), ((), ())), "tn": (((0,), (0,)), ((), ()))}


def matmul(a, b, mode, out_dtype, name, add=None, tm_cap=512, tn_cap=512, after=None, b_rows=None, out_rows=None,
           into=None, into_row=0):
    if mode == "nn":
        (m, k), n = a.shape, b.shape[1]
        k = b_rows if b_rows is not None else k
    elif mode == "nt":
        (m, k), n = a.shape, (b_rows if b_rows is not None else b.shape[0])
    else:
        (k, m), n = a.shape, b.shape[1]
    tm = _divisor_tile(m, tm_cap, LANES)
    tn = _divisor_tile(n, tn_cap, LANES)
    row0 = into_row // tm
    assert row0 * tm == into_row
    a_spec = pl.BlockSpec((k, tm), lambda i, j: (0, i)) if mode == "tn" else pl.BlockSpec((tm, k), lambda i, j: (i, 0))
    b_spec = pl.BlockSpec((tn, k), lambda i, j: (j, 0)) if mode == "nt" else pl.BlockSpec((k, tn), lambda i, j: (0, j))
    o_spec = pl.BlockSpec((tm, tn), lambda i, j: (i + row0, j))
    dims = _DIMS[mode]

    def body(*refs):
        a_ref, b_ref = refs[0], refs[1]
        o_ref = refs[-1]
        acc = lax.dot_general(a_ref[...].astype(BF16), b_ref[...].astype(BF16), dims, preferred_element_type=F32)
        if add is not None:
            acc = acc + refs[2][...].astype(F32)
        o_ref[...] = acc.astype(out_dtype)

    unread = tuple(t for t in (after, into) if t is not None)
    operands = (a, b) + (() if add is None else (add,)) + unread
    in_specs = [a_spec, b_spec] + ([o_spec] if add is not None else [])
    in_specs += [pl.BlockSpec(memory_space=pl.ANY)] * len(unread)
    aliases = {len(operands) - 1: 0} if into is not None else {}
    return pl.pallas_call(
        body, name=name,
        out_shape=jax.ShapeDtypeStruct((into.shape[0] if into is not None else out_rows or m, n), out_dtype),
        grid=(m // tm, n // tn),
        in_specs=in_specs, out_specs=o_spec, input_output_aliases=aliases,
        compiler_params=_params(("parallel", "parallel")))(*operands)


def _sigmoid(x):
    return 1.0 / (1.0 + jnp.exp(-x))


def _gelu(x):
    return 0.5 * x * (1.0 + lax.erf(x * _SQRT_HALF))


def _gelu_grad(x):
    return 0.5 * (1.0 + lax.erf(x * _SQRT_HALF)) + x * _INV_SQRT_2PI * jnp.exp(-0.5 * x * x)


def _rms_stats(x):
    r = lax.rsqrt(jnp.mean(x * x, axis=-1, keepdims=True) + EPS)
    return x * r, r


def _rms_bwd(dxn, xhat, r, g):
    h = dxn * g
    return r * (h - xhat * jnp.mean(h * xhat, axis=-1, keepdims=True))


def _ln_stats(x):
    mu = jnp.mean(x, axis=-1, keepdims=True)
    xc = x - mu
    rstd = lax.rsqrt(jnp.mean(xc * xc, axis=-1, keepdims=True) + EPS)
    return xc * rstd, rstd


def _ln_bwd(dy, xhat, rstd, g):
    dxhat = dy * g
    return rstd * (dxhat - jnp.mean(dxhat, axis=-1, keepdims=True)
                   - xhat * jnp.mean(dxhat * xhat, axis=-1, keepdims=True))


def _accumulate(ref, value, first):
    @pl.when(first)
    def _():
        ref[...] = value

    @pl.when(jnp.logical_not(first))
    def _():
        ref[...] += value


def _shift_rows(win, shift, n_out):
    n = win.shape[0]
    if shift % n == 0:
        return win[:n_out]
    return pltpu.roll(win, n - shift, axis=0)[:n_out]


def _row_spec(width):
    return pl.BlockSpec((1, width), lambda i: (0, 0))


def rmsnorm_fwd(x, g, name):
    s, d = x.shape

    def body(x_ref, g_ref, o_ref):
        xhat, _ = _rms_stats(x_ref[...])
        o_ref[...] = (xhat * g_ref[...]).astype(BF16)

    tile = pl.BlockSpec((NORM_TILE, d), lambda i: (i, 0))
    return pl.pallas_call(
        body, name=name, out_shape=jax.ShapeDtypeStruct((s, d), BF16), grid=(s // NORM_TILE,),
        in_specs=[tile, _row_spec(d)], out_specs=tile, compiler_params=_params(("parallel",)))(x, g)


def rmsnorm_bwd(x, g, dxn, dres, name):
    s, d = x.shape

    def body(x_ref, g_ref, dxn_ref, dres_ref, dx_ref, dg_ref):
        xhat, r = _rms_stats(x_ref[...])
        dxn_v = dxn_ref[...].astype(F32)
        dx_ref[...] = dres_ref[...] + _rms_bwd(dxn_v, xhat, r, g_ref[...])
        _accumulate(dg_ref, jnp.sum(dxn_v * xhat, axis=0, keepdims=True), pl.program_id(0) == 0)

    tile = pl.BlockSpec((NORM_TILE, d), lambda i: (i, 0))
    return pl.pallas_call(
        body, name=name, out_shape=(jax.ShapeDtypeStruct((s, d), F32), jax.ShapeDtypeStruct((1, d), F32)),
        grid=(s // NORM_TILE,), in_specs=[tile, _row_spec(d), tile, tile], out_specs=(tile, _row_spec(d)),
        compiler_params=_params(("arbitrary",)))(x, g, dxn, dres)


def loss_head(x, g, target, name):
    s, d = x.shape

    def body(x_ref, g_ref, t_ref, loss_ref, dx_ref, dg_ref):
        first = pl.program_id(0) == 0
        xhat, r = _rms_stats(x_ref[...])
        gv = g_ref[...]
        err = xhat * gv - t_ref[...]
        part = 0.5 * jnp.sum(jnp.sum(err * err, axis=-1, keepdims=True), axis=0, keepdims=True) / d
        _accumulate(loss_ref, jnp.broadcast_to(part, (1, LANES)), first)
        dy = err / d
        dx_ref[...] = _rms_bwd(dy, xhat, r, gv)
        _accumulate(dg_ref, jnp.sum(dy * xhat, axis=0, keepdims=True), first)

    tile = pl.BlockSpec((NORM_TILE, d), lambda i: (i, 0))
    return pl.pallas_call(
        body, name=name,
        out_shape=(jax.ShapeDtypeStruct((1, LANES), F32), jax.ShapeDtypeStruct((s, d), F32),
                   jax.ShapeDtypeStruct((1, d), F32)),
        grid=(s // NORM_TILE,), in_specs=[tile, _row_spec(d), tile],
        out_specs=(_row_spec(LANES), tile, _row_spec(d)), compiler_params=_params(("arbitrary",)))(x, g, target)


def _tril_mask():
    row = lax.broadcasted_iota(jnp.int32, (SG_CHUNK, SG_CHUNK), 0)
    col = lax.broadcasted_iota(jnp.int32, (SG_CHUNK, SG_CHUNK), 1)
    return row >= col


def _sg_specs():
    vec = _row_spec(SG_WIDTH)
    mat = pl.BlockSpec((SG_GROUPS, SG_CHUNK, SG_CHUNK), lambda i: (0, 0, 0))
    return vec, mat


def mixer_a_fwd(proj, ln_g, ln_b, w_s, b_s, name):
    s = proj.shape[0]
    chunks = TOKEN_TILE // SG_CHUNK

    def body(z_ref, lg_ref, lb_ref, w_ref, b_ref, o_ref):
        ge = _gelu(z_ref[...].astype(F32))
        u = ge[:, :SG_WIDTH]
        xhat, _ = _ln_stats(ge[:, SG_WIDTH:])
        vn = xhat * lg_ref[...] + lb_ref[...]
        tril = _tril_mask()
        for ci in range(chunks):
            rows = slice(ci * SG_CHUNK, (ci + 1) * SG_CHUNK)
            for g in range(SG_GROUPS):
                cols = slice(g * LANES, (g + 1) * LANES)
                wm = jnp.where(tril, w_ref[g], 0.0).astype(BF16)
                mixed = jnp.dot(wm, vn[rows, cols].astype(BF16), preferred_element_type=F32) + b_ref[g]
                o_ref[rows, cols] = (u[rows, cols] * mixed).astype(BF16)

    vec, mat = _sg_specs()
    return pl.pallas_call(
        body, name=name, out_shape=jax.ShapeDtypeStruct((s, SG_WIDTH), BF16), grid=(s // TOKEN_TILE,),
        in_specs=[pl.BlockSpec((TOKEN_TILE, 2 * SG_WIDTH), lambda i: (i, 0)), vec, vec, mat, mat],
        out_specs=pl.BlockSpec((TOKEN_TILE, SG_WIDTH), lambda i: (i, 0)),
        compiler_params=_params(("parallel",)))(proj, ln_g, ln_b, w_s, b_s)


def mixer_a_bwd(proj, dy, ln_g, ln_b, w_s, b_s, name):
    s = proj.shape[0]
    chunks = TOKEN_TILE // SG_CHUNK

    def body(z_ref, dy_ref, lg_ref, lb_ref, w_ref, b_ref, dz_ref, dlg_ref, dlb_ref, dw_ref, db_ref, du_scr, dvn_scr):
        first = pl.program_id(0) == 0

        @pl.when(first)
        def _():
            dw_ref[...] = jnp.zeros_like(dw_ref)
            db_ref[...] = jnp.zeros_like(db_ref)

        z = z_ref[...].astype(F32)
        ge = _gelu(z)
        u = ge[:, :SG_WIDTH]
        xhat, rstd = _ln_stats(ge[:, SG_WIDTH:])
        lg = lg_ref[...]
        vn = xhat * lg + lb_ref[...]
        dyv = dy_ref[...].astype(F32)
        tril = _tril_mask()
        for ci in range(chunks):
            rows = slice(ci * SG_CHUNK, (ci + 1) * SG_CHUNK)
            for g in range(SG_GROUPS):
                cols = slice(g * LANES, (g + 1) * LANES)
                wm = jnp.where(tril, w_ref[g], 0.0).astype(BF16)
                vg = vn[rows, cols].astype(BF16)
                mixed = jnp.dot(wm, vg, preferred_element_type=F32) + b_ref[g]
                dyb = dyv[rows, cols]
                du_scr[rows, cols] = dyb * mixed
                dmix = dyb * u[rows, cols]
                db_ref[g] += jnp.broadcast_to(jnp.sum(dmix, axis=1, keepdims=True), (SG_CHUNK, LANES))
                dmb = dmix.astype(BF16)
                dwg = lax.dot_general(dmb, vg, _DIMS["nt"], preferred_element_type=F32)
                dw_ref[g] += jnp.where(tril, dwg, 0.0)
                dvn_scr[rows, cols] = lax.dot_general(wm, dmb, _DIMS["tn"], preferred_element_type=F32)
        dvn = dvn_scr[...]
        _accumulate(dlg_ref, jnp.sum(dvn * xhat, axis=0, keepdims=True), first)
        _accumulate(dlb_ref, jnp.sum(dvn, axis=0, keepdims=True), first)
        dvv = _ln_bwd(dvn, xhat, rstd, lg)
        gg = _gelu_grad(z)
        dz_ref[:, :SG_WIDTH] = (du_scr[...] * gg[:, :SG_WIDTH]).astype(BF16)
        dz_ref[:, SG_WIDTH:] = (dvv * gg[:, SG_WIDTH:]).astype(BF16)

    vec, mat = _sg_specs()
    mat_shape = jax.ShapeDtypeStruct((SG_GROUPS, SG_CHUNK, SG_CHUNK), F32)
    vec_shape = jax.ShapeDtypeStruct((1, SG_WIDTH), F32)
    return pl.pallas_call(
        body, name=name,
        out_shape=(jax.ShapeDtypeStruct((s, 2 * SG_WIDTH), BF16), vec_shape, vec_shape, mat_shape, mat_shape),
        grid=(s // TOKEN_TILE,),
        in_specs=[pl.BlockSpec((TOKEN_TILE, 2 * SG_WIDTH), lambda i: (i, 0)),
                  pl.BlockSpec((TOKEN_TILE, SG_WIDTH), lambda i: (i, 0)), vec, vec, mat, mat],
        out_specs=(pl.BlockSpec((TOKEN_TILE, 2 * SG_WIDTH), lambda i: (i, 0)), vec, vec, mat, mat),
        scratch_shapes=[pltpu.VMEM((TOKEN_TILE, SG_WIDTH), F32), pltpu.VMEM((TOKEN_TILE, SG_WIDTH), F32)],
        compiler_params=_params(("arbitrary",)))(proj, dy, ln_g, ln_b, w_s, b_s)


_B_A_BLOCK = 1024 // LANES
_B_G_BLOCK = 1536 // LANES
_CH_TILES = CV_WIDTH // LANES


def _col_spec(s, first_block):
    return pl.BlockSpec((s, LANES), lambda j: (0, first_block + j))


def conv_b_fwd(proj, w_pad, bias, name):
    s = proj.shape[0]

    def body(a_ref, g_ref, w_ref, b_ref, c_ref, upad):
        upad[0:HALO, :] = jnp.zeros((HALO, LANES), F32)
        upad[HALO:, :] = a_ref[...].astype(F32) * _sigmoid(g_ref[...].astype(F32))
        w = w_ref[...]
        bv = b_ref[...]

        def block(bi, carry):
            start = pl.multiple_of(bi * CONV_ROWS, CONV_ROWS)
            win = upad[pl.ds(start, CONV_ROWS + HALO), :]
            acc = jnp.zeros((CONV_ROWS, LANES), F32)
            for k in range(CV_KERNEL):
                acc = acc + w[k:k + 1, :] * _shift_rows(win, HALO - (CV_KERNEL - 1) + k, CONV_ROWS)
            c_ref[pl.ds(start, CONV_ROWS), :] = acc + bv
            return carry

        lax.fori_loop(0, s // CONV_ROWS, block, 0)

    return pl.pallas_call(
        body, name=name, out_shape=jax.ShapeDtypeStruct((s, CV_WIDTH), F32), grid=(_CH_TILES,),
        in_specs=[_col_spec(s, _B_A_BLOCK), _col_spec(s, _B_G_BLOCK), _col_spec(HALO, 0), _col_spec(1, 0)],
        out_specs=_col_spec(s, 0), scratch_shapes=[pltpu.VMEM((s + HALO, LANES), F32)],
        compiler_params=_params(("parallel",)))(proj, proj, w_pad, bias)


def conv_b_bwd(proj, w_pad, dc, name):
    s = proj.shape[0]

    def body(a_ref, g_ref, w_ref, dc_ref, da_ref, dg_ref, dw_ref, db_ref, upad, dpad, dw_scr):
        upad[0:HALO, :] = jnp.zeros((HALO, LANES), F32)
        upad[HALO:, :] = a_ref[...].astype(F32) * _sigmoid(g_ref[...].astype(F32))
        dcv = dc_ref[...]
        dpad[0:s, :] = dcv
        dpad[s:, :] = jnp.zeros((HALO, LANES), F32)
        db_ref[...] = jnp.sum(dcv, axis=0, keepdims=True)
        dw_scr[...] = jnp.zeros((HALO, LANES), F32)
        w = w_ref[...]

        def block(bi, carry):
            start = pl.multiple_of(bi * CONV_ROWS, CONV_ROWS)
            uwin = upad[pl.ds(start, CONV_ROWS + HALO), :]
            dwin = dpad[pl.ds(start, CONV_ROWS + HALO), :]
            dcb = dwin[:CONV_ROWS]
            du = jnp.zeros((CONV_ROWS, LANES), F32)
            for k in range(CV_KERNEL):
                du = du + w[k:k + 1, :] * _shift_rows(dwin, CV_KERNEL - 1 - k, CONV_ROWS)
                ush = _shift_rows(uwin, HALO - (CV_KERNEL - 1) + k, CONV_ROWS)
                dw_scr[k:k + 1, :] += jnp.sum(dcb * ush, axis=0, keepdims=True)
            av = a_ref[pl.ds(start, CONV_ROWS), :].astype(F32)
            sg = _sigmoid(g_ref[pl.ds(start, CONV_ROWS), :].astype(F32))
            da_ref[pl.ds(start, CONV_ROWS), :] = (du * sg).astype(BF16)
            dg_ref[pl.ds(start, CONV_ROWS), :] = (du * av * sg * (1.0 - sg)).astype(BF16)
            return carry

        lax.fori_loop(0, s // CONV_ROWS, block, 0)
        dw_ref[...] = dw_scr[...]

    act = jax.ShapeDtypeStruct((s, CV_WIDTH), BF16)
    return pl.pallas_call(
        body, name=name,
        out_shape=(act, act, jax.ShapeDtypeStruct((HALO, CV_WIDTH), F32), jax.ShapeDtypeStruct((1, CV_WIDTH), F32)),
        grid=(_CH_TILES,),
        in_specs=[_col_spec(s, _B_A_BLOCK), _col_spec(s, _B_G_BLOCK), _col_spec(HALO, 0), _col_spec(s, 0)],
        out_specs=(_col_spec(s, 0), _col_spec(s, 0), _col_spec(HALO, 0), _col_spec(1, 0)),
        scratch_shapes=[pltpu.VMEM((s + HALO, LANES), F32), pltpu.VMEM((s + HALO, LANES), F32),
                        pltpu.VMEM((HALO, LANES), F32)],
        compiler_params=_params(("parallel",)))(proj, proj, w_pad, dc)


def ln_silu_fwd(c, ln_g, ln_b, name):
    s, d = c.shape

    def body(c_ref, g_ref, b_ref, o_ref):
        xhat, _ = _ln_stats(c_ref[...])
        cn = xhat * g_ref[...] + b_ref[...]
        o_ref[...] = (cn * _sigmoid(cn)).astype(BF16)

    tile = pl.BlockSpec((NORM_TILE, d), lambda i: (i, 0))
    return pl.pallas_call(
        body, name=name, out_shape=jax.ShapeDtypeStruct((s, d), BF16), grid=(s // NORM_TILE,),
        in_specs=[tile, _row_spec(d), _row_spec(d)], out_specs=tile,
        compiler_params=_params(("parallel",)))(c, ln_g, ln_b)


def ln_silu_bwd(c, dy, ln_g, ln_b, name, after=None):
    s, d = c.shape
    unread = () if after is None else (after,)

    def body(c_ref, dy_ref, g_ref, b_ref, *rest):
        dc_ref, dg_ref, db_ref = rest[-3:]
        first = pl.program_id(0) == 0
        xhat, rstd = _ln_stats(c_ref[...])
        gv = g_ref[...]
        cn = xhat * gv + b_ref[...]
        sg = _sigmoid(cn)
        dcn = dy_ref[...].astype(F32) * sg * (1.0 + cn * (1.0 - sg))
        _accumulate(dg_ref, jnp.sum(dcn * xhat, axis=0, keepdims=True), first)
        _accumulate(db_ref, jnp.sum(dcn, axis=0, keepdims=True), first)
        dc_ref[...] = _ln_bwd(dcn, xhat, rstd, gv)

    tile = pl.BlockSpec((NORM_TILE, d), lambda i: (i, 0))
    vec_shape = jax.ShapeDtypeStruct((1, d), F32)
    return pl.pallas_call(
        body, name=name, out_shape=(jax.ShapeDtypeStruct((s, d), F32), vec_shape, vec_shape),
        grid=(s // NORM_TILE,),
        in_specs=[tile, tile, _row_spec(d), _row_spec(d)] + [pl.BlockSpec(memory_space=pl.ANY)] * len(unread),
        out_specs=(tile, _row_spec(d), _row_spec(d)),
        compiler_params=_params(("arbitrary",)))(c, dy, ln_g, ln_b, *unread)


_D_BLOCK = 2816 // LANES


def _conv3(win, w):
    acc = jnp.zeros((CONV_ROWS, LANES), F32)
    for k in range(SC_KERNEL):
        acc = acc + w[k:k + 1, :] * _shift_rows(win, HALO - (SC_KERNEL - 1) + k, CONV_ROWS)
    return acc


def conv_d_fwd(proj, w_pad, name):
    s = proj.shape[0]

    def body(bg_ref, cg_ref, h_ref, w_ref, o_ref, ppad):
        ppad[0:HALO, :] = jnp.zeros((HALO, LANES), F32)
        ppad[HALO:, :] = cg_ref[...].astype(F32) * h_ref[...].astype(F32)
        w = w_ref[...]

        def block(bi, carry):
            start = pl.multiple_of(bi * CONV_ROWS, CONV_ROWS)
            cv = _conv3(ppad[pl.ds(start, CONV_ROWS + HALO), :], w)
            o_ref[pl.ds(start, CONV_ROWS), :] = (bg_ref[pl.ds(start, CONV_ROWS), :].astype(F32) * cv).astype(BF16)
            return carry

        lax.fori_loop(0, s // CONV_ROWS, block, 0)

    return pl.pallas_call(
        body, name=name, out_shape=jax.ShapeDtypeStruct((s, SC_WIDTH), BF16), grid=(_CH_TILES,),
        in_specs=[_col_spec(s, _D_BLOCK), _col_spec(s, _D_BLOCK + _CH_TILES), _col_spec(s, _D_BLOCK + 2 * _CH_TILES),
                  _col_spec(SUBLANES, 0)],
        out_specs=_col_spec(s, 0), scratch_shapes=[pltpu.VMEM((s + HALO, LANES), F32)],
        compiler_params=_params(("parallel",)))(proj, proj, proj, w_pad)


def conv_d_bwd(proj, w_pad, dy, name):
    s = proj.shape[0]

    def body(bg_ref, cg_ref, h_ref, w_ref, dy_ref, dbg_ref, dcg_ref, dh_ref, dw_ref, ppad, dpad, dw_scr):
        ppad[0:HALO, :] = jnp.zeros((HALO, LANES), F32)
        ppad[HALO:, :] = cg_ref[...].astype(F32) * h_ref[...].astype(F32)
        dpad[0:s, :] = dy_ref[...].astype(F32) * bg_ref[...].astype(F32)
        dpad[s:, :] = jnp.zeros((HALO, LANES), F32)
        dw_scr[...] = jnp.zeros((SUBLANES, LANES), F32)
        w = w_ref[...]

        def block(bi, carry):
            start = pl.multiple_of(bi * CONV_ROWS, CONV_ROWS)
            rows = pl.ds(start, CONV_ROWS)
            pwin = ppad[pl.ds(start, CONV_ROWS + HALO), :]
            dwin = dpad[pl.ds(start, CONV_ROWS + HALO), :]
            dcvb = dwin[:CONV_ROWS]
            dbg_ref[rows, :] = (dy_ref[rows, :].astype(F32) * _conv3(pwin, w)).astype(BF16)
            dp = jnp.zeros((CONV_ROWS, LANES), F32)
            for k in range(SC_KERNEL):
                dp = dp + w[k:k + 1, :] * _shift_rows(dwin, SC_KERNEL - 1 - k, CONV_ROWS)
                psh = _shift_rows(pwin, HALO - (SC_KERNEL - 1) + k, CONV_ROWS)
                dw_scr[k:k + 1, :] += jnp.sum(dcvb * psh, axis=0, keepdims=True)
            dcg_ref[rows, :] = (dp * h_ref[rows, :].astype(F32)).astype(BF16)
            dh_ref[rows, :] = (dp * cg_ref[rows, :].astype(F32)).astype(BF16)
            return carry

        lax.fori_loop(0, s // CONV_ROWS, block, 0)
        dw_ref[...] = dw_scr[...]

    act = jax.ShapeDtypeStruct((s, SC_WIDTH), BF16)
    return pl.pallas_call(
        body, name=name, out_shape=(act, act, act, jax.ShapeDtypeStruct((SUBLANES, SC_WIDTH), F32)),
        grid=(_CH_TILES,),
        in_specs=[_col_spec(s, _D_BLOCK), _col_spec(s, _D_BLOCK + _CH_TILES), _col_spec(s, _D_BLOCK + 2 * _CH_TILES),
                  _col_spec(SUBLANES, 0), _col_spec(s, 0)],
        out_specs=(_col_spec(s, 0), _col_spec(s, 0), _col_spec(s, 0), _col_spec(SUBLANES, 0)),
        scratch_shapes=[pltpu.VMEM((s + HALO, LANES), F32), pltpu.VMEM((s + HALO, LANES), F32),
                        pltpu.VMEM((SUBLANES, LANES), F32)],
        compiler_params=_params(("parallel",)))(proj, proj, proj, w_pad, dy)


_QK_BLOCK = 2048 // LANES
_QK_BLOCKS = (Q_WIDTH + KV_WIDTH) // LANES


def _swap_halves(t):
    lane = lax.broadcasted_iota(jnp.int32, t.shape, 1)
    low = (lane % HEAD_DIM) < (HEAD_DIM // 2)
    return jnp.where(low, pltpu.roll(t, LANES - HEAD_DIM // 2, axis=1), pltpu.roll(t, HEAD_DIM // 2, axis=1))


def rope_fwd(proj, cos_t, sin_t, name, after=None):
    s = proj.shape[0]

    def body(t_ref, c_ref, s_ref, *rest):
        t = t_ref[...].astype(F32)
        rest[-1][...] = (t * c_ref[...] + _swap_halves(t) * s_ref[...]).astype(BF16)

    tr = min(ROPE_TILE, s)
    tab = pl.BlockSpec((tr, LANES), lambda i, j: (i, 0))
    extra = () if after is None else (after,)
    return pl.pallas_call(
        body, name=name, out_shape=jax.ShapeDtypeStruct((s, Q_WIDTH + KV_WIDTH), BF16),
        grid=(s // tr, _QK_BLOCKS),
        in_specs=[pl.BlockSpec((tr, LANES), lambda i, j: (i, _QK_BLOCK + j)), tab, tab]
        + [pl.BlockSpec(memory_space=pl.ANY) for _ in extra],
        out_specs=pl.BlockSpec((tr, LANES), lambda i, j: (i, j)),
        compiler_params=_params(("parallel", "parallel")))(proj, cos_t, sin_t, *extra)


def rope_bwd(d_cur, d_prev, cos_t, sin_t, name):
    s, w = d_cur.shape

    def body(a_ref, b_ref, c_ref, s_ref, o_ref):
        d = a_ref[...] + b_ref[...]
        o_ref[...] = (d * c_ref[...] + _swap_halves(d) * s_ref[...]).astype(BF16)

    tr = min(ROPE_TILE, s)
    tab = pl.BlockSpec((tr, LANES), lambda i, j: (i, 0))
    blk = pl.BlockSpec((tr, LANES), lambda i, j: (i, j))
    return pl.pallas_call(
        body, name=name, out_shape=jax.ShapeDtypeStruct((s, w), BF16), grid=(s // tr, w // LANES),
        in_specs=[blk, blk, tab, tab], out_specs=blk,
        compiler_params=_params(("parallel", "parallel")))(d_cur, d_prev, cos_t, sin_t)


_GROUP = N_Q_HEADS // N_KV_HEADS
_NEG = -1e30


def _attn_specs():
    q_spec = pl.BlockSpec((_GROUP, WINDOW, HEAD_DIM), lambda h, n: (h, n, 0))
    cur = pl.BlockSpec((1, WINDOW, HEAD_DIM), lambda h, n: (h, n, 0))
    prev = pl.BlockSpec((1, WINDOW, HEAD_DIM), lambda h, n: (h, jnp.maximum(n - 1, 0), 0))
    sink = pl.BlockSpec((_GROUP, 1, LANES), lambda h, n: (h, 0, 0))
    return q_spec, cur, prev, sink


_STACK = _GROUP * WINDOW


def _attn_valid(n):
    qi = lax.broadcasted_iota(jnp.int32, (_STACK, 2 * WINDOW), 0) % WINDOW
    kj = lax.broadcasted_iota(jnp.int32, (_STACK, 2 * WINDOW), 1)
    delta = qi + WINDOW - kj
    return (delta >= 0) & (delta < WINDOW) & ((kj >= WINDOW) | (n > 0))


def _stack_heads(ref):
    return jnp.concatenate([ref[g] for g in range(_GROUP)], axis=0)


def _sink_column(sk_ref):
    return jnp.concatenate(
        [jnp.broadcast_to(jnp.max(sk_ref[g], axis=-1, keepdims=True), (WINDOW, 1)) for g in range(_GROUP)], axis=0)


def _attn_probs(q, kcat, valid, sink):
    sc = lax.dot_general(q, kcat, _DIMS["nt"], preferred_element_type=F32) * (HEAD_DIM ** -0.5)
    sc = jnp.where(valid, sc, _NEG)
    m = jnp.maximum(jnp.max(sc, axis=-1, keepdims=True), sink)
    p = jnp.where(valid, jnp.exp(sc - m), 0.0)
    es = jnp.exp(sink - m)
    inv = 1.0 / (jnp.sum(p, axis=-1, keepdims=True) + es)
    return p * inv, es * inv


def attention_fwd(qh, kh, vh, sinks_b, name, after=None):
    s = qh.shape[1]
    unread = () if after is None else (after,)

    def body(q_ref, kc_ref, kp_ref, vc_ref, vp_ref, sk_ref, *rest):
        o_ref = rest[-1]
        valid = _attn_valid(pl.program_id(1))
        kcat = jnp.concatenate([kp_ref[0], kc_ref[0]], axis=0)
        vcat = jnp.concatenate([vp_ref[0], vc_ref[0]], axis=0)
        probs, _ = _attn_probs(_stack_heads(q_ref), kcat, valid, _sink_column(sk_ref))
        out = jnp.dot(probs.astype(BF16), vcat, preferred_element_type=F32)
        for g in range(_GROUP):
            o_ref[g] = out[g * WINDOW:(g + 1) * WINDOW].astype(BF16)

    q_spec, cur, prev, sink = _attn_specs()
    return pl.pallas_call(
        body, name=name, out_shape=jax.ShapeDtypeStruct(qh.shape, BF16), grid=(N_KV_HEADS, s // WINDOW),
        in_specs=[q_spec, cur, prev, cur, prev, sink] + [pl.BlockSpec(memory_space=pl.ANY)] * len(unread),
        out_specs=q_spec, compiler_params=_params(("parallel", "parallel")))(qh, kh, kh, vh, vh, sinks_b, *unread)


def attention_bwd(qh, kh, vh, sinks_b, doh, name):
    s = qh.shape[1]

    def body(q_ref, kc_ref, kp_ref, vc_ref, vp_ref, sk_ref, do_ref, dq_ref, dkc_ref, dkp_ref, dvc_ref, dvp_ref, ds_ref):
        n = pl.program_id(1)
        valid = _attn_valid(n)
        kcat = jnp.concatenate([kp_ref[0], kc_ref[0]], axis=0)
        vcat = jnp.concatenate([vp_ref[0], vc_ref[0]], axis=0)
        q = _stack_heads(q_ref)
        do = _stack_heads(do_ref)
        probs, ps = _attn_probs(q, kcat, valid, _sink_column(sk_ref))
        dprobs = lax.dot_general(do, vcat, _DIMS["nt"], preferred_element_type=F32)
        dv = lax.dot_general(probs.astype(BF16), do, _DIMS["tn"], preferred_element_type=F32)
        rs = jnp.sum(probs * dprobs, axis=-1, keepdims=True)
        dsb = (probs * (dprobs - rs) * (HEAD_DIM ** -0.5)).astype(BF16)
        dq = jnp.dot(dsb, kcat, preferred_element_type=F32)
        dk = lax.dot_general(dsb, q, _DIMS["tn"], preferred_element_type=F32)
        sink_term = ps * rs
        for g in range(_GROUP):
            rows = slice(g * WINDOW, (g + 1) * WINDOW)
            dq_ref[g] = dq[rows]
            dsink = jnp.broadcast_to(-jnp.sum(sink_term[rows], axis=0, keepdims=True), (1, LANES))

            @pl.when(n == 0)
            def _():
                ds_ref[g] = dsink

            @pl.when(n > 0)
            def _():
                ds_ref[g] += dsink

        dkp_ref[0] = dk[:WINDOW]
        dkc_ref[0] = dk[WINDOW:]
        dvp_ref[0] = dv[:WINDOW]
        dvc_ref[0] = dv[WINDOW:]

    q_spec, cur, prev, sink = _attn_specs()
    kv_shape = jax.ShapeDtypeStruct(kh.shape, F32)
    return pl.pallas_call(
        body, name=name,
        out_shape=(jax.ShapeDtypeStruct(qh.shape, F32), kv_shape, kv_shape, kv_shape, kv_shape,
                   jax.ShapeDtypeStruct(sinks_b.shape, F32)),
        grid=(N_KV_HEADS, s // WINDOW), in_specs=[q_spec, cur, prev, cur, prev, sink, q_spec],
        out_specs=(q_spec, cur, cur, cur, cur, sink),
        compiler_params=_params(("parallel", "arbitrary")))(qh, kh, kh, vh, vh, sinks_b, doh)


def _to_heads(t, heads):
    return t.reshape(t.shape[0], heads, HEAD_DIM).transpose(1, 0, 2)


def _from_heads(t):
    return t.transpose(1, 0, 2).reshape(t.shape[1], t.shape[0] * HEAD_DIM)


def _shift_window(t):
    return jnp.concatenate([t[:, WINDOW:], jnp.zeros_like(t[:, :WINDOW])], axis=1)


def merge_fwd(zg, branches, name):
    s = zg.shape[0]

    def body(zg_ref, b0, b1, b2, b3, o_ref):
        acc = jnp.zeros((TOKEN_TILE, D_MODEL), F32)
        for n, b_ref in enumerate((b0, b1, b2, b3)):
            gate = _sigmoid(zg_ref[:, n * D_MODEL:(n + 1) * D_MODEL].astype(F32))
            acc = acc + gate * b_ref[...].astype(F32)
        o_ref[...] = acc.astype(BF16)

    tile = pl.BlockSpec((TOKEN_TILE, D_MODEL), lambda i: (i, 0))
    wide = pl.BlockSpec((TOKEN_TILE, N_BRANCH * D_MODEL), lambda i: (i, 0))
    return pl.pallas_call(
        body, name=name, out_shape=jax.ShapeDtypeStruct((s, D_MODEL), BF16), grid=(s // TOKEN_TILE,),
        in_specs=[wide, tile, tile, tile, tile], out_specs=tile,
        compiler_params=_params(("parallel",)))(zg, *branches)


def merge_bwd(zg, branches, dm, name):
    s = zg.shape[0]

    def body(zg_ref, b0, b1, b2, b3, dm_ref, dzg_ref, d0, d1, d2, d3):
        dmv = dm_ref[...].astype(F32)
        for n, (b_ref, d_ref) in enumerate(((b0, d0), (b1, d1), (b2, d2), (b3, d3))):
            cols = slice(n * D_MODEL, (n + 1) * D_MODEL)
            gate = _sigmoid(zg_ref[:, cols].astype(F32))
            d_ref[...] = (gate * dmv).astype(BF16)
            dzg_ref[:, cols] = (dmv * b_ref[...].astype(F32) * gate * (1.0 - gate)).astype(BF16)

    tile = pl.BlockSpec((TOKEN_TILE, D_MODEL), lambda i: (i, 0))
    wide = pl.BlockSpec((TOKEN_TILE, N_BRANCH * D_MODEL), lambda i: (i, 0))
    act = jax.ShapeDtypeStruct((s, D_MODEL), BF16)
    return pl.pallas_call(
        body, name=name, out_shape=(jax.ShapeDtypeStruct((s, N_BRANCH * D_MODEL), BF16), act, act, act, act),
        grid=(s // TOKEN_TILE,), in_specs=[wide, tile, tile, tile, tile, tile],
        out_specs=(wide, tile, tile, tile, tile), compiler_params=_params(("parallel",)))(zg, *branches, dm)


def swiglu_fwd(gu, name):
    s = gu.shape[0]

    def body(g_ref, u_ref, o_ref):
        gate = g_ref[...].astype(F32)
        o_ref[...] = (gate * _sigmoid(gate) * u_ref[...].astype(F32)).astype(BF16)

    return pl.pallas_call(
        body, name=name, out_shape=jax.ShapeDtypeStruct((s, D_FF), BF16), grid=(s // TOKEN_TILE,),
        in_specs=[pl.BlockSpec((TOKEN_TILE, D_FF), lambda i: (i, 0)), pl.BlockSpec((TOKEN_TILE, D_FF), lambda i: (i, 1))],
        out_specs=pl.BlockSpec((TOKEN_TILE, D_FF), lambda i: (i, 0)), compiler_params=_params(("parallel",)))(gu, gu)


def swiglu_bwd(gu, dact, name):
    s = gu.shape[0]

    def body(g_ref, u_ref, da_ref, o_ref):
        gate = g_ref[...].astype(F32)
        sg = _sigmoid(gate)
        da = da_ref[...].astype(F32)
        o_ref[:, :D_FF] = (da * u_ref[...].astype(F32) * sg * (1.0 + gate * (1.0 - sg))).astype(BF16)
        o_ref[:, D_FF:] = (da * gate * sg).astype(BF16)

    half = pl.BlockSpec((TOKEN_TILE, D_FF), lambda i: (i, 0))
    return pl.pallas_call(
        body, name=name, out_shape=jax.ShapeDtypeStruct((s, 2 * D_FF), BF16), grid=(s // TOKEN_TILE,),
        in_specs=[half, pl.BlockSpec((TOKEN_TILE, D_FF), lambda i: (i, 1)), half],
        out_specs=pl.BlockSpec((TOKEN_TILE, 2 * D_FF), lambda i: (i, 0)),
        compiler_params=_params(("parallel",)))(gu, gu, dact)


ADAMW_BLOCK_BYTES = 1 << 20


def adamw(parts, w, m, v, name):
    n_parts, r, c = w.shape
    tr = _divisor_tile(r, max(SUBLANES, ADAMW_BLOCK_BYTES // (4 * c)), SUBLANES)
    tiles = r // tr

    def part_spec(j):
        return pl.BlockSpec((N_DEV, tr, c), lambda i: (0, jnp.clip(i - j * tiles, 0, tiles - 1), 0))

    def body(*refs):
        p_refs = refs[:n_parts]
        w_ref, m_ref, v_ref, g_ref, d_ref, nm_ref, nv_ref = refs[n_parts:]
        which = pl.program_id(0) // tiles
        g = None
        for j, p_ref in enumerate(p_refs):
            gj = p_ref[0].astype(F32)
            for i in range(1, N_DEV):
                gj = gj + p_ref[i].astype(F32)
            g = gj if g is None else jnp.where(which == j, gj, g)
        nm = ADAM_B1 * m_ref[0] + (1.0 - ADAM_B1) * g
        nv = ADAM_B2 * v_ref[0] + (1.0 - ADAM_B2) * (g * g)
        m_hat = nm / (1.0 - ADAM_B1 ** ADAM_STEP)
        v_hat = nv / (1.0 - ADAM_B2 ** ADAM_STEP)
        g_ref[0] = g
        d_ref[0] = -ADAM_LR * (m_hat / (jnp.sqrt(v_hat) + ADAM_EPS) + ADAM_WD * w_ref[0])
        nm_ref[0] = nm
        nv_ref[0] = nv

    tile = pl.BlockSpec((1, tr, c), lambda i: (i // tiles, i % tiles, 0))
    shape = jax.ShapeDtypeStruct(w.shape, F32)
    return pl.pallas_call(
        body, name=name, out_shape=(shape, shape, shape, shape), grid=(n_parts * tiles,),
        in_specs=[part_spec(j) for j in range(n_parts)] + [tile, tile, tile],
        out_specs=(tile, tile, tile, tile), compiler_params=_params(("parallel",)))(*parts, w, m, v)


_RELATIONS = [(a, b, e) for a in (0, 1) for b in (0, 1) for e in (0, 1)][1:]


_HBM_SPEC = pl.BlockSpec(memory_space=pltpu.HBM)
_SEM_SPEC = pl.BlockSpec(memory_space=pltpu.SEMAPHORE)
_ANY_SPEC = pl.BlockSpec(memory_space=pl.ANY)
_DATAFLOW = pltpu.SideEffectType.DATAFLOW_SIDE_EFFECTING


_OTHER_CHIPS = [(1, 0), (0, 1), (1, 1)]
_FIRST_LEVEL = [(0, 0, 1)] + [(a, b, 0) for a, b in _OTHER_CHIPS]


def _remote_copies(ins, lands, send_sems, recv_sems, scatter, relations):
    x, y, c = lax.axis_index("x"), lax.axis_index("y"), lax.axis_index("c")
    me = 4 * x + 2 * y + c
    copies = []
    for t in range(len(ins)):
        for k, (a, b, e) in enumerate(relations):
            px, py, pc = (x + a) % 2, (y + b) % 2, (c + e) % 2
            src = ins[t].at[4 * px + 2 * py + pc] if scatter[t] else ins[t]
            copies.append(pltpu.make_async_remote_copy(
                src_ref=src, dst_ref=lands[t].at[me], send_sem=send_sems.at[t * len(relations) + k],
                recv_sem=recv_sems.at[t * len(relations) + k],
                device_id=(px, py, pc), device_id_type=pl.DeviceIdType.MESH))
    return copies


def _forward_copies(lands, send_sems, recv_sems):
    x, y, c = lax.axis_index("x"), lax.axis_index("y"), lax.axis_index("c")
    copies = []
    for t in range(len(lands)):
        for k, (a, b) in enumerate(_OTHER_CHIPS):
            slot = lands[t].at[4 * ((x + a) % 2) + 2 * ((y + b) % 2) + c]
            copies.append(pltpu.make_async_remote_copy(
                src_ref=slot, dst_ref=slot, send_sem=send_sems.at[t * len(_OTHER_CHIPS) + k],
                recv_sem=recv_sems.at[t * len(_OTHER_CHIPS) + k],
                device_id=(x, y, 1 - c), device_id_type=pl.DeviceIdType.MESH))
    return copies


def _place_own(landed, own, me):
    return lax.dynamic_update_index_in_dim(landed, own, me, 0)


def exchange_start(arrays, scatter, name, after=None, relations=_RELATIONS):
    n = len(arrays)
    n_rel = len(relations)
    land_shapes = [a.shape if scatter[t] else (N_DEV,) + a.shape for t, a in enumerate(arrays)]
    zones = [lax.empty(s, a.dtype) for s, a in zip(land_shapes, arrays)]

    def body(*refs):
        ins, lands = refs[:n], refs[n:2 * n]
        send_sems, recv_sems = refs[-2 * n - 3], refs[-2 * n - 2]
        token = refs[-1]
        for cp in _remote_copies(ins, lands, send_sems, recv_sems, scatter, relations):
            cp.start()
        token[...] = jnp.zeros_like(token)

    sems = pltpu.SemaphoreType.DMA((n * n_rel,))
    out_shape = ((sems, sems) + tuple(pltpu.HBM(a.shape, a.dtype) for a in arrays)
                 + tuple(pltpu.HBM(s, a.dtype) for s, a in zip(land_shapes, arrays))
                 + (jax.ShapeDtypeStruct((SUBLANES, LANES), F32),))
    operands = [pltpu.with_memory_space_constraint(a, pltpu.HBM) for a in arrays]
    operands += [pltpu.with_memory_space_constraint(z, pltpu.HBM) for z in zones]
    in_specs = [_HBM_SPEC] * (2 * n)
    if after is not None:
        operands.append(after)
        in_specs.append(_ANY_SPEC)
    res = pl.pallas_call(
        body, name=name, out_shape=out_shape, in_specs=in_specs,
        out_specs=(_SEM_SPEC, _SEM_SPEC) + (_HBM_SPEC,) * (2 * n) + (pl.BlockSpec(memory_space=pltpu.VMEM),),
        input_output_aliases={i: 2 + i for i in range(2 * n)},
        compiler_params=pltpu.CompilerParams(has_side_effects=_DATAFLOW))(*operands)
    handle = (res[0], res[1], res[2:2 + n], res[2 + n:2 + 2 * n], tuple(scatter), relations)
    return handle, res[-1]


def exchange_wait(handle, after, name):
    send_sems, recv_sems, sources, lands, scatter, relations = handle
    n = len(sources)

    def body(*refs):
        ins, lzs = refs[:n], refs[n:2 * n]
        send_ref, recv_ref = refs[2 * n], refs[2 * n + 1]
        for cp in _remote_copies(ins, lzs, send_ref, recv_ref, scatter, relations):
            cp.wait_send()
            cp.wait_recv()

    out_shape = (tuple(pltpu.HBM(a.shape, a.dtype) for a in sources) + tuple(pltpu.HBM(a.shape, a.dtype) for a in lands))
    res = pl.pallas_call(
        body, name=name, out_shape=out_shape, in_specs=[_HBM_SPEC] * (2 * n) + [_SEM_SPEC, _SEM_SPEC, _ANY_SPEC],
        out_specs=(_HBM_SPEC,) * (2 * n), input_output_aliases={i: i for i in range(2 * n)},
        compiler_params=pltpu.CompilerParams(has_side_effects=_DATAFLOW))(*sources, *lands, send_sems, recv_sems, after)
    return res[n:], res[:n]


def forward_start(lands, name):
    n = len(lands)

    def body(*refs):
        send_sems, recv_sems, token = refs[n], refs[n + 1], refs[-1]
        for cp in _forward_copies(refs[:n], send_sems, recv_sems):
            cp.start()
        token[...] = jnp.zeros_like(token)

    sems = pltpu.SemaphoreType.DMA((n * len(_OTHER_CHIPS),))
    res = pl.pallas_call(
        body, name=name,
        out_shape=(sems, sems) + tuple(pltpu.HBM(a.shape, a.dtype) for a in lands)
        + (jax.ShapeDtypeStruct((SUBLANES, LANES), F32),),
        in_specs=[_HBM_SPEC] * n,
        out_specs=(_SEM_SPEC, _SEM_SPEC) + (_HBM_SPEC,) * n + (pl.BlockSpec(memory_space=pltpu.VMEM),),
        input_output_aliases={i: 2 + i for i in range(n)},
        compiler_params=pltpu.CompilerParams(has_side_effects=_DATAFLOW))(*lands)
    return (res[0], res[1], res[2:2 + n]), res[-1]


def forward_wait(handle, after, name):
    send_sems, recv_sems, lands = handle
    n = len(lands)

    def body(*refs):
        for cp in _forward_copies(refs[:n], refs[n], refs[n + 1]):
            cp.wait_send()
            cp.wait_recv()

    return pl.pallas_call(
        body, name=name, out_shape=tuple(pltpu.HBM(a.shape, a.dtype) for a in lands),
        in_specs=[_HBM_SPEC] * n + [_SEM_SPEC, _SEM_SPEC, _ANY_SPEC], out_specs=(_HBM_SPEC,) * n,
        input_output_aliases={i: i for i in range(n)},
        compiler_params=pltpu.CompilerParams(has_side_effects=_DATAFLOW))(*lands, send_sems, recv_sems, after)


_SMALL = ("norm_mix", "sg_ln_g", "sg_ln_b", "sg_b", "cv_b", "cv_ln_g", "cv_ln_b", "attn_sinks", "norm_ffn",
          "norm_final")
_SMALL_LATE = ("norm_mix",)
_PACK_UNIT = SUBLANES * LANES


def _pack(entries):
    flat = []
    for entry in entries:
        parts = [t.reshape(-1) for t in (entry if isinstance(entry, (list, tuple)) else [entry])]
        size = sum(t.shape[0] for t in parts)
        flat += parts + [jnp.zeros(((-size) % _PACK_UNIT,), parts[0].dtype)]
    return jnp.concatenate(flat).reshape(-1, LANES)


def _unpack(packed, like):
    out, row = [], 0
    for t in like:
        size = 1
        for d in t.shape:
            size *= d
        rows = -(-size // _PACK_UNIT) * SUBLANES
        out.append(packed[row:row + rows].reshape(-1)[:size].reshape(t.shape))
        row += rows
    return out


def _layer_fwd(l, x, p, late_params, mid_hook=None, ffn_hook=None):
    tag = f"l{l}_"
    xn = rmsnorm_fwd(x, p["norm_mix"], tag + "norm_mix")
    proj = matmul(xn, p["w_in_t"], "nt", BF16, tag + "proj_a", tm_cap=1024, tn_cap=2176, b_rows=PROJ_A)
    zg = matmul(xn, p["w_in_t_g"], "nt", BF16, tag + "proj_g", tm_cap=1024, tn_cap=2048)
    y_a = mixer_a_fwd(proj, p["sg_ln_g"], p["sg_ln_b"], p["sg_w"], p["sg_b"], tag + "mix_a")
    conv = conv_b_fwd(proj, p["cv_w"], p["cv_b"], tag + "conv_b")
    y_b = ln_silu_fwd(conv, p["cv_ln_g"], p["cv_ln_b"], tag + "ln_silu")
    y_d = conv_d_fwd(proj, p["sc_w"], tag + "conv_d")
    qk = rope_fwd(proj, p["cos"], p["sin"], tag + "rope")
    qh = _to_heads(qk[:, :Q_WIDTH], N_Q_HEADS)
    kh = _to_heads(qk[:, Q_WIDTH:], N_KV_HEADS)
    vh = _to_heads(proj[:, 2688:2816], N_KV_HEADS)
    token = mid_hook(qh) if mid_hook is not None else None
    oh = attention_fwd(qh, kh, vh, p["sinks"], tag + "attn", after=token)
    y_c = _from_heads(oh)
    ys = (y_a, y_b, y_c, y_d)
    p = {**p, **late_params(y_c)}
    branches = tuple(matmul(ys[n], p["w_branch"][n], "nn", BF16, tag + f"branch{n}", tm_cap=1024, tn_cap=1024)
                     for n in range(N_BRANCH))
    merged = merge_fwd(zg, branches, tag + "merge")
    x_mid = matmul(merged, p["w_out"], "nn", F32, tag + "out", add=x, tm_cap=1024, tn_cap=1024)
    token = ffn_hook(x_mid) if ffn_hook is not None else None
    hn = rmsnorm_fwd(x_mid, p["norm_ffn"], tag + "norm_ffn")
    gu = matmul(hn, p["w_gate_up_t"], "nt", BF16, tag + "gate_up", tm_cap=512, tn_cap=2816, after=token)
    act = swiglu_fwd(gu, tag + "swiglu")
    x_out = matmul(act, p["w_down"], "nn", F32, tag + "down", add=x_mid, tm_cap=512, tn_cap=1024)
    saved = dict(x=x, xn=xn, proj=proj, zg=zg, conv=conv, qh=qh, kh=kh, vh=vh, ys=ys, branches=branches,
                 merged=merged, x_mid=x_mid, hn=hn, gu=gu, act=act)
    return x_out, saved, p


def _layer_bwd(l, dx_out, p, sv, emit, after=None, before_last=None):
    tag = f"l{l}_b_"
    g = {}
    dact = matmul(dx_out, p["w_down"], "nt", BF16, tag + "dact", after=after, tm_cap=512, tn_cap=2816)
    dw_down = matmul(sv["act"], dx_out, "tn", BF16, tag + "dw_down", tm_cap=1408, tn_cap=512)
    dgu = swiglu_bwd(sv["gu"], dact, tag + "swiglu")
    dhn = matmul(dgu, p["w_gate_up_t"], "nn", BF16, tag + "dhn", tm_cap=512, tn_cap=512)
    dw_gate_up = matmul(dgu, sv["hn"], "tn", BF16, tag + "dw_gate_up", tm_cap=1408, tn_cap=1024)
    token = emit("a", {"w_gate_up": dw_gate_up, "w_down": dw_down})
    dx_mid, g["norm_ffn"] = rmsnorm_bwd(sv["x_mid"], p["norm_ffn"], dhn, dx_out, tag + "norm_ffn")
    dmerged = matmul(dx_mid, p["w_out"], "nt", BF16, tag + "dmerged", after=token, tm_cap=1024, tn_cap=1024)
    dw_out = matmul(sv["merged"], dx_mid, "tn", BF16, tag + "dw_out", tm_cap=1024, tn_cap=512)
    dzg, *dbranches = merge_bwd(sv["zg"], sv["branches"], dmerged, tag + "merge")
    dys = [matmul(dbranches[n], p["w_branch"][n], "nt", BF16, tag + f"dy{n}", tm_cap=1024, tn_cap=512)
           for n in range(N_BRANCH)]
    dw_branch = jnp.stack(
        [matmul(sv["ys"][n], dbranches[n], "tn", BF16, tag + f"dw_branch{n}", tm_cap=512, tn_cap=1024)
         for n in range(N_BRANCH)])
    proj = sv["proj"]
    dz_a, g["sg_ln_g"], g["sg_ln_b"], dsw_a, dsb = mixer_a_bwd(
        proj, dys[0], p["sg_ln_g"], p["sg_ln_b"], p["sg_w"], p["sg_b"], tag + "mix_a")
    g["sg_b"] = dsb[:, :, 0]
    token = emit("b", {"w_branch": dw_branch, "w_out": dw_out}, {"sg_w": dsw_a})
    dconv, g["cv_ln_g"], g["cv_ln_b"] = ln_silu_bwd(sv["conv"], dys[1], p["cv_ln_g"], p["cv_ln_b"], tag + "ln_silu",
                                                    after=token)
    da, dgate, dcw, g["cv_b"] = conv_b_bwd(proj, p["cv_w"], dconv, tag + "conv_b")
    g["cv_w"] = dcw[:CV_KERNEL]
    doh = _to_heads(dys[2], N_Q_HEADS)
    dqh, dkc, dkp, dvc, dvp, dsk = attention_bwd(sv["qh"], sv["kh"], sv["vh"], p["sinks"], doh, tag + "attn")
    g["attn_sinks"] = dsk[:, 0, 0]
    dqk_cur = jnp.concatenate([_from_heads(dqh), _from_heads(dkc)], axis=1)
    dqk_prev = jnp.concatenate([jnp.zeros((SEQ, Q_WIDTH), F32), _from_heads(_shift_window(dkp))], axis=1)
    dqk = rope_bwd(dqk_cur, dqk_prev, p["cos"], -p["sin"], tag + "rope")
    dv = (_from_heads(dvc) + _from_heads(_shift_window(dvp))).astype(BF16)
    dbg, dcg, dh, dsw = conv_d_bwd(proj, p["sc_w"], dys[3], tag + "conv_d")
    g["sc_w"] = dsw[:SC_KERNEL]
    dproj = jnp.concatenate([dz_a, da, dgate, dqk, dv, dbg, dcg, dh], axis=1)
    if before_last is not None:
        token = before_last(g)
    dw_in = matmul(dproj, sv["xn"], "tn", BF16, tag + "dw_in_a", after=token, tm_cap=2176, tn_cap=512,
                   out_rows=PROJ_WIDTH)
    g["w_in"] = matmul(dzg, sv["xn"], "tn", BF16, tag + "dw_in_g", tm_cap=256, tn_cap=1024, into=dw_in,
                       into_row=PROJ_A)
    token = emit("c", {n: g.pop(n) for n in _EARLY})
    dxn = matmul(dproj, p["w_in_t"], "nn", F32, tag + "dxn_a", after=token, tm_cap=512, tn_cap=512, b_rows=PROJ_A)
    dxn = matmul(dzg, p["w_in_t_g"], "nn", F32, tag + "dxn_g", add=dxn, tm_cap=512, tn_cap=512)
    dx_in, g["norm_mix"] = rmsnorm_bwd(sv["x"], p["norm_mix"], dxn, dx_mid, tag + "norm_mix")
    return dx_in, g, token


_EARLY = ("w_in", "cv_w", "sc_w")
_LATE = ("w_branch", "w_out", "w_gate_up", "w_down")


_TRANSPOSED = ("w_in", "w_gate_up")


def _shard_view(name, t):
    return jnp.swapaxes(t, 1, 2) if name in _TRANSPOSED else t


def _full_weight(name, t):
    if name in ("w_out", "w_down") + _TRANSPOSED:
        return t.reshape(-1, t.shape[-1])
    if name == "w_branch":
        return t.transpose(1, 2, 0, 3).reshape(N_BRANCH, SG_WIDTH, D_MODEL)
    return t.transpose(1, 0, 2).reshape(t.shape[1], -1)


def _to_blocks(name, full):
    if name in ("w_out", "w_down") + _TRANSPOSED:
        return full.reshape(N_DEV, -1, full.shape[-1])
    if name == "w_branch":
        return full.reshape(N_BRANCH, SG_WIDTH, N_DEV, -1).transpose(2, 0, 1, 3)
    return full.reshape(full.shape[0], N_DEV, -1).transpose(1, 0, 2)


def _rope_tables():
    pos = jnp.arange(SEQ, dtype=F32)
    inv_freq = 1.0 / (ROPE_THETA ** (jnp.arange(0, HEAD_DIM, 2, dtype=F32) / HEAD_DIM))
    ang = pos[:, None] * inv_freq[None, :]
    cos, sin = jnp.cos(ang), jnp.sin(ang)
    reps = LANES // HEAD_DIM
    return jnp.tile(jnp.concatenate([cos, cos], axis=1), (1, reps)), jnp.tile(jnp.concatenate([-sin, sin], axis=1), (1, reps))


def kernel(x, norm_mix, w_in, sg_ln_g, sg_ln_b, sg_w, sg_b, cv_w, cv_b, cv_ln_g, cv_ln_b, attn_sinks, sc_w, w_branch, w_out, norm_ffn, w_gate_up, w_down, norm_final, loss_target, m_norm_mix, m_w_in, m_sg_ln_g, m_sg_ln_b, m_sg_w, m_sg_b, m_cv_w, m_cv_b, m_cv_ln_g, m_cv_ln_b, m_attn_sinks, m_sc_w, m_w_branch, m_w_out, m_norm_ffn, m_w_gate_up, m_w_down, m_norm_final, v_norm_mix, v_w_in, v_sg_ln_g, v_sg_ln_b, v_sg_w, v_sg_b, v_cv_w, v_cv_b, v_cv_ln_g, v_cv_ln_b, v_attn_sinks, v_sc_w, v_w_branch, v_w_out, v_norm_ffn, v_w_gate_up, v_w_down, v_norm_final):
    names = ("norm_mix", "w_in", "sg_ln_g", "sg_ln_b", "sg_w", "sg_b", "cv_w", "cv_b", "cv_ln_g", "cv_ln_b",
             "attn_sinks", "sc_w", "w_branch", "w_out", "norm_ffn", "w_gate_up", "w_down", "norm_final")
    w = dict(zip(names, (norm_mix, w_in, sg_ln_g, sg_ln_b, sg_w, sg_b, cv_w, cv_b, cv_ln_g, cv_ln_b, attn_sinks,
                         sc_w, w_branch, w_out, norm_ffn, w_gate_up, w_down, norm_final)))
    m = dict(zip(names, (m_norm_mix, m_w_in, m_sg_ln_g, m_sg_ln_b, m_sg_w, m_sg_b, m_cv_w, m_cv_b, m_cv_ln_g,
                         m_cv_ln_b, m_attn_sinks, m_sc_w, m_w_branch, m_w_out, m_norm_ffn, m_w_gate_up, m_w_down,
                         m_norm_final)))
    v = dict(zip(names, (v_norm_mix, v_w_in, v_sg_ln_g, v_sg_ln_b, v_sg_w, v_sg_b, v_cv_w, v_cv_b, v_cv_ln_g,
                         v_cv_ln_b, v_attn_sinks, v_sc_w, v_w_branch, v_w_out, v_norm_ffn, v_w_gate_up, v_w_down,
                         v_norm_final)))

    me = 4 * lax.axis_index("x") + 2 * lax.axis_index("y") + lax.axis_index("c")
    groups = [(l, group) for l in range(DEPTH) for group in (_EARLY, _LATE)]
    shards = {(l, group): [_shard_view(n, w[n])[l].astype(BF16) for n in group] for l, group in groups}
    gathers, forwards, own_shards, token = {}, {}, {}, None
    for l, group in groups:
        gathers[(l, group)], token = exchange_start(
            shards[(l, group)], [False] * len(group), f"gather_start{l}_{group[0]}", after=token,
            relations=_FIRST_LEVEL)

    def begin_forward(l, group, after):
        landed, own_shards[(l, group)] = exchange_wait(gathers[(l, group)], after, f"gather_wait{l}_{group[0]}")
        forwards[(l, group)], tok = forward_start(landed, f"forward_start{l}_{group[0]}")
        return tok

    def landed_weights(l, group, after):
        landed = forward_wait(forwards[(l, group)], after, f"forward_wait{l}_{group[0]}")
        return {n + "_t" if n in _TRANSPOSED else n: _full_weight(n, _place_own(t, own, me))
                for n, t, own in zip(group, landed, own_shards[(l, group)])}

    cos_t, sin_t = _rope_tables()

    def early_params(l, after):
        full = landed_weights(l, _EARLY, after)
        return dict(
            norm_mix=w["norm_mix"][l][None], norm_ffn=w["norm_ffn"][l][None],
            w_in_t=full["w_in_t"], w_in_t_g=full["w_in_t"][PROJ_A:],
            sg_ln_g=w["sg_ln_g"][l][None], sg_ln_b=w["sg_ln_b"][l][None], sg_w=w["sg_w"][l],
            sg_b=jnp.broadcast_to(w["sg_b"][l][:, :, None], (SG_GROUPS, SG_CHUNK, LANES)),
            cv_w=jnp.pad(full["cv_w"].astype(F32), ((0, HALO - CV_KERNEL), (0, 0))),
            cv_b=w["cv_b"][l][None], cv_ln_g=w["cv_ln_g"][l][None], cv_ln_b=w["cv_ln_b"][l][None],
            sinks=jnp.broadcast_to(w["attn_sinks"][l][:, None, None], (N_Q_HEADS, 1, LANES)),
            sc_w=jnp.pad(full["sc_w"].astype(F32), ((0, SUBLANES - SC_KERNEL), (0, 0))),
            cos=cos_t, sin=sin_t)

    params, saved = [None] * DEPTH, [None] * DEPTH
    h = x[0]
    after = begin_forward(0, _EARLY, token)
    for l in range(DEPTH):
        h, saved[l], params[l] = _layer_fwd(
            l, h, early_params(l, after), lambda behind, l=l: landed_weights(l, _LATE, behind),
            mid_hook=lambda behind, l=l: begin_forward(l, _LATE, behind),
            ffn_hook=(lambda behind, l=l: begin_forward(l + 1, _EARLY, behind)) if l + 1 < DEPTH else None)
        after = h
    loss_row, dh, d_norm_final = loss_head(h, w["norm_final"][None], loss_target[0], "loss_head")

    sent = {}

    def emitter(l):
        def emit(group, grads_of, replicated=None):
            replicated = replicated or {}
            send = [_to_blocks(n, grads_of[n].astype(BF16)) for n in grads_of] + list(replicated.values())
            flags = [True] * len(grads_of) + [False] * len(replicated)
            handle, tok = exchange_start(send, flags, f"grads_start{l}{group}")
            sent[(l, group)] = (handle, tuple(grads_of) + tuple(replicated), flags)
            return tok
        return emit

    small_sent = {}
    no_state = jnp.zeros((1,), F32)
    small_early = tuple(n for n in _SMALL if n not in _SMALL_LATE) + ("loss",)

    def start_small(tag, tag_names, entries, after):
        small_sent[tag], tok = exchange_start([_pack(entries)], [False], f"grads_start_small_{tag}", after=after)
        small_sent[tag] = (small_sent[tag], tag_names)
        return tok

    grads = [None] * DEPTH

    def before_last(g0):
        by_name = {"norm_final": [d_norm_final], "loss": [loss_row[0, :1]]}
        return start_small("early", small_early, [by_name.get(n) or [g0[n], grads[1][n]] for n in small_early], None)

    token = None
    for l in reversed(range(DEPTH)):
        dh, grads[l], token = _layer_bwd(l, dh, params[l], saved[l], emitter(l), after=token,
                                         before_last=before_last if l == 0 else None)
    grad_x = dh[None]

    token = start_small("late", _SMALL_LATE, [[grads[l][n] for l in range(DEPTH)] for n in _SMALL_LATE], token)

    def received(l, group, after):
        handle, group_names, flags = sent[(l, group)]
        landed, sources = exchange_wait(handle, after, f"grads_wait{l}{group}")
        return {n: _place_own(t, lax.dynamic_index_in_dim(s, me, 0, keepdims=False) if scattered else s, me)
                for n, t, s, scattered in zip(group_names, landed, sources, flags)}

    out_g, out_d, out_m, out_v = {}, {}, {}, {}

    def update(n, by_layer):
        shape = _shard_view(n, w[n]).shape
        view = (DEPTH, w[n].size // (DEPTH * shape[-1]), shape[-1])
        parts = [t.reshape((N_DEV,) + view[1:]) for t in by_layer]
        res = adamw(parts, *[_shard_view(n, t).reshape(view) for t in (w[n], m[n], v[n])], "adamw_" + n)
        out_g[n], out_d[n], out_m[n], out_v[n] = (_shard_view(n, t.reshape(shape)) for t in res)
        return res[0]

    def finish_small(tag, after):
        handle, tag_names = small_sent[tag]
        landed, sources = exchange_wait(handle, after, f"grads_wait_small_{tag}")
        like, m_like, v_like = ([no_state if n == "loss" else t[n] for n in tag_names] for t in (w, m, v))
        res = adamw([_place_own(landed[0], sources[0], me)], _pack(like)[None], _pack(m_like)[None],
                    _pack(v_like)[None], f"adamw_small_{tag}")
        for store, packed in zip((out_g, out_d, out_m, out_v), res):
            for n, t in zip(tag_names, _unpack(packed[0], like)):
                store[n] = t
        return res[0]

    behind = token
    for group in ("a", "b", "c"):
        if group == "c":
            behind = finish_small("early", behind)
        r1 = received(1, group, behind)
        r0 = received(0, group, next(iter(r1.values())))
        for n in r0:
            behind = update(n, [r0[n], r1[n]])
    finish_small("late", behind)

    loss = out_g["loss"][0]
    return (loss, grad_x, *[out_g[n] for n in names], *[out_d[n] for n in names], *[out_m[n] for n in names],
            *[out_v[n] for n in names])
```

```python
import jax
import jax.numpy as jnp
from jax import lax
from jax.experimental import pallas as pl
from jax.experimental.pallas import tpu as pltpu

F32 = jnp.float32
BF16 = jnp.bfloat16

SEQ = 2048
D_MODEL = 1024
DEPTH = 2
SG_WIDTH = 512
SG_CHUNK = 128
SG_GROUPS = 4
CV_WIDTH = 512
CV_KERNEL = 31
HEAD_DIM = 64
N_Q_HEADS = 8
N_KV_HEADS = 2
Q_WIDTH = 512
KV_WIDTH = 128
WINDOW = 128
SC_WIDTH = 512
SC_KERNEL = 3
N_BRANCH = 4
D_FF = 2816
EPS = 1e-6
ROPE_THETA = 10000.0
COL_B = 2 * SG_WIDTH
COL_Q = COL_B + 2 * CV_WIDTH
COL_K = COL_Q + Q_WIDTH
COL_V = COL_K + KV_WIDTH
COL_D = COL_V + KV_WIDTH
PROJ_A = COL_D + 3 * SC_WIDTH
PROJ_WIDTH = PROJ_A + N_BRANCH * D_MODEL
N_DEV = 8

ADAM_LR = 0.001
ADAM_B1 = 0.9
ADAM_B2 = 0.999
ADAM_EPS = 1e-08
ADAM_WD = 0.01
ADAM_STEP = 10

LANES = 128
SUBLANES = 8
VMEM_LIMIT_BYTES = 48 * 1024 * 1024
HALO = 32
CONV_ROWS = 256
TOKEN_TILE = 256
NORM_TILE = 512
ROPE_TILE = 1024

_SQRT_HALF = 0.7071067811865476
_INV_SQRT_2PI = 0.3989422804014327


def _params(semantics=None):
    return pltpu.CompilerParams(dimension_semantics=semantics, vmem_limit_bytes=VMEM_LIMIT_BYTES)


def _divisor_tile(n, cap, unit):
    best = None
    for t in range(unit, min(n, cap) + 1, unit):
        if n % t == 0:
            best = t
    return best if best is not None else n


_DIMS = {"nn": (((1,), (0,)), ((), ())), "nt": (((1,), (1,)), ((), ())), "tn": (((0,), (0,)), ((), ()))}


def matmul(a, b, mode, out_dtype, name, add=None, tm_cap=512, tn_cap=512, after=None, b_rows=None, out_rows=None,
           into=None, into_row=0):
    if mode == "nn":
        (m, k), n = a.shape, b.shape[1]
        k = b_rows if b_rows is not None else k
    elif mode == "nt":
        (m, k), n = a.shape, (b_rows if b_rows is not None else b.shape[0])
    else:
        (k, m), n = a.shape, b.shape[1]
    tm = _divisor_tile(m, tm_cap, LANES)
    tn = _divisor_tile(n, tn_cap, LANES)
    row0 = into_row // tm
    assert row0 * tm == into_row
    a_spec = pl.BlockSpec((k, tm), lambda i, j: (0, i)) if mode == "tn" else pl.BlockSpec((tm, k), lambda i, j: (i, 0))
    b_spec = pl.BlockSpec((tn, k), lambda i, j: (j, 0)) if mode == "nt" else pl.BlockSpec((k, tn), lambda i, j: (0, j))
    o_spec = pl.BlockSpec((tm, tn), lambda i, j: (i + row0, j))
    dims = _DIMS[mode]

    def body(*refs):
        a_ref, b_ref = refs[0], refs[1]
        o_ref = refs[-1]
        acc = lax.dot_general(a_ref[...].astype(BF16), b_ref[...].astype(BF16), dims, preferred_element_type=F32)
        if add is not None:
            acc = acc + refs[2][...].astype(F32)
        o_ref[...] = acc.astype(out_dtype)

    unread = tuple(t for t in (after, into) if t is not None)
    operands = (a, b) + (() if add is None else (add,)) + unread
    in_specs = [a_spec, b_spec] + ([o_spec] if add is not None else [])
    in_specs += [pl.BlockSpec(memory_space=pl.ANY)] * len(unread)
    aliases = {len(operands) - 1: 0} if into is not None else {}
    return pl.pallas_call(
        body, name=name,
        out_shape=jax.ShapeDtypeStruct((into.shape[0] if into is not None else out_rows or m, n), out_dtype),
        grid=(m // tm, n // tn),
        in_specs=in_specs, out_specs=o_spec, input_output_aliases=aliases,
        compiler_params=_params(("parallel", "parallel")))(*operands)


def _sigmoid(x):
    return 1.0 / (1.0 + jnp.exp(-x))


def _gelu(x):
    return 0.5 * x * (1.0 + lax.erf(x * _SQRT_HALF))


def _gelu_grad(x):
    return 0.5 * (1.0 + lax.erf(x * _SQRT_HALF)) + x * _INV_SQRT_2PI * jnp.exp(-0.5 * x * x)


def _rms_stats(x):
    r = lax.rsqrt(jnp.mean(x * x, axis=-1, keepdims=True) + EPS)
    return x * r, r


def _rms_bwd(dxn, xhat, r, g):
    h = dxn * g
    return r * (h - xhat * jnp.mean(h * xhat, axis=-1, keepdims=True))


def _ln_stats(x):
    mu = jnp.mean(x, axis=-1, keepdims=True)
    xc = x - mu
    rstd = lax.rsqrt(jnp.mean(xc * xc, axis=-1, keepdims=True) + EPS)
    return xc * rstd, rstd


def _ln_bwd(dy, xhat, rstd, g):
    dxhat = dy * g
    return rstd * (dxhat - jnp.mean(dxhat, axis=-1, keepdims=True)
                   - xhat * jnp.mean(dxhat * xhat, axis=-1, keepdims=True))


def _accumulate(ref, value, first):
    @pl.when(first)
    def _():
        ref[...] = value

    @pl.when(jnp.logical_not(first))
    def _():
        ref[...] += value


def _shift_rows(win, shift, n_out):
    n = win.shape[0]
    if shift % n == 0:
        return win[:n_out]
    return pltpu.roll(win, n - shift, axis=0)[:n_out]


def _row_spec(width):
    return pl.BlockSpec((1, width), lambda i: (0, 0))


def rmsnorm_fwd(x, g, name):
    s, d = x.shape

    def body(x_ref, g_ref, o_ref):
        xhat, _ = _rms_stats(x_ref[...])
        o_ref[...] = (xhat * g_ref[...]).astype(BF16)

    tile = pl.BlockSpec((NORM_TILE, d), lambda i: (i, 0))
    return pl.pallas_call(
        body, name=name, out_shape=jax.ShapeDtypeStruct((s, d), BF16), grid=(s // NORM_TILE,),
        in_specs=[tile, _row_spec(d)], out_specs=tile, compiler_params=_params(("parallel",)))(x, g)


def rmsnorm_bwd(x, g, dxn, dres, name):
    s, d = x.shape

    def body(x_ref, g_ref, dxn_ref, dres_ref, dx_ref, dg_ref):
        xhat, r = _rms_stats(x_ref[...])
        dxn_v = dxn_ref[...].astype(F32)
        dx_ref[...] = dres_ref[...] + _rms_bwd(dxn_v, xhat, r, g_ref[...])
        _accumulate(dg_ref, jnp.sum(dxn_v * xhat, axis=0, keepdims=True), pl.program_id(0) == 0)

    tile = pl.BlockSpec((NORM_TILE, d), lambda i: (i, 0))
    return pl.pallas_call(
        body, name=name, out_shape=(jax.ShapeDtypeStruct((s, d), F32), jax.ShapeDtypeStruct((1, d), F32)),
        grid=(s // NORM_TILE,), in_specs=[tile, _row_spec(d), tile, tile], out_specs=(tile, _row_spec(d)),
        compiler_params=_params(("arbitrary",)))(x, g, dxn, dres)


def loss_head(x, g, target, name):
    s, d = x.shape

    def body(x_ref, g_ref, t_ref, loss_ref, dx_ref, dg_ref):
        first = pl.program_id(0) == 0
        xhat, r = _rms_stats(x_ref[...])
        gv = g_ref[...]
        err = xhat * gv - t_ref[...]
        part = 0.5 * jnp.sum(jnp.sum(err * err, axis=-1, keepdims=True), axis=0, keepdims=True) / d
        _accumulate(loss_ref, jnp.broadcast_to(part, (1, LANES)), first)
        dy = err / d
        dx_ref[...] = _rms_bwd(dy, xhat, r, gv)
        _accumulate(dg_ref, jnp.sum(dy * xhat, axis=0, keepdims=True), first)

    tile = pl.BlockSpec((NORM_TILE, d), lambda i: (i, 0))
    return pl.pallas_call(
        body, name=name,
        out_shape=(jax.ShapeDtypeStruct((1, LANES), F32), jax.ShapeDtypeStruct((s, d), F32),
                   jax.ShapeDtypeStruct((1, d), F32)),
        grid=(s // NORM_TILE,), in_specs=[tile, _row_spec(d), tile],
        out_specs=(_row_spec(LANES), tile, _row_spec(d)), compiler_params=_params(("arbitrary",)))(x, g, target)


def _tril_mask():
    row = lax.broadcasted_iota(jnp.int32, (SG_CHUNK, SG_CHUNK), 0)
    col = lax.broadcasted_iota(jnp.int32, (SG_CHUNK, SG_CHUNK), 1)
    return row >= col


def _sg_specs():
    vec = _row_spec(SG_WIDTH)
    mat = pl.BlockSpec((SG_GROUPS, SG_CHUNK, SG_CHUNK), lambda i: (0, 0, 0))
    return vec, mat


def mixer_a_fwd(proj, ln_g, ln_b, w_s, b_s, name):
    s = proj.shape[0]
    chunks = TOKEN_TILE // SG_CHUNK

    def body(z_ref, lg_ref, lb_ref, w_ref, b_ref, o_ref):
        ge = _gelu(z_ref[...].astype(F32))
        u = ge[:, :SG_WIDTH]
        xhat, _ = _ln_stats(ge[:, SG_WIDTH:])
        vn = xhat * lg_ref[...] + lb_ref[...]
        tril = _tril_mask()
        for ci in range(chunks):
            rows = slice(ci * SG_CHUNK, (ci + 1) * SG_CHUNK)
            for g in range(SG_GROUPS):
                cols = slice(g * LANES, (g + 1) * LANES)
                wm = jnp.where(tril, w_ref[g], 0.0).astype(BF16)
                mixed = jnp.dot(wm, vn[rows, cols].astype(BF16), preferred_element_type=F32) + b_ref[g]
                o_ref[rows, cols] = (u[rows, cols] * mixed).astype(BF16)

    vec, mat = _sg_specs()
    return pl.pallas_call(
        body, name=name, out_shape=jax.ShapeDtypeStruct((s, SG_WIDTH), BF16), grid=(s // TOKEN_TILE,),
        in_specs=[pl.BlockSpec((TOKEN_TILE, 2 * SG_WIDTH), lambda i: (i, 0)), vec, vec, mat, mat],
        out_specs=pl.BlockSpec((TOKEN_TILE, SG_WIDTH), lambda i: (i, 0)),
        compiler_params=_params(("parallel",)))(proj, ln_g, ln_b, w_s, b_s)


def mixer_a_bwd(proj, dy, ln_g, ln_b, w_s, b_s, name):
    s = proj.shape[0]
    chunks = TOKEN_TILE // SG_CHUNK

    def body(z_ref, dy_ref, lg_ref, lb_ref, w_ref, b_ref, dz_ref, dlg_ref, dlb_ref, dw_ref, db_ref, du_scr, dvn_scr):
        first = pl.program_id(0) == 0

        @pl.when(first)
        def _():
            dw_ref[...] = jnp.zeros_like(dw_ref)
            db_ref[...] = jnp.zeros_like(db_ref)

        z = z_ref[...].astype(F32)
        ge = _gelu(z)
        u = ge[:, :SG_WIDTH]
        xhat, rstd = _ln_stats(ge[:, SG_WIDTH:])
        lg = lg_ref[...]
        vn = xhat * lg + lb_ref[...]
        dyv = dy_ref[...].astype(F32)
        tril = _tril_mask()
        for ci in range(chunks):
            rows = slice(ci * SG_CHUNK, (ci + 1) * SG_CHUNK)
            for g in range(SG_GROUPS):
                cols = slice(g * LANES, (g + 1) * LANES)
                wm = jnp.where(tril, w_ref[g], 0.0).astype(BF16)
                vg = vn[rows, cols].astype(BF16)
                mixed = jnp.dot(wm, vg, preferred_element_type=F32) + b_ref[g]
                dyb = dyv[rows, cols]
                du_scr[rows, cols] = dyb * mixed
                dmix = dyb * u[rows, cols]
                db_ref[g] += jnp.broadcast_to(jnp.sum(dmix, axis=1, keepdims=True), (SG_CHUNK, LANES))
                dmb = dmix.astype(BF16)
                dwg = lax.dot_general(dmb, vg, _DIMS["nt"], preferred_element_type=F32)
                dw_ref[g] += jnp.where(tril, dwg, 0.0)
                dvn_scr[rows, cols] = lax.dot_general(wm, dmb, _DIMS["tn"], preferred_element_type=F32)
        dvn = dvn_scr[...]
        _accumulate(dlg_ref, jnp.sum(dvn * xhat, axis=0, keepdims=True), first)
        _accumulate(dlb_ref, jnp.sum(dvn, axis=0, keepdims=True), first)
        dvv = _ln_bwd(dvn, xhat, rstd, lg)
        gg = _gelu_grad(z)
        dz_ref[:, :SG_WIDTH] = (du_scr[...] * gg[:, :SG_WIDTH]).astype(BF16)
        dz_ref[:, SG_WIDTH:] = (dvv * gg[:, SG_WIDTH:]).astype(BF16)

    vec, mat = _sg_specs()
    mat_shape = jax.ShapeDtypeStruct((SG_GROUPS, SG_CHUNK, SG_CHUNK), F32)
    vec_shape = jax.ShapeDtypeStruct((1, SG_WIDTH), F32)
    return pl.pallas_call(
        body, name=name,
        out_shape=(jax.ShapeDtypeStruct((s, 2 * SG_WIDTH), BF16), vec_shape, vec_shape, mat_shape, mat_shape),
        grid=(s // TOKEN_TILE,),
        in_specs=[pl.BlockSpec((TOKEN_TILE, 2 * SG_WIDTH), lambda i: (i, 0)),
                  pl.BlockSpec((TOKEN_TILE, SG_WIDTH), lambda i: (i, 0)), vec, vec, mat, mat],
        out_specs=(pl.BlockSpec((TOKEN_TILE, 2 * SG_WIDTH), lambda i: (i, 0)), vec, vec, mat, mat),
        scratch_shapes=[pltpu.VMEM((TOKEN_TILE, SG_WIDTH), F32), pltpu.VMEM((TOKEN_TILE, SG_WIDTH), F32)],
        compiler_params=_params(("arbitrary",)))(proj, dy, ln_g, ln_b, w_s, b_s)


_B_A_BLOCK = COL_B // LANES
_B_G_BLOCK = (COL_B + CV_WIDTH) // LANES
_CH_TILES = CV_WIDTH // LANES


def _col_spec(s, first_block):
    return pl.BlockSpec((s, LANES), lambda j: (0, first_block + j))


def conv_b_fwd(proj, w_pad, bias, name):
    s = proj.shape[0]

    def body(a_ref, g_ref, w_ref, b_ref, c_ref, upad):
        upad[0:HALO, :] = jnp.zeros((HALO, LANES), F32)
        upad[HALO:, :] = a_ref[...].astype(F32) * _sigmoid(g_ref[...].astype(F32))
        w = w_ref[...]
        bv = b_ref[...]

        def block(bi, carry):
            start = pl.multiple_of(bi * CONV_ROWS, CONV_ROWS)
            win = upad[pl.ds(start, CONV_ROWS + HALO), :]
            acc = jnp.zeros((CONV_ROWS, LANES), F32)
            for k in range(CV_KERNEL):
                acc = acc + w[k:k + 1, :] * _shift_rows(win, HALO - (CV_KERNEL - 1) + k, CONV_ROWS)
            c_ref[pl.ds(start, CONV_ROWS), :] = acc + bv
            return carry

        lax.fori_loop(0, s // CONV_ROWS, block, 0)

    return pl.pallas_call(
        body, name=name, out_shape=jax.ShapeDtypeStruct((s, CV_WIDTH), F32), grid=(_CH_TILES,),
        in_specs=[_col_spec(s, _B_A_BLOCK), _col_spec(s, _B_G_BLOCK), _col_spec(HALO, 0), _col_spec(1, 0)],
        out_specs=_col_spec(s, 0), scratch_shapes=[pltpu.VMEM((s + HALO, LANES), F32)],
        compiler_params=_params(("parallel",)))(proj, proj, w_pad, bias)


def conv_b_bwd(proj, w_pad, dc, name):
    s = proj.shape[0]

    def body(a_ref, g_ref, w_ref, dc_ref, da_ref, dg_ref, dw_ref, db_ref, upad, dpad, dw_scr):
        upad[0:HALO, :] = jnp.zeros((HALO, LANES), F32)
        upad[HALO:, :] = a_ref[...].astype(F32) * _sigmoid(g_ref[...].astype(F32))
        dcv = dc_ref[...]
        dpad[0:s, :] = dcv
        dpad[s:, :] = jnp.zeros((HALO, LANES), F32)
        db_ref[...] = jnp.sum(dcv, axis=0, keepdims=True)
        dw_scr[...] = jnp.zeros((HALO, LANES), F32)
        w = w_ref[...]

        def block(bi, carry):
            start = pl.multiple_of(bi * CONV_ROWS, CONV_ROWS)
            uwin = upad[pl.ds(start, CONV_ROWS + HALO), :]
            dwin = dpad[pl.ds(start, CONV_ROWS + HALO), :]
            dcb = dwin[:CONV_ROWS]
            du = jnp.zeros((CONV_ROWS, LANES), F32)
            for k in range(CV_KERNEL):
                du = du + w[k:k + 1, :] * _shift_rows(dwin, CV_KERNEL - 1 - k, CONV_ROWS)
                ush = _shift_rows(uwin, HALO - (CV_KERNEL - 1) + k, CONV_ROWS)
                dw_scr[k:k + 1, :] += jnp.sum(dcb * ush, axis=0, keepdims=True)
            av = a_ref[pl.ds(start, CONV_ROWS), :].astype(F32)
            sg = _sigmoid(g_ref[pl.ds(start, CONV_ROWS), :].astype(F32))
            da_ref[pl.ds(start, CONV_ROWS), :] = (du * sg).astype(BF16)
            dg_ref[pl.ds(start, CONV_ROWS), :] = (du * av * sg * (1.0 - sg)).astype(BF16)
            return carry

        lax.fori_loop(0, s // CONV_ROWS, block, 0)
        dw_ref[...] = dw_scr[...]

    act = jax.ShapeDtypeStruct((s, CV_WIDTH), BF16)
    return pl.pallas_call(
        body, name=name,
        out_shape=(act, act, jax.ShapeDtypeStruct((HALO, CV_WIDTH), F32), jax.ShapeDtypeStruct((1, CV_WIDTH), F32)),
        grid=(_CH_TILES,),
        in_specs=[_col_spec(s, _B_A_BLOCK), _col_spec(s, _B_G_BLOCK), _col_spec(HALO, 0), _col_spec(s, 0)],
        out_specs=(_col_spec(s, 0), _col_spec(s, 0), _col_spec(HALO, 0), _col_spec(1, 0)),
        scratch_shapes=[pltpu.VMEM((s + HALO, LANES), F32), pltpu.VMEM((s + HALO, LANES), F32),
                        pltpu.VMEM((HALO, LANES), F32)],
        compiler_params=_params(("parallel",)))(proj, proj, w_pad, dc)


def ln_silu_fwd(c, ln_g, ln_b, name):
    s, d = c.shape

    def body(c_ref, g_ref, b_ref, o_ref):
        xhat, _ = _ln_stats(c_ref[...])
        cn = xhat * g_ref[...] + b_ref[...]
        o_ref[...] = (cn * _sigmoid(cn)).astype(BF16)

    tile = pl.BlockSpec((NORM_TILE, d), lambda i: (i, 0))
    return pl.pallas_call(
        body, name=name, out_shape=jax.ShapeDtypeStruct((s, d), BF16), grid=(s // NORM_TILE,),
        in_specs=[tile, _row_spec(d), _row_spec(d)], out_specs=tile,
        compiler_params=_params(("parallel",)))(c, ln_g, ln_b)


def ln_silu_bwd(c, dy, ln_g, ln_b, name, after=None):
    s, d = c.shape
    unread = () if after is None else (after,)

    def body(c_ref, dy_ref, g_ref, b_ref, *rest):
        dc_ref, dg_ref, db_ref = rest[-3:]
        first = pl.program_id(0) == 0
        xhat, rstd = _ln_stats(c_ref[...])
        gv = g_ref[...]
        cn = xhat * gv + b_ref[...]
        sg = _sigmoid(cn)
        dcn = dy_ref[...].astype(F32) * sg * (1.0 + cn * (1.0 - sg))
        _accumulate(dg_ref, jnp.sum(dcn * xhat, axis=0, keepdims=True), first)
        _accumulate(db_ref, jnp.sum(dcn, axis=0, keepdims=True), first)
        dc_ref[...] = _ln_bwd(dcn, xhat, rstd, gv)

    tile = pl.BlockSpec((NORM_TILE, d), lambda i: (i, 0))
    vec_shape = jax.ShapeDtypeStruct((1, d), F32)
    return pl.pallas_call(
        body, name=name, out_shape=(jax.ShapeDtypeStruct((s, d), F32), vec_shape, vec_shape),
        grid=(s // NORM_TILE,),
        in_specs=[tile, tile, _row_spec(d), _row_spec(d)] + [pl.BlockSpec(memory_space=pl.ANY)] * len(unread),
        out_specs=(tile, _row_spec(d), _row_spec(d)),
        compiler_params=_params(("arbitrary",)))(c, dy, ln_g, ln_b, *unread)


_D_BLOCK = COL_D // LANES


def _conv3(win, w):
    acc = jnp.zeros((CONV_ROWS, LANES), F32)
    for k in range(SC_KERNEL):
        acc = acc + w[k:k + 1, :] * _shift_rows(win, HALO - (SC_KERNEL - 1) + k, CONV_ROWS)
    return acc


def conv_d_fwd(proj, w_pad, name):
    s = proj.shape[0]

    def body(bg_ref, cg_ref, h_ref, w_ref, o_ref, ppad):
        ppad[0:HALO, :] = jnp.zeros((HALO, LANES), F32)
        ppad[HALO:, :] = cg_ref[...].astype(F32) * h_ref[...].astype(F32)
        w = w_ref[...]

        def block(bi, carry):
            start = pl.multiple_of(bi * CONV_ROWS, CONV_ROWS)
            cv = _conv3(ppad[pl.ds(start, CONV_ROWS + HALO), :], w)
            o_ref[pl.ds(start, CONV_ROWS), :] = (bg_ref[pl.ds(start, CONV_ROWS), :].astype(F32) * cv).astype(BF16)
            return carry

        lax.fori_loop(0, s // CONV_ROWS, block, 0)

    return pl.pallas_call(
        body, name=name, out_shape=jax.ShapeDtypeStruct((s, SC_WIDTH), BF16), grid=(_CH_TILES,),
        in_specs=[_col_spec(s, _D_BLOCK), _col_spec(s, _D_BLOCK + _CH_TILES), _col_spec(s, _D_BLOCK + 2 * _CH_TILES),
                  _col_spec(SUBLANES, 0)],
        out_specs=_col_spec(s, 0), scratch_shapes=[pltpu.VMEM((s + HALO, LANES), F32)],
        compiler_params=_params(("parallel",)))(proj, proj, proj, w_pad)


def conv_d_bwd(proj, w_pad, dy, name):
    s = proj.shape[0]

    def body(bg_ref, cg_ref, h_ref, w_ref, dy_ref, dbg_ref, dcg_ref, dh_ref, dw_ref, ppad, dpad, dw_scr):
        ppad[0:HALO, :] = jnp.zeros((HALO, LANES), F32)
        ppad[HALO:, :] = cg_ref[...].astype(F32) * h_ref[...].astype(F32)
        dpad[0:s, :] = dy_ref[...].astype(F32) * bg_ref[...].astype(F32)
        dpad[s:, :] = jnp.zeros((HALO, LANES), F32)
        dw_scr[...] = jnp.zeros((SUBLANES, LANES), F32)
        w = w_ref[...]

        def block(bi, carry):
            start = pl.multiple_of(bi * CONV_ROWS, CONV_ROWS)
            rows = pl.ds(start, CONV_ROWS)
            pwin = ppad[pl.ds(start, CONV_ROWS + HALO), :]
            dwin = dpad[pl.ds(start, CONV_ROWS + HALO), :]
            dcvb = dwin[:CONV_ROWS]
            dbg_ref[rows, :] = (dy_ref[rows, :].astype(F32) * _conv3(pwin, w)).astype(BF16)
            dp = jnp.zeros((CONV_ROWS, LANES), F32)
            for k in range(SC_KERNEL):
                dp = dp + w[k:k + 1, :] * _shift_rows(dwin, SC_KERNEL - 1 - k, CONV_ROWS)
                psh = _shift_rows(pwin, HALO - (SC_KERNEL - 1) + k, CONV_ROWS)
                dw_scr[k:k + 1, :] += jnp.sum(dcvb * psh, axis=0, keepdims=True)
            dcg_ref[rows, :] = (dp * h_ref[rows, :].astype(F32)).astype(BF16)
            dh_ref[rows, :] = (dp * cg_ref[rows, :].astype(F32)).astype(BF16)
            return carry

        lax.fori_loop(0, s // CONV_ROWS, block, 0)
        dw_ref[...] = dw_scr[...]

    act = jax.ShapeDtypeStruct((s, SC_WIDTH), BF16)
    return pl.pallas_call(
        body, name=name, out_shape=(act, act, act, jax.ShapeDtypeStruct((SUBLANES, SC_WIDTH), F32)),
        grid=(_CH_TILES,),
        in_specs=[_col_spec(s, _D_BLOCK), _col_spec(s, _D_BLOCK + _CH_TILES), _col_spec(s, _D_BLOCK + 2 * _CH_TILES),
                  _col_spec(SUBLANES, 0), _col_spec(s, 0)],
        out_specs=(_col_spec(s, 0), _col_spec(s, 0), _col_spec(s, 0), _col_spec(SUBLANES, 0)),
        scratch_shapes=[pltpu.VMEM((s + HALO, LANES), F32), pltpu.VMEM((s + HALO, LANES), F32),
                        pltpu.VMEM((SUBLANES, LANES), F32)],
        compiler_params=_params(("parallel",)))(proj, proj, proj, w_pad, dy)


_QK_BLOCK = COL_Q // LANES
_QK_BLOCKS = (Q_WIDTH + KV_WIDTH) // LANES


def _swap_halves(t):
    lane = lax.broadcasted_iota(jnp.int32, t.shape, 1)
    low = (lane % HEAD_DIM) < (HEAD_DIM // 2)
    return jnp.where(low, pltpu.roll(t, LANES - HEAD_DIM // 2, axis=1), pltpu.roll(t, HEAD_DIM // 2, axis=1))


def rope_fwd(proj, cos_t, sin_t, name):
    s = proj.shape[0]

    def body(t_ref, c_ref, s_ref, o_ref):
        t = t_ref[...].astype(F32)
        o_ref[...] = (t * c_ref[...] + _swap_halves(t) * s_ref[...]).astype(BF16)

    tr = min(ROPE_TILE, s)
    tab = pl.BlockSpec((tr, LANES), lambda i, j: (i, 0))
    return pl.pallas_call(
        body, name=name, out_shape=jax.ShapeDtypeStruct((s, Q_WIDTH + KV_WIDTH), BF16),
        grid=(s // tr, _QK_BLOCKS),
        in_specs=[pl.BlockSpec((tr, LANES), lambda i, j: (i, _QK_BLOCK + j)), tab, tab],
        out_specs=pl.BlockSpec((tr, LANES), lambda i, j: (i, j)),
        compiler_params=_params(("parallel", "parallel")))(proj, cos_t, sin_t)


def rope_bwd(d_cur, d_prev, cos_t, sin_t, name):
    s, w = d_cur.shape

    def body(a_ref, b_ref, c_ref, s_ref, o_ref):
        d = a_ref[...] + b_ref[...]
        o_ref[...] = (d * c_ref[...] + _swap_halves(d) * s_ref[...]).astype(BF16)

    tr = min(ROPE_TILE, s)
    tab = pl.BlockSpec((tr, LANES), lambda i, j: (i, 0))
    blk = pl.BlockSpec((tr, LANES), lambda i, j: (i, j))
    return pl.pallas_call(
        body, name=name, out_shape=jax.ShapeDtypeStruct((s, w), BF16), grid=(s // tr, w // LANES),
        in_specs=[blk, blk, tab, tab], out_specs=blk,
        compiler_params=_params(("parallel", "parallel")))(d_cur, d_prev, cos_t, sin_t)


_GROUP = N_Q_HEADS // N_KV_HEADS
_NEG = -1e30


def _attn_specs():
    q_spec = pl.BlockSpec((_GROUP, WINDOW, HEAD_DIM), lambda h, n: (h, n, 0))
    cur = pl.BlockSpec((1, WINDOW, HEAD_DIM), lambda h, n: (h, n, 0))
    prev = pl.BlockSpec((1, WINDOW, HEAD_DIM), lambda h, n: (h, jnp.maximum(n - 1, 0), 0))
    sink = pl.BlockSpec((_GROUP, 1, LANES), lambda h, n: (h, 0, 0))
    return q_spec, cur, prev, sink


def _attn_valid(n):
    qi = lax.broadcasted_iota(jnp.int32, (WINDOW, 2 * WINDOW), 0)
    kj = lax.broadcasted_iota(jnp.int32, (WINDOW, 2 * WINDOW), 1)
    delta = qi + WINDOW - kj
    return (delta >= 0) & (delta < WINDOW) & ((kj >= WINDOW) | (n > 0))


def _attn_probs(q, kcat, valid, sink_row):
    sc = lax.dot_general(q, kcat, _DIMS["nt"], preferred_element_type=F32) * (HEAD_DIM ** -0.5)
    sc = jnp.where(valid, sc, _NEG)
    sink = jnp.max(sink_row, axis=-1, keepdims=True)
    m = jnp.maximum(jnp.max(sc, axis=-1, keepdims=True), sink)
    p = jnp.where(valid, jnp.exp(sc - m), 0.0)
    es = jnp.exp(sink - m)
    inv = 1.0 / (jnp.sum(p, axis=-1, keepdims=True) + es)
    return p * inv, es * inv


def attention_fwd(qh, kh, vh, sinks_b, name, after=None):
    s = qh.shape[1]
    unread = () if after is None else (after,)

    def body(q_ref, kc_ref, kp_ref, vc_ref, vp_ref, sk_ref, *rest):
        o_ref = rest[-1]
        valid = _attn_valid(pl.program_id(1))
        kcat = jnp.concatenate([kp_ref[0], kc_ref[0]], axis=0)
        vcat = jnp.concatenate([vp_ref[0], vc_ref[0]], axis=0)
        for g in range(_GROUP):
            probs, _ = _attn_probs(q_ref[g], kcat, valid, sk_ref[g])
            o_ref[g] = jnp.dot(probs.astype(BF16), vcat, preferred_element_type=F32).astype(BF16)

    q_spec, cur, prev, sink = _attn_specs()
    return pl.pallas_call(
        body, name=name, out_shape=jax.ShapeDtypeStruct(qh.shape, BF16), grid=(N_KV_HEADS, s // WINDOW),
        in_specs=[q_spec, cur, prev, cur, prev, sink] + [pl.BlockSpec(memory_space=pl.ANY)] * len(unread),
        out_specs=q_spec, compiler_params=_params(("parallel", "parallel")))(qh, kh, kh, vh, vh, sinks_b, *unread)


def attention_bwd(qh, kh, vh, sinks_b, doh, name):
    s = qh.shape[1]

    def body(q_ref, kc_ref, kp_ref, vc_ref, vp_ref, sk_ref, do_ref, dq_ref, dkc_ref, dkp_ref, dvc_ref, dvp_ref, ds_ref):
        n = pl.program_id(1)
        valid = _attn_valid(n)
        kcat = jnp.concatenate([kp_ref[0], kc_ref[0]], axis=0)
        vcat = jnp.concatenate([vp_ref[0], vc_ref[0]], axis=0)
        dk = jnp.zeros((2 * WINDOW, HEAD_DIM), F32)
        dv = jnp.zeros((2 * WINDOW, HEAD_DIM), F32)
        for g in range(_GROUP):
            q = q_ref[g]
            do = do_ref[g]
            probs, ps = _attn_probs(q, kcat, valid, sk_ref[g])
            dprobs = lax.dot_general(do, vcat, _DIMS["nt"], preferred_element_type=F32)
            dv = dv + lax.dot_general(probs.astype(BF16), do, _DIMS["tn"], preferred_element_type=F32)
            rs = jnp.sum(probs * dprobs, axis=-1, keepdims=True)
            dsb = (probs * (dprobs - rs) * (HEAD_DIM ** -0.5)).astype(BF16)
            dq_ref[g] = jnp.dot(dsb, kcat, preferred_element_type=F32)
            dk = dk + lax.dot_general(dsb, q, _DIMS["tn"], preferred_element_type=F32)
            dsink = jnp.broadcast_to(-jnp.sum(ps * rs, axis=0, keepdims=True), (1, LANES))

            @pl.when(n == 0)
            def _():
                ds_ref[g] = dsink

            @pl.when(n > 0)
            def _():
                ds_ref[g] += dsink

        dkp_ref[0] = dk[:WINDOW]
        dkc_ref[0] = dk[WINDOW:]
        dvp_ref[0] = dv[:WINDOW]
        dvc_ref[0] = dv[WINDOW:]

    q_spec, cur, prev, sink = _attn_specs()
    kv_shape = jax.ShapeDtypeStruct(kh.shape, F32)
    return pl.pallas_call(
        body, name=name,
        out_shape=(jax.ShapeDtypeStruct(qh.shape, F32), kv_shape, kv_shape, kv_shape, kv_shape,
                   jax.ShapeDtypeStruct(sinks_b.shape, F32)),
        grid=(N_KV_HEADS, s // WINDOW), in_specs=[q_spec, cur, prev, cur, prev, sink, q_spec],
        out_specs=(q_spec, cur, cur, cur, cur, sink),
        compiler_params=_params(("parallel", "arbitrary")))(qh, kh, kh, vh, vh, sinks_b, doh)


def _to_heads(t, heads):
    return t.reshape(t.shape[0], heads, HEAD_DIM).transpose(1, 0, 2)


def _from_heads(t):
    return t.transpose(1, 0, 2).reshape(t.shape[1], t.shape[0] * HEAD_DIM)


def _shift_window(t):
    return jnp.concatenate([t[:, WINDOW:], jnp.zeros_like(t[:, :WINDOW])], axis=1)


def merge_fwd(zg, branches, name):
    s = zg.shape[0]

    def body(zg_ref, b0, b1, b2, b3, o_ref):
        acc = jnp.zeros((TOKEN_TILE, D_MODEL), F32)
        for n, b_ref in enumerate((b0, b1, b2, b3)):
            gate = _sigmoid(zg_ref[:, n * D_MODEL:(n + 1) * D_MODEL].astype(F32))
            acc = acc + gate * b_ref[...].astype(F32)
        o_ref[...] = acc.astype(BF16)

    tile = pl.BlockSpec((TOKEN_TILE, D_MODEL), lambda i: (i, 0))
    wide = pl.BlockSpec((TOKEN_TILE, N_BRANCH * D_MODEL), lambda i: (i, 0))
    return pl.pallas_call(
        body, name=name, out_shape=jax.ShapeDtypeStruct((s, D_MODEL), BF16), grid=(s // TOKEN_TILE,),
        in_specs=[wide, tile, tile, tile, tile], out_specs=tile,
        compiler_params=_params(("parallel",)))(zg, *branches)


def merge_bwd(zg, branches, dm, name):
    s = zg.shape[0]

    def body(zg_ref, b0, b1, b2, b3, dm_ref, dzg_ref, d0, d1, d2, d3):
        dmv = dm_ref[...].astype(F32)
        for n, (b_ref, d_ref) in enumerate(((b0, d0), (b1, d1), (b2, d2), (b3, d3))):
            cols = slice(n * D_MODEL, (n + 1) * D_MODEL)
            gate = _sigmoid(zg_ref[:, cols].astype(F32))
            d_ref[...] = (gate * dmv).astype(BF16)
            dzg_ref[:, cols] = (dmv * b_ref[...].astype(F32) * gate * (1.0 - gate)).astype(BF16)

    tile = pl.BlockSpec((TOKEN_TILE, D_MODEL), lambda i: (i, 0))
    wide = pl.BlockSpec((TOKEN_TILE, N_BRANCH * D_MODEL), lambda i: (i, 0))
    act = jax.ShapeDtypeStruct((s, D_MODEL), BF16)
    return pl.pallas_call(
        body, name=name, out_shape=(jax.ShapeDtypeStruct((s, N_BRANCH * D_MODEL), BF16), act, act, act, act),
        grid=(s // TOKEN_TILE,), in_specs=[wide, tile, tile, tile, tile, tile],
        out_specs=(wide, tile, tile, tile, tile), compiler_params=_params(("parallel",)))(zg, *branches, dm)


def swiglu_fwd(gu, name):
    s = gu.shape[0]

    def body(g_ref, u_ref, o_ref):
        gate = g_ref[...].astype(F32)
        o_ref[...] = (gate * _sigmoid(gate) * u_ref[...].astype(F32)).astype(BF16)

    return pl.pallas_call(
        body, name=name, out_shape=jax.ShapeDtypeStruct((s, D_FF), BF16), grid=(s // TOKEN_TILE,),
        in_specs=[pl.BlockSpec((TOKEN_TILE, D_FF), lambda i: (i, 0)), pl.BlockSpec((TOKEN_TILE, D_FF), lambda i: (i, 1))],
        out_specs=pl.BlockSpec((TOKEN_TILE, D_FF), lambda i: (i, 0)), compiler_params=_params(("parallel",)))(gu, gu)


def swiglu_bwd(gu, dact, name):
    s = gu.shape[0]

    def body(g_ref, u_ref, da_ref, o_ref):
        gate = g_ref[...].astype(F32)
        sg = _sigmoid(gate)
        da = da_ref[...].astype(F32)
        o_ref[:, :D_FF] = (da * u_ref[...].astype(F32) * sg * (1.0 + gate * (1.0 - sg))).astype(BF16)
        o_ref[:, D_FF:] = (da * gate * sg).astype(BF16)

    half = pl.BlockSpec((TOKEN_TILE, D_FF), lambda i: (i, 0))
    return pl.pallas_call(
        body, name=name, out_shape=jax.ShapeDtypeStruct((s, 2 * D_FF), BF16), grid=(s // TOKEN_TILE,),
        in_specs=[half, pl.BlockSpec((TOKEN_TILE, D_FF), lambda i: (i, 1)), half],
        out_specs=pl.BlockSpec((TOKEN_TILE, 2 * D_FF), lambda i: (i, 0)),
        compiler_params=_params(("parallel",)))(gu, gu, dact)


ADAMW_BLOCK_BYTES = 1 << 20


def add_partials(a, b, name):
    q, r, c = a.shape
    tr = _divisor_tile(r, max(SUBLANES, ADAMW_BLOCK_BYTES // (2 * c)), SUBLANES)

    def body(a_ref, b_ref, o_ref):
        o_ref[...] = (a_ref[...].astype(F32) + b_ref[...].astype(F32)).astype(a.dtype)

    tile = pl.BlockSpec((1, tr, c), lambda i, j: (i, j, 0))
    return pl.pallas_call(
        body, name=name, out_shape=jax.ShapeDtypeStruct(a.shape, a.dtype), grid=(q, r // tr),
        in_specs=[tile, tile], out_specs=tile, compiler_params=_params(("parallel", "parallel")))(a, b)


def adamw(parts, w, m, v, name):
    n_parts, r, c = w.shape
    tr = _divisor_tile(r, max(SUBLANES, ADAMW_BLOCK_BYTES // (4 * c)), SUBLANES)
    tiles = r // tr

    def part_spec(j):
        return pl.BlockSpec((parts[j].shape[0], tr, c), lambda i: (0, jnp.clip(i - j * tiles, 0, tiles - 1), 0))

    def body(*refs):
        p_refs = refs[:n_parts]
        w_ref, m_ref, v_ref, g_ref, d_ref, nm_ref, nv_ref = refs[n_parts:]
        which = pl.program_id(0) // tiles
        g = None
        for j, p_ref in enumerate(p_refs):
            gj = p_ref[0].astype(F32)
            for i in range(1, p_ref.shape[0]):
                gj = gj + p_ref[i].astype(F32)
            g = gj if g is None else jnp.where(which == j, gj, g)
        nm = ADAM_B1 * m_ref[0] + (1.0 - ADAM_B1) * g
        nv = ADAM_B2 * v_ref[0] + (1.0 - ADAM_B2) * (g * g)
        m_hat = nm / (1.0 - ADAM_B1 ** ADAM_STEP)
        v_hat = nv / (1.0 - ADAM_B2 ** ADAM_STEP)
        g_ref[0] = g
        d_ref[0] = -ADAM_LR * (m_hat / (jnp.sqrt(v_hat) + ADAM_EPS) + ADAM_WD * w_ref[0])
        nm_ref[0] = nm
        nv_ref[0] = nv

    tile = pl.BlockSpec((1, tr, c), lambda i: (i // tiles, i % tiles, 0))
    shape = jax.ShapeDtypeStruct(w.shape, F32)
    return pl.pallas_call(
        body, name=name, out_shape=(shape, shape, shape, shape), grid=(n_parts * tiles,),
        in_specs=[part_spec(j) for j in range(n_parts)] + [tile, tile, tile],
        out_specs=(tile, tile, tile, tile), compiler_params=_params(("parallel",)))(*parts, w, m, v)


_RELATIONS = [(a, b, e) for a in (0, 1) for b in (0, 1) for e in (0, 1)][1:]


_HBM_SPEC = pl.BlockSpec(memory_space=pltpu.HBM)
_SEM_SPEC = pl.BlockSpec(memory_space=pltpu.SEMAPHORE)
_ANY_SPEC = pl.BlockSpec(memory_space=pl.ANY)
_DATAFLOW = pltpu.SideEffectType.DATAFLOW_SIDE_EFFECTING


_OTHER_CHIPS = [(1, 0), (0, 1), (1, 1)]
_FIRST_LEVEL = [(0, 0, 1)] + [(a, b, 0) for a, b in _OTHER_CHIPS]


def _remote_copies(ins, lands, send_sems, recv_sems, scatter, relations):
    x, y, c = lax.axis_index("x"), lax.axis_index("y"), lax.axis_index("c")
    me = 4 * x + 2 * y + c
    copies = []
    for t in range(len(ins)):
        for k, (a, b, e) in enumerate(relations):
            px, py, pc = (x + a) % 2, (y + b) % 2, (c + e) % 2
            src = ins[t].at[4 * px + 2 * py + pc] if scatter[t] else ins[t]
            copies.append(pltpu.make_async_remote_copy(
                src_ref=src, dst_ref=lands[t].at[me], send_sem=send_sems.at[t * len(relations) + k],
                recv_sem=recv_sems.at[t * len(relations) + k],
                device_id=(px, py, pc), device_id_type=pl.DeviceIdType.MESH))
    return copies


def _forward_copies(lands, send_sems, recv_sems):
    x, y, c = lax.axis_index("x"), lax.axis_index("y"), lax.axis_index("c")
    copies = []
    for t in range(len(lands)):
        for k, (a, b) in enumerate(_OTHER_CHIPS):
            slot = lands[t].at[4 * ((x + a) % 2) + 2 * ((y + b) % 2) + c]
            copies.append(pltpu.make_async_remote_copy(
                src_ref=slot, dst_ref=slot, send_sem=send_sems.at[t * len(_OTHER_CHIPS) + k],
                recv_sem=recv_sems.at[t * len(_OTHER_CHIPS) + k],
                device_id=(x, y, 1 - c), device_id_type=pl.DeviceIdType.MESH))
    return copies


def _place_own(landed, own, me):
    return lax.dynamic_update_index_in_dim(landed, own, me, 0)


N_CHIPS = 4


def _to_sibling_plan(ins, lands, send_sems, recv_sems):
    x, y, c = lax.axis_index("x"), lax.axis_index("y"), lax.axis_index("c")
    return [pltpu.make_async_remote_copy(
        src_ref=ins[0].at[2 * q + (1 - c)], dst_ref=lands[0].at[q], send_sem=send_sems.at[q], recv_sem=recv_sems.at[q],
        device_id=(x, y, 1 - c), device_id_type=pl.DeviceIdType.MESH) for q in range(N_CHIPS)]


def _to_chips_plan(ins, lands, send_sems, recv_sems):
    x, y, c = lax.axis_index("x"), lax.axis_index("y"), lax.axis_index("c")
    copies = []
    for k, (a, b) in enumerate(_OTHER_CHIPS):
        px, py = (x + a) % 2, (y + b) % 2
        copies.append(pltpu.make_async_remote_copy(
            src_ref=ins[0].at[2 * px + py], dst_ref=lands[0].at[2 * x + y], send_sem=send_sems.at[k],
            recv_sem=recv_sems.at[k], device_id=(px, py, c), device_id_type=pl.DeviceIdType.MESH))
    return copies


def exchange_start(arrays, scatter, name, after=None, relations=_RELATIONS, plan=None, land_shapes=None):
    n = len(arrays)
    n_rel = len(relations)
    if land_shapes is None:
        land_shapes = [a.shape if scatter[t] else (N_DEV,) + a.shape for t, a in enumerate(arrays)]
    zones = [lax.empty(s, a.dtype) for s, a in zip(land_shapes, arrays)]
    if plan is None:
        def plan(ins, lands, send_sems, recv_sems):
            return _remote_copies(ins, lands, send_sems, recv_sems, scatter, relations)

    def body(*refs):
        ins, lands = refs[:n], refs[n:2 * n]
        send_sems, recv_sems = refs[-2 * n - 3], refs[-2 * n - 2]
        token = refs[-1]
        for cp in plan(ins, lands, send_sems, recv_sems):
            cp.start()
        token[...] = jnp.zeros_like(token)

    sems = pltpu.SemaphoreType.DMA((n * n_rel,))
    out_shape = ((sems, sems) + tuple(pltpu.HBM(a.shape, a.dtype) for a in arrays)
                 + tuple(pltpu.HBM(s, a.dtype) for s, a in zip(land_shapes, arrays))
                 + (jax.ShapeDtypeStruct((SUBLANES, LANES), F32),))
    operands = [pltpu.with_memory_space_constraint(a, pltpu.HBM) for a in arrays]
    operands += [pltpu.with_memory_space_constraint(z, pltpu.HBM) for z in zones]
    in_specs = [_HBM_SPEC] * (2 * n)
    if after is not None:
        operands.append(after)
        in_specs.append(_ANY_SPEC)
    res = pl.pallas_call(
        body, name=name, out_shape=out_shape, in_specs=in_specs,
        out_specs=(_SEM_SPEC, _SEM_SPEC) + (_HBM_SPEC,) * (2 * n) + (pl.BlockSpec(memory_space=pltpu.VMEM),),
        input_output_aliases={i: 2 + i for i in range(2 * n)},
        compiler_params=pltpu.CompilerParams(has_side_effects=_DATAFLOW))(*operands)
    handle = (res[0], res[1], res[2:2 + n], res[2 + n:2 + 2 * n], plan)
    return handle, res[-1]


def exchange_wait(handle, after, name):
    send_sems, recv_sems, sources, lands, plan = handle
    n = len(sources)

    def body(*refs):
        ins, lzs = refs[:n], refs[n:2 * n]
        send_ref, recv_ref = refs[2 * n], refs[2 * n + 1]
        for cp in plan(ins, lzs, send_ref, recv_ref):
            cp.wait_send()
            cp.wait_recv()

    out_shape = (tuple(pltpu.HBM(a.shape, a.dtype) for a in sources) + tuple(pltpu.HBM(a.shape, a.dtype) for a in lands))
    res = pl.pallas_call(
        body, name=name, out_shape=out_shape, in_specs=[_HBM_SPEC] * (2 * n) + [_SEM_SPEC, _SEM_SPEC, _ANY_SPEC],
        out_specs=(_HBM_SPEC,) * (2 * n), input_output_aliases={i: i for i in range(2 * n)},
        compiler_params=pltpu.CompilerParams(has_side_effects=_DATAFLOW))(*sources, *lands, send_sems, recv_sems, after)
    return res[n:], res[:n]


def forward_start(lands, name):
    n = len(lands)

    def body(*refs):
        send_sems, recv_sems, token = refs[n], refs[n + 1], refs[-1]
        for cp in _forward_copies(refs[:n], send_sems, recv_sems):
            cp.start()
        token[...] = jnp.zeros_like(token)

    sems = pltpu.SemaphoreType.DMA((n * len(_OTHER_CHIPS),))
    res = pl.pallas_call(
        body, name=name,
        out_shape=(sems, sems) + tuple(pltpu.HBM(a.shape, a.dtype) for a in lands)
        + (jax.ShapeDtypeStruct((SUBLANES, LANES), F32),),
        in_specs=[_HBM_SPEC] * n,
        out_specs=(_SEM_SPEC, _SEM_SPEC) + (_HBM_SPEC,) * n + (pl.BlockSpec(memory_space=pltpu.VMEM),),
        input_output_aliases={i: 2 + i for i in range(n)},
        compiler_params=pltpu.CompilerParams(has_side_effects=_DATAFLOW))(*lands)
    return (res[0], res[1], res[2:2 + n]), res[-1]


def forward_wait(handle, after, name):
    send_sems, recv_sems, lands = handle
    n = len(lands)

    def body(*refs):
        for cp in _forward_copies(refs[:n], refs[n], refs[n + 1]):
            cp.wait_send()
            cp.wait_recv()

    return pl.pallas_call(
        body, name=name, out_shape=tuple(pltpu.HBM(a.shape, a.dtype) for a in lands),
        in_specs=[_HBM_SPEC] * n + [_SEM_SPEC, _SEM_SPEC, _ANY_SPEC], out_specs=(_HBM_SPEC,) * n,
        input_output_aliases={i: i for i in range(n)},
        compiler_params=pltpu.CompilerParams(has_side_effects=_DATAFLOW))(*lands, send_sems, recv_sems, after)


_SMALL = ("norm_mix", "sg_ln_g", "sg_ln_b", "sg_b", "cv_b", "cv_ln_g", "cv_ln_b", "attn_sinks", "norm_ffn",
          "norm_final")
_PACK_UNIT = SUBLANES * LANES


def _pack(entries):
    flat = []
    for entry in entries:
        parts = [t.reshape(-1) for t in (entry if isinstance(entry, (list, tuple)) else [entry])]
        size = sum(t.shape[0] for t in parts)
        flat += parts + [jnp.zeros(((-size) % _PACK_UNIT,), parts[0].dtype)]
    return jnp.concatenate(flat).reshape(-1, LANES)


def _unpack(packed, like):
    out, row = [], 0
    for t in like:
        size = 1
        for d in t.shape:
            size *= d
        rows = -(-size // _PACK_UNIT) * SUBLANES
        out.append(packed[row:row + rows].reshape(-1)[:size].reshape(t.shape))
        row += rows
    return out


def _layer_fwd(l, x, p, late_params, mid_hook=None, ffn_hook=None):
    tag = f"l{l}_"
    xn = rmsnorm_fwd(x, p["norm_mix"], tag + "norm_mix")
    proj = matmul(xn, p["w_in_t"], "nt", BF16, tag + "proj_a", tm_cap=1024, tn_cap=2176, b_rows=PROJ_A)
    zg = matmul(xn, p["w_in_t_g"], "nt", BF16, tag + "proj_g", tm_cap=1024, tn_cap=2048)
    y_a = mixer_a_fwd(proj, p["sg_ln_g"], p["sg_ln_b"], p["sg_w"], p["sg_b"], tag + "mix_a")
    conv = conv_b_fwd(proj, p["cv_w"], p["cv_b"], tag + "conv_b")
    y_b = ln_silu_fwd(conv, p["cv_ln_g"], p["cv_ln_b"], tag + "ln_silu")
    y_d = conv_d_fwd(proj, p["sc_w"], tag + "conv_d")
    qk = rope_fwd(proj, p["cos"], p["sin"], tag + "rope")
    qh = _to_heads(qk[:, :Q_WIDTH], N_Q_HEADS)
    kh = _to_heads(qk[:, Q_WIDTH:], N_KV_HEADS)
    vh = _to_heads(proj[:, COL_V:COL_D], N_KV_HEADS)
    token = mid_hook(qh) if mid_hook is not None else None
    oh = attention_fwd(qh, kh, vh, p["sinks"], tag + "attn", after=token)
    y_c = _from_heads(oh)
    ys = (y_a, y_b, y_c, y_d)
    p = {**p, **late_params(y_c)}
    branches = tuple(matmul(ys[n], p["w_branch"][n], "nn", BF16, tag + f"branch{n}", tm_cap=1024, tn_cap=1024)
                     for n in range(N_BRANCH))
    merged = merge_fwd(zg, branches, tag + "merge")
    x_mid = matmul(merged, p["w_out"], "nn", F32, tag + "out", add=x, tm_cap=1024, tn_cap=1024)
    token = ffn_hook(x_mid) if ffn_hook is not None else None
    hn = rmsnorm_fwd(x_mid, p["norm_ffn"], tag + "norm_ffn")
    gu = matmul(hn, p["w_gate_up_t"], "nt", BF16, tag + "gate_up", tm_cap=512, tn_cap=2816, after=token)
    act = swiglu_fwd(gu, tag + "swiglu")
    x_out = matmul(act, p["w_down"], "nn", F32, tag + "down", add=x_mid, tm_cap=512, tn_cap=1024)
    saved = dict(x=x, xn=xn, proj=proj, zg=zg, conv=conv, qh=qh, kh=kh, vh=vh, ys=ys, branches=branches,
                 merged=merged, x_mid=x_mid, hn=hn, gu=gu, act=act)
    return x_out, saved, p


def _layer_bwd(l, dx_out, p, sv, emit, after=None):
    tag = f"l{l}_b_"
    g = {}
    dact = matmul(dx_out, p["w_down"], "nt", BF16, tag + "dact", after=after, tm_cap=512, tn_cap=2816)
    dw_down = matmul(sv["act"], dx_out, "tn", BF16, tag + "dw_down", tm_cap=1408, tn_cap=512)
    dgu = swiglu_bwd(sv["gu"], dact, tag + "swiglu")
    dhn = matmul(dgu, p["w_gate_up_t"], "nn", BF16, tag + "dhn", tm_cap=512, tn_cap=512)
    dw_gate_up = matmul(dgu, sv["hn"], "tn", BF16, tag + "dw_gate_up", tm_cap=1408, tn_cap=1024)
    token = emit("a", {"w_gate_up": dw_gate_up, "w_down": dw_down})
    dx_mid, g["norm_ffn"] = rmsnorm_bwd(sv["x_mid"], p["norm_ffn"], dhn, dx_out, tag + "norm_ffn")
    dmerged = matmul(dx_mid, p["w_out"], "nt", BF16, tag + "dmerged", after=token, tm_cap=1024, tn_cap=1024)
    dw_out = matmul(sv["merged"], dx_mid, "tn", BF16, tag + "dw_out", tm_cap=1024, tn_cap=512)
    dzg, *dbranches = merge_bwd(sv["zg"], sv["branches"], dmerged, tag + "merge")
    dys = [matmul(dbranches[n], p["w_branch"][n], "nt", BF16, tag + f"dy{n}", tm_cap=1024, tn_cap=512)
           for n in range(N_BRANCH)]
    dw_branch = jnp.stack(
        [matmul(sv["ys"][n], dbranches[n], "tn", BF16, tag + f"dw_branch{n}", tm_cap=512, tn_cap=1024)
         for n in range(N_BRANCH)])
    proj = sv["proj"]
    dz_a, g["sg_ln_g"], g["sg_ln_b"], dsw_a, dsb = mixer_a_bwd(
        proj, dys[0], p["sg_ln_g"], p["sg_ln_b"], p["sg_w"], p["sg_b"], tag + "mix_a")
    g["sg_b"] = dsb[:, :, 0]
    token = emit("b", {"w_branch": dw_branch, "w_out": dw_out}, {"sg_w": dsw_a})
    dconv, g["cv_ln_g"], g["cv_ln_b"] = ln_silu_bwd(sv["conv"], dys[1], p["cv_ln_g"], p["cv_ln_b"], tag + "ln_silu",
                                                    after=token)
    da, dgate, dcw, g["cv_b"] = conv_b_bwd(proj, p["cv_w"], dconv, tag + "conv_b")
    g["cv_w"] = dcw[:CV_KERNEL]
    doh = _to_heads(dys[2], N_Q_HEADS)
    dqh, dkc, dkp, dvc, dvp, dsk = attention_bwd(sv["qh"], sv["kh"], sv["vh"], p["sinks"], doh, tag + "attn")
    g["attn_sinks"] = dsk[:, 0, 0]
    dqk_cur = jnp.concatenate([_from_heads(dqh), _from_heads(dkc)], axis=1)
    dqk_prev = jnp.concatenate([jnp.zeros((SEQ, Q_WIDTH), F32), _from_heads(_shift_window(dkp))], axis=1)
    dqk = rope_bwd(dqk_cur, dqk_prev, p["cos"], -p["sin"], tag + "rope")
    dv = (_from_heads(dvc) + _from_heads(_shift_window(dvp))).astype(BF16)
    dbg, dcg, dh, dsw = conv_d_bwd(proj, p["sc_w"], dys[3], tag + "conv_d")
    g["sc_w"] = dsw[:SC_KERNEL]
    dproj = jnp.concatenate([dz_a, da, dgate, dqk, dv, dbg, dcg, dh], axis=1)
    dw_in = matmul(dproj, sv["xn"], "tn", BF16, tag + "dw_in_a", after=token, tm_cap=2176, tn_cap=512,
                   out_rows=PROJ_WIDTH)
    g["w_in"] = matmul(dzg, sv["xn"], "tn", BF16, tag + "dw_in_g", tm_cap=256, tn_cap=1024, into=dw_in,
                       into_row=PROJ_A)
    token = emit("c", {n: g.pop(n) for n in _EARLY})
    dxn = matmul(dproj, p["w_in_t"], "nn", F32, tag + "dxn_a", after=token, tm_cap=512, tn_cap=512, b_rows=PROJ_A)
    dxn = matmul(dzg, p["w_in_t_g"], "nn", F32, tag + "dxn_g", add=dxn, tm_cap=512, tn_cap=512)
    dx_in, g["norm_mix"] = rmsnorm_bwd(sv["x"], p["norm_mix"], dxn, dx_mid, tag + "norm_mix")
    return dx_in, g, token


_EARLY = ("w_in", "cv_w", "sc_w")
_LATE = ("w_branch", "w_out", "w_gate_up", "w_down")


_TRANSPOSED = ("w_in", "w_gate_up")


def _shard_view(name, t):
    return jnp.swapaxes(t, 1, 2) if name in _TRANSPOSED else t


def _full_weight(name, t):
    if name in ("w_out", "w_down") + _TRANSPOSED:
        return t.reshape(-1, t.shape[-1])
    if name == "w_branch":
        return t.transpose(1, 2, 0, 3).reshape(N_BRANCH, SG_WIDTH, D_MODEL)
    return t.transpose(1, 0, 2).reshape(t.shape[1], -1)


def _to_blocks(name, full):
    if name in ("w_out", "w_down") + _TRANSPOSED:
        return full.reshape(N_DEV, -1, full.shape[-1])
    if name == "w_branch":
        return full.reshape(N_BRANCH, SG_WIDTH, N_DEV, -1).transpose(2, 0, 1, 3)
    return full.reshape(full.shape[0], N_DEV, -1).transpose(1, 0, 2)


def _rope_tables():
    pos = jnp.arange(SEQ, dtype=F32)
    inv_freq = 1.0 / (ROPE_THETA ** (jnp.arange(0, HEAD_DIM, 2, dtype=F32) / HEAD_DIM))
    ang = pos[:, None] * inv_freq[None, :]
    cos, sin = jnp.cos(ang), jnp.sin(ang)
    reps = LANES // HEAD_DIM
    return jnp.tile(jnp.concatenate([cos, cos], axis=1), (1, reps)), jnp.tile(jnp.concatenate([-sin, sin], axis=1), (1, reps))


def kernel(x, norm_mix, w_in, sg_ln_g, sg_ln_b, sg_w, sg_b, cv_w, cv_b, cv_ln_g, cv_ln_b, attn_sinks, sc_w, w_branch, w_out, norm_ffn, w_gate_up, w_down, norm_final, loss_target, m_norm_mix, m_w_in, m_sg_ln_g, m_sg_ln_b, m_sg_w, m_sg_b, m_cv_w, m_cv_b, m_cv_ln_g, m_cv_ln_b, m_attn_sinks, m_sc_w, m_w_branch, m_w_out, m_norm_ffn, m_w_gate_up, m_w_down, m_norm_final, v_norm_mix, v_w_in, v_sg_ln_g, v_sg_ln_b, v_sg_w, v_sg_b, v_cv_w, v_cv_b, v_cv_ln_g, v_cv_ln_b, v_attn_sinks, v_sc_w, v_w_branch, v_w_out, v_norm_ffn, v_w_gate_up, v_w_down, v_norm_final):
    names = ("norm_mix", "w_in", "sg_ln_g", "sg_ln_b", "sg_w", "sg_b", "cv_w", "cv_b", "cv_ln_g", "cv_ln_b",
             "attn_sinks", "sc_w", "w_branch", "w_out", "norm_ffn", "w_gate_up", "w_down", "norm_final")
    w = dict(zip(names, (norm_mix, w_in, sg_ln_g, sg_ln_b, sg_w, sg_b, cv_w, cv_b, cv_ln_g, cv_ln_b, attn_sinks,
                         sc_w, w_branch, w_out, norm_ffn, w_gate_up, w_down, norm_final)))
    m = dict(zip(names, (m_norm_mix, m_w_in, m_sg_ln_g, m_sg_ln_b, m_sg_w, m_sg_b, m_cv_w, m_cv_b, m_cv_ln_g,
                         m_cv_ln_b, m_attn_sinks, m_sc_w, m_w_branch, m_w_out, m_norm_ffn, m_w_gate_up, m_w_down,
                         m_norm_final)))
    v = dict(zip(names, (v_norm_mix, v_w_in, v_sg_ln_g, v_sg_ln_b, v_sg_w, v_sg_b, v_cv_w, v_cv_b, v_cv_ln_g,
                         v_cv_ln_b, v_attn_sinks, v_sc_w, v_w_branch, v_w_out, v_norm_ffn, v_w_gate_up, v_w_down,
                         v_norm_final)))

    me = 4 * lax.axis_index("x") + 2 * lax.axis_index("y") + lax.axis_index("c")
    groups = [(l, group) for l in range(DEPTH) for group in (_EARLY, _LATE)]
    shards = {(l, group): [_shard_view(n, w[n])[l].astype(BF16) for n in group] for l, group in groups}
    gathers, forwards, own_shards, token = {}, {}, {}, None
    for l, group in groups:
        gathers[(l, group)], token = exchange_start(
            shards[(l, group)], [False] * len(group), f"gather_start{l}_{group[0]}", after=token,
            relations=_FIRST_LEVEL)

    def begin_forward(l, group, after):
        landed, own_shards[(l, group)] = exchange_wait(gathers[(l, group)], after, f"gather_wait{l}_{group[0]}")
        forwards[(l, group)], tok = forward_start(landed, f"forward_start{l}_{group[0]}")
        return tok

    def landed_weights(l, group, after):
        landed = forward_wait(forwards[(l, group)], after, f"forward_wait{l}_{group[0]}")
        return {n + "_t" if n in _TRANSPOSED else n: _full_weight(n, _place_own(t, own, me))
                for n, t, own in zip(group, landed, own_shards[(l, group)])}

    cos_t, sin_t = _rope_tables()

    def early_params(l, after):
        full = landed_weights(l, _EARLY, after)
        return dict(
            norm_mix=w["norm_mix"][l][None], norm_ffn=w["norm_ffn"][l][None],
            w_in_t=full["w_in_t"], w_in_t_g=full["w_in_t"][PROJ_A:],
            sg_ln_g=w["sg_ln_g"][l][None], sg_ln_b=w["sg_ln_b"][l][None], sg_w=w["sg_w"][l],
            sg_b=jnp.broadcast_to(w["sg_b"][l][:, :, None], (SG_GROUPS, SG_CHUNK, LANES)),
            cv_w=jnp.pad(full["cv_w"].astype(F32), ((0, HALO - CV_KERNEL), (0, 0))),
            cv_b=w["cv_b"][l][None], cv_ln_g=w["cv_ln_g"][l][None], cv_ln_b=w["cv_ln_b"][l][None],
            sinks=jnp.broadcast_to(w["attn_sinks"][l][:, None, None], (N_Q_HEADS, 1, LANES)),
            sc_w=jnp.pad(full["sc_w"].astype(F32), ((0, SUBLANES - SC_KERNEL), (0, 0))),
            cos=cos_t, sin=sin_t)

    params, saved = [None] * DEPTH, [None] * DEPTH
    h = x[0]
    after = begin_forward(0, _EARLY, token)
    for l in range(DEPTH):
        h, saved[l], params[l] = _layer_fwd(
            l, h, early_params(l, after), lambda behind, l=l: landed_weights(l, _LATE, behind),
            mid_hook=lambda behind, l=l: begin_forward(l, _LATE, behind),
            ffn_hook=(lambda behind, l=l: begin_forward(l + 1, _EARLY, behind)) if l + 1 < DEPTH else None)
        after = h
    loss_row, dh, d_norm_final = loss_head(h, w["norm_final"][None], loss_target[0], "loss_head")

    sent = {}

    two_level = {}

    def reduce_scatter_two_level(name, grad):
        blocks = _to_blocks(name, grad.astype(BF16))
        stage_shape = (N_CHIPS,) + blocks.shape[1:]
        handle, tok = exchange_start([blocks], None, "rs_start_sibling", plan=_to_sibling_plan,
                                     land_shapes=[stage_shape])
        (from_sibling,), (blocks,) = exchange_wait(handle, tok, "rs_wait_sibling")
        mine = lax.dynamic_index_in_dim(blocks.reshape((N_CHIPS, 2) + blocks.shape[1:]), lax.axis_index("c"), 1,
                                        keepdims=False)
        partial = add_partials(mine, from_sibling, "rs_add")
        two_level[name], tok = exchange_start([partial], None, "rs_start_chips", plan=_to_chips_plan,
                                              land_shapes=[stage_shape])
        return tok

    def emitter(l):
        def emit(group, grads_of, replicated=None):
            replicated = replicated or {}
            behind_tok = None
            if l == 0 and group == "c":
                grads_of = dict(grads_of)
                behind_tok = reduce_scatter_two_level("w_in", grads_of.pop("w_in"))
            send = [_to_blocks(n, grads_of[n].astype(BF16)) for n in grads_of] + list(replicated.values())
            flags = [True] * len(grads_of) + [False] * len(replicated)
            handle, tok = exchange_start(send, flags, f"grads_start{l}{group}", after=behind_tok)
            sent[(l, group)] = (handle, tuple(grads_of) + tuple(replicated), flags)
            return tok
        return emit

    grads = [None] * DEPTH
    token = None
    for l in reversed(range(DEPTH)):
        dh, grads[l], token = _layer_bwd(l, dh, params[l], saved[l], emitter(l), after=token)
    grad_x = dh[None]

    small_grads = [[d_norm_final] if n == "norm_final" else [grads[l][n] for l in range(DEPTH)] for n in _SMALL]
    no_state = jnp.zeros((1,), F32)
    small_like = [w[n] for n in _SMALL] + [no_state]
    small_part = _pack(small_grads + [loss_row[0, :1]])
    handle_small, token = exchange_start([small_part], [False], "grads_start_small", after=token)

    def received(l, group, after):
        handle, group_names, flags = sent[(l, group)]
        landed, sources = exchange_wait(handle, after, f"grads_wait{l}{group}")
        return {n: _place_own(t, lax.dynamic_index_in_dim(s, me, 0, keepdims=False) if scattered else s, me)
                for n, t, s, scattered in zip(group_names, landed, sources, flags)}

    out_g, out_d, out_m, out_v = {}, {}, {}, {}

    def update(n, by_layer):
        shape = _shard_view(n, w[n]).shape
        view = (DEPTH, w[n].size // (DEPTH * shape[-1]), shape[-1])
        parts = [t.reshape(t.shape[:1] + view[1:]) for t in by_layer]
        res = adamw(parts, *[_shard_view(n, t).reshape(view) for t in (w[n], m[n], v[n])], "adamw_" + n)
        out_g[n], out_d[n], out_m[n], out_v[n] = (_shard_view(n, t.reshape(shape)) for t in res)
        return res[0]

    behind = token
    for group in ("a", "b", "c"):
        r1 = received(1, group, behind)
        r0 = received(0, group, next(iter(r1.values())))
        for n in r0:
            behind = update(n, [r0[n], r1[n]])
    (by_chip,), (partial,) = exchange_wait(two_level["w_in"], behind, "rs_wait_chips")
    chip = 2 * lax.axis_index("x") + lax.axis_index("y")
    by_chip = _place_own(by_chip, lax.dynamic_index_in_dim(partial, chip, 0, keepdims=False), chip)
    behind = update("w_in", [by_chip, r1["w_in"]])
    landed, sources = exchange_wait(handle_small, behind, "grads_wait_small")
    res = adamw([_place_own(landed[0], sources[0], me)], _pack(small_like)[None],
                _pack([m[n] for n in _SMALL] + [no_state])[None], _pack([v[n] for n in _SMALL] + [no_state])[None],
                "adamw_small")
    for store, packed in zip((out_g, out_d, out_m, out_v), res):
        for n, t in zip(_SMALL + ("loss",), _unpack(packed[0], small_like)):
            store[n] = t

    loss = out_g["loss"][0]
    return (loss, grad_x, *[out_g[n] for n in names], *[out_d[n] for n in names], *[out_m[n] for n in names],
            *[out_v[n] for n in names])
```

```python
import jax
import jax.numpy as jnp
from jax import lax
from jax.experimental import pallas as pl
from jax.experimental.pallas import tpu as pltpu

F32 = jnp.float32
BF16 = jnp.bfloat16

SEQ = 2048
D_MODEL = 1024
DEPTH = 2
SG_WIDTH = 512
SG_CHUNK = 128
SG_GROUPS = 4
CV_WIDTH = 512
CV_KERNEL = 31
HEAD_DIM = 64
N_Q_HEADS = 8
N_KV_HEADS = 2
Q_WIDTH = 512
KV_WIDTH = 128
WINDOW = 128
SC_WIDTH = 512
SC_KERNEL = 3
N_BRANCH = 4
D_FF = 2816
EPS = 1e-6
ROPE_THETA = 10000.0
COL_B = 2 * SG_WIDTH
COL_Q = COL_B + 2 * CV_WIDTH
COL_K = COL_Q + Q_WIDTH
COL_V = COL_K + KV_WIDTH
COL_D = COL_V + KV_WIDTH
PROJ_A = COL_D + 3 * SC_WIDTH
PROJ_WIDTH = PROJ_A + N_BRANCH * D_MODEL
N_DEV = 8

ADAM_LR = 0.001
ADAM_B1 = 0.9
ADAM_B2 = 0.999
ADAM_EPS = 1e-08
ADAM_WD = 0.01
ADAM_STEP = 10

LANES = 128
SUBLANES = 8
VMEM_LIMIT_BYTES = 48 * 1024 * 1024
HALO = 32
CONV_ROWS = 256
TOKEN_TILE = 256
NORM_TILE = 512
ROPE_TILE = 1024

_SQRT_HALF = 0.7071067811865476
_INV_SQRT_2PI = 0.3989422804014327


def _params(semantics=None):
    return pltpu.CompilerParams(dimension_semantics=semantics, vmem_limit_bytes=VMEM_LIMIT_BYTES)


def _divisor_tile(n, cap, unit):
    best = None
    for t in range(unit, min(n, cap) + 1, unit):
        if n % t == 0:
            best = t
    return best if best is not None else n


_DIMS = {"nn": (((1,), (0,)), ((), ())), "nt": (((1,), (1,)), ((), ())), "tn": (((0,), (0,)), ((), ()))}


def matmul(a, b, mode, out_dtype, name, add=None, tm_cap=512, tn_cap=512, after=None, b_rows=None, out_rows=None,
           into=None, into_row=0):
    if mode == "nn":
        (m, k), n = a.shape, b.shape[1]
        k = b_rows if b_rows is not None else k
    elif mode == "nt":
        (m, k), n = a.shape, (b_rows if b_rows is not None else b.shape[0])
    else:
        (k, m), n = a.shape, b.shape[1]
    tm = _divisor_tile(m, tm_cap, LANES)
    tn = _divisor_tile(n, tn_cap, LANES)
    row0 = into_row // tm
    assert row0 * tm == into_row
    a_spec = pl.BlockSpec((k, tm), lambda i, j: (0, i)) if mode == "tn" else pl.BlockSpec((tm, k), lambda i, j: (i, 0))
    b_spec = pl.BlockSpec((tn, k), lambda i, j: (j, 0)) if mode == "nt" else pl.BlockSpec((k, tn), lambda i, j: (0, j))
    o_spec = pl.BlockSpec((tm, tn), lambda i, j: (i + row0, j))
    dims = _DIMS[mode]

    def body(*refs):
        a_ref, b_ref = refs[0], refs[1]
        o_ref = refs[-1]
        acc = lax.dot_general(a_ref[...].astype(BF16), b_ref[...].astype(BF16), dims, preferred_element_type=F32)
        if add is not None:
            acc = acc + refs[2][...].astype(F32)
        o_ref[...] = acc.astype(out_dtype)

    unread = tuple(t for t in (after, into) if t is not None)
    operands = (a, b) + (() if add is None else (add,)) + unread
    in_specs = [a_spec, b_spec] + ([o_spec] if add is not None else [])
    in_specs += [pl.BlockSpec(memory_space=pl.ANY)] * len(unread)
    aliases = {len(operands) - 1: 0} if into is not None else {}
    return pl.pallas_call(
        body, name=name,
        out_shape=jax.ShapeDtypeStruct((into.shape[0] if into is not None else out_rows or m, n), out_dtype),
        grid=(m // tm, n // tn),
        in_specs=in_specs, out_specs=o_spec, input_output_aliases=aliases,
        compiler_params=_params(("parallel", "parallel")))(*operands)


def _sigmoid(x):
    return 1.0 / (1.0 + jnp.exp(-x))


def _gelu(x):
    return 0.5 * x * (1.0 + lax.erf(x * _SQRT_HALF))


def _gelu_grad(x):
    return 0.5 * (1.0 + lax.erf(x * _SQRT_HALF)) + x * _INV_SQRT_2PI * jnp.exp(-0.5 * x * x)


def _rms_stats(x):
    r = lax.rsqrt(jnp.mean(x * x, axis=-1, keepdims=True) + EPS)
    return x * r, r


def _rms_bwd(dxn, xhat, r, g):
    h = dxn * g
    return r * (h - xhat * jnp.mean(h * xhat, axis=-1, keepdims=True))


def _ln_stats(x):
    mu = jnp.mean(x, axis=-1, keepdims=True)
    xc = x - mu
    rstd = lax.rsqrt(jnp.mean(xc * xc, axis=-1, keepdims=True) + EPS)
    return xc * rstd, rstd


def _ln_bwd(dy, xhat, rstd, g):
    dxhat = dy * g
    return rstd * (dxhat - jnp.mean(dxhat, axis=-1, keepdims=True)
                   - xhat * jnp.mean(dxhat * xhat, axis=-1, keepdims=True))


def _accumulate(ref, value, first):
    @pl.when(first)
    def _():
        ref[...] = value

    @pl.when(jnp.logical_not(first))
    def _():
        ref[...] += value


def _shift_rows(win, shift, n_out):
    n = win.shape[0]
    if shift % n == 0:
        return win[:n_out]
    return pltpu.roll(win, n - shift, axis=0)[:n_out]


def _row_spec(width):
    return pl.BlockSpec((1, width), lambda i: (0, 0))


def rmsnorm_fwd(x, g, name):
    s, d = x.shape

    def body(x_ref, g_ref, o_ref):
        xhat, _ = _rms_stats(x_ref[...])
        o_ref[...] = (xhat * g_ref[...]).astype(BF16)

    tile = pl.BlockSpec((NORM_TILE, d), lambda i: (i, 0))
    return pl.pallas_call(
        body, name=name, out_shape=jax.ShapeDtypeStruct((s, d), BF16), grid=(s // NORM_TILE,),
        in_specs=[tile, _row_spec(d)], out_specs=tile, compiler_params=_params(("parallel",)))(x, g)


def rmsnorm_bwd(x, g, dxn, dres, name):
    s, d = x.shape

    def body(x_ref, g_ref, dxn_ref, dres_ref, dx_ref, dg_ref):
        xhat, r = _rms_stats(x_ref[...])
        dxn_v = dxn_ref[...].astype(F32)
        dx_ref[...] = dres_ref[...] + _rms_bwd(dxn_v, xhat, r, g_ref[...])
        _accumulate(dg_ref, jnp.sum(dxn_v * xhat, axis=0, keepdims=True), pl.program_id(0) == 0)

    tile = pl.BlockSpec((NORM_TILE, d), lambda i: (i, 0))
    return pl.pallas_call(
        body, name=name, out_shape=(jax.ShapeDtypeStruct((s, d), F32), jax.ShapeDtypeStruct((1, d), F32)),
        grid=(s // NORM_TILE,), in_specs=[tile, _row_spec(d), tile, tile], out_specs=(tile, _row_spec(d)),
        compiler_params=_params(("arbitrary",)))(x, g, dxn, dres)


def loss_head(x, g, target, name):
    s, d = x.shape

    def body(x_ref, g_ref, t_ref, loss_ref, dx_ref, dg_ref):
        first = pl.program_id(0) == 0
        xhat, r = _rms_stats(x_ref[...])
        gv = g_ref[...]
        err = xhat * gv - t_ref[...]
        part = 0.5 * jnp.sum(jnp.sum(err * err, axis=-1, keepdims=True), axis=0, keepdims=True) / d
        _accumulate(loss_ref, jnp.broadcast_to(part, (1, LANES)), first)
        dy = err / d
        dx_ref[...] = _rms_bwd(dy, xhat, r, gv)
        _accumulate(dg_ref, jnp.sum(dy * xhat, axis=0, keepdims=True), first)

    tile = pl.BlockSpec((NORM_TILE, d), lambda i: (i, 0))
    return pl.pallas_call(
        body, name=name,
        out_shape=(jax.ShapeDtypeStruct((1, LANES), F32), jax.ShapeDtypeStruct((s, d), F32),
                   jax.ShapeDtypeStruct((1, d), F32)),
        grid=(s // NORM_TILE,), in_specs=[tile, _row_spec(d), tile],
        out_specs=(_row_spec(LANES), tile, _row_spec(d)), compiler_params=_params(("arbitrary",)))(x, g, target)


def _tril_mask():
    row = lax.broadcasted_iota(jnp.int32, (SG_CHUNK, SG_CHUNK), 0)
    col = lax.broadcasted_iota(jnp.int32, (SG_CHUNK, SG_CHUNK), 1)
    return row >= col


def _sg_specs():
    vec = _row_spec(SG_WIDTH)
    mat = pl.BlockSpec((SG_GROUPS, SG_CHUNK, SG_CHUNK), lambda i: (0, 0, 0))
    return vec, mat


def mixer_a_fwd(proj, ln_g, ln_b, w_s, b_s, name):
    s = proj.shape[0]
    chunks = TOKEN_TILE // SG_CHUNK

    def body(z_ref, lg_ref, lb_ref, w_ref, b_ref, o_ref):
        ge = _gelu(z_ref[...].astype(F32))
        u = ge[:, :SG_WIDTH]
        xhat, _ = _ln_stats(ge[:, SG_WIDTH:])
        vn = xhat * lg_ref[...] + lb_ref[...]
        tril = _tril_mask()
        for ci in range(chunks):
            rows = slice(ci * SG_CHUNK, (ci + 1) * SG_CHUNK)
            for g in range(SG_GROUPS):
                cols = slice(g * LANES, (g + 1) * LANES)
                wm = jnp.where(tril, w_ref[g], 0.0).astype(BF16)
                mixed = jnp.dot(wm, vn[rows, cols].astype(BF16), preferred_element_type=F32) + b_ref[g]
                o_ref[rows, cols] = (u[rows, cols] * mixed).astype(BF16)

    vec, mat = _sg_specs()
    return pl.pallas_call(
        body, name=name, out_shape=jax.ShapeDtypeStruct((s, SG_WIDTH), BF16), grid=(s // TOKEN_TILE,),
        in_specs=[pl.BlockSpec((TOKEN_TILE, 2 * SG_WIDTH), lambda i: (i, 0)), vec, vec, mat, mat],
        out_specs=pl.BlockSpec((TOKEN_TILE, SG_WIDTH), lambda i: (i, 0)),
        compiler_params=_params(("parallel",)))(proj, ln_g, ln_b, w_s, b_s)


def mixer_a_bwd(proj, dy, ln_g, ln_b, w_s, b_s, name):
    s = proj.shape[0]
    chunks = TOKEN_TILE // SG_CHUNK

    def body(z_ref, dy_ref, lg_ref, lb_ref, w_ref, b_ref, dz_ref, dlg_ref, dlb_ref, dw_ref, db_ref, du_scr, dvn_scr):
        first = pl.program_id(0) == 0

        @pl.when(first)
        def _():
            dw_ref[...] = jnp.zeros_like(dw_ref)
            db_ref[...] = jnp.zeros_like(db_ref)

        z = z_ref[...].astype(F32)
        ge = _gelu(z)
        u = ge[:, :SG_WIDTH]
        xhat, rstd = _ln_stats(ge[:, SG_WIDTH:])
        lg = lg_ref[...]
        vn = xhat * lg + lb_ref[...]
        dyv = dy_ref[...].astype(F32)
        tril = _tril_mask()
        for ci in range(chunks):
            rows = slice(ci * SG_CHUNK, (ci + 1) * SG_CHUNK)
            for g in range(SG_GROUPS):
                cols = slice(g * LANES, (g + 1) * LANES)
                wm = jnp.where(tril, w_ref[g], 0.0).astype(BF16)
                vg = vn[rows, cols].astype(BF16)
                mixed = jnp.dot(wm, vg, preferred_element_type=F32) + b_ref[g]
                dyb = dyv[rows, cols]
                du_scr[rows, cols] = dyb * mixed
                dmix = dyb * u[rows, cols]
                db_ref[g] += jnp.broadcast_to(jnp.sum(dmix, axis=1, keepdims=True), (SG_CHUNK, LANES))
                dmb = dmix.astype(BF16)
                dwg = lax.dot_general(dmb, vg, _DIMS["nt"], preferred_element_type=F32)
                dw_ref[g] += jnp.where(tril, dwg, 0.0)
                dvn_scr[rows, cols] = lax.dot_general(wm, dmb, _DIMS["tn"], preferred_element_type=F32)
        dvn = dvn_scr[...]
        _accumulate(dlg_ref, jnp.sum(dvn * xhat, axis=0, keepdims=True), first)
        _accumulate(dlb_ref, jnp.sum(dvn, axis=0, keepdims=True), first)
        dvv = _ln_bwd(dvn, xhat, rstd, lg)
        gg = _gelu_grad(z)
        dz_ref[:, :SG_WIDTH] = (du_scr[...] * gg[:, :SG_WIDTH]).astype(BF16)
        dz_ref[:, SG_WIDTH:] = (dvv * gg[:, SG_WIDTH:]).astype(BF16)

    vec, mat = _sg_specs()
    mat_shape = jax.ShapeDtypeStruct((SG_GROUPS, SG_CHUNK, SG_CHUNK), F32)
    vec_shape = jax.ShapeDtypeStruct((1, SG_WIDTH), F32)
    return pl.pallas_call(
        body, name=name,
        out_shape=(jax.ShapeDtypeStruct((s, 2 * SG_WIDTH), BF16), vec_shape, vec_shape, mat_shape, mat_shape),
        grid=(s // TOKEN_TILE,),
        in_specs=[pl.BlockSpec((TOKEN_TILE, 2 * SG_WIDTH), lambda i: (i, 0)),
                  pl.BlockSpec((TOKEN_TILE, SG_WIDTH), lambda i: (i, 0)), vec, vec, mat, mat],
        out_specs=(pl.BlockSpec((TOKEN_TILE, 2 * SG_WIDTH), lambda i: (i, 0)), vec, vec, mat, mat),
        scratch_shapes=[pltpu.VMEM((TOKEN_TILE, SG_WIDTH), F32), pltpu.VMEM((TOKEN_TILE, SG_WIDTH), F32)],
        compiler_params=_params(("arbitrary",)))(proj, dy, ln_g, ln_b, w_s, b_s)


_B_A_BLOCK = COL_B // LANES
_B_G_BLOCK = (COL_B + CV_WIDTH) // LANES
_CH_TILES = CV_WIDTH // LANES


def _col_spec(s, first_block):
    return pl.BlockSpec((s, LANES), lambda j: (0, first_block + j))


def conv_b_fwd(proj, w_pad, bias, name):
    s = proj.shape[0]

    def body(a_ref, g_ref, w_ref, b_ref, c_ref, upad):
        upad[0:HALO, :] = jnp.zeros((HALO, LANES), F32)
        upad[HALO:, :] = a_ref[...].astype(F32) * _sigmoid(g_ref[...].astype(F32))
        w = w_ref[...]
        bv = b_ref[...]

        def block(bi, carry):
            start = pl.multiple_of(bi * CONV_ROWS, CONV_ROWS)
            win = upad[pl.ds(start, CONV_ROWS + HALO), :]
            acc = jnp.zeros((CONV_ROWS, LANES), F32)
            for k in range(CV_KERNEL):
                acc = acc + w[k:k + 1, :] * _shift_rows(win, HALO - (CV_KERNEL - 1) + k, CONV_ROWS)
            c_ref[pl.ds(start, CONV_ROWS), :] = acc + bv
            return carry

        lax.fori_loop(0, s // CONV_ROWS, block, 0)

    return pl.pallas_call(
        body, name=name, out_shape=jax.ShapeDtypeStruct((s, CV_WIDTH), F32), grid=(_CH_TILES,),
        in_specs=[_col_spec(s, _B_A_BLOCK), _col_spec(s, _B_G_BLOCK), _col_spec(HALO, 0), _col_spec(1, 0)],
        out_specs=_col_spec(s, 0), scratch_shapes=[pltpu.VMEM((s + HALO, LANES), F32)],
        compiler_params=_params(("parallel",)))(proj, proj, w_pad, bias)


def conv_b_bwd(proj, w_pad, dc, name):
    s = proj.shape[0]

    def body(a_ref, g_ref, w_ref, dc_ref, da_ref, dg_ref, dw_ref, db_ref, upad, dpad, dw_scr):
        upad[0:HALO, :] = jnp.zeros((HALO, LANES), F32)
        upad[HALO:, :] = a_ref[...].astype(F32) * _sigmoid(g_ref[...].astype(F32))
        dcv = dc_ref[...]
        dpad[0:s, :] = dcv
        dpad[s:, :] = jnp.zeros((HALO, LANES), F32)
        db_ref[...] = jnp.sum(dcv, axis=0, keepdims=True)
        dw_scr[...] = jnp.zeros((HALO, LANES), F32)
        w = w_ref[...]

        def block(bi, carry):
            start = pl.multiple_of(bi * CONV_ROWS, CONV_ROWS)
            uwin = upad[pl.ds(start, CONV_ROWS + HALO), :]
            dwin = dpad[pl.ds(start, CONV_ROWS + HALO), :]
            dcb = dwin[:CONV_ROWS]
            du = jnp.zeros((CONV_ROWS, LANES), F32)
            for k in range(CV_KERNEL):
                du = du + w[k:k + 1, :] * _shift_rows(dwin, CV_KERNEL - 1 - k, CONV_ROWS)
                ush = _shift_rows(uwin, HALO - (CV_KERNEL - 1) + k, CONV_ROWS)
                dw_scr[k:k + 1, :] += jnp.sum(dcb * ush, axis=0, keepdims=True)
            av = a_ref[pl.ds(start, CONV_ROWS), :].astype(F32)
            sg = _sigmoid(g_ref[pl.ds(start, CONV_ROWS), :].astype(F32))
            da_ref[pl.ds(start, CONV_ROWS), :] = (du * sg).astype(BF16)
            dg_ref[pl.ds(start, CONV_ROWS), :] = (du * av * sg * (1.0 - sg)).astype(BF16)
            return carry

        lax.fori_loop(0, s // CONV_ROWS, block, 0)
        dw_ref[...] = dw_scr[...]

    act = jax.ShapeDtypeStruct((s, CV_WIDTH), BF16)
    return pl.pallas_call(
        body, name=name,
        out_shape=(act, act, jax.ShapeDtypeStruct((HALO, CV_WIDTH), F32), jax.ShapeDtypeStruct((1, CV_WIDTH), F32)),
        grid=(_CH_TILES,),
        in_specs=[_col_spec(s, _B_A_BLOCK), _col_spec(s, _B_G_BLOCK), _col_spec(HALO, 0), _col_spec(s, 0)],
        out_specs=(_col_spec(s, 0), _col_spec(s, 0), _col_spec(HALO, 0), _col_spec(1, 0)),
        scratch_shapes=[pltpu.VMEM((s + HALO, LANES), F32), pltpu.VMEM((s + HALO, LANES), F32),
                        pltpu.VMEM((HALO, LANES), F32)],
        compiler_params=_params(("parallel",)))(proj, proj, w_pad, dc)


def ln_silu_fwd(c, ln_g, ln_b, name):
    s, d = c.shape

    def body(c_ref, g_ref, b_ref, o_ref):
        xhat, _ = _ln_stats(c_ref[...])
        cn = xhat * g_ref[...] + b_ref[...]
        o_ref[...] = (cn * _sigmoid(cn)).astype(BF16)

    tile = pl.BlockSpec((NORM_TILE, d), lambda i: (i, 0))
    return pl.pallas_call(
        body, name=name, out_shape=jax.ShapeDtypeStruct((s, d), BF16), grid=(s // NORM_TILE,),
        in_specs=[tile, _row_spec(d), _row_spec(d)], out_specs=tile,
        compiler_params=_params(("parallel",)))(c, ln_g, ln_b)


def ln_silu_bwd(c, dy, ln_g, ln_b, name, after=None):
    s, d = c.shape
    unread = () if after is None else (after,)

    def body(c_ref, dy_ref, g_ref, b_ref, *rest):
        dc_ref, dg_ref, db_ref = rest[-3:]
        first = pl.program_id(0) == 0
        xhat, rstd = _ln_stats(c_ref[...])
        gv = g_ref[...]
        cn = xhat * gv + b_ref[...]
        sg = _sigmoid(cn)
        dcn = dy_ref[...].astype(F32) * sg * (1.0 + cn * (1.0 - sg))
        _accumulate(dg_ref, jnp.sum(dcn * xhat, axis=0, keepdims=True), first)
        _accumulate(db_ref, jnp.sum(dcn, axis=0, keepdims=True), first)
        dc_ref[...] = _ln_bwd(dcn, xhat, rstd, gv)

    tile = pl.BlockSpec((NORM_TILE, d), lambda i: (i, 0))
    vec_shape = jax.ShapeDtypeStruct((1, d), F32)
    return pl.pallas_call(
        body, name=name, out_shape=(jax.ShapeDtypeStruct((s, d), F32), vec_shape, vec_shape),
        grid=(s // NORM_TILE,),
        in_specs=[tile, tile, _row_spec(d), _row_spec(d)] + [pl.BlockSpec(memory_space=pl.ANY)] * len(unread),
        out_specs=(tile, _row_spec(d), _row_spec(d)),
        compiler_params=_params(("arbitrary",)))(c, dy, ln_g, ln_b, *unread)


_D_BLOCK = COL_D // LANES


def _conv3(win, w):
    acc = jnp.zeros((CONV_ROWS, LANES), F32)
    for k in range(SC_KERNEL):
        acc = acc + w[k:k + 1, :] * _shift_rows(win, HALO - (SC_KERNEL - 1) + k, CONV_ROWS)
    return acc


def conv_d_fwd(proj, w_pad, name):
    s = proj.shape[0]

    def body(bg_ref, cg_ref, h_ref, w_ref, o_ref, ppad):
        ppad[0:HALO, :] = jnp.zeros((HALO, LANES), F32)
        ppad[HALO:, :] = cg_ref[...].astype(F32) * h_ref[...].astype(F32)
        w = w_ref[...]

        def block(bi, carry):
            start = pl.multiple_of(bi * CONV_ROWS, CONV_ROWS)
            cv = _conv3(ppad[pl.ds(start, CONV_ROWS + HALO), :], w)
            o_ref[pl.ds(start, CONV_ROWS), :] = (bg_ref[pl.ds(start, CONV_ROWS), :].astype(F32) * cv).astype(BF16)
            return carry

        lax.fori_loop(0, s // CONV_ROWS, block, 0)

    return pl.pallas_call(
        body, name=name, out_shape=jax.ShapeDtypeStruct((s, SC_WIDTH), BF16), grid=(_CH_TILES,),
        in_specs=[_col_spec(s, _D_BLOCK), _col_spec(s, _D_BLOCK + _CH_TILES), _col_spec(s, _D_BLOCK + 2 * _CH_TILES),
                  _col_spec(SUBLANES, 0)],
        out_specs=_col_spec(s, 0), scratch_shapes=[pltpu.VMEM((s + HALO, LANES), F32)],
        compiler_params=_params(("parallel",)))(proj, proj, proj, w_pad)


def conv_d_bwd(proj, w_pad, dy, name):
    s = proj.shape[0]

    def body(bg_ref, cg_ref, h_ref, w_ref, dy_ref, dbg_ref, dcg_ref, dh_ref, dw_ref, ppad, dpad, dw_scr):
        ppad[0:HALO, :] = jnp.zeros((HALO, LANES), F32)
        ppad[HALO:, :] = cg_ref[...].astype(F32) * h_ref[...].astype(F32)
        dpad[0:s, :] = dy_ref[...].astype(F32) * bg_ref[...].astype(F32)
        dpad[s:, :] = jnp.zeros((HALO, LANES), F32)
        dw_scr[...] = jnp.zeros((SUBLANES, LANES), F32)
        w = w_ref[...]

        def block(bi, carry):
            start = pl.multiple_of(bi * CONV_ROWS, CONV_ROWS)
            rows = pl.ds(start, CONV_ROWS)
            pwin = ppad[pl.ds(start, CONV_ROWS + HALO), :]
            dwin = dpad[pl.ds(start, CONV_ROWS + HALO), :]
            dcvb = dwin[:CONV_ROWS]
            dbg_ref[rows, :] = (dy_ref[rows, :].astype(F32) * _conv3(pwin, w)).astype(BF16)
            dp = jnp.zeros((CONV_ROWS, LANES), F32)
            for k in range(SC_KERNEL):
                dp = dp + w[k:k + 1, :] * _shift_rows(dwin, SC_KERNEL - 1 - k, CONV_ROWS)
                psh = _shift_rows(pwin, HALO - (SC_KERNEL - 1) + k, CONV_ROWS)
                dw_scr[k:k + 1, :] += jnp.sum(dcvb * psh, axis=0, keepdims=True)
            dcg_ref[rows, :] = (dp * h_ref[rows, :].astype(F32)).astype(BF16)
            dh_ref[rows, :] = (dp * cg_ref[rows, :].astype(F32)).astype(BF16)
            return carry

        lax.fori_loop(0, s // CONV_ROWS, block, 0)
        dw_ref[...] = dw_scr[...]

    act = jax.ShapeDtypeStruct((s, SC_WIDTH), BF16)
    return pl.pallas_call(
        body, name=name, out_shape=(act, act, act, jax.ShapeDtypeStruct((SUBLANES, SC_WIDTH), F32)),
        grid=(_CH_TILES,),
        in_specs=[_col_spec(s, _D_BLOCK), _col_spec(s, _D_BLOCK + _CH_TILES), _col_spec(s, _D_BLOCK + 2 * _CH_TILES),
                  _col_spec(SUBLANES, 0), _col_spec(s, 0)],
        out_specs=(_col_spec(s, 0), _col_spec(s, 0), _col_spec(s, 0), _col_spec(SUBLANES, 0)),
        scratch_shapes=[pltpu.VMEM((s + HALO, LANES), F32), pltpu.VMEM((s + HALO, LANES), F32),
                        pltpu.VMEM((SUBLANES, LANES), F32)],
        compiler_params=_params(("parallel",)))(proj, proj, proj, w_pad, dy)


_QK_BLOCK = COL_Q // LANES
_QK_BLOCKS = (Q_WIDTH + KV_WIDTH) // LANES


def _swap_halves(t):
    lane = lax.broadcasted_iota(jnp.int32, t.shape, 1)
    low = (lane % HEAD_DIM) < (HEAD_DIM // 2)
    return jnp.where(low, pltpu.roll(t, LANES - HEAD_DIM // 2, axis=1), pltpu.roll(t, HEAD_DIM // 2, axis=1))


def rope_fwd(proj, cos_t, sin_t, name):
    s = proj.shape[0]

    def body(t_ref, c_ref, s_ref, o_ref):
        t = t_ref[...].astype(F32)
        o_ref[...] = (t * c_ref[...] + _swap_halves(t) * s_ref[...]).astype(BF16)

    tr = min(ROPE_TILE, s)
    tab = pl.BlockSpec((tr, LANES), lambda i, j: (i, 0))
    return pl.pallas_call(
        body, name=name, out_shape=jax.ShapeDtypeStruct((s, Q_WIDTH + KV_WIDTH), BF16),
        grid=(s // tr, _QK_BLOCKS),
        in_specs=[pl.BlockSpec((tr, LANES), lambda i, j: (i, _QK_BLOCK + j)), tab, tab],
        out_specs=pl.BlockSpec((tr, LANES), lambda i, j: (i, j)),
        compiler_params=_params(("parallel", "parallel")))(proj, cos_t, sin_t)


def rope_bwd(d_cur, d_prev, cos_t, sin_t, name):
    s, w = d_cur.shape

    def body(a_ref, b_ref, c_ref, s_ref, o_ref):
        d = a_ref[...] + b_ref[...]
        o_ref[...] = (d * c_ref[...] + _swap_halves(d) * s_ref[...]).astype(BF16)

    tr = min(ROPE_TILE, s)
    tab = pl.BlockSpec((tr, LANES), lambda i, j: (i, 0))
    blk = pl.BlockSpec((tr, LANES), lambda i, j: (i, j))
    return pl.pallas_call(
        body, name=name, out_shape=jax.ShapeDtypeStruct((s, w), BF16), grid=(s // tr, w // LANES),
        in_specs=[blk, blk, tab, tab], out_specs=blk,
        compiler_params=_params(("parallel", "parallel")))(d_cur, d_prev, cos_t, sin_t)


_GROUP = N_Q_HEADS // N_KV_HEADS
_NEG = -1e30


def _attn_specs():
    q_spec = pl.BlockSpec((_GROUP, WINDOW, HEAD_DIM), lambda h, n: (h, n, 0))
    cur = pl.BlockSpec((1, WINDOW, HEAD_DIM), lambda h, n: (h, n, 0))
    prev = pl.BlockSpec((1, WINDOW, HEAD_DIM), lambda h, n: (h, jnp.maximum(n - 1, 0), 0))
    sink = pl.BlockSpec((_GROUP, 1, LANES), lambda h, n: (h, 0, 0))
    return q_spec, cur, prev, sink


def _attn_valid(n):
    qi = lax.broadcasted_iota(jnp.int32, (WINDOW, 2 * WINDOW), 0)
    kj = lax.broadcasted_iota(jnp.int32, (WINDOW, 2 * WINDOW), 1)
    delta = qi + WINDOW - kj
    return (delta >= 0) & (delta < WINDOW) & ((kj >= WINDOW) | (n > 0))


def _attn_probs(q, kcat, valid, sink_row):
    sc = lax.dot_general(q, kcat, _DIMS["nt"], preferred_element_type=F32) * (HEAD_DIM ** -0.5)
    sc = jnp.where(valid, sc, _NEG)
    sink = jnp.max(sink_row, axis=-1, keepdims=True)
    m = jnp.maximum(jnp.max(sc, axis=-1, keepdims=True), sink)
    p = jnp.where(valid, jnp.exp(sc - m), 0.0)
    es = jnp.exp(sink - m)
    inv = 1.0 / (jnp.sum(p, axis=-1, keepdims=True) + es)
    return p * inv, es * inv


def attention_fwd(qh, kh, vh, sinks_b, name, after=None):
    s = qh.shape[1]
    unread = () if after is None else (after,)

    def body(q_ref, kc_ref, kp_ref, vc_ref, vp_ref, sk_ref, *rest):
        o_ref = rest[-1]
        valid = _attn_valid(pl.program_id(1))
        kcat = jnp.concatenate([kp_ref[0], kc_ref[0]], axis=0)
        vcat = jnp.concatenate([vp_ref[0], vc_ref[0]], axis=0)
        for g in range(_GROUP):
            probs, _ = _attn_probs(q_ref[g], kcat, valid, sk_ref[g])
            o_ref[g] = jnp.dot(probs.astype(BF16), vcat, preferred_element_type=F32).astype(BF16)

    q_spec, cur, prev, sink = _attn_specs()
    return pl.pallas_call(
        body, name=name, out_shape=jax.ShapeDtypeStruct(qh.shape, BF16), grid=(N_KV_HEADS, s // WINDOW),
        in_specs=[q_spec, cur, prev, cur, prev, sink] + [pl.BlockSpec(memory_space=pl.ANY)] * len(unread),
        out_specs=q_spec, compiler_params=_params(("parallel", "parallel")))(qh, kh, kh, vh, vh, sinks_b, *unread)


def attention_bwd(qh, kh, vh, sinks_b, doh, name):
    s = qh.shape[1]

    def body(q_ref, kc_ref, kp_ref, vc_ref, vp_ref, sk_ref, do_ref, dq_ref, dkc_ref, dkp_ref, dvc_ref, dvp_ref, ds_ref):
        n = pl.program_id(1)
        valid = _attn_valid(n)
        kcat = jnp.concatenate([kp_ref[0], kc_ref[0]], axis=0)
        vcat = jnp.concatenate([vp_ref[0], vc_ref[0]], axis=0)
        dk = jnp.zeros((2 * WINDOW, HEAD_DIM), F32)
        dv = jnp.zeros((2 * WINDOW, HEAD_DIM), F32)
        for g in range(_GROUP):
            q = q_ref[g]
            do = do_ref[g]
            probs, ps = _attn_probs(q, kcat, valid, sk_ref[g])
            dprobs = lax.dot_general(do, vcat, _DIMS["nt"], preferred_element_type=F32)
            dv = dv + lax.dot_general(probs.astype(BF16), do, _DIMS["tn"], preferred_element_type=F32)
            rs = jnp.sum(probs * dprobs, axis=-1, keepdims=True)
            dsb = (probs * (dprobs - rs) * (HEAD_DIM ** -0.5)).astype(BF16)
            dq_ref[g] = jnp.dot(dsb, kcat, preferred_element_type=F32)
            dk = dk + lax.dot_general(dsb, q, _DIMS["tn"], preferred_element_type=F32)
            dsink = jnp.broadcast_to(-jnp.sum(ps * rs, axis=0, keepdims=True), (1, LANES))

            @pl.when(n == 0)
            def _():
                ds_ref[g] = dsink

            @pl.when(n > 0)
            def _():
                ds_ref[g] += dsink

        dkp_ref[0] = dk[:WINDOW]
        dkc_ref[0] = dk[WINDOW:]
        dvp_ref[0] = dv[:WINDOW]
        dvc_ref[0] = dv[WINDOW:]

    q_spec, cur, prev, sink = _attn_specs()
    kv_shape = jax.ShapeDtypeStruct(kh.shape, F32)
    return pl.pallas_call(
        body, name=name,
        out_shape=(jax.ShapeDtypeStruct(qh.shape, F32), kv_shape, kv_shape, kv_shape, kv_shape,
                   jax.ShapeDtypeStruct(sinks_b.shape, F32)),
        grid=(N_KV_HEADS, s // WINDOW), in_specs=[q_spec, cur, prev, cur, prev, sink, q_spec],
        out_specs=(q_spec, cur, cur, cur, cur, sink),
        compiler_params=_params(("parallel", "arbitrary")))(qh, kh, kh, vh, vh, sinks_b, doh)


def _to_heads(t, heads):
    return t.reshape(t.shape[0], heads, HEAD_DIM).transpose(1, 0, 2)


def _from_heads(t):
    return t.transpose(1, 0, 2).reshape(t.shape[1], t.shape[0] * HEAD_DIM)


def _shift_window(t):
    return jnp.concatenate([t[:, WINDOW:], jnp.zeros_like(t[:, :WINDOW])], axis=1)


def merge_fwd(zg, branches, name):
    s = zg.shape[0]

    def body(zg_ref, b0, b1, b2, b3, o_ref):
        acc = jnp.zeros((TOKEN_TILE, D_MODEL), F32)
        for n, b_ref in enumerate((b0, b1, b2, b3)):
            gate = _sigmoid(zg_ref[:, n * D_MODEL:(n + 1) * D_MODEL].astype(F32))
            acc = acc + gate * b_ref[...].astype(F32)
        o_ref[...] = acc.astype(BF16)

    tile = pl.BlockSpec((TOKEN_TILE, D_MODEL), lambda i: (i, 0))
    wide = pl.BlockSpec((TOKEN_TILE, N_BRANCH * D_MODEL), lambda i: (i, 0))
    return pl.pallas_call(
        body, name=name, out_shape=jax.ShapeDtypeStruct((s, D_MODEL), BF16), grid=(s // TOKEN_TILE,),
        in_specs=[wide, tile, tile, tile, tile], out_specs=tile,
        compiler_params=_params(("parallel",)))(zg, *branches)


def merge_bwd(zg, branches, dm, name):
    s = zg.shape[0]

    def body(zg_ref, b0, b1, b2, b3, dm_ref, dzg_ref, d0, d1, d2, d3):
        dmv = dm_ref[...].astype(F32)
        for n, (b_ref, d_ref) in enumerate(((b0, d0), (b1, d1), (b2, d2), (b3, d3))):
            cols = slice(n * D_MODEL, (n + 1) * D_MODEL)
            gate = _sigmoid(zg_ref[:, cols].astype(F32))
            d_ref[...] = (gate * dmv).astype(BF16)
            dzg_ref[:, cols] = (dmv * b_ref[...].astype(F32) * gate * (1.0 - gate)).astype(BF16)

    tile = pl.BlockSpec((TOKEN_TILE, D_MODEL), lambda i: (i, 0))
    wide = pl.BlockSpec((TOKEN_TILE, N_BRANCH * D_MODEL), lambda i: (i, 0))
    act = jax.ShapeDtypeStruct((s, D_MODEL), BF16)
    return pl.pallas_call(
        body, name=name, out_shape=(jax.ShapeDtypeStruct((s, N_BRANCH * D_MODEL), BF16), act, act, act, act),
        grid=(s // TOKEN_TILE,), in_specs=[wide, tile, tile, tile, tile, tile],
        out_specs=(wide, tile, tile, tile, tile), compiler_params=_params(("parallel",)))(zg, *branches, dm)


def swiglu_fwd(gu, name):
    s = gu.shape[0]

    def body(g_ref, u_ref, o_ref):
        gate = g_ref[...].astype(F32)
        o_ref[...] = (gate * _sigmoid(gate) * u_ref[...].astype(F32)).astype(BF16)

    return pl.pallas_call(
        body, name=name, out_shape=jax.ShapeDtypeStruct((s, D_FF), BF16), grid=(s // TOKEN_TILE,),
        in_specs=[pl.BlockSpec((TOKEN_TILE, D_FF), lambda i: (i, 0)), pl.BlockSpec((TOKEN_TILE, D_FF), lambda i: (i, 1))],
        out_specs=pl.BlockSpec((TOKEN_TILE, D_FF), lambda i: (i, 0)), compiler_params=_params(("parallel",)))(gu, gu)


def swiglu_bwd(gu, dact, name):
    s = gu.shape[0]

    def body(g_ref, u_ref, da_ref, o_ref):
        gate = g_ref[...].astype(F32)
        sg = _sigmoid(gate)
        da = da_ref[...].astype(F32)
        o_ref[:, :D_FF] = (da * u_ref[...].astype(F32) * sg * (1.0 + gate * (1.0 - sg))).astype(BF16)
        o_ref[:, D_FF:] = (da * gate * sg).astype(BF16)

    half = pl.BlockSpec((TOKEN_TILE, D_FF), lambda i: (i, 0))
    return pl.pallas_call(
        body, name=name, out_shape=jax.ShapeDtypeStruct((s, 2 * D_FF), BF16), grid=(s // TOKEN_TILE,),
        in_specs=[half, pl.BlockSpec((TOKEN_TILE, D_FF), lambda i: (i, 1)), half],
        out_specs=pl.BlockSpec((TOKEN_TILE, 2 * D_FF), lambda i: (i, 0)),
        compiler_params=_params(("parallel",)))(gu, gu, dact)


ADAMW_BLOCK_BYTES = 1 << 20


def add_partials(a, b, name):
    q, r, c = a.shape
    tr = _divisor_tile(r, max(SUBLANES, ADAMW_BLOCK_BYTES // (2 * c)), SUBLANES)

    def body(a_ref, b_ref, o_ref):
        o_ref[...] = (a_ref[...].astype(F32) + b_ref[...].astype(F32)).astype(a.dtype)

    tile = pl.BlockSpec((1, tr, c), lambda i, j: (i, j, 0))
    return pl.pallas_call(
        body, name=name, out_shape=jax.ShapeDtypeStruct(a.shape, a.dtype), grid=(q, r // tr),
        in_specs=[tile, tile], out_specs=tile, compiler_params=_params(("parallel", "parallel")))(a, b)


def adamw(parts, w, m, v, name):
    n_parts, r, c = w.shape
    tr = _divisor_tile(r, max(SUBLANES, ADAMW_BLOCK_BYTES // (4 * c)), SUBLANES)
    tiles = r // tr

    def part_spec(j):
        return pl.BlockSpec((parts[j].shape[0], tr, c), lambda i: (0, jnp.clip(i - j * tiles, 0, tiles - 1), 0))

    def body(*refs):
        p_refs = refs[:n_parts]
        w_ref, m_ref, v_ref, g_ref, d_ref, nm_ref, nv_ref = refs[n_parts:]
        which = pl.program_id(0) // tiles
        g = None
        for j, p_ref in enumerate(p_refs):
            gj = p_ref[0].astype(F32)
            for i in range(1, p_ref.shape[0]):
                gj = gj + p_ref[i].astype(F32)
            g = gj if g is None else jnp.where(which == j, gj, g)
        nm = ADAM_B1 * m_ref[0] + (1.0 - ADAM_B1) * g
        nv = ADAM_B2 * v_ref[0] + (1.0 - ADAM_B2) * (g * g)
        m_hat = nm / (1.0 - ADAM_B1 ** ADAM_STEP)
        v_hat = nv / (1.0 - ADAM_B2 ** ADAM_STEP)
        g_ref[0] = g
        d_ref[0] = -ADAM_LR * (m_hat / (jnp.sqrt(v_hat) + ADAM_EPS) + ADAM_WD * w_ref[0])
        nm_ref[0] = nm
        nv_ref[0] = nv

    tile = pl.BlockSpec((1, tr, c), lambda i: (i // tiles, i % tiles, 0))
    shape = jax.ShapeDtypeStruct(w.shape, F32)
    return pl.pallas_call(
        body, name=name, out_shape=(shape, shape, shape, shape), grid=(n_parts * tiles,),
        in_specs=[part_spec(j) for j in range(n_parts)] + [tile, tile, tile],
        out_specs=(tile, tile, tile, tile), compiler_params=_params(("parallel",)))(*parts, w, m, v)


_RELATIONS = [(a, b, e) for a in (0, 1) for b in (0, 1) for e in (0, 1)][1:]


_HBM_SPEC = pl.BlockSpec(memory_space=pltpu.HBM)
_SEM_SPEC = pl.BlockSpec(memory_space=pltpu.SEMAPHORE)
_ANY_SPEC = pl.BlockSpec(memory_space=pl.ANY)
_DATAFLOW = pltpu.SideEffectType.DATAFLOW_SIDE_EFFECTING


_OTHER_CHIPS = [(1, 0), (0, 1), (1, 1)]
_FIRST_LEVEL = [(0, 0, 1)] + [(a, b, 0) for a, b in _OTHER_CHIPS]


def _remote_copies(ins, lands, send_sems, recv_sems, scatter, relations):
    x, y, c = lax.axis_index("x"), lax.axis_index("y"), lax.axis_index("c")
    me = 4 * x + 2 * y + c
    copies = []
    for t in range(len(ins)):
        for k, (a, b, e) in enumerate(relations):
            px, py, pc = (x + a) % 2, (y + b) % 2, (c + e) % 2
            src = ins[t].at[4 * px + 2 * py + pc] if scatter[t] else ins[t]
            copies.append(pltpu.make_async_remote_copy(
                src_ref=src, dst_ref=lands[t].at[me], send_sem=send_sems.at[t * len(relations) + k],
                recv_sem=recv_sems.at[t * len(relations) + k],
                device_id=(px, py, pc), device_id_type=pl.DeviceIdType.MESH))
    return copies


def _forward_copies(lands, send_sems, recv_sems):
    x, y, c = lax.axis_index("x"), lax.axis_index("y"), lax.axis_index("c")
    copies = []
    for t in range(len(lands)):
        for k, (a, b) in enumerate(_OTHER_CHIPS):
            slot = lands[t].at[4 * ((x + a) % 2) + 2 * ((y + b) % 2) + c]
            copies.append(pltpu.make_async_remote_copy(
                src_ref=slot, dst_ref=slot, send_sem=send_sems.at[t * len(_OTHER_CHIPS) + k],
                recv_sem=recv_sems.at[t * len(_OTHER_CHIPS) + k],
                device_id=(x, y, 1 - c), device_id_type=pl.DeviceIdType.MESH))
    return copies


def _place_own(landed, own, me):
    return lax.dynamic_update_index_in_dim(landed, own, me, 0)


N_CHIPS = 4


def _to_sibling_plan(ins, lands, send_sems, recv_sems):
    x, y, c = lax.axis_index("x"), lax.axis_index("y"), lax.axis_index("c")
    return [pltpu.make_async_remote_copy(
        src_ref=ins[0].at[2 * q + (1 - c)], dst_ref=lands[0].at[q], send_sem=send_sems.at[q], recv_sem=recv_sems.at[q],
        device_id=(x, y, 1 - c), device_id_type=pl.DeviceIdType.MESH) for q in range(N_CHIPS)]


def _to_chips_plan(ins, lands, send_sems, recv_sems):
    x, y, c = lax.axis_index("x"), lax.axis_index("y"), lax.axis_index("c")
    copies = []
    for k, (a, b) in enumerate(_OTHER_CHIPS):
        px, py = (x + a) % 2, (y + b) % 2
        copies.append(pltpu.make_async_remote_copy(
            src_ref=ins[0].at[2 * px + py], dst_ref=lands[0].at[2 * x + y], send_sem=send_sems.at[k],
            recv_sem=recv_sems.at[k], device_id=(px, py, c), device_id_type=pl.DeviceIdType.MESH))
    return copies


def exchange_start(arrays, scatter, name, after=None, relations=_RELATIONS, plan=None, land_shapes=None):
    n = len(arrays)
    n_rel = len(relations)
    if land_shapes is None:
        land_shapes = [a.shape if scatter[t] else (N_DEV,) + a.shape for t, a in enumerate(arrays)]
    zones = [lax.empty(s, a.dtype) for s, a in zip(land_shapes, arrays)]
    if plan is None:
        def plan(ins, lands, send_sems, recv_sems):
            return _remote_copies(ins, lands, send_sems, recv_sems, scatter, relations)

    def body(*refs):
        ins, lands = refs[:n], refs[n:2 * n]
        send_sems, recv_sems = refs[-2 * n - 3], refs[-2 * n - 2]
        token = refs[-1]
        for cp in plan(ins, lands, send_sems, recv_sems):
            cp.start()
        token[...] = jnp.zeros_like(token)

    sems = pltpu.SemaphoreType.DMA((n * n_rel,))
    out_shape = ((sems, sems) + tuple(pltpu.HBM(a.shape, a.dtype) for a in arrays)
                 + tuple(pltpu.HBM(s, a.dtype) for s, a in zip(land_shapes, arrays))
                 + (jax.ShapeDtypeStruct((SUBLANES, LANES), F32),))
    operands = [pltpu.with_memory_space_constraint(a, pltpu.HBM) for a in arrays]
    operands += [pltpu.with_memory_space_constraint(z, pltpu.HBM) for z in zones]
    in_specs = [_HBM_SPEC] * (2 * n)
    if after is not None:
        operands.append(after)
        in_specs.append(_ANY_SPEC)
    res = pl.pallas_call(
        body, name=name, out_shape=out_shape, in_specs=in_specs,
        out_specs=(_SEM_SPEC, _SEM_SPEC) + (_HBM_SPEC,) * (2 * n) + (pl.BlockSpec(memory_space=pltpu.VMEM),),
        input_output_aliases={i: 2 + i for i in range(2 * n)},
        compiler_params=pltpu.CompilerParams(has_side_effects=_DATAFLOW))(*operands)
    handle = (res[0], res[1], res[2:2 + n], res[2 + n:2 + 2 * n], plan)
    return handle, res[-1]


def exchange_wait(handle, after, name):
    send_sems, recv_sems, sources, lands, plan = handle
    n = len(sources)

    def body(*refs):
        ins, lzs = refs[:n], refs[n:2 * n]
        send_ref, recv_ref = refs[2 * n], refs[2 * n + 1]
        for cp in plan(ins, lzs, send_ref, recv_ref):
            cp.wait_send()
            cp.wait_recv()

    out_shape = (tuple(pltpu.HBM(a.shape, a.dtype) for a in sources) + tuple(pltpu.HBM(a.shape, a.dtype) for a in lands))
    res = pl.pallas_call(
        body, name=name, out_shape=out_shape, in_specs=[_HBM_SPEC] * (2 * n) + [_SEM_SPEC, _SEM_SPEC, _ANY_SPEC],
        out_specs=(_HBM_SPEC,) * (2 * n), input_output_aliases={i: i for i in range(2 * n)},
        compiler_params=pltpu.CompilerParams(has_side_effects=_DATAFLOW))(*sources, *lands, send_sems, recv_sems, after)
    return res[n:], res[:n]


def forward_start(lands, name):
    n = len(lands)

    def body(*refs):
        send_sems, recv_sems, token = refs[n], refs[n + 1], refs[-1]
        for cp in _forward_copies(refs[:n], send_sems, recv_sems):
            cp.start()
        token[...] = jnp.zeros_like(token)

    sems = pltpu.SemaphoreType.DMA((n * len(_OTHER_CHIPS),))
    res = pl.pallas_call(
        body, name=name,
        out_shape=(sems, sems) + tuple(pltpu.HBM(a.shape, a.dtype) for a in lands)
        + (jax.ShapeDtypeStruct((SUBLANES, LANES), F32),),
        in_specs=[_HBM_SPEC] * n,
        out_specs=(_SEM_SPEC, _SEM_SPEC) + (_HBM_SPEC,) * n + (pl.BlockSpec(memory_space=pltpu.VMEM),),
        input_output_aliases={i: 2 + i for i in range(n)},
        compiler_params=pltpu.CompilerParams(has_side_effects=_DATAFLOW))(*lands)
    return (res[0], res[1], res[2:2 + n]), res[-1]


def forward_wait(handle, after, name):
    send_sems, recv_sems, lands = handle
    n = len(lands)

    def body(*refs):
        for cp in _forward_copies(refs[:n], refs[n], refs[n + 1]):
            cp.wait_send()
            cp.wait_recv()

    return pl.pallas_call(
        body, name=name, out_shape=tuple(pltpu.HBM(a.shape, a.dtype) for a in lands),
        in_specs=[_HBM_SPEC] * n + [_SEM_SPEC, _SEM_SPEC, _ANY_SPEC], out_specs=(_HBM_SPEC,) * n,
        input_output_aliases={i: i for i in range(n)},
        compiler_params=pltpu.CompilerParams(has_side_effects=_DATAFLOW))(*lands, send_sems, recv_sems, after)


_SMALL = ("norm_mix", "sg_ln_g", "sg_ln_b", "sg_b", "cv_b", "cv_ln_g", "cv_ln_b", "attn_sinks", "norm_ffn",
          "norm_final")
_PACK_UNIT = SUBLANES * LANES


def _pack(entries):
    flat = []
    for entry in entries:
        parts = [t.reshape(-1) for t in (entry if isinstance(entry, (list, tuple)) else [entry])]
        size = sum(t.shape[0] for t in parts)
        flat += parts + [jnp.zeros(((-size) % _PACK_UNIT,), parts[0].dtype)]
    return jnp.concatenate(flat).reshape(-1, LANES)


def _unpack(packed, like):
    out, row = [], 0
    for t in like:
        size = 1
        for d in t.shape:
            size *= d
        rows = -(-size // _PACK_UNIT) * SUBLANES
        out.append(packed[row:row + rows].reshape(-1)[:size].reshape(t.shape))
        row += rows
    return out


def _layer_fwd(l, x, p, late_params, mid_hook=None, ffn_hook=None):
    tag = f"l{l}_"
    xn = rmsnorm_fwd(x, p["norm_mix"], tag + "norm_mix")
    proj = matmul(xn, p["w_in_t"], "nt", BF16, tag + "proj_a", tm_cap=1024, tn_cap=2176, b_rows=PROJ_A)
    zg = matmul(xn, p["w_in_t_g"], "nt", BF16, tag + "proj_g", tm_cap=1024, tn_cap=2048)
    y_a = mixer_a_fwd(proj, p["sg_ln_g"], p["sg_ln_b"], p["sg_w"], p["sg_b"], tag + "mix_a")
    conv = conv_b_fwd(proj, p["cv_w"], p["cv_b"], tag + "conv_b")
    y_b = ln_silu_fwd(conv, p["cv_ln_g"], p["cv_ln_b"], tag + "ln_silu")
    y_d = conv_d_fwd(proj, p["sc_w"], tag + "conv_d")
    qk = rope_fwd(proj, p["cos"], p["sin"], tag + "rope")
    qh = _to_heads(qk[:, :Q_WIDTH], N_Q_HEADS)
    kh = _to_heads(qk[:, Q_WIDTH:], N_KV_HEADS)
    vh = _to_heads(proj[:, COL_V:COL_D], N_KV_HEADS)
    token = mid_hook(qh) if mid_hook is not None else None
    oh = attention_fwd(qh, kh, vh, p["sinks"], tag + "attn", after=token)
    y_c = _from_heads(oh)
    ys = (y_a, y_b, y_c, y_d)
    p = {**p, **late_params(y_c)}
    branches = tuple(matmul(ys[n], p["w_branch"][n], "nn", BF16, tag + f"branch{n}", tm_cap=1024, tn_cap=1024)
                     for n in range(N_BRANCH))
    merged = merge_fwd(zg, branches, tag + "merge")
    x_mid = matmul(merged, p["w_out"], "nn", F32, tag + "out", add=x, tm_cap=1024, tn_cap=1024)
    token = ffn_hook(x_mid) if ffn_hook is not None else None
    hn = rmsnorm_fwd(x_mid, p["norm_ffn"], tag + "norm_ffn")
    gu = matmul(hn, p["w_gate_up_t"], "nt", BF16, tag + "gate_up", tm_cap=512, tn_cap=2816, after=token)
    act = swiglu_fwd(gu, tag + "swiglu")
    x_out = matmul(act, p["w_down"], "nn", F32, tag + "down", add=x_mid, tm_cap=512, tn_cap=1024)
    saved = dict(x=x, xn=xn, proj=proj, zg=zg, conv=conv, qh=qh, kh=kh, vh=vh, ys=ys, branches=branches,
                 merged=merged, x_mid=x_mid, hn=hn, gu=gu, act=act)
    return x_out, saved, p


def _layer_bwd(l, dx_out, p, sv, emit, after=None, after_dxn_a=None):
    tag = f"l{l}_b_"
    g = {}
    dact = matmul(dx_out, p["w_down"], "nt", BF16, tag + "dact", after=after, tm_cap=512, tn_cap=2816)
    dw_down = matmul(sv["act"], dx_out, "tn", BF16, tag + "dw_down", tm_cap=1408, tn_cap=512)
    dgu = swiglu_bwd(sv["gu"], dact, tag + "swiglu")
    dhn = matmul(dgu, p["w_gate_up_t"], "nn", BF16, tag + "dhn", tm_cap=512, tn_cap=512)
    dw_gate_up = matmul(dgu, sv["hn"], "tn", BF16, tag + "dw_gate_up", tm_cap=1408, tn_cap=1024)
    token = emit("a", {"w_gate_up": dw_gate_up, "w_down": dw_down})
    dx_mid, g["norm_ffn"] = rmsnorm_bwd(sv["x_mid"], p["norm_ffn"], dhn, dx_out, tag + "norm_ffn")
    dmerged = matmul(dx_mid, p["w_out"], "nt", BF16, tag + "dmerged", after=token, tm_cap=1024, tn_cap=1024)
    dw_out = matmul(sv["merged"], dx_mid, "tn", BF16, tag + "dw_out", tm_cap=1024, tn_cap=512)
    dzg, *dbranches = merge_bwd(sv["zg"], sv["branches"], dmerged, tag + "merge")
    dys = [matmul(dbranches[n], p["w_branch"][n], "nt", BF16, tag + f"dy{n}", tm_cap=1024, tn_cap=512)
           for n in range(N_BRANCH)]
    dw_branch = jnp.stack(
        [matmul(sv["ys"][n], dbranches[n], "tn", BF16, tag + f"dw_branch{n}", tm_cap=512, tn_cap=1024)
         for n in range(N_BRANCH)])
    proj = sv["proj"]
    dz_a, g["sg_ln_g"], g["sg_ln_b"], dsw_a, dsb = mixer_a_bwd(
        proj, dys[0], p["sg_ln_g"], p["sg_ln_b"], p["sg_w"], p["sg_b"], tag + "mix_a")
    g["sg_b"] = dsb[:, :, 0]
    token = emit("b", {"w_branch": dw_branch, "w_out": dw_out}, {"sg_w": dsw_a})
    dconv, g["cv_ln_g"], g["cv_ln_b"] = ln_silu_bwd(sv["conv"], dys[1], p["cv_ln_g"], p["cv_ln_b"], tag + "ln_silu",
                                                    after=token)
    da, dgate, dcw, g["cv_b"] = conv_b_bwd(proj, p["cv_w"], dconv, tag + "conv_b")
    g["cv_w"] = dcw[:CV_KERNEL]
    doh = _to_heads(dys[2], N_Q_HEADS)
    dqh, dkc, dkp, dvc, dvp, dsk = attention_bwd(sv["qh"], sv["kh"], sv["vh"], p["sinks"], doh, tag + "attn")
    g["attn_sinks"] = dsk[:, 0, 0]
    dqk_cur = jnp.concatenate([_from_heads(dqh), _from_heads(dkc)], axis=1)
    dqk_prev = jnp.concatenate([jnp.zeros((SEQ, Q_WIDTH), F32), _from_heads(_shift_window(dkp))], axis=1)
    dqk = rope_bwd(dqk_cur, dqk_prev, p["cos"], -p["sin"], tag + "rope")
    dv = (_from_heads(dvc) + _from_heads(_shift_window(dvp))).astype(BF16)
    dbg, dcg, dh, dsw = conv_d_bwd(proj, p["sc_w"], dys[3], tag + "conv_d")
    g["sc_w"] = dsw[:SC_KERNEL]
    dproj = jnp.concatenate([dz_a, da, dgate, dqk, dv, dbg, dcg, dh], axis=1)
    dw_in = matmul(dproj, sv["xn"], "tn", BF16, tag + "dw_in_a", after=token, tm_cap=2176, tn_cap=512,
                   out_rows=PROJ_WIDTH)
    g["w_in"] = matmul(dzg, sv["xn"], "tn", BF16, tag + "dw_in_g", tm_cap=256, tn_cap=1024, into=dw_in,
                       into_row=PROJ_A)
    token = emit("c", {n: g.pop(n) for n in _EARLY})
    dxn = matmul(dproj, p["w_in_t"], "nn", F32, tag + "dxn_a", after=token, tm_cap=512, tn_cap=512, b_rows=PROJ_A)
    if after_dxn_a is not None:
        token = after_dxn_a(dxn)
    dxn = matmul(dzg, p["w_in_t_g"], "nn", F32, tag + "dxn_g", add=dxn, after=token, tm_cap=512, tn_cap=512)
    dx_in, g["norm_mix"] = rmsnorm_bwd(sv["x"], p["norm_mix"], dxn, dx_mid, tag + "norm_mix")
    return dx_in, g, token


_EARLY = ("w_in", "cv_w", "sc_w")
_LATE = ("w_branch", "w_out", "w_gate_up", "w_down")


_TRANSPOSED = ("w_in", "w_gate_up")


def _shard_view(name, t):
    return jnp.swapaxes(t, 1, 2) if name in _TRANSPOSED else t


def _full_weight(name, t):
    if name in ("w_out", "w_down") + _TRANSPOSED:
        return t.reshape(-1, t.shape[-1])
    if name == "w_branch":
        return t.transpose(1, 2, 0, 3).reshape(N_BRANCH, SG_WIDTH, D_MODEL)
    return t.transpose(1, 0, 2).reshape(t.shape[1], -1)


def _to_blocks(name, full):
    if name in ("w_out", "w_down") + _TRANSPOSED:
        return full.reshape(N_DEV, -1, full.shape[-1])
    if name == "w_branch":
        return full.reshape(N_BRANCH, SG_WIDTH, N_DEV, -1).transpose(2, 0, 1, 3)
    return full.reshape(full.shape[0], N_DEV, -1).transpose(1, 0, 2)


def _rope_tables():
    pos = jnp.arange(SEQ, dtype=F32)
    inv_freq = 1.0 / (ROPE_THETA ** (jnp.arange(0, HEAD_DIM, 2, dtype=F32) / HEAD_DIM))
    ang = pos[:, None] * inv_freq[None, :]
    cos, sin = jnp.cos(ang), jnp.sin(ang)
    reps = LANES // HEAD_DIM
    return jnp.tile(jnp.concatenate([cos, cos], axis=1), (1, reps)), jnp.tile(jnp.concatenate([-sin, sin], axis=1), (1, reps))


def kernel(x, norm_mix, w_in, sg_ln_g, sg_ln_b, sg_w, sg_b, cv_w, cv_b, cv_ln_g, cv_ln_b, attn_sinks, sc_w, w_branch, w_out, norm_ffn, w_gate_up, w_down, norm_final, loss_target, m_norm_mix, m_w_in, m_sg_ln_g, m_sg_ln_b, m_sg_w, m_sg_b, m_cv_w, m_cv_b, m_cv_ln_g, m_cv_ln_b, m_attn_sinks, m_sc_w, m_w_branch, m_w_out, m_norm_ffn, m_w_gate_up, m_w_down, m_norm_final, v_norm_mix, v_w_in, v_sg_ln_g, v_sg_ln_b, v_sg_w, v_sg_b, v_cv_w, v_cv_b, v_cv_ln_g, v_cv_ln_b, v_attn_sinks, v_sc_w, v_w_branch, v_w_out, v_norm_ffn, v_w_gate_up, v_w_down, v_norm_final):
    names = ("norm_mix", "w_in", "sg_ln_g", "sg_ln_b", "sg_w", "sg_b", "cv_w", "cv_b", "cv_ln_g", "cv_ln_b",
             "attn_sinks", "sc_w", "w_branch", "w_out", "norm_ffn", "w_gate_up", "w_down", "norm_final")
    w = dict(zip(names, (norm_mix, w_in, sg_ln_g, sg_ln_b, sg_w, sg_b, cv_w, cv_b, cv_ln_g, cv_ln_b, attn_sinks,
                         sc_w, w_branch, w_out, norm_ffn, w_gate_up, w_down, norm_final)))
    m = dict(zip(names, (m_norm_mix, m_w_in, m_sg_ln_g, m_sg_ln_b, m_sg_w, m_sg_b, m_cv_w, m_cv_b, m_cv_ln_g,
                         m_cv_ln_b, m_attn_sinks, m_sc_w, m_w_branch, m_w_out, m_norm_ffn, m_w_gate_up, m_w_down,
                         m_norm_final)))
    v = dict(zip(names, (v_norm_mix, v_w_in, v_sg_ln_g, v_sg_ln_b, v_sg_w, v_sg_b, v_cv_w, v_cv_b, v_cv_ln_g,
                         v_cv_ln_b, v_attn_sinks, v_sc_w, v_w_branch, v_w_out, v_norm_ffn, v_w_gate_up, v_w_down,
                         v_norm_final)))

    me = 4 * lax.axis_index("x") + 2 * lax.axis_index("y") + lax.axis_index("c")
    groups = [(l, group) for l in range(DEPTH) for group in (_EARLY, _LATE)]
    shards = {(l, group): [_shard_view(n, w[n])[l].astype(BF16) for n in group] for l, group in groups}
    gathers, forwards, own_shards, token = {}, {}, {}, None
    for l, group in groups:
        gathers[(l, group)], token = exchange_start(
            shards[(l, group)], [False] * len(group), f"gather_start{l}_{group[0]}", after=token,
            relations=_FIRST_LEVEL)

    def begin_forward(l, group, after):
        landed, own_shards[(l, group)] = exchange_wait(gathers[(l, group)], after, f"gather_wait{l}_{group[0]}")
        forwards[(l, group)], tok = forward_start(landed, f"forward_start{l}_{group[0]}")
        return tok

    def landed_weights(l, group, after):
        landed = forward_wait(forwards[(l, group)], after, f"forward_wait{l}_{group[0]}")
        return {n + "_t" if n in _TRANSPOSED else n: _full_weight(n, _place_own(t, own, me))
                for n, t, own in zip(group, landed, own_shards[(l, group)])}

    cos_t, sin_t = _rope_tables()

    def early_params(l, after):
        full = landed_weights(l, _EARLY, after)
        return dict(
            norm_mix=w["norm_mix"][l][None], norm_ffn=w["norm_ffn"][l][None],
            w_in_t=full["w_in_t"], w_in_t_g=full["w_in_t"][PROJ_A:],
            sg_ln_g=w["sg_ln_g"][l][None], sg_ln_b=w["sg_ln_b"][l][None], sg_w=w["sg_w"][l],
            sg_b=jnp.broadcast_to(w["sg_b"][l][:, :, None], (SG_GROUPS, SG_CHUNK, LANES)),
            cv_w=jnp.pad(full["cv_w"].astype(F32), ((0, HALO - CV_KERNEL), (0, 0))),
            cv_b=w["cv_b"][l][None], cv_ln_g=w["cv_ln_g"][l][None], cv_ln_b=w["cv_ln_b"][l][None],
            sinks=jnp.broadcast_to(w["attn_sinks"][l][:, None, None], (N_Q_HEADS, 1, LANES)),
            sc_w=jnp.pad(full["sc_w"].astype(F32), ((0, SUBLANES - SC_KERNEL), (0, 0))),
            cos=cos_t, sin=sin_t)

    params, saved = [None] * DEPTH, [None] * DEPTH
    h = x[0]
    after = begin_forward(0, _EARLY, token)
    for l in range(DEPTH):
        h, saved[l], params[l] = _layer_fwd(
            l, h, early_params(l, after), lambda behind, l=l: landed_weights(l, _LATE, behind),
            mid_hook=lambda behind, l=l: begin_forward(l, _LATE, behind),
            ffn_hook=(lambda behind, l=l: begin_forward(l + 1, _EARLY, behind)) if l + 1 < DEPTH else None)
        after = h
    loss_row, dh, d_norm_final = loss_head(h, w["norm_final"][None], loss_target[0], "loss_head")

    sent = {}

    two_level = {}

    def reduce_scatter_two_level(name, grad):
        blocks = _to_blocks(name, grad.astype(BF16))
        stage_shape = (N_CHIPS,) + blocks.shape[1:]
        two_level["sibling"], tok = exchange_start([blocks], None, "rs_start_sibling", plan=_to_sibling_plan,
                                                   land_shapes=[stage_shape])
        two_level["name"] = name
        return tok

    def reduce_scatter_second_level(behind_this):
        (from_sibling,), (blocks,) = exchange_wait(two_level["sibling"], behind_this, "rs_wait_sibling")
        mine = lax.dynamic_index_in_dim(blocks.reshape((N_CHIPS, 2) + blocks.shape[1:]), lax.axis_index("c"), 1,
                                        keepdims=False)
        partial = add_partials(mine, from_sibling, "rs_add")
        two_level[two_level["name"]], tok = exchange_start([partial], None, "rs_start_chips", plan=_to_chips_plan,
                                                           land_shapes=[from_sibling.shape])
        return tok

    def emitter(l):
        def emit(group, grads_of, replicated=None):
            replicated = replicated or {}
            behind_tok = None
            if l == 0 and group == "c":
                grads_of = dict(grads_of)
                behind_tok = reduce_scatter_two_level("w_in", grads_of.pop("w_in"))
            send = [_to_blocks(n, grads_of[n].astype(BF16)) for n in grads_of] + list(replicated.values())
            flags = [True] * len(grads_of) + [False] * len(replicated)
            handle, tok = exchange_start(send, flags, f"grads_start{l}{group}", after=behind_tok)
            sent[(l, group)] = (handle, tuple(grads_of) + tuple(replicated), flags)
            return tok
        return emit

    grads = [None] * DEPTH
    token = None
    for l in reversed(range(DEPTH)):
        dh, grads[l], token = _layer_bwd(l, dh, params[l], saved[l], emitter(l), after=token,
                                         after_dxn_a=reduce_scatter_second_level if l == 0 else None)
    grad_x = dh[None]

    small_grads = [[d_norm_final] if n == "norm_final" else [grads[l][n] for l in range(DEPTH)] for n in _SMALL]
    no_state = jnp.zeros((1,), F32)
    small_like = [w[n] for n in _SMALL] + [no_state]
    small_part = _pack(small_grads + [loss_row[0, :1]])
    handle_small, token = exchange_start([small_part], [False], "grads_start_small", after=token)

    def received(l, group, after):
        handle, group_names, flags = sent[(l, group)]
        landed, sources = exchange_wait(handle, after, f"grads_wait{l}{group}")
        return {n: _place_own(t, lax.dynamic_index_in_dim(s, me, 0, keepdims=False) if scattered else s, me)
                for n, t, s, scattered in zip(group_names, landed, sources, flags)}

    out_g, out_d, out_m, out_v = {}, {}, {}, {}

    def update(n, by_layer):
        shape = _shard_view(n, w[n]).shape
        view = (DEPTH, w[n].size // (DEPTH * shape[-1]), shape[-1])
        parts = [t.reshape(t.shape[:1] + view[1:]) for t in by_layer]
        res = adamw(parts, *[_shard_view(n, t).reshape(view) for t in (w[n], m[n], v[n])], "adamw_" + n)
        out_g[n], out_d[n], out_m[n], out_v[n] = (_shard_view(n, t.reshape(shape)) for t in res)
        return res[0]

    behind = token
    for group in ("a", "b", "c"):
        r1 = received(1, group, behind)
        r0 = received(0, group, next(iter(r1.values())))
        for n in r0:
            behind = update(n, [r0[n], r1[n]])
    (by_chip,), (partial,) = exchange_wait(two_level["w_in"], behind, "rs_wait_chips")
    chip = 2 * lax.axis_index("x") + lax.axis_index("y")
    by_chip = _place_own(by_chip, lax.dynamic_index_in_dim(partial, chip, 0, keepdims=False), chip)
    behind = update("w_in", [by_chip, r1["w_in"]])
    landed, sources = exchange_wait(handle_small, behind, "grads_wait_small")
    res = adamw([_place_own(landed[0], sources[0], me)], _pack(small_like)[None],
                _pack([m[n] for n in _SMALL] + [no_state])[None], _pack([v[n] for n in _SMALL] + [no_state])[None],
                "adamw_small")
    for store, packed in zip((out_g, out_d, out_m, out_v), res):
        for n, t in zip(_SMALL + ("loss",), _unpack(packed[0], small_like)):
            store[n] = t

    loss = out_g["loss"][0]
    return (loss, grad_x, *[out_g[n] for n in names], *[out_d[n] for n in names], *[out_m[n] for n in names],
            *[out_v[n] for n in names])
```
